```python
import math
import jax, jax.numpy as jnp
from jax import lax
import numpy as np

D_MODEL = 1024
BATCH = 8
SEQ = 2048
DEPTH = 4
DEC_BATCH = 32
DEC_SEQ = 8
PAST_LEN = 16384
PAGE_SIZE = 128

N_MIXERS = 3
LAYER_MIXER = tuple(i % N_MIXERS for i in range(DEPTH))
N_RWKV = LAYER_MIXER.count(0)
N_MLA = LAYER_MIXER.count(1)
N_GDN = LAYER_MIXER.count(2)
N_VRES = max(N_RWKV - 1, 0)
NORM_EPS = 1e-6

RW_N = 64
RW_H = D_MODEL // RW_N
RW_DECAY_LORA = 64
RW_A_LORA = 64
RW_V_LORA = 32
RW_G_LORA = 160
RW_LNX_EPS = 64e-5

MLA_H = 16
MLA_NOPE = 64
MLA_ROPE = 32
MLA_V = 64
MLA_Q_LORA = 512
MLA_KV_LORA = 256
MLA_SCALE = (MLA_NOPE + MLA_ROPE) ** -0.5
ROPE_THETA = 10000.0
Q_BLOCK = 128

GDN_QK_H = 8
GDN_V_H = 16
GDN_DK = 128
GDN_DV = 128
GDN_QK_DIM = GDN_QK_H * GDN_DK
GDN_V_DIM = GDN_V_H * GDN_DV
GDN_CONV_DIM = 2 * GDN_QK_DIM + GDN_V_DIM
GDN_IN = GDN_CONV_DIM + GDN_V_DIM + 2 * GDN_V_H
GDN_CONV = 4
GDN_CHUNK = 64

D_FF = 2816
FFN_CONV = 3

kernel_name = 'hybrid_rwkv7_mla_gdn_convffn_step'


def rmsnorm(x, w, eps=NORM_EPS):
    xf = x.astype(jnp.float32)
    y = xf * lax.rsqrt(jnp.mean(xf * xf, axis=-1, keepdims=True) + eps)
    return (y * w.astype(jnp.float32)).astype(x.dtype)


def l2norm(x, eps=1e-6):
    xf = x.astype(jnp.float32)
    return xf * lax.rsqrt(jnp.sum(xf * xf, axis=-1, keepdims=True) + eps)


def causal_conv(x, buf, w):
    K = w.shape[0]
    T = x.shape[1]
    xp = jnp.concatenate([buf.astype(x.dtype), x], axis=1)
    y = xp[:, 0:T] * w[0]
    for j in range(1, K):
        y = y + xp[:, j:j + T] * w[j]
    return y, xp[:, T:]


def rope(x, pos):
    half = x.shape[-1] // 2
    inv = ROPE_THETA ** (-jnp.arange(half, dtype=jnp.float32) / half)
    ang = pos.astype(jnp.float32)[:, None] * inv[None, :]
    ang = ang.reshape(ang.shape[:1] + (1,) * (x.ndim - 3) + (half,))
    cos, sin = jnp.cos(ang), jnp.sin(ang)
    xf = x.astype(jnp.float32)
    x1, x2 = xf[..., :half], xf[..., half:]
    return jnp.concatenate([x1 * cos - x2 * sin, x1 * sin + x2 * cos], axis=-1).astype(x.dtype)


def rwkv7_time_mix(x, shift_prev, S0, v_first, p, ri):
    B, T, D = x.shape
    f32 = jnp.float32
    x_prev = jnp.concatenate([shift_prev[:, None, :].astype(x.dtype), x[:, :-1]], axis=1)
    xm = x[None] + (x_prev - x)[None] * p['rw_mu'][ri][:, None, None, :]
    r, k, v = jnp.einsum('nbtd,nde->nbte', xm[:3], p['rw_wrkv'][ri])
    xw, xa, xg = xm[3], xm[4], xm[5]
    w = -jax.nn.softplus(-(p['rw_w0'][ri] + jnp.tanh(xw @ p['rw_w1'][ri]) @ p['rw_w2'][ri])) - 0.5
    if v_first is None:
        v_first = v
    else:
        vi = ri - 1
        v = v + (v_first - v) * jax.nn.sigmoid(p['rw_v0'][vi] + (xm[2] @ p['rw_v1'][vi]) @ p['rw_v2'][vi])
    a = jax.nn.sigmoid(p['rw_a0'][ri] + (xa @ p['rw_a1'][ri]) @ p['rw_a2'][ri])
    g = jax.nn.sigmoid(xg @ p['rw_g1'][ri]) @ p['rw_g2'][ri]
    hs = (B, T, RW_H, RW_N)
    kk = l2norm((k * p['rw_kk'][ri]).reshape(hs))
    k = k * (1 + (a - 1) * p['rw_ka'][ri])
    rh, kh, vh, ah = (t.reshape(hs).astype(f32) for t in (r, k, v, a))
    decay = jnp.exp(-jnp.exp(w.reshape(hs).astype(f32)))

    def tm(t):
        return jnp.moveaxis(t, 1, 0)

    def step(S, inp):
        r_t, w_t, k_t, v_t, a_t, b_t = inp
        sa = jnp.einsum('bhvk,bhk->bhv', S, a_t)
        S = S * w_t[:, :, None, :] + sa[..., None] * b_t[:, :, None, :] + v_t[..., None] * k_t[:, :, None, :]
        return S, jnp.einsum('bhvk,bhk->bhv', S, r_t)

    S, y = lax.scan(step, S0.astype(f32), (tm(rh), tm(decay), tm(kh), tm(vh), tm(-kk), tm(kk * ah)))
    y = tm(y)
    mu = jnp.mean(y, axis=-1, keepdims=True)
    var = jnp.mean(jnp.square(y - mu), axis=-1, keepdims=True)
    y = ((y - mu) * lax.rsqrt(var + RW_LNX_EPS)).reshape(B, T, D) * p['rw_lnx_w'][ri] + p['rw_lnx_b'][ri]
    bonus = jnp.sum(rh * kh * p['rw_rk'][ri], axis=-1, keepdims=True) * vh
    y = (y + bonus.reshape(B, T, D)).astype(x.dtype) * g
    return y @ p['rw_wo'][ri], x[:, -1], S, v_first


def mla_project(x, pos, p, mi):
    B, T, _ = x.shape
    h = x @ p['mla_w_in'][mi]
    cq = rmsnorm(h[..., :MLA_Q_LORA], p['mla_q_norm'][mi])
    c = rmsnorm(h[..., MLA_Q_LORA:MLA_Q_LORA + MLA_KV_LORA], p['mla_kv_norm'][mi])
    kr = rope(h[..., MLA_Q_LORA + MLA_KV_LORA:], pos)
    q = (cq @ p['mla_w_qb'][mi]).reshape(B, T, MLA_H, MLA_NOPE + MLA_ROPE)
    q_lat = jnp.einsum('bthn,chn->bthc', q[..., :MLA_NOPE], p['mla_w_uk'][mi])
    q_rope = rope(q[..., MLA_NOPE:], pos)
    return q_lat, q_rope, c, kr


def mla_attend_prompt(q_lat, q_rope, c, kr):
    B, T, H, C = q_lat.shape
    nb = T // Q_BLOCK

    def blk(t):
        return jnp.moveaxis(t.reshape((B, nb, Q_BLOCK) + t.shape[2:]), 1, 0)

    kpos = jnp.arange(T)

    def one_block(args):
        ql, qr, i = args
        s = (jnp.einsum('bqhc,bkc->bhqk', ql, c) + jnp.einsum('bqhr,bkr->bhqk', qr, kr)).astype(jnp.float32) * MLA_SCALE
        qpos = i * Q_BLOCK + jnp.arange(Q_BLOCK)
        s = jnp.where(kpos[None, :] <= qpos[:, None], s, -jnp.inf)
        pr = jax.nn.softmax(s, axis=-1).astype(c.dtype)
        return jnp.einsum('bhqk,bkc->bqhc', pr, c)

    o = lax.map(one_block, (blk(q_lat), blk(q_rope), jnp.arange(nb)))
    return jnp.moveaxis(o, 0, 1).reshape(B, T, H, C)


def mla_attend_sample(q_lat, q_rope, c, kr, pages_c, pages_kr, page_table):
    B, T, H, C = q_lat.shape
    c_past = pages_c[page_table].reshape(B, -1, C).astype(c.dtype)
    kr_past = pages_kr[page_table].reshape(B, -1, MLA_ROPE).astype(kr.dtype)
    P = c_past.shape[1]
    s_past = jnp.einsum('bqhc,bkc->bhqk', q_lat, c_past) + jnp.einsum('bqhr,bkr->bhqk', q_rope, kr_past)
    s_new = jnp.einsum('bqhc,bkc->bhqk', q_lat, c) + jnp.einsum('bqhr,bkr->bhqk', q_rope, kr)
    causal = jnp.tril(jnp.ones((T, T), dtype=bool))
    s_new = jnp.where(causal, s_new.astype(jnp.float32), -jnp.inf)
    s = jnp.concatenate([s_past.astype(jnp.float32), s_new], axis=-1) * MLA_SCALE
    pr = jax.nn.softmax(s, axis=-1).astype(c.dtype)
    return jnp.einsum('bhqk,bkc->bqhc', pr[..., :P], c_past) + jnp.einsum('bhqk,bkc->bqhc', pr[..., P:], c)


def mla_out(o_lat, p, mi):
    B, T = o_lat.shape[:2]
    v = jnp.einsum('bthc,chv->bthv', o_lat, p['mla_w_uv'][mi])
    return v.reshape(B, T, MLA_H * MLA_V) @ p['mla_wo'][mi]


def chunk_gated_delta(q, k, v, beta, g, S0):
    B, T, H, DK = q.shape
    DV = v.shape[-1]
    C = GDN_CHUNK if T % GDN_CHUNK == 0 else T
    n = T // C

    def cs(t):
        t = t.reshape((B, n, C) + t.shape[2:])
        return jnp.moveaxis(jnp.moveaxis(t, 1, 0), 2, 3)

    q, k, v, beta, g = (cs(t) for t in (q, k, v, beta, g))
    gc = jnp.cumsum(g, axis=-1)
    idx = jnp.arange(C)
    lower = idx[:, None] >= idx[None, :]
    strict = idx[:, None] > idx[None, :]
    L = jnp.exp(jnp.where(lower, gc[..., :, None] - gc[..., None, :], -jnp.inf))
    kb = k * beta[..., None]
    A = jnp.where(strict, jnp.einsum('...ik,...jk->...ij', kb, k) * L, 0.0)
    Tm = A + jnp.eye(C, dtype=A.dtype)
    rhs = jnp.concatenate([v * beta[..., None], kb * jnp.exp(gc)[..., None]], axis=-1)
    sol = lax.linalg.triangular_solve(Tm, rhs, left_side=True, lower=True, unit_diagonal=True)
    u, w = sol[..., :DV], sol[..., DV:]
    Aqk = jnp.einsum('...ik,...jk->...ij', q, k) * L
    qg = q * jnp.exp(gc)[..., None]
    kg = k * jnp.exp(gc[..., -1:] - gc)[..., None]
    gl = jnp.exp(gc[..., -1])

    def step(S, inp):
        u_i, w_i, aqk_i, qg_i, kg_i, gl_i = inp
        v_new = u_i - jnp.einsum('bhck,bhkv->bhcv', w_i, S)
        o = jnp.einsum('bhck,bhkv->bhcv', qg_i, S) + jnp.einsum('bhij,bhjv->bhiv', aqk_i, v_new)
        S = S * gl_i[..., None, None] + jnp.einsum('bhck,bhcv->bhkv', kg_i, v_new)
        return S, o

    S, o = lax.scan(step, S0, (u, w, Aqk, qg, kg, gl))
    o = jnp.moveaxis(jnp.moveaxis(o, 2, 3), 0, 1).reshape(B, T, H, DV)
    return o, S


def gdn_mix(x, conv_buf, S0, p, gi):
    B, T, _ = x.shape
    f32 = jnp.float32
    h = x @ p['gdn_w_in'][gi]
    o1 = GDN_CONV_DIM
    o2 = o1 + GDN_V_DIM
    o3 = o2 + GDN_V_H
    qkv, conv_new = causal_conv(h[..., :o1], conv_buf, p['gdn_conv_w'][gi])
    qkv = jax.nn.silu(qkv)
    z = h[..., o1:o2]
    beta = jax.nn.sigmoid(h[..., o2:o3].astype(f32))
    gdec = -jnp.exp(p['gdn_a_log'][gi].astype(f32)) * jax.nn.softplus(h[..., o3:].astype(f32) + p['gdn_dt_bias'][gi].astype(f32))
    rep = GDN_V_H // GDN_QK_H
    q = l2norm(qkv[..., :GDN_QK_DIM].reshape(B, T, GDN_QK_H, GDN_DK)) * (GDN_DK ** -0.5)
    k = l2norm(qkv[..., GDN_QK_DIM:2 * GDN_QK_DIM].reshape(B, T, GDN_QK_H, GDN_DK))
    q = jnp.repeat(q, rep, axis=2)
    k = jnp.repeat(k, rep, axis=2)
    v = qkv[..., 2 * GDN_QK_DIM:].reshape(B, T, GDN_V_H, GDN_DV).astype(f32)
    o, S = chunk_gated_delta(q, k, v, beta, gdec, S0.astype(f32))
    o = rmsnorm(o, p['gdn_norm_w'][gi]) * jax.nn.silu(z.reshape(B, T, GDN_V_H, GDN_DV).astype(f32))
    return o.reshape(B, T, GDN_V_DIM).astype(x.dtype) @ p['gdn_wo'][gi], conv_new, S


def conv_ffn(x, buf, p, l):
    u, buf_new = causal_conv(x @ p['ffn_w_up'][l], buf, p['ffn_conv_w'][l])
    u = u + p['ffn_conv_b'][l]
    return (jax.nn.silu(u[..., :D_FF]) * u[..., D_FF:]) @ p['ffn_w_down'][l], buf_new


def trunk(x, pos, rw_S, rw_shift, gdn_S, gdn_conv, ffn_conv, p, paged):
    new = {'rw_S': [], 'rw_shift': [], 'mla_c': [], 'mla_kr': [], 'gdn_S': [], 'gdn_conv': [], 'ffn_conv': []}
    v_first = None
    ri = 0
    mi = 0
    gi = 0
    for l in range(DEPTH):
        nw = p['norm_w'][l]
        h = rmsnorm(x, nw[0])
        kind = LAYER_MIXER[l]
        if kind == 0:
            out, sh, S, v_first = rwkv7_time_mix(h, rw_shift[ri], rw_S[ri], v_first, p, ri)
            new['rw_S'].append(S)
            new['rw_shift'].append(sh)
            ri += 1
        elif kind == 1:
            q_lat, q_rope, c, kr = mla_project(h, pos, p, mi)
            if paged is None:
                o = mla_attend_prompt(q_lat, q_rope, c, kr)
            else:
                o = mla_attend_sample(q_lat, q_rope, c, kr, paged[0][mi], paged[1][mi], paged[2])
            out = mla_out(o, p, mi)
            new['mla_c'].append(c)
            new['mla_kr'].append(kr)
            mi += 1
        else:
            out, cb, S = gdn_mix(h, gdn_conv[gi], gdn_S[gi], p, gi)
            new['gdn_S'].append(S)
            new['gdn_conv'].append(cb)
            gi += 1
        x = x + rmsnorm(out.astype(x.dtype), nw[1])
        f, fb = conv_ffn(rmsnorm(x, nw[2]), ffn_conv[l], p, l)
        new['ffn_conv'].append(fb)
        x = x + rmsnorm(f, nw[3])
    stacked = {name: jnp.stack(vals).astype(x.dtype) for name, vals in new.items()}
    return x, stacked


def setup_inputs(seed: int = 0) -> dict:
    key = jax.random.key(seed)
    keys = jax.random.split(key, 64)
    counter = iter(range(64))
    f32 = jnp.float32

    def nrm(shape, scale=1.0):
        return jax.random.normal(keys[next(counter)], shape, f32) * scale

    def unif(shape, lo, hi):
        return jax.random.uniform(keys[next(counter)], shape, f32, lo, hi)

    d = D_MODEL
    n_pages = PAST_LEN // PAGE_SIZE
    n_pool = (DEC_BATCH * n_pages * 5) // 4
    page_table = jax.random.permutation(keys[next(counter)], n_pool)[:DEC_BATCH * n_pages].reshape(DEC_BATCH, n_pages).astype(jnp.int32)
    dt = jnp.exp(unif((N_GDN, GDN_V_H), math.log(1e-3), math.log(1e-1)))
    return {
        'x_prompt': nrm((BATCH, SEQ, d)),
        'x_sample': nrm((DEC_BATCH, DEC_SEQ, d)),
        'state_rwkv_wkv': nrm((N_RWKV, DEC_BATCH, RW_H, RW_N, RW_N), 0.1),
        'state_rwkv_shift': nrm((N_RWKV, DEC_BATCH, d)),
        'cache_mla_latent': nrm((N_MLA, n_pool, PAGE_SIZE, MLA_KV_LORA)),
        'cache_mla_krope': nrm((N_MLA, n_pool, PAGE_SIZE, MLA_ROPE)),
        'state_gdn_S': nrm((N_GDN, DEC_BATCH, GDN_V_H, GDN_DK, GDN_DV), 0.1),
        'state_gdn_conv': nrm((N_GDN, DEC_BATCH, GDN_CONV - 1, GDN_CONV_DIM)),
        'state_ffn_conv': nrm((DEPTH, DEC_BATCH, FFN_CONV - 1, 2 * D_FF), 0.5),
        'page_table': page_table,
        'norm_w': 1.0 + nrm((DEPTH, 4, d), 0.05),
        'rw_mu': unif((N_RWKV, 6, d), 0.0, 1.0),
        'rw_wrkv': nrm((N_RWKV, 3, d, d), d ** -0.5),
        'rw_w0': unif((N_RWKV, d), -5.0, 0.5),
        'rw_w1': nrm((N_RWKV, d, RW_DECAY_LORA), d ** -0.5),
        'rw_w2': nrm((N_RWKV, RW_DECAY_LORA, d), 0.5 * RW_DECAY_LORA ** -0.5),
        'rw_a0': nrm((N_RWKV, d), 0.1),
        'rw_a1': nrm((N_RWKV, d, RW_A_LORA), d ** -0.5),
        'rw_a2': nrm((N_RWKV, RW_A_LORA, d), RW_A_LORA ** -0.5),
        'rw_v0': nrm((N_VRES, d), 0.1),
        'rw_v1': nrm((N_VRES, d, RW_V_LORA), d ** -0.5),
        'rw_v2': nrm((N_VRES, RW_V_LORA, d), RW_V_LORA ** -0.5),
        'rw_g1': nrm((N_RWKV, d, RW_G_LORA), d ** -0.5),
        'rw_g2': nrm((N_RWKV, RW_G_LORA, d), RW_G_LORA ** -0.5),
        'rw_kk': 0.85 + nrm((N_RWKV, d), 0.05),
        'rw_ka': 1.0 + nrm((N_RWKV, d), 0.05),
        'rw_rk': nrm((N_RWKV, RW_H, RW_N), 0.1),
        'rw_lnx_w': 1.0 + nrm((N_RWKV, d), 0.05),
        'rw_lnx_b': nrm((N_RWKV, d), 0.01),
        'rw_wo': nrm((N_RWKV, d, d), d ** -0.5),
        'mla_w_in': nrm((N_MLA, d, MLA_Q_LORA + MLA_KV_LORA + MLA_ROPE), d ** -0.5),
        'mla_q_norm': 1.0 + nrm((N_MLA, MLA_Q_LORA), 0.05),
        'mla_kv_norm': 1.0 + nrm((N_MLA, MLA_KV_LORA), 0.05),
        'mla_w_qb': nrm((N_MLA, MLA_Q_LORA, MLA_H * (MLA_NOPE + MLA_ROPE)), MLA_Q_LORA ** -0.5),
        'mla_w_uk': nrm((N_MLA, MLA_KV_LORA, MLA_H, MLA_NOPE), MLA_KV_LORA ** -0.5),
        'mla_w_uv': nrm((N_MLA, MLA_KV_LORA, MLA_H, MLA_V), MLA_KV_LORA ** -0.5),
        'mla_wo': nrm((N_MLA, MLA_H * MLA_V, d), (MLA_H * MLA_V) ** -0.5),
        'gdn_w_in': nrm((N_GDN, d, GDN_IN), d ** -0.5),
        'gdn_conv_w': nrm((N_GDN, GDN_CONV, GDN_CONV_DIM), GDN_CONV ** -0.5),
        'gdn_a_log': jnp.log(unif((N_GDN, GDN_V_H), 1.0, 16.0)),
        'gdn_dt_bias': jnp.log(jnp.expm1(dt)),
        'gdn_norm_w': 1.0 + nrm((N_GDN, GDN_DV), 0.05),
        'gdn_wo': nrm((N_GDN, GDN_V_DIM, d), GDN_V_DIM ** -0.5),
        'ffn_w_up': nrm((DEPTH, d, 2 * D_FF), d ** -0.5),
        'ffn_conv_w': nrm((DEPTH, FFN_CONV, 2 * D_FF), FFN_CONV ** -0.5),
        'ffn_conv_b': nrm((DEPTH, 2 * D_FF), 0.01),
        'ffn_w_down': nrm((DEPTH, D_FF, d), D_FF ** -0.5),
    }


def reference(x_prompt, x_sample, state_rwkv_wkv, state_rwkv_shift, cache_mla_latent, cache_mla_krope,
              state_gdn_S, state_gdn_conv, state_ffn_conv, page_table, norm_w,
              rw_mu, rw_wrkv, rw_w0, rw_w1, rw_w2, rw_a0, rw_a1, rw_a2, rw_v0, rw_v1, rw_v2,
              rw_g1, rw_g2, rw_kk, rw_ka, rw_rk, rw_lnx_w, rw_lnx_b, rw_wo,
              mla_w_in, mla_q_norm, mla_kv_norm, mla_w_qb, mla_w_uk, mla_w_uv, mla_wo,
              gdn_w_in, gdn_conv_w, gdn_a_log, gdn_dt_bias, gdn_norm_w, gdn_wo,
              ffn_w_up, ffn_conv_w, ffn_conv_b, ffn_w_down):
    p = {
        'norm_w': norm_w,
        'rw_mu': rw_mu, 'rw_wrkv': rw_wrkv, 'rw_w0': rw_w0, 'rw_w1': rw_w1, 'rw_w2': rw_w2,
        'rw_a0': rw_a0, 'rw_a1': rw_a1, 'rw_a2': rw_a2, 'rw_v0': rw_v0, 'rw_v1': rw_v1, 'rw_v2': rw_v2,
        'rw_g1': rw_g1, 'rw_g2': rw_g2, 'rw_kk': rw_kk, 'rw_ka': rw_ka, 'rw_rk': rw_rk,
        'rw_lnx_w': rw_lnx_w, 'rw_lnx_b': rw_lnx_b, 'rw_wo': rw_wo,
        'mla_w_in': mla_w_in, 'mla_q_norm': mla_q_norm, 'mla_kv_norm': mla_kv_norm, 'mla_w_qb': mla_w_qb,
        'mla_w_uk': mla_w_uk, 'mla_w_uv': mla_w_uv, 'mla_wo': mla_wo,
        'gdn_w_in': gdn_w_in, 'gdn_conv_w': gdn_conv_w, 'gdn_a_log': gdn_a_log, 'gdn_dt_bias': gdn_dt_bias,
        'gdn_norm_w': gdn_norm_w, 'gdn_wo': gdn_wo,
        'ffn_w_up': ffn_w_up, 'ffn_conv_w': ffn_conv_w, 'ffn_conv_b': ffn_conv_b, 'ffn_w_down': ffn_w_down,
    }
    B, T = x_prompt.shape[0], x_prompt.shape[1]
    dt = x_prompt.dtype
    y_prompt, sp = trunk(
        x_prompt, jnp.arange(T),
        jnp.zeros((N_RWKV, B, RW_H, RW_N, RW_N), dt),
        jnp.zeros((N_RWKV, B, D_MODEL), dt),
        jnp.zeros((N_GDN, B, GDN_V_H, GDN_DK, GDN_DV), dt),
        jnp.zeros((N_GDN, B, GDN_CONV - 1, GDN_CONV_DIM), dt),
        jnp.zeros((DEPTH, B, FFN_CONV - 1, 2 * D_FF), dt),
        p, None)
    past_len = page_table.shape[1] * cache_mla_latent.shape[2]
    pos_s = past_len + jnp.arange(x_sample.shape[1])
    y_sample, ss = trunk(
        x_sample, pos_s, state_rwkv_wkv, state_rwkv_shift, state_gdn_S, state_gdn_conv, state_ffn_conv,
        p, (cache_mla_latent, cache_mla_krope, page_table))
    return (y_prompt, y_sample,
            sp['rw_S'], sp['rw_shift'], sp['mla_c'], sp['mla_kr'], sp['gdn_S'], sp['gdn_conv'], sp['ffn_conv'],
            ss['rw_S'], ss['rw_shift'], ss['mla_c'], ss['mla_kr'], ss['gdn_S'], ss['gdn_conv'], ss['ffn_conv'])
```

```python
import functools

import jax
import jax.numpy as jnp
from jax import lax
from jax.experimental import pallas as pl
from jax.experimental.pallas import tpu as pltpu

F32 = jnp.float32
BF16 = jnp.bfloat16
HIGHEST = lax.Precision.HIGHEST

D_MODEL = 1024
NORM_EPS = 1e-6
RW_N = 64
RW_H = D_MODEL // RW_N
RW_LNX_EPS = 64e-5
MLA_H = 16
MLA_NOPE = 64
MLA_ROPE = 32
MLA_V = 64
MLA_Q_LORA = 512
MLA_KV_LORA = 256
MLA_SCALE = (MLA_NOPE + MLA_ROPE) ** -0.5
ROPE_THETA = 10000.0
MLA_QK = MLA_KV_LORA + 128
GDN_QK_H = 8
GDN_V_H = 16
GDN_DK = 128
GDN_DV = 128
GDN_QK_DIM = GDN_QK_H * GDN_DK
GDN_V_DIM = GDN_V_H * GDN_DV
GDN_CONV_DIM = 2 * GDN_QK_DIM + GDN_V_DIM
GDN_CONV = 4
D_FF = 2816
FFN_CONV = 3
LAYER_MIXER = (0, 1, 2, 0)

SUBLANES = 8
LANES = 128
GROUP_ROWS = 128
VMEM_LIMIT = 56 * 1024 * 1024


def _rms(x, w):
    return x * lax.rsqrt(jnp.mean(x * x, axis=-1, keepdims=True) + NORM_EPS) * w


def _bdot(a, b):
    return jnp.dot(a.astype(BF16), b.astype(BF16), preferred_element_type=F32)


def _bdot_nt(a, b):
    return lax.dot_general(a.astype(BF16), b.astype(BF16), (((1,), (1,)), ((), ())),
                           preferred_element_type=F32)


def _bdot_tn(a, b):
    return lax.dot_general(a.astype(BF16), b.astype(BF16), (((0,), (0,)), ((), ())),
                           preferred_element_type=F32)


def _hdot(a, b):
    return jnp.dot(a, b, precision=HIGHEST, preferred_element_type=F32)


def _sigmoid(x):
    return 1.0 / (1.0 + jnp.exp(-x))


def _softplus(x):
    return jnp.maximum(x, 0.0) + jnp.log(1.0 + jnp.exp(-jnp.abs(x)))


def _silu(x):
    return x * _sigmoid(x)


def _shift_rows(u, prev, s):
    rows, cols = u.shape
    p = prev.shape[0]
    rolled = pltpu.roll(u, s, 0)
    fix = pltpu.roll(prev, (p - SUBLANES + s) % p, 0)
    t = lax.broadcasted_iota(jnp.int32, (p, cols), 0) % SUBLANES
    if p == rows:
        return jnp.where(t < s, fix, rolled)
    head = jnp.where(t < s, fix, rolled[:SUBLANES])
    return jnp.concatenate([head, rolled[SUBLANES:]], axis=0)


def _lane_group_sum(x, ones_bd):
    parts = [_hdot(x[:, i * LANES:(i + 1) * LANES], ones_bd) for i in range(x.shape[1] // LANES)]
    return parts[0] if len(parts) == 1 else jnp.concatenate(parts, axis=1)


def _chunk_cumsum(x, tri):
    bc = tri.shape[0]
    parts = [_hdot(tri, x[i * bc:(i + 1) * bc]) for i in range(x.shape[0] // bc)]
    return parts[0] if len(parts) == 1 else jnp.concatenate(parts, axis=0)


def _unit_lower_inverse(a_strict, eye, chunk):
    p = a_strict
    x = eye + p
    span = 2
    while span < chunk:
        p = _bdot(p, p)
        x = x + _bdot(p, x)
        span *= 2
    return x


def _tiling(b, t, tt_max):
    if t == SUBLANES:
        return dict(G=1, J=1, R=b * t, P=b * t)
    tt = min(t, tt_max)
    assert t % tt == 0 and tt % 64 == 0, (t, tt)
    return dict(G=b, J=t // tt, R=tt, P=SUBLANES)


def _chunk_of(t):
    return 64 if t % 64 == 0 else t


def _const_spec(shape):
    nd = len(shape)
    return pl.BlockSpec(shape, lambda *_: (0,) * nd, pipeline_mode=pl.Buffered(1))


def _params(n_axes):
    return pltpu.CompilerParams(dimension_semantics=("arbitrary",) * n_axes,
                                vmem_limit_bytes=VMEM_LIMIT)


def _pad_state(st):
    b, k1, c = st.shape
    return jnp.pad(st, ((0, 0), (SUBLANES - k1, 0), (0, 0))).reshape(b * SUBLANES, c)


def _chunk_masks(chunk, rows):
    i = jnp.arange(rows)
    same = (i[:, None] // chunk) == (i[None, :] // chunk)
    tri = same & ((i[None, :] % chunk) <= (i[:, None] % chunk))
    return tri.astype(F32)


FFN_CW = 256


def _ffn_body(x_ref, prev_ref, nw_ref, wup_ref, cwb_ref, wdn_ref, xo_ref, st_ref, carry_ref, acc_ref):
    @pl.when(pl.program_id(1) == 0)
    def _():
        carry_ref[...] = prev_ref[...]

    x = x_ref[...]
    rows = x.shape[0]
    p = carry_ref.shape[0]
    h = _rms(x, nw_ref[0:1, :]).astype(BF16)
    for c in range(D_FF // FFN_CW):
        ys = []
        for half in range(2):
            sl = slice(half * D_FF + c * FFN_CW, half * D_FF + (c + 1) * FFN_CW)
            u = jnp.dot(h, wup_ref[:, sl], preferred_element_type=F32)
            prev = carry_ref[:, sl]
            y = (cwb_ref[0:1, sl] * _shift_rows(u, prev, 2) + cwb_ref[1:2, sl] * _shift_rows(u, prev, 1)
                 + cwb_ref[2:3, sl] * u + cwb_ref[3:4, sl])
            tail = u[rows - p:, :]
            carry_ref[:, sl] = tail
            st_ref[:, sl] = tail
            ys.append(y)
        act = (_silu(ys[0]) * ys[1]).astype(BF16)
        contrib = jnp.dot(act, wdn_ref[c * FFN_CW:(c + 1) * FFN_CW, :], preferred_element_type=F32)
        if c == 0:
            acc_ref[...] = contrib
        else:
            acc_ref[...] += contrib
    xo_ref[...] = x + _rms(acc_ref[...], nw_ref[1:2, :])


def _ffn(x2d, prev, nw, wup, cwb, wdn, b, t):
    tl = _tiling(b, t, 512)
    G, J, R, P = tl["G"], tl["J"], tl["R"], tl["P"]
    n = b * t
    return pl.pallas_call(
        _ffn_body,
        grid=(G, J),
        in_specs=[
            pl.BlockSpec((R, D_MODEL), lambda g, j: (g * J + j, 0)),
            pl.BlockSpec((P, 2 * D_FF), lambda g, j: (g, 0)),
            _const_spec((8, D_MODEL)),
            _const_spec((D_MODEL, 2 * D_FF)),
            _const_spec((8, 2 * D_FF)),
            _const_spec((D_FF, D_MODEL)),
        ],
        out_specs=[
            pl.BlockSpec((R, D_MODEL), lambda g, j: (g * J + j, 0)),
            pl.BlockSpec((P, 2 * D_FF), lambda g, j: (g, 0)),
        ],
        out_shape=[jax.ShapeDtypeStruct((n, D_MODEL), F32),
                   jax.ShapeDtypeStruct((b * SUBLANES, 2 * D_FF), F32)],
        scratch_shapes=[pltpu.VMEM((P, 2 * D_FF), F32), pltpu.VMEM((R, D_MODEL), F32)],
        compiler_params=_params(2),
        name="conv_ffn",
    )(x2d, prev, nw, wup, cwb, wdn)


def _outproj_body(*refs, gated):
    x_ref, y_ref = refs[0], refs[1]
    wo_ref, nw_ref, xo_ref = refs[-3:]
    y = y_ref[...]
    if gated:
        y = y * refs[2][...]
    o = jnp.dot(y.astype(BF16), wo_ref[...], preferred_element_type=F32)
    xo_ref[...] = x_ref[...] + _rms(o, nw_ref[0:1, :])


def _outproj(x2d, y2d, gate2d, wo, nw, name):
    n, k = y2d.shape
    R = min(n, 512)
    row = lambda i: (i, 0)
    acts = [y2d] if gate2d is None else [y2d, gate2d]
    return pl.pallas_call(
        functools.partial(_outproj_body, gated=gate2d is not None),
        grid=(n // R,),
        in_specs=[pl.BlockSpec((R, D_MODEL), row)] + [pl.BlockSpec((R, k), row)] * len(acts)
        + [_const_spec((k, D_MODEL)), _const_spec((8, D_MODEL))],
        out_specs=pl.BlockSpec((R, D_MODEL), row),
        out_shape=jax.ShapeDtypeStruct((n, D_MODEL), F32),
        compiler_params=_params(1),
        name=name,
    )(x2d, *acts, wo, nw)


def _rwkv_proj_body(*refs, has_vres, chunk):
    it = iter(refs)
    x_ref, prev_ref = next(it), next(it)
    vf_ref = next(it) if has_vres else None
    vec_ref, wrkv_ref, w1_ref, w2_ref, a1_ref, a2_ref = (next(it) for _ in range(6))
    v1_ref, v2_ref = (next(it), next(it)) if has_vres else (None, None)
    g1_ref, g2_ref, tri_ref, ones_ref = (next(it) for _ in range(4))
    rt_ref, kt_ref, at_ref, bt_ref, v_ref, g_ref, gl_ref, hl_ref, carry_ref = (next(it) for _ in range(9))

    @pl.when(pl.program_id(1) == 0)
    def _():
        carry_ref[...] = prev_ref[...]

    x = x_ref[...]
    rows = x.shape[0]
    p = carry_ref.shape[0]
    h = _rms(x, vec_ref[10:11, :])
    d = _shift_rows(h, carry_ref[...], 1) - h
    tail = h[rows - p:, :]
    carry_ref[...] = tail
    hl_ref[...] = tail

    def mix(i):
        return (h + d * vec_ref[i:i + 1, :]).astype(BF16)

    r = jnp.dot(mix(0), wrkv_ref[0], preferred_element_type=F32)
    k = jnp.dot(mix(1), wrkv_ref[1], preferred_element_type=F32)
    xv = mix(2)
    v = jnp.dot(xv, wrkv_ref[2], preferred_element_type=F32)
    w_raw = vec_ref[6:7, :] + _bdot(jnp.tanh(_bdot(mix(3), w1_ref[...])), w2_ref[...])
    w = -_softplus(-w_raw) - 0.5
    if has_vres:
        gate_v = _sigmoid(vec_ref[11:12, :] + _bdot(_bdot(xv, v1_ref[...]), v2_ref[...]))
        v = v + (vf_ref[...] - v) * gate_v
    a = _sigmoid(vec_ref[7:8, :] + _bdot(_bdot(mix(4), a1_ref[...]), a2_ref[...]))
    g_ref[...] = _bdot(_sigmoid(_bdot(mix(5), g1_ref[...])), g2_ref[...])
    v_ref[...] = v

    kk = k * vec_ref[8:9, :]
    kk = kk * lax.rsqrt(_lane_group_sum(kk * kk, ones_ref[...]) + 1e-6)
    k = k * (1.0 + (a - 1.0) * vec_ref[9:10, :])

    lw = -jnp.exp(w)
    cum = _chunk_cumsum(lw, tri_ref[...])
    e_fwd = jnp.exp(cum)
    e_bwd = jnp.exp(-cum)
    rt_ref[...] = r * e_fwd
    kt_ref[...] = k * e_bwd
    at_ref[...] = -kk * jnp.exp(cum - lw)
    bt_ref[...] = kk * a * e_bwd
    for c in range(rows // chunk):
        gl_ref[c] = jnp.exp(cum[(c + 1) * chunk - 1:(c + 1) * chunk, :])


def _rwkv_scan_body(rt_ref, kt_ref, at_ref, bt_ref, v_ref, gl_ref, h0_ref, vec_ref, ones_ref,
                    y_ref, ho_ref, *, nh, chunk):
    @pl.when(pl.program_id(1) == 0)
    def _():
        ho_ref[...] = h0_ref[...]

    gl_lanes = nh * RW_N
    ng = RW_H // nh
    gc = nh * chunk
    row_head = lax.broadcasted_iota(jnp.int32, (gc, gl_lanes), 0) // chunk
    lane_head = lax.broadcasted_iota(jnp.int32, (gc, gl_lanes), 1) // RW_N
    own = row_head == lane_head
    ri = lax.broadcasted_iota(jnp.int32, (gc, gc), 0)
    ci = lax.broadcasted_iota(jnp.int32, (gc, gc), 1)
    same = (ri // chunk) == (ci // chunk)
    strict = same & ((ri % chunk) > (ci % chunk))
    incl = same & ((ri % chunk) >= (ci % chunk))
    eye = (ri == ci).astype(F32)
    eye_l = (lax.broadcasted_iota(jnp.int32, (gl_lanes, gl_lanes), 0)
             == lax.broadcasted_iota(jnp.int32, (gl_lanes, gl_lanes), 1))
    ones_bd = ones_ref[...]

    def blockdiag(xg):
        xx = jnp.concatenate([xg] * nh, axis=0) if nh > 1 else xg
        return jnp.where(own, xx, 0.0)

    for q in range(ng):
        sl = slice(q * gl_lanes, (q + 1) * gl_lanes)
        rt, kt, vv = rt_ref[:, sl], kt_ref[:, sl], v_ref[:, sl]
        r_bd, k_bd, a_bd, b_bd, v_bd = (blockdiag(t) for t in (rt, kt, at_ref[:, sl], bt_ref[:, sl], vv))
        gl_row = gl_ref[0, :, sl]
        a_ab = jnp.where(strict, _bdot_nt(a_bd, b_bd), 0.0)
        a_ak = jnp.where(strict, _bdot_nt(a_bd, k_bd), 0.0)
        a_rb = jnp.where(incl, _bdot_nt(r_bd, b_bd), 0.0)
        a_rk = jnp.where(incl, _bdot_nt(r_bd, k_bd), 0.0)
        tinv = _unit_lower_inverse(a_ab, eye, chunk)
        hs = ho_ref[0, q]
        u = _bdot(tinv, _bdot(a_bd, hs) + _bdot(a_ak, v_bd))
        y_bd = _bdot(r_bd, hs) + _bdot(a_rb, u) + _bdot(a_rk, v_bd)
        gl_col = jnp.sum(jnp.where(eye_l, gl_row, 0.0), axis=1, keepdims=True)
        ho_ref[0, q] = hs * gl_col + _bdot_tn(b_bd * gl_row, u) + _bdot_tn(k_bd * gl_row, v_bd)

        y = y_bd[0:chunk]
        for hh in range(1, nh):
            y = y + y_bd[hh * chunk:(hh + 1) * chunk]
        mu = _lane_group_sum(y, ones_bd) * (1.0 / RW_N)
        yc = y - mu
        var = _lane_group_sum(yc * yc, ones_bd) * (1.0 / RW_N)
        yn = yc * lax.rsqrt(var + RW_LNX_EPS) * vec_ref[0:1, sl] + vec_ref[1:2, sl]
        bonus = _lane_group_sum(rt * kt * vec_ref[2:3, sl], ones_bd) * vv
        y_ref[:, sl] = yn + bonus


def _rwkv_layer(x2d, shift_prev, s0, v_first, w, ri, nw, b, t):
    n = b * t
    chunk = _chunk_of(t)
    tl = _tiling(b, t, 256)
    G, J, R, P = tl["G"], tl["J"], tl["R"], tl["P"]
    has_vres = v_first is not None
    vi = ri - 1
    bc = chunk if chunk == 64 else R
    tri = _chunk_masks(chunk, bc)
    li = jnp.arange(LANES)
    ones_bd = ((li[:, None] // RW_N) == (li[None, :] // RW_N)).astype(F32)
    zero = jnp.zeros((D_MODEL,), F32)
    vec = jnp.stack([*(w["rw_mu"][ri][i] for i in range(6)), w["rw_w0"][ri], w["rw_a0"][ri], w["rw_kk"][ri],
                     w["rw_ka"][ri], nw[0], w["rw_v0"][vi] if has_vres else zero, zero, zero, zero, zero])
    row = lambda g, j: (g * J + j, 0)
    row_spec = pl.BlockSpec((R, D_MODEL), row)
    ins = [x2d, _pad_state(shift_prev[:, None, :])]
    specs = [row_spec, pl.BlockSpec((P, D_MODEL), lambda g, j: (g, 0))]
    if has_vres:
        ins.append(v_first)
        specs.append(row_spec)
    wl = [vec, w["rw_wrkv"][ri].astype(BF16), w["rw_w1"][ri].astype(BF16), w["rw_w2"][ri].astype(BF16),
          w["rw_a1"][ri].astype(BF16), w["rw_a2"][ri].astype(BF16)]
    if has_vres:
        wl += [w["rw_v1"][vi].astype(BF16), w["rw_v2"][vi].astype(BF16)]
    wl += [w["rw_g1"][ri].astype(BF16), w["rw_g2"][ri].astype(BF16), tri, ones_bd]
    ins += wl
    specs += [_const_spec(a.shape) for a in wl]
    nc_tile = R // chunk
    outs = pl.pallas_call(
        functools.partial(_rwkv_proj_body, has_vres=has_vres, chunk=chunk),
        grid=(G, J),
        in_specs=specs,
        out_specs=[row_spec] * 6 + [pl.BlockSpec((nc_tile, 1, D_MODEL), lambda g, j: (g * J + j, 0, 0)),
                                    pl.BlockSpec((P, D_MODEL), lambda g, j: (g, 0))],
        out_shape=[jax.ShapeDtypeStruct((n, D_MODEL), F32)] * 6
        + [jax.ShapeDtypeStruct((n // chunk, 1, D_MODEL), F32), jax.ShapeDtypeStruct((b * SUBLANES, D_MODEL), F32)],
        scratch_shapes=[pltpu.VMEM((P, D_MODEL), F32)],
        compiler_params=_params(2),
        name="rwkv_proj",
    )(*ins)
    rt, kt, at, bt, v, g, gl, hl = outs
    shift_new = hl.reshape(b, SUBLANES, D_MODEL)[:, -1]

    nh = GROUP_ROWS // chunk
    ng = RW_H // nh
    gl_lanes = nh * RW_N
    hkv = jnp.swapaxes(s0, -1, -2).reshape(b, ng, nh, RW_N, RW_N)
    h0 = jnp.einsum("bqikv,ij->bqikjv", hkv, jnp.eye(nh, dtype=F32)).reshape(b, ng, gl_lanes, gl_lanes)
    svec = jnp.stack([w["rw_lnx_w"][ri], w["rw_lnx_b"][ri], w["rw_rk"][ri].reshape(D_MODEL),
                      zero, zero, zero, zero, zero])
    nct = t // chunk
    crow = lambda bb, j: (bb * nct + j, 0)
    cspec = pl.BlockSpec((chunk, D_MODEL), crow)
    hspec = pl.BlockSpec((1, ng, gl_lanes, gl_lanes), lambda bb, j: (bb, 0, 0, 0))
    y, hout = pl.pallas_call(
        functools.partial(_rwkv_scan_body, nh=nh, chunk=chunk),
        grid=(b, nct),
        in_specs=[cspec] * 5 + [pl.BlockSpec((1, 1, D_MODEL), lambda bb, j: (bb * nct + j, 0, 0)), hspec,
                                _const_spec((8, D_MODEL)), _const_spec((LANES, LANES))],
        out_specs=[cspec, hspec],
        out_shape=[jax.ShapeDtypeStruct((n, D_MODEL), F32),
                   jax.ShapeDtypeStruct((b, ng, gl_lanes, gl_lanes), F32)],
        compiler_params=_params(2),
        name="rwkv_scan",
    )(rt, kt, at, bt, v, gl, h0, svec, ones_bd)
    h6 = hout.reshape(b, ng, nh, RW_N, nh, RW_N)
    s_new = jnp.stack([h6[:, :, i, :, i, :] for i in range(nh)], axis=2)
    s_new = jnp.swapaxes(s_new, -1, -2).reshape(b, RW_H, RW_N, RW_N)
    nwp = jnp.concatenate([nw[1:2], jnp.zeros((7, D_MODEL), F32)])
    x_new = _outproj(x2d, y, g, w["rw_wo"][ri].astype(BF16), nwp, "rwkv_out")
    return x_new, shift_new, s_new, (v if not has_vres else v_first)


def _rope_lanes(x, tab_ref):
    half = MLA_ROPE // 2
    return (x * tab_ref[0] + pltpu.roll(x, LANES - half, 1) * tab_ref[1] + pltpu.roll(x, half, 1) * tab_ref[2])


def _mla_proj_body(x_ref, nw_ref, tab_ref, winq_ref, winc_ref, wink_ref, qn_ref, kvn_ref, wqn_ref, wqr_ref,
                   wuk_ref, c_ref, kr_ref, kcat_ref, qcat_ref):
    h = _rms(x_ref[...], nw_ref[0:1, :]).astype(BF16)
    cq = _rms(jnp.dot(h, winq_ref[...], preferred_element_type=F32), qn_ref[...]).astype(BF16)
    c = _rms(jnp.dot(h, winc_ref[...], preferred_element_type=F32), kvn_ref[...])
    kr = _rope_lanes(jnp.dot(h, wink_ref[...], preferred_element_type=F32), tab_ref)
    c_ref[...] = c
    kr_ref[...] = kr
    adt = kcat_ref.dtype
    kcat_ref[:, 0:MLA_KV_LORA] = c.astype(adt)
    kcat_ref[:, MLA_KV_LORA:MLA_QK] = kr.astype(adt)
    qn = jnp.dot(cq, wqn_ref[...], preferred_element_type=F32).astype(BF16)
    qr = jnp.dot(cq, wqr_ref[...], preferred_element_type=F32)
    for pr in range(MLA_H // 2):
        ql = jnp.dot(qn[:, pr * LANES:(pr + 1) * LANES], wuk_ref[pr], preferred_element_type=F32)
        qcat_ref[2 * pr, :, 0:MLA_KV_LORA] = ql[:, :MLA_KV_LORA].astype(adt)
        qcat_ref[2 * pr + 1, :, 0:MLA_KV_LORA] = ql[:, MLA_KV_LORA:].astype(adt)
    for hh in range(MLA_H):
        qcat_ref[hh, :, MLA_KV_LORA:MLA_QK] = _rope_lanes(qr[:, hh * LANES:(hh + 1) * LANES], tab_ref).astype(adt)


MLA_TQ = 64
MLA_TK = 256


def _mla_prompt_body(q_ref, k_ref, o_ref, m_ref, l_ref, acc_ref):
    i = pl.program_id(1)
    rows = MLA_H * MLA_TQ
    q = q_ref[...].reshape(rows, MLA_QK)
    m_ref[...] = jnp.full((rows, 1), -jnp.inf, F32)
    l_ref[...] = jnp.zeros((rows, 1), F32)
    acc_ref[...] = jnp.zeros((rows, MLA_KV_LORA), F32)
    qpos = i * MLA_TQ + lax.broadcasted_iota(jnp.int32, (rows, MLA_TK), 0) % MLA_TQ
    koff = lax.broadcasted_iota(jnp.int32, (rows, MLA_TK), 1)

    def step(kb, carry):
        k0 = pl.multiple_of(kb * MLA_TK, MLA_TK)
        kblk = k_ref[pl.ds(k0, MLA_TK), :]
        s = lax.dot_general(q, kblk, (((1,), (1,)), ((), ())), preferred_element_type=F32) * MLA_SCALE
        s = jnp.where(koff + k0 <= qpos, s, -jnp.inf)
        m_old = m_ref[...]
        m_new = jnp.maximum(m_old, jnp.max(s, axis=1, keepdims=True))
        alpha = jnp.exp(m_old - m_new)
        p = jnp.exp(s - m_new)
        l_ref[...] = l_ref[...] * alpha + jnp.sum(p, axis=1, keepdims=True)
        acc_ref[...] = acc_ref[...] * alpha + jnp.dot(p.astype(BF16), kblk[:, :MLA_KV_LORA],
                                                      preferred_element_type=F32)
        m_ref[...] = m_new
        return carry

    lax.fori_loop(0, (i * MLA_TQ + MLA_TQ + MLA_TK - 1) // MLA_TK, step, 0)
    o = acc_ref[...] / l_ref[...]
    o_ref[...] = o.reshape(MLA_H, MLA_TQ, MLA_KV_LORA).astype(BF16)


MLA_PP = 8


def _mla_sample_body(pt_ref, q_ref, kn_ref, *rest):
    lat_refs = rest[:MLA_PP]
    kro_refs = rest[MLA_PP:2 * MLA_PP]
    o_ref, m_ref, l_ref, acc_ref = rest[2 * MLA_PP:]
    j = pl.program_id(1)
    t = q_ref.shape[1]
    rows = MLA_H * t
    q = q_ref[...].reshape(rows, MLA_QK).astype(BF16)
    ql = q[:, :MLA_KV_LORA]
    qr = q[:, MLA_KV_LORA:MLA_KV_LORA + MLA_ROPE]

    @pl.when(j == 0)
    def _():
        m_ref[...] = jnp.full((rows, 1), -jnp.inf, F32)
        l_ref[...] = jnp.zeros((rows, 1), F32)
        acc_ref[...] = jnp.zeros((rows, MLA_KV_LORA), F32)

    def update(s_list, v_list):
        m_old = m_ref[...]
        m_new = m_old
        for s in s_list:
            m_new = jnp.maximum(m_new, jnp.max(s, axis=1, keepdims=True))
        alpha = jnp.exp(m_old - m_new)
        l_new = l_ref[...] * alpha
        acc = acc_ref[...] * alpha
        for s, vblk in zip(s_list, v_list):
            p = jnp.exp(s - m_new)
            l_new = l_new + jnp.sum(p, axis=1, keepdims=True)
            acc = acc + jnp.dot(p.astype(BF16), vblk, preferred_element_type=F32)
        m_ref[...] = m_new
        l_ref[...] = l_new
        acc_ref[...] = acc

    s_list, v_list = [], []
    for pp in range(MLA_PP):
        cb = lat_refs[pp][0].astype(BF16)
        kb = kro_refs[pp][0].astype(BF16)
        s = (lax.dot_general(ql, cb, (((1,), (1,)), ((), ())), preferred_element_type=F32)
             + lax.dot_general(qr, kb, (((1,), (1,)), ((), ())), preferred_element_type=F32)) * MLA_SCALE
        s_list.append(s)
        v_list.append(cb)
    update(s_list, v_list)

    @pl.when(j == pl.num_programs(1) - 1)
    def _():
        kn = kn_ref[...].astype(BF16)
        s = lax.dot_general(q, kn, (((1,), (1,)), ((), ())), preferred_element_type=F32) * MLA_SCALE
        qpos = lax.broadcasted_iota(jnp.int32, (rows, t), 0) % t
        kpos = lax.broadcasted_iota(jnp.int32, (rows, t), 1)
        s = jnp.where(kpos <= qpos, s, -jnp.inf)
        update([s], [kn[:, :MLA_KV_LORA]])
        o = acc_ref[...] / l_ref[...]
        o_ref[...] = o.reshape(MLA_H, t, MLA_KV_LORA).astype(o_ref.dtype)


def _mla_out_body(x_ref, o_ref, wuv_ref, wo_ref, nw_ref, xo_ref):
    parts = []
    for pr in range(MLA_H // 2):
        wp = wuv_ref[pr]
        parts.append(jnp.dot(o_ref[2 * pr].astype(BF16), wp[:MLA_KV_LORA], preferred_element_type=F32)
                     + jnp.dot(o_ref[2 * pr + 1].astype(BF16), wp[MLA_KV_LORA:], preferred_element_type=F32))
    v = jnp.concatenate(parts, axis=1).astype(BF16)
    o = jnp.dot(v, wo_ref[...], preferred_element_type=F32)
    xo_ref[...] = x_ref[...] + _rms(o, nw_ref[0:1, :])


def _mla_layer(x2d, pos, w, mi, nw, b, t, paged):
    n = b * t
    tl = _tiling(b, t, 512)
    G, J, R = tl["G"], tl["J"], tl["R"]
    adt = BF16 if t % 16 == 0 else F32
    half = MLA_ROPE // 2
    inv = ROPE_THETA ** (-jnp.arange(half, dtype=F32) / half)
    ang = pos.astype(F32)[:, None] * inv[None, :]
    cos, sin = jnp.cos(ang), jnp.sin(ang)
    zpad = jnp.zeros((t, LANES - MLA_ROPE), F32)
    zh = jnp.zeros((t, half), F32)
    tab = jnp.stack([jnp.concatenate([cos, cos, zpad], 1), jnp.concatenate([-sin, zh, zpad], 1),
                     jnp.concatenate([zh, sin, zpad], 1)])
    if G == 1:
        tab = jnp.tile(tab, (1, b, 1))
    w_in = w["mla_w_in"][mi]
    winq = w_in[:, :MLA_Q_LORA].astype(BF16)
    winc = w_in[:, MLA_Q_LORA:MLA_Q_LORA + MLA_KV_LORA].astype(BF16)
    wink = jnp.pad(w_in[:, MLA_Q_LORA + MLA_KV_LORA:], ((0, 0), (0, LANES - MLA_ROPE))).astype(BF16)
    wqb = w["mla_w_qb"][mi].reshape(MLA_Q_LORA, MLA_H, MLA_NOPE + MLA_ROPE)
    wqn = wqb[:, :, :MLA_NOPE].reshape(MLA_Q_LORA, MLA_H * MLA_NOPE).astype(BF16)
    wqr = jnp.pad(wqb[:, :, MLA_NOPE:], ((0, 0), (0, 0), (0, LANES - MLA_ROPE))
                  ).reshape(MLA_Q_LORA, MLA_H * LANES).astype(BF16)
    wuk = jnp.transpose(w["mla_w_uk"][mi], (1, 2, 0)).reshape(MLA_H // 2, 2, MLA_NOPE, MLA_KV_LORA)
    wuk_bd = jnp.einsum("pinc,ij->pinjc", wuk, jnp.eye(2, dtype=F32)).reshape(
        MLA_H // 2, 2 * MLA_NOPE, 2 * MLA_KV_LORA).astype(BF16)
    wuv = jnp.transpose(w["mla_w_uv"][mi], (1, 0, 2)).reshape(MLA_H // 2, 2, MLA_KV_LORA, MLA_V)
    wuv_bd = jnp.einsum("picv,ij->picjv", wuv, jnp.eye(2, dtype=F32)).reshape(
        MLA_H // 2, 2 * MLA_KV_LORA, 2 * MLA_V).astype(BF16)
    nwa = jnp.concatenate([nw[0:1], jnp.zeros((7, D_MODEL), F32)])
    nwb = jnp.concatenate([nw[1:2], jnp.zeros((7, D_MODEL), F32)])
    row = lambda g, j: (g * J + j, 0)
    wl = [winq, winc, wink, w["mla_q_norm"][mi][None, :], w["mla_kv_norm"][mi][None, :], wqn, wqr, wuk_bd]
    c, kr, kcat, qcat = pl.pallas_call(
        _mla_proj_body,
        grid=(G, J),
        in_specs=[pl.BlockSpec((R, D_MODEL), row), _const_spec((8, D_MODEL)),
                  pl.BlockSpec((3, R, LANES), lambda g, j: (0, j, 0))] + [_const_spec(a.shape) for a in wl],
        out_specs=[pl.BlockSpec((R, MLA_KV_LORA), row), pl.BlockSpec((R, LANES), row),
                   pl.BlockSpec((R, MLA_QK), row), pl.BlockSpec((MLA_H, R, MLA_QK), lambda g, j: (0, g * J + j, 0))],
        out_shape=[jax.ShapeDtypeStruct((n, MLA_KV_LORA), F32), jax.ShapeDtypeStruct((n, LANES), F32),
                   jax.ShapeDtypeStruct((n, MLA_QK), adt), jax.ShapeDtypeStruct((MLA_H, n, MLA_QK), adt)],
        compiler_params=_params(2),
        name="mla_proj",
    )(x2d, nwa, tab, *wl)

    if paged is None:
        nq = t // MLA_TQ
        rows = MLA_H * MLA_TQ
        o = pl.pallas_call(
            _mla_prompt_body,
            grid=(b, nq),
            in_specs=[pl.BlockSpec((MLA_H, MLA_TQ, MLA_QK), lambda bb, i: (0, bb * nq + i, 0)),
                      pl.BlockSpec((t, MLA_QK), lambda bb, i: (bb, 0))],
            out_specs=pl.BlockSpec((MLA_H, MLA_TQ, MLA_KV_LORA), lambda bb, i: (0, bb * nq + i, 0)),
            out_shape=jax.ShapeDtypeStruct((MLA_H, n, MLA_KV_LORA), BF16),
            scratch_shapes=[pltpu.VMEM((rows, 1), F32), pltpu.VMEM((rows, 1), F32),
                            pltpu.VMEM((rows, MLA_KV_LORA), F32)],
            compiler_params=_params(2),
            name="mla_attend_prompt",
        )(qcat, kcat)
    else:
        pages_c, pages_kr, page_table = paged
        page = pages_c.shape[1]
        npg = page_table.shape[1]
        assert npg % MLA_PP == 0
        rows = MLA_H * t

        def page_map(pp):
            return lambda bb, j, pt: (pt[bb, j * MLA_PP + pp], 0, 0)

        grid_spec = pltpu.PrefetchScalarGridSpec(
            num_scalar_prefetch=1,
            grid=(b, npg // MLA_PP),
            in_specs=[pl.BlockSpec((MLA_H, t, MLA_QK), lambda bb, j, pt: (0, bb, 0)),
                      pl.BlockSpec((t, MLA_QK), lambda bb, j, pt: (bb, 0))]
            + [pl.BlockSpec((1, page, MLA_KV_LORA), page_map(pp)) for pp in range(MLA_PP)]
            + [pl.BlockSpec((1, page, MLA_ROPE), page_map(pp)) for pp in range(MLA_PP)],
            out_specs=pl.BlockSpec((MLA_H, t, MLA_KV_LORA), lambda bb, j, pt: (0, bb, 0)),
            scratch_shapes=[pltpu.VMEM((rows, 1), F32), pltpu.VMEM((rows, 1), F32),
                            pltpu.VMEM((rows, MLA_KV_LORA), F32)],
        )
        o = pl.pallas_call(
            _mla_sample_body,
            grid_spec=grid_spec,
            out_shape=jax.ShapeDtypeStruct((MLA_H, n, MLA_KV_LORA), adt),
            compiler_params=_params(2),
            name="mla_attend_sample",
        )(page_table, qcat, kcat, *([pages_c] * MLA_PP), *([pages_kr] * MLA_PP))

    Ro = min(n, 512)
    x_new = pl.pallas_call(
        _mla_out_body,
        grid=(n // Ro,),
        in_specs=[pl.BlockSpec((Ro, D_MODEL), lambda i: (i, 0)),
                  pl.BlockSpec((MLA_H, Ro, MLA_KV_LORA), lambda i: (0, i, 0)),
                  _const_spec(wuv_bd.shape), _const_spec((MLA_H * MLA_V, D_MODEL)), _const_spec((8, D_MODEL))],
        out_specs=pl.BlockSpec((Ro, D_MODEL), lambda i: (i, 0)),
        out_shape=jax.ShapeDtypeStruct((n, D_MODEL), F32),
        compiler_params=_params(1),
        name="mla_out",
    )(x2d, o, wuv_bd, w["mla_wo"][mi].astype(BF16), nwb)
    return x_new, c.reshape(b, t, MLA_KV_LORA), kr[:, :MLA_ROPE].reshape(b, t, MLA_ROPE)


GDN_CW = 512


def _gdn_proj_body(x_ref, prev_ref, nw_ref, wqkv_ref, wz_ref, wbg_ref, cw_ref, gvec_ref, tri_ref,
                   q_ref, k_ref, v_ref, z_ref, beta_ref, gc_ref, st_ref, carry_ref):
    @pl.when(pl.program_id(1) == 0)
    def _():
        carry_ref[...] = prev_ref[...]

    rows = x_ref.shape[0]
    p = carry_ref.shape[0]
    h = _rms(x_ref[...], nw_ref[0:1, :]).astype(BF16)
    z_ref[...] = jnp.dot(h, wz_ref[...], preferred_element_type=F32)
    bg = jnp.dot(h, wbg_ref[...], preferred_element_type=F32)
    beta_ref[...] = _sigmoid(bg)
    g = -jnp.exp(gvec_ref[0:1, :]) * _softplus(bg + gvec_ref[1:2, :])
    gc_ref[...] = _chunk_cumsum(g, tri_ref[...])
    for c in range(GDN_CONV_DIM // GDN_CW):
        sl = slice(c * GDN_CW, (c + 1) * GDN_CW)
        u = jnp.dot(h, wqkv_ref[:, sl], preferred_element_type=F32)
        prev = carry_ref[:, sl]
        y = cw_ref[3:4, sl] * u
        for s in range(1, GDN_CONV):
            y = y + cw_ref[3 - s:4 - s, sl] * _shift_rows(u, prev, s)
        tail = u[rows - p:, :]
        carry_ref[:, sl] = tail
        st_ref[:, sl] = tail
        y = _silu(y)
        off = c * GDN_CW
        if off < 2 * GDN_QK_DIM:
            dst, base, scale = (q_ref, off, GDN_DK ** -0.5) if off < GDN_QK_DIM else (k_ref, off - GDN_QK_DIM, 1.0)
            for hh in range(GDN_CW // GDN_DK):
                yh = y[:, hh * GDN_DK:(hh + 1) * GDN_DK]
                yh = yh * lax.rsqrt(jnp.sum(yh * yh, axis=-1, keepdims=True) + 1e-6)
                dst[:, base + hh * GDN_DK:base + (hh + 1) * GDN_DK] = yh * scale if scale != 1.0 else yh
        else:
            v_ref[:, off - 2 * GDN_QK_DIM:off - 2 * GDN_QK_DIM + GDN_CW] = y


def _gdn_chunk_body(q_ref, k_ref, v_ref, z_ref, gcol_ref, grow_ref, bcol_ref, s0_ref, nw_ref,
                    o_ref, so_ref, *, nh, chunk):
    @pl.when(pl.program_id(1) == 0)
    def _():
        so_ref[...] = s0_ref[...]

    ng = GDN_V_H // nh
    gc = nh * chunk
    rep = GDN_V_H // GDN_QK_H
    ri = lax.broadcasted_iota(jnp.int32, (gc, gc), 0)
    ci = lax.broadcasted_iota(jnp.int32, (gc, gc), 1)
    same = (ri // chunk) == (ci // chunk)
    strict = same & ((ri % chunk) > (ci % chunk))
    incl = same & ((ri % chunk) >= (ci % chunk))
    last = same & ((ci % chunk) == chunk - 1)
    eye = (ri == ci).astype(F32)
    row_head = lax.broadcasted_iota(jnp.int32, (gc, GDN_DK), 0) // chunk

    def stack(ref, heads, width):
        parts = [ref[:, hd * width:(hd + 1) * width] for hd in heads]
        return parts[0] if len(parts) == 1 else jnp.concatenate(parts, axis=0)

    for q in range(ng):
        heads = [q * nh + i for i in range(nh)]
        k_st = stack(k_ref, [hd // rep for hd in heads], GDN_DK)
        q_st = stack(q_ref, [hd // rep for hd in heads], GDN_DK)
        v_st = stack(v_ref, heads, GDN_DV)
        gcol = gcol_ref[0, :, q:q + 1]
        grow = grow_ref[0, q:q + 1, :]
        bcol = bcol_ref[0, :, q:q + 1]
        decay = jnp.exp(jnp.where(incl, gcol - grow, -jnp.inf))
        kk = _bdot_nt(k_st, k_st)
        a = jnp.where(strict, kk * bcol * decay, 0.0)
        tinv = _unit_lower_inverse(-a, eye, chunk)
        egc = jnp.exp(gcol)
        u = _bdot(tinv, v_st * bcol)
        wm = _bdot(tinv, k_st * (bcol * egc))
        aqk = jnp.where(incl, _bdot_nt(q_st, k_st) * decay, 0.0)
        qg = q_st * egc
        glast = jnp.sum(jnp.where(last, grow, 0.0), axis=1, keepdims=True)
        kg = k_st * jnp.exp(glast - gcol)
        states = [so_ref[0, hd] for hd in heads]
        ws = None
        qs = None
        for i, s_h in enumerate(states):
            mine = row_head == i
            t1 = _bdot(jnp.where(mine, wm, 0.0), s_h)
            t2 = _bdot(jnp.where(mine, qg, 0.0), s_h)
            ws = t1 if ws is None else ws + t1
            qs = t2 if qs is None else qs + t2
        v_new = u - ws
        o_st = qs + _bdot(aqk, v_new)
        for i, hd in enumerate(heads):
            mine = row_head == i
            gl_h = jnp.exp(glast[i * chunk:i * chunk + 1, :])
            so_ref[0, hd] = states[i] * gl_h + _bdot_tn(jnp.where(mine, kg, 0.0), v_new)
        z_st = stack(z_ref, heads, GDN_DV)
        o_st = _rms(o_st, nw_ref[0:1, :]) * _silu(z_st)
        for i, hd in enumerate(heads):
            o_ref[:, hd * GDN_DV:(hd + 1) * GDN_DV] = o_st[i * chunk:(i + 1) * chunk]


def _gdn_layer(x2d, conv_prev, s0, w, gi, nw, b, t):
    n = b * t
    chunk = _chunk_of(t)
    tl = _tiling(b, t, 256)
    G, J, R, P = tl["G"], tl["J"], tl["R"], tl["P"]
    w_in = w["gdn_w_in"][gi]
    o1 = GDN_CONV_DIM
    o2 = o1 + GDN_V_DIM
    wqkv = w_in[:, :o1].astype(BF16)
    wz = w_in[:, o1:o2].astype(BF16)
    wbg = jnp.pad(w_in[:, o2:], ((0, 0), (0, LANES - 2 * GDN_V_H))).astype(BF16)
    cw = jnp.pad(w["gdn_conv_w"][gi], ((0, 8 - GDN_CONV), (0, 0)))
    gvec = jnp.zeros((8, LANES), F32)
    gvec = gvec.at[0, GDN_V_H:2 * GDN_V_H].set(w["gdn_a_log"][gi]).at[1, GDN_V_H:2 * GDN_V_H].set(w["gdn_dt_bias"][gi])
    bc = chunk if chunk == 64 else R
    tri = _chunk_masks(chunk, bc)
    nwa = jnp.concatenate([nw[0:1], jnp.zeros((7, D_MODEL), F32)])
    row = lambda g, j: (g * J + j, 0)
    st_spec = pl.BlockSpec((P, GDN_CONV_DIM), lambda g, j: (g, 0))
    qn, kn, v, z, beta, gcs, st = pl.pallas_call(
        _gdn_proj_body,
        grid=(G, J),
        in_specs=[pl.BlockSpec((R, D_MODEL), row), st_spec, _const_spec((8, D_MODEL)), _const_spec(wqkv.shape),
                  _const_spec(wz.shape), _const_spec(wbg.shape), _const_spec(cw.shape), _const_spec(gvec.shape),
                  _const_spec(tri.shape)],
        out_specs=[pl.BlockSpec((R, GDN_QK_DIM), row), pl.BlockSpec((R, GDN_QK_DIM), row),
                   pl.BlockSpec((R, GDN_V_DIM), row), pl.BlockSpec((R, GDN_V_DIM), row),
                   pl.BlockSpec((R, LANES), row), pl.BlockSpec((R, LANES), row), st_spec],
        out_shape=[jax.ShapeDtypeStruct((n, GDN_QK_DIM), F32), jax.ShapeDtypeStruct((n, GDN_QK_DIM), F32),
                   jax.ShapeDtypeStruct((n, GDN_V_DIM), F32), jax.ShapeDtypeStruct((n, GDN_V_DIM), F32),
                   jax.ShapeDtypeStruct((n, LANES), F32), jax.ShapeDtypeStruct((n, LANES), F32),
                   jax.ShapeDtypeStruct((b * SUBLANES, GDN_CONV_DIM), F32)],
        scratch_shapes=[pltpu.VMEM((P, GDN_CONV_DIM), F32)],
        compiler_params=_params(2),
        name="gdn_proj",
    )(x2d, _pad_state(conv_prev), nwa, wqkv, wz, wbg, cw, gvec, tri)
    conv_new = st.reshape(b, SUBLANES, GDN_CONV_DIM)[:, SUBLANES - (GDN_CONV - 1):]

    nh = GROUP_ROWS // chunk
    ng = GDN_V_H // nh
    nct = t // chunk
    ncs = n // chunk
    grow = jnp.swapaxes(gcs[:, GDN_V_H:2 * GDN_V_H].reshape(ncs, chunk, GDN_V_H), 1, 2).reshape(ncs, ng, nh * chunk)
    brow = jnp.swapaxes(beta[:, :GDN_V_H].reshape(ncs, chunk, GDN_V_H), 1, 2).reshape(ncs, ng, nh * chunk)
    gcol = jnp.swapaxes(grow, 1, 2)
    bcol = jnp.swapaxes(brow, 1, 2)
    crow = lambda bb, j: (bb * nct + j, 0)
    c3 = lambda bb, j: (bb * nct + j, 0, 0)
    sspec = pl.BlockSpec((1, GDN_V_H, GDN_DK, GDN_DV), lambda bb, j: (bb, 0, 0, 0))
    nwn = jnp.concatenate([w["gdn_norm_w"][gi][None, :], jnp.zeros((7, GDN_DV), F32)])
    o, s_new = pl.pallas_call(
        functools.partial(_gdn_chunk_body, nh=nh, chunk=chunk),
        grid=(b, nct),
        in_specs=[pl.BlockSpec((chunk, GDN_QK_DIM), crow), pl.BlockSpec((chunk, GDN_QK_DIM), crow),
                  pl.BlockSpec((chunk, GDN_V_DIM), crow), pl.BlockSpec((chunk, GDN_V_DIM), crow),
                  pl.BlockSpec((1, nh * chunk, ng), c3), pl.BlockSpec((1, ng, nh * chunk), c3),
                  pl.BlockSpec((1, nh * chunk, ng), c3), sspec, _const_spec((8, GDN_DV))],
        out_specs=[pl.BlockSpec((chunk, GDN_V_DIM), crow), sspec],
        out_shape=[jax.ShapeDtypeStruct((n, GDN_V_DIM), F32),
                   jax.ShapeDtypeStruct((b, GDN_V_H, GDN_DK, GDN_DV), F32)],
        compiler_params=_params(2),
        name="gdn_chunk",
    )(qn, kn, v, z, gcol, grow, bcol, s0, nwn)
    nwb = jnp.concatenate([nw[1:2], jnp.zeros((7, D_MODEL), F32)])
    x_new = _outproj(x2d, o, None, w["gdn_wo"][gi].astype(BF16), nwb, "gdn_out")
    return x_new, conv_new, s_new


def _trunk(x, pos, rw_s, rw_shift, gdn_s, gdn_conv, ffn_conv, w, paged):
    b, t, _ = x.shape
    x2d = x.reshape(b * t, D_MODEL)
    new = {k: [] for k in ("rw_S", "rw_shift", "mla_c", "mla_kr", "gdn_S", "gdn_conv", "ffn_conv")}
    v_first = None
    ri = mi = gi = 0
    for l, kind in enumerate(LAYER_MIXER):
        nw = w["norm_w"][l]
        if kind == 0:
            x2d, sh, s_new, v_first = _rwkv_layer(x2d, rw_shift[ri], rw_s[ri], v_first, w, ri, nw, b, t)
            new["rw_S"].append(s_new)
            new["rw_shift"].append(sh)
            ri += 1
        elif kind == 1:
            x2d, c, kr = _mla_layer(x2d, pos, w, mi, nw, b, t, None if paged is None else
                                    (paged[0][mi], paged[1][mi], paged[2]))
            new["mla_c"].append(c)
            new["mla_kr"].append(kr)
            mi += 1
        else:
            x2d, cb, s_new = _gdn_layer(x2d, gdn_conv[gi], gdn_s[gi], w, gi, nw, b, t)
            new["gdn_S"].append(s_new)
            new["gdn_conv"].append(cb)
            gi += 1
        nwf = jnp.concatenate([nw[2:4], jnp.zeros((6, D_MODEL), F32)])
        cwb = jnp.concatenate([w["ffn_conv_w"][l], w["ffn_conv_b"][l][None, :],
                               jnp.zeros((8 - FFN_CONV - 1, 2 * D_FF), F32)])
        x2d, st = _ffn(x2d, _pad_state(ffn_conv[l]), nwf, w["ffn_w_up"][l].astype(BF16), cwb,
                       w["ffn_w_down"][l].astype(BF16), b, t)
        new["ffn_conv"].append(st.reshape(b, SUBLANES, 2 * D_FF)[:, SUBLANES - (FFN_CONV - 1):])
    return x2d.reshape(b, t, D_MODEL), {k: jnp.stack(v) for k, v in new.items()}


def kernel(x_prompt, x_sample, state_rwkv_wkv, state_rwkv_shift, cache_mla_latent, cache_mla_krope, state_gdn_S, state_gdn_conv, state_ffn_conv, page_table, norm_w, rw_mu, rw_wrkv, rw_w0, rw_w1, rw_w2, rw_a0, rw_a1, rw_a2, rw_v0, rw_v1, rw_v2, rw_g1, rw_g2, rw_kk, rw_ka, rw_rk, rw_lnx_w, rw_lnx_b, rw_wo, mla_w_in, mla_q_norm, mla_kv_norm, mla_w_qb, mla_w_uk, mla_w_uv, mla_wo, gdn_w_in, gdn_conv_w, gdn_a_log, gdn_dt_bias, gdn_norm_w, gdn_wo, ffn_w_up, ffn_conv_w, ffn_conv_b, ffn_w_down):
    w = dict(norm_w=norm_w, rw_mu=rw_mu, rw_wrkv=rw_wrkv, rw_w0=rw_w0, rw_w1=rw_w1, rw_w2=rw_w2, rw_a0=rw_a0,
             rw_a1=rw_a1, rw_a2=rw_a2, rw_v0=rw_v0, rw_v1=rw_v1, rw_v2=rw_v2, rw_g1=rw_g1, rw_g2=rw_g2,
             rw_kk=rw_kk, rw_ka=rw_ka, rw_rk=rw_rk, rw_lnx_w=rw_lnx_w, rw_lnx_b=rw_lnx_b, rw_wo=rw_wo,
             mla_w_in=mla_w_in, mla_q_norm=mla_q_norm, mla_kv_norm=mla_kv_norm, mla_w_qb=mla_w_qb,
             mla_w_uk=mla_w_uk, mla_w_uv=mla_w_uv, mla_wo=mla_wo, gdn_w_in=gdn_w_in, gdn_conv_w=gdn_conv_w,
             gdn_a_log=gdn_a_log, gdn_dt_bias=gdn_dt_bias, gdn_norm_w=gdn_norm_w, gdn_wo=gdn_wo,
             ffn_w_up=ffn_w_up, ffn_conv_w=ffn_conv_w, ffn_conv_b=ffn_conv_b, ffn_w_down=ffn_w_down)
    b, t = x_prompt.shape[0], x_prompt.shape[1]
    n_rw, n_gdn, depth = state_rwkv_wkv.shape[0], state_gdn_S.shape[0], state_ffn_conv.shape[0]
    y_p, sp = _trunk(
        x_prompt, jnp.arange(t),
        jnp.zeros((n_rw, b) + state_rwkv_wkv.shape[2:], F32), jnp.zeros((n_rw, b, D_MODEL), F32),
        jnp.zeros((n_gdn, b) + state_gdn_S.shape[2:], F32), jnp.zeros((n_gdn, b) + state_gdn_conv.shape[2:], F32),
        jnp.zeros((depth, b) + state_ffn_conv.shape[2:], F32), w, None)
    past_len = page_table.shape[1] * cache_mla_latent.shape[2]
    pos_s = past_len + jnp.arange(x_sample.shape[1])
    y_s, ss = _trunk(x_sample, pos_s, state_rwkv_wkv, state_rwkv_shift, state_gdn_S, state_gdn_conv,
                     state_ffn_conv, w, (cache_mla_latent, cache_mla_krope, page_table))
    names = ("rw_S", "rw_shift", "mla_c", "mla_kr", "gdn_S", "gdn_conv", "ffn_conv")
    return (y_p, y_s) + tuple(sp[k] for k in names) + tuple(ss[k] for k in names)
```

```python
import functools

import jax
import jax.numpy as jnp
from jax import lax
from jax.experimental import pallas as pl
from jax.experimental.pallas import tpu as pltpu

F32 = jnp.float32
BF16 = jnp.bfloat16
HIGHEST = lax.Precision.HIGHEST

D_MODEL = 1024
NORM_EPS = 1e-6
RW_N = 64
RW_H = D_MODEL // RW_N
RW_LNX_EPS = 64e-5
MLA_H = 16
MLA_NOPE = 64
MLA_ROPE = 32
MLA_V = 64
MLA_Q_LORA = 512
MLA_KV_LORA = 256
MLA_SCALE = (MLA_NOPE + MLA_ROPE) ** -0.5
ROPE_THETA = 10000.0
MLA_QK = MLA_KV_LORA + 128
GDN_QK_H = 8
GDN_V_H = 16
GDN_DK = 128
GDN_DV = 128
GDN_QK_DIM = GDN_QK_H * GDN_DK
GDN_V_DIM = GDN_V_H * GDN_DV
GDN_CONV_DIM = 2 * GDN_QK_DIM + GDN_V_DIM
GDN_CONV = 4
D_FF = 2816
FFN_CONV = 3
LAYER_MIXER = (0, 1, 2, 0)

SUBLANES = 8
LANES = 128
GROUP_ROWS = 128
VMEM_LIMIT = 56 * 1024 * 1024


def _rms(x, w):
    return x * lax.rsqrt(jnp.mean(x * x, axis=-1, keepdims=True) + NORM_EPS) * w


def _bdot(a, b):
    return jnp.dot(a.astype(BF16), b.astype(BF16), preferred_element_type=F32)


def _bdot_nt(a, b):
    return lax.dot_general(a.astype(BF16), b.astype(BF16), (((1,), (1,)), ((), ())),
                           preferred_element_type=F32)


def _bdot_tn(a, b):
    return lax.dot_general(a.astype(BF16), b.astype(BF16), (((0,), (0,)), ((), ())),
                           preferred_element_type=F32)


def _hdot(a, b):
    return jnp.dot(a, b, precision=HIGHEST, preferred_element_type=F32)


def _sigmoid(x):
    return 1.0 / (1.0 + jnp.exp(-x))


def _softplus(x):
    return jnp.maximum(x, 0.0) + jnp.log(1.0 + jnp.exp(-jnp.abs(x)))


def _silu(x):
    return x * _sigmoid(x)


def _shift_rows(u, prev, s):
    rows, cols = u.shape
    p = prev.shape[0]
    rolled = pltpu.roll(u, s, 0)
    fix = pltpu.roll(prev, (p - SUBLANES + s) % p, 0)
    t = lax.broadcasted_iota(jnp.int32, (p, cols), 0) % SUBLANES
    if p == rows:
        return jnp.where(t < s, fix, rolled)
    head = jnp.where(t < s, fix, rolled[:SUBLANES])
    return jnp.concatenate([head, rolled[SUBLANES:]], axis=0)


def _lane_group_sum(x, ones2):
    parts = []
    for i in range(x.shape[1] // LANES):
        xs = x[:, i * LANES:(i + 1) * LANES]
        hi = xs.astype(BF16)
        lo = (xs - hi.astype(F32)).astype(BF16)
        parts.append(jnp.dot(jnp.concatenate([hi, lo], axis=1), ones2, preferred_element_type=F32))
    return parts[0] if len(parts) == 1 else jnp.concatenate(parts, axis=1)


def _chunk_cumsum(x, tri):
    bc = tri.shape[0]
    parts = [_hdot(tri, x[i * bc:(i + 1) * bc]) for i in range(x.shape[0] // bc)]
    return parts[0] if len(parts) == 1 else jnp.concatenate(parts, axis=0)


def _tiling(b, t, tt_max):
    if t == SUBLANES:
        return dict(G=1, J=1, R=b * t, P=b * t)
    tt = min(t, tt_max)
    assert t % tt == 0 and tt % 64 == 0, (t, tt)
    return dict(G=b, J=t // tt, R=tt, P=SUBLANES)


def _chunk_of(t):
    return 64 if t % 64 == 0 else t


def _const_spec(shape):
    nd = len(shape)
    return pl.BlockSpec(shape, lambda *_: (0,) * nd, pipeline_mode=pl.Buffered(1))


def _params(n_axes):
    return pltpu.CompilerParams(dimension_semantics=("arbitrary",) * n_axes,
                                vmem_limit_bytes=VMEM_LIMIT)


def _pad_state(st):
    b, k1, c = st.shape
    return jnp.pad(st, ((0, 0), (SUBLANES - k1, 0), (0, 0))).reshape(b * SUBLANES, c)


def _chunk_masks(chunk, rows):
    i = jnp.arange(rows)
    same = (i[:, None] // chunk) == (i[None, :] // chunk)
    tri = same & ((i[None, :] % chunk) <= (i[:, None] % chunk))
    return tri.astype(F32)


FFN_CW = 256


def _ffn_body(x_ref, prev_ref, nw_ref, wup_ref, cwb_ref, wdn_ref, xo_ref, st_ref, carry_ref, acc_ref):
    @pl.when(pl.program_id(1) == 0)
    def _():
        carry_ref[...] = prev_ref[...]

    x = x_ref[...]
    rows = x.shape[0]
    p = carry_ref.shape[0]
    h = _rms(x, nw_ref[0:1, :]).astype(BF16)
    for c in range(D_FF // FFN_CW):
        ys = []
        for half in range(2):
            sl = slice(half * D_FF + c * FFN_CW, half * D_FF + (c + 1) * FFN_CW)
            u = jnp.dot(h, wup_ref[:, sl], preferred_element_type=F32)
            prev = carry_ref[:, sl]
            y = (cwb_ref[0:1, sl] * _shift_rows(u, prev, 2) + cwb_ref[1:2, sl] * _shift_rows(u, prev, 1)
                 + cwb_ref[2:3, sl] * u + cwb_ref[3:4, sl])
            tail = u[rows - p:, :]
            carry_ref[:, sl] = tail
            st_ref[:, sl] = tail
            ys.append(y)
        act = (_silu(ys[0]) * ys[1]).astype(BF16)
        contrib = jnp.dot(act, wdn_ref[c * FFN_CW:(c + 1) * FFN_CW, :], preferred_element_type=F32)
        if c == 0:
            acc_ref[...] = contrib
        else:
            acc_ref[...] += contrib
    xo_ref[...] = x + _rms(acc_ref[...], nw_ref[1:2, :])


def _ffn(x2d, prev, nw, wup, cwb, wdn, b, t):
    tl = _tiling(b, t, 512)
    G, J, R, P = tl["G"], tl["J"], tl["R"], tl["P"]
    n = b * t
    return pl.pallas_call(
        _ffn_body,
        grid=(G, J),
        in_specs=[
            pl.BlockSpec((R, D_MODEL), lambda g, j: (g * J + j, 0)),
            pl.BlockSpec((P, 2 * D_FF), lambda g, j: (g, 0)),
            _const_spec((8, D_MODEL)),
            _const_spec((D_MODEL, 2 * D_FF)),
            _const_spec((8, 2 * D_FF)),
            _const_spec((D_FF, D_MODEL)),
        ],
        out_specs=[
            pl.BlockSpec((R, D_MODEL), lambda g, j: (g * J + j, 0)),
            pl.BlockSpec((P, 2 * D_FF), lambda g, j: (g, 0)),
        ],
        out_shape=[jax.ShapeDtypeStruct((n, D_MODEL), F32),
                   jax.ShapeDtypeStruct((b * SUBLANES, 2 * D_FF), F32)],
        scratch_shapes=[pltpu.VMEM((P, 2 * D_FF), F32), pltpu.VMEM((R, D_MODEL), F32)],
        compiler_params=_params(2),
        name="conv_ffn",
    )(x2d, prev, nw, wup, cwb, wdn)


def _outproj_body(*refs, gated):
    x_ref, y_ref = refs[0], refs[1]
    wo_ref, nw_ref, xo_ref = refs[-3:]
    y = y_ref[...]
    if gated:
        y = y * refs[2][...]
    o = jnp.dot(y.astype(BF16), wo_ref[...], preferred_element_type=F32)
    xo_ref[...] = x_ref[...] + _rms(o, nw_ref[0:1, :])


def _outproj(x2d, y2d, gate2d, wo, nw, name):
    n, k = y2d.shape
    R = min(n, 512)
    row = lambda i: (i, 0)
    acts = [y2d] if gate2d is None else [y2d, gate2d]
    return pl.pallas_call(
        functools.partial(_outproj_body, gated=gate2d is not None),
        grid=(n // R,),
        in_specs=[pl.BlockSpec((R, D_MODEL), row)] + [pl.BlockSpec((R, k), row)] * len(acts)
        + [_const_spec((k, D_MODEL)), _const_spec((8, D_MODEL))],
        out_specs=pl.BlockSpec((R, D_MODEL), row),
        out_shape=jax.ShapeDtypeStruct((n, D_MODEL), F32),
        compiler_params=_params(1),
        name=name,
    )(x2d, *acts, wo, nw)


def _rwkv_proj_body(*refs, has_vres, chunk):
    it = iter(refs)
    x_ref, prev_ref = next(it), next(it)
    vf_ref = next(it) if has_vres else None
    vec_ref, wrkv_ref, w1_ref, w2_ref, a1_ref, a2_ref = (next(it) for _ in range(6))
    v1_ref, v2_ref = (next(it), next(it)) if has_vres else (None, None)
    g1_ref, g2_ref, tri_ref, ones_ref = (next(it) for _ in range(4))
    rt_ref, kt_ref, at_ref, bt_ref, v_ref, g_ref, gl_ref, hl_ref, carry_ref = (next(it) for _ in range(9))

    @pl.when(pl.program_id(1) == 0)
    def _():
        carry_ref[...] = prev_ref[...]

    x = x_ref[...]
    rows = x.shape[0]
    p = carry_ref.shape[0]
    h = _rms(x, vec_ref[10:11, :])
    d = _shift_rows(h, carry_ref[...], 1) - h
    tail = h[rows - p:, :]
    carry_ref[...] = tail
    hl_ref[...] = tail

    def mix(i):
        return (h + d * vec_ref[i:i + 1, :]).astype(BF16)

    r = jnp.dot(mix(0), wrkv_ref[0], preferred_element_type=F32)
    k = jnp.dot(mix(1), wrkv_ref[1], preferred_element_type=F32)
    xv = mix(2)
    v = jnp.dot(xv, wrkv_ref[2], preferred_element_type=F32)
    w_raw = vec_ref[6:7, :] + _bdot(jnp.tanh(_bdot(mix(3), w1_ref[...])), w2_ref[...])
    w = -_softplus(-w_raw) - 0.5
    if has_vres:
        gate_v = _sigmoid(vec_ref[11:12, :] + _bdot(_bdot(xv, v1_ref[...]), v2_ref[...]))
        v = v + (vf_ref[...] - v) * gate_v
    a = _sigmoid(vec_ref[7:8, :] + _bdot(_bdot(mix(4), a1_ref[...]), a2_ref[...]))
    g_ref[...] = _bdot(_sigmoid(_bdot(mix(5), g1_ref[...])), g2_ref[...])
    v_ref[...] = v

    kk = k * vec_ref[8:9, :]
    kk = kk * lax.rsqrt(_lane_group_sum(kk * kk, ones_ref[...]) + 1e-6)
    k = k * (1.0 + (a - 1.0) * vec_ref[9:10, :])

    lw = -jnp.exp(w)
    cum = _chunk_cumsum(lw, tri_ref[...])
    e_fwd = jnp.exp(cum)
    e_bwd = jnp.exp(-cum)
    rt_ref[...] = r * e_fwd
    kt_ref[...] = k * e_bwd
    at_ref[...] = -kk * jnp.exp(cum - lw)
    bt_ref[...] = kk * a * e_bwd
    for c in range(rows // chunk):
        gl_ref[c] = jnp.exp(cum[(c + 1) * chunk - 1:(c + 1) * chunk, :])


def _rwkv_scan_body(rt_ref, kt_ref, at_ref, bt_ref, v_ref, gl_ref, h0_ref, vec_ref, ones_ref,
                    y_ref, ho_ref, *, nh, chunk):
    @pl.when(pl.program_id(1) == 0)
    def _():
        ho_ref[...] = h0_ref[...]

    gl_lanes = nh * RW_N
    ng = RW_H // nh
    gc = nh * chunk
    row_head = lax.broadcasted_iota(jnp.int32, (gc, gl_lanes), 0) // chunk
    lane_head = lax.broadcasted_iota(jnp.int32, (gc, gl_lanes), 1) // RW_N
    own = row_head == lane_head
    ri = lax.broadcasted_iota(jnp.int32, (gc, gc), 0)
    ci = lax.broadcasted_iota(jnp.int32, (gc, gc), 1)
    same = (ri // chunk) == (ci // chunk)
    strict = same & ((ri % chunk) > (ci % chunk))
    incl = same & ((ri % chunk) >= (ci % chunk))
    eye = (ri == ci).astype(F32)
    eye_l = (lax.broadcasted_iota(jnp.int32, (gl_lanes, gl_lanes), 0)
             == lax.broadcasted_iota(jnp.int32, (gl_lanes, gl_lanes), 1))
    ones_bd = ones_ref[...]

    def blockdiag(xg):
        xx = jnp.concatenate([xg] * nh, axis=0) if nh > 1 else xg
        return jnp.where(own, xx, 0.0)

    groups = range(ng)
    sls = [slice(q * gl_lanes, (q + 1) * gl_lanes) for q in groups]
    merged = gc == GROUP_ROWS
    gl_rows = [gl_ref[0, :, sl] for sl in sls]
    r_bd = [blockdiag(rt_ref[:, sl]) for sl in sls]
    k_bd = [blockdiag(kt_ref[:, sl]) for sl in sls]
    a_bd = [blockdiag(at_ref[:, sl]) for sl in sls]
    b_bd = [blockdiag(bt_ref[:, sl]) for sl in sls]
    v_bd = [blockdiag(v_ref[:, sl]).astype(BF16) for sl in sls]
    hs = [ho_ref[0, q] for q in groups]
    hs_b = [h.astype(BF16) for h in hs]
    if merged:
        ar = [jnp.concatenate([a_bd[q], r_bd[q]], axis=0).astype(BF16) for q in groups]
        bk = [jnp.concatenate([b_bd[q], k_bd[q]], axis=0).astype(BF16) for q in groups]
        amat = [_bdot_nt(ar[q], bk[q]) for q in groups]
        a_ab = [jnp.where(strict, m[:gc, :gc], 0.0) for m in amat]
        a_ak = [jnp.where(strict, m[:gc, gc:], 0.0).astype(BF16) for m in amat]
        a_rbk = [jnp.concatenate([jnp.where(incl, m[gc:, :gc], 0.0), jnp.where(incl, m[gc:, gc:], 0.0)],
                                 axis=1).astype(BF16) for m in amat]
        arh = [_bdot(ar[q], hs_b[q]) for q in groups]
        ah = [m[:gc] for m in arh]
        rh = [m[gc:] for m in arh]
    else:
        ab_, rb_ = [x.astype(BF16) for x in a_bd], [x.astype(BF16) for x in r_bd]
        bb_, kb_ = [x.astype(BF16) for x in b_bd], [x.astype(BF16) for x in k_bd]
        a_ab = [jnp.where(strict, _bdot_nt(ab_[q], bb_[q]), 0.0) for q in groups]
        a_ak = [jnp.where(strict, _bdot_nt(ab_[q], kb_[q]), 0.0).astype(BF16) for q in groups]
        a_rb = [jnp.where(incl, _bdot_nt(rb_[q], bb_[q]), 0.0).astype(BF16) for q in groups]
        a_rk = [jnp.where(incl, _bdot_nt(rb_[q], kb_[q]), 0.0).astype(BF16) for q in groups]
        ah = [_bdot(ab_[q], hs_b[q]) for q in groups]
        rh = [_bdot(rb_[q], hs_b[q]) for q in groups]
    akv = [_bdot(a_ak[q], v_bd[q]) for q in groups]

    p = [m.astype(BF16) for m in a_ab]
    x = [eye + m for m in a_ab]
    span = 2
    if span < chunk:
        p = [_bdot(p[q], p[q]) for q in groups]
    while span < chunk:
        last = span * 2 >= chunk
        if merged and not last:
            px = [_bdot(p[q], jnp.concatenate([p[q].astype(BF16), x[q].astype(BF16)], axis=1)) for q in groups]
            p = [m[:, :gc] for m in px]
            x = [x[q] + px[q][:, gc:] for q in groups]
        else:
            pb = [m.astype(BF16) for m in p]
            x = [x[q] + _bdot(pb[q], x[q]) for q in groups]
            if not last:
                p = [_bdot(pb[q], pb[q]) for q in groups]
        span *= 2

    u = [_bdot(x[q], ah[q] + akv[q]).astype(BF16) for q in groups]
    if merged:
        uv = [jnp.concatenate([u[q], v_bd[q]], axis=0) for q in groups]
        y_bd = [rh[q] + _bdot(a_rbk[q], uv[q]) for q in groups]
        for q in groups:
            gl_col = jnp.sum(jnp.where(eye_l, gl_rows[q], 0.0), axis=1, keepdims=True)
            bk_g = jnp.concatenate([b_bd[q] * gl_rows[q], k_bd[q] * gl_rows[q]], axis=0)
            ho_ref[0, q] = hs[q] * gl_col + _bdot_tn(bk_g, uv[q])
    else:
        y_bd = [rh[q] + _bdot(a_rb[q], u[q]) + _bdot(a_rk[q], v_bd[q]) for q in groups]
        for q in groups:
            gl_col = jnp.sum(jnp.where(eye_l, gl_rows[q], 0.0), axis=1, keepdims=True)
            ho_ref[0, q] = (hs[q] * gl_col + _bdot_tn(b_bd[q] * gl_rows[q], u[q])
                            + _bdot_tn(k_bd[q] * gl_rows[q], v_bd[q]))

    for q in groups:
        sl = sls[q]
        y = y_bd[q][0:chunk]
        for hh in range(1, nh):
            y = y + y_bd[q][hh * chunk:(hh + 1) * chunk]
        mu = _lane_group_sum(y, ones_bd) * (1.0 / RW_N)
        yc = y - mu
        var = _lane_group_sum(yc * yc, ones_bd) * (1.0 / RW_N)
        yn = yc * lax.rsqrt(var + RW_LNX_EPS) * vec_ref[0:1, sl] + vec_ref[1:2, sl]
        bonus = _lane_group_sum(rt_ref[:, sl] * kt_ref[:, sl] * vec_ref[2:3, sl], ones_bd) * v_ref[:, sl]
        y_ref[:, sl] = yn + bonus


def _rwkv_layer(x2d, shift_prev, s0, v_first, w, ri, nw, b, t):
    n = b * t
    chunk = _chunk_of(t)
    tl = _tiling(b, t, 256)
    G, J, R, P = tl["G"], tl["J"], tl["R"], tl["P"]
    has_vres = v_first is not None
    vi = ri - 1
    bc = chunk if chunk == 64 else R
    tri = _chunk_masks(chunk, bc)
    li = jnp.arange(LANES)
    ones_bd = ((li[:, None] // RW_N) == (li[None, :] // RW_N)).astype(BF16)
    ones_bd = jnp.concatenate([ones_bd, ones_bd], axis=0)
    zero = jnp.zeros((D_MODEL,), F32)
    vec = jnp.stack([*(w["rw_mu"][ri][i] for i in range(6)), w["rw_w0"][ri], w["rw_a0"][ri], w["rw_kk"][ri],
                     w["rw_ka"][ri], nw[0], w["rw_v0"][vi] if has_vres else zero, zero, zero, zero, zero])
    row = lambda g, j: (g * J + j, 0)
    row_spec = pl.BlockSpec((R, D_MODEL), row)
    ins = [x2d, _pad_state(shift_prev[:, None, :])]
    specs = [row_spec, pl.BlockSpec((P, D_MODEL), lambda g, j: (g, 0))]
    if has_vres:
        ins.append(v_first)
        specs.append(row_spec)
    wl = [vec, w["rw_wrkv"][ri].astype(BF16), w["rw_w1"][ri].astype(BF16), w["rw_w2"][ri].astype(BF16),
          w["rw_a1"][ri].astype(BF16), w["rw_a2"][ri].astype(BF16)]
    if has_vres:
        wl += [w["rw_v1"][vi].astype(BF16), w["rw_v2"][vi].astype(BF16)]
    wl += [w["rw_g1"][ri].astype(BF16), w["rw_g2"][ri].astype(BF16), tri, ones_bd]
    ins += wl
    specs += [_const_spec(a.shape) for a in wl]
    nc_tile = R // chunk
    outs = pl.pallas_call(
        functools.partial(_rwkv_proj_body, has_vres=has_vres, chunk=chunk),
        grid=(G, J),
        in_specs=specs,
        out_specs=[row_spec] * 6 + [pl.BlockSpec((nc_tile, 1, D_MODEL), lambda g, j: (g * J + j, 0, 0)),
                                    pl.BlockSpec((P, D_MODEL), lambda g, j: (g, 0))],
        out_shape=[jax.ShapeDtypeStruct((n, D_MODEL), F32)] * 6
        + [jax.ShapeDtypeStruct((n // chunk, 1, D_MODEL), F32), jax.ShapeDtypeStruct((b * SUBLANES, D_MODEL), F32)],
        scratch_shapes=[pltpu.VMEM((P, D_MODEL), F32)],
        compiler_params=_params(2),
        name="rwkv_proj",
    )(*ins)
    rt, kt, at, bt, v, g, gl, hl = outs
    shift_new = hl.reshape(b, SUBLANES, D_MODEL)[:, -1]

    nh = LANES // RW_N
    ng = RW_H // nh
    gl_lanes = nh * RW_N
    hkv = jnp.swapaxes(s0, -1, -2).reshape(b, ng, nh, RW_N, RW_N)
    h0 = jnp.einsum("bqikv,ij->bqikjv", hkv, jnp.eye(nh, dtype=F32)).reshape(b, ng, gl_lanes, gl_lanes)
    svec = jnp.stack([w["rw_lnx_w"][ri], w["rw_lnx_b"][ri], w["rw_rk"][ri].reshape(D_MODEL),
                      zero, zero, zero, zero, zero])
    nct = t // chunk
    crow = lambda bb, j: (bb * nct + j, 0)
    cspec = pl.BlockSpec((chunk, D_MODEL), crow)
    hspec = pl.BlockSpec((1, ng, gl_lanes, gl_lanes), lambda bb, j: (bb, 0, 0, 0))
    y, hout = pl.pallas_call(
        functools.partial(_rwkv_scan_body, nh=nh, chunk=chunk),
        grid=(b, nct),
        in_specs=[cspec] * 5 + [pl.BlockSpec((1, 1, D_MODEL), lambda bb, j: (bb * nct + j, 0, 0)), hspec,
                                _const_spec((8, D_MODEL)), _const_spec((2 * LANES, LANES))],
        out_specs=[cspec, hspec],
        out_shape=[jax.ShapeDtypeStruct((n, D_MODEL), F32),
                   jax.ShapeDtypeStruct((b, ng, gl_lanes, gl_lanes), F32)],
        compiler_params=_params(2),
        name="rwkv_scan",
    )(rt, kt, at, bt, v, gl, h0, svec, ones_bd)
    h6 = hout.reshape(b, ng, nh, RW_N, nh, RW_N)
    s_new = jnp.stack([h6[:, :, i, :, i, :] for i in range(nh)], axis=2)
    s_new = jnp.swapaxes(s_new, -1, -2).reshape(b, RW_H, RW_N, RW_N)
    nwp = jnp.concatenate([nw[1:2], jnp.zeros((7, D_MODEL), F32)])
    x_new = _outproj(x2d, y, g, w["rw_wo"][ri].astype(BF16), nwp, "rwkv_out")
    return x_new, shift_new, s_new, (v if not has_vres else v_first)


def _rope_lanes(x, tab_ref):
    half = MLA_ROPE // 2
    return (x * tab_ref[0] + pltpu.roll(x, LANES - half, 1) * tab_ref[1] + pltpu.roll(x, half, 1) * tab_ref[2])


def _mla_proj_body(x_ref, nw_ref, tab_ref, winq_ref, winc_ref, wink_ref, qn_ref, kvn_ref, wqn_ref, wqr_ref,
                   wuk_ref, c_ref, kr_ref, kcat_ref, qcat_ref):
    h = _rms(x_ref[...], nw_ref[0:1, :]).astype(BF16)
    cq = _rms(jnp.dot(h, winq_ref[...], preferred_element_type=F32), qn_ref[...]).astype(BF16)
    c = _rms(jnp.dot(h, winc_ref[...], preferred_element_type=F32), kvn_ref[...])
    kr = _rope_lanes(jnp.dot(h, wink_ref[...], preferred_element_type=F32), tab_ref)
    c_ref[...] = c
    kr_ref[...] = kr
    adt = kcat_ref.dtype
    kcat_ref[:, 0:MLA_KV_LORA] = c.astype(adt)
    kcat_ref[:, MLA_KV_LORA:MLA_QK] = kr.astype(adt)
    qn = jnp.dot(cq, wqn_ref[...], preferred_element_type=F32).astype(BF16)
    qr = jnp.dot(cq, wqr_ref[...], preferred_element_type=F32)
    for pr in range(MLA_H // 2):
        ql = jnp.dot(qn[:, pr * LANES:(pr + 1) * LANES], wuk_ref[pr], preferred_element_type=F32) * MLA_SCALE
        qcat_ref[2 * pr, :, 0:MLA_KV_LORA] = ql[:, :MLA_KV_LORA].astype(adt)
        qcat_ref[2 * pr + 1, :, 0:MLA_KV_LORA] = ql[:, MLA_KV_LORA:].astype(adt)
    for hh in range(MLA_H):
        qro = _rope_lanes(qr[:, hh * LANES:(hh + 1) * LANES], tab_ref) * MLA_SCALE
        qcat_ref[hh, :, MLA_KV_LORA:MLA_QK] = qro.astype(adt)


MLA_TQ = 64
MLA_TK = 256


def _mla_prompt_body(q_ref, k_ref, o_ref, m_ref, l_ref, acc_ref):
    i = pl.program_id(1)
    rows = MLA_H * MLA_TQ
    q = q_ref[...].reshape(rows, MLA_QK)
    m_ref[...] = jnp.full((rows, LANES), -jnp.inf, F32)
    l_ref[...] = jnp.zeros((rows, LANES), F32)
    acc_ref[...] = jnp.zeros((rows, MLA_KV_LORA), F32)
    ones = jnp.ones((MLA_TK, LANES), BF16)
    reps = MLA_TK // LANES

    def block(k0, masked):
        kblk = k_ref[pl.ds(k0, MLA_TK), :]
        s = lax.dot_general(q, kblk, (((1,), (1,)), ((), ())), preferred_element_type=F32)
        if masked:
            qpos = i * MLA_TQ + lax.broadcasted_iota(jnp.int32, (rows, MLA_TK), 0) % MLA_TQ
            kpos = k0 + lax.broadcasted_iota(jnp.int32, (rows, MLA_TK), 1)
            s = jnp.where(kpos <= qpos, s, -jnp.inf)
        m_old = m_ref[...]
        m_new = jnp.maximum(m_old, jnp.max(s, axis=1, keepdims=True))
        alpha = jnp.exp(m_old - m_new)
        p = jnp.exp(s - jnp.concatenate([m_new] * reps, axis=1)).astype(BF16)
        l_ref[...] = l_ref[...] * alpha + jnp.dot(p, ones, preferred_element_type=F32)
        acc_ref[...] = (acc_ref[...] * jnp.concatenate([alpha] * (MLA_KV_LORA // LANES), axis=1)
                        + jnp.dot(p, kblk[:, :MLA_KV_LORA], preferred_element_type=F32))
        m_ref[...] = m_new

    def full_step(kb, carry):
        block(pl.multiple_of(kb * MLA_TK, MLA_TK), False)
        return carry

    n_full = (i * MLA_TQ) // MLA_TK
    lax.fori_loop(0, n_full, full_step, 0)
    block(pl.multiple_of(n_full * MLA_TK, MLA_TK), True)
    o = acc_ref[...] / jnp.concatenate([l_ref[...]] * (MLA_KV_LORA // LANES), axis=1)
    o_ref[...] = o.reshape(MLA_H, MLA_TQ, MLA_KV_LORA).astype(BF16)


MLA_PP = 16


def _mla_sample_body(pt_ref, q_ref, kn_ref, *rest):
    lat_refs = rest[:MLA_PP]
    kro_refs = rest[MLA_PP:2 * MLA_PP]
    o_ref, m_ref, l_ref, acc_ref = rest[2 * MLA_PP:]
    j = pl.program_id(1)
    t = q_ref.shape[1]
    rows = MLA_H * t
    q = q_ref[...].reshape(rows, MLA_QK).astype(BF16)
    ql = q[:, :MLA_KV_LORA]
    qr = q[:, MLA_KV_LORA:MLA_KV_LORA + MLA_ROPE]

    @pl.when(j == 0)
    def _():
        m_ref[...] = jnp.full((rows, LANES), -jnp.inf, F32)
        l_ref[...] = jnp.zeros((rows, LANES), F32)
        acc_ref[...] = jnp.zeros((rows, MLA_KV_LORA), F32)

    vrep = MLA_KV_LORA // LANES

    def update(s, vals, row_sum):
        m_old = m_ref[...]
        m_new = jnp.maximum(m_old, jnp.max(s, axis=1, keepdims=True))
        alpha = jnp.exp(m_old - m_new)
        if s.shape[1] % LANES == 0:
            p = jnp.exp(s - jnp.concatenate([m_new] * (s.shape[1] // LANES), axis=1)).astype(BF16)
        else:
            p = jnp.exp(s - m_new[:, 0:1]).astype(BF16)
        l_ref[...] = l_ref[...] * alpha + row_sum(p)
        acc_ref[...] = (acc_ref[...] * jnp.concatenate([alpha] * vrep, axis=1)
                        + jnp.dot(p, vals, preferred_element_type=F32))
        m_ref[...] = m_new

    cbs, s_parts = [], []
    for pp in range(MLA_PP):
        cb = lat_refs[pp][0].astype(BF16)
        kb = kro_refs[pp][0].astype(BF16)
        s_parts.append(lax.dot_general(ql, cb, (((1,), (1,)), ((), ())), preferred_element_type=F32)
                       + lax.dot_general(qr, kb, (((1,), (1,)), ((), ())), preferred_element_type=F32))
        cbs.append(cb)
    s_all = jnp.concatenate(s_parts, axis=1)
    ones = jnp.ones((s_all.shape[1], LANES), BF16)
    update(s_all, jnp.concatenate(cbs, axis=0), lambda p: jnp.dot(p, ones, preferred_element_type=F32))

    @pl.when(j == pl.num_programs(1) - 1)
    def _():
        kn = kn_ref[...].astype(BF16)
        s = lax.dot_general(q, kn, (((1,), (1,)), ((), ())), preferred_element_type=F32)
        qpos = lax.broadcasted_iota(jnp.int32, (rows, t), 0) % t
        kpos = lax.broadcasted_iota(jnp.int32, (rows, t), 1)
        s = jnp.where(kpos <= qpos, s, -jnp.inf)
        update(s, kn[:, :MLA_KV_LORA], lambda p: jnp.sum(p.astype(F32), axis=1, keepdims=True))
        o = acc_ref[...] / jnp.concatenate([l_ref[...]] * vrep, axis=1)
        o_ref[...] = o.reshape(MLA_H, t, MLA_KV_LORA).astype(o_ref.dtype)


def _mla_out_body(x_ref, o_ref, wuv_ref, wo_ref, nw_ref, xo_ref):
    parts = []
    for pr in range(MLA_H // 2):
        wp = wuv_ref[pr]
        parts.append(jnp.dot(o_ref[2 * pr].astype(BF16), wp[:MLA_KV_LORA], preferred_element_type=F32)
                     + jnp.dot(o_ref[2 * pr + 1].astype(BF16), wp[MLA_KV_LORA:], preferred_element_type=F32))
    v = jnp.concatenate(parts, axis=1).astype(BF16)
    o = jnp.dot(v, wo_ref[...], preferred_element_type=F32)
    xo_ref[...] = x_ref[...] + _rms(o, nw_ref[0:1, :])


def _mla_layer(x2d, pos, w, mi, nw, b, t, paged):
    n = b * t
    tl = _tiling(b, t, 512)
    G, J, R = tl["G"], tl["J"], tl["R"]
    adt = BF16 if t % 16 == 0 else F32
    half = MLA_ROPE // 2
    inv = ROPE_THETA ** (-jnp.arange(half, dtype=F32) / half)
    ang = pos.astype(F32)[:, None] * inv[None, :]
    cos, sin = jnp.cos(ang), jnp.sin(ang)
    zpad = jnp.zeros((t, LANES - MLA_ROPE), F32)
    zh = jnp.zeros((t, half), F32)
    tab = jnp.stack([jnp.concatenate([cos, cos, zpad], 1), jnp.concatenate([-sin, zh, zpad], 1),
                     jnp.concatenate([zh, sin, zpad], 1)])
    if G == 1:
        tab = jnp.tile(tab, (1, b, 1))
    w_in = w["mla_w_in"][mi]
    winq = w_in[:, :MLA_Q_LORA].astype(BF16)
    winc = w_in[:, MLA_Q_LORA:MLA_Q_LORA + MLA_KV_LORA].astype(BF16)
    wink = jnp.pad(w_in[:, MLA_Q_LORA + MLA_KV_LORA:], ((0, 0), (0, LANES - MLA_ROPE))).astype(BF16)
    wqb = w["mla_w_qb"][mi].reshape(MLA_Q_LORA, MLA_H, MLA_NOPE + MLA_ROPE)
    wqn = wqb[:, :, :MLA_NOPE].reshape(MLA_Q_LORA, MLA_H * MLA_NOPE).astype(BF16)
    wqr = jnp.pad(wqb[:, :, MLA_NOPE:], ((0, 0), (0, 0), (0, LANES - MLA_ROPE))
                  ).reshape(MLA_Q_LORA, MLA_H * LANES).astype(BF16)
    wuk = jnp.transpose(w["mla_w_uk"][mi], (1, 2, 0)).reshape(MLA_H // 2, 2, MLA_NOPE, MLA_KV_LORA)
    wuk_bd = jnp.einsum("pinc,ij->pinjc", wuk, jnp.eye(2, dtype=F32)).reshape(
        MLA_H // 2, 2 * MLA_NOPE, 2 * MLA_KV_LORA).astype(BF16)
    wuv = jnp.transpose(w["mla_w_uv"][mi], (1, 0, 2)).reshape(MLA_H // 2, 2, MLA_KV_LORA, MLA_V)
    wuv_bd = jnp.einsum("picv,ij->picjv", wuv, jnp.eye(2, dtype=F32)).reshape(
        MLA_H // 2, 2 * MLA_KV_LORA, 2 * MLA_V).astype(BF16)
    nwa = jnp.concatenate([nw[0:1], jnp.zeros((7, D_MODEL), F32)])
    nwb = jnp.concatenate([nw[1:2], jnp.zeros((7, D_MODEL), F32)])
    row = lambda g, j: (g * J + j, 0)
    wl = [winq, winc, wink, w["mla_q_norm"][mi][None, :], w["mla_kv_norm"][mi][None, :], wqn, wqr, wuk_bd]
    c, kr, kcat, qcat = pl.pallas_call(
        _mla_proj_body,
        grid=(G, J),
        in_specs=[pl.BlockSpec((R, D_MODEL), row), _const_spec((8, D_MODEL)),
                  pl.BlockSpec((3, R, LANES), lambda g, j: (0, j, 0))] + [_const_spec(a.shape) for a in wl],
        out_specs=[pl.BlockSpec((R, MLA_KV_LORA), row), pl.BlockSpec((R, LANES), row),
                   pl.BlockSpec((R, MLA_QK), row), pl.BlockSpec((MLA_H, R, MLA_QK), lambda g, j: (0, g * J + j, 0))],
        out_shape=[jax.ShapeDtypeStruct((n, MLA_KV_LORA), F32), jax.ShapeDtypeStruct((n, LANES), F32),
                   jax.ShapeDtypeStruct((n, MLA_QK), adt), jax.ShapeDtypeStruct((MLA_H, n, MLA_QK), adt)],
        compiler_params=_params(2),
        name="mla_proj",
    )(x2d, nwa, tab, *wl)

    if paged is None:
        nq = t // MLA_TQ
        rows = MLA_H * MLA_TQ
        o = pl.pallas_call(
            _mla_prompt_body,
            grid=(b, nq),
            in_specs=[pl.BlockSpec((MLA_H, MLA_TQ, MLA_QK), lambda bb, i: (0, bb * nq + i, 0)),
                      pl.BlockSpec((t, MLA_QK), lambda bb, i: (bb, 0))],
            out_specs=pl.BlockSpec((MLA_H, MLA_TQ, MLA_KV_LORA), lambda bb, i: (0, bb * nq + i, 0)),
            out_shape=jax.ShapeDtypeStruct((MLA_H, n, MLA_KV_LORA), BF16),
            scratch_shapes=[pltpu.VMEM((rows, LANES), F32), pltpu.VMEM((rows, LANES), F32),
                            pltpu.VMEM((rows, MLA_KV_LORA), F32)],
            compiler_params=_params(2),
            name="mla_attend_prompt",
        )(qcat, kcat)
    else:
        pages_c, pages_kr, page_table = paged
        page = pages_c.shape[1]
        npg = page_table.shape[1]
        assert npg % MLA_PP == 0
        rows = MLA_H * t

        def page_map(pp):
            return lambda bb, j, pt: (pt[bb, j * MLA_PP + pp], 0, 0)

        grid_spec = pltpu.PrefetchScalarGridSpec(
            num_scalar_prefetch=1,
            grid=(b, npg // MLA_PP),
            in_specs=[pl.BlockSpec((MLA_H, t, MLA_QK), lambda bb, j, pt: (0, bb, 0)),
                      pl.BlockSpec((t, MLA_QK), lambda bb, j, pt: (bb, 0))]
            + [pl.BlockSpec((1, page, MLA_KV_LORA), page_map(pp)) for pp in range(MLA_PP)]
            + [pl.BlockSpec((1, page, MLA_ROPE), page_map(pp)) for pp in range(MLA_PP)],
            out_specs=pl.BlockSpec((MLA_H, t, MLA_KV_LORA), lambda bb, j, pt: (0, bb, 0)),
            scratch_shapes=[pltpu.VMEM((rows, LANES), F32), pltpu.VMEM((rows, LANES), F32),
                            pltpu.VMEM((rows, MLA_KV_LORA), F32)],
        )
        o = pl.pallas_call(
            _mla_sample_body,
            grid_spec=grid_spec,
            out_shape=jax.ShapeDtypeStruct((MLA_H, n, MLA_KV_LORA), adt),
            compiler_params=_params(2),
            name="mla_attend_sample",
        )(page_table, qcat, kcat, *([pages_c] * MLA_PP), *([pages_kr] * MLA_PP))

    Ro = min(n, 512)
    x_new = pl.pallas_call(
        _mla_out_body,
        grid=(n // Ro,),
        in_specs=[pl.BlockSpec((Ro, D_MODEL), lambda i: (i, 0)),
                  pl.BlockSpec((MLA_H, Ro, MLA_KV_LORA), lambda i: (0, i, 0)),
                  _const_spec(wuv_bd.shape), _const_spec((MLA_H * MLA_V, D_MODEL)), _const_spec((8, D_MODEL))],
        out_specs=pl.BlockSpec((Ro, D_MODEL), lambda i: (i, 0)),
        out_shape=jax.ShapeDtypeStruct((n, D_MODEL), F32),
        compiler_params=_params(1),
        name="mla_out",
    )(x2d, o, wuv_bd, w["mla_wo"][mi].astype(BF16), nwb)
    return x_new, c.reshape(b, t, MLA_KV_LORA), kr[:, :MLA_ROPE].reshape(b, t, MLA_ROPE)


GDN_CW = 512


def _gdn_proj_body(x_ref, prev_ref, nw_ref, wqkv_ref, wz_ref, wbg_ref, cw_ref, gvec_ref, tri_ref,
                   q_ref, k_ref, v_ref, z_ref, beta_ref, gc_ref, st_ref, carry_ref):
    @pl.when(pl.program_id(1) == 0)
    def _():
        carry_ref[...] = prev_ref[...]

    rows = x_ref.shape[0]
    p = carry_ref.shape[0]
    h = _rms(x_ref[...], nw_ref[0:1, :]).astype(BF16)
    z_ref[...] = jnp.dot(h, wz_ref[...], preferred_element_type=F32)
    bg = jnp.dot(h, wbg_ref[...], preferred_element_type=F32)
    beta_ref[...] = _sigmoid(bg)
    g = -jnp.exp(gvec_ref[0:1, :]) * _softplus(bg + gvec_ref[1:2, :])
    gc_ref[...] = _chunk_cumsum(g, tri_ref[...])
    for c in range(GDN_CONV_DIM // GDN_CW):
        sl = slice(c * GDN_CW, (c + 1) * GDN_CW)
        u = jnp.dot(h, wqkv_ref[:, sl], preferred_element_type=F32)
        prev = carry_ref[:, sl]
        y = cw_ref[3:4, sl] * u
        for s in range(1, GDN_CONV):
            y = y + cw_ref[3 - s:4 - s, sl] * _shift_rows(u, prev, s)
        tail = u[rows - p:, :]
        carry_ref[:, sl] = tail
        st_ref[:, sl] = tail
        y = _silu(y)
        off = c * GDN_CW
        if off < 2 * GDN_QK_DIM:
            dst, base, scale = (q_ref, off, GDN_DK ** -0.5) if off < GDN_QK_DIM else (k_ref, off - GDN_QK_DIM, 1.0)
            for hh in range(GDN_CW // GDN_DK):
                yh = y[:, hh * GDN_DK:(hh + 1) * GDN_DK]
                yh = yh * lax.rsqrt(jnp.sum(yh * yh, axis=-1, keepdims=True) + 1e-6)
                dst[:, base + hh * GDN_DK:base + (hh + 1) * GDN_DK] = yh * scale if scale != 1.0 else yh
        else:
            v_ref[:, off - 2 * GDN_QK_DIM:off - 2 * GDN_QK_DIM + GDN_CW] = y


def _gdn_chunk_body(q_ref, k_ref, v_ref, z_ref, gcol_ref, grow_ref, bcol_ref, s0_ref, nw_ref,
                    o_ref, so_ref, *, nh, chunk):
    @pl.when(pl.program_id(1) == 0)
    def _():
        so_ref[...] = s0_ref[...]

    ng = GDN_V_H // nh
    gc = nh * chunk
    rep = GDN_V_H // GDN_QK_H
    ri = lax.broadcasted_iota(jnp.int32, (gc, gc), 0)
    ci = lax.broadcasted_iota(jnp.int32, (gc, gc), 1)
    same = (ri // chunk) == (ci // chunk)
    strict = same & ((ri % chunk) > (ci % chunk))
    incl = same & ((ri % chunk) >= (ci % chunk))
    last = same & ((ci % chunk) == chunk - 1)
    eye = (ri == ci).astype(F32)
    row_head = lax.broadcasted_iota(jnp.int32, (gc, GDN_DK), 0) // chunk

    def stack(ref, heads, width):
        parts = [ref[:, hd * width:(hd + 1) * width] for hd in heads]
        return parts[0] if len(parts) == 1 else jnp.concatenate(parts, axis=0)

    groups = range(ng)
    heads = [[q * nh + i for i in range(nh)] for q in groups]
    k_st = [stack(k_ref, [hd // rep for hd in heads[q]], GDN_DK) for q in groups]
    q_st = [stack(q_ref, [hd // rep for hd in heads[q]], GDN_DK) for q in groups]
    v_st = [stack(v_ref, heads[q], GDN_DV) for q in groups]
    gcol = [gcol_ref[0, :, q:q + 1] for q in groups]
    grow = [grow_ref[0, q:q + 1, :] for q in groups]
    bcol = [bcol_ref[0, :, q:q + 1] for q in groups]
    k_b = [x.astype(BF16) for x in k_st]
    kq = [_bdot_nt(jnp.concatenate([k_b[q], q_st[q].astype(BF16)], axis=0), k_b[q]) for q in groups]
    decay = [jnp.exp(jnp.where(incl, gcol[q] - grow[q], -jnp.inf)) for q in groups]
    a = [jnp.where(strict, kq[q][:gc] * bcol[q] * decay[q], 0.0) for q in groups]
    aqk = [jnp.where(incl, kq[q][gc:] * decay[q], 0.0).astype(BF16) for q in groups]

    p = [(-m).astype(BF16) for m in a]
    x = [eye - m for m in a]
    span = 2
    if span < chunk:
        p = [_bdot(p[q], p[q]) for q in groups]
    while span < chunk:
        if span * 2 < chunk:
            px = [_bdot(p[q], jnp.concatenate([p[q].astype(BF16), x[q].astype(BF16)], axis=1)) for q in groups]
            p = [m[:, :gc] for m in px]
            x = [x[q] + px[q][:, gc:] for q in groups]
        else:
            x = [x[q] + _bdot(p[q], x[q]) for q in groups]
        span *= 2

    egc = [jnp.exp(g) for g in gcol]
    uw = [_bdot(x[q], jnp.concatenate([v_st[q] * bcol[q], k_st[q] * (bcol[q] * egc[q])], axis=1)) for q in groups]
    glast = [jnp.sum(jnp.where(last, grow[q], 0.0), axis=1, keepdims=True) for q in groups]
    kg = [k_st[q] * jnp.exp(glast[q] - gcol[q]) for q in groups]
    states = [[so_ref[0, hd] for hd in heads[q]] for q in groups]
    wq_s = []
    for q in groups:
        wm = uw[q][:, GDN_DV:]
        qg = q_st[q] * egc[q]
        parts = []
        for i in range(nh):
            rs = slice(i * chunk, (i + 1) * chunk)
            parts.append(_bdot(jnp.concatenate([wm[rs], qg[rs]], axis=0), states[q][i]))
        wq_s.append(parts)
    v_new, o_st = [], []
    for q in groups:
        ws = jnp.concatenate([m[:chunk] for m in wq_s[q]], axis=0) if nh > 1 else wq_s[q][0][:chunk]
        qs = jnp.concatenate([m[chunk:] for m in wq_s[q]], axis=0) if nh > 1 else wq_s[q][0][chunk:]
        vn = (uw[q][:, :GDN_DV] - ws).astype(BF16)
        v_new.append(vn)
        o_st.append(qs + _bdot(aqk[q], vn))
    for q in groups:
        for i, hd in enumerate(heads[q]):
            rs = slice(i * chunk, (i + 1) * chunk)
            gl_h = jnp.exp(glast[q][i * chunk:i * chunk + 1, :])
            if chunk % 16 == 0:
                upd = _bdot_tn(kg[q][rs], v_new[q][rs])
            else:
                upd = _bdot_tn(jnp.where(row_head == i, kg[q], 0.0), v_new[q])
            so_ref[0, hd] = states[q][i] * gl_h + upd
    for q in groups:
        z_st = stack(z_ref, heads[q], GDN_DV)
        og = _rms(o_st[q], nw_ref[0:1, :]) * _silu(z_st)
        for i, hd in enumerate(heads[q]):
            o_ref[:, hd * GDN_DV:(hd + 1) * GDN_DV] = og[i * chunk:(i + 1) * chunk]


def _gdn_layer(x2d, conv_prev, s0, w, gi, nw, b, t):
    n = b * t
    chunk = _chunk_of(t)
    tl = _tiling(b, t, 256)
    G, J, R, P = tl["G"], tl["J"], tl["R"], tl["P"]
    w_in = w["gdn_w_in"][gi]
    o1 = GDN_CONV_DIM
    o2 = o1 + GDN_V_DIM
    wqkv = w_in[:, :o1].astype(BF16)
    wz = w_in[:, o1:o2].astype(BF16)
    wbg = jnp.pad(w_in[:, o2:], ((0, 0), (0, LANES - 2 * GDN_V_H))).astype(BF16)
    cw = jnp.pad(w["gdn_conv_w"][gi], ((0, 8 - GDN_CONV), (0, 0)))
    gvec = jnp.zeros((8, LANES), F32)
    gvec = gvec.at[0, GDN_V_H:2 * GDN_V_H].set(w["gdn_a_log"][gi]).at[1, GDN_V_H:2 * GDN_V_H].set(w["gdn_dt_bias"][gi])
    bc = chunk if chunk == 64 else R
    tri = _chunk_masks(chunk, bc)
    nwa = jnp.concatenate([nw[0:1], jnp.zeros((7, D_MODEL), F32)])
    row = lambda g, j: (g * J + j, 0)
    st_spec = pl.BlockSpec((P, GDN_CONV_DIM), lambda g, j: (g, 0))
    qn, kn, v, z, beta, gcs, st = pl.pallas_call(
        _gdn_proj_body,
        grid=(G, J),
        in_specs=[pl.BlockSpec((R, D_MODEL), row), st_spec, _const_spec((8, D_MODEL)), _const_spec(wqkv.shape),
                  _const_spec(wz.shape), _const_spec(wbg.shape), _const_spec(cw.shape), _const_spec(gvec.shape),
                  _const_spec(tri.shape)],
        out_specs=[pl.BlockSpec((R, GDN_QK_DIM), row), pl.BlockSpec((R, GDN_QK_DIM), row),
                   pl.BlockSpec((R, GDN_V_DIM), row), pl.BlockSpec((R, GDN_V_DIM), row),
                   pl.BlockSpec((R, LANES), row), pl.BlockSpec((R, LANES), row), st_spec],
        out_shape=[jax.ShapeDtypeStruct((n, GDN_QK_DIM), F32), jax.ShapeDtypeStruct((n, GDN_QK_DIM), F32),
                   jax.ShapeDtypeStruct((n, GDN_V_DIM), F32), jax.ShapeDtypeStruct((n, GDN_V_DIM), F32),
                   jax.ShapeDtypeStruct((n, LANES), F32), jax.ShapeDtypeStruct((n, LANES), F32),
                   jax.ShapeDtypeStruct((b * SUBLANES, GDN_CONV_DIM), F32)],
        scratch_shapes=[pltpu.VMEM((P, GDN_CONV_DIM), F32)],
        compiler_params=_params(2),
        name="gdn_proj",
    )(x2d, _pad_state(conv_prev), nwa, wqkv, wz, wbg, cw, gvec, tri)
    conv_new = st.reshape(b, SUBLANES, GDN_CONV_DIM)[:, SUBLANES - (GDN_CONV - 1):]

    nh = GROUP_ROWS // chunk
    ng = GDN_V_H // nh
    nct = t // chunk
    ncs = n // chunk
    grow = jnp.swapaxes(gcs[:, GDN_V_H:2 * GDN_V_H].reshape(ncs, chunk, GDN_V_H), 1, 2).reshape(ncs, ng, nh * chunk)
    brow = jnp.swapaxes(beta[:, :GDN_V_H].reshape(ncs, chunk, GDN_V_H), 1, 2).reshape(ncs, ng, nh * chunk)
    gcol = jnp.swapaxes(grow, 1, 2)
    bcol = jnp.swapaxes(brow, 1, 2)
    crow = lambda bb, j: (bb * nct + j, 0)
    c3 = lambda bb, j: (bb * nct + j, 0, 0)
    sspec = pl.BlockSpec((1, GDN_V_H, GDN_DK, GDN_DV), lambda bb, j: (bb, 0, 0, 0))
    nwn = jnp.concatenate([w["gdn_norm_w"][gi][None, :], jnp.zeros((7, GDN_DV), F32)])
    o, s_new = pl.pallas_call(
        functools.partial(_gdn_chunk_body, nh=nh, chunk=chunk),
        grid=(b, nct),
        in_specs=[pl.BlockSpec((chunk, GDN_QK_DIM), crow), pl.BlockSpec((chunk, GDN_QK_DIM), crow),
                  pl.BlockSpec((chunk, GDN_V_DIM), crow), pl.BlockSpec((chunk, GDN_V_DIM), crow),
                  pl.BlockSpec((1, nh * chunk, ng), c3), pl.BlockSpec((1, ng, nh * chunk), c3),
                  pl.BlockSpec((1, nh * chunk, ng), c3), sspec, _const_spec((8, GDN_DV))],
        out_specs=[pl.BlockSpec((chunk, GDN_V_DIM), crow), sspec],
        out_shape=[jax.ShapeDtypeStruct((n, GDN_V_DIM), F32),
                   jax.ShapeDtypeStruct((b, GDN_V_H, GDN_DK, GDN_DV), F32)],
        compiler_params=_params(2),
        name="gdn_chunk",
    )(qn, kn, v, z, gcol, grow, bcol, s0, nwn)
    nwb = jnp.concatenate([nw[1:2], jnp.zeros((7, D_MODEL), F32)])
    x_new = _outproj(x2d, o, None, w["gdn_wo"][gi].astype(BF16), nwb, "gdn_out")
    return x_new, conv_new, s_new


def _trunk(x, pos, rw_s, rw_shift, gdn_s, gdn_conv, ffn_conv, w, paged):
    b, t, _ = x.shape
    x2d = x.reshape(b * t, D_MODEL)
    new = {k: [] for k in ("rw_S", "rw_shift", "mla_c", "mla_kr", "gdn_S", "gdn_conv", "ffn_conv")}
    v_first = None
    ri = mi = gi = 0
    for l, kind in enumerate(LAYER_MIXER):
        nw = w["norm_w"][l]
        if kind == 0:
            x2d, sh, s_new, v_first = _rwkv_layer(x2d, rw_shift[ri], rw_s[ri], v_first, w, ri, nw, b, t)
            new["rw_S"].append(s_new)
            new["rw_shift"].append(sh)
            ri += 1
        elif kind == 1:
            x2d, c, kr = _mla_layer(x2d, pos, w, mi, nw, b, t, None if paged is None else
                                    (paged[0][mi], paged[1][mi], paged[2]))
            new["mla_c"].append(c)
            new["mla_kr"].append(kr)
            mi += 1
        else:
            x2d, cb, s_new = _gdn_layer(x2d, gdn_conv[gi], gdn_s[gi], w, gi, nw, b, t)
            new["gdn_S"].append(s_new)
            new["gdn_conv"].append(cb)
            gi += 1
        nwf = jnp.concatenate([nw[2:4], jnp.zeros((6, D_MODEL), F32)])
        cwb = jnp.concatenate([w["ffn_conv_w"][l], w["ffn_conv_b"][l][None, :],
                               jnp.zeros((8 - FFN_CONV - 1, 2 * D_FF), F32)])
        x2d, st = _ffn(x2d, _pad_state(ffn_conv[l]), nwf, w["ffn_w_up"][l].astype(BF16), cwb,
                       w["ffn_w_down"][l].astype(BF16), b, t)
        new["ffn_conv"].append(st.reshape(b, SUBLANES, 2 * D_FF)[:, SUBLANES - (FFN_CONV - 1):])
    return x2d.reshape(b, t, D_MODEL), {k: jnp.stack(v) for k, v in new.items()}


def kernel(x_prompt, x_sample, state_rwkv_wkv, state_rwkv_shift, cache_mla_latent, cache_mla_krope, state_gdn_S, state_gdn_conv, state_ffn_conv, page_table, norm_w, rw_mu, rw_wrkv, rw_w0, rw_w1, rw_w2, rw_a0, rw_a1, rw_a2, rw_v0, rw_v1, rw_v2, rw_g1, rw_g2, rw_kk, rw_ka, rw_rk, rw_lnx_w, rw_lnx_b, rw_wo, mla_w_in, mla_q_norm, mla_kv_norm, mla_w_qb, mla_w_uk, mla_w_uv, mla_wo, gdn_w_in, gdn_conv_w, gdn_a_log, gdn_dt_bias, gdn_norm_w, gdn_wo, ffn_w_up, ffn_conv_w, ffn_conv_b, ffn_w_down):
    w = dict(norm_w=norm_w, rw_mu=rw_mu, rw_wrkv=rw_wrkv, rw_w0=rw_w0, rw_w1=rw_w1, rw_w2=rw_w2, rw_a0=rw_a0,
             rw_a1=rw_a1, rw_a2=rw_a2, rw_v0=rw_v0, rw_v1=rw_v1, rw_v2=rw_v2, rw_g1=rw_g1, rw_g2=rw_g2,
             rw_kk=rw_kk, rw_ka=rw_ka, rw_rk=rw_rk, rw_lnx_w=rw_lnx_w, rw_lnx_b=rw_lnx_b, rw_wo=rw_wo,
             mla_w_in=mla_w_in, mla_q_norm=mla_q_norm, mla_kv_norm=mla_kv_norm, mla_w_qb=mla_w_qb,
             mla_w_uk=mla_w_uk, mla_w_uv=mla_w_uv, mla_wo=mla_wo, gdn_w_in=gdn_w_in, gdn_conv_w=gdn_conv_w,
             gdn_a_log=gdn_a_log, gdn_dt_bias=gdn_dt_bias, gdn_norm_w=gdn_norm_w, gdn_wo=gdn_wo,
             ffn_w_up=ffn_w_up, ffn_conv_w=ffn_conv_w, ffn_conv_b=ffn_conv_b, ffn_w_down=ffn_w_down)
    b, t = x_prompt.shape[0], x_prompt.shape[1]
    n_rw, n_gdn, depth = state_rwkv_wkv.shape[0], state_gdn_S.shape[0], state_ffn_conv.shape[0]
    y_p, sp = _trunk(
        x_prompt, jnp.arange(t),
        jnp.zeros((n_rw, b) + state_rwkv_wkv.shape[2:], F32), jnp.zeros((n_rw, b, D_MODEL), F32),
        jnp.zeros((n_gdn, b) + state_gdn_S.shape[2:], F32), jnp.zeros((n_gdn, b) + state_gdn_conv.shape[2:], F32),
        jnp.zeros((depth, b) + state_ffn_conv.shape[2:], F32), w, None)
    past_len = page_table.shape[1] * cache_mla_latent.shape[2]
    pos_s = past_len + jnp.arange(x_sample.shape[1])
    y_s, ss = _trunk(x_sample, pos_s, state_rwkv_wkv, state_rwkv_shift, state_gdn_S, state_gdn_conv,
                     state_ffn_conv, w, (cache_mla_latent, cache_mla_krope, page_table))
    names = ("rw_S", "rw_shift", "mla_c", "mla_kr", "gdn_S", "gdn_conv", "ffn_conv")
    return (y_p, y_s) + tuple(sp[k] for k in names) + tuple(ss[k] for k in names)
```

```python
import functools

import jax
import jax.numpy as jnp
from jax import lax
from jax.experimental import pallas as pl
from jax.experimental.pallas import tpu as pltpu

F32 = jnp.float32
BF16 = jnp.bfloat16
HIGHEST = lax.Precision.HIGHEST

D_MODEL = 1024
NORM_EPS = 1e-6
RW_N = 64
RW_H = D_MODEL // RW_N
RW_LNX_EPS = 64e-5
MLA_H = 16
MLA_NOPE = 64
MLA_ROPE = 32
MLA_V = 64
MLA_Q_LORA = 512
MLA_KV_LORA = 256
MLA_SCALE = (MLA_NOPE + MLA_ROPE) ** -0.5
ROPE_THETA = 10000.0
MLA_QK = MLA_KV_LORA + 128
GDN_QK_H = 8
GDN_V_H = 16
GDN_DK = 128
GDN_DV = 128
GDN_QK_DIM = GDN_QK_H * GDN_DK
GDN_V_DIM = GDN_V_H * GDN_DV
GDN_CONV_DIM = 2 * GDN_QK_DIM + GDN_V_DIM
GDN_CONV = 4
D_FF = 2816
FFN_CONV = 3
LAYER_MIXER = (0, 1, 2, 0)

SUBLANES = 8
LANES = 128
GROUP_ROWS = 128
VMEM_LIMIT = 56 * 1024 * 1024


def _rms(x, w):
    return x * lax.rsqrt(jnp.mean(x * x, axis=-1, keepdims=True) + NORM_EPS) * w


def _bdot(a, b):
    return jnp.dot(a.astype(BF16), b.astype(BF16), preferred_element_type=F32)


def _bdot_nt(a, b):
    return lax.dot_general(a.astype(BF16), b.astype(BF16), (((1,), (1,)), ((), ())),
                           preferred_element_type=F32)


def _bdot_tn(a, b):
    return lax.dot_general(a.astype(BF16), b.astype(BF16), (((0,), (0,)), ((), ())),
                           preferred_element_type=F32)


def _hdot(a, b):
    return jnp.dot(a, b, precision=HIGHEST, preferred_element_type=F32)


def _sigmoid(x):
    return 1.0 / (1.0 + jnp.exp(-x))


def _softplus(x):
    return jnp.maximum(x, 0.0) + jnp.log(1.0 + jnp.exp(-jnp.abs(x)))


def _silu(x):
    return x * _sigmoid(x)


def _shift_rows(u, prev, s):
    rows, cols = u.shape
    p = prev.shape[0]
    rolled = pltpu.roll(u, s, 0)
    fix = pltpu.roll(prev, (p - SUBLANES + s) % p, 0)
    t = lax.broadcasted_iota(jnp.int32, (p, cols), 0) % SUBLANES
    if p == rows:
        return jnp.where(t < s, fix, rolled)
    head = jnp.where(t < s, fix, rolled[:SUBLANES])
    return jnp.concatenate([head, rolled[SUBLANES:]], axis=0)


def _lane_group_sum(x, ones2):
    parts = []
    for i in range(x.shape[1] // LANES):
        xs = x[:, i * LANES:(i + 1) * LANES]
        hi = xs.astype(BF16)
        lo = (xs - hi.astype(F32)).astype(BF16)
        parts.append(jnp.dot(jnp.concatenate([hi, lo], axis=1), ones2, preferred_element_type=F32))
    return parts[0] if len(parts) == 1 else jnp.concatenate(parts, axis=1)


def _chunk_cumsum(x, tri):
    bc = tri.shape[0]
    parts = [_hdot(tri, x[i * bc:(i + 1) * bc]) for i in range(x.shape[0] // bc)]
    return parts[0] if len(parts) == 1 else jnp.concatenate(parts, axis=0)


def _tiling(b, t, tt_max):
    if t == SUBLANES:
        return dict(G=1, J=1, R=b * t, P=b * t)
    tt = min(t, tt_max)
    assert t % tt == 0 and tt % 64 == 0, (t, tt)
    return dict(G=b, J=t // tt, R=tt, P=SUBLANES)


def _chunk_of(t):
    return 64 if t % 64 == 0 else t


def _const_spec(shape):
    nd = len(shape)
    return pl.BlockSpec(shape, lambda *_: (0,) * nd, pipeline_mode=pl.Buffered(1))


def _params(n_axes):
    return pltpu.CompilerParams(dimension_semantics=("arbitrary",) * n_axes,
                                vmem_limit_bytes=VMEM_LIMIT)


def _pad_state(st):
    b, k1, c = st.shape
    return jnp.pad(st, ((0, 0), (SUBLANES - k1, 0), (0, 0))).reshape(b * SUBLANES, c)


def _chunk_masks(chunk, rows):
    i = jnp.arange(rows)
    same = (i[:, None] // chunk) == (i[None, :] // chunk)
    tri = same & ((i[None, :] % chunk) <= (i[:, None] % chunk))
    return tri.astype(F32)


FFN_CW = 256


ROW_BLOCK = 64


def _stage(buf_ref, u, carry_ref, st_ref, sl):
    rows = u.shape[0]
    buf_ref[0:SUBLANES, :] = carry_ref[:, sl]
    buf_ref[SUBLANES:SUBLANES + rows, :] = u
    tail = u[rows - SUBLANES:, :]
    carry_ref[:, sl] = tail
    st_ref[:, sl] = tail


def _taps(buf_ref, r0, nrows, shifts):
    slab = buf_ref[r0:r0 + nrows + SUBLANES, :]
    return slab[SUBLANES:], [pltpu.roll(slab, s, 0)[SUBLANES:] for s in shifts]


def _ffn_body(x_ref, prev_ref, nw_ref, wup_ref, cwb_ref, wdn_ref, xo_ref, st_ref, carry_ref, act_ref, buf_ref):
    @pl.when(pl.program_id(1) == 0)
    def _():
        carry_ref[...] = prev_ref[...]

    x = x_ref[...]
    rows = x.shape[0]
    stacked = carry_ref.shape[0] == rows
    h = _rms(x, nw_ref[0:1, :]).astype(BF16)
    nch = D_FF // FFN_CW

    def cols(c, half):
        return slice(half * D_FF + c * FFN_CW, half * D_FF + (c + 1) * FFN_CW)

    def conv_gate(c, taps):
        ys = []
        for half in range(2):
            sl = cols(c, half)
            u, (u2, u1) = taps[half]
            ys.append(cwb_ref[0:1, sl] * u2 + cwb_ref[1:2, sl] * u1 + cwb_ref[2:3, sl] * u + cwb_ref[3:4, sl])
        return (_silu(ys[0]) * ys[1]).astype(BF16)

    def up(c):
        us = [jnp.dot(h, wup_ref[:, cols(c, half)], preferred_element_type=F32) for half in range(2)]
        if stacked:
            return us
        for half in range(2):
            _stage(buf_ref.at[(c % 2) * 2 + half], us[half], carry_ref, st_ref, cols(c, half))
        return None

    u_cur = up(0)
    for c in range(nch):
        u_nxt = up(c + 1) if c + 1 < nch else None
        csl = slice(c * FFN_CW, (c + 1) * FFN_CW)
        if stacked:
            taps = []
            for half in range(2):
                sl = cols(c, half)
                u = u_cur[half]
                prev = carry_ref[:, sl]
                taps.append((u, [_shift_rows(u, prev, 2), _shift_rows(u, prev, 1)]))
                carry_ref[:, sl] = u
                st_ref[:, sl] = u
            act_ref[:, csl] = conv_gate(c, taps)
        else:
            for r0 in range(0, rows, ROW_BLOCK):
                taps = [_taps(buf_ref.at[(c % 2) * 2 + half], r0, ROW_BLOCK, (2, 1)) for half in range(2)]
                act_ref[r0:r0 + ROW_BLOCK, csl] = conv_gate(c, taps)
        u_cur = u_nxt
    f = jnp.dot(act_ref[...], wdn_ref[...], preferred_element_type=F32)
    xo_ref[...] = x + _rms(f, nw_ref[1:2, :])


def _ffn(x2d, prev, nw, wup, cwb, wdn, b, t):
    tl = _tiling(b, t, 512)
    G, J, R, P = tl["G"], tl["J"], tl["R"], tl["P"]
    n = b * t
    return pl.pallas_call(
        _ffn_body,
        grid=(G, J),
        in_specs=[
            pl.BlockSpec((R, D_MODEL), lambda g, j: (g * J + j, 0)),
            pl.BlockSpec((P, 2 * D_FF), lambda g, j: (g, 0)),
            _const_spec((8, D_MODEL)),
            _const_spec((D_MODEL, 2 * D_FF)),
            _const_spec((8, 2 * D_FF)),
            _const_spec((D_FF, D_MODEL)),
        ],
        out_specs=[
            pl.BlockSpec((R, D_MODEL), lambda g, j: (g * J + j, 0)),
            pl.BlockSpec((P, 2 * D_FF), lambda g, j: (g, 0)),
        ],
        out_shape=[jax.ShapeDtypeStruct((n, D_MODEL), F32),
                   jax.ShapeDtypeStruct((b * SUBLANES, 2 * D_FF), F32)],
        scratch_shapes=[pltpu.VMEM((P, 2 * D_FF), F32), pltpu.VMEM((R, D_FF), BF16),
                        pltpu.VMEM((4, R + SUBLANES, FFN_CW), F32)],
        compiler_params=_params(2),
        name="conv_ffn",
    )(x2d, prev, nw, wup, cwb, wdn)


def _outproj_body(*refs, gated):
    x_ref, y_ref = refs[0], refs[1]
    wo_ref, nw_ref, xo_ref = refs[-3:]
    y = y_ref[...]
    if gated:
        y = y * refs[2][...]
    o = jnp.dot(y.astype(BF16), wo_ref[...], preferred_element_type=F32)
    xo_ref[...] = x_ref[...] + _rms(o, nw_ref[0:1, :])


def _outproj(x2d, y2d, gate2d, wo, nw, name):
    n, k = y2d.shape
    R = min(n, 512)
    row = lambda i: (i, 0)
    acts = [y2d] if gate2d is None else [y2d, gate2d]
    return pl.pallas_call(
        functools.partial(_outproj_body, gated=gate2d is not None),
        grid=(n // R,),
        in_specs=[pl.BlockSpec((R, D_MODEL), row)] + [pl.BlockSpec((R, k), row)] * len(acts)
        + [_const_spec((k, D_MODEL)), _const_spec((8, D_MODEL))],
        out_specs=pl.BlockSpec((R, D_MODEL), row),
        out_shape=jax.ShapeDtypeStruct((n, D_MODEL), F32),
        compiler_params=_params(1),
        name=name,
    )(x2d, *acts, wo, nw)


def _rwkv_proj_body(*refs, has_vres, chunk):
    it = iter(refs)
    x_ref, prev_ref = next(it), next(it)
    vf_ref = next(it) if has_vres else None
    vec_ref, wrkv_ref, w1_ref, w2_ref, a1_ref, a2_ref = (next(it) for _ in range(6))
    v1_ref, v2_ref = (next(it), next(it)) if has_vres else (None, None)
    g1_ref, g2_ref, tri_ref, ones_ref = (next(it) for _ in range(4))
    rt_ref, kt_ref, at_ref, bt_ref, v_ref, g_ref, gl_ref, hl_ref, carry_ref = (next(it) for _ in range(9))

    @pl.when(pl.program_id(1) == 0)
    def _():
        carry_ref[...] = prev_ref[...]

    x = x_ref[...]
    rows = x.shape[0]
    p = carry_ref.shape[0]
    h = _rms(x, vec_ref[10:11, :])
    d = _shift_rows(h, carry_ref[...], 1) - h
    tail = h[rows - p:, :]
    carry_ref[...] = tail
    hl_ref[...] = tail

    def mix(i):
        return (h + d * vec_ref[i:i + 1, :]).astype(BF16)

    r = jnp.dot(mix(0), wrkv_ref[0], preferred_element_type=F32)
    k = jnp.dot(mix(1), wrkv_ref[1], preferred_element_type=F32)
    xv = mix(2)
    v = jnp.dot(xv, wrkv_ref[2], preferred_element_type=F32)
    w_raw = vec_ref[6:7, :] + _bdot(jnp.tanh(_bdot(mix(3), w1_ref[...])), w2_ref[...])
    w = -_softplus(-w_raw) - 0.5
    if has_vres:
        gate_v = _sigmoid(vec_ref[11:12, :] + _bdot(_bdot(xv, v1_ref[...]), v2_ref[...]))
        v = v + (vf_ref[...] - v) * gate_v
    a = _sigmoid(vec_ref[7:8, :] + _bdot(_bdot(mix(4), a1_ref[...]), a2_ref[...]))
    g_ref[...] = _bdot(_sigmoid(_bdot(mix(5), g1_ref[...])), g2_ref[...])
    v_ref[...] = v

    kk = k * vec_ref[8:9, :]
    kk = kk * lax.rsqrt(_lane_group_sum(kk * kk, ones_ref[...]) + 1e-6)
    k = k * (1.0 + (a - 1.0) * vec_ref[9:10, :])

    lw = -jnp.exp(w)
    cum = _chunk_cumsum(lw, tri_ref[...])
    e_fwd = jnp.exp(cum)
    e_bwd = jnp.exp(-cum)
    rt_ref[...] = r * e_fwd
    kt_ref[...] = k * e_bwd
    at_ref[...] = -kk * jnp.exp(cum - lw)
    bt_ref[...] = kk * a * e_bwd
    for c in range(rows // chunk):
        gl_ref[c] = jnp.exp(cum[(c + 1) * chunk - 1:(c + 1) * chunk, :])


def _rwkv_scan_body(rt_ref, kt_ref, at_ref, bt_ref, v_ref, gl_ref, h0_ref, vec_ref, ones_ref,
                    y_ref, ho_ref, *, nh, chunk):
    @pl.when(pl.program_id(1) == 0)
    def _():
        ho_ref[...] = h0_ref[...]

    gl_lanes = nh * RW_N
    ng = RW_H // nh
    gc = nh * chunk
    row_head = lax.broadcasted_iota(jnp.int32, (gc, gl_lanes), 0) // chunk
    lane_head = lax.broadcasted_iota(jnp.int32, (gc, gl_lanes), 1) // RW_N
    own = row_head == lane_head
    ri = lax.broadcasted_iota(jnp.int32, (gc, gc), 0)
    ci = lax.broadcasted_iota(jnp.int32, (gc, gc), 1)
    same = (ri // chunk) == (ci // chunk)
    strict = same & ((ri % chunk) > (ci % chunk))
    incl = same & ((ri % chunk) >= (ci % chunk))
    eye = (ri == ci).astype(F32)
    eye_l = (lax.broadcasted_iota(jnp.int32, (gl_lanes, gl_lanes), 0)
             == lax.broadcasted_iota(jnp.int32, (gl_lanes, gl_lanes), 1))
    ones_bd = ones_ref[...]

    def blockdiag(xg):
        xx = jnp.concatenate([xg] * nh, axis=0) if nh > 1 else xg
        return jnp.where(own, xx, 0.0)

    groups = range(ng)
    sls = [slice(q * gl_lanes, (q + 1) * gl_lanes) for q in groups]
    merged = gc == GROUP_ROWS
    gl_rows = [gl_ref[0, :, sl] for sl in sls]
    r_bd = [blockdiag(rt_ref[:, sl]) for sl in sls]
    k_bd = [blockdiag(kt_ref[:, sl]) for sl in sls]
    a_bd = [blockdiag(at_ref[:, sl]) for sl in sls]
    b_bd = [blockdiag(bt_ref[:, sl]) for sl in sls]
    v_bd = [blockdiag(v_ref[:, sl]).astype(BF16) for sl in sls]
    hs = [ho_ref[0, q] for q in groups]
    hs_b = [h.astype(BF16) for h in hs]
    if merged:
        ar = [jnp.concatenate([a_bd[q], r_bd[q]], axis=0).astype(BF16) for q in groups]
        bk = [jnp.concatenate([b_bd[q], k_bd[q]], axis=0).astype(BF16) for q in groups]
        amat = [_bdot_nt(ar[q], bk[q]) for q in groups]
        a_ab = [jnp.where(strict, m[:gc, :gc], 0.0) for m in amat]
        a_ak = [jnp.where(strict, m[:gc, gc:], 0.0).astype(BF16) for m in amat]
        a_rbk = [jnp.concatenate([jnp.where(incl, m[gc:, :gc], 0.0), jnp.where(incl, m[gc:, gc:], 0.0)],
                                 axis=1).astype(BF16) for m in amat]
        arh = [_bdot(ar[q], hs_b[q]) for q in groups]
        ah = [m[:gc] for m in arh]
        rh = [m[gc:] for m in arh]
    else:
        ab_, rb_ = [x.astype(BF16) for x in a_bd], [x.astype(BF16) for x in r_bd]
        bb_, kb_ = [x.astype(BF16) for x in b_bd], [x.astype(BF16) for x in k_bd]
        a_ab = [jnp.where(strict, _bdot_nt(ab_[q], bb_[q]), 0.0) for q in groups]
        a_ak = [jnp.where(strict, _bdot_nt(ab_[q], kb_[q]), 0.0).astype(BF16) for q in groups]
        a_rb = [jnp.where(incl, _bdot_nt(rb_[q], bb_[q]), 0.0).astype(BF16) for q in groups]
        a_rk = [jnp.where(incl, _bdot_nt(rb_[q], kb_[q]), 0.0).astype(BF16) for q in groups]
        ah = [_bdot(ab_[q], hs_b[q]) for q in groups]
        rh = [_bdot(rb_[q], hs_b[q]) for q in groups]
    akv = [_bdot(a_ak[q], v_bd[q]) for q in groups]

    p = [m.astype(BF16) for m in a_ab]
    x = [eye + m for m in a_ab]
    span = 2
    if span < chunk:
        p = [_bdot(p[q], p[q]) for q in groups]
    while span < chunk:
        last = span * 2 >= chunk
        if merged and not last:
            px = [_bdot(p[q], jnp.concatenate([p[q].astype(BF16), x[q].astype(BF16)], axis=1)) for q in groups]
            p = [m[:, :gc] for m in px]
            x = [x[q] + px[q][:, gc:] for q in groups]
        else:
            pb = [m.astype(BF16) for m in p]
            x = [x[q] + _bdot(pb[q], x[q]) for q in groups]
            if not last:
                p = [_bdot(pb[q], pb[q]) for q in groups]
        span *= 2

    u = [_bdot(x[q], ah[q] + akv[q]).astype(BF16) for q in groups]
    if merged:
        uv = [jnp.concatenate([u[q], v_bd[q]], axis=0) for q in groups]
        y_bd = [rh[q] + _bdot(a_rbk[q], uv[q]) for q in groups]
        for q in groups:
            gl_col = jnp.sum(jnp.where(eye_l, gl_rows[q], 0.0), axis=1, keepdims=True)
            bk_g = jnp.concatenate([b_bd[q] * gl_rows[q], k_bd[q] * gl_rows[q]], axis=0)
            ho_ref[0, q] = hs[q] * gl_col + _bdot_tn(bk_g, uv[q])
    else:
        y_bd = [rh[q] + _bdot(a_rb[q], u[q]) + _bdot(a_rk[q], v_bd[q]) for q in groups]
        for q in groups:
            gl_col = jnp.sum(jnp.where(eye_l, gl_rows[q], 0.0), axis=1, keepdims=True)
            ho_ref[0, q] = (hs[q] * gl_col + _bdot_tn(b_bd[q] * gl_rows[q], u[q])
                            + _bdot_tn(k_bd[q] * gl_rows[q], v_bd[q]))

    for q in groups:
        sl = sls[q]
        y = y_bd[q][0:chunk]
        for hh in range(1, nh):
            y = y + y_bd[q][hh * chunk:(hh + 1) * chunk]
        mu = _lane_group_sum(y, ones_bd) * (1.0 / RW_N)
        yc = y - mu
        var = _lane_group_sum(yc * yc, ones_bd) * (1.0 / RW_N)
        yn = yc * lax.rsqrt(var + RW_LNX_EPS) * vec_ref[0:1, sl] + vec_ref[1:2, sl]
        bonus = _lane_group_sum(rt_ref[:, sl] * kt_ref[:, sl] * vec_ref[2:3, sl], ones_bd) * v_ref[:, sl]
        y_ref[:, sl] = yn + bonus


def _rwkv_layer(x2d, shift_prev, s0, v_first, w, ri, nw, b, t):
    n = b * t
    chunk = _chunk_of(t)
    tl = _tiling(b, t, 256)
    G, J, R, P = tl["G"], tl["J"], tl["R"], tl["P"]
    has_vres = v_first is not None
    vi = ri - 1
    bc = chunk if chunk == 64 else R
    tri = _chunk_masks(chunk, bc)
    li = jnp.arange(LANES)
    ones_bd = ((li[:, None] // RW_N) == (li[None, :] // RW_N)).astype(BF16)
    ones_bd = jnp.concatenate([ones_bd, ones_bd], axis=0)
    zero = jnp.zeros((D_MODEL,), F32)
    vec = jnp.stack([*(w["rw_mu"][ri][i] for i in range(6)), w["rw_w0"][ri], w["rw_a0"][ri], w["rw_kk"][ri],
                     w["rw_ka"][ri], nw[0], w["rw_v0"][vi] if has_vres else zero, zero, zero, zero, zero])
    row = lambda g, j: (g * J + j, 0)
    row_spec = pl.BlockSpec((R, D_MODEL), row)
    ins = [x2d, _pad_state(shift_prev[:, None, :])]
    specs = [row_spec, pl.BlockSpec((P, D_MODEL), lambda g, j: (g, 0))]
    if has_vres:
        ins.append(v_first)
        specs.append(row_spec)
    wl = [vec, w["rw_wrkv"][ri].astype(BF16), w["rw_w1"][ri].astype(BF16), w["rw_w2"][ri].astype(BF16),
          w["rw_a1"][ri].astype(BF16), w["rw_a2"][ri].astype(BF16)]
    if has_vres:
        wl += [w["rw_v1"][vi].astype(BF16), w["rw_v2"][vi].astype(BF16)]
    wl += [w["rw_g1"][ri].astype(BF16), w["rw_g2"][ri].astype(BF16), tri, ones_bd]
    ins += wl
    specs += [_const_spec(a.shape) for a in wl]
    nc_tile = R // chunk
    outs = pl.pallas_call(
        functools.partial(_rwkv_proj_body, has_vres=has_vres, chunk=chunk),
        grid=(G, J),
        in_specs=specs,
        out_specs=[row_spec] * 6 + [pl.BlockSpec((nc_tile, 1, D_MODEL), lambda g, j: (g * J + j, 0, 0)),
                                    pl.BlockSpec((P, D_MODEL), lambda g, j: (g, 0))],
        out_shape=[jax.ShapeDtypeStruct((n, D_MODEL), F32)] * 6
        + [jax.ShapeDtypeStruct((n // chunk, 1, D_MODEL), F32), jax.ShapeDtypeStruct((b * SUBLANES, D_MODEL), F32)],
        scratch_shapes=[pltpu.VMEM((P, D_MODEL), F32)],
        compiler_params=_params(2),
        name="rwkv_proj",
    )(*ins)
    rt, kt, at, bt, v, g, gl, hl = outs
    shift_new = hl.reshape(b, SUBLANES, D_MODEL)[:, -1]

    nh = LANES // RW_N
    ng = RW_H // nh
    gl_lanes = nh * RW_N
    hkv = jnp.swapaxes(s0, -1, -2).reshape(b, ng, nh, RW_N, RW_N)
    h0 = jnp.einsum("bqikv,ij->bqikjv", hkv, jnp.eye(nh, dtype=F32)).reshape(b, ng, gl_lanes, gl_lanes)
    svec = jnp.stack([w["rw_lnx_w"][ri], w["rw_lnx_b"][ri], w["rw_rk"][ri].reshape(D_MODEL),
                      zero, zero, zero, zero, zero])
    nct = t // chunk
    crow = lambda bb, j: (bb * nct + j, 0)
    cspec = pl.BlockSpec((chunk, D_MODEL), crow)
    hspec = pl.BlockSpec((1, ng, gl_lanes, gl_lanes), lambda bb, j: (bb, 0, 0, 0))
    y, hout = pl.pallas_call(
        functools.partial(_rwkv_scan_body, nh=nh, chunk=chunk),
        grid=(b, nct),
        in_specs=[cspec] * 5 + [pl.BlockSpec((1, 1, D_MODEL), lambda bb, j: (bb * nct + j, 0, 0)), hspec,
                                _const_spec((8, D_MODEL)), _const_spec((2 * LANES, LANES))],
        out_specs=[cspec, hspec],
        out_shape=[jax.ShapeDtypeStruct((n, D_MODEL), F32),
                   jax.ShapeDtypeStruct((b, ng, gl_lanes, gl_lanes), F32)],
        compiler_params=_params(2),
        name="rwkv_scan",
    )(rt, kt, at, bt, v, gl, h0, svec, ones_bd)
    h6 = hout.reshape(b, ng, nh, RW_N, nh, RW_N)
    s_new = jnp.stack([h6[:, :, i, :, i, :] for i in range(nh)], axis=2)
    s_new = jnp.swapaxes(s_new, -1, -2).reshape(b, RW_H, RW_N, RW_N)
    nwp = jnp.concatenate([nw[1:2], jnp.zeros((7, D_MODEL), F32)])
    x_new = _outproj(x2d, y, g, w["rw_wo"][ri].astype(BF16), nwp, "rwkv_out")
    return x_new, shift_new, s_new, (v if not has_vres else v_first)


def _rope_lanes(x, tab_ref):
    half = MLA_ROPE // 2
    return (x * tab_ref[0] + pltpu.roll(x, LANES - half, 1) * tab_ref[1] + pltpu.roll(x, half, 1) * tab_ref[2])


def _mla_proj_body(x_ref, nw_ref, tab_ref, winq_ref, winc_ref, wink_ref, qn_ref, kvn_ref, wqn_ref, wqr_ref,
                   wuk_ref, c_ref, kr_ref, kcat_ref, qcat_ref):
    h = _rms(x_ref[...], nw_ref[0:1, :]).astype(BF16)
    cq = _rms(jnp.dot(h, winq_ref[...], preferred_element_type=F32), qn_ref[...]).astype(BF16)
    c = _rms(jnp.dot(h, winc_ref[...], preferred_element_type=F32), kvn_ref[...])
    kr = _rope_lanes(jnp.dot(h, wink_ref[...], preferred_element_type=F32), tab_ref)
    c_ref[...] = c
    kr_ref[...] = kr
    adt = kcat_ref.dtype
    kcat_ref[:, 0:MLA_KV_LORA] = c.astype(adt)
    kcat_ref[:, MLA_KV_LORA:MLA_QK] = kr.astype(adt)
    qn = jnp.dot(cq, wqn_ref[...], preferred_element_type=F32).astype(BF16)
    qr = jnp.dot(cq, wqr_ref[...], preferred_element_type=F32)
    for pr in range(MLA_H // 2):
        ql = jnp.dot(qn[:, pr * LANES:(pr + 1) * LANES], wuk_ref[pr], preferred_element_type=F32) * MLA_SCALE
        qcat_ref[2 * pr, :, 0:MLA_KV_LORA] = ql[:, :MLA_KV_LORA].astype(adt)
        qcat_ref[2 * pr + 1, :, 0:MLA_KV_LORA] = ql[:, MLA_KV_LORA:].astype(adt)
    for hh in range(MLA_H):
        qro = _rope_lanes(qr[:, hh * LANES:(hh + 1) * LANES], tab_ref) * MLA_SCALE
        qcat_ref[hh, :, MLA_KV_LORA:MLA_QK] = qro.astype(adt)


MLA_TQ = 128
MLA_TK = 256
MLA_SPLIT = 8


def _mla_prompt_body(q_ref, k_ref, o_ref, m_ref, l_ref, acc_ref):
    i = pl.program_id(1)
    rows = MLA_H * MLA_TQ
    q = q_ref[...].reshape(rows, MLA_QK)
    m_ref[...] = jnp.full((rows, LANES), -jnp.inf, F32)
    l_ref[...] = jnp.zeros((rows, LANES), F32)
    acc_ref[...] = jnp.zeros((rows, MLA_KV_LORA), F32)
    ones = jnp.ones((MLA_TK, LANES), BF16)
    reps = MLA_TK // LANES

    sub = rows // MLA_SPLIT

    def block(k0, masked):
        kblk = k_ref[pl.ds(k0, MLA_TK), :]
        ss = [lax.dot_general(q[g * sub:(g + 1) * sub], kblk, (((1,), (1,)), ((), ())),
                              preferred_element_type=F32) for g in range(MLA_SPLIT)]
        for g in range(MLA_SPLIT):
            rs = slice(g * sub, (g + 1) * sub)
            s = ss[g]
            if masked:
                qpos = i * MLA_TQ + (g * sub + lax.broadcasted_iota(jnp.int32, (sub, MLA_TK), 0)) % MLA_TQ
                kpos = k0 + lax.broadcasted_iota(jnp.int32, (sub, MLA_TK), 1)
                s = jnp.where(kpos <= qpos, s, -jnp.inf)
            m_old = m_ref[rs, :]
            m_new = jnp.maximum(m_old, jnp.max(s, axis=1, keepdims=True))
            alpha = jnp.exp(m_old - m_new)
            p = jnp.exp(s - jnp.concatenate([m_new] * reps, axis=1)).astype(BF16)
            l_ref[rs, :] = l_ref[rs, :] * alpha + jnp.dot(p, ones, preferred_element_type=F32)
            acc_ref[rs, :] = (acc_ref[rs, :] * jnp.concatenate([alpha] * (MLA_KV_LORA // LANES), axis=1)
                              + jnp.dot(p, kblk[:, :MLA_KV_LORA], preferred_element_type=F32))
            m_ref[rs, :] = m_new

    def full_step(kb, carry):
        block(pl.multiple_of(kb * MLA_TK, MLA_TK), False)
        return carry

    n_full = (i * MLA_TQ) // MLA_TK
    lax.fori_loop(0, n_full, full_step, 0)
    block(pl.multiple_of(n_full * MLA_TK, MLA_TK), True)
    o = acc_ref[...] / jnp.concatenate([l_ref[...]] * (MLA_KV_LORA // LANES), axis=1)
    o_ref[...] = o.reshape(MLA_H, MLA_TQ, MLA_KV_LORA).astype(BF16)


MLA_PP = 16


def _mla_sample_body(pt_ref, q_ref, kn_ref, *rest):
    lat_refs = rest[:MLA_PP]
    kro_refs = rest[MLA_PP:2 * MLA_PP]
    o_ref, m_ref, l_ref, acc_ref = rest[2 * MLA_PP:]
    j = pl.program_id(1)
    t = q_ref.shape[1]
    rows = MLA_H * t
    q = q_ref[...].reshape(rows, MLA_QK).astype(BF16)
    ql = q[:, :MLA_KV_LORA]
    qr = q[:, MLA_KV_LORA:MLA_KV_LORA + MLA_ROPE]

    @pl.when(j == 0)
    def _():
        m_ref[...] = jnp.full((rows, LANES), -jnp.inf, F32)
        l_ref[...] = jnp.zeros((rows, LANES), F32)
        acc_ref[...] = jnp.zeros((rows, MLA_KV_LORA), F32)

    vrep = MLA_KV_LORA // LANES

    def update(s, vals, row_sum):
        m_old = m_ref[...]
        m_new = jnp.maximum(m_old, jnp.max(s, axis=1, keepdims=True))
        alpha = jnp.exp(m_old - m_new)
        if s.shape[1] % LANES == 0:
            p = jnp.exp(s - jnp.concatenate([m_new] * (s.shape[1] // LANES), axis=1)).astype(BF16)
        else:
            p = jnp.exp(s - m_new[:, 0:1]).astype(BF16)
        l_ref[...] = l_ref[...] * alpha + row_sum(p)
        acc_ref[...] = (acc_ref[...] * jnp.concatenate([alpha] * vrep, axis=1)
                        + jnp.dot(p, vals, preferred_element_type=F32))
        m_ref[...] = m_new

    cbs, s_parts = [], []
    for pp in range(MLA_PP):
        cb = lat_refs[pp][0].astype(BF16)
        kb = kro_refs[pp][0].astype(BF16)
        s_parts.append(lax.dot_general(ql, cb, (((1,), (1,)), ((), ())), preferred_element_type=F32)
                       + lax.dot_general(qr, kb, (((1,), (1,)), ((), ())), preferred_element_type=F32))
        cbs.append(cb)
    s_all = jnp.concatenate(s_parts, axis=1)
    ones = jnp.ones((s_all.shape[1], LANES), BF16)
    update(s_all, jnp.concatenate(cbs, axis=0), lambda p: jnp.dot(p, ones, preferred_element_type=F32))

    @pl.when(j == pl.num_programs(1) - 1)
    def _():
        kn = kn_ref[...].astype(BF16)
        s = lax.dot_general(q, kn, (((1,), (1,)), ((), ())), preferred_element_type=F32)
        qpos = lax.broadcasted_iota(jnp.int32, (rows, t), 0) % t
        kpos = lax.broadcasted_iota(jnp.int32, (rows, t), 1)
        s = jnp.where(kpos <= qpos, s, -jnp.inf)
        update(s, kn[:, :MLA_KV_LORA], lambda p: jnp.sum(p.astype(F32), axis=1, keepdims=True))
        o = acc_ref[...] / jnp.concatenate([l_ref[...]] * vrep, axis=1)
        o_ref[...] = o.reshape(MLA_H, t, MLA_KV_LORA).astype(o_ref.dtype)


def _mla_out_body(x_ref, o_ref, wuv_ref, wo_ref, nw_ref, xo_ref):
    parts = []
    for pr in range(MLA_H // 2):
        wp = wuv_ref[pr]
        parts.append(jnp.dot(o_ref[2 * pr].astype(BF16), wp[:MLA_KV_LORA], preferred_element_type=F32)
                     + jnp.dot(o_ref[2 * pr + 1].astype(BF16), wp[MLA_KV_LORA:], preferred_element_type=F32))
    v = jnp.concatenate(parts, axis=1).astype(BF16)
    o = jnp.dot(v, wo_ref[...], preferred_element_type=F32)
    xo_ref[...] = x_ref[...] + _rms(o, nw_ref[0:1, :])


def _mla_layer(x2d, pos, w, mi, nw, b, t, paged):
    n = b * t
    tl = _tiling(b, t, 512)
    G, J, R = tl["G"], tl["J"], tl["R"]
    adt = BF16 if t % 16 == 0 else F32
    half = MLA_ROPE // 2
    inv = ROPE_THETA ** (-jnp.arange(half, dtype=F32) / half)
    ang = pos.astype(F32)[:, None] * inv[None, :]
    cos, sin = jnp.cos(ang), jnp.sin(ang)
    zpad = jnp.zeros((t, LANES - MLA_ROPE), F32)
    zh = jnp.zeros((t, half), F32)
    tab = jnp.stack([jnp.concatenate([cos, cos, zpad], 1), jnp.concatenate([-sin, zh, zpad], 1),
                     jnp.concatenate([zh, sin, zpad], 1)])
    if G == 1:
        tab = jnp.tile(tab, (1, b, 1))
    w_in = w["mla_w_in"][mi]
    winq = w_in[:, :MLA_Q_LORA].astype(BF16)
    winc = w_in[:, MLA_Q_LORA:MLA_Q_LORA + MLA_KV_LORA].astype(BF16)
    wink = jnp.pad(w_in[:, MLA_Q_LORA + MLA_KV_LORA:], ((0, 0), (0, LANES - MLA_ROPE))).astype(BF16)
    wqb = w["mla_w_qb"][mi].reshape(MLA_Q_LORA, MLA_H, MLA_NOPE + MLA_ROPE)
    wqn = wqb[:, :, :MLA_NOPE].reshape(MLA_Q_LORA, MLA_H * MLA_NOPE).astype(BF16)
    wqr = jnp.pad(wqb[:, :, MLA_NOPE:], ((0, 0), (0, 0), (0, LANES - MLA_ROPE))
                  ).reshape(MLA_Q_LORA, MLA_H * LANES).astype(BF16)
    wuk = jnp.transpose(w["mla_w_uk"][mi], (1, 2, 0)).reshape(MLA_H // 2, 2, MLA_NOPE, MLA_KV_LORA)
    wuk_bd = jnp.einsum("pinc,ij->pinjc", wuk, jnp.eye(2, dtype=F32)).reshape(
        MLA_H // 2, 2 * MLA_NOPE, 2 * MLA_KV_LORA).astype(BF16)
    wuv = jnp.transpose(w["mla_w_uv"][mi], (1, 0, 2)).reshape(MLA_H // 2, 2, MLA_KV_LORA, MLA_V)
    wuv_bd = jnp.einsum("picv,ij->picjv", wuv, jnp.eye(2, dtype=F32)).reshape(
        MLA_H // 2, 2 * MLA_KV_LORA, 2 * MLA_V).astype(BF16)
    nwa = jnp.concatenate([nw[0:1], jnp.zeros((7, D_MODEL), F32)])
    nwb = jnp.concatenate([nw[1:2], jnp.zeros((7, D_MODEL), F32)])
    row = lambda g, j: (g * J + j, 0)
    wl = [winq, winc, wink, w["mla_q_norm"][mi][None, :], w["mla_kv_norm"][mi][None, :], wqn, wqr, wuk_bd]
    c, kr, kcat, qcat = pl.pallas_call(
        _mla_proj_body,
        grid=(G, J),
        in_specs=[pl.BlockSpec((R, D_MODEL), row), _const_spec((8, D_MODEL)),
                  pl.BlockSpec((3, R, LANES), lambda g, j: (0, j, 0))] + [_const_spec(a.shape) for a in wl],
        out_specs=[pl.BlockSpec((R, MLA_KV_LORA), row), pl.BlockSpec((R, LANES), row),
                   pl.BlockSpec((R, MLA_QK), row), pl.BlockSpec((MLA_H, R, MLA_QK), lambda g, j: (0, g * J + j, 0))],
        out_shape=[jax.ShapeDtypeStruct((n, MLA_KV_LORA), F32), jax.ShapeDtypeStruct((n, LANES), F32),
                   jax.ShapeDtypeStruct((n, MLA_QK), adt), jax.ShapeDtypeStruct((MLA_H, n, MLA_QK), adt)],
        compiler_params=_params(2),
        name="mla_proj",
    )(x2d, nwa, tab, *wl)

    if paged is None:
        nq = t // MLA_TQ
        rows = MLA_H * MLA_TQ
        o = pl.pallas_call(
            _mla_prompt_body,
            grid=(b, nq),
            in_specs=[pl.BlockSpec((MLA_H, MLA_TQ, MLA_QK), lambda bb, i: (0, bb * nq + i, 0)),
                      pl.BlockSpec((t, MLA_QK), lambda bb, i: (bb, 0))],
            out_specs=pl.BlockSpec((MLA_H, MLA_TQ, MLA_KV_LORA), lambda bb, i: (0, bb * nq + i, 0)),
            out_shape=jax.ShapeDtypeStruct((MLA_H, n, MLA_KV_LORA), BF16),
            scratch_shapes=[pltpu.VMEM((rows, LANES), F32), pltpu.VMEM((rows, LANES), F32),
                            pltpu.VMEM((rows, MLA_KV_LORA), F32)],
            compiler_params=_params(2),
            name="mla_attend_prompt",
        )(qcat, kcat)
    else:
        pages_c, pages_kr, page_table = paged
        page = pages_c.shape[1]
        npg = page_table.shape[1]
        assert npg % MLA_PP == 0
        rows = MLA_H * t

        def page_map(pp):
            return lambda bb, j, pt: (pt[bb, j * MLA_PP + pp], 0, 0)

        grid_spec = pltpu.PrefetchScalarGridSpec(
            num_scalar_prefetch=1,
            grid=(b, npg // MLA_PP),
            in_specs=[pl.BlockSpec((MLA_H, t, MLA_QK), lambda bb, j, pt: (0, bb, 0)),
                      pl.BlockSpec((t, MLA_QK), lambda bb, j, pt: (bb, 0))]
            + [pl.BlockSpec((1, page, MLA_KV_LORA), page_map(pp)) for pp in range(MLA_PP)]
            + [pl.BlockSpec((1, page, MLA_ROPE), page_map(pp)) for pp in range(MLA_PP)],
            out_specs=pl.BlockSpec((MLA_H, t, MLA_KV_LORA), lambda bb, j, pt: (0, bb, 0)),
            scratch_shapes=[pltpu.VMEM((rows, LANES), F32), pltpu.VMEM((rows, LANES), F32),
                            pltpu.VMEM((rows, MLA_KV_LORA), F32)],
        )
        o = pl.pallas_call(
            _mla_sample_body,
            grid_spec=grid_spec,
            out_shape=jax.ShapeDtypeStruct((MLA_H, n, MLA_KV_LORA), adt),
            compiler_params=_params(2),
            name="mla_attend_sample",
        )(page_table, qcat, kcat, *([pages_c] * MLA_PP), *([pages_kr] * MLA_PP))

    Ro = min(n, 512)
    x_new = pl.pallas_call(
        _mla_out_body,
        grid=(n // Ro,),
        in_specs=[pl.BlockSpec((Ro, D_MODEL), lambda i: (i, 0)),
                  pl.BlockSpec((MLA_H, Ro, MLA_KV_LORA), lambda i: (0, i, 0)),
                  _const_spec(wuv_bd.shape), _const_spec((MLA_H * MLA_V, D_MODEL)), _const_spec((8, D_MODEL))],
        out_specs=pl.BlockSpec((Ro, D_MODEL), lambda i: (i, 0)),
        out_shape=jax.ShapeDtypeStruct((n, D_MODEL), F32),
        compiler_params=_params(1),
        name="mla_out",
    )(x2d, o, wuv_bd, w["mla_wo"][mi].astype(BF16), nwb)
    return x_new, c.reshape(b, t, MLA_KV_LORA), kr[:, :MLA_ROPE].reshape(b, t, MLA_ROPE)


GDN_CW = 512


def _gdn_proj_body(x_ref, prev_ref, nw_ref, wqkv_ref, wz_ref, wbg_ref, cw_ref, gvec_ref, tri_ref,
                   q_ref, k_ref, v_ref, z_ref, beta_ref, gc_ref, st_ref, carry_ref):
    @pl.when(pl.program_id(1) == 0)
    def _():
        carry_ref[...] = prev_ref[...]

    rows = x_ref.shape[0]
    p = carry_ref.shape[0]
    h = _rms(x_ref[...], nw_ref[0:1, :]).astype(BF16)
    z_ref[...] = jnp.dot(h, wz_ref[...], preferred_element_type=F32)
    bg = jnp.dot(h, wbg_ref[...], preferred_element_type=F32)
    beta_ref[...] = _sigmoid(bg)
    g = -jnp.exp(gvec_ref[0:1, :]) * _softplus(bg + gvec_ref[1:2, :])
    gc_ref[...] = _chunk_cumsum(g, tri_ref[...])
    nch = GDN_CONV_DIM // GDN_CW

    def up(c):
        return jnp.dot(h, wqkv_ref[:, c * GDN_CW:(c + 1) * GDN_CW], preferred_element_type=F32)

    u_nxt = up(0)
    for c in range(nch):
        sl = slice(c * GDN_CW, (c + 1) * GDN_CW)
        u = u_nxt
        u_nxt = up(c + 1) if c + 1 < nch else None
        prev = carry_ref[:, sl]
        y = cw_ref[3:4, sl] * u
        for s in range(1, GDN_CONV):
            y = y + cw_ref[3 - s:4 - s, sl] * _shift_rows(u, prev, s)
        tail = u[rows - p:, :]
        carry_ref[:, sl] = tail
        st_ref[:, sl] = tail
        y = _silu(y)
        off = c * GDN_CW
        if off < 2 * GDN_QK_DIM:
            dst, base, scale = (q_ref, off, GDN_DK ** -0.5) if off < GDN_QK_DIM else (k_ref, off - GDN_QK_DIM, 1.0)
            for hh in range(GDN_CW // GDN_DK):
                yh = y[:, hh * GDN_DK:(hh + 1) * GDN_DK]
                yh = yh * lax.rsqrt(jnp.sum(yh * yh, axis=-1, keepdims=True) + 1e-6)
                dst[:, base + hh * GDN_DK:base + (hh + 1) * GDN_DK] = yh * scale if scale != 1.0 else yh
        else:
            v_ref[:, off - 2 * GDN_QK_DIM:off - 2 * GDN_QK_DIM + GDN_CW] = y


def _gdn_chunk_body(q_ref, k_ref, v_ref, z_ref, gc_ref, beta_ref, s0_ref, nw_ref,
                    o_ref, so_ref, *, nh, chunk):
    @pl.when(pl.program_id(1) == 0)
    def _():
        so_ref[...] = s0_ref[...]

    ng = GDN_V_H // nh
    gc = nh * chunk
    rep = GDN_V_H // GDN_QK_H
    ri = lax.broadcasted_iota(jnp.int32, (gc, gc), 0)
    ci = lax.broadcasted_iota(jnp.int32, (gc, gc), 1)
    same = (ri // chunk) == (ci // chunk)
    strict = same & ((ri % chunk) > (ci % chunk))
    incl = same & ((ri % chunk) >= (ci % chunk))
    last = same & ((ci % chunk) == chunk - 1)
    eye = (ri == ci).astype(F32)
    row_head = lax.broadcasted_iota(jnp.int32, (gc, GDN_DK), 0) // chunk

    def stack(ref, heads, width):
        parts = [ref[:, hd * width:(hd + 1) * width] for hd in heads]
        return parts[0] if len(parts) == 1 else jnp.concatenate(parts, axis=0)

    groups = range(ng)
    heads = [[q * nh + i for i in range(nh)] for q in groups]
    k_st = [stack(k_ref, [hd // rep for hd in heads[q]], GDN_DK) for q in groups]
    q_st = [stack(q_ref, [hd // rep for hd in heads[q]], GDN_DK) for q in groups]
    v_st = [stack(v_ref, heads[q], GDN_DV) for q in groups]
    def col(ref, lanes):
        parts = [ref[:, ln:ln + 1] for ln in lanes]
        return parts[0] if len(parts) == 1 else jnp.concatenate(parts, axis=0)

    gcol = [col(gc_ref, [GDN_V_H + hd for hd in heads[q]]) for q in groups]
    bcol = [col(beta_ref, heads[q]) for q in groups]
    grow = [jnp.sum(jnp.where(ri == ci, gcol[q], 0.0), axis=0, keepdims=True) for q in groups]
    k_b = [x.astype(BF16) for x in k_st]
    kq = [_bdot_nt(jnp.concatenate([k_b[q], q_st[q].astype(BF16)], axis=0), k_b[q]) for q in groups]
    decay = [jnp.exp(jnp.where(incl, gcol[q] - grow[q], -jnp.inf)) for q in groups]
    a = [jnp.where(strict, kq[q][:gc] * bcol[q] * decay[q], 0.0) for q in groups]
    aqk = [jnp.where(incl, kq[q][gc:] * decay[q], 0.0).astype(BF16) for q in groups]

    p = [(-m).astype(BF16) for m in a]
    x = [eye - m for m in a]
    span = 2
    if span < chunk:
        p = [_bdot(p[q], p[q]) for q in groups]
    while span < chunk:
        if span * 2 < chunk:
            px = [_bdot(p[q], jnp.concatenate([p[q].astype(BF16), x[q].astype(BF16)], axis=1)) for q in groups]
            p = [m[:, :gc] for m in px]
            x = [x[q] + px[q][:, gc:] for q in groups]
        else:
            x = [x[q] + _bdot(p[q], x[q]) for q in groups]
        span *= 2

    egc = [jnp.exp(g) for g in gcol]
    uw = [_bdot(x[q], jnp.concatenate([v_st[q] * bcol[q], k_st[q] * (bcol[q] * egc[q])], axis=1)) for q in groups]
    glast = [jnp.sum(jnp.where(last, grow[q], 0.0), axis=1, keepdims=True) for q in groups]
    kg = [k_st[q] * jnp.exp(glast[q] - gcol[q]) for q in groups]
    states = [[so_ref[0, hd] for hd in heads[q]] for q in groups]
    wq_s = []
    for q in groups:
        wm = uw[q][:, GDN_DV:]
        qg = q_st[q] * egc[q]
        parts = []
        for i in range(nh):
            rs = slice(i * chunk, (i + 1) * chunk)
            parts.append(_bdot(jnp.concatenate([wm[rs], qg[rs]], axis=0), states[q][i]))
        wq_s.append(parts)
    v_new, o_st = [], []
    for q in groups:
        ws = jnp.concatenate([m[:chunk] for m in wq_s[q]], axis=0) if nh > 1 else wq_s[q][0][:chunk]
        qs = jnp.concatenate([m[chunk:] for m in wq_s[q]], axis=0) if nh > 1 else wq_s[q][0][chunk:]
        vn = (uw[q][:, :GDN_DV] - ws).astype(BF16)
        v_new.append(vn)
        o_st.append(qs + _bdot(aqk[q], vn))
    for q in groups:
        for i, hd in enumerate(heads[q]):
            rs = slice(i * chunk, (i + 1) * chunk)
            gl_h = jnp.exp(glast[q][i * chunk:i * chunk + 1, :])
            if chunk % 16 == 0:
                upd = _bdot_tn(kg[q][rs], v_new[q][rs])
            else:
                upd = _bdot_tn(jnp.where(row_head == i, kg[q], 0.0), v_new[q])
            so_ref[0, hd] = states[q][i] * gl_h + upd
    for q in groups:
        z_st = stack(z_ref, heads[q], GDN_DV)
        og = _rms(o_st[q], nw_ref[0:1, :]) * _silu(z_st)
        for i, hd in enumerate(heads[q]):
            o_ref[:, hd * GDN_DV:(hd + 1) * GDN_DV] = og[i * chunk:(i + 1) * chunk]


def _gdn_layer(x2d, conv_prev, s0, w, gi, nw, b, t):
    n = b * t
    chunk = _chunk_of(t)
    tl = _tiling(b, t, 256)
    G, J, R, P = tl["G"], tl["J"], tl["R"], tl["P"]
    w_in = w["gdn_w_in"][gi]
    o1 = GDN_CONV_DIM
    o2 = o1 + GDN_V_DIM
    wqkv = w_in[:, :o1].astype(BF16)
    wz = w_in[:, o1:o2].astype(BF16)
    wbg = jnp.pad(w_in[:, o2:], ((0, 0), (0, LANES - 2 * GDN_V_H))).astype(BF16)
    cw = jnp.pad(w["gdn_conv_w"][gi], ((0, 8 - GDN_CONV), (0, 0)))
    gvec = jnp.zeros((8, LANES), F32)
    gvec = gvec.at[0, GDN_V_H:2 * GDN_V_H].set(w["gdn_a_log"][gi]).at[1, GDN_V_H:2 * GDN_V_H].set(w["gdn_dt_bias"][gi])
    bc = chunk if chunk == 64 else R
    tri = _chunk_masks(chunk, bc)
    nwa = jnp.concatenate([nw[0:1], jnp.zeros((7, D_MODEL), F32)])
    row = lambda g, j: (g * J + j, 0)
    st_spec = pl.BlockSpec((P, GDN_CONV_DIM), lambda g, j: (g, 0))
    qn, kn, v, z, beta, gcs, st = pl.pallas_call(
        _gdn_proj_body,
        grid=(G, J),
        in_specs=[pl.BlockSpec((R, D_MODEL), row), st_spec, _const_spec((8, D_MODEL)), _const_spec(wqkv.shape),
                  _const_spec(wz.shape), _const_spec(wbg.shape), _const_spec(cw.shape), _const_spec(gvec.shape),
                  _const_spec(tri.shape)],
        out_specs=[pl.BlockSpec((R, GDN_QK_DIM), row), pl.BlockSpec((R, GDN_QK_DIM), row),
                   pl.BlockSpec((R, GDN_V_DIM), row), pl.BlockSpec((R, GDN_V_DIM), row),
                   pl.BlockSpec((R, LANES), row), pl.BlockSpec((R, LANES), row), st_spec],
        out_shape=[jax.ShapeDtypeStruct((n, GDN_QK_DIM), F32), jax.ShapeDtypeStruct((n, GDN_QK_DIM), F32),
                   jax.ShapeDtypeStruct((n, GDN_V_DIM), F32), jax.ShapeDtypeStruct((n, GDN_V_DIM), F32),
                   jax.ShapeDtypeStruct((n, LANES), F32), jax.ShapeDtypeStruct((n, LANES), F32),
                   jax.ShapeDtypeStruct((b * SUBLANES, GDN_CONV_DIM), F32)],
        scratch_shapes=[pltpu.VMEM((P, GDN_CONV_DIM), F32)],
        compiler_params=_params(2),
        name="gdn_proj",
    )(x2d, _pad_state(conv_prev), nwa, wqkv, wz, wbg, cw, gvec, tri)
    conv_new = st.reshape(b, SUBLANES, GDN_CONV_DIM)[:, SUBLANES - (GDN_CONV - 1):]

    nh = GROUP_ROWS // chunk
    ng = GDN_V_H // nh
    nct = t // chunk
    crow = lambda bb, j: (bb * nct + j, 0)
    sspec = pl.BlockSpec((1, GDN_V_H, GDN_DK, GDN_DV), lambda bb, j: (bb, 0, 0, 0))
    nwn = jnp.concatenate([w["gdn_norm_w"][gi][None, :], jnp.zeros((7, GDN_DV), F32)])
    o, s_new = pl.pallas_call(
        functools.partial(_gdn_chunk_body, nh=nh, chunk=chunk),
        grid=(b, nct),
        in_specs=[pl.BlockSpec((chunk, GDN_QK_DIM), crow), pl.BlockSpec((chunk, GDN_QK_DIM), crow),
                  pl.BlockSpec((chunk, GDN_V_DIM), crow), pl.BlockSpec((chunk, GDN_V_DIM), crow),
                  pl.BlockSpec((chunk, LANES), crow), pl.BlockSpec((chunk, LANES), crow),
                  sspec, _const_spec((8, GDN_DV))],
        out_specs=[pl.BlockSpec((chunk, GDN_V_DIM), crow), sspec],
        out_shape=[jax.ShapeDtypeStruct((n, GDN_V_DIM), F32),
                   jax.ShapeDtypeStruct((b, GDN_V_H, GDN_DK, GDN_DV), F32)],
        compiler_params=_params(2),
        name="gdn_chunk",
    )(qn, kn, v, z, gcs, beta, s0, nwn)
    nwb = jnp.concatenate([nw[1:2], jnp.zeros((7, D_MODEL), F32)])
    x_new = _outproj(x2d, o, None, w["gdn_wo"][gi].astype(BF16), nwb, "gdn_out")
    return x_new, conv_new, s_new


def _trunk(x, pos, rw_s, rw_shift, gdn_s, gdn_conv, ffn_conv, w, paged):
    b, t, _ = x.shape
    x2d = x.reshape(b * t, D_MODEL)
    new = {k: [] for k in ("rw_S", "rw_shift", "mla_c", "mla_kr", "gdn_S", "gdn_conv", "ffn_conv")}
    v_first = None
    ri = mi = gi = 0
    for l, kind in enumerate(LAYER_MIXER):
        nw = w["norm_w"][l]
        if kind == 0:
            x2d, sh, s_new, v_first = _rwkv_layer(x2d, rw_shift[ri], rw_s[ri], v_first, w, ri, nw, b, t)
            new["rw_S"].append(s_new)
            new["rw_shift"].append(sh)
            ri += 1
        elif kind == 1:
            x2d, c, kr = _mla_layer(x2d, pos, w, mi, nw, b, t, None if paged is None else
                                    (paged[0][mi], paged[1][mi], paged[2]))
            new["mla_c"].append(c)
            new["mla_kr"].append(kr)
            mi += 1
        else:
            x2d, cb, s_new = _gdn_layer(x2d, gdn_conv[gi], gdn_s[gi], w, gi, nw, b, t)
            new["gdn_S"].append(s_new)
            new["gdn_conv"].append(cb)
            gi += 1
        nwf = jnp.concatenate([nw[2:4], jnp.zeros((6, D_MODEL), F32)])
        cwb = jnp.concatenate([w["ffn_conv_w"][l], w["ffn_conv_b"][l][None, :],
                               jnp.zeros((8 - FFN_CONV - 1, 2 * D_FF), F32)])
        x2d, st = _ffn(x2d, _pad_state(ffn_conv[l]), nwf, w["ffn_w_up"][l].astype(BF16), cwb,
                       w["ffn_w_down"][l].astype(BF16), b, t)
        new["ffn_conv"].append(st.reshape(b, SUBLANES, 2 * D_FF)[:, SUBLANES - (FFN_CONV - 1):])
    return x2d.reshape(b, t, D_MODEL), {k: jnp.stack(v) for k, v in new.items()}


def kernel(x_prompt, x_sample, state_rwkv_wkv, state_rwkv_shift, cache_mla_latent, cache_mla_krope, state_gdn_S, state_gdn_conv, state_ffn_conv, page_table, norm_w, rw_mu, rw_wrkv, rw_w0, rw_w1, rw_w2, rw_a0, rw_a1, rw_a2, rw_v0, rw_v1, rw_v2, rw_g1, rw_g2, rw_kk, rw_ka, rw_rk, rw_lnx_w, rw_lnx_b, rw_wo, mla_w_in, mla_q_norm, mla_kv_norm, mla_w_qb, mla_w_uk, mla_w_uv, mla_wo, gdn_w_in, gdn_conv_w, gdn_a_log, gdn_dt_bias, gdn_norm_w, gdn_wo, ffn_w_up, ffn_conv_w, ffn_conv_b, ffn_w_down):
    w = dict(norm_w=norm_w, rw_mu=rw_mu, rw_wrkv=rw_wrkv, rw_w0=rw_w0, rw_w1=rw_w1, rw_w2=rw_w2, rw_a0=rw_a0,
             rw_a1=rw_a1, rw_a2=rw_a2, rw_v0=rw_v0, rw_v1=rw_v1, rw_v2=rw_v2, rw_g1=rw_g1, rw_g2=rw_g2,
             rw_kk=rw_kk, rw_ka=rw_ka, rw_rk=rw_rk, rw_lnx_w=rw_lnx_w, rw_lnx_b=rw_lnx_b, rw_wo=rw_wo,
             mla_w_in=mla_w_in, mla_q_norm=mla_q_norm, mla_kv_norm=mla_kv_norm, mla_w_qb=mla_w_qb,
             mla_w_uk=mla_w_uk, mla_w_uv=mla_w_uv, mla_wo=mla_wo, gdn_w_in=gdn_w_in, gdn_conv_w=gdn_conv_w,
             gdn_a_log=gdn_a_log, gdn_dt_bias=gdn_dt_bias, gdn_norm_w=gdn_norm_w, gdn_wo=gdn_wo,
             ffn_w_up=ffn_w_up, ffn_conv_w=ffn_conv_w, ffn_conv_b=ffn_conv_b, ffn_w_down=ffn_w_down)
    b, t = x_prompt.shape[0], x_prompt.shape[1]
    n_rw, n_gdn, depth = state_rwkv_wkv.shape[0], state_gdn_S.shape[0], state_ffn_conv.shape[0]
    y_p, sp = _trunk(
        x_prompt, jnp.arange(t),
        jnp.zeros((n_rw, b) + state_rwkv_wkv.shape[2:], F32), jnp.zeros((n_rw, b, D_MODEL), F32),
        jnp.zeros((n_gdn, b) + state_gdn_S.shape[2:], F32), jnp.zeros((n_gdn, b) + state_gdn_conv.shape[2:], F32),
        jnp.zeros((depth, b) + state_ffn_conv.shape[2:], F32), w, None)
    past_len = page_table.shape[1] * cache_mla_latent.shape[2]
    pos_s = past_len + jnp.arange(x_sample.shape[1])
    y_s, ss = _trunk(x_sample, pos_s, state_rwkv_wkv, state_rwkv_shift, state_gdn_S, state_gdn_conv,
                     state_ffn_conv, w, (cache_mla_latent, cache_mla_krope, page_table))
    names = ("rw_S", "rw_shift", "mla_c", "mla_kr", "gdn_S", "gdn_conv", "ffn_conv")
    return (y_p, y_s) + tuple(sp[k] for k in names) + tuple(ss[k] for k in names)
```

```python
import functools

import jax
import jax.numpy as jnp
from jax import lax
from jax.experimental import pallas as pl
from jax.experimental.pallas import tpu as pltpu

F32 = jnp.float32
BF16 = jnp.bfloat16
HIGHEST = lax.Precision.HIGHEST

D_MODEL = 1024
NORM_EPS = 1e-6
RW_N = 64
RW_H = D_MODEL // RW_N
RW_LNX_EPS = 64e-5
MLA_H = 16
MLA_NOPE = 64
MLA_ROPE = 32
MLA_V = 64
MLA_Q_LORA = 512
MLA_KV_LORA = 256
MLA_SCALE = (MLA_NOPE + MLA_ROPE) ** -0.5
ROPE_THETA = 10000.0
MLA_QK = MLA_KV_LORA + 128
GDN_QK_H = 8
GDN_V_H = 16
GDN_DK = 128
GDN_DV = 128
GDN_QK_DIM = GDN_QK_H * GDN_DK
GDN_V_DIM = GDN_V_H * GDN_DV
GDN_CONV_DIM = 2 * GDN_QK_DIM + GDN_V_DIM
GDN_CONV = 4
D_FF = 2816
FFN_CONV = 3
LAYER_MIXER = (0, 1, 2, 0)

SUBLANES = 8
LANES = 128
GROUP_ROWS = 128
VMEM_LIMIT = 56 * 1024 * 1024


def _rms(x, w):
    return x * lax.rsqrt(jnp.mean(x * x, axis=-1, keepdims=True) + NORM_EPS) * w


def _bdot(a, b):
    return jnp.dot(a.astype(BF16), b.astype(BF16), preferred_element_type=F32)


def _bdot_nt(a, b):
    return lax.dot_general(a.astype(BF16), b.astype(BF16), (((1,), (1,)), ((), ())),
                           preferred_element_type=F32)


def _bdot_tn(a, b):
    return lax.dot_general(a.astype(BF16), b.astype(BF16), (((0,), (0,)), ((), ())),
                           preferred_element_type=F32)


def _hdot(a, b):
    return jnp.dot(a, b, precision=HIGHEST, preferred_element_type=F32)


def _sigmoid(x):
    return 1.0 / (1.0 + jnp.exp(-x))


def _softplus(x):
    return jnp.maximum(x, 0.0) + jnp.log(1.0 + jnp.exp(-jnp.abs(x)))


def _silu(x):
    return x * _sigmoid(x)


def _shift_rows(u, prev, s):
    rows, cols = u.shape
    p = prev.shape[0]
    rolled = pltpu.roll(u, s, 0)
    fix = pltpu.roll(prev, (p - SUBLANES + s) % p, 0)
    t = lax.broadcasted_iota(jnp.int32, (p, cols), 0) % SUBLANES
    if p == rows:
        return jnp.where(t < s, fix, rolled)
    head = jnp.where(t < s, fix, rolled[:SUBLANES])
    return jnp.concatenate([head, rolled[SUBLANES:]], axis=0)


def _lane_group_sum(x, ones2):
    parts = []
    for i in range(x.shape[1] // LANES):
        xs = x[:, i * LANES:(i + 1) * LANES]
        hi = xs.astype(BF16)
        lo = (xs - hi.astype(F32)).astype(BF16)
        parts.append(jnp.dot(jnp.concatenate([hi, lo], axis=1), ones2, preferred_element_type=F32))
    return parts[0] if len(parts) == 1 else jnp.concatenate(parts, axis=1)


def _chunk_cumsum(x, tri):
    bc = tri.shape[0]
    parts = [_hdot(tri, x[i * bc:(i + 1) * bc]) for i in range(x.shape[0] // bc)]
    return parts[0] if len(parts) == 1 else jnp.concatenate(parts, axis=0)


def _tiling(b, t, tt_max):
    if t == SUBLANES:
        return dict(G=1, J=1, R=b * t, P=b * t)
    tt = min(t, tt_max)
    assert t % tt == 0 and tt % 64 == 0, (t, tt)
    return dict(G=b, J=t // tt, R=tt, P=SUBLANES)


def _chunk_of(t):
    return 64 if t % 64 == 0 else t


def _const_spec(shape):
    nd = len(shape)
    return pl.BlockSpec(shape, lambda *_: (0,) * nd, pipeline_mode=pl.Buffered(1))


def _params(n_axes):
    return pltpu.CompilerParams(dimension_semantics=("arbitrary",) * n_axes,
                                vmem_limit_bytes=VMEM_LIMIT)


def _pad_state(st):
    b, k1, c = st.shape
    return jnp.pad(st, ((0, 0), (SUBLANES - k1, 0), (0, 0))).reshape(b * SUBLANES, c)


def _chunk_masks(chunk, rows):
    i = jnp.arange(rows)
    same = (i[:, None] // chunk) == (i[None, :] // chunk)
    tri = same & ((i[None, :] % chunk) <= (i[:, None] % chunk))
    return tri.astype(F32)


FFN_CW = 256


ROW_BLOCK = 64


def _stage(buf_ref, u, carry_ref, st_ref, sl):
    rows = u.shape[0]
    buf_ref[0:SUBLANES, :] = carry_ref[:, sl]
    buf_ref[SUBLANES:SUBLANES + rows, :] = u
    tail = u[rows - SUBLANES:, :]
    carry_ref[:, sl] = tail
    st_ref[:, sl] = tail


def _taps(buf_ref, r0, nrows, shifts):
    slab = buf_ref[r0:r0 + nrows + SUBLANES, :]
    return slab[SUBLANES:], [pltpu.roll(slab, s, 0)[SUBLANES:] for s in shifts]


def _ffn_body(x_ref, prev_ref, nw_ref, wup_ref, cwb_ref, wdn_ref, xo_ref, st_ref, carry_ref, act_ref, buf_ref):
    @pl.when(pl.program_id(1) == 0)
    def _():
        carry_ref[...] = prev_ref[...]

    x = x_ref[...]
    rows = x.shape[0]
    stacked = carry_ref.shape[0] == rows
    h = _rms(x, nw_ref[0:1, :]).astype(BF16)
    nch = D_FF // FFN_CW

    def cols(c, half):
        return slice(half * D_FF + c * FFN_CW, half * D_FF + (c + 1) * FFN_CW)

    def conv_gate(c, taps):
        ys = []
        for half in range(2):
            sl = cols(c, half)
            u, (u2, u1) = taps[half]
            ys.append(cwb_ref[0:1, sl] * u2 + cwb_ref[1:2, sl] * u1 + cwb_ref[2:3, sl] * u + cwb_ref[3:4, sl])
        return (_silu(ys[0]) * ys[1]).astype(BF16)

    def up(c):
        us = [jnp.dot(h, wup_ref[:, cols(c, half)], preferred_element_type=F32) for half in range(2)]
        if stacked:
            return us
        for half in range(2):
            _stage(buf_ref.at[(c % 2) * 2 + half], us[half], carry_ref, st_ref, cols(c, half))
        return None

    u_cur = up(0)
    for c in range(nch):
        u_nxt = up(c + 1) if c + 1 < nch else None
        csl = slice(c * FFN_CW, (c + 1) * FFN_CW)
        if stacked:
            taps = []
            for half in range(2):
                sl = cols(c, half)
                u = u_cur[half]
                prev = carry_ref[:, sl]
                taps.append((u, [_shift_rows(u, prev, 2), _shift_rows(u, prev, 1)]))
                carry_ref[:, sl] = u
                st_ref[:, sl] = u
            act_ref[:, csl] = conv_gate(c, taps)
        else:
            for r0 in range(0, rows, ROW_BLOCK):
                taps = [_taps(buf_ref.at[(c % 2) * 2 + half], r0, ROW_BLOCK, (2, 1)) for half in range(2)]
                act_ref[r0:r0 + ROW_BLOCK, csl] = conv_gate(c, taps)
        u_cur = u_nxt
    f = jnp.dot(act_ref[...], wdn_ref[...], preferred_element_type=F32)
    xo_ref[...] = x + _rms(f, nw_ref[1:2, :])


def _ffn(x2d, prev, nw, wup, cwb, wdn, b, t):
    tl = _tiling(b, t, 512)
    G, J, R, P = tl["G"], tl["J"], tl["R"], tl["P"]
    n = b * t
    return pl.pallas_call(
        _ffn_body,
        grid=(G, J),
        in_specs=[
            pl.BlockSpec((R, D_MODEL), lambda g, j: (g * J + j, 0)),
            pl.BlockSpec((P, 2 * D_FF), lambda g, j: (g, 0)),
            _const_spec((8, D_MODEL)),
            _const_spec((D_MODEL, 2 * D_FF)),
            _const_spec((8, 2 * D_FF)),
            _const_spec((D_FF, D_MODEL)),
        ],
        out_specs=[
            pl.BlockSpec((R, D_MODEL), lambda g, j: (g * J + j, 0)),
            pl.BlockSpec((P, 2 * D_FF), lambda g, j: (g, 0)),
        ],
        out_shape=[jax.ShapeDtypeStruct((n, D_MODEL), F32),
                   jax.ShapeDtypeStruct((b * SUBLANES, 2 * D_FF), F32)],
        scratch_shapes=[pltpu.VMEM((P, 2 * D_FF), F32), pltpu.VMEM((R, D_FF), BF16),
                        pltpu.VMEM((4, R + SUBLANES, FFN_CW), F32)],
        compiler_params=_params(2),
        name="conv_ffn",
    )(x2d, prev, nw, wup, cwb, wdn)


def _outproj_body(*refs, gated):
    x_ref, y_ref = refs[0], refs[1]
    wo_ref, nw_ref, xo_ref = refs[-3:]
    y = y_ref[...]
    if gated:
        y = y * refs[2][...]
    o = jnp.dot(y.astype(BF16), wo_ref[...], preferred_element_type=F32)
    xo_ref[...] = x_ref[...] + _rms(o, nw_ref[0:1, :])


def _outproj(x2d, y2d, gate2d, wo, nw, name):
    n, k = y2d.shape
    R = min(n, 512)
    row = lambda i: (i, 0)
    acts = [y2d] if gate2d is None else [y2d, gate2d]
    return pl.pallas_call(
        functools.partial(_outproj_body, gated=gate2d is not None),
        grid=(n // R,),
        in_specs=[pl.BlockSpec((R, D_MODEL), row)] + [pl.BlockSpec((R, k), row)] * len(acts)
        + [_const_spec((k, D_MODEL)), _const_spec((8, D_MODEL))],
        out_specs=pl.BlockSpec((R, D_MODEL), row),
        out_shape=jax.ShapeDtypeStruct((n, D_MODEL), F32),
        compiler_params=_params(1),
        name=name,
    )(x2d, *acts, wo, nw)


def _rwkv_proj_body(*refs, has_vres, chunk):
    it = iter(refs)
    x_ref, prev_ref = next(it), next(it)
    vf_ref = next(it) if has_vres else None
    vec_ref, wrkv_ref, w1_ref, w2_ref, a1_ref, a2_ref = (next(it) for _ in range(6))
    v1_ref, v2_ref = (next(it), next(it)) if has_vres else (None, None)
    g1_ref, g2_ref, tri_ref, ones_ref = (next(it) for _ in range(4))
    rt_ref, kt_ref, at_ref, bt_ref, v_ref, g_ref, gl_ref, hl_ref, carry_ref = (next(it) for _ in range(9))

    @pl.when(pl.program_id(1) == 0)
    def _():
        carry_ref[...] = prev_ref[...]

    x = x_ref[...]
    rows = x.shape[0]
    p = carry_ref.shape[0]
    h = _rms(x, vec_ref[10:11, :])
    d = _shift_rows(h, carry_ref[...], 1) - h
    tail = h[rows - p:, :]
    carry_ref[...] = tail
    hl_ref[...] = tail

    def mix(i):
        return (h + d * vec_ref[i:i + 1, :]).astype(BF16)

    r = jnp.dot(mix(0), wrkv_ref[0], preferred_element_type=F32)
    k = jnp.dot(mix(1), wrkv_ref[1], preferred_element_type=F32)
    xv = mix(2)
    v = jnp.dot(xv, wrkv_ref[2], preferred_element_type=F32)
    w_raw = vec_ref[6:7, :] + _bdot(jnp.tanh(_bdot(mix(3), w1_ref[...])), w2_ref[...])
    w = -_softplus(-w_raw) - 0.5
    if has_vres:
        gate_v = _sigmoid(vec_ref[11:12, :] + _bdot(_bdot(xv, v1_ref[...]), v2_ref[...]))
        v = v + (vf_ref[...] - v) * gate_v
    a = _sigmoid(vec_ref[7:8, :] + _bdot(_bdot(mix(4), a1_ref[...]), a2_ref[...]))
    g_ref[...] = _bdot(_sigmoid(_bdot(mix(5), g1_ref[...])), g2_ref[...])
    v_ref[...] = v

    kk = k * vec_ref[8:9, :]
    kk = kk * lax.rsqrt(_lane_group_sum(kk * kk, ones_ref[...]) + 1e-6)
    k = k * (1.0 + (a - 1.0) * vec_ref[9:10, :])

    lw = -jnp.exp(w)
    cum = _chunk_cumsum(lw, tri_ref[...])
    e_fwd = jnp.exp(cum)
    e_bwd = jnp.exp(-cum)
    rt_ref[...] = r * e_fwd
    kt_ref[...] = k * e_bwd
    at_ref[...] = -kk * jnp.exp(cum - lw)
    bt_ref[...] = kk * a * e_bwd
    for c in range(rows // chunk):
        gl_ref[c] = jnp.exp(cum[(c + 1) * chunk - 1:(c + 1) * chunk, :])


def _rwkv_scan_body(rt_ref, kt_ref, at_ref, bt_ref, v_ref, gl_ref, h0_ref, vec_ref, ones_ref,
                    y_ref, ho_ref, *, nh, chunk):
    @pl.when(pl.program_id(1) == 0)
    def _():
        ho_ref[...] = h0_ref[...]

    gl_lanes = nh * RW_N
    ng = RW_H // nh
    gc = nh * chunk
    row_head = lax.broadcasted_iota(jnp.int32, (gc, gl_lanes), 0) // chunk
    lane_head = lax.broadcasted_iota(jnp.int32, (gc, gl_lanes), 1) // RW_N
    own = row_head == lane_head
    ri = lax.broadcasted_iota(jnp.int32, (gc, gc), 0)
    ci = lax.broadcasted_iota(jnp.int32, (gc, gc), 1)
    same = (ri // chunk) == (ci // chunk)
    strict = same & ((ri % chunk) > (ci % chunk))
    incl = same & ((ri % chunk) >= (ci % chunk))
    eye = (ri == ci).astype(F32)
    eye_l = (lax.broadcasted_iota(jnp.int32, (gl_lanes, gl_lanes), 0)
             == lax.broadcasted_iota(jnp.int32, (gl_lanes, gl_lanes), 1))
    ones_bd = ones_ref[...]

    def blockdiag(xg):
        xx = jnp.concatenate([xg] * nh, axis=0) if nh > 1 else xg
        return jnp.where(own, xx, 0.0)

    groups = range(ng)
    sls = [slice(q * gl_lanes, (q + 1) * gl_lanes) for q in groups]
    merged = gc == GROUP_ROWS
    gl_rows = [gl_ref[0, :, sl] for sl in sls]
    r_bd = [blockdiag(rt_ref[:, sl]) for sl in sls]
    k_bd = [blockdiag(kt_ref[:, sl]) for sl in sls]
    a_bd = [blockdiag(at_ref[:, sl]) for sl in sls]
    b_bd = [blockdiag(bt_ref[:, sl]) for sl in sls]
    v_bd = [blockdiag(v_ref[:, sl]).astype(BF16) for sl in sls]
    hs = [ho_ref[0, q] for q in groups]
    hs_b = [h.astype(BF16) for h in hs]
    if merged:
        ar = [jnp.concatenate([a_bd[q], r_bd[q]], axis=0).astype(BF16) for q in groups]
        bk = [jnp.concatenate([b_bd[q], k_bd[q]], axis=0).astype(BF16) for q in groups]
        amat = [_bdot_nt(ar[q], bk[q]) for q in groups]
        a_ab = [jnp.where(strict, m[:gc, :gc], 0.0) for m in amat]
        a_ak = [jnp.where(strict, m[:gc, gc:], 0.0).astype(BF16) for m in amat]
        a_rbk = [jnp.concatenate([jnp.where(incl, m[gc:, :gc], 0.0), jnp.where(incl, m[gc:, gc:], 0.0)],
                                 axis=1).astype(BF16) for m in amat]
        arh = [_bdot(ar[q], hs_b[q]) for q in groups]
        ah = [m[:gc] for m in arh]
        rh = [m[gc:] for m in arh]
    else:
        ab_, rb_ = [x.astype(BF16) for x in a_bd], [x.astype(BF16) for x in r_bd]
        bb_, kb_ = [x.astype(BF16) for x in b_bd], [x.astype(BF16) for x in k_bd]
        a_ab = [jnp.where(strict, _bdot_nt(ab_[q], bb_[q]), 0.0) for q in groups]
        a_ak = [jnp.where(strict, _bdot_nt(ab_[q], kb_[q]), 0.0).astype(BF16) for q in groups]
        a_rb = [jnp.where(incl, _bdot_nt(rb_[q], bb_[q]), 0.0).astype(BF16) for q in groups]
        a_rk = [jnp.where(incl, _bdot_nt(rb_[q], kb_[q]), 0.0).astype(BF16) for q in groups]
        ah = [_bdot(ab_[q], hs_b[q]) for q in groups]
        rh = [_bdot(rb_[q], hs_b[q]) for q in groups]
    akv = [_bdot(a_ak[q], v_bd[q]) for q in groups]

    p = [m.astype(BF16) for m in a_ab]
    x = [eye + m for m in a_ab]
    span = 2
    if span < chunk:
        p = [_bdot(p[q], p[q]) for q in groups]
    while span < chunk:
        last = span * 2 >= chunk
        if merged and not last:
            px = [_bdot(p[q], jnp.concatenate([p[q].astype(BF16), x[q].astype(BF16)], axis=1)) for q in groups]
            p = [m[:, :gc] for m in px]
            x = [x[q] + px[q][:, gc:] for q in groups]
        else:
            pb = [m.astype(BF16) for m in p]
            x = [x[q] + _bdot(pb[q], x[q]) for q in groups]
            if not last:
                p = [_bdot(pb[q], pb[q]) for q in groups]
        span *= 2

    u = [_bdot(x[q], ah[q] + akv[q]).astype(BF16) for q in groups]
    if merged:
        uv = [jnp.concatenate([u[q], v_bd[q]], axis=0) for q in groups]
        y_bd = [rh[q] + _bdot(a_rbk[q], uv[q]) for q in groups]
        for q in groups:
            gl_col = jnp.sum(jnp.where(eye_l, gl_rows[q], 0.0), axis=1, keepdims=True)
            bk_g = jnp.concatenate([b_bd[q] * gl_rows[q], k_bd[q] * gl_rows[q]], axis=0)
            ho_ref[0, q] = hs[q] * gl_col + _bdot_tn(bk_g, uv[q])
    else:
        y_bd = [rh[q] + _bdot(a_rb[q], u[q]) + _bdot(a_rk[q], v_bd[q]) for q in groups]
        for q in groups:
            gl_col = jnp.sum(jnp.where(eye_l, gl_rows[q], 0.0), axis=1, keepdims=True)
            ho_ref[0, q] = (hs[q] * gl_col + _bdot_tn(b_bd[q] * gl_rows[q], u[q])
                            + _bdot_tn(k_bd[q] * gl_rows[q], v_bd[q]))

    for q in groups:
        sl = sls[q]
        y = y_bd[q][0:chunk]
        for hh in range(1, nh):
            y = y + y_bd[q][hh * chunk:(hh + 1) * chunk]
        mu = _lane_group_sum(y, ones_bd) * (1.0 / RW_N)
        yc = y - mu
        var = _lane_group_sum(yc * yc, ones_bd) * (1.0 / RW_N)
        yn = yc * lax.rsqrt(var + RW_LNX_EPS) * vec_ref[0:1, sl] + vec_ref[1:2, sl]
        bonus = _lane_group_sum(rt_ref[:, sl] * kt_ref[:, sl] * vec_ref[2:3, sl], ones_bd) * v_ref[:, sl]
        y_ref[:, sl] = yn + bonus


def _rwkv_layer(x2d, shift_prev, s0, v_first, w, ri, nw, b, t):
    n = b * t
    chunk = _chunk_of(t)
    tl = _tiling(b, t, 256)
    G, J, R, P = tl["G"], tl["J"], tl["R"], tl["P"]
    has_vres = v_first is not None
    vi = ri - 1
    bc = chunk if chunk == 64 else R
    tri = _chunk_masks(chunk, bc)
    li = jnp.arange(LANES)
    ones_bd = ((li[:, None] // RW_N) == (li[None, :] // RW_N)).astype(BF16)
    ones_bd = jnp.concatenate([ones_bd, ones_bd], axis=0)
    zero = jnp.zeros((D_MODEL,), F32)
    vec = jnp.stack([*(w["rw_mu"][ri][i] for i in range(6)), w["rw_w0"][ri], w["rw_a0"][ri], w["rw_kk"][ri],
                     w["rw_ka"][ri], nw[0], w["rw_v0"][vi] if has_vres else zero, zero, zero, zero, zero])
    row = lambda g, j: (g * J + j, 0)
    row_spec = pl.BlockSpec((R, D_MODEL), row)
    ins = [x2d, _pad_state(shift_prev[:, None, :])]
    specs = [row_spec, pl.BlockSpec((P, D_MODEL), lambda g, j: (g, 0))]
    if has_vres:
        ins.append(v_first)
        specs.append(row_spec)
    wl = [vec, w["rw_wrkv"][ri].astype(BF16), w["rw_w1"][ri].astype(BF16), w["rw_w2"][ri].astype(BF16),
          w["rw_a1"][ri].astype(BF16), w["rw_a2"][ri].astype(BF16)]
    if has_vres:
        wl += [w["rw_v1"][vi].astype(BF16), w["rw_v2"][vi].astype(BF16)]
    wl += [w["rw_g1"][ri].astype(BF16), w["rw_g2"][ri].astype(BF16), tri, ones_bd]
    ins += wl
    specs += [_const_spec(a.shape) for a in wl]
    nc_tile = R // chunk
    outs = pl.pallas_call(
        functools.partial(_rwkv_proj_body, has_vres=has_vres, chunk=chunk),
        grid=(G, J),
        in_specs=specs,
        out_specs=[row_spec] * 6 + [pl.BlockSpec((nc_tile, 1, D_MODEL), lambda g, j: (g * J + j, 0, 0)),
                                    pl.BlockSpec((P, D_MODEL), lambda g, j: (g, 0))],
        out_shape=[jax.ShapeDtypeStruct((n, D_MODEL), F32)] * 6
        + [jax.ShapeDtypeStruct((n // chunk, 1, D_MODEL), F32), jax.ShapeDtypeStruct((b * SUBLANES, D_MODEL), F32)],
        scratch_shapes=[pltpu.VMEM((P, D_MODEL), F32)],
        compiler_params=_params(2),
        name="rwkv_proj",
    )(*ins)
    rt, kt, at, bt, v, g, gl, hl = outs
    shift_new = hl.reshape(b, SUBLANES, D_MODEL)[:, -1]

    nh = LANES // RW_N
    ng = RW_H // nh
    gl_lanes = nh * RW_N
    hkv = jnp.swapaxes(s0, -1, -2).reshape(b, ng, nh, RW_N, RW_N)
    zblk = jnp.zeros((b, ng, RW_N, RW_N), F32)
    h0 = jnp.concatenate(
        [jnp.concatenate([hkv[:, :, i] if i == jj else zblk for jj in range(nh)], axis=-1) for i in range(nh)],
        axis=-2)
    svec = jnp.stack([w["rw_lnx_w"][ri], w["rw_lnx_b"][ri], w["rw_rk"][ri].reshape(D_MODEL),
                      zero, zero, zero, zero, zero])
    nct = t // chunk
    crow = lambda bb, j: (bb * nct + j, 0)
    cspec = pl.BlockSpec((chunk, D_MODEL), crow)
    hspec = pl.BlockSpec((1, ng, gl_lanes, gl_lanes), lambda bb, j: (bb, 0, 0, 0))
    y, hout = pl.pallas_call(
        functools.partial(_rwkv_scan_body, nh=nh, chunk=chunk),
        grid=(b, nct),
        in_specs=[cspec] * 5 + [pl.BlockSpec((1, 1, D_MODEL), lambda bb, j: (bb * nct + j, 0, 0)), hspec,
                                _const_spec((8, D_MODEL)), _const_spec((2 * LANES, LANES))],
        out_specs=[cspec, hspec],
        out_shape=[jax.ShapeDtypeStruct((n, D_MODEL), F32),
                   jax.ShapeDtypeStruct((b, ng, gl_lanes, gl_lanes), F32)],
        compiler_params=_params(2),
        name="rwkv_scan",
    )(rt, kt, at, bt, v, gl, h0, svec, ones_bd)
    s_new = jnp.stack([hout[:, :, i * RW_N:(i + 1) * RW_N, i * RW_N:(i + 1) * RW_N] for i in range(nh)],
                      axis=2)
    s_new = jnp.swapaxes(s_new, -1, -2).reshape(b, RW_H, RW_N, RW_N)
    nwp = jnp.concatenate([nw[1:2], jnp.zeros((7, D_MODEL), F32)])
    x_new = _outproj(x2d, y, g, w["rw_wo"][ri].astype(BF16), nwp, "rwkv_out")
    return x_new, shift_new, s_new, (v if not has_vres else v_first)


def _rope_lanes(x, tab_ref):
    half = MLA_ROPE // 2
    return (x * tab_ref[0] + pltpu.roll(x, LANES - half, 1) * tab_ref[1] + pltpu.roll(x, half, 1) * tab_ref[2])


def _mla_proj_body(x_ref, nw_ref, tab_ref, winq_ref, winc_ref, wink_ref, qn_ref, kvn_ref, wqn_ref, wqr_ref,
                   wuk_ref, c_ref, kr_ref, kcat_ref, qcat_ref):
    h = _rms(x_ref[...], nw_ref[0:1, :]).astype(BF16)
    cq = _rms(jnp.dot(h, winq_ref[...], preferred_element_type=F32), qn_ref[...]).astype(BF16)
    c = _rms(jnp.dot(h, winc_ref[...], preferred_element_type=F32), kvn_ref[...])
    kr = _rope_lanes(jnp.dot(h, wink_ref[...], preferred_element_type=F32), tab_ref)
    c_ref[...] = c
    kr_ref[...] = kr
    adt = kcat_ref.dtype
    kcat_ref[:, 0:MLA_KV_LORA] = c.astype(adt)
    kcat_ref[:, MLA_KV_LORA:MLA_QK] = kr.astype(adt)
    qn = jnp.dot(cq, wqn_ref[...], preferred_element_type=F32).astype(BF16)
    qr = jnp.dot(cq, wqr_ref[...], preferred_element_type=F32)
    for pr in range(MLA_H // 2):
        ql = jnp.dot(qn[:, pr * LANES:(pr + 1) * LANES], wuk_ref[pr], preferred_element_type=F32) * MLA_SCALE
        qcat_ref[2 * pr, :, 0:MLA_KV_LORA] = ql[:, :MLA_KV_LORA].astype(adt)
        qcat_ref[2 * pr + 1, :, 0:MLA_KV_LORA] = ql[:, MLA_KV_LORA:].astype(adt)
    for hh in range(MLA_H):
        qro = _rope_lanes(qr[:, hh * LANES:(hh + 1) * LANES], tab_ref) * MLA_SCALE
        qcat_ref[hh, :, MLA_KV_LORA:MLA_QK] = qro.astype(adt)


MLA_TQ = 128
MLA_TK = 256
MLA_SPLIT = 8


def _mla_prompt_body(q_ref, k_ref, o_ref, m_ref, l_ref, acc_ref):
    i = pl.program_id(1)
    rows = MLA_H * MLA_TQ
    q = q_ref[...].reshape(rows, MLA_QK)
    m_ref[...] = jnp.full((rows, LANES), -jnp.inf, F32)
    l_ref[...] = jnp.zeros((rows, LANES), F32)
    acc_ref[...] = jnp.zeros((rows, MLA_KV_LORA), F32)
    ones = jnp.ones((MLA_TK, LANES), BF16)
    reps = MLA_TK // LANES

    sub = rows // MLA_SPLIT

    def block(k0, masked):
        kblk = k_ref[pl.ds(k0, MLA_TK), :]
        ss = [lax.dot_general(q[g * sub:(g + 1) * sub], kblk, (((1,), (1,)), ((), ())),
                              preferred_element_type=F32) for g in range(MLA_SPLIT)]
        for g in range(MLA_SPLIT):
            rs = slice(g * sub, (g + 1) * sub)
            s = ss[g]
            if masked:
                qpos = i * MLA_TQ + (g * sub + lax.broadcasted_iota(jnp.int32, (sub, MLA_TK), 0)) % MLA_TQ
                kpos = k0 + lax.broadcasted_iota(jnp.int32, (sub, MLA_TK), 1)
                s = jnp.where(kpos <= qpos, s, -jnp.inf)
            m_old = m_ref[rs, :]
            m_new = jnp.maximum(m_old, jnp.max(s, axis=1, keepdims=True))
            alpha = jnp.exp(m_old - m_new)
            p = jnp.exp(s - jnp.concatenate([m_new] * reps, axis=1)).astype(BF16)
            l_ref[rs, :] = l_ref[rs, :] * alpha + jnp.dot(p, ones, preferred_element_type=F32)
            acc_ref[rs, :] = (acc_ref[rs, :] * jnp.concatenate([alpha] * (MLA_KV_LORA // LANES), axis=1)
                              + jnp.dot(p, kblk[:, :MLA_KV_LORA], preferred_element_type=F32))
            m_ref[rs, :] = m_new

    def full_step(kb, carry):
        block(pl.multiple_of(kb * MLA_TK, MLA_TK), False)
        return carry

    n_full = (i * MLA_TQ) // MLA_TK
    lax.fori_loop(0, n_full, full_step, 0)
    block(pl.multiple_of(n_full * MLA_TK, MLA_TK), True)
    o = acc_ref[...] / jnp.concatenate([l_ref[...]] * (MLA_KV_LORA // LANES), axis=1)
    o_ref[...] = o.reshape(MLA_H, MLA_TQ, MLA_KV_LORA).astype(BF16)


MLA_PP = 16
MLA_GROUPS = 2


def _mla_sample_body(pt_ref, q_ref, kn_ref, *rest):
    lat_refs = rest[:MLA_PP]
    kro_refs = rest[MLA_PP:2 * MLA_PP]
    o_ref, m_ref, l_ref, acc_ref = rest[2 * MLA_PP:]
    j = pl.program_id(1)
    t = q_ref.shape[1]
    rows = MLA_H * t
    q = q_ref[...].reshape(rows, MLA_QK).astype(BF16)
    ql = q[:, :MLA_KV_LORA]
    qr = q[:, MLA_KV_LORA:MLA_KV_LORA + MLA_ROPE]

    @pl.when(j == 0)
    def _():
        m_ref[...] = jnp.full(m_ref.shape, -jnp.inf, F32)
        l_ref[...] = jnp.zeros(l_ref.shape, F32)
        acc_ref[...] = jnp.zeros(acc_ref.shape, F32)

    vrep = MLA_KV_LORA // LANES

    def update(g, s, vals, row_sum):
        m_old = m_ref[g]
        m_new = jnp.maximum(m_old, jnp.max(s, axis=1, keepdims=True))
        alpha = jnp.exp(m_old - m_new)
        if s.shape[1] % LANES == 0:
            p = jnp.exp(s - jnp.concatenate([m_new] * (s.shape[1] // LANES), axis=1)).astype(BF16)
        else:
            p = jnp.exp(s - m_new[:, 0:1]).astype(BF16)
        l_ref[g] = l_ref[g] * alpha + row_sum(p)
        acc_ref[g] = (acc_ref[g] * jnp.concatenate([alpha] * vrep, axis=1)
                      + jnp.dot(p, vals, preferred_element_type=F32))
        m_ref[g] = m_new

    per = MLA_PP // MLA_GROUPS
    ones = jnp.ones((per * lat_refs[0].shape[1], LANES), BF16)
    scores, values = [], []
    for g in range(MLA_GROUPS):
        cbs, s_parts = [], []
        for pp in range(g * per, (g + 1) * per):
            cb = lat_refs[pp][0].astype(BF16)
            kbt = kro_refs[pp][0].astype(BF16)
            s_parts.append(lax.dot_general(ql, cb, (((1,), (1,)), ((), ())), preferred_element_type=F32)
                           + jnp.dot(qr, kbt, preferred_element_type=F32))
            cbs.append(cb)
        scores.append(jnp.concatenate(s_parts, axis=1))
        values.append(jnp.concatenate(cbs, axis=0))
    for g in range(MLA_GROUPS):
        update(g, scores[g], values[g], lambda p: jnp.dot(p, ones, preferred_element_type=F32))

    @pl.when(j == pl.num_programs(1) - 1)
    def _():
        kn = kn_ref[...].astype(BF16)
        s = lax.dot_general(q, kn, (((1,), (1,)), ((), ())), preferred_element_type=F32)
        qpos = lax.broadcasted_iota(jnp.int32, (rows, t), 0) % t
        kpos = lax.broadcasted_iota(jnp.int32, (rows, t), 1)
        s = jnp.where(kpos <= qpos, s, -jnp.inf)
        update(0, s, kn[:, :MLA_KV_LORA], lambda p: jnp.sum(p.astype(F32), axis=1, keepdims=True))
        m_all = m_ref[0]
        for g in range(1, MLA_GROUPS):
            m_all = jnp.maximum(m_all, m_ref[g])
        l_all = jnp.zeros((rows, LANES), F32)
        acc = jnp.zeros((rows, MLA_KV_LORA), F32)
        for g in range(MLA_GROUPS):
            wgt = jnp.exp(m_ref[g] - m_all)
            l_all = l_all + l_ref[g] * wgt
            acc = acc + acc_ref[g] * jnp.concatenate([wgt] * vrep, axis=1)
        o = acc / jnp.concatenate([l_all] * vrep, axis=1)
        o_ref[...] = o.reshape(MLA_H, t, MLA_KV_LORA).astype(o_ref.dtype)


def _mla_out_body(x_ref, o_ref, wuv_ref, wo_ref, nw_ref, xo_ref):
    parts = []
    for pr in range(MLA_H // 2):
        wp = wuv_ref[pr]
        parts.append(jnp.dot(o_ref[2 * pr].astype(BF16), wp[:MLA_KV_LORA], preferred_element_type=F32)
                     + jnp.dot(o_ref[2 * pr + 1].astype(BF16), wp[MLA_KV_LORA:], preferred_element_type=F32))
    v = jnp.concatenate(parts, axis=1).astype(BF16)
    o = jnp.dot(v, wo_ref[...], preferred_element_type=F32)
    xo_ref[...] = x_ref[...] + _rms(o, nw_ref[0:1, :])


def _mla_layer(x2d, pos, w, mi, nw, b, t, paged):
    n = b * t
    tl = _tiling(b, t, 512)
    G, J, R = tl["G"], tl["J"], tl["R"]
    adt = BF16 if t % 16 == 0 else F32
    half = MLA_ROPE // 2
    inv = ROPE_THETA ** (-jnp.arange(half, dtype=F32) / half)
    ang = pos.astype(F32)[:, None] * inv[None, :]
    cos, sin = jnp.cos(ang), jnp.sin(ang)
    zpad = jnp.zeros((t, LANES - MLA_ROPE), F32)
    zh = jnp.zeros((t, half), F32)
    tab = jnp.stack([jnp.concatenate([cos, cos, zpad], 1), jnp.concatenate([-sin, zh, zpad], 1),
                     jnp.concatenate([zh, sin, zpad], 1)])
    if G == 1:
        tab = jnp.tile(tab, (1, b, 1))
    w_in = w["mla_w_in"][mi]
    winq = w_in[:, :MLA_Q_LORA].astype(BF16)
    winc = w_in[:, MLA_Q_LORA:MLA_Q_LORA + MLA_KV_LORA].astype(BF16)
    wink = jnp.pad(w_in[:, MLA_Q_LORA + MLA_KV_LORA:], ((0, 0), (0, LANES - MLA_ROPE))).astype(BF16)
    wqb = w["mla_w_qb"][mi].reshape(MLA_Q_LORA, MLA_H, MLA_NOPE + MLA_ROPE)
    wqn = wqb[:, :, :MLA_NOPE].reshape(MLA_Q_LORA, MLA_H * MLA_NOPE).astype(BF16)
    wqr = jnp.pad(wqb[:, :, MLA_NOPE:], ((0, 0), (0, 0), (0, LANES - MLA_ROPE))
                  ).reshape(MLA_Q_LORA, MLA_H * LANES).astype(BF16)
    wuk = jnp.transpose(w["mla_w_uk"][mi], (1, 2, 0)).reshape(MLA_H // 2, 2, MLA_NOPE, MLA_KV_LORA)
    wuk_bd = jnp.einsum("pinc,ij->pinjc", wuk, jnp.eye(2, dtype=F32)).reshape(
        MLA_H // 2, 2 * MLA_NOPE, 2 * MLA_KV_LORA).astype(BF16)
    wuv = jnp.transpose(w["mla_w_uv"][mi], (1, 0, 2)).reshape(MLA_H // 2, 2, MLA_KV_LORA, MLA_V)
    wuv_bd = jnp.einsum("picv,ij->picjv", wuv, jnp.eye(2, dtype=F32)).reshape(
        MLA_H // 2, 2 * MLA_KV_LORA, 2 * MLA_V).astype(BF16)
    nwa = jnp.concatenate([nw[0:1], jnp.zeros((7, D_MODEL), F32)])
    nwb = jnp.concatenate([nw[1:2], jnp.zeros((7, D_MODEL), F32)])
    row = lambda g, j: (g * J + j, 0)
    wl = [winq, winc, wink, w["mla_q_norm"][mi][None, :], w["mla_kv_norm"][mi][None, :], wqn, wqr, wuk_bd]
    c, kr, kcat, qcat = pl.pallas_call(
        _mla_proj_body,
        grid=(G, J),
        in_specs=[pl.BlockSpec((R, D_MODEL), row), _const_spec((8, D_MODEL)),
                  pl.BlockSpec((3, R, LANES), lambda g, j: (0, j, 0))] + [_const_spec(a.shape) for a in wl],
        out_specs=[pl.BlockSpec((R, MLA_KV_LORA), row), pl.BlockSpec((R, LANES), row),
                   pl.BlockSpec((R, MLA_QK), row), pl.BlockSpec((MLA_H, R, MLA_QK), lambda g, j: (0, g * J + j, 0))],
        out_shape=[jax.ShapeDtypeStruct((n, MLA_KV_LORA), F32), jax.ShapeDtypeStruct((n, LANES), F32),
                   jax.ShapeDtypeStruct((n, MLA_QK), adt), jax.ShapeDtypeStruct((MLA_H, n, MLA_QK), adt)],
        compiler_params=_params(2),
        name="mla_proj",
    )(x2d, nwa, tab, *wl)

    if paged is None:
        nq = t // MLA_TQ
        rows = MLA_H * MLA_TQ
        o = pl.pallas_call(
            _mla_prompt_body,
            grid=(b, nq),
            in_specs=[pl.BlockSpec((MLA_H, MLA_TQ, MLA_QK), lambda bb, i: (0, bb * nq + i, 0)),
                      pl.BlockSpec((t, MLA_QK), lambda bb, i: (bb, 0))],
            out_specs=pl.BlockSpec((MLA_H, MLA_TQ, MLA_KV_LORA), lambda bb, i: (0, bb * nq + i, 0)),
            out_shape=jax.ShapeDtypeStruct((MLA_H, n, MLA_KV_LORA), BF16),
            scratch_shapes=[pltpu.VMEM((rows, LANES), F32), pltpu.VMEM((rows, LANES), F32),
                            pltpu.VMEM((rows, MLA_KV_LORA), F32)],
            compiler_params=_params(2),
            name="mla_attend_prompt",
        )(qcat, kcat)
    else:
        pages_c, pages_kr, page_table = paged
        page = pages_c.shape[1]
        npg = page_table.shape[1]
        assert npg % MLA_PP == 0
        rows = MLA_H * t

        def page_map(pp):
            return lambda bb, j, pt: (pt[bb, j * MLA_PP + pp], 0, 0)

        grid_spec = pltpu.PrefetchScalarGridSpec(
            num_scalar_prefetch=1,
            grid=(b, npg // MLA_PP),
            in_specs=[pl.BlockSpec((MLA_H, t, MLA_QK), lambda bb, j, pt: (0, bb, 0)),
                      pl.BlockSpec((t, MLA_QK), lambda bb, j, pt: (bb, 0))]
            + [pl.BlockSpec((1, page, MLA_KV_LORA), page_map(pp)) for pp in range(MLA_PP)]
            + [pl.BlockSpec((1, MLA_ROPE, page), page_map(pp)) for pp in range(MLA_PP)],
            out_specs=pl.BlockSpec((MLA_H, t, MLA_KV_LORA), lambda bb, j, pt: (0, bb, 0)),
            scratch_shapes=[pltpu.VMEM((MLA_GROUPS, rows, LANES), F32), pltpu.VMEM((MLA_GROUPS, rows, LANES), F32),
                            pltpu.VMEM((MLA_GROUPS, rows, MLA_KV_LORA), F32)],
        )
        o = pl.pallas_call(
            _mla_sample_body,
            grid_spec=grid_spec,
            out_shape=jax.ShapeDtypeStruct((MLA_H, n, MLA_KV_LORA), adt),
            compiler_params=_params(2),
            name="mla_attend_sample",
        )(page_table, qcat, kcat, *([pages_c] * MLA_PP), *([jnp.swapaxes(pages_kr, 1, 2)] * MLA_PP))

    Ro = min(n, 512)
    x_new = pl.pallas_call(
        _mla_out_body,
        grid=(n // Ro,),
        in_specs=[pl.BlockSpec((Ro, D_MODEL), lambda i: (i, 0)),
                  pl.BlockSpec((MLA_H, Ro, MLA_KV_LORA), lambda i: (0, i, 0)),
                  _const_spec(wuv_bd.shape), _const_spec((MLA_H * MLA_V, D_MODEL)), _const_spec((8, D_MODEL))],
        out_specs=pl.BlockSpec((Ro, D_MODEL), lambda i: (i, 0)),
        out_shape=jax.ShapeDtypeStruct((n, D_MODEL), F32),
        compiler_params=_params(1),
        name="mla_out",
    )(x2d, o, wuv_bd, w["mla_wo"][mi].astype(BF16), nwb)
    return x_new, c.reshape(b, t, MLA_KV_LORA), kr[:, :MLA_ROPE].reshape(b, t, MLA_ROPE)


GDN_CW = 512


def _gdn_proj_body(x_ref, prev_ref, nw_ref, wqkv_ref, wz_ref, wbg_ref, cw_ref, gvec_ref, tri_ref,
                   q_ref, k_ref, v_ref, z_ref, beta_ref, gc_ref, st_ref, carry_ref):
    @pl.when(pl.program_id(1) == 0)
    def _():
        carry_ref[...] = prev_ref[...]

    rows = x_ref.shape[0]
    p = carry_ref.shape[0]
    h = _rms(x_ref[...], nw_ref[0:1, :]).astype(BF16)
    z_ref[...] = jnp.dot(h, wz_ref[...], preferred_element_type=F32)
    bg = jnp.dot(h, wbg_ref[...], preferred_element_type=F32)
    beta_ref[...] = _sigmoid(bg)
    g = -jnp.exp(gvec_ref[0:1, :]) * _softplus(bg + gvec_ref[1:2, :])
    gc_ref[...] = _chunk_cumsum(g, tri_ref[...])
    nch = GDN_CONV_DIM // GDN_CW

    def up(c):
        return jnp.dot(h, wqkv_ref[:, c * GDN_CW:(c + 1) * GDN_CW], preferred_element_type=F32)

    u_nxt = up(0)
    for c in range(nch):
        sl = slice(c * GDN_CW, (c + 1) * GDN_CW)
        u = u_nxt
        u_nxt = up(c + 1) if c + 1 < nch else None
        prev = carry_ref[:, sl]
        y = cw_ref[3:4, sl] * u
        for s in range(1, GDN_CONV):
            y = y + cw_ref[3 - s:4 - s, sl] * _shift_rows(u, prev, s)
        tail = u[rows - p:, :]
        carry_ref[:, sl] = tail
        st_ref[:, sl] = tail
        y = _silu(y)
        off = c * GDN_CW
        if off < 2 * GDN_QK_DIM:
            dst, base, scale = (q_ref, off, GDN_DK ** -0.5) if off < GDN_QK_DIM else (k_ref, off - GDN_QK_DIM, 1.0)
            for hh in range(GDN_CW // GDN_DK):
                yh = y[:, hh * GDN_DK:(hh + 1) * GDN_DK]
                yh = yh * lax.rsqrt(jnp.sum(yh * yh, axis=-1, keepdims=True) + 1e-6)
                dst[:, base + hh * GDN_DK:base + (hh + 1) * GDN_DK] = yh * scale if scale != 1.0 else yh
        else:
            v_ref[:, off - 2 * GDN_QK_DIM:off - 2 * GDN_QK_DIM + GDN_CW] = y


def _gdn_chunk_body(q_ref, k_ref, v_ref, z_ref, gc_ref, beta_ref, s0_ref, nw_ref,
                    o_ref, so_ref, *, nh, chunk):
    @pl.when(pl.program_id(1) == 0)
    def _():
        so_ref[...] = s0_ref[...]

    ng = GDN_V_H // nh
    gc = nh * chunk
    rep = GDN_V_H // GDN_QK_H
    ri = lax.broadcasted_iota(jnp.int32, (gc, gc), 0)
    ci = lax.broadcasted_iota(jnp.int32, (gc, gc), 1)
    same = (ri // chunk) == (ci // chunk)
    strict = same & ((ri % chunk) > (ci % chunk))
    incl = same & ((ri % chunk) >= (ci % chunk))
    last = same & ((ci % chunk) == chunk - 1)
    eye = (ri == ci).astype(F32)
    row_head = lax.broadcasted_iota(jnp.int32, (gc, GDN_DK), 0) // chunk

    def stack(ref, heads, width):
        parts = [ref[:, hd * width:(hd + 1) * width] for hd in heads]
        return parts[0] if len(parts) == 1 else jnp.concatenate(parts, axis=0)

    groups = range(ng)
    heads = [[q * nh + i for i in range(nh)] for q in groups]
    k_st = [stack(k_ref, [hd // rep for hd in heads[q]], GDN_DK) for q in groups]
    q_st = [stack(q_ref, [hd // rep for hd in heads[q]], GDN_DK) for q in groups]
    v_st = [stack(v_ref, heads[q], GDN_DV) for q in groups]
    def col(ref, lanes):
        parts = [ref[:, ln:ln + 1] for ln in lanes]
        return parts[0] if len(parts) == 1 else jnp.concatenate(parts, axis=0)

    gcol = [col(gc_ref, [GDN_V_H + hd for hd in heads[q]]) for q in groups]
    bcol = [col(beta_ref, heads[q]) for q in groups]
    grow = [jnp.sum(jnp.where(ri == ci, gcol[q], 0.0), axis=0, keepdims=True) for q in groups]
    k_b = [x.astype(BF16) for x in k_st]
    kq = [_bdot_nt(jnp.concatenate([k_b[q], q_st[q].astype(BF16)], axis=0), k_b[q]) for q in groups]
    decay = [jnp.exp(jnp.where(incl, gcol[q] - grow[q], -jnp.inf)) for q in groups]
    a = [jnp.where(strict, kq[q][:gc] * bcol[q] * decay[q], 0.0) for q in groups]
    aqk = [jnp.where(incl, kq[q][gc:] * decay[q], 0.0).astype(BF16) for q in groups]

    p = [(-m).astype(BF16) for m in a]
    x = [eye - m for m in a]
    span = 2
    if span < chunk:
        p = [_bdot(p[q], p[q]) for q in groups]
    while span < chunk:
        if span * 2 < chunk:
            px = [_bdot(p[q], jnp.concatenate([p[q].astype(BF16), x[q].astype(BF16)], axis=1)) for q in groups]
            p = [m[:, :gc] for m in px]
            x = [x[q] + px[q][:, gc:] for q in groups]
        else:
            x = [x[q] + _bdot(p[q], x[q]) for q in groups]
        span *= 2

    egc = [jnp.exp(g) for g in gcol]
    uw = [_bdot(x[q], jnp.concatenate([v_st[q] * bcol[q], k_st[q] * (bcol[q] * egc[q])], axis=1)) for q in groups]
    glast = [jnp.sum(jnp.where(last, grow[q], 0.0), axis=1, keepdims=True) for q in groups]
    kg = [k_st[q] * jnp.exp(glast[q] - gcol[q]) for q in groups]
    states = [[so_ref[0, hd] for hd in heads[q]] for q in groups]
    wq_s = []
    for q in groups:
        wm = uw[q][:, GDN_DV:]
        qg = q_st[q] * egc[q]
        parts = []
        for i in range(nh):
            rs = slice(i * chunk, (i + 1) * chunk)
            parts.append(_bdot(jnp.concatenate([wm[rs], qg[rs]], axis=0), states[q][i]))
        wq_s.append(parts)
    v_new, o_st = [], []
    for q in groups:
        ws = jnp.concatenate([m[:chunk] for m in wq_s[q]], axis=0) if nh > 1 else wq_s[q][0][:chunk]
        qs = jnp.concatenate([m[chunk:] for m in wq_s[q]], axis=0) if nh > 1 else wq_s[q][0][chunk:]
        vn = (uw[q][:, :GDN_DV] - ws).astype(BF16)
        v_new.append(vn)
        o_st.append(qs + _bdot(aqk[q], vn))
    for q in groups:
        for i, hd in enumerate(heads[q]):
            rs = slice(i * chunk, (i + 1) * chunk)
            gl_h = jnp.exp(glast[q][i * chunk:i * chunk + 1, :])
            if chunk % 16 == 0:
                upd = _bdot_tn(kg[q][rs], v_new[q][rs])
            else:
                upd = _bdot_tn(jnp.where(row_head == i, kg[q], 0.0), v_new[q])
            so_ref[0, hd] = states[q][i] * gl_h + upd
    for q in groups:
        z_st = stack(z_ref, heads[q], GDN_DV)
        og = _rms(o_st[q], nw_ref[0:1, :]) * _silu(z_st)
        for i, hd in enumerate(heads[q]):
            o_ref[:, hd * GDN_DV:(hd + 1) * GDN_DV] = og[i * chunk:(i + 1) * chunk]


def _gdn_layer(x2d, conv_prev, s0, w, gi, nw, b, t):
    n = b * t
    chunk = _chunk_of(t)
    tl = _tiling(b, t, 256)
    G, J, R, P = tl["G"], tl["J"], tl["R"], tl["P"]
    w_in = w["gdn_w_in"][gi]
    o1 = GDN_CONV_DIM
    o2 = o1 + GDN_V_DIM
    wqkv = w_in[:, :o1].astype(BF16)
    wz = w_in[:, o1:o2].astype(BF16)
    wbg = jnp.pad(w_in[:, o2:], ((0, 0), (0, LANES - 2 * GDN_V_H))).astype(BF16)
    cw = jnp.pad(w["gdn_conv_w"][gi], ((0, 8 - GDN_CONV), (0, 0)))
    gvec = jnp.zeros((8, LANES), F32)
    gvec = gvec.at[0, GDN_V_H:2 * GDN_V_H].set(w["gdn_a_log"][gi]).at[1, GDN_V_H:2 * GDN_V_H].set(w["gdn_dt_bias"][gi])
    bc = chunk if chunk == 64 else R
    tri = _chunk_masks(chunk, bc)
    nwa = jnp.concatenate([nw[0:1], jnp.zeros((7, D_MODEL), F32)])
    row = lambda g, j: (g * J + j, 0)
    st_spec = pl.BlockSpec((P, GDN_CONV_DIM), lambda g, j: (g, 0))
    qn, kn, v, z, beta, gcs, st = pl.pallas_call(
        _gdn_proj_body,
        grid=(G, J),
        in_specs=[pl.BlockSpec((R, D_MODEL), row), st_spec, _const_spec((8, D_MODEL)), _const_spec(wqkv.shape),
                  _const_spec(wz.shape), _const_spec(wbg.shape), _const_spec(cw.shape), _const_spec(gvec.shape),
                  _const_spec(tri.shape)],
        out_specs=[pl.BlockSpec((R, GDN_QK_DIM), row), pl.BlockSpec((R, GDN_QK_DIM), row),
                   pl.BlockSpec((R, GDN_V_DIM), row), pl.BlockSpec((R, GDN_V_DIM), row),
                   pl.BlockSpec((R, LANES), row), pl.BlockSpec((R, LANES), row), st_spec],
        out_shape=[jax.ShapeDtypeStruct((n, GDN_QK_DIM), F32), jax.ShapeDtypeStruct((n, GDN_QK_DIM), F32),
                   jax.ShapeDtypeStruct((n, GDN_V_DIM), F32), jax.ShapeDtypeStruct((n, GDN_V_DIM), F32),
                   jax.ShapeDtypeStruct((n, LANES), F32), jax.ShapeDtypeStruct((n, LANES), F32),
                   jax.ShapeDtypeStruct((b * SUBLANES, GDN_CONV_DIM), F32)],
        scratch_shapes=[pltpu.VMEM((P, GDN_CONV_DIM), F32)],
        compiler_params=_params(2),
        name="gdn_proj",
    )(x2d, _pad_state(conv_prev), nwa, wqkv, wz, wbg, cw, gvec, tri)
    conv_new = st.reshape(b, SUBLANES, GDN_CONV_DIM)[:, SUBLANES - (GDN_CONV - 1):]

    nh = GROUP_ROWS // chunk
    ng = GDN_V_H // nh
    nct = t // chunk
    crow = lambda bb, j: (bb * nct + j, 0)
    sspec = pl.BlockSpec((1, GDN_V_H, GDN_DK, GDN_DV), lambda bb, j: (bb, 0, 0, 0))
    nwn = jnp.concatenate([w["gdn_norm_w"][gi][None, :], jnp.zeros((7, GDN_DV), F32)])
    o, s_new = pl.pallas_call(
        functools.partial(_gdn_chunk_body, nh=nh, chunk=chunk),
        grid=(b, nct),
        in_specs=[pl.BlockSpec((chunk, GDN_QK_DIM), crow), pl.BlockSpec((chunk, GDN_QK_DIM), crow),
                  pl.BlockSpec((chunk, GDN_V_DIM), crow), pl.BlockSpec((chunk, GDN_V_DIM), crow),
                  pl.BlockSpec((chunk, LANES), crow), pl.BlockSpec((chunk, LANES), crow),
                  sspec, _const_spec((8, GDN_DV))],
        out_specs=[pl.BlockSpec((chunk, GDN_V_DIM), crow), sspec],
        out_shape=[jax.ShapeDtypeStruct((n, GDN_V_DIM), F32),
                   jax.ShapeDtypeStruct((b, GDN_V_H, GDN_DK, GDN_DV), F32)],
        compiler_params=_params(2),
        name="gdn_chunk",
    )(qn, kn, v, z, gcs, beta, s0, nwn)
    nwb = jnp.concatenate([nw[1:2], jnp.zeros((7, D_MODEL), F32)])
    x_new = _outproj(x2d, o, None, w["gdn_wo"][gi].astype(BF16), nwb, "gdn_out")
    return x_new, conv_new, s_new


def _trunk(x, pos, rw_s, rw_shift, gdn_s, gdn_conv, ffn_conv, w, paged):
    b, t, _ = x.shape
    x2d = x.reshape(b * t, D_MODEL)
    new = {k: [] for k in ("rw_S", "rw_shift", "mla_c", "mla_kr", "gdn_S", "gdn_conv", "ffn_conv")}
    v_first = None
    ri = mi = gi = 0
    for l, kind in enumerate(LAYER_MIXER):
        nw = w["norm_w"][l]
        if kind == 0:
            x2d, sh, s_new, v_first = _rwkv_layer(x2d, rw_shift[ri], rw_s[ri], v_first, w, ri, nw, b, t)
            new["rw_S"].append(s_new)
            new["rw_shift"].append(sh)
            ri += 1
        elif kind == 1:
            x2d, c, kr = _mla_layer(x2d, pos, w, mi, nw, b, t, None if paged is None else
                                    (paged[0][mi], paged[1][mi], paged[2]))
            new["mla_c"].append(c)
            new["mla_kr"].append(kr)
            mi += 1
        else:
            x2d, cb, s_new = _gdn_layer(x2d, gdn_conv[gi], gdn_s[gi], w, gi, nw, b, t)
            new["gdn_S"].append(s_new)
            new["gdn_conv"].append(cb)
            gi += 1
        nwf = jnp.concatenate([nw[2:4], jnp.zeros((6, D_MODEL), F32)])
        cwb = jnp.concatenate([w["ffn_conv_w"][l], w["ffn_conv_b"][l][None, :],
                               jnp.zeros((8 - FFN_CONV - 1, 2 * D_FF), F32)])
        x2d, st = _ffn(x2d, _pad_state(ffn_conv[l]), nwf, w["ffn_w_up"][l].astype(BF16), cwb,
                       w["ffn_w_down"][l].astype(BF16), b, t)
        new["ffn_conv"].append(st.reshape(b, SUBLANES, 2 * D_FF)[:, SUBLANES - (FFN_CONV - 1):])
    return x2d.reshape(b, t, D_MODEL), {k: jnp.stack(v) for k, v in new.items()}


def kernel(x_prompt, x_sample, state_rwkv_wkv, state_rwkv_shift, cache_mla_latent, cache_mla_krope, state_gdn_S, state_gdn_conv, state_ffn_conv, page_table, norm_w, rw_mu, rw_wrkv, rw_w0, rw_w1, rw_w2, rw_a0, rw_a1, rw_a2, rw_v0, rw_v1, rw_v2, rw_g1, rw_g2, rw_kk, rw_ka, rw_rk, rw_lnx_w, rw_lnx_b, rw_wo, mla_w_in, mla_q_norm, mla_kv_norm, mla_w_qb, mla_w_uk, mla_w_uv, mla_wo, gdn_w_in, gdn_conv_w, gdn_a_log, gdn_dt_bias, gdn_norm_w, gdn_wo, ffn_w_up, ffn_conv_w, ffn_conv_b, ffn_w_down):
    w = dict(norm_w=norm_w, rw_mu=rw_mu, rw_wrkv=rw_wrkv, rw_w0=rw_w0, rw_w1=rw_w1, rw_w2=rw_w2, rw_a0=rw_a0,
             rw_a1=rw_a1, rw_a2=rw_a2, rw_v0=rw_v0, rw_v1=rw_v1, rw_v2=rw_v2, rw_g1=rw_g1, rw_g2=rw_g2,
             rw_kk=rw_kk, rw_ka=rw_ka, rw_rk=rw_rk, rw_lnx_w=rw_lnx_w, rw_lnx_b=rw_lnx_b, rw_wo=rw_wo,
             mla_w_in=mla_w_in, mla_q_norm=mla_q_norm, mla_kv_norm=mla_kv_norm, mla_w_qb=mla_w_qb,
             mla_w_uk=mla_w_uk, mla_w_uv=mla_w_uv, mla_wo=mla_wo, gdn_w_in=gdn_w_in, gdn_conv_w=gdn_conv_w,
             gdn_a_log=gdn_a_log, gdn_dt_bias=gdn_dt_bias, gdn_norm_w=gdn_norm_w, gdn_wo=gdn_wo,
             ffn_w_up=ffn_w_up, ffn_conv_w=ffn_conv_w, ffn_conv_b=ffn_conv_b, ffn_w_down=ffn_w_down)
    b, t = x_prompt.shape[0], x_prompt.shape[1]
    n_rw, n_gdn, depth = state_rwkv_wkv.shape[0], state_gdn_S.shape[0], state_ffn_conv.shape[0]
    y_p, sp = _trunk(
        x_prompt, jnp.arange(t),
        jnp.zeros((n_rw, b) + state_rwkv_wkv.shape[2:], F32), jnp.zeros((n_rw, b, D_MODEL), F32),
        jnp.zeros((n_gdn, b) + state_gdn_S.shape[2:], F32), jnp.zeros((n_gdn, b) + state_gdn_conv.shape[2:], F32),
        jnp.zeros((depth, b) + state_ffn_conv.shape[2:], F32), w, None)
    past_len = page_table.shape[1] * cache_mla_latent.shape[2]
    pos_s = past_len + jnp.arange(x_sample.shape[1])
    y_s, ss = _trunk(x_sample, pos_s, state_rwkv_wkv, state_rwkv_shift, state_gdn_S, state_gdn_conv,
                     state_ffn_conv, w, (cache_mla_latent, cache_mla_krope, page_table))
    names = ("rw_S", "rw_shift", "mla_c", "mla_kr", "gdn_S", "gdn_conv", "ffn_conv")
    return (y_p, y_s) + tuple(sp[k] for k in names) + tuple(ss[k] for k in names)
```

```python
import functools

import jax
import jax.numpy as jnp
from jax import lax
from jax.experimental import pallas as pl
from jax.experimental.pallas import tpu as pltpu

F32 = jnp.float32
BF16 = jnp.bfloat16
HIGHEST = lax.Precision.HIGHEST

D_MODEL = 1024
NORM_EPS = 1e-6
RW_N = 64
RW_H = D_MODEL // RW_N
RW_LNX_EPS = 64e-5
MLA_H = 16
MLA_NOPE = 64
MLA_ROPE = 32
MLA_V = 64
MLA_Q_LORA = 512
MLA_KV_LORA = 256
MLA_SCALE = (MLA_NOPE + MLA_ROPE) ** -0.5
ROPE_THETA = 10000.0
MLA_QK = MLA_KV_LORA + 128
GDN_QK_H = 8
GDN_V_H = 16
GDN_DK = 128
GDN_DV = 128
GDN_QK_DIM = GDN_QK_H * GDN_DK
GDN_V_DIM = GDN_V_H * GDN_DV
GDN_CONV_DIM = 2 * GDN_QK_DIM + GDN_V_DIM
GDN_CONV = 4
D_FF = 2816
FFN_CONV = 3
LAYER_MIXER = (0, 1, 2, 0)

SUBLANES = 8
LANES = 128
GROUP_ROWS = 128
VMEM_LIMIT = 56 * 1024 * 1024


def _rms(x, w):
    return x * lax.rsqrt(jnp.mean(x * x, axis=-1, keepdims=True) + NORM_EPS) * w


def _bdot(a, b):
    return jnp.dot(a.astype(BF16), b.astype(BF16), preferred_element_type=F32)


def _bdot_nt(a, b):
    return lax.dot_general(a.astype(BF16), b.astype(BF16), (((1,), (1,)), ((), ())),
                           preferred_element_type=F32)


def _bdot_tn(a, b):
    return lax.dot_general(a.astype(BF16), b.astype(BF16), (((0,), (0,)), ((), ())),
                           preferred_element_type=F32)


def _hdot(a, b):
    return jnp.dot(a, b, precision=HIGHEST, preferred_element_type=F32)


def _sigmoid(x):
    return 1.0 / (1.0 + jnp.exp(-x))


def _softplus(x):
    return jnp.maximum(x, 0.0) + jnp.log(1.0 + jnp.exp(-jnp.abs(x)))


def _silu(x):
    return x * _sigmoid(x)


def _shift_rows(u, prev, s):
    rows, cols = u.shape
    p = prev.shape[0]
    rolled = pltpu.roll(u, s, 0)
    fix = pltpu.roll(prev, (p - SUBLANES + s) % p, 0)
    t = lax.broadcasted_iota(jnp.int32, (p, cols), 0) % SUBLANES
    if p == rows:
        return jnp.where(t < s, fix, rolled)
    head = jnp.where(t < s, fix, rolled[:SUBLANES])
    return jnp.concatenate([head, rolled[SUBLANES:]], axis=0)


def _lane_group_sum(x, ones2):
    parts = []
    for i in range(x.shape[1] // LANES):
        xs = x[:, i * LANES:(i + 1) * LANES]
        hi = xs.astype(BF16)
        lo = (xs - hi.astype(F32)).astype(BF16)
        parts.append(jnp.dot(jnp.concatenate([hi, lo], axis=1), ones2, preferred_element_type=F32))
    return parts[0] if len(parts) == 1 else jnp.concatenate(parts, axis=1)


def _chunk_cumsum(x, tri):
    bc = tri.shape[0]
    parts = [_hdot(tri, x[i * bc:(i + 1) * bc]) for i in range(x.shape[0] // bc)]
    return parts[0] if len(parts) == 1 else jnp.concatenate(parts, axis=0)


def _tiling(b, t, tt_max):
    if t == SUBLANES:
        return dict(G=1, J=1, R=b * t, P=b * t)
    tt = min(t, tt_max)
    assert t % tt == 0 and tt % 64 == 0, (t, tt)
    return dict(G=b, J=t // tt, R=tt, P=SUBLANES)


def _chunk_of(t):
    return 64 if t % 64 == 0 else t


def _const_spec(shape):
    nd = len(shape)
    return pl.BlockSpec(shape, lambda *_: (0,) * nd, pipeline_mode=pl.Buffered(1))


def _params(n_axes):
    return pltpu.CompilerParams(dimension_semantics=("arbitrary",) * n_axes,
                                vmem_limit_bytes=VMEM_LIMIT)


def _pad_state(st):
    b, k1, c = st.shape
    return jnp.pad(st, ((0, 0), (SUBLANES - k1, 0), (0, 0))).reshape(b * SUBLANES, c)


def _chunk_masks(chunk, rows):
    i = jnp.arange(rows)
    same = (i[:, None] // chunk) == (i[None, :] // chunk)
    tri = same & ((i[None, :] % chunk) <= (i[:, None] % chunk))
    return tri.astype(F32)


FFN_CW = 256


ROW_BLOCK = 64


def _stage(buf_ref, u, carry_ref, st_ref, sl):
    rows = u.shape[0]
    buf_ref[0:SUBLANES, :] = carry_ref[:, sl]
    buf_ref[SUBLANES:SUBLANES + rows, :] = u
    tail = u[rows - SUBLANES:, :]
    carry_ref[:, sl] = tail
    st_ref[:, sl] = tail


def _taps(buf_ref, r0, nrows, shifts):
    slab = buf_ref[r0:r0 + nrows + SUBLANES, :]
    return slab[SUBLANES:], [pltpu.roll(slab, s, 0)[SUBLANES:] for s in shifts]


def _ffn_body(x_ref, prev_ref, nw_ref, wup_ref, cwb_ref, wdn_ref, xo_ref, st_ref, carry_ref, act_ref, buf_ref):
    @pl.when(pl.program_id(1) == 0)
    def _():
        carry_ref[...] = prev_ref[...]

    x = x_ref[...]
    rows = x.shape[0]
    stacked = carry_ref.shape[0] == rows
    h = _rms(x, nw_ref[0:1, :]).astype(BF16)
    nch = D_FF // FFN_CW

    def cols(c, half):
        return slice(half * D_FF + c * FFN_CW, half * D_FF + (c + 1) * FFN_CW)

    def conv_gate(c, taps):
        ys = []
        for half in range(2):
            sl = cols(c, half)
            u, (u2, u1) = taps[half]
            ys.append(cwb_ref[0:1, sl] * u2 + cwb_ref[1:2, sl] * u1 + cwb_ref[2:3, sl] * u + cwb_ref[3:4, sl])
        return (_silu(ys[0]) * ys[1]).astype(BF16)

    def up(c):
        us = [jnp.dot(h, wup_ref[:, cols(c, half)], preferred_element_type=F32) for half in range(2)]
        if stacked:
            return us
        for half in range(2):
            _stage(buf_ref.at[(c % 2) * 2 + half], us[half], carry_ref, st_ref, cols(c, half))
        return None

    u_cur = up(0)
    for c in range(nch):
        u_nxt = up(c + 1) if c + 1 < nch else None
        csl = slice(c * FFN_CW, (c + 1) * FFN_CW)
        if stacked:
            taps = []
            for half in range(2):
                sl = cols(c, half)
                u = u_cur[half]
                prev = carry_ref[:, sl]
                taps.append((u, [_shift_rows(u, prev, 2), _shift_rows(u, prev, 1)]))
                carry_ref[:, sl] = u
                st_ref[:, sl] = u
            act_ref[:, csl] = conv_gate(c, taps)
        else:
            for r0 in range(0, rows, ROW_BLOCK):
                taps = [_taps(buf_ref.at[(c % 2) * 2 + half], r0, ROW_BLOCK, (2, 1)) for half in range(2)]
                act_ref[r0:r0 + ROW_BLOCK, csl] = conv_gate(c, taps)
        u_cur = u_nxt
    f = jnp.dot(act_ref[...], wdn_ref[...], preferred_element_type=F32)
    xo_ref[...] = x + _rms(f, nw_ref[1:2, :])


def _ffn(x2d, prev, nw, wup, cwb, wdn, b, t):
    tl = _tiling(b, t, 512)
    G, J, R, P = tl["G"], tl["J"], tl["R"], tl["P"]
    n = b * t
    return pl.pallas_call(
        _ffn_body,
        grid=(G, J),
        in_specs=[
            pl.BlockSpec((R, D_MODEL), lambda g, j: (g * J + j, 0)),
            pl.BlockSpec((P, 2 * D_FF), lambda g, j: (g, 0)),
            _const_spec((8, D_MODEL)),
            _const_spec((D_MODEL, 2 * D_FF)),
            _const_spec((8, 2 * D_FF)),
            _const_spec((D_FF, D_MODEL)),
        ],
        out_specs=[
            pl.BlockSpec((R, D_MODEL), lambda g, j: (g * J + j, 0)),
            pl.BlockSpec((P, 2 * D_FF), lambda g, j: (g, 0)),
        ],
        out_shape=[jax.ShapeDtypeStruct((n, D_MODEL), F32),
                   jax.ShapeDtypeStruct((b * SUBLANES, 2 * D_FF), F32)],
        scratch_shapes=[pltpu.VMEM((P, 2 * D_FF), F32), pltpu.VMEM((R, D_FF), BF16),
                        pltpu.VMEM((4, R + SUBLANES, FFN_CW), F32)],
        compiler_params=_params(2),
        name="conv_ffn",
    )(x2d, prev, nw, wup, cwb, wdn)


def _outproj_body(*refs, gated):
    x_ref, y_ref = refs[0], refs[1]
    wo_ref, nw_ref, xo_ref = refs[-3:]
    y = y_ref[...]
    if gated:
        y = y * refs[2][...]
    o = jnp.dot(y.astype(BF16), wo_ref[...], preferred_element_type=F32)
    xo_ref[...] = x_ref[...] + _rms(o, nw_ref[0:1, :])


def _outproj(x2d, y2d, gate2d, wo, nw, name):
    n, k = y2d.shape
    R = min(n, 512)
    row = lambda i: (i, 0)
    acts = [y2d] if gate2d is None else [y2d, gate2d]
    return pl.pallas_call(
        functools.partial(_outproj_body, gated=gate2d is not None),
        grid=(n // R,),
        in_specs=[pl.BlockSpec((R, D_MODEL), row)] + [pl.BlockSpec((R, k), row)] * len(acts)
        + [_const_spec((k, D_MODEL)), _const_spec((8, D_MODEL))],
        out_specs=pl.BlockSpec((R, D_MODEL), row),
        out_shape=jax.ShapeDtypeStruct((n, D_MODEL), F32),
        compiler_params=_params(1),
        name=name,
    )(x2d, *acts, wo, nw)


def _rwkv_proj_body(*refs, has_vres, chunk):
    it = iter(refs)
    x_ref, prev_ref = next(it), next(it)
    vf_ref = next(it) if has_vres else None
    vec_ref, wrkv_ref, w1_ref, w2_ref, a1_ref, a2_ref = (next(it) for _ in range(6))
    v1_ref, v2_ref = (next(it), next(it)) if has_vres else (None, None)
    g1_ref, g2_ref, tri_ref, ones_ref = (next(it) for _ in range(4))
    rt_ref, kt_ref, at_ref, bt_ref, v_ref, g_ref, gl_ref, hl_ref, carry_ref = (next(it) for _ in range(9))

    @pl.when(pl.program_id(1) == 0)
    def _():
        carry_ref[...] = prev_ref[...]

    x = x_ref[...]
    rows = x.shape[0]
    p = carry_ref.shape[0]
    h = _rms(x, vec_ref[10:11, :])
    d = _shift_rows(h, carry_ref[...], 1) - h
    tail = h[rows - p:, :]
    carry_ref[...] = tail
    hl_ref[...] = tail

    def mix(i):
        return (h + d * vec_ref[i:i + 1, :]).astype(BF16)

    r = jnp.dot(mix(0), wrkv_ref[0], preferred_element_type=F32)
    k = jnp.dot(mix(1), wrkv_ref[1], preferred_element_type=F32)
    xv = mix(2)
    v = jnp.dot(xv, wrkv_ref[2], preferred_element_type=F32)
    w_raw = vec_ref[6:7, :] + _bdot(jnp.tanh(_bdot(mix(3), w1_ref[...])), w2_ref[...])
    w = -_softplus(-w_raw) - 0.5
    if has_vres:
        gate_v = _sigmoid(vec_ref[11:12, :] + _bdot(_bdot(xv, v1_ref[...]), v2_ref[...]))
        v = v + (vf_ref[...] - v) * gate_v
    a = _sigmoid(vec_ref[7:8, :] + _bdot(_bdot(mix(4), a1_ref[...]), a2_ref[...]))
    g_ref[...] = _bdot(_sigmoid(_bdot(mix(5), g1_ref[...])), g2_ref[...])
    v_ref[...] = v

    kk = k * vec_ref[8:9, :]
    kk = kk * lax.rsqrt(_lane_group_sum(kk * kk, ones_ref[...]) + 1e-6)
    k = k * (1.0 + (a - 1.0) * vec_ref[9:10, :])

    lw = -jnp.exp(w)
    cum = _chunk_cumsum(lw, tri_ref[...])
    e_fwd = jnp.exp(cum)
    e_bwd = jnp.exp(-cum)
    rt_ref[...] = r * e_fwd
    kt_ref[...] = k * e_bwd
    at_ref[...] = -kk * jnp.exp(cum - lw)
    bt_ref[...] = kk * a * e_bwd
    for c in range(rows // chunk):
        gl_ref[c] = jnp.exp(cum[(c + 1) * chunk - 1:(c + 1) * chunk, :])


def _rwkv_scan_body(rt_ref, kt_ref, at_ref, bt_ref, v_ref, gl_ref, h0_ref, vec_ref, y_ref, ho_ref,
                    *, nh, chunk, nsub):
    @pl.when(pl.program_id(1) == 0)
    def _():
        ho_ref[...] = h0_ref[...]

    gl_lanes = nh * RW_N
    ng = RW_H // nh
    gc = nh * chunk
    row_head = lax.broadcasted_iota(jnp.int32, (gc, gl_lanes), 0) // chunk
    lane_head = lax.broadcasted_iota(jnp.int32, (gc, gl_lanes), 1) // RW_N
    own = row_head == lane_head
    ri = lax.broadcasted_iota(jnp.int32, (gc, gc), 0)
    ci = lax.broadcasted_iota(jnp.int32, (gc, gc), 1)
    same = (ri // chunk) == (ci // chunk)
    strict = same & ((ri % chunk) > (ci % chunk))
    incl = same & ((ri % chunk) >= (ci % chunk))
    eye = (ri == ci).astype(F32)
    eye_l = (lax.broadcasted_iota(jnp.int32, (gl_lanes, gl_lanes), 0)
             == lax.broadcasted_iota(jnp.int32, (gl_lanes, gl_lanes), 1))
    merged = gc == GROUP_ROWS
    groups = range(ng)
    sls = [slice(q * gl_lanes, (q + 1) * gl_lanes) for q in groups]
    keys = [(sc, q) for sc in range(nsub) for q in groups]

    def blockdiag(ref, key):
        xg = ref[key[0] * chunk:(key[0] + 1) * chunk, sls[key[1]]]
        xx = jnp.concatenate([xg] * nh, axis=0) if nh > 1 else xg
        return jnp.where(own, xx, 0.0)

    r_bd = {k: blockdiag(rt_ref, k) for k in keys}
    k_bd = {k: blockdiag(kt_ref, k) for k in keys}
    a_bd = {k: blockdiag(at_ref, k) for k in keys}
    b_bd = {k: blockdiag(bt_ref, k) for k in keys}
    v_f = {k: blockdiag(v_ref, k) for k in keys}
    v_bd = {k: v_f[k].astype(BF16) for k in keys}
    bonus = {k: jnp.sum(r_bd[k] * k_bd[k] * vec_ref[2:3, sls[k[1]]], axis=1, keepdims=True) for k in keys}
    if merged:
        ar = {k: jnp.concatenate([a_bd[k], r_bd[k]], axis=0).astype(BF16) for k in keys}
        bk = {k: jnp.concatenate([b_bd[k], k_bd[k]], axis=0).astype(BF16) for k in keys}
        amat = {k: _bdot_nt(ar[k], bk[k]) for k in keys}
        a_ab = {k: jnp.where(strict, amat[k][:gc, :gc], 0.0) for k in keys}
        a_ak = {k: jnp.where(strict, amat[k][:gc, gc:], 0.0).astype(BF16) for k in keys}
        a_rbk = {k: jnp.concatenate([jnp.where(incl, amat[k][gc:, :gc], 0.0),
                                     jnp.where(incl, amat[k][gc:, gc:], 0.0)], axis=1).astype(BF16) for k in keys}
    else:
        ab_ = {k: a_bd[k].astype(BF16) for k in keys}
        rb_ = {k: r_bd[k].astype(BF16) for k in keys}
        bb_ = {k: b_bd[k].astype(BF16) for k in keys}
        kb_ = {k: k_bd[k].astype(BF16) for k in keys}
        a_ab = {k: jnp.where(strict, _bdot_nt(ab_[k], bb_[k]), 0.0) for k in keys}
        a_ak = {k: jnp.where(strict, _bdot_nt(ab_[k], kb_[k]), 0.0).astype(BF16) for k in keys}
        a_rb = {k: jnp.where(incl, _bdot_nt(rb_[k], bb_[k]), 0.0).astype(BF16) for k in keys}
        a_rk = {k: jnp.where(incl, _bdot_nt(rb_[k], kb_[k]), 0.0).astype(BF16) for k in keys}
    akv = {k: _bdot(a_ak[k], v_bd[k]) for k in keys}

    p = {k: a_ab[k].astype(BF16) for k in keys}
    x = {k: eye + a_ab[k] for k in keys}
    span = 2
    if span < chunk:
        p = {k: _bdot(p[k], p[k]) for k in keys}
    while span < chunk:
        last = span * 2 >= chunk
        if merged and not last:
            px = {k: _bdot(p[k], jnp.concatenate([p[k].astype(BF16), x[k].astype(BF16)], axis=1)) for k in keys}
            p = {k: px[k][:, :gc] for k in keys}
            x = {k: x[k] + px[k][:, gc:] for k in keys}
        else:
            pb = {k: p[k].astype(BF16) for k in keys}
            x = {k: x[k] + _bdot(pb[k], x[k]) for k in keys}
            if not last:
                p = {k: _bdot(pb[k], pb[k]) for k in keys}
        span *= 2
    tinv = {k: x[k].astype(BF16) for k in keys}

    for sc in range(nsub):
        hs = [ho_ref[0, q] for q in groups]
        hs_b = [h.astype(BF16) for h in hs]
        gl_rows = [gl_ref[sc, :, sl] for sl in sls]
        if merged:
            arh = [_bdot(ar[(sc, q)], hs_b[q]) for q in groups]
            u = [_bdot(tinv[(sc, q)], arh[q][:gc] + akv[(sc, q)]).astype(BF16) for q in groups]
            uv = [jnp.concatenate([u[q], v_bd[(sc, q)]], axis=0) for q in groups]
            y_bd = [arh[q][gc:] + _bdot(a_rbk[(sc, q)], uv[q]) for q in groups]
            for q in groups:
                gl_col = jnp.sum(jnp.where(eye_l, gl_rows[q], 0.0), axis=1, keepdims=True)
                bk_g = jnp.concatenate([b_bd[(sc, q)] * gl_rows[q], k_bd[(sc, q)] * gl_rows[q]], axis=0)
                ho_ref[0, q] = hs[q] * gl_col + _bdot_tn(bk_g, uv[q])
        else:
            ah = [_bdot(ab_[(sc, q)], hs_b[q]) for q in groups]
            rh = [_bdot(rb_[(sc, q)], hs_b[q]) for q in groups]
            u = [_bdot(tinv[(sc, q)], ah[q] + akv[(sc, q)]).astype(BF16) for q in groups]
            y_bd = [rh[q] + _bdot(a_rb[(sc, q)], u[q]) + _bdot(a_rk[(sc, q)], v_bd[(sc, q)]) for q in groups]
            for q in groups:
                gl_col = jnp.sum(jnp.where(eye_l, gl_rows[q], 0.0), axis=1, keepdims=True)
                ho_ref[0, q] = (hs[q] * gl_col + _bdot_tn(b_bd[(sc, q)] * gl_rows[q], u[q])
                                + _bdot_tn(k_bd[(sc, q)] * gl_rows[q], v_bd[(sc, q)]))

        for q in groups:
            sl = sls[q]
            mu = jnp.sum(y_bd[q], axis=1, keepdims=True) * (1.0 / RW_N)
            yc = jnp.where(own, y_bd[q] - mu, 0.0)
            var = jnp.sum(yc * yc, axis=1, keepdims=True) * (1.0 / RW_N)
            tot = (yc * lax.rsqrt(var + RW_LNX_EPS) * vec_ref[0:1, sl] + jnp.where(own, vec_ref[1:2, sl], 0.0)
                   + bonus[(sc, q)] * v_f[(sc, q)])
            y = tot[0:chunk]
            for hh in range(1, nh):
                y = y + tot[hh * chunk:(hh + 1) * chunk]
            y_ref[sc * chunk:(sc + 1) * chunk, sl] = y


def _rwkv_scan_body_old(rt_ref, kt_ref, at_ref, bt_ref, v_ref, gl_ref, h0_ref, vec_ref, ones_ref,
                        y_ref, ho_ref, *, nh, chunk):
    @pl.when(pl.program_id(1) == 0)
    def _():
        ho_ref[...] = h0_ref[...]

    gl_lanes = nh * RW_N
    ng = RW_H // nh
    gc = nh * chunk
    row_head = lax.broadcasted_iota(jnp.int32, (gc, gl_lanes), 0) // chunk
    lane_head = lax.broadcasted_iota(jnp.int32, (gc, gl_lanes), 1) // RW_N
    own = row_head == lane_head
    ri = lax.broadcasted_iota(jnp.int32, (gc, gc), 0)
    ci = lax.broadcasted_iota(jnp.int32, (gc, gc), 1)
    same = (ri // chunk) == (ci // chunk)
    strict = same & ((ri % chunk) > (ci % chunk))
    incl = same & ((ri % chunk) >= (ci % chunk))
    eye = (ri == ci).astype(F32)
    eye_l = (lax.broadcasted_iota(jnp.int32, (gl_lanes, gl_lanes), 0)
             == lax.broadcasted_iota(jnp.int32, (gl_lanes, gl_lanes), 1))
    ones_bd = ones_ref[...]

    def blockdiag(xg):
        xx = jnp.concatenate([xg] * nh, axis=0) if nh > 1 else xg
        return jnp.where(own, xx, 0.0)

    groups = range(ng)
    sls = [slice(q * gl_lanes, (q + 1) * gl_lanes) for q in groups]
    merged = gc == GROUP_ROWS
    gl_rows = [gl_ref[0, :, sl] for sl in sls]
    r_bd = [blockdiag(rt_ref[:, sl]) for sl in sls]
    k_bd = [blockdiag(kt_ref[:, sl]) for sl in sls]
    a_bd = [blockdiag(at_ref[:, sl]) for sl in sls]
    b_bd = [blockdiag(bt_ref[:, sl]) for sl in sls]
    v_bd = [blockdiag(v_ref[:, sl]).astype(BF16) for sl in sls]
    hs = [ho_ref[0, q] for q in groups]
    hs_b = [h.astype(BF16) for h in hs]
    if merged:
        ar = [jnp.concatenate([a_bd[q], r_bd[q]], axis=0).astype(BF16) for q in groups]
        bk = [jnp.concatenate([b_bd[q], k_bd[q]], axis=0).astype(BF16) for q in groups]
        amat = [_bdot_nt(ar[q], bk[q]) for q in groups]
        a_ab = [jnp.where(strict, m[:gc, :gc], 0.0) for m in amat]
        a_ak = [jnp.where(strict, m[:gc, gc:], 0.0).astype(BF16) for m in amat]
        a_rbk = [jnp.concatenate([jnp.where(incl, m[gc:, :gc], 0.0), jnp.where(incl, m[gc:, gc:], 0.0)],
                                 axis=1).astype(BF16) for m in amat]
        arh = [_bdot(ar[q], hs_b[q]) for q in groups]
        ah = [m[:gc] for m in arh]
        rh = [m[gc:] for m in arh]
    else:
        ab_, rb_ = [x.astype(BF16) for x in a_bd], [x.astype(BF16) for x in r_bd]
        bb_, kb_ = [x.astype(BF16) for x in b_bd], [x.astype(BF16) for x in k_bd]
        a_ab = [jnp.where(strict, _bdot_nt(ab_[q], bb_[q]), 0.0) for q in groups]
        a_ak = [jnp.where(strict, _bdot_nt(ab_[q], kb_[q]), 0.0).astype(BF16) for q in groups]
        a_rb = [jnp.where(incl, _bdot_nt(rb_[q], bb_[q]), 0.0).astype(BF16) for q in groups]
        a_rk = [jnp.where(incl, _bdot_nt(rb_[q], kb_[q]), 0.0).astype(BF16) for q in groups]
        ah = [_bdot(ab_[q], hs_b[q]) for q in groups]
        rh = [_bdot(rb_[q], hs_b[q]) for q in groups]
    akv = [_bdot(a_ak[q], v_bd[q]) for q in groups]

    p = [m.astype(BF16) for m in a_ab]
    x = [eye + m for m in a_ab]
    span = 2
    if span < chunk:
        p = [_bdot(p[q], p[q]) for q in groups]
    while span < chunk:
        last = span * 2 >= chunk
        if merged and not last:
            px = [_bdot(p[q], jnp.concatenate([p[q].astype(BF16), x[q].astype(BF16)], axis=1)) for q in groups]
            p = [m[:, :gc] for m in px]
            x = [x[q] + px[q][:, gc:] for q in groups]
        else:
            pb = [m.astype(BF16) for m in p]
            x = [x[q] + _bdot(pb[q], x[q]) for q in groups]
            if not last:
                p = [_bdot(pb[q], pb[q]) for q in groups]
        span *= 2

    u = [_bdot(x[q], ah[q] + akv[q]).astype(BF16) for q in groups]
    if merged:
        uv = [jnp.concatenate([u[q], v_bd[q]], axis=0) for q in groups]
        y_bd = [rh[q] + _bdot(a_rbk[q], uv[q]) for q in groups]
        for q in groups:
            gl_col = jnp.sum(jnp.where(eye_l, gl_rows[q], 0.0), axis=1, keepdims=True)
            bk_g = jnp.concatenate([b_bd[q] * gl_rows[q], k_bd[q] * gl_rows[q]], axis=0)
            ho_ref[0, q] = hs[q] * gl_col + _bdot_tn(bk_g, uv[q])
    else:
        y_bd = [rh[q] + _bdot(a_rb[q], u[q]) + _bdot(a_rk[q], v_bd[q]) for q in groups]
        for q in groups:
            gl_col = jnp.sum(jnp.where(eye_l, gl_rows[q], 0.0), axis=1, keepdims=True)
            ho_ref[0, q] = (hs[q] * gl_col + _bdot_tn(b_bd[q] * gl_rows[q], u[q])
                            + _bdot_tn(k_bd[q] * gl_rows[q], v_bd[q]))

    for q in groups:
        sl = sls[q]
        y = y_bd[q][0:chunk]
        for hh in range(1, nh):
            y = y + y_bd[q][hh * chunk:(hh + 1) * chunk]
        mu = _lane_group_sum(y, ones_bd) * (1.0 / RW_N)
        yc = y - mu
        var = _lane_group_sum(yc * yc, ones_bd) * (1.0 / RW_N)
        yn = yc * lax.rsqrt(var + RW_LNX_EPS) * vec_ref[0:1, sl] + vec_ref[1:2, sl]
        bonus = _lane_group_sum(rt_ref[:, sl] * kt_ref[:, sl] * vec_ref[2:3, sl], ones_bd) * v_ref[:, sl]
        y_ref[:, sl] = yn + bonus


RWKV_SUBCHUNKS = 2


def _rwkv_layer(x2d, shift_prev, s0, v_first, w, ri, nw, b, t):
    n = b * t
    chunk = _chunk_of(t)
    tl = _tiling(b, t, 256)
    G, J, R, P = tl["G"], tl["J"], tl["R"], tl["P"]
    has_vres = v_first is not None
    vi = ri - 1
    bc = chunk if chunk == 64 else R
    tri = _chunk_masks(chunk, bc)
    li = jnp.arange(LANES)
    ones_bd = ((li[:, None] // RW_N) == (li[None, :] // RW_N)).astype(BF16)
    ones_bd = jnp.concatenate([ones_bd, ones_bd], axis=0)
    zero = jnp.zeros((D_MODEL,), F32)
    vec = jnp.stack([*(w["rw_mu"][ri][i] for i in range(6)), w["rw_w0"][ri], w["rw_a0"][ri], w["rw_kk"][ri],
                     w["rw_ka"][ri], nw[0], w["rw_v0"][vi] if has_vres else zero, zero, zero, zero, zero])
    row = lambda g, j: (g * J + j, 0)
    row_spec = pl.BlockSpec((R, D_MODEL), row)
    ins = [x2d, _pad_state(shift_prev[:, None, :])]
    specs = [row_spec, pl.BlockSpec((P, D_MODEL), lambda g, j: (g, 0))]
    if has_vres:
        ins.append(v_first)
        specs.append(row_spec)
    wl = [vec, w["rw_wrkv"][ri].astype(BF16), w["rw_w1"][ri].astype(BF16), w["rw_w2"][ri].astype(BF16),
          w["rw_a1"][ri].astype(BF16), w["rw_a2"][ri].astype(BF16)]
    if has_vres:
        wl += [w["rw_v1"][vi].astype(BF16), w["rw_v2"][vi].astype(BF16)]
    wl += [w["rw_g1"][ri].astype(BF16), w["rw_g2"][ri].astype(BF16), tri, ones_bd]
    ins += wl
    specs += [_const_spec(a.shape) for a in wl]
    nc_tile = R // chunk
    outs = pl.pallas_call(
        functools.partial(_rwkv_proj_body, has_vres=has_vres, chunk=chunk),
        grid=(G, J),
        in_specs=specs,
        out_specs=[row_spec] * 6 + [pl.BlockSpec((nc_tile, 1, D_MODEL), lambda g, j: (g * J + j, 0, 0)),
                                    pl.BlockSpec((P, D_MODEL), lambda g, j: (g, 0))],
        out_shape=[jax.ShapeDtypeStruct((n, D_MODEL), F32)] * 6
        + [jax.ShapeDtypeStruct((n // chunk, 1, D_MODEL), F32), jax.ShapeDtypeStruct((b * SUBLANES, D_MODEL), F32)],
        scratch_shapes=[pltpu.VMEM((P, D_MODEL), F32)],
        compiler_params=_params(2),
        name="rwkv_proj",
    )(*ins)
    rt, kt, at, bt, v, g, gl, hl = outs
    shift_new = hl.reshape(b, SUBLANES, D_MODEL)[:, -1]

    nh = LANES // RW_N
    ng = RW_H // nh
    gl_lanes = nh * RW_N
    hkv = jnp.swapaxes(s0, -1, -2).reshape(b, ng, nh, RW_N, RW_N)
    zblk = jnp.zeros((b, ng, RW_N, RW_N), F32)
    h0 = jnp.concatenate(
        [jnp.concatenate([hkv[:, :, i] if i == jj else zblk for jj in range(nh)], axis=-1) for i in range(nh)],
        axis=-2)
    svec = jnp.stack([w["rw_lnx_w"][ri], w["rw_lnx_b"][ri], w["rw_rk"][ri].reshape(D_MODEL),
                      zero, zero, zero, zero, zero])
    nsub = RWKV_SUBCHUNKS if (t // chunk) % RWKV_SUBCHUNKS == 0 else 1
    nct = t // (chunk * nsub)
    crow = lambda bb, j: (bb * nct + j, 0)
    cspec = pl.BlockSpec((chunk * nsub, D_MODEL), crow)
    hspec = pl.BlockSpec((1, ng, gl_lanes, gl_lanes), lambda bb, j: (bb, 0, 0, 0))
    y, hout = pl.pallas_call(
        functools.partial(_rwkv_scan_body, nh=nh, chunk=chunk, nsub=nsub),
        grid=(b, nct),
        in_specs=[cspec] * 5 + [pl.BlockSpec((nsub, 1, D_MODEL), lambda bb, j: (bb * nct + j, 0, 0)), hspec,
                                _const_spec((8, D_MODEL))],
        out_specs=[cspec, hspec],
        out_shape=[jax.ShapeDtypeStruct((n, D_MODEL), F32),
                   jax.ShapeDtypeStruct((b, ng, gl_lanes, gl_lanes), F32)],
        compiler_params=_params(2),
        name="rwkv_scan",
    )(rt, kt, at, bt, v, gl, h0, svec)
    s_new = jnp.stack([hout[:, :, i * RW_N:(i + 1) * RW_N, i * RW_N:(i + 1) * RW_N] for i in range(nh)],
                      axis=2)
    s_new = jnp.swapaxes(s_new, -1, -2).reshape(b, RW_H, RW_N, RW_N)
    nwp = jnp.concatenate([nw[1:2], jnp.zeros((7, D_MODEL), F32)])
    x_new = _outproj(x2d, y, g, w["rw_wo"][ri].astype(BF16), nwp, "rwkv_out")
    return x_new, shift_new, s_new, (v if not has_vres else v_first)


def _rope_lanes(x, tab_ref):
    half = MLA_ROPE // 2
    return (x * tab_ref[0] + pltpu.roll(x, LANES - half, 1) * tab_ref[1] + pltpu.roll(x, half, 1) * tab_ref[2])


def _mla_proj_body(x_ref, nw_ref, tab_ref, winq_ref, winc_ref, wink_ref, qn_ref, kvn_ref, wqn_ref, wqr_ref,
                   wuk_ref, c_ref, kr_ref, kcat_ref, qcat_ref):
    h = _rms(x_ref[...], nw_ref[0:1, :]).astype(BF16)
    cq = _rms(jnp.dot(h, winq_ref[...], preferred_element_type=F32), qn_ref[...]).astype(BF16)
    c = _rms(jnp.dot(h, winc_ref[...], preferred_element_type=F32), kvn_ref[...])
    kr = _rope_lanes(jnp.dot(h, wink_ref[...], preferred_element_type=F32), tab_ref)
    c_ref[...] = c
    kr_ref[...] = kr
    adt = kcat_ref.dtype
    kcat_ref[:, 0:MLA_KV_LORA] = c.astype(adt)
    kcat_ref[:, MLA_KV_LORA:MLA_QK] = kr.astype(adt)
    qn = jnp.dot(cq, wqn_ref[...], preferred_element_type=F32).astype(BF16)
    qr = jnp.dot(cq, wqr_ref[...], preferred_element_type=F32)
    for pr in range(MLA_H // 2):
        ql = jnp.dot(qn[:, pr * LANES:(pr + 1) * LANES], wuk_ref[pr], preferred_element_type=F32) * MLA_SCALE
        qcat_ref[2 * pr, :, 0:MLA_KV_LORA] = ql[:, :MLA_KV_LORA].astype(adt)
        qcat_ref[2 * pr + 1, :, 0:MLA_KV_LORA] = ql[:, MLA_KV_LORA:].astype(adt)
    for hh in range(MLA_H):
        qro = _rope_lanes(qr[:, hh * LANES:(hh + 1) * LANES], tab_ref) * MLA_SCALE
        qcat_ref[hh, :, MLA_KV_LORA:MLA_QK] = qro.astype(adt)


MLA_TQ = 128
MLA_TK = 256
MLA_SPLIT = 8


def _mla_prompt_body(q_ref, k_ref, o_ref, m_ref, l_ref, acc_ref):
    i = pl.program_id(1)
    rows = MLA_H * MLA_TQ
    q = q_ref[...].reshape(rows, MLA_QK)
    m_ref[...] = jnp.full((rows, LANES), -jnp.inf, F32)
    l_ref[...] = jnp.zeros((rows, LANES), F32)
    acc_ref[...] = jnp.zeros((rows, MLA_KV_LORA), F32)
    ones = jnp.ones((MLA_TK, LANES), BF16)
    reps = MLA_TK // LANES

    sub = rows // MLA_SPLIT

    def block(k0, masked):
        kblk = k_ref[pl.ds(k0, MLA_TK), :]
        ss = [lax.dot_general(q[g * sub:(g + 1) * sub], kblk, (((1,), (1,)), ((), ())),
                              preferred_element_type=F32) for g in range(MLA_SPLIT)]
        for g in range(MLA_SPLIT):
            rs = slice(g * sub, (g + 1) * sub)
            s = ss[g]
            if masked:
                qpos = i * MLA_TQ + (g * sub + lax.broadcasted_iota(jnp.int32, (sub, MLA_TK), 0)) % MLA_TQ
                kpos = k0 + lax.broadcasted_iota(jnp.int32, (sub, MLA_TK), 1)
                s = jnp.where(kpos <= qpos, s, -jnp.inf)
            m_old = m_ref[rs, :]
            m_new = jnp.maximum(m_old, jnp.max(s, axis=1, keepdims=True))
            alpha = jnp.exp(m_old - m_new)
            p = jnp.exp(s - jnp.concatenate([m_new] * reps, axis=1)).astype(BF16)
            l_ref[rs, :] = l_ref[rs, :] * alpha + jnp.dot(p, ones, preferred_element_type=F32)
            acc_ref[rs, :] = (acc_ref[rs, :] * jnp.concatenate([alpha] * (MLA_KV_LORA // LANES), axis=1)
                              + jnp.dot(p, kblk[:, :MLA_KV_LORA], preferred_element_type=F32))
            m_ref[rs, :] = m_new

    def full_step(kb, carry):
        block(pl.multiple_of(kb * MLA_TK, MLA_TK), False)
        return carry

    n_full = (i * MLA_TQ) // MLA_TK
    lax.fori_loop(0, n_full, full_step, 0)
    block(pl.multiple_of(n_full * MLA_TK, MLA_TK), True)
    o = acc_ref[...] / jnp.concatenate([l_ref[...]] * (MLA_KV_LORA // LANES), axis=1)
    o_ref[...] = o.reshape(MLA_H, MLA_TQ, MLA_KV_LORA).astype(BF16)


MLA_PP = 16
MLA_GROUPS = 2


def _mla_sample_body(pt_ref, q_ref, kn_ref, *rest):
    lat_refs = rest[:MLA_PP]
    kro_refs = rest[MLA_PP:2 * MLA_PP]
    o_ref, m_ref, l_ref, acc_ref = rest[2 * MLA_PP:]
    j = pl.program_id(1)
    t = q_ref.shape[1]
    rows = MLA_H * t
    q = q_ref[...].reshape(rows, MLA_QK).astype(BF16)
    ql = q[:, :MLA_KV_LORA]
    qr = q[:, MLA_KV_LORA:MLA_KV_LORA + MLA_ROPE]

    @pl.when(j == 0)
    def _():
        m_ref[...] = jnp.full(m_ref.shape, -jnp.inf, F32)
        l_ref[...] = jnp.zeros(l_ref.shape, F32)
        acc_ref[...] = jnp.zeros(acc_ref.shape, F32)

    vrep = MLA_KV_LORA // LANES

    def update(g, s, vals, row_sum):
        m_old = m_ref[g]
        m_new = jnp.maximum(m_old, jnp.max(s, axis=1, keepdims=True))
        alpha = jnp.exp(m_old - m_new)
        if s.shape[1] % LANES == 0:
            p = jnp.exp(s - jnp.concatenate([m_new] * (s.shape[1] // LANES), axis=1)).astype(BF16)
        else:
            p = jnp.exp(s - m_new[:, 0:1]).astype(BF16)
        l_ref[g] = l_ref[g] * alpha + row_sum(p)
        acc_ref[g] = (acc_ref[g] * jnp.concatenate([alpha] * vrep, axis=1)
                      + jnp.dot(p, vals, preferred_element_type=F32))
        m_ref[g] = m_new

    per = MLA_PP // MLA_GROUPS
    ones = jnp.ones((per * lat_refs[0].shape[1], LANES), BF16)
    scores, values = [], []
    for g in range(MLA_GROUPS):
        cbs, s_parts = [], []
        for pp in range(g * per, (g + 1) * per):
            cb = lat_refs[pp][0].astype(BF16)
            kbt = kro_refs[pp][0].astype(BF16)
            s_parts.append(lax.dot_general(ql, cb, (((1,), (1,)), ((), ())), preferred_element_type=F32)
                           + jnp.dot(qr, kbt, preferred_element_type=F32))
            cbs.append(cb)
        scores.append(jnp.concatenate(s_parts, axis=1))
        values.append(jnp.concatenate(cbs, axis=0))
    for g in range(MLA_GROUPS):
        update(g, scores[g], values[g], lambda p: jnp.dot(p, ones, preferred_element_type=F32))

    @pl.when(j == pl.num_programs(1) - 1)
    def _():
        kn = kn_ref[...].astype(BF16)
        s = lax.dot_general(q, kn, (((1,), (1,)), ((), ())), preferred_element_type=F32)
        qpos = lax.broadcasted_iota(jnp.int32, (rows, t), 0) % t
        kpos = lax.broadcasted_iota(jnp.int32, (rows, t), 1)
        s = jnp.where(kpos <= qpos, s, -jnp.inf)
        update(0, s, kn[:, :MLA_KV_LORA], lambda p: jnp.sum(p.astype(F32), axis=1, keepdims=True))
        m_all = m_ref[0]
        for g in range(1, MLA_GROUPS):
            m_all = jnp.maximum(m_all, m_ref[g])
        l_all = jnp.zeros((rows, LANES), F32)
        acc = jnp.zeros((rows, MLA_KV_LORA), F32)
        for g in range(MLA_GROUPS):
            wgt = jnp.exp(m_ref[g] - m_all)
            l_all = l_all + l_ref[g] * wgt
            acc = acc + acc_ref[g] * jnp.concatenate([wgt] * vrep, axis=1)
        o = acc / jnp.concatenate([l_all] * vrep, axis=1)
        o_ref[...] = o.reshape(MLA_H, t, MLA_KV_LORA).astype(o_ref.dtype)


def _mla_out_body(x_ref, o_ref, wuv_ref, wo_ref, nw_ref, xo_ref):
    parts = []
    for pr in range(MLA_H // 2):
        wp = wuv_ref[pr]
        parts.append(jnp.dot(o_ref[2 * pr].astype(BF16), wp[:MLA_KV_LORA], preferred_element_type=F32)
                     + jnp.dot(o_ref[2 * pr + 1].astype(BF16), wp[MLA_KV_LORA:], preferred_element_type=F32))
    v = jnp.concatenate(parts, axis=1).astype(BF16)
    o = jnp.dot(v, wo_ref[...], preferred_element_type=F32)
    xo_ref[...] = x_ref[...] + _rms(o, nw_ref[0:1, :])


def _mla_layer(x2d, pos, w, mi, nw, b, t, paged):
    n = b * t
    tl = _tiling(b, t, 512)
    G, J, R = tl["G"], tl["J"], tl["R"]
    adt = BF16 if t % 16 == 0 else F32
    half = MLA_ROPE // 2
    inv = ROPE_THETA ** (-jnp.arange(half, dtype=F32) / half)
    ang = pos.astype(F32)[:, None] * inv[None, :]
    cos, sin = jnp.cos(ang), jnp.sin(ang)
    zpad = jnp.zeros((t, LANES - MLA_ROPE), F32)
    zh = jnp.zeros((t, half), F32)
    tab = jnp.stack([jnp.concatenate([cos, cos, zpad], 1), jnp.concatenate([-sin, zh, zpad], 1),
                     jnp.concatenate([zh, sin, zpad], 1)])
    if G == 1:
        tab = jnp.tile(tab, (1, b, 1))
    w_in = w["mla_w_in"][mi]
    winq = w_in[:, :MLA_Q_LORA].astype(BF16)
    winc = w_in[:, MLA_Q_LORA:MLA_Q_LORA + MLA_KV_LORA].astype(BF16)
    wink = jnp.pad(w_in[:, MLA_Q_LORA + MLA_KV_LORA:], ((0, 0), (0, LANES - MLA_ROPE))).astype(BF16)
    wqb = w["mla_w_qb"][mi].reshape(MLA_Q_LORA, MLA_H, MLA_NOPE + MLA_ROPE)
    wqn = wqb[:, :, :MLA_NOPE].reshape(MLA_Q_LORA, MLA_H * MLA_NOPE).astype(BF16)
    wqr = jnp.pad(wqb[:, :, MLA_NOPE:], ((0, 0), (0, 0), (0, LANES - MLA_ROPE))
                  ).reshape(MLA_Q_LORA, MLA_H * LANES).astype(BF16)
    wuk = jnp.transpose(w["mla_w_uk"][mi], (1, 2, 0)).reshape(MLA_H // 2, 2, MLA_NOPE, MLA_KV_LORA)
    wuk_bd = jnp.einsum("pinc,ij->pinjc", wuk, jnp.eye(2, dtype=F32)).reshape(
        MLA_H // 2, 2 * MLA_NOPE, 2 * MLA_KV_LORA).astype(BF16)
    wuv = jnp.transpose(w["mla_w_uv"][mi], (1, 0, 2)).reshape(MLA_H // 2, 2, MLA_KV_LORA, MLA_V)
    wuv_bd = jnp.einsum("picv,ij->picjv", wuv, jnp.eye(2, dtype=F32)).reshape(
        MLA_H // 2, 2 * MLA_KV_LORA, 2 * MLA_V).astype(BF16)
    nwa = jnp.concatenate([nw[0:1], jnp.zeros((7, D_MODEL), F32)])
    nwb = jnp.concatenate([nw[1:2], jnp.zeros((7, D_MODEL), F32)])
    row = lambda g, j: (g * J + j, 0)
    wl = [winq, winc, wink, w["mla_q_norm"][mi][None, :], w["mla_kv_norm"][mi][None, :], wqn, wqr, wuk_bd]
    c, kr, kcat, qcat = pl.pallas_call(
        _mla_proj_body,
        grid=(G, J),
        in_specs=[pl.BlockSpec((R, D_MODEL), row), _const_spec((8, D_MODEL)),
                  pl.BlockSpec((3, R, LANES), lambda g, j: (0, j, 0))] + [_const_spec(a.shape) for a in wl],
        out_specs=[pl.BlockSpec((R, MLA_KV_LORA), row), pl.BlockSpec((R, LANES), row),
                   pl.BlockSpec((R, MLA_QK), row), pl.BlockSpec((MLA_H, R, MLA_QK), lambda g, j: (0, g * J + j, 0))],
        out_shape=[jax.ShapeDtypeStruct((n, MLA_KV_LORA), F32), jax.ShapeDtypeStruct((n, LANES), F32),
                   jax.ShapeDtypeStruct((n, MLA_QK), adt), jax.ShapeDtypeStruct((MLA_H, n, MLA_QK), adt)],
        compiler_params=_params(2),
        name="mla_proj",
    )(x2d, nwa, tab, *wl)

    if paged is None:
        nq = t // MLA_TQ
        rows = MLA_H * MLA_TQ
        o = pl.pallas_call(
            _mla_prompt_body,
            grid=(b, nq),
            in_specs=[pl.BlockSpec((MLA_H, MLA_TQ, MLA_QK), lambda bb, i: (0, bb * nq + i, 0)),
                      pl.BlockSpec((t, MLA_QK), lambda bb, i: (bb, 0))],
            out_specs=pl.BlockSpec((MLA_H, MLA_TQ, MLA_KV_LORA), lambda bb, i: (0, bb * nq + i, 0)),
            out_shape=jax.ShapeDtypeStruct((MLA_H, n, MLA_KV_LORA), BF16),
            scratch_shapes=[pltpu.VMEM((rows, LANES), F32), pltpu.VMEM((rows, LANES), F32),
                            pltpu.VMEM((rows, MLA_KV_LORA), F32)],
            compiler_params=_params(2),
            name="mla_attend_prompt",
        )(qcat, kcat)
    else:
        pages_c, pages_kr, page_table = paged
        page = pages_c.shape[1]
        npg = page_table.shape[1]
        assert npg % MLA_PP == 0
        rows = MLA_H * t

        def page_map(pp):
            return lambda bb, j, pt: (pt[bb, j * MLA_PP + pp], 0, 0)

        grid_spec = pltpu.PrefetchScalarGridSpec(
            num_scalar_prefetch=1,
            grid=(b, npg // MLA_PP),
            in_specs=[pl.BlockSpec((MLA_H, t, MLA_QK), lambda bb, j, pt: (0, bb, 0)),
                      pl.BlockSpec((t, MLA_QK), lambda bb, j, pt: (bb, 0))]
            + [pl.BlockSpec((1, page, MLA_KV_LORA), page_map(pp)) for pp in range(MLA_PP)]
            + [pl.BlockSpec((1, MLA_ROPE, page), page_map(pp)) for pp in range(MLA_PP)],
            out_specs=pl.BlockSpec((MLA_H, t, MLA_KV_LORA), lambda bb, j, pt: (0, bb, 0)),
            scratch_shapes=[pltpu.VMEM((MLA_GROUPS, rows, LANES), F32), pltpu.VMEM((MLA_GROUPS, rows, LANES), F32),
                            pltpu.VMEM((MLA_GROUPS, rows, MLA_KV_LORA), F32)],
        )
        o = pl.pallas_call(
            _mla_sample_body,
            grid_spec=grid_spec,
            out_shape=jax.ShapeDtypeStruct((MLA_H, n, MLA_KV_LORA), adt),
            compiler_params=_params(2),
            name="mla_attend_sample",
        )(page_table, qcat, kcat, *([pages_c] * MLA_PP), *([jnp.swapaxes(pages_kr, 1, 2)] * MLA_PP))

    Ro = min(n, 512)
    x_new = pl.pallas_call(
        _mla_out_body,
        grid=(n // Ro,),
        in_specs=[pl.BlockSpec((Ro, D_MODEL), lambda i: (i, 0)),
                  pl.BlockSpec((MLA_H, Ro, MLA_KV_LORA), lambda i: (0, i, 0)),
                  _const_spec(wuv_bd.shape), _const_spec((MLA_H * MLA_V, D_MODEL)), _const_spec((8, D_MODEL))],
        out_specs=pl.BlockSpec((Ro, D_MODEL), lambda i: (i, 0)),
        out_shape=jax.ShapeDtypeStruct((n, D_MODEL), F32),
        compiler_params=_params(1),
        name="mla_out",
    )(x2d, o, wuv_bd, w["mla_wo"][mi].astype(BF16), nwb)
    return x_new, c.reshape(b, t, MLA_KV_LORA), kr[:, :MLA_ROPE].reshape(b, t, MLA_ROPE)


GDN_CW = 512


def _gdn_proj_body(x_ref, prev_ref, nw_ref, wqkv_ref, wz_ref, wbg_ref, cw_ref, gvec_ref, tri_ref,
                   q_ref, k_ref, v_ref, z_ref, beta_ref, gc_ref, st_ref, carry_ref):
    @pl.when(pl.program_id(1) == 0)
    def _():
        carry_ref[...] = prev_ref[...]

    rows = x_ref.shape[0]
    p = carry_ref.shape[0]
    h = _rms(x_ref[...], nw_ref[0:1, :]).astype(BF16)
    z_ref[...] = jnp.dot(h, wz_ref[...], preferred_element_type=F32)
    bg = jnp.dot(h, wbg_ref[...], preferred_element_type=F32)
    beta_ref[...] = _sigmoid(bg)
    g = -jnp.exp(gvec_ref[0:1, :]) * _softplus(bg + gvec_ref[1:2, :])
    gc_ref[...] = _chunk_cumsum(g, tri_ref[...])
    nch = GDN_CONV_DIM // GDN_CW

    def up(c):
        return jnp.dot(h, wqkv_ref[:, c * GDN_CW:(c + 1) * GDN_CW], preferred_element_type=F32)

    u_nxt = up(0)
    for c in range(nch):
        sl = slice(c * GDN_CW, (c + 1) * GDN_CW)
        u = u_nxt
        u_nxt = up(c + 1) if c + 1 < nch else None
        prev = carry_ref[:, sl]
        y = cw_ref[3:4, sl] * u
        for s in range(1, GDN_CONV):
            y = y + cw_ref[3 - s:4 - s, sl] * _shift_rows(u, prev, s)
        tail = u[rows - p:, :]
        carry_ref[:, sl] = tail
        st_ref[:, sl] = tail
        y = _silu(y)
        off = c * GDN_CW
        if off < 2 * GDN_QK_DIM:
            dst, base, scale = (q_ref, off, GDN_DK ** -0.5) if off < GDN_QK_DIM else (k_ref, off - GDN_QK_DIM, 1.0)
            for hh in range(GDN_CW // GDN_DK):
                yh = y[:, hh * GDN_DK:(hh + 1) * GDN_DK]
                yh = yh * lax.rsqrt(jnp.sum(yh * yh, axis=-1, keepdims=True) + 1e-6)
                dst[:, base + hh * GDN_DK:base + (hh + 1) * GDN_DK] = yh * scale if scale != 1.0 else yh
        else:
            v_ref[:, off - 2 * GDN_QK_DIM:off - 2 * GDN_QK_DIM + GDN_CW] = y


def _gdn_chunk_body(q_ref, k_ref, v_ref, z_ref, gc_ref, beta_ref, s0_ref, nw_ref,
                    o_ref, so_ref, *, nh, chunk, nsub):
    @pl.when(pl.program_id(1) == 0)
    def _():
        so_ref[...] = s0_ref[...]

    ng = GDN_V_H // nh
    gc = nh * chunk
    rep = GDN_V_H // GDN_QK_H
    ri = lax.broadcasted_iota(jnp.int32, (gc, gc), 0)
    ci = lax.broadcasted_iota(jnp.int32, (gc, gc), 1)
    same = (ri // chunk) == (ci // chunk)
    strict = same & ((ri % chunk) > (ci % chunk))
    incl = same & ((ri % chunk) >= (ci % chunk))
    last = same & ((ci % chunk) == chunk - 1)
    eye = (ri == ci).astype(F32)
    row_head = lax.broadcasted_iota(jnp.int32, (gc, GDN_DK), 0) // chunk
    groups = range(ng)
    heads = [[q * nh + i for i in range(nh)] for q in groups]
    keys = [(sc, q) for sc in range(nsub) for q in groups]

    def stack(ref, sc, hds, width):
        parts = [ref[sc * chunk:(sc + 1) * chunk, hd * width:(hd + 1) * width] for hd in hds]
        return parts[0] if len(parts) == 1 else jnp.concatenate(parts, axis=0)

    def col(ref, sc, lanes):
        parts = [ref[sc * chunk:(sc + 1) * chunk, ln:ln + 1] for ln in lanes]
        return parts[0] if len(parts) == 1 else jnp.concatenate(parts, axis=0)

    k_st = {k: stack(k_ref, k[0], [hd // rep for hd in heads[k[1]]], GDN_DK) for k in keys}
    q_st = {k: stack(q_ref, k[0], [hd // rep for hd in heads[k[1]]], GDN_DK) for k in keys}
    v_st = {k: stack(v_ref, k[0], heads[k[1]], GDN_DV) for k in keys}
    gcol = {k: col(gc_ref, k[0], [GDN_V_H + hd for hd in heads[k[1]]]) for k in keys}
    bcol = {k: col(beta_ref, k[0], heads[k[1]]) for k in keys}
    grow = {k: jnp.sum(jnp.where(ri == ci, gcol[k], 0.0), axis=0, keepdims=True) for k in keys}
    k_b = {k: k_st[k].astype(BF16) for k in keys}
    kq = {k: _bdot_nt(jnp.concatenate([k_b[k], q_st[k].astype(BF16)], axis=0), k_b[k]) for k in keys}
    decay = {k: jnp.exp(jnp.where(incl, gcol[k] - grow[k], -jnp.inf)) for k in keys}
    a = {k: jnp.where(strict, kq[k][:gc] * bcol[k] * decay[k], 0.0) for k in keys}
    aqk = {k: jnp.where(incl, kq[k][gc:] * decay[k], 0.0).astype(BF16) for k in keys}

    p = {k: (-a[k]).astype(BF16) for k in keys}
    x = {k: eye - a[k] for k in keys}
    span = 2
    if span < chunk:
        p = {k: _bdot(p[k], p[k]) for k in keys}
    while span < chunk:
        if span * 2 < chunk:
            px = {k: _bdot(p[k], jnp.concatenate([p[k].astype(BF16), x[k].astype(BF16)], axis=1)) for k in keys}
            p = {k: px[k][:, :gc] for k in keys}
            x = {k: x[k] + px[k][:, gc:] for k in keys}
        else:
            x = {k: x[k] + _bdot(p[k], x[k]) for k in keys}
        span *= 2

    egc = {k: jnp.exp(gcol[k]) for k in keys}
    uw = {k: _bdot(x[k], jnp.concatenate([v_st[k] * bcol[k], k_st[k] * (bcol[k] * egc[k])], axis=1)) for k in keys}
    glast = {k: jnp.sum(jnp.where(last, grow[k], 0.0), axis=1, keepdims=True) for k in keys}
    kg = {k: k_st[k] * jnp.exp(glast[k] - gcol[k]) for k in keys}
    qg = {k: q_st[k] * egc[k] for k in keys}

    for sc in range(nsub):
        states = [[so_ref[0, hd] for hd in heads[q]] for q in groups]
        wq_s = []
        for q in groups:
            wm = uw[(sc, q)][:, GDN_DV:]
            parts = []
            for i in range(nh):
                rs = slice(i * chunk, (i + 1) * chunk)
                parts.append(_bdot(jnp.concatenate([wm[rs], qg[(sc, q)][rs]], axis=0), states[q][i]))
            wq_s.append(parts)
        v_new, o_st = [], []
        for q in groups:
            ws = jnp.concatenate([m[:chunk] for m in wq_s[q]], axis=0) if nh > 1 else wq_s[q][0][:chunk]
            qs = jnp.concatenate([m[chunk:] for m in wq_s[q]], axis=0) if nh > 1 else wq_s[q][0][chunk:]
            vn = (uw[(sc, q)][:, :GDN_DV] - ws).astype(BF16)
            v_new.append(vn)
            o_st.append(qs + _bdot(aqk[(sc, q)], vn))
        for q in groups:
            for i, hd in enumerate(heads[q]):
                rs = slice(i * chunk, (i + 1) * chunk)
                gl_h = jnp.exp(glast[(sc, q)][i * chunk:i * chunk + 1, :])
                if chunk % 16 == 0:
                    upd = _bdot_tn(kg[(sc, q)][rs], v_new[q][rs])
                else:
                    upd = _bdot_tn(jnp.where(row_head == i, kg[(sc, q)], 0.0), v_new[q])
                so_ref[0, hd] = states[q][i] * gl_h + upd
        for q in groups:
            z_st = stack(z_ref, sc, heads[q], GDN_DV)
            og = _rms(o_st[q], nw_ref[0:1, :]) * _silu(z_st)
            for i, hd in enumerate(heads[q]):
                o_ref[sc * chunk:(sc + 1) * chunk, hd * GDN_DV:(hd + 1) * GDN_DV] = og[i * chunk:(i + 1) * chunk]


def _gdn_chunk_body_old(q_ref, k_ref, v_ref, z_ref, gc_ref, beta_ref, s0_ref, nw_ref,
                        o_ref, so_ref, *, nh, chunk):
    @pl.when(pl.program_id(1) == 0)
    def _():
        so_ref[...] = s0_ref[...]

    ng = GDN_V_H // nh
    gc = nh * chunk
    rep = GDN_V_H // GDN_QK_H
    ri = lax.broadcasted_iota(jnp.int32, (gc, gc), 0)
    ci = lax.broadcasted_iota(jnp.int32, (gc, gc), 1)
    same = (ri // chunk) == (ci // chunk)
    strict = same & ((ri % chunk) > (ci % chunk))
    incl = same & ((ri % chunk) >= (ci % chunk))
    last = same & ((ci % chunk) == chunk - 1)
    eye = (ri == ci).astype(F32)
    row_head = lax.broadcasted_iota(jnp.int32, (gc, GDN_DK), 0) // chunk

    def stack(ref, heads, width):
        parts = [ref[:, hd * width:(hd + 1) * width] for hd in heads]
        return parts[0] if len(parts) == 1 else jnp.concatenate(parts, axis=0)

    groups = range(ng)
    heads = [[q * nh + i for i in range(nh)] for q in groups]
    k_st = [stack(k_ref, [hd // rep for hd in heads[q]], GDN_DK) for q in groups]
    q_st = [stack(q_ref, [hd // rep for hd in heads[q]], GDN_DK) for q in groups]
    v_st = [stack(v_ref, heads[q], GDN_DV) for q in groups]
    def col(ref, lanes):
        parts = [ref[:, ln:ln + 1] for ln in lanes]
        return parts[0] if len(parts) == 1 else jnp.concatenate(parts, axis=0)

    gcol = [col(gc_ref, [GDN_V_H + hd for hd in heads[q]]) for q in groups]
    bcol = [col(beta_ref, heads[q]) for q in groups]
    grow = [jnp.sum(jnp.where(ri == ci, gcol[q], 0.0), axis=0, keepdims=True) for q in groups]
    k_b = [x.astype(BF16) for x in k_st]
    kq = [_bdot_nt(jnp.concatenate([k_b[q], q_st[q].astype(BF16)], axis=0), k_b[q]) for q in groups]
    decay = [jnp.exp(jnp.where(incl, gcol[q] - grow[q], -jnp.inf)) for q in groups]
    a = [jnp.where(strict, kq[q][:gc] * bcol[q] * decay[q], 0.0) for q in groups]
    aqk = [jnp.where(incl, kq[q][gc:] * decay[q], 0.0).astype(BF16) for q in groups]

    p = [(-m).astype(BF16) for m in a]
    x = [eye - m for m in a]
    span = 2
    if span < chunk:
        p = [_bdot(p[q], p[q]) for q in groups]
    while span < chunk:
        if span * 2 < chunk:
            px = [_bdot(p[q], jnp.concatenate([p[q].astype(BF16), x[q].astype(BF16)], axis=1)) for q in groups]
            p = [m[:, :gc] for m in px]
            x = [x[q] + px[q][:, gc:] for q in groups]
        else:
            x = [x[q] + _bdot(p[q], x[q]) for q in groups]
        span *= 2

    egc = [jnp.exp(g) for g in gcol]
    uw = [_bdot(x[q], jnp.concatenate([v_st[q] * bcol[q], k_st[q] * (bcol[q] * egc[q])], axis=1)) for q in groups]
    glast = [jnp.sum(jnp.where(last, grow[q], 0.0), axis=1, keepdims=True) for q in groups]
    kg = [k_st[q] * jnp.exp(glast[q] - gcol[q]) for q in groups]
    states = [[so_ref[0, hd] for hd in heads[q]] for q in groups]
    wq_s = []
    for q in groups:
        wm = uw[q][:, GDN_DV:]
        qg = q_st[q] * egc[q]
        parts = []
        for i in range(nh):
            rs = slice(i * chunk, (i + 1) * chunk)
            parts.append(_bdot(jnp.concatenate([wm[rs], qg[rs]], axis=0), states[q][i]))
        wq_s.append(parts)
    v_new, o_st = [], []
    for q in groups:
        ws = jnp.concatenate([m[:chunk] for m in wq_s[q]], axis=0) if nh > 1 else wq_s[q][0][:chunk]
        qs = jnp.concatenate([m[chunk:] for m in wq_s[q]], axis=0) if nh > 1 else wq_s[q][0][chunk:]
        vn = (uw[q][:, :GDN_DV] - ws).astype(BF16)
        v_new.append(vn)
        o_st.append(qs + _bdot(aqk[q], vn))
    for q in groups:
        for i, hd in enumerate(heads[q]):
            rs = slice(i * chunk, (i + 1) * chunk)
            gl_h = jnp.exp(glast[q][i * chunk:i * chunk + 1, :])
            if chunk % 16 == 0:
                upd = _bdot_tn(kg[q][rs], v_new[q][rs])
            else:
                upd = _bdot_tn(jnp.where(row_head == i, kg[q], 0.0), v_new[q])
            so_ref[0, hd] = states[q][i] * gl_h + upd
    for q in groups:
        z_st = stack(z_ref, heads[q], GDN_DV)
        og = _rms(o_st[q], nw_ref[0:1, :]) * _silu(z_st)
        for i, hd in enumerate(heads[q]):
            o_ref[:, hd * GDN_DV:(hd + 1) * GDN_DV] = og[i * chunk:(i + 1) * chunk]


GDN_SUBCHUNKS = 2


def _gdn_layer(x2d, conv_prev, s0, w, gi, nw, b, t):
    n = b * t
    chunk = _chunk_of(t)
    tl = _tiling(b, t, 256)
    G, J, R, P = tl["G"], tl["J"], tl["R"], tl["P"]
    w_in = w["gdn_w_in"][gi]
    o1 = GDN_CONV_DIM
    o2 = o1 + GDN_V_DIM
    wqkv = w_in[:, :o1].astype(BF16)
    wz = w_in[:, o1:o2].astype(BF16)
    wbg = jnp.pad(w_in[:, o2:], ((0, 0), (0, LANES - 2 * GDN_V_H))).astype(BF16)
    cw = jnp.pad(w["gdn_conv_w"][gi], ((0, 8 - GDN_CONV), (0, 0)))
    gvec = jnp.zeros((8, LANES), F32)
    gvec = gvec.at[0, GDN_V_H:2 * GDN_V_H].set(w["gdn_a_log"][gi]).at[1, GDN_V_H:2 * GDN_V_H].set(w["gdn_dt_bias"][gi])
    bc = chunk if chunk == 64 else R
    tri = _chunk_masks(chunk, bc)
    nwa = jnp.concatenate([nw[0:1], jnp.zeros((7, D_MODEL), F32)])
    row = lambda g, j: (g * J + j, 0)
    st_spec = pl.BlockSpec((P, GDN_CONV_DIM), lambda g, j: (g, 0))
    qn, kn, v, z, beta, gcs, st = pl.pallas_call(
        _gdn_proj_body,
        grid=(G, J),
        in_specs=[pl.BlockSpec((R, D_MODEL), row), st_spec, _const_spec((8, D_MODEL)), _const_spec(wqkv.shape),
                  _const_spec(wz.shape), _const_spec(wbg.shape), _const_spec(cw.shape), _const_spec(gvec.shape),
                  _const_spec(tri.shape)],
        out_specs=[pl.BlockSpec((R, GDN_QK_DIM), row), pl.BlockSpec((R, GDN_QK_DIM), row),
                   pl.BlockSpec((R, GDN_V_DIM), row), pl.BlockSpec((R, GDN_V_DIM), row),
                   pl.BlockSpec((R, LANES), row), pl.BlockSpec((R, LANES), row), st_spec],
        out_shape=[jax.ShapeDtypeStruct((n, GDN_QK_DIM), F32), jax.ShapeDtypeStruct((n, GDN_QK_DIM), F32),
                   jax.ShapeDtypeStruct((n, GDN_V_DIM), F32), jax.ShapeDtypeStruct((n, GDN_V_DIM), F32),
                   jax.ShapeDtypeStruct((n, LANES), F32), jax.ShapeDtypeStruct((n, LANES), F32),
                   jax.ShapeDtypeStruct((b * SUBLANES, GDN_CONV_DIM), F32)],
        scratch_shapes=[pltpu.VMEM((P, GDN_CONV_DIM), F32)],
        compiler_params=_params(2),
        name="gdn_proj",
    )(x2d, _pad_state(conv_prev), nwa, wqkv, wz, wbg, cw, gvec, tri)
    conv_new = st.reshape(b, SUBLANES, GDN_CONV_DIM)[:, SUBLANES - (GDN_CONV - 1):]

    nh = GROUP_ROWS // chunk
    ng = GDN_V_H // nh
    nsub = GDN_SUBCHUNKS if (t // chunk) % GDN_SUBCHUNKS == 0 else 1
    nct = t // (chunk * nsub)
    br = chunk * nsub
    crow = lambda bb, j: (bb * nct + j, 0)
    sspec = pl.BlockSpec((1, GDN_V_H, GDN_DK, GDN_DV), lambda bb, j: (bb, 0, 0, 0))
    nwn = jnp.concatenate([w["gdn_norm_w"][gi][None, :], jnp.zeros((7, GDN_DV), F32)])
    o, s_new = pl.pallas_call(
        functools.partial(_gdn_chunk_body, nh=nh, chunk=chunk, nsub=nsub),
        grid=(b, nct),
        in_specs=[pl.BlockSpec((br, GDN_QK_DIM), crow), pl.BlockSpec((br, GDN_QK_DIM), crow),
                  pl.BlockSpec((br, GDN_V_DIM), crow), pl.BlockSpec((br, GDN_V_DIM), crow),
                  pl.BlockSpec((br, LANES), crow), pl.BlockSpec((br, LANES), crow),
                  sspec, _const_spec((8, GDN_DV))],
        out_specs=[pl.BlockSpec((br, GDN_V_DIM), crow), sspec],
        out_shape=[jax.ShapeDtypeStruct((n, GDN_V_DIM), F32),
                   jax.ShapeDtypeStruct((b, GDN_V_H, GDN_DK, GDN_DV), F32)],
        compiler_params=_params(2),
        name="gdn_chunk",
    )(qn, kn, v, z, gcs, beta, s0, nwn)
    nwb = jnp.concatenate([nw[1:2], jnp.zeros((7, D_MODEL), F32)])
    x_new = _outproj(x2d, o, None, w["gdn_wo"][gi].astype(BF16), nwb, "gdn_out")
    return x_new, conv_new, s_new


def _trunk(x, pos, rw_s, rw_shift, gdn_s, gdn_conv, ffn_conv, w, paged):
    b, t, _ = x.shape
    x2d = x.reshape(b * t, D_MODEL)
    new = {k: [] for k in ("rw_S", "rw_shift", "mla_c", "mla_kr", "gdn_S", "gdn_conv", "ffn_conv")}
    v_first = None
    ri = mi = gi = 0
    for l, kind in enumerate(LAYER_MIXER):
        nw = w["norm_w"][l]
        if kind == 0:
            x2d, sh, s_new, v_first = _rwkv_layer(x2d, rw_shift[ri], rw_s[ri], v_first, w, ri, nw, b, t)
            new["rw_S"].append(s_new)
            new["rw_shift"].append(sh)
            ri += 1
        elif kind == 1:
            x2d, c, kr = _mla_layer(x2d, pos, w, mi, nw, b, t, None if paged is None else
                                    (paged[0][mi], paged[1][mi], paged[2]))
            new["mla_c"].append(c)
            new["mla_kr"].append(kr)
            mi += 1
        else:
            x2d, cb, s_new = _gdn_layer(x2d, gdn_conv[gi], gdn_s[gi], w, gi, nw, b, t)
            new["gdn_S"].append(s_new)
            new["gdn_conv"].append(cb)
            gi += 1
        nwf = jnp.concatenate([nw[2:4], jnp.zeros((6, D_MODEL), F32)])
        cwb = jnp.concatenate([w["ffn_conv_w"][l], w["ffn_conv_b"][l][None, :],
                               jnp.zeros((8 - FFN_CONV - 1, 2 * D_FF), F32)])
        x2d, st = _ffn(x2d, _pad_state(ffn_conv[l]), nwf, w["ffn_w_up"][l].astype(BF16), cwb,
                       w["ffn_w_down"][l].astype(BF16), b, t)
        new["ffn_conv"].append(st.reshape(b, SUBLANES, 2 * D_FF)[:, SUBLANES - (FFN_CONV - 1):])
    return x2d.reshape(b, t, D_MODEL), {k: jnp.stack(v) for k, v in new.items()}


def kernel(x_prompt, x_sample, state_rwkv_wkv, state_rwkv_shift, cache_mla_latent, cache_mla_krope, state_gdn_S, state_gdn_conv, state_ffn_conv, page_table, norm_w, rw_mu, rw_wrkv, rw_w0, rw_w1, rw_w2, rw_a0, rw_a1, rw_a2, rw_v0, rw_v1, rw_v2, rw_g1, rw_g2, rw_kk, rw_ka, rw_rk, rw_lnx_w, rw_lnx_b, rw_wo, mla_w_in, mla_q_norm, mla_kv_norm, mla_w_qb, mla_w_uk, mla_w_uv, mla_wo, gdn_w_in, gdn_conv_w, gdn_a_log, gdn_dt_bias, gdn_norm_w, gdn_wo, ffn_w_up, ffn_conv_w, ffn_conv_b, ffn_w_down):
    w = dict(norm_w=norm_w, rw_mu=rw_mu, rw_wrkv=rw_wrkv, rw_w0=rw_w0, rw_w1=rw_w1, rw_w2=rw_w2, rw_a0=rw_a0,
             rw_a1=rw_a1, rw_a2=rw_a2, rw_v0=rw_v0, rw_v1=rw_v1, rw_v2=rw_v2, rw_g1=rw_g1, rw_g2=rw_g2,
             rw_kk=rw_kk, rw_ka=rw_ka, rw_rk=rw_rk, rw_lnx_w=rw_lnx_w, rw_lnx_b=rw_lnx_b, rw_wo=rw_wo,
             mla_w_in=mla_w_in, mla_q_norm=mla_q_norm, mla_kv_norm=mla_kv_norm, mla_w_qb=mla_w_qb,
             mla_w_uk=mla_w_uk, mla_w_uv=mla_w_uv, mla_wo=mla_wo, gdn_w_in=gdn_w_in, gdn_conv_w=gdn_conv_w,
             gdn_a_log=gdn_a_log, gdn_dt_bias=gdn_dt_bias, gdn_norm_w=gdn_norm_w, gdn_wo=gdn_wo,
             ffn_w_up=ffn_w_up, ffn_conv_w=ffn_conv_w, ffn_conv_b=ffn_conv_b, ffn_w_down=ffn_w_down)
    b, t = x_prompt.shape[0], x_prompt.shape[1]
    n_rw, n_gdn, depth = state_rwkv_wkv.shape[0], state_gdn_S.shape[0], state_ffn_conv.shape[0]
    y_p, sp = _trunk(
        x_prompt, jnp.arange(t),
        jnp.zeros((n_rw, b) + state_rwkv_wkv.shape[2:], F32), jnp.zeros((n_rw, b, D_MODEL), F32),
        jnp.zeros((n_gdn, b) + state_gdn_S.shape[2:], F32), jnp.zeros((n_gdn, b) + state_gdn_conv.shape[2:], F32),
        jnp.zeros((depth, b) + state_ffn_conv.shape[2:], F32), w, None)
    past_len = page_table.shape[1] * cache_mla_latent.shape[2]
    pos_s = past_len + jnp.arange(x_sample.shape[1])
    y_s, ss = _trunk(x_sample, pos_s, state_rwkv_wkv, state_rwkv_shift, state_gdn_S, state_gdn_conv,
                     state_ffn_conv, w, (cache_mla_latent, cache_mla_krope, page_table))
    names = ("rw_S", "rw_shift", "mla_c", "mla_kr", "gdn_S", "gdn_conv", "ffn_conv")
    return (y_p, y_s) + tuple(sp[k] for k in names) + tuple(ss[k] for k in names)
```

```python
import functools

import jax
import jax.numpy as jnp
from jax import lax
from jax.experimental import pallas as pl
from jax.experimental.pallas import tpu as pltpu

F32 = jnp.float32
BF16 = jnp.bfloat16
HIGHEST = lax.Precision.HIGHEST

D_MODEL = 1024
NORM_EPS = 1e-6
RW_N = 64
RW_H = D_MODEL // RW_N
RW_LNX_EPS = 64e-5
MLA_H = 16
MLA_NOPE = 64
MLA_ROPE = 32
MLA_V = 64
MLA_Q_LORA = 512
MLA_KV_LORA = 256
MLA_SCALE = (MLA_NOPE + MLA_ROPE) ** -0.5
ROPE_THETA = 10000.0
MLA_QK = MLA_KV_LORA + 128
GDN_QK_H = 8
GDN_V_H = 16
GDN_DK = 128
GDN_DV = 128
GDN_QK_DIM = GDN_QK_H * GDN_DK
GDN_V_DIM = GDN_V_H * GDN_DV
GDN_CONV_DIM = 2 * GDN_QK_DIM + GDN_V_DIM
GDN_CONV = 4
D_FF = 2816
FFN_CONV = 3
LAYER_MIXER = (0, 1, 2, 0)

SUBLANES = 8
LANES = 128
GROUP_ROWS = 128
VMEM_LIMIT = 56 * 1024 * 1024


def _rms(x, w):
    return x * lax.rsqrt(jnp.mean(x * x, axis=-1, keepdims=True) + NORM_EPS) * w


def _bdot(a, b):
    return jnp.dot(a.astype(BF16), b.astype(BF16), preferred_element_type=F32)


def _bdot_nt(a, b):
    return lax.dot_general(a.astype(BF16), b.astype(BF16), (((1,), (1,)), ((), ())),
                           preferred_element_type=F32)


def _bdot_tn(a, b):
    return lax.dot_general(a.astype(BF16), b.astype(BF16), (((0,), (0,)), ((), ())),
                           preferred_element_type=F32)


def _hdot(a, b):
    return jnp.dot(a, b, precision=HIGHEST, preferred_element_type=F32)


def _sigmoid(x):
    return 1.0 / (1.0 + jnp.exp(-x))


def _softplus(x):
    return jnp.maximum(x, 0.0) + jnp.log(1.0 + jnp.exp(-jnp.abs(x)))


def _silu(x):
    return x * _sigmoid(x)


def _shift_rows(u, prev, s):
    rows, cols = u.shape
    p = prev.shape[0]
    rolled = pltpu.roll(u, s, 0)
    fix = pltpu.roll(prev, (p - SUBLANES + s) % p, 0)
    t = lax.broadcasted_iota(jnp.int32, (p, cols), 0) % SUBLANES
    if p == rows:
        return jnp.where(t < s, fix, rolled)
    head = jnp.where(t < s, fix, rolled[:SUBLANES])
    return jnp.concatenate([head, rolled[SUBLANES:]], axis=0)


def _lane_group_sum(x, ones2):
    parts = []
    for i in range(x.shape[1] // LANES):
        xs = x[:, i * LANES:(i + 1) * LANES]
        hi = xs.astype(BF16)
        lo = (xs - hi.astype(F32)).astype(BF16)
        parts.append(jnp.dot(jnp.concatenate([hi, lo], axis=1), ones2, preferred_element_type=F32))
    return parts[0] if len(parts) == 1 else jnp.concatenate(parts, axis=1)


def _split_dot(m2, x):
    hi = x.astype(BF16)
    lo = (x - hi.astype(F32)).astype(BF16)
    return jnp.dot(m2, jnp.concatenate([hi, lo], axis=0), preferred_element_type=F32)


def _chunk_cumsum(x, tri):
    bc = tri.shape[0]
    parts = [_hdot(tri, x[i * bc:(i + 1) * bc]) for i in range(x.shape[0] // bc)]
    return parts[0] if len(parts) == 1 else jnp.concatenate(parts, axis=0)


def _tiling(b, t, tt_max):
    if t == SUBLANES:
        return dict(G=1, J=1, R=b * t, P=b * t)
    tt = min(t, tt_max)
    assert t % tt == 0 and tt % 64 == 0, (t, tt)
    return dict(G=b, J=t // tt, R=tt, P=SUBLANES)


def _chunk_of(t):
    return 64 if t % 64 == 0 else t


def _const_spec(shape):
    nd = len(shape)
    return pl.BlockSpec(shape, lambda *_: (0,) * nd, pipeline_mode=pl.Buffered(1))


def _params(n_axes):
    return pltpu.CompilerParams(dimension_semantics=("arbitrary",) * n_axes,
                                vmem_limit_bytes=VMEM_LIMIT)


def _pad_state(st):
    b, k1, c = st.shape
    return jnp.pad(st, ((0, 0), (SUBLANES - k1, 0), (0, 0))).reshape(b * SUBLANES, c)


def _chunk_masks(chunk, rows):
    i = jnp.arange(rows)
    same = (i[:, None] // chunk) == (i[None, :] // chunk)
    tri = same & ((i[None, :] % chunk) <= (i[:, None] % chunk))
    return tri.astype(F32)


FFN_CW = 256


ROW_BLOCK = 64


def _load_strided(ref):
    return pltpu.einshape("(sv)d->(vs)d", ref[...], s=SUBLANES)


def _store_strided(ref, val):
    ref[...] = pltpu.einshape("(vs)d->(sv)d", val, s=SUBLANES)


def _stage(buf_ref, u, carry_ref, st_ref, sl, taps):
    rows = u.shape[0]
    hb = taps * SUBLANES
    buf_ref[hb:hb + rows, :] = u
    first = lax.broadcasted_iota(jnp.int32, (SUBLANES, u.shape[1]), 0) == 0
    for i in range(taps):
        back = taps - i
        src = u[rows - back * SUBLANES:rows - (back - 1) * SUBLANES, :]
        crow = carry_ref[SUBLANES - back:SUBLANES - back + 1, sl]
        buf_ref[i * SUBLANES:(i + 1) * SUBLANES, :] = jnp.where(first, crow, pltpu.roll(src, 1, 0))
    sq = SUBLANES * SUBLANES
    tail = pltpu.einshape("(vs)d->(sv)d", u[rows - sq:, :], s=SUBLANES)[sq - SUBLANES:, :]
    carry_ref[:, sl] = tail
    st_ref[:, sl] = tail


def _taps(buf_ref, r0, nrows, taps):
    hb = taps * SUBLANES
    cur = buf_ref[hb + r0:hb + r0 + nrows, :]
    return cur, [buf_ref[hb - j * SUBLANES + r0:hb - j * SUBLANES + r0 + nrows, :] for j in range(taps, 0, -1)]


def _ffn_body(x_ref, prev_ref, nw_ref, wup_ref, cwb_ref, wdn_ref, xo_ref, st_ref, carry_ref, act_ref, buf_ref):
    @pl.when(pl.program_id(1) == 0)
    def _():
        carry_ref[...] = prev_ref[...]

    rows = x_ref.shape[0]
    stacked = carry_ref.shape[0] == rows
    x = x_ref[...] if stacked else _load_strided(x_ref)
    h = _rms(x, nw_ref[0:1, :]).astype(BF16)
    nch = D_FF // FFN_CW

    def cols(c, half):
        return slice(half * D_FF + c * FFN_CW, half * D_FF + (c + 1) * FFN_CW)

    def conv_gate(c, taps):
        ys = []
        for half in range(2):
            sl = cols(c, half)
            u, (u2, u1) = taps[half]
            ys.append(cwb_ref[0:1, sl] * u2 + cwb_ref[1:2, sl] * u1 + cwb_ref[2:3, sl] * u + cwb_ref[3:4, sl])
        return (_silu(ys[0]) * ys[1]).astype(BF16)

    def up(c):
        us = [jnp.dot(h, wup_ref[:, cols(c, half)], preferred_element_type=F32) for half in range(2)]
        if stacked:
            return us
        for half in range(2):
            _stage(buf_ref.at[(c % 2) * 2 + half], us[half], carry_ref, st_ref, cols(c, half), FFN_CONV - 1)
        return None

    u_cur = up(0)
    for c in range(nch):
        u_nxt = up(c + 1) if c + 1 < nch else None
        csl = slice(c * FFN_CW, (c + 1) * FFN_CW)
        if stacked:
            taps = []
            for half in range(2):
                sl = cols(c, half)
                u = u_cur[half]
                prev = carry_ref[:, sl]
                taps.append((u, [_shift_rows(u, prev, 2), _shift_rows(u, prev, 1)]))
                carry_ref[:, sl] = u
                st_ref[:, sl] = u
            act_ref[:, csl] = conv_gate(c, taps)
        else:
            for r0 in range(0, rows, ROW_BLOCK):
                taps = [_taps(buf_ref.at[(c % 2) * 2 + half], r0, ROW_BLOCK, FFN_CONV - 1) for half in range(2)]
                act_ref[r0:r0 + ROW_BLOCK, csl] = conv_gate(c, taps)
        u_cur = u_nxt
    f = jnp.dot(act_ref[...], wdn_ref[...], preferred_element_type=F32)
    out = x + _rms(f, nw_ref[1:2, :])
    if stacked:
        xo_ref[...] = out
    else:
        _store_strided(xo_ref, out)


def _ffn(x2d, prev, nw, wup, cwb, wdn, b, t):
    tl = _tiling(b, t, 512)
    G, J, R, P = tl["G"], tl["J"], tl["R"], tl["P"]
    n = b * t
    return pl.pallas_call(
        _ffn_body,
        grid=(G, J),
        in_specs=[
            pl.BlockSpec((R, D_MODEL), lambda g, j: (g * J + j, 0)),
            pl.BlockSpec((P, 2 * D_FF), lambda g, j: (g, 0)),
            _const_spec((8, D_MODEL)),
            _const_spec((D_MODEL, 2 * D_FF)),
            _const_spec((8, 2 * D_FF)),
            _const_spec((D_FF, D_MODEL)),
        ],
        out_specs=[
            pl.BlockSpec((R, D_MODEL), lambda g, j: (g * J + j, 0)),
            pl.BlockSpec((P, 2 * D_FF), lambda g, j: (g, 0)),
        ],
        out_shape=[jax.ShapeDtypeStruct((n, D_MODEL), F32),
                   jax.ShapeDtypeStruct((b * SUBLANES, 2 * D_FF), F32)],
        scratch_shapes=[pltpu.VMEM((P, 2 * D_FF), F32), pltpu.VMEM((R, D_FF), BF16),
                        pltpu.VMEM((4, R + (FFN_CONV - 1) * SUBLANES, FFN_CW), F32)],
        compiler_params=_params(2),
        name="conv_ffn",
    )(x2d, prev, nw, wup, cwb, wdn)


def _outproj_body(*refs, gated):
    x_ref, y_ref = refs[0], refs[1]
    wo_ref, nw_ref, xo_ref = refs[-3:]
    y = y_ref[...]
    if gated:
        y = y * refs[2][...]
    o = jnp.dot(y.astype(BF16), wo_ref[...], preferred_element_type=F32)
    xo_ref[...] = x_ref[...] + _rms(o, nw_ref[0:1, :])


def _outproj(x2d, y2d, gate2d, wo, nw, name):
    n, k = y2d.shape
    R = min(n, 512)
    row = lambda i: (i, 0)
    acts = [y2d] if gate2d is None else [y2d, gate2d]
    return pl.pallas_call(
        functools.partial(_outproj_body, gated=gate2d is not None),
        grid=(n // R,),
        in_specs=[pl.BlockSpec((R, D_MODEL), row)] + [pl.BlockSpec((R, k), row)] * len(acts)
        + [_const_spec((k, D_MODEL)), _const_spec((8, D_MODEL))],
        out_specs=pl.BlockSpec((R, D_MODEL), row),
        out_shape=jax.ShapeDtypeStruct((n, D_MODEL), F32),
        compiler_params=_params(1),
        name=name,
    )(x2d, *acts, wo, nw)


def _rwkv_proj_body(*refs, has_vres, chunk):
    it = iter(refs)
    x_ref, prev_ref = next(it), next(it)
    vf_ref = next(it) if has_vres else None
    vec_ref, wrkv_ref, w1_ref, w2_ref, a1_ref, a2_ref = (next(it) for _ in range(6))
    v1_ref, v2_ref = (next(it), next(it)) if has_vres else (None, None)
    g1_ref, g2_ref, tri_ref, ones_ref = (next(it) for _ in range(4))
    rt_ref, kt_ref, at_ref, bt_ref, v_ref, g_ref, gl_ref, hl_ref, carry_ref = (next(it) for _ in range(9))

    @pl.when(pl.program_id(1) == 0)
    def _():
        carry_ref[...] = prev_ref[...]

    x = x_ref[...]
    rows = x.shape[0]
    p = carry_ref.shape[0]
    h = _rms(x, vec_ref[10:11, :])
    d = _shift_rows(h, carry_ref[...], 1) - h
    tail = h[rows - p:, :]
    carry_ref[...] = tail
    hl_ref[...] = tail

    def mix(i):
        return (h + d * vec_ref[i:i + 1, :]).astype(BF16)

    r = jnp.dot(mix(0), wrkv_ref[0], preferred_element_type=F32)
    k = jnp.dot(mix(1), wrkv_ref[1], preferred_element_type=F32)
    xv = mix(2)
    v = jnp.dot(xv, wrkv_ref[2], preferred_element_type=F32)
    w_lora = _bdot(jnp.tanh(_bdot(mix(3), w1_ref[...])), w2_ref[...])
    v_lora = _bdot(_bdot(xv, v1_ref[...]), v2_ref[...]) if has_vres else None
    a_lora = _bdot(_bdot(mix(4), a1_ref[...]), a2_ref[...])
    g_ref[...] = _bdot(_sigmoid(_bdot(mix(5), g1_ref[...])), g2_ref[...])

    bc = tri_ref.shape[0]
    for r0 in range(0, rows, bc):
        for l0 in range(0, D_MODEL, PROJ_LANES):
            rs, ls = slice(r0, r0 + bc), slice(l0, l0 + PROJ_LANES)
            vb = v[rs, ls]
            if has_vres:
                vb = vb + (vf_ref[rs, ls] - vb) * _sigmoid(vec_ref[11:12, ls] + v_lora[rs, ls])
            v_ref[rs, ls] = vb
            a = _sigmoid(vec_ref[7:8, ls] + a_lora[rs, ls])
            kb = k[rs, ls]
            kk = kb * vec_ref[8:9, ls]
            kk = kk * lax.rsqrt(_lane_group_sum(kk * kk, ones_ref[...]) + 1e-6)
            kb = kb * (1.0 + (a - 1.0) * vec_ref[9:10, ls])
            w = -_softplus(-(vec_ref[6:7, ls] + w_lora[rs, ls])) - 0.5
            lw = -jnp.exp(w)
            cum = _split_dot(tri_ref[...], lw)
            e_bwd = jnp.exp(-cum)
            rt_ref[rs, ls] = r[rs, ls] * jnp.exp(cum)
            kt_ref[rs, ls] = kb * e_bwd
            at_ref[rs, ls] = -kk * jnp.exp(cum - lw)
            bt_ref[rs, ls] = kk * a * e_bwd
            for c in range(bc // chunk):
                row = (c + 1) * chunk - 1
                gl_ref[r0 // chunk + c, :, ls] = jnp.exp(cum[row:row + 1, :])


def _rwkv_scan_body(rt_ref, kt_ref, at_ref, bt_ref, v_ref, gl_ref, h0_ref, vec_ref, y_ref, ho_ref,
                    *, nh, chunk, nsub):
    @pl.when(pl.program_id(1) == 0)
    def _():
        ho_ref[...] = h0_ref[...]

    gl_lanes = nh * RW_N
    ng = RW_H // nh
    gc = nh * chunk
    row_head = lax.broadcasted_iota(jnp.int32, (gc, gl_lanes), 0) // chunk
    lane_head = lax.broadcasted_iota(jnp.int32, (gc, gl_lanes), 1) // RW_N
    own = row_head == lane_head
    ri = lax.broadcasted_iota(jnp.int32, (gc, gc), 0)
    ci = lax.broadcasted_iota(jnp.int32, (gc, gc), 1)
    same = (ri // chunk) == (ci // chunk)
    strict = same & ((ri % chunk) > (ci % chunk))
    incl = same & ((ri % chunk) >= (ci % chunk))
    eye = (ri == ci).astype(F32)
    eye_l = (lax.broadcasted_iota(jnp.int32, (gl_lanes, gl_lanes), 0)
             == lax.broadcasted_iota(jnp.int32, (gl_lanes, gl_lanes), 1))
    merged = gc == GROUP_ROWS
    groups = range(ng)
    sls = [slice(q * gl_lanes, (q + 1) * gl_lanes) for q in groups]
    keys = [(sc, q) for sc in range(nsub) for q in groups]

    def blockdiag(ref, key):
        xg = ref[key[0] * chunk:(key[0] + 1) * chunk, sls[key[1]]]
        xx = jnp.concatenate([xg] * nh, axis=0) if nh > 1 else xg
        return jnp.where(own, xx, 0.0)

    r_bd = {k: blockdiag(rt_ref, k) for k in keys}
    k_bd = {k: blockdiag(kt_ref, k) for k in keys}
    a_bd = {k: blockdiag(at_ref, k) for k in keys}
    b_bd = {k: blockdiag(bt_ref, k) for k in keys}
    v_f = {k: blockdiag(v_ref, k) for k in keys}
    v_bd = {k: v_f[k].astype(BF16) for k in keys}
    bonus = {k: jnp.sum(r_bd[k] * k_bd[k] * vec_ref[2:3, sls[k[1]]], axis=1, keepdims=True) for k in keys}
    if merged:
        ar = {k: jnp.concatenate([a_bd[k], r_bd[k]], axis=0).astype(BF16) for k in keys}
        bk = {k: jnp.concatenate([b_bd[k], k_bd[k]], axis=0).astype(BF16) for k in keys}
        amat = {k: _bdot_nt(ar[k], bk[k]) for k in keys}
        a_ab = {k: jnp.where(strict, amat[k][:gc, :gc], 0.0) for k in keys}
        a_ak = {k: jnp.where(strict, amat[k][:gc, gc:], 0.0).astype(BF16) for k in keys}
        a_rbk = {k: jnp.concatenate([jnp.where(incl, amat[k][gc:, :gc], 0.0),
                                     jnp.where(incl, amat[k][gc:, gc:], 0.0)], axis=1).astype(BF16) for k in keys}
    else:
        ab_ = {k: a_bd[k].astype(BF16) for k in keys}
        rb_ = {k: r_bd[k].astype(BF16) for k in keys}
        bb_ = {k: b_bd[k].astype(BF16) for k in keys}
        kb_ = {k: k_bd[k].astype(BF16) for k in keys}
        a_ab = {k: jnp.where(strict, _bdot_nt(ab_[k], bb_[k]), 0.0) for k in keys}
        a_ak = {k: jnp.where(strict, _bdot_nt(ab_[k], kb_[k]), 0.0).astype(BF16) for k in keys}
        a_rb = {k: jnp.where(incl, _bdot_nt(rb_[k], bb_[k]), 0.0).astype(BF16) for k in keys}
        a_rk = {k: jnp.where(incl, _bdot_nt(rb_[k], kb_[k]), 0.0).astype(BF16) for k in keys}
    akv = {k: _bdot(a_ak[k], v_bd[k]) for k in keys}

    p = {k: a_ab[k].astype(BF16) for k in keys}
    x = {k: eye + a_ab[k] for k in keys}
    span = 2
    if span < chunk:
        p = {k: _bdot(p[k], p[k]) for k in keys}
    while span < chunk:
        last = span * 2 >= chunk
        if merged and not last:
            px = {k: _bdot(p[k], jnp.concatenate([p[k].astype(BF16), x[k].astype(BF16)], axis=1)) for k in keys}
            p = {k: px[k][:, :gc] for k in keys}
            x = {k: x[k] + px[k][:, gc:] for k in keys}
        else:
            pb = {k: p[k].astype(BF16) for k in keys}
            x = {k: x[k] + _bdot(pb[k], x[k]) for k in keys}
            if not last:
                p = {k: _bdot(pb[k], pb[k]) for k in keys}
        span *= 2
    tinv = {k: x[k].astype(BF16) for k in keys}

    for sc in range(nsub):
        hs = [ho_ref[0, q] for q in groups]
        hs_b = [h.astype(BF16) for h in hs]
        gl_rows = [gl_ref[sc, :, sl] for sl in sls]
        if merged:
            arh = [_bdot(ar[(sc, q)], hs_b[q]) for q in groups]
            u = [_bdot(tinv[(sc, q)], arh[q][:gc] + akv[(sc, q)]).astype(BF16) for q in groups]
            uv = [jnp.concatenate([u[q], v_bd[(sc, q)]], axis=0) for q in groups]
            y_bd = [arh[q][gc:] + _bdot(a_rbk[(sc, q)], uv[q]) for q in groups]
            for q in groups:
                gl_col = jnp.sum(jnp.where(eye_l, gl_rows[q], 0.0), axis=1, keepdims=True)
                bk_g = jnp.concatenate([b_bd[(sc, q)] * gl_rows[q], k_bd[(sc, q)] * gl_rows[q]], axis=0)
                ho_ref[0, q] = hs[q] * gl_col + _bdot_tn(bk_g, uv[q])
        else:
            ah = [_bdot(ab_[(sc, q)], hs_b[q]) for q in groups]
            rh = [_bdot(rb_[(sc, q)], hs_b[q]) for q in groups]
            u = [_bdot(tinv[(sc, q)], ah[q] + akv[(sc, q)]).astype(BF16) for q in groups]
            y_bd = [rh[q] + _bdot(a_rb[(sc, q)], u[q]) + _bdot(a_rk[(sc, q)], v_bd[(sc, q)]) for q in groups]
            for q in groups:
                gl_col = jnp.sum(jnp.where(eye_l, gl_rows[q], 0.0), axis=1, keepdims=True)
                ho_ref[0, q] = (hs[q] * gl_col + _bdot_tn(b_bd[(sc, q)] * gl_rows[q], u[q])
                                + _bdot_tn(k_bd[(sc, q)] * gl_rows[q], v_bd[(sc, q)]))

        for q in groups:
            sl = sls[q]
            mu = jnp.sum(y_bd[q], axis=1, keepdims=True) * (1.0 / RW_N)
            yc = jnp.where(own, y_bd[q] - mu, 0.0)
            var = jnp.sum(yc * yc, axis=1, keepdims=True) * (1.0 / RW_N)
            tot = (yc * lax.rsqrt(var + RW_LNX_EPS) * vec_ref[0:1, sl] + jnp.where(own, vec_ref[1:2, sl], 0.0)
                   + bonus[(sc, q)] * v_f[(sc, q)])
            y = tot[0:chunk]
            for hh in range(1, nh):
                y = y + tot[hh * chunk:(hh + 1) * chunk]
            y_ref[sc * chunk:(sc + 1) * chunk, sl] = y


def _rwkv_scan_body_old(rt_ref, kt_ref, at_ref, bt_ref, v_ref, gl_ref, h0_ref, vec_ref, ones_ref,
                        y_ref, ho_ref, *, nh, chunk):
    @pl.when(pl.program_id(1) == 0)
    def _():
        ho_ref[...] = h0_ref[...]

    gl_lanes = nh * RW_N
    ng = RW_H // nh
    gc = nh * chunk
    row_head = lax.broadcasted_iota(jnp.int32, (gc, gl_lanes), 0) // chunk
    lane_head = lax.broadcasted_iota(jnp.int32, (gc, gl_lanes), 1) // RW_N
    own = row_head == lane_head
    ri = lax.broadcasted_iota(jnp.int32, (gc, gc), 0)
    ci = lax.broadcasted_iota(jnp.int32, (gc, gc), 1)
    same = (ri // chunk) == (ci // chunk)
    strict = same & ((ri % chunk) > (ci % chunk))
    incl = same & ((ri % chunk) >= (ci % chunk))
    eye = (ri == ci).astype(F32)
    eye_l = (lax.broadcasted_iota(jnp.int32, (gl_lanes, gl_lanes), 0)
             == lax.broadcasted_iota(jnp.int32, (gl_lanes, gl_lanes), 1))
    ones_bd = ones_ref[...]

    def blockdiag(xg):
        xx = jnp.concatenate([xg] * nh, axis=0) if nh > 1 else xg
        return jnp.where(own, xx, 0.0)

    groups = range(ng)
    sls = [slice(q * gl_lanes, (q + 1) * gl_lanes) for q in groups]
    merged = gc == GROUP_ROWS
    gl_rows = [gl_ref[0, :, sl] for sl in sls]
    r_bd = [blockdiag(rt_ref[:, sl]) for sl in sls]
    k_bd = [blockdiag(kt_ref[:, sl]) for sl in sls]
    a_bd = [blockdiag(at_ref[:, sl]) for sl in sls]
    b_bd = [blockdiag(bt_ref[:, sl]) for sl in sls]
    v_bd = [blockdiag(v_ref[:, sl]).astype(BF16) for sl in sls]
    hs = [ho_ref[0, q] for q in groups]
    hs_b = [h.astype(BF16) for h in hs]
    if merged:
        ar = [jnp.concatenate([a_bd[q], r_bd[q]], axis=0).astype(BF16) for q in groups]
        bk = [jnp.concatenate([b_bd[q], k_bd[q]], axis=0).astype(BF16) for q in groups]
        amat = [_bdot_nt(ar[q], bk[q]) for q in groups]
        a_ab = [jnp.where(strict, m[:gc, :gc], 0.0) for m in amat]
        a_ak = [jnp.where(strict, m[:gc, gc:], 0.0).astype(BF16) for m in amat]
        a_rbk = [jnp.concatenate([jnp.where(incl, m[gc:, :gc], 0.0), jnp.where(incl, m[gc:, gc:], 0.0)],
                                 axis=1).astype(BF16) for m in amat]
        arh = [_bdot(ar[q], hs_b[q]) for q in groups]
        ah = [m[:gc] for m in arh]
        rh = [m[gc:] for m in arh]
    else:
        ab_, rb_ = [x.astype(BF16) for x in a_bd], [x.astype(BF16) for x in r_bd]
        bb_, kb_ = [x.astype(BF16) for x in b_bd], [x.astype(BF16) for x in k_bd]
        a_ab = [jnp.where(strict, _bdot_nt(ab_[q], bb_[q]), 0.0) for q in groups]
        a_ak = [jnp.where(strict, _bdot_nt(ab_[q], kb_[q]), 0.0).astype(BF16) for q in groups]
        a_rb = [jnp.where(incl, _bdot_nt(rb_[q], bb_[q]), 0.0).astype(BF16) for q in groups]
        a_rk = [jnp.where(incl, _bdot_nt(rb_[q], kb_[q]), 0.0).astype(BF16) for q in groups]
        ah = [_bdot(ab_[q], hs_b[q]) for q in groups]
        rh = [_bdot(rb_[q], hs_b[q]) for q in groups]
    akv = [_bdot(a_ak[q], v_bd[q]) for q in groups]

    p = [m.astype(BF16) for m in a_ab]
    x = [eye + m for m in a_ab]
    span = 2
    if span < chunk:
        p = [_bdot(p[q], p[q]) for q in groups]
    while span < chunk:
        last = span * 2 >= chunk
        if merged and not last:
            px = [_bdot(p[q], jnp.concatenate([p[q].astype(BF16), x[q].astype(BF16)], axis=1)) for q in groups]
            p = [m[:, :gc] for m in px]
            x = [x[q] + px[q][:, gc:] for q in groups]
        else:
            pb = [m.astype(BF16) for m in p]
            x = [x[q] + _bdot(pb[q], x[q]) for q in groups]
            if not last:
                p = [_bdot(pb[q], pb[q]) for q in groups]
        span *= 2

    u = [_bdot(x[q], ah[q] + akv[q]).astype(BF16) for q in groups]
    if merged:
        uv = [jnp.concatenate([u[q], v_bd[q]], axis=0) for q in groups]
        y_bd = [rh[q] + _bdot(a_rbk[q], uv[q]) for q in groups]
        for q in groups:
            gl_col = jnp.sum(jnp.where(eye_l, gl_rows[q], 0.0), axis=1, keepdims=True)
            bk_g = jnp.concatenate([b_bd[q] * gl_rows[q], k_bd[q] * gl_rows[q]], axis=0)
            ho_ref[0, q] = hs[q] * gl_col + _bdot_tn(bk_g, uv[q])
    else:
        y_bd = [rh[q] + _bdot(a_rb[q], u[q]) + _bdot(a_rk[q], v_bd[q]) for q in groups]
        for q in groups:
            gl_col = jnp.sum(jnp.where(eye_l, gl_rows[q], 0.0), axis=1, keepdims=True)
            ho_ref[0, q] = (hs[q] * gl_col + _bdot_tn(b_bd[q] * gl_rows[q], u[q])
                            + _bdot_tn(k_bd[q] * gl_rows[q], v_bd[q]))

    for q in groups:
        sl = sls[q]
        y = y_bd[q][0:chunk]
        for hh in range(1, nh):
            y = y + y_bd[q][hh * chunk:(hh + 1) * chunk]
        mu = _lane_group_sum(y, ones_bd) * (1.0 / RW_N)
        yc = y - mu
        var = _lane_group_sum(yc * yc, ones_bd) * (1.0 / RW_N)
        yn = yc * lax.rsqrt(var + RW_LNX_EPS) * vec_ref[0:1, sl] + vec_ref[1:2, sl]
        bonus = _lane_group_sum(rt_ref[:, sl] * kt_ref[:, sl] * vec_ref[2:3, sl], ones_bd) * v_ref[:, sl]
        y_ref[:, sl] = yn + bonus


RWKV_SUBCHUNKS = 4
PROJ_LANES = 256


def _rwkv_layer(x2d, shift_prev, s0, v_first, w, ri, nw, b, t):
    n = b * t
    chunk = _chunk_of(t)
    tl = _tiling(b, t, 256)
    G, J, R, P = tl["G"], tl["J"], tl["R"], tl["P"]
    has_vres = v_first is not None
    vi = ri - 1
    bc = chunk if chunk == 64 else R
    tri = _chunk_masks(chunk, bc).astype(BF16)
    tri = jnp.concatenate([tri, tri], axis=1)
    li = jnp.arange(LANES)
    ones_bd = ((li[:, None] // RW_N) == (li[None, :] // RW_N)).astype(BF16)
    ones_bd = jnp.concatenate([ones_bd, ones_bd], axis=0)
    zero = jnp.zeros((D_MODEL,), F32)
    vec = jnp.stack([*(w["rw_mu"][ri][i] for i in range(6)), w["rw_w0"][ri], w["rw_a0"][ri], w["rw_kk"][ri],
                     w["rw_ka"][ri], nw[0], w["rw_v0"][vi] if has_vres else zero, zero, zero, zero, zero])
    row = lambda g, j: (g * J + j, 0)
    row_spec = pl.BlockSpec((R, D_MODEL), row)
    ins = [x2d, _pad_state(shift_prev[:, None, :])]
    specs = [row_spec, pl.BlockSpec((P, D_MODEL), lambda g, j: (g, 0))]
    if has_vres:
        ins.append(v_first)
        specs.append(row_spec)
    wl = [vec, w["rw_wrkv"][ri].astype(BF16), w["rw_w1"][ri].astype(BF16), w["rw_w2"][ri].astype(BF16),
          w["rw_a1"][ri].astype(BF16), w["rw_a2"][ri].astype(BF16)]
    if has_vres:
        wl += [w["rw_v1"][vi].astype(BF16), w["rw_v2"][vi].astype(BF16)]
    wl += [w["rw_g1"][ri].astype(BF16), w["rw_g2"][ri].astype(BF16), tri, ones_bd]
    ins += wl
    specs += [_const_spec(a.shape) for a in wl]
    nc_tile = R // chunk
    outs = pl.pallas_call(
        functools.partial(_rwkv_proj_body, has_vres=has_vres, chunk=chunk),
        grid=(G, J),
        in_specs=specs,
        out_specs=[row_spec] * 6 + [pl.BlockSpec((nc_tile, 1, D_MODEL), lambda g, j: (g * J + j, 0, 0)),
                                    pl.BlockSpec((P, D_MODEL), lambda g, j: (g, 0))],
        out_shape=[jax.ShapeDtypeStruct((n, D_MODEL), F32)] * 6
        + [jax.ShapeDtypeStruct((n // chunk, 1, D_MODEL), F32), jax.ShapeDtypeStruct((b * SUBLANES, D_MODEL), F32)],
        scratch_shapes=[pltpu.VMEM((P, D_MODEL), F32)],
        compiler_params=_params(2),
        name="rwkv_proj",
    )(*ins)
    rt, kt, at, bt, v, g, gl, hl = outs
    shift_new = hl.reshape(b, SUBLANES, D_MODEL)[:, -1]

    nh = LANES // RW_N
    ng = RW_H // nh
    gl_lanes = nh * RW_N
    hkv = jnp.swapaxes(s0, -1, -2).reshape(b, ng, nh, RW_N, RW_N)
    zblk = jnp.zeros((b, ng, RW_N, RW_N), F32)
    h0 = jnp.concatenate(
        [jnp.concatenate([hkv[:, :, i] if i == jj else zblk for jj in range(nh)], axis=-1) for i in range(nh)],
        axis=-2)
    svec = jnp.stack([w["rw_lnx_w"][ri], w["rw_lnx_b"][ri], w["rw_rk"][ri].reshape(D_MODEL),
                      zero, zero, zero, zero, zero])
    nsub = RWKV_SUBCHUNKS if (t // chunk) % RWKV_SUBCHUNKS == 0 else 1
    nct = t // (chunk * nsub)
    crow = lambda bb, j: (bb * nct + j, 0)
    cspec = pl.BlockSpec((chunk * nsub, D_MODEL), crow)
    hspec = pl.BlockSpec((1, ng, gl_lanes, gl_lanes), lambda bb, j: (bb, 0, 0, 0))
    y, hout = pl.pallas_call(
        functools.partial(_rwkv_scan_body, nh=nh, chunk=chunk, nsub=nsub),
        grid=(b, nct),
        in_specs=[cspec] * 5 + [pl.BlockSpec((nsub, 1, D_MODEL), lambda bb, j: (bb * nct + j, 0, 0)), hspec,
                                _const_spec((8, D_MODEL))],
        out_specs=[cspec, hspec],
        out_shape=[jax.ShapeDtypeStruct((n, D_MODEL), F32),
                   jax.ShapeDtypeStruct((b, ng, gl_lanes, gl_lanes), F32)],
        compiler_params=_params(2),
        name="rwkv_scan",
    )(rt, kt, at, bt, v, gl, h0, svec)
    s_new = jnp.stack([hout[:, :, i * RW_N:(i + 1) * RW_N, i * RW_N:(i + 1) * RW_N] for i in range(nh)],
                      axis=2)
    s_new = jnp.swapaxes(s_new, -1, -2).reshape(b, RW_H, RW_N, RW_N)
    nwp = jnp.concatenate([nw[1:2], jnp.zeros((7, D_MODEL), F32)])
    x_new = _outproj(x2d, y, g, w["rw_wo"][ri].astype(BF16), nwp, "rwkv_out")
    return x_new, shift_new, s_new, (v if not has_vres else v_first)


def _rope_lanes(x, tab_ref):
    half = MLA_ROPE // 2
    return (x * tab_ref[0] + pltpu.roll(x, LANES - half, 1) * tab_ref[1] + pltpu.roll(x, half, 1) * tab_ref[2])


def _mla_proj_body(x_ref, nw_ref, tab_ref, winq_ref, winc_ref, wink_ref, qn_ref, kvn_ref, wqn_ref, wqr_ref,
                   wuk_ref, c_ref, kr_ref, kcat_ref, qcat_ref):
    h = _rms(x_ref[...], nw_ref[0:1, :]).astype(BF16)
    cq = _rms(jnp.dot(h, winq_ref[...], preferred_element_type=F32), qn_ref[...]).astype(BF16)
    c = _rms(jnp.dot(h, winc_ref[...], preferred_element_type=F32), kvn_ref[...])
    kr = _rope_lanes(jnp.dot(h, wink_ref[...], preferred_element_type=F32), tab_ref)
    c_ref[...] = c
    kr_ref[...] = kr
    adt = kcat_ref.dtype
    kcat_ref[:, 0:MLA_KV_LORA] = c.astype(adt)
    kcat_ref[:, MLA_KV_LORA:MLA_QK] = kr.astype(adt)
    qn = jnp.dot(cq, wqn_ref[...], preferred_element_type=F32).astype(BF16)
    qr = jnp.dot(cq, wqr_ref[...], preferred_element_type=F32)
    for pr in range(MLA_H // 2):
        ql = jnp.dot(qn[:, pr * LANES:(pr + 1) * LANES], wuk_ref[pr], preferred_element_type=F32) * MLA_SCALE
        qcat_ref[2 * pr, :, 0:MLA_KV_LORA] = ql[:, :MLA_KV_LORA].astype(adt)
        qcat_ref[2 * pr + 1, :, 0:MLA_KV_LORA] = ql[:, MLA_KV_LORA:].astype(adt)
    for hh in range(MLA_H):
        qro = _rope_lanes(qr[:, hh * LANES:(hh + 1) * LANES], tab_ref) * MLA_SCALE
        qcat_ref[hh, :, MLA_KV_LORA:MLA_QK] = qro.astype(adt)


MLA_TQ = 128
MLA_TK = 256
MLA_SPLIT = 8


def _mla_prompt_body(q_ref, k_ref, o_ref, m_ref, l_ref, acc_ref):
    i = pl.program_id(1)
    rows = MLA_H * MLA_TQ
    q = q_ref[...].reshape(rows, MLA_QK)
    m_ref[...] = jnp.full((rows, LANES), -jnp.inf, F32)
    l_ref[...] = jnp.zeros((rows, LANES), F32)
    acc_ref[...] = jnp.zeros((rows, MLA_KV_LORA), F32)
    ones = jnp.ones((MLA_TK, LANES), BF16)
    reps = MLA_TK // LANES

    sub = rows // MLA_SPLIT

    def block(k0, masked):
        kblk = k_ref[pl.ds(k0, MLA_TK), :]
        ss = [lax.dot_general(q[g * sub:(g + 1) * sub], kblk, (((1,), (1,)), ((), ())),
                              preferred_element_type=F32) for g in range(MLA_SPLIT)]
        for g in range(MLA_SPLIT):
            rs = slice(g * sub, (g + 1) * sub)
            s = ss[g]
            if masked:
                qpos = i * MLA_TQ + (g * sub + lax.broadcasted_iota(jnp.int32, (sub, MLA_TK), 0)) % MLA_TQ
                kpos = k0 + lax.broadcasted_iota(jnp.int32, (sub, MLA_TK), 1)
                s = jnp.where(kpos <= qpos, s, -jnp.inf)
            m_old = m_ref[rs, :]
            m_new = jnp.maximum(m_old, jnp.max(s, axis=1, keepdims=True))
            alpha = jnp.exp(m_old - m_new)
            pf = jnp.exp(s - jnp.concatenate([m_new] * reps, axis=1))
            l_ref[rs, :] = l_ref[rs, :] * alpha + jnp.sum(pf, axis=1, keepdims=True)
            acc_ref[rs, :] = (acc_ref[rs, :] * jnp.concatenate([alpha] * (MLA_KV_LORA // LANES), axis=1)
                              + jnp.dot(pf.astype(BF16), kblk[:, :MLA_KV_LORA], preferred_element_type=F32))
            m_ref[rs, :] = m_new

    def full_step(kb, carry):
        block(pl.multiple_of(kb * MLA_TK, MLA_TK), False)
        return carry

    n_full = (i * MLA_TQ) // MLA_TK
    lax.fori_loop(0, n_full, full_step, 0)
    block(pl.multiple_of(n_full * MLA_TK, MLA_TK), True)
    o = acc_ref[...] / jnp.concatenate([l_ref[...]] * (MLA_KV_LORA // LANES), axis=1)
    o_ref[...] = o.reshape(MLA_H, MLA_TQ, MLA_KV_LORA).astype(BF16)


MLA_PP = 16
MLA_GROUPS = 2


def _mla_sample_body(pt_ref, q_ref, kn_ref, *rest):
    lat_refs = rest[:MLA_PP]
    kro_refs = rest[MLA_PP:2 * MLA_PP]
    o_ref, m_ref, l_ref, acc_ref = rest[2 * MLA_PP:]
    j = pl.program_id(1)
    t = q_ref.shape[1]
    rows = MLA_H * t
    q = q_ref[...].reshape(rows, MLA_QK).astype(BF16)
    ql = q[:, :MLA_KV_LORA]
    qr = q[:, MLA_KV_LORA:MLA_KV_LORA + MLA_ROPE]

    @pl.when(j == 0)
    def _():
        m_ref[...] = jnp.full(m_ref.shape, -jnp.inf, F32)
        l_ref[...] = jnp.zeros(l_ref.shape, F32)
        acc_ref[...] = jnp.zeros(acc_ref.shape, F32)

    vrep = MLA_KV_LORA // LANES

    def update(g, s, vals, row_sum):
        m_old = m_ref[g]
        m_new = jnp.maximum(m_old, jnp.max(s, axis=1, keepdims=True))
        alpha = jnp.exp(m_old - m_new)
        if s.shape[1] % LANES == 0:
            p = jnp.exp(s - jnp.concatenate([m_new] * (s.shape[1] // LANES), axis=1)).astype(BF16)
        else:
            p = jnp.exp(s - m_new[:, 0:1]).astype(BF16)
        l_ref[g] = l_ref[g] * alpha + row_sum(p)
        acc_ref[g] = (acc_ref[g] * jnp.concatenate([alpha] * vrep, axis=1)
                      + jnp.dot(p, vals, preferred_element_type=F32))
        m_ref[g] = m_new

    per = MLA_PP // MLA_GROUPS
    ones = jnp.ones((per * lat_refs[0].shape[1], LANES), BF16)
    scores, values = [], []
    for g in range(MLA_GROUPS):
        cbs, s_parts = [], []
        for pp in range(g * per, (g + 1) * per):
            cb = lat_refs[pp][0].astype(BF16)
            kbt = kro_refs[pp][0].astype(BF16)
            s_parts.append(lax.dot_general(ql, cb, (((1,), (1,)), ((), ())), preferred_element_type=F32)
                           + jnp.dot(qr, kbt, preferred_element_type=F32))
            cbs.append(cb)
        scores.append(jnp.concatenate(s_parts, axis=1))
        values.append(jnp.concatenate(cbs, axis=0))
    for g in range(MLA_GROUPS):
        update(g, scores[g], values[g], lambda p: jnp.dot(p, ones, preferred_element_type=F32))

    @pl.when(j == pl.num_programs(1) - 1)
    def _():
        kn = kn_ref[...].astype(BF16)
        s = lax.dot_general(q, kn, (((1,), (1,)), ((), ())), preferred_element_type=F32)
        qpos = lax.broadcasted_iota(jnp.int32, (rows, t), 0) % t
        kpos = lax.broadcasted_iota(jnp.int32, (rows, t), 1)
        s = jnp.where(kpos <= qpos, s, -jnp.inf)
        update(0, s, kn[:, :MLA_KV_LORA], lambda p: jnp.sum(p.astype(F32), axis=1, keepdims=True))
        m_all = m_ref[0]
        for g in range(1, MLA_GROUPS):
            m_all = jnp.maximum(m_all, m_ref[g])
        l_all = jnp.zeros((rows, LANES), F32)
        acc = jnp.zeros((rows, MLA_KV_LORA), F32)
        for g in range(MLA_GROUPS):
            wgt = jnp.exp(m_ref[g] - m_all)
            l_all = l_all + l_ref[g] * wgt
            acc = acc + acc_ref[g] * jnp.concatenate([wgt] * vrep, axis=1)
        o = acc / jnp.concatenate([l_all] * vrep, axis=1)
        o_ref[...] = o.reshape(MLA_H, t, MLA_KV_LORA).astype(o_ref.dtype)


def _mla_out_body(x_ref, o_ref, wuv_ref, wo_ref, nw_ref, xo_ref):
    parts = []
    for pr in range(MLA_H // 2):
        wp = wuv_ref[pr]
        parts.append(jnp.dot(o_ref[2 * pr].astype(BF16), wp[:MLA_KV_LORA], preferred_element_type=F32)
                     + jnp.dot(o_ref[2 * pr + 1].astype(BF16), wp[MLA_KV_LORA:], preferred_element_type=F32))
    v = jnp.concatenate(parts, axis=1).astype(BF16)
    o = jnp.dot(v, wo_ref[...], preferred_element_type=F32)
    xo_ref[...] = x_ref[...] + _rms(o, nw_ref[0:1, :])


def _mla_layer(x2d, pos, w, mi, nw, b, t, paged):
    n = b * t
    tl = _tiling(b, t, 512)
    G, J, R = tl["G"], tl["J"], tl["R"]
    adt = BF16 if t % 16 == 0 else F32
    half = MLA_ROPE // 2
    inv = ROPE_THETA ** (-jnp.arange(half, dtype=F32) / half)
    ang = pos.astype(F32)[:, None] * inv[None, :]
    cos, sin = jnp.cos(ang), jnp.sin(ang)
    zpad = jnp.zeros((t, LANES - MLA_ROPE), F32)
    zh = jnp.zeros((t, half), F32)
    tab = jnp.stack([jnp.concatenate([cos, cos, zpad], 1), jnp.concatenate([-sin, zh, zpad], 1),
                     jnp.concatenate([zh, sin, zpad], 1)])
    if G == 1:
        tab = jnp.tile(tab, (1, b, 1))
    w_in = w["mla_w_in"][mi]
    winq = w_in[:, :MLA_Q_LORA].astype(BF16)
    winc = w_in[:, MLA_Q_LORA:MLA_Q_LORA + MLA_KV_LORA].astype(BF16)
    wink = jnp.pad(w_in[:, MLA_Q_LORA + MLA_KV_LORA:], ((0, 0), (0, LANES - MLA_ROPE))).astype(BF16)
    wqb = w["mla_w_qb"][mi].reshape(MLA_Q_LORA, MLA_H, MLA_NOPE + MLA_ROPE)
    wqn = wqb[:, :, :MLA_NOPE].reshape(MLA_Q_LORA, MLA_H * MLA_NOPE).astype(BF16)
    wqr = jnp.pad(wqb[:, :, MLA_NOPE:], ((0, 0), (0, 0), (0, LANES - MLA_ROPE))
                  ).reshape(MLA_Q_LORA, MLA_H * LANES).astype(BF16)
    wuk = jnp.transpose(w["mla_w_uk"][mi], (1, 2, 0)).reshape(MLA_H // 2, 2, MLA_NOPE, MLA_KV_LORA)
    wuk_bd = jnp.einsum("pinc,ij->pinjc", wuk, jnp.eye(2, dtype=F32)).reshape(
        MLA_H // 2, 2 * MLA_NOPE, 2 * MLA_KV_LORA).astype(BF16)
    wuv = jnp.transpose(w["mla_w_uv"][mi], (1, 0, 2)).reshape(MLA_H // 2, 2, MLA_KV_LORA, MLA_V)
    wuv_bd = jnp.einsum("picv,ij->picjv", wuv, jnp.eye(2, dtype=F32)).reshape(
        MLA_H // 2, 2 * MLA_KV_LORA, 2 * MLA_V).astype(BF16)
    nwa = jnp.concatenate([nw[0:1], jnp.zeros((7, D_MODEL), F32)])
    nwb = jnp.concatenate([nw[1:2], jnp.zeros((7, D_MODEL), F32)])
    row = lambda g, j: (g * J + j, 0)
    wl = [winq, winc, wink, w["mla_q_norm"][mi][None, :], w["mla_kv_norm"][mi][None, :], wqn, wqr, wuk_bd]
    c, kr, kcat, qcat = pl.pallas_call(
        _mla_proj_body,
        grid=(G, J),
        in_specs=[pl.BlockSpec((R, D_MODEL), row), _const_spec((8, D_MODEL)),
                  pl.BlockSpec((3, R, LANES), lambda g, j: (0, j, 0))] + [_const_spec(a.shape) for a in wl],
        out_specs=[pl.BlockSpec((R, MLA_KV_LORA), row), pl.BlockSpec((R, LANES), row),
                   pl.BlockSpec((R, MLA_QK), row), pl.BlockSpec((MLA_H, R, MLA_QK), lambda g, j: (0, g * J + j, 0))],
        out_shape=[jax.ShapeDtypeStruct((n, MLA_KV_LORA), F32), jax.ShapeDtypeStruct((n, LANES), F32),
                   jax.ShapeDtypeStruct((n, MLA_QK), adt), jax.ShapeDtypeStruct((MLA_H, n, MLA_QK), adt)],
        compiler_params=_params(2),
        name="mla_proj",
    )(x2d, nwa, tab, *wl)

    if paged is None:
        nq = t // MLA_TQ
        rows = MLA_H * MLA_TQ
        o = pl.pallas_call(
            _mla_prompt_body,
            grid=(b, nq),
            in_specs=[pl.BlockSpec((MLA_H, MLA_TQ, MLA_QK), lambda bb, i: (0, bb * nq + i, 0)),
                      pl.BlockSpec((t, MLA_QK), lambda bb, i: (bb, 0))],
            out_specs=pl.BlockSpec((MLA_H, MLA_TQ, MLA_KV_LORA), lambda bb, i: (0, bb * nq + i, 0)),
            out_shape=jax.ShapeDtypeStruct((MLA_H, n, MLA_KV_LORA), BF16),
            scratch_shapes=[pltpu.VMEM((rows, LANES), F32), pltpu.VMEM((rows, LANES), F32),
                            pltpu.VMEM((rows, MLA_KV_LORA), F32)],
            compiler_params=_params(2),
            name="mla_attend_prompt",
        )(qcat, kcat)
    else:
        pages_c, pages_kr, page_table = paged
        page = pages_c.shape[1]
        npg = page_table.shape[1]
        assert npg % MLA_PP == 0
        rows = MLA_H * t

        def page_map(pp):
            return lambda bb, j, pt: (pt[bb, j * MLA_PP + pp], 0, 0)

        grid_spec = pltpu.PrefetchScalarGridSpec(
            num_scalar_prefetch=1,
            grid=(b, npg // MLA_PP),
            in_specs=[pl.BlockSpec((MLA_H, t, MLA_QK), lambda bb, j, pt: (0, bb, 0)),
                      pl.BlockSpec((t, MLA_QK), lambda bb, j, pt: (bb, 0))]
            + [pl.BlockSpec((1, page, MLA_KV_LORA), page_map(pp)) for pp in range(MLA_PP)]
            + [pl.BlockSpec((1, MLA_ROPE, page), page_map(pp)) for pp in range(MLA_PP)],
            out_specs=pl.BlockSpec((MLA_H, t, MLA_KV_LORA), lambda bb, j, pt: (0, bb, 0)),
            scratch_shapes=[pltpu.VMEM((MLA_GROUPS, rows, LANES), F32), pltpu.VMEM((MLA_GROUPS, rows, LANES), F32),
                            pltpu.VMEM((MLA_GROUPS, rows, MLA_KV_LORA), F32)],
        )
        o = pl.pallas_call(
            _mla_sample_body,
            grid_spec=grid_spec,
            out_shape=jax.ShapeDtypeStruct((MLA_H, n, MLA_KV_LORA), adt),
            compiler_params=_params(2),
            name="mla_attend_sample",
        )(page_table, qcat, kcat, *([pages_c] * MLA_PP), *([jnp.swapaxes(pages_kr, 1, 2)] * MLA_PP))

    Ro = min(n, 512)
    x_new = pl.pallas_call(
        _mla_out_body,
        grid=(n // Ro,),
        in_specs=[pl.BlockSpec((Ro, D_MODEL), lambda i: (i, 0)),
                  pl.BlockSpec((MLA_H, Ro, MLA_KV_LORA), lambda i: (0, i, 0)),
                  _const_spec(wuv_bd.shape), _const_spec((MLA_H * MLA_V, D_MODEL)), _const_spec((8, D_MODEL))],
        out_specs=pl.BlockSpec((Ro, D_MODEL), lambda i: (i, 0)),
        out_shape=jax.ShapeDtypeStruct((n, D_MODEL), F32),
        compiler_params=_params(1),
        name="mla_out",
    )(x2d, o, wuv_bd, w["mla_wo"][mi].astype(BF16), nwb)
    return x_new, c.reshape(b, t, MLA_KV_LORA), kr[:, :MLA_ROPE].reshape(b, t, MLA_ROPE)


GDN_CW = 512


def _gdn_proj_body(x_ref, prev_ref, nw_ref, wqkv_ref, wz_ref, wbg_ref, cw_ref, gvec_ref, tri_ref,
                   q_ref, k_ref, v_ref, z_ref, beta_ref, gc_ref, st_ref, carry_ref):
    @pl.when(pl.program_id(1) == 0)
    def _():
        carry_ref[...] = prev_ref[...]

    rows = x_ref.shape[0]
    p = carry_ref.shape[0]
    h = _rms(x_ref[...], nw_ref[0:1, :]).astype(BF16)
    z_ref[...] = jnp.dot(h, wz_ref[...], preferred_element_type=F32)
    bg = jnp.dot(h, wbg_ref[...], preferred_element_type=F32)
    beta_ref[...] = _sigmoid(bg)
    g = -jnp.exp(gvec_ref[0:1, :]) * _softplus(bg + gvec_ref[1:2, :])
    gc_ref[...] = _chunk_cumsum(g, tri_ref[...])
    nch = GDN_CONV_DIM // GDN_CW

    def up(c):
        return jnp.dot(h, wqkv_ref[:, c * GDN_CW:(c + 1) * GDN_CW], preferred_element_type=F32)

    u_nxt = up(0)
    for c in range(nch):
        sl = slice(c * GDN_CW, (c + 1) * GDN_CW)
        u = u_nxt
        u_nxt = up(c + 1) if c + 1 < nch else None
        prev = carry_ref[:, sl]
        y = cw_ref[3:4, sl] * u
        for s in range(1, GDN_CONV):
            y = y + cw_ref[3 - s:4 - s, sl] * _shift_rows(u, prev, s)
        tail = u[rows - p:, :]
        carry_ref[:, sl] = tail
        st_ref[:, sl] = tail
        y = _silu(y)
        off = c * GDN_CW
        if off < 2 * GDN_QK_DIM:
            dst, base, scale = (q_ref, off, GDN_DK ** -0.5) if off < GDN_QK_DIM else (k_ref, off - GDN_QK_DIM, 1.0)
            for hh in range(GDN_CW // GDN_DK):
                yh = y[:, hh * GDN_DK:(hh + 1) * GDN_DK]
                yh = yh * lax.rsqrt(jnp.sum(yh * yh, axis=-1, keepdims=True) + 1e-6)
                dst[:, base + hh * GDN_DK:base + (hh + 1) * GDN_DK] = yh * scale if scale != 1.0 else yh
        else:
            v_ref[:, off - 2 * GDN_QK_DIM:off - 2 * GDN_QK_DIM + GDN_CW] = y


def _gdn_chunk_body(q_ref, k_ref, v_ref, z_ref, gc_ref, beta_ref, s0_ref, nw_ref,
                    o_ref, so_ref, *, nh, chunk, nsub):
    @pl.when(pl.program_id(1) == 0)
    def _():
        so_ref[...] = s0_ref[...]

    ng = GDN_V_H // nh
    gc = nh * chunk
    rep = GDN_V_H // GDN_QK_H
    ri = lax.broadcasted_iota(jnp.int32, (gc, gc), 0)
    ci = lax.broadcasted_iota(jnp.int32, (gc, gc), 1)
    same = (ri // chunk) == (ci // chunk)
    strict = same & ((ri % chunk) > (ci % chunk))
    incl = same & ((ri % chunk) >= (ci % chunk))
    last = same & ((ci % chunk) == chunk - 1)
    eye = (ri == ci).astype(F32)
    row_head = lax.broadcasted_iota(jnp.int32, (gc, GDN_DK), 0) // chunk
    groups = range(ng)
    heads = [[q * nh + i for i in range(nh)] for q in groups]
    keys = [(sc, q) for sc in range(nsub) for q in groups]

    def stack(ref, sc, hds, width):
        parts = [ref[sc * chunk:(sc + 1) * chunk, hd * width:(hd + 1) * width] for hd in hds]
        return parts[0] if len(parts) == 1 else jnp.concatenate(parts, axis=0)

    def col(ref, sc, lanes):
        parts = [ref[sc * chunk:(sc + 1) * chunk, ln:ln + 1] for ln in lanes]
        return parts[0] if len(parts) == 1 else jnp.concatenate(parts, axis=0)

    k_st = {k: stack(k_ref, k[0], [hd // rep for hd in heads[k[1]]], GDN_DK) for k in keys}
    q_st = {k: stack(q_ref, k[0], [hd // rep for hd in heads[k[1]]], GDN_DK) for k in keys}
    v_st = {k: stack(v_ref, k[0], heads[k[1]], GDN_DV) for k in keys}
    gcol = {k: col(gc_ref, k[0], [GDN_V_H + hd for hd in heads[k[1]]]) for k in keys}
    bcol = {k: col(beta_ref, k[0], heads[k[1]]) for k in keys}
    grow = {k: jnp.sum(jnp.where(ri == ci, gcol[k], 0.0), axis=0, keepdims=True) for k in keys}
    k_b = {k: k_st[k].astype(BF16) for k in keys}
    kq = {k: _bdot_nt(jnp.concatenate([k_b[k], q_st[k].astype(BF16)], axis=0), k_b[k]) for k in keys}
    decay = {k: jnp.exp(jnp.where(incl, gcol[k] - grow[k], -jnp.inf)) for k in keys}
    a = {k: jnp.where(strict, kq[k][:gc] * bcol[k] * decay[k], 0.0) for k in keys}
    aqk = {k: jnp.where(incl, kq[k][gc:] * decay[k], 0.0).astype(BF16) for k in keys}

    p = {k: (-a[k]).astype(BF16) for k in keys}
    x = {k: eye - a[k] for k in keys}
    span = 2
    if span < chunk:
        p = {k: _bdot(p[k], p[k]) for k in keys}
    while span < chunk:
        if span * 2 < chunk:
            px = {k: _bdot(p[k], jnp.concatenate([p[k].astype(BF16), x[k].astype(BF16)], axis=1)) for k in keys}
            p = {k: px[k][:, :gc] for k in keys}
            x = {k: x[k] + px[k][:, gc:] for k in keys}
        else:
            x = {k: x[k] + _bdot(p[k], x[k]) for k in keys}
        span *= 2

    egc = {k: jnp.exp(gcol[k]) for k in keys}
    uw = {k: _bdot(x[k], jnp.concatenate([v_st[k] * bcol[k], k_st[k] * (bcol[k] * egc[k])], axis=1)) for k in keys}
    glast = {k: jnp.sum(jnp.where(last, grow[k], 0.0), axis=1, keepdims=True) for k in keys}
    kg = {k: k_st[k] * jnp.exp(glast[k] - gcol[k]) for k in keys}
    qg = {k: q_st[k] * egc[k] for k in keys}

    for sc in range(nsub):
        states = [[so_ref[0, hd] for hd in heads[q]] for q in groups]
        wq_s = []
        for q in groups:
            wm = uw[(sc, q)][:, GDN_DV:]
            parts = []
            for i in range(nh):
                rs = slice(i * chunk, (i + 1) * chunk)
                parts.append(_bdot(jnp.concatenate([wm[rs], qg[(sc, q)][rs]], axis=0), states[q][i]))
            wq_s.append(parts)
        v_new, o_st = [], []
        for q in groups:
            ws = jnp.concatenate([m[:chunk] for m in wq_s[q]], axis=0) if nh > 1 else wq_s[q][0][:chunk]
            qs = jnp.concatenate([m[chunk:] for m in wq_s[q]], axis=0) if nh > 1 else wq_s[q][0][chunk:]
            vn = (uw[(sc, q)][:, :GDN_DV] - ws).astype(BF16)
            v_new.append(vn)
            o_st.append(qs + _bdot(aqk[(sc, q)], vn))
        for q in groups:
            for i, hd in enumerate(heads[q]):
                rs = slice(i * chunk, (i + 1) * chunk)
                gl_h = jnp.exp(glast[(sc, q)][i * chunk:i * chunk + 1, :])
                if chunk % 16 == 0:
                    upd = _bdot_tn(kg[(sc, q)][rs], v_new[q][rs])
                else:
                    upd = _bdot_tn(jnp.where(row_head == i, kg[(sc, q)], 0.0), v_new[q])
                so_ref[0, hd] = states[q][i] * gl_h + upd
        for q in groups:
            z_st = stack(z_ref, sc, heads[q], GDN_DV)
            og = _rms(o_st[q], nw_ref[0:1, :]) * _silu(z_st)
            for i, hd in enumerate(heads[q]):
                o_ref[sc * chunk:(sc + 1) * chunk, hd * GDN_DV:(hd + 1) * GDN_DV] = og[i * chunk:(i + 1) * chunk]


def _gdn_chunk_body_old(q_ref, k_ref, v_ref, z_ref, gc_ref, beta_ref, s0_ref, nw_ref,
                        o_ref, so_ref, *, nh, chunk):
    @pl.when(pl.program_id(1) == 0)
    def _():
        so_ref[...] = s0_ref[...]

    ng = GDN_V_H // nh
    gc = nh * chunk
    rep = GDN_V_H // GDN_QK_H
    ri = lax.broadcasted_iota(jnp.int32, (gc, gc), 0)
    ci = lax.broadcasted_iota(jnp.int32, (gc, gc), 1)
    same = (ri // chunk) == (ci // chunk)
    strict = same & ((ri % chunk) > (ci % chunk))
    incl = same & ((ri % chunk) >= (ci % chunk))
    last = same & ((ci % chunk) == chunk - 1)
    eye = (ri == ci).astype(F32)
    row_head = lax.broadcasted_iota(jnp.int32, (gc, GDN_DK), 0) // chunk

    def stack(ref, heads, width):
        parts = [ref[:, hd * width:(hd + 1) * width] for hd in heads]
        return parts[0] if len(parts) == 1 else jnp.concatenate(parts, axis=0)

    groups = range(ng)
    heads = [[q * nh + i for i in range(nh)] for q in groups]
    k_st = [stack(k_ref, [hd // rep for hd in heads[q]], GDN_DK) for q in groups]
    q_st = [stack(q_ref, [hd // rep for hd in heads[q]], GDN_DK) for q in groups]
    v_st = [stack(v_ref, heads[q], GDN_DV) for q in groups]
    def col(ref, lanes):
        parts = [ref[:, ln:ln + 1] for ln in lanes]
        return parts[0] if len(parts) == 1 else jnp.concatenate(parts, axis=0)

    gcol = [col(gc_ref, [GDN_V_H + hd for hd in heads[q]]) for q in groups]
    bcol = [col(beta_ref, heads[q]) for q in groups]
    grow = [jnp.sum(jnp.where(ri == ci, gcol[q], 0.0), axis=0, keepdims=True) for q in groups]
    k_b = [x.astype(BF16) for x in k_st]
    kq = [_bdot_nt(jnp.concatenate([k_b[q], q_st[q].astype(BF16)], axis=0), k_b[q]) for q in groups]
    decay = [jnp.exp(jnp.where(incl, gcol[q] - grow[q], -jnp.inf)) for q in groups]
    a = [jnp.where(strict, kq[q][:gc] * bcol[q] * decay[q], 0.0) for q in groups]
    aqk = [jnp.where(incl, kq[q][gc:] * decay[q], 0.0).astype(BF16) for q in groups]

    p = [(-m).astype(BF16) for m in a]
    x = [eye - m for m in a]
    span = 2
    if span < chunk:
        p = [_bdot(p[q], p[q]) for q in groups]
    while span < chunk:
        if span * 2 < chunk:
            px = [_bdot(p[q], jnp.concatenate([p[q].astype(BF16), x[q].astype(BF16)], axis=1)) for q in groups]
            p = [m[:, :gc] for m in px]
            x = [x[q] + px[q][:, gc:] for q in groups]
        else:
            x = [x[q] + _bdot(p[q], x[q]) for q in groups]
        span *= 2

    egc = [jnp.exp(g) for g in gcol]
    uw = [_bdot(x[q], jnp.concatenate([v_st[q] * bcol[q], k_st[q] * (bcol[q] * egc[q])], axis=1)) for q in groups]
    glast = [jnp.sum(jnp.where(last, grow[q], 0.0), axis=1, keepdims=True) for q in groups]
    kg = [k_st[q] * jnp.exp(glast[q] - gcol[q]) for q in groups]
    states = [[so_ref[0, hd] for hd in heads[q]] for q in groups]
    wq_s = []
    for q in groups:
        wm = uw[q][:, GDN_DV:]
        qg = q_st[q] * egc[q]
        parts = []
        for i in range(nh):
            rs = slice(i * chunk, (i + 1) * chunk)
            parts.append(_bdot(jnp.concatenate([wm[rs], qg[rs]], axis=0), states[q][i]))
        wq_s.append(parts)
    v_new, o_st = [], []
    for q in groups:
        ws = jnp.concatenate([m[:chunk] for m in wq_s[q]], axis=0) if nh > 1 else wq_s[q][0][:chunk]
        qs = jnp.concatenate([m[chunk:] for m in wq_s[q]], axis=0) if nh > 1 else wq_s[q][0][chunk:]
        vn = (uw[q][:, :GDN_DV] - ws).astype(BF16)
        v_new.append(vn)
        o_st.append(qs + _bdot(aqk[q], vn))
    for q in groups:
        for i, hd in enumerate(heads[q]):
            rs = slice(i * chunk, (i + 1) * chunk)
            gl_h = jnp.exp(glast[q][i * chunk:i * chunk + 1, :])
            if chunk % 16 == 0:
                upd = _bdot_tn(kg[q][rs], v_new[q][rs])
            else:
                upd = _bdot_tn(jnp.where(row_head == i, kg[q], 0.0), v_new[q])
            so_ref[0, hd] = states[q][i] * gl_h + upd
    for q in groups:
        z_st = stack(z_ref, heads[q], GDN_DV)
        og = _rms(o_st[q], nw_ref[0:1, :]) * _silu(z_st)
        for i, hd in enumerate(heads[q]):
            o_ref[:, hd * GDN_DV:(hd + 1) * GDN_DV] = og[i * chunk:(i + 1) * chunk]


GDN_SUBCHUNKS = 4


def _gdn_layer(x2d, conv_prev, s0, w, gi, nw, b, t):
    n = b * t
    chunk = _chunk_of(t)
    tl = _tiling(b, t, 256)
    G, J, R, P = tl["G"], tl["J"], tl["R"], tl["P"]
    w_in = w["gdn_w_in"][gi]
    o1 = GDN_CONV_DIM
    o2 = o1 + GDN_V_DIM
    wqkv = w_in[:, :o1].astype(BF16)
    wz = w_in[:, o1:o2].astype(BF16)
    wbg = jnp.pad(w_in[:, o2:], ((0, 0), (0, LANES - 2 * GDN_V_H))).astype(BF16)
    cw = jnp.pad(w["gdn_conv_w"][gi], ((0, 8 - GDN_CONV), (0, 0)))
    gvec = jnp.zeros((8, LANES), F32)
    gvec = gvec.at[0, GDN_V_H:2 * GDN_V_H].set(w["gdn_a_log"][gi]).at[1, GDN_V_H:2 * GDN_V_H].set(w["gdn_dt_bias"][gi])
    bc = chunk if chunk == 64 else R
    tri = _chunk_masks(chunk, bc)
    nwa = jnp.concatenate([nw[0:1], jnp.zeros((7, D_MODEL), F32)])
    row = lambda g, j: (g * J + j, 0)
    st_spec = pl.BlockSpec((P, GDN_CONV_DIM), lambda g, j: (g, 0))
    qn, kn, v, z, beta, gcs, st = pl.pallas_call(
        _gdn_proj_body,
        grid=(G, J),
        in_specs=[pl.BlockSpec((R, D_MODEL), row), st_spec, _const_spec((8, D_MODEL)), _const_spec(wqkv.shape),
                  _const_spec(wz.shape), _const_spec(wbg.shape), _const_spec(cw.shape), _const_spec(gvec.shape),
                  _const_spec(tri.shape)],
        out_specs=[pl.BlockSpec((R, GDN_QK_DIM), row), pl.BlockSpec((R, GDN_QK_DIM), row),
                   pl.BlockSpec((R, GDN_V_DIM), row), pl.BlockSpec((R, GDN_V_DIM), row),
                   pl.BlockSpec((R, LANES), row), pl.BlockSpec((R, LANES), row), st_spec],
        out_shape=[jax.ShapeDtypeStruct((n, GDN_QK_DIM), F32), jax.ShapeDtypeStruct((n, GDN_QK_DIM), F32),
                   jax.ShapeDtypeStruct((n, GDN_V_DIM), F32), jax.ShapeDtypeStruct((n, GDN_V_DIM), F32),
                   jax.ShapeDtypeStruct((n, LANES), F32), jax.ShapeDtypeStruct((n, LANES), F32),
                   jax.ShapeDtypeStruct((b * SUBLANES, GDN_CONV_DIM), F32)],
        scratch_shapes=[pltpu.VMEM((P, GDN_CONV_DIM), F32)],
        compiler_params=_params(2),
        name="gdn_proj",
    )(x2d, _pad_state(conv_prev), nwa, wqkv, wz, wbg, cw, gvec, tri)
    conv_new = st.reshape(b, SUBLANES, GDN_CONV_DIM)[:, SUBLANES - (GDN_CONV - 1):]

    nh = GROUP_ROWS // chunk
    ng = GDN_V_H // nh
    nsub = GDN_SUBCHUNKS if (t // chunk) % GDN_SUBCHUNKS == 0 else 1
    nct = t // (chunk * nsub)
    br = chunk * nsub
    crow = lambda bb, j: (bb * nct + j, 0)
    sspec = pl.BlockSpec((1, GDN_V_H, GDN_DK, GDN_DV), lambda bb, j: (bb, 0, 0, 0))
    nwn = jnp.concatenate([w["gdn_norm_w"][gi][None, :], jnp.zeros((7, GDN_DV), F32)])
    o, s_new = pl.pallas_call(
        functools.partial(_gdn_chunk_body, nh=nh, chunk=chunk, nsub=nsub),
        grid=(b, nct),
        in_specs=[pl.BlockSpec((br, GDN_QK_DIM), crow), pl.BlockSpec((br, GDN_QK_DIM), crow),
                  pl.BlockSpec((br, GDN_V_DIM), crow), pl.BlockSpec((br, GDN_V_DIM), crow),
                  pl.BlockSpec((br, LANES), crow), pl.BlockSpec((br, LANES), crow),
                  sspec, _const_spec((8, GDN_DV))],
        out_specs=[pl.BlockSpec((br, GDN_V_DIM), crow), sspec],
        out_shape=[jax.ShapeDtypeStruct((n, GDN_V_DIM), F32),
                   jax.ShapeDtypeStruct((b, GDN_V_H, GDN_DK, GDN_DV), F32)],
        compiler_params=_params(2),
        name="gdn_chunk",
    )(qn, kn, v, z, gcs, beta, s0, nwn)
    nwb = jnp.concatenate([nw[1:2], jnp.zeros((7, D_MODEL), F32)])
    x_new = _outproj(x2d, o, None, w["gdn_wo"][gi].astype(BF16), nwb, "gdn_out")
    return x_new, conv_new, s_new


def _trunk(x, pos, rw_s, rw_shift, gdn_s, gdn_conv, ffn_conv, w, paged):
    b, t, _ = x.shape
    x2d = x.reshape(b * t, D_MODEL)
    new = {k: [] for k in ("rw_S", "rw_shift", "mla_c", "mla_kr", "gdn_S", "gdn_conv", "ffn_conv")}
    v_first = None
    ri = mi = gi = 0
    for l, kind in enumerate(LAYER_MIXER):
        nw = w["norm_w"][l]
        if kind == 0:
            x2d, sh, s_new, v_first = _rwkv_layer(x2d, rw_shift[ri], rw_s[ri], v_first, w, ri, nw, b, t)
            new["rw_S"].append(s_new)
            new["rw_shift"].append(sh)
            ri += 1
        elif kind == 1:
            x2d, c, kr = _mla_layer(x2d, pos, w, mi, nw, b, t, None if paged is None else
                                    (paged[0][mi], paged[1][mi], paged[2]))
            new["mla_c"].append(c)
            new["mla_kr"].append(kr)
            mi += 1
        else:
            x2d, cb, s_new = _gdn_layer(x2d, gdn_conv[gi], gdn_s[gi], w, gi, nw, b, t)
            new["gdn_S"].append(s_new)
            new["gdn_conv"].append(cb)
            gi += 1
        nwf = jnp.concatenate([nw[2:4], jnp.zeros((6, D_MODEL), F32)])
        cwb = jnp.concatenate([w["ffn_conv_w"][l], w["ffn_conv_b"][l][None, :],
                               jnp.zeros((8 - FFN_CONV - 1, 2 * D_FF), F32)])
        x2d, st = _ffn(x2d, _pad_state(ffn_conv[l]), nwf, w["ffn_w_up"][l].astype(BF16), cwb,
                       w["ffn_w_down"][l].astype(BF16), b, t)
        new["ffn_conv"].append(st.reshape(b, SUBLANES, 2 * D_FF)[:, SUBLANES - (FFN_CONV - 1):])
    return x2d.reshape(b, t, D_MODEL), {k: jnp.stack(v) for k, v in new.items()}


def kernel(x_prompt, x_sample, state_rwkv_wkv, state_rwkv_shift, cache_mla_latent, cache_mla_krope, state_gdn_S, state_gdn_conv, state_ffn_conv, page_table, norm_w, rw_mu, rw_wrkv, rw_w0, rw_w1, rw_w2, rw_a0, rw_a1, rw_a2, rw_v0, rw_v1, rw_v2, rw_g1, rw_g2, rw_kk, rw_ka, rw_rk, rw_lnx_w, rw_lnx_b, rw_wo, mla_w_in, mla_q_norm, mla_kv_norm, mla_w_qb, mla_w_uk, mla_w_uv, mla_wo, gdn_w_in, gdn_conv_w, gdn_a_log, gdn_dt_bias, gdn_norm_w, gdn_wo, ffn_w_up, ffn_conv_w, ffn_conv_b, ffn_w_down):
    w = dict(norm_w=norm_w, rw_mu=rw_mu, rw_wrkv=rw_wrkv, rw_w0=rw_w0, rw_w1=rw_w1, rw_w2=rw_w2, rw_a0=rw_a0,
             rw_a1=rw_a1, rw_a2=rw_a2, rw_v0=rw_v0, rw_v1=rw_v1, rw_v2=rw_v2, rw_g1=rw_g1, rw_g2=rw_g2,
             rw_kk=rw_kk, rw_ka=rw_ka, rw_rk=rw_rk, rw_lnx_w=rw_lnx_w, rw_lnx_b=rw_lnx_b, rw_wo=rw_wo,
             mla_w_in=mla_w_in, mla_q_norm=mla_q_norm, mla_kv_norm=mla_kv_norm, mla_w_qb=mla_w_qb,
             mla_w_uk=mla_w_uk, mla_w_uv=mla_w_uv, mla_wo=mla_wo, gdn_w_in=gdn_w_in, gdn_conv_w=gdn_conv_w,
             gdn_a_log=gdn_a_log, gdn_dt_bias=gdn_dt_bias, gdn_norm_w=gdn_norm_w, gdn_wo=gdn_wo,
             ffn_w_up=ffn_w_up, ffn_conv_w=ffn_conv_w, ffn_conv_b=ffn_conv_b, ffn_w_down=ffn_w_down)
    b, t = x_prompt.shape[0], x_prompt.shape[1]
    n_rw, n_gdn, depth = state_rwkv_wkv.shape[0], state_gdn_S.shape[0], state_ffn_conv.shape[0]
    y_p, sp = _trunk(
        x_prompt, jnp.arange(t),
        jnp.zeros((n_rw, b) + state_rwkv_wkv.shape[2:], F32), jnp.zeros((n_rw, b, D_MODEL), F32),
        jnp.zeros((n_gdn, b) + state_gdn_S.shape[2:], F32), jnp.zeros((n_gdn, b) + state_gdn_conv.shape[2:], F32),
        jnp.zeros((depth, b) + state_ffn_conv.shape[2:], F32), w, None)
    past_len = page_table.shape[1] * cache_mla_latent.shape[2]
    pos_s = past_len + jnp.arange(x_sample.shape[1])
    y_s, ss = _trunk(x_sample, pos_s, state_rwkv_wkv, state_rwkv_shift, state_gdn_S, state_gdn_conv,
                     state_ffn_conv, w, (cache_mla_latent, cache_mla_krope, page_table))
    names = ("rw_S", "rw_shift", "mla_c", "mla_kr", "gdn_S", "gdn_conv", "ffn_conv")
    return (y_p, y_s) + tuple(sp[k] for k in names) + tuple(ss[k] for k in names)
```

```python
import functools

import jax
import jax.numpy as jnp
from jax import lax
from jax.experimental import pallas as pl
from jax.experimental.pallas import tpu as pltpu

F32 = jnp.float32
BF16 = jnp.bfloat16
HIGHEST = lax.Precision.HIGHEST

D_MODEL = 1024
NORM_EPS = 1e-6
RW_N = 64
RW_H = D_MODEL // RW_N
RW_LNX_EPS = 64e-5
MLA_H = 16
MLA_NOPE = 64
MLA_ROPE = 32
MLA_V = 64
MLA_Q_LORA = 512
MLA_KV_LORA = 256
MLA_SCALE = (MLA_NOPE + MLA_ROPE) ** -0.5
ROPE_THETA = 10000.0
MLA_QK = MLA_KV_LORA + 128
GDN_QK_H = 8
GDN_V_H = 16
GDN_DK = 128
GDN_DV = 128
GDN_QK_DIM = GDN_QK_H * GDN_DK
GDN_V_DIM = GDN_V_H * GDN_DV
GDN_CONV_DIM = 2 * GDN_QK_DIM + GDN_V_DIM
GDN_CONV = 4
D_FF = 2816
FFN_CONV = 3
LAYER_MIXER = (0, 1, 2, 0)

SUBLANES = 8
LANES = 128
GROUP_ROWS = 128
VMEM_LIMIT = 56 * 1024 * 1024


def _rms(x, w):
    return x * lax.rsqrt(jnp.mean(x * x, axis=-1, keepdims=True) + NORM_EPS) * w


def _bdot(a, b):
    return jnp.dot(a.astype(BF16), b.astype(BF16), preferred_element_type=F32)


def _bdot_nt(a, b):
    return lax.dot_general(a.astype(BF16), b.astype(BF16), (((1,), (1,)), ((), ())),
                           preferred_element_type=F32)


def _bdot_tn(a, b):
    return lax.dot_general(a.astype(BF16), b.astype(BF16), (((0,), (0,)), ((), ())),
                           preferred_element_type=F32)


def _hdot(a, b):
    return jnp.dot(a, b, precision=HIGHEST, preferred_element_type=F32)


def _sigmoid(x):
    return 1.0 / (1.0 + jnp.exp(-x))


def _softplus(x):
    return jnp.maximum(x, 0.0) + jnp.log(1.0 + jnp.exp(-jnp.abs(x)))


def _silu(x):
    return x * _sigmoid(x)


def _shift_rows(u, prev, s):
    rows, cols = u.shape
    p = prev.shape[0]
    rolled = pltpu.roll(u, s, 0)
    fix = pltpu.roll(prev, (p - SUBLANES + s) % p, 0)
    t = lax.broadcasted_iota(jnp.int32, (p, cols), 0) % SUBLANES
    if p == rows:
        return jnp.where(t < s, fix, rolled)
    head = jnp.where(t < s, fix, rolled[:SUBLANES])
    return jnp.concatenate([head, rolled[SUBLANES:]], axis=0)


def _lane_group_sum(x, ones2):
    parts = []
    for i in range(x.shape[1] // LANES):
        xs = x[:, i * LANES:(i + 1) * LANES]
        hi = xs.astype(BF16)
        lo = (xs - hi.astype(F32)).astype(BF16)
        parts.append(jnp.dot(jnp.concatenate([hi, lo], axis=1), ones2, preferred_element_type=F32))
    return parts[0] if len(parts) == 1 else jnp.concatenate(parts, axis=1)


def _split_dot(m2, x):
    hi = x.astype(BF16)
    lo = (x - hi.astype(F32)).astype(BF16)
    return jnp.dot(m2, jnp.concatenate([hi, lo], axis=0), preferred_element_type=F32)


def _chunk_cumsum(x, tri):
    bc = tri.shape[0]
    parts = [_hdot(tri, x[i * bc:(i + 1) * bc]) for i in range(x.shape[0] // bc)]
    return parts[0] if len(parts) == 1 else jnp.concatenate(parts, axis=0)


def _tiling(b, t, tt_max):
    if t == SUBLANES:
        return dict(G=1, J=1, R=b * t, P=b * t)
    tt = min(t, tt_max)
    assert t % tt == 0 and tt % 64 == 0, (t, tt)
    return dict(G=b, J=t // tt, R=tt, P=SUBLANES)


def _chunk_of(t):
    return 64 if t % 64 == 0 else t


def _const_spec(shape):
    nd = len(shape)
    return pl.BlockSpec(shape, lambda *_: (0,) * nd, pipeline_mode=pl.Buffered(1))


def _params(n_axes):
    return pltpu.CompilerParams(dimension_semantics=("arbitrary",) * n_axes,
                                vmem_limit_bytes=VMEM_LIMIT)


def _pad_state(st):
    b, k1, c = st.shape
    return jnp.pad(st, ((0, 0), (SUBLANES - k1, 0), (0, 0))).reshape(b * SUBLANES, c)


def _chunk_masks(chunk, rows):
    i = jnp.arange(rows)
    same = (i[:, None] // chunk) == (i[None, :] // chunk)
    tri = same & ((i[None, :] % chunk) <= (i[:, None] % chunk))
    return tri.astype(F32)


FFN_CW = 256


ROW_BLOCK = 64


def _load_strided(ref):
    return pltpu.einshape("(sv)d->(vs)d", ref[...], s=SUBLANES)


def _store_strided(ref, val):
    ref[...] = pltpu.einshape("(vs)d->(sv)d", val, s=SUBLANES)


def _stage(buf_ref, u, carry_ref, st_ref, sl, taps):
    rows = u.shape[0]
    hb = taps * SUBLANES
    buf_ref[hb:hb + rows, :] = u
    first = lax.broadcasted_iota(jnp.int32, (SUBLANES, u.shape[1]), 0) == 0
    for i in range(taps):
        back = taps - i
        src = u[rows - back * SUBLANES:rows - (back - 1) * SUBLANES, :]
        crow = carry_ref[SUBLANES - back:SUBLANES - back + 1, sl]
        buf_ref[i * SUBLANES:(i + 1) * SUBLANES, :] = jnp.where(first, crow, pltpu.roll(src, 1, 0))
    sq = SUBLANES * SUBLANES
    tail = pltpu.einshape("(vs)d->(sv)d", u[rows - sq:, :], s=SUBLANES)[sq - SUBLANES:, :]
    carry_ref[:, sl] = tail
    st_ref[:, sl] = tail


def _taps(buf_ref, r0, nrows, taps):
    hb = taps * SUBLANES
    cur = buf_ref[hb + r0:hb + r0 + nrows, :]
    return cur, [buf_ref[hb - j * SUBLANES + r0:hb - j * SUBLANES + r0 + nrows, :] for j in range(taps, 0, -1)]


def _ffn_body(x_ref, prev_ref, nw_ref, wup_ref, cwb_ref, wdn_ref, xo_ref, st_ref, carry_ref, act_ref, buf_ref):
    @pl.when(pl.program_id(1) == 0)
    def _():
        carry_ref[...] = prev_ref[...]

    rows = x_ref.shape[0]
    stacked = carry_ref.shape[0] == rows
    x = x_ref[...] if stacked else _load_strided(x_ref)
    h = _rms(x, nw_ref[0:1, :]).astype(BF16)
    nch = D_FF // FFN_CW

    def cols(c, half):
        return slice(half * D_FF + c * FFN_CW, half * D_FF + (c + 1) * FFN_CW)

    def conv_gate(c, taps):
        ys = []
        for half in range(2):
            sl = cols(c, half)
            u, (u2, u1) = taps[half]
            ys.append(cwb_ref[0:1, sl] * u2 + cwb_ref[1:2, sl] * u1 + cwb_ref[2:3, sl] * u + cwb_ref[3:4, sl])
        return (_silu(ys[0]) * ys[1]).astype(BF16)

    def up(c):
        us = [jnp.dot(h, wup_ref[:, cols(c, half)], preferred_element_type=F32) for half in range(2)]
        if stacked:
            return us
        for half in range(2):
            _stage(buf_ref.at[(c % 2) * 2 + half], us[half], carry_ref, st_ref, cols(c, half), FFN_CONV - 1)
        return None

    u_cur = up(0)
    for c in range(nch):
        u_nxt = up(c + 1) if c + 1 < nch else None
        csl = slice(c * FFN_CW, (c + 1) * FFN_CW)
        if stacked:
            taps = []
            for half in range(2):
                sl = cols(c, half)
                u = u_cur[half]
                prev = carry_ref[:, sl]
                taps.append((u, [_shift_rows(u, prev, 2), _shift_rows(u, prev, 1)]))
                carry_ref[:, sl] = u
                st_ref[:, sl] = u
            act_ref[:, csl] = conv_gate(c, taps)
        else:
            for r0 in range(0, rows, ROW_BLOCK):
                taps = [_taps(buf_ref.at[(c % 2) * 2 + half], r0, ROW_BLOCK, FFN_CONV - 1) for half in range(2)]
                act_ref[r0:r0 + ROW_BLOCK, csl] = conv_gate(c, taps)
        u_cur = u_nxt
    f = jnp.dot(act_ref[...], wdn_ref[...], preferred_element_type=F32)
    out = x + _rms(f, nw_ref[1:2, :])
    if stacked:
        xo_ref[...] = out
    else:
        _store_strided(xo_ref, out)


def _ffn(x2d, prev, nw, wup, cwb, wdn, b, t):
    tl = _tiling(b, t, 512)
    G, J, R, P = tl["G"], tl["J"], tl["R"], tl["P"]
    n = b * t
    return pl.pallas_call(
        _ffn_body,
        grid=(G, J),
        in_specs=[
            pl.BlockSpec((R, D_MODEL), lambda g, j: (g * J + j, 0)),
            pl.BlockSpec((P, 2 * D_FF), lambda g, j: (g, 0)),
            _const_spec((8, D_MODEL)),
            _const_spec((D_MODEL, 2 * D_FF)),
            _const_spec((8, 2 * D_FF)),
            _const_spec((D_FF, D_MODEL)),
        ],
        out_specs=[
            pl.BlockSpec((R, D_MODEL), lambda g, j: (g * J + j, 0)),
            pl.BlockSpec((P, 2 * D_FF), lambda g, j: (g, 0)),
        ],
        out_shape=[jax.ShapeDtypeStruct((n, D_MODEL), F32),
                   jax.ShapeDtypeStruct((b * SUBLANES, 2 * D_FF), F32)],
        scratch_shapes=[pltpu.VMEM((P, 2 * D_FF), F32), pltpu.VMEM((R, D_FF), BF16),
                        pltpu.VMEM((4, R + (FFN_CONV - 1) * SUBLANES, FFN_CW), F32)],
        compiler_params=_params(2),
        name="conv_ffn",
    )(x2d, prev, nw, wup, cwb, wdn)


def _outproj_body(*refs, gated):
    x_ref, y_ref = refs[0], refs[1]
    wo_ref, nw_ref, xo_ref = refs[-3:]
    y = y_ref[...]
    if gated:
        y = y.astype(F32) * refs[2][...].astype(F32)
    o = jnp.dot(y.astype(BF16), wo_ref[...], preferred_element_type=F32)
    xo_ref[...] = x_ref[...] + _rms(o, nw_ref[0:1, :])


def _outproj(x2d, y2d, gate2d, wo, nw, name):
    n, k = y2d.shape
    R = min(n, 512)
    row = lambda i: (i, 0)
    acts = [y2d] if gate2d is None else [y2d, gate2d]
    return pl.pallas_call(
        functools.partial(_outproj_body, gated=gate2d is not None),
        grid=(n // R,),
        in_specs=[pl.BlockSpec((R, D_MODEL), row)] + [pl.BlockSpec((R, k), row)] * len(acts)
        + [_const_spec((k, D_MODEL)), _const_spec((8, D_MODEL))],
        out_specs=pl.BlockSpec((R, D_MODEL), row),
        out_shape=jax.ShapeDtypeStruct((n, D_MODEL), F32),
        compiler_params=_params(1),
        name=name,
    )(x2d, *acts, wo, nw)


def _rwkv_proj_body(*refs, has_vres, chunk):
    it = iter(refs)
    x_ref, prev_ref = next(it), next(it)
    vf_ref = next(it) if has_vres else None
    vec_ref, wrkv_ref, w1_ref, w2_ref, a1_ref, a2_ref = (next(it) for _ in range(6))
    v1_ref, v2_ref = (next(it), next(it)) if has_vres else (None, None)
    g1_ref, g2_ref, tri_ref, ones_ref = (next(it) for _ in range(4))
    rt_ref, kt_ref, at_ref, bt_ref, v_ref, g_ref, gl_ref, hl_ref, carry_ref = (next(it) for _ in range(9))

    @pl.when(pl.program_id(1) == 0)
    def _():
        carry_ref[...] = prev_ref[...]

    x = x_ref[...]
    rows = x.shape[0]
    p = carry_ref.shape[0]
    h = _rms(x, vec_ref[10:11, :])
    d = _shift_rows(h, carry_ref[...], 1) - h
    tail = h[rows - p:, :]
    carry_ref[...] = tail
    hl_ref[...] = tail

    def mix(i):
        return (h + d * vec_ref[i:i + 1, :]).astype(BF16)

    r = jnp.dot(mix(0), wrkv_ref[0], preferred_element_type=F32)
    k = jnp.dot(mix(1), wrkv_ref[1], preferred_element_type=F32)
    xv = mix(2)
    v = jnp.dot(xv, wrkv_ref[2], preferred_element_type=F32)
    w_lora = _bdot(jnp.tanh(_bdot(mix(3), w1_ref[...])), w2_ref[...])
    v_lora = _bdot(_bdot(xv, v1_ref[...]), v2_ref[...]) if has_vres else None
    a_lora = _bdot(_bdot(mix(4), a1_ref[...]), a2_ref[...])
    g_ref[...] = _bdot(_sigmoid(_bdot(mix(5), g1_ref[...])), g2_ref[...]).astype(g_ref.dtype)
    adt = rt_ref.dtype

    bc = tri_ref.shape[0]
    for r0 in range(0, rows, bc):
        for l0 in range(0, D_MODEL, PROJ_LANES):
            rs, ls = slice(r0, r0 + bc), slice(l0, l0 + PROJ_LANES)
            vb = v[rs, ls]
            if has_vres:
                vb = vb + (vf_ref[rs, ls] - vb) * _sigmoid(vec_ref[11:12, ls] + v_lora[rs, ls])
            v_ref[rs, ls] = vb.astype(adt)
            a = _sigmoid(vec_ref[7:8, ls] + a_lora[rs, ls])
            kb = k[rs, ls]
            kk = kb * vec_ref[8:9, ls]
            kk = kk * lax.rsqrt(_lane_group_sum(kk * kk, ones_ref[...]) + 1e-6)
            kb = kb * (1.0 + (a - 1.0) * vec_ref[9:10, ls])
            w = -_softplus(-(vec_ref[6:7, ls] + w_lora[rs, ls])) - 0.5
            lw = -jnp.exp(w)
            cum = _split_dot(tri_ref[...], lw)
            e_bwd = jnp.exp(-cum)
            rt_ref[rs, ls] = (r[rs, ls] * jnp.exp(cum)).astype(adt)
            kt_ref[rs, ls] = (kb * e_bwd).astype(adt)
            at_ref[rs, ls] = (-kk * jnp.exp(cum - lw)).astype(adt)
            bt_ref[rs, ls] = (kk * a * e_bwd).astype(adt)
            for c in range(bc // chunk):
                row = (c + 1) * chunk - 1
                gl_ref[r0 // chunk + c, :, ls] = jnp.exp(cum[row:row + 1, :])


def _rwkv_scan_body(rt_ref, kt_ref, at_ref, bt_ref, v_ref, gl_ref, h0_ref, vec_ref, y_ref, ho_ref,
                    *, nh, chunk, nsub):
    @pl.when(pl.program_id(1) == 0)
    def _():
        ho_ref[...] = h0_ref[...]

    gl_lanes = nh * RW_N
    ng = RW_H // nh
    gc = nh * chunk
    row_head = lax.broadcasted_iota(jnp.int32, (gc, gl_lanes), 0) // chunk
    lane_head = lax.broadcasted_iota(jnp.int32, (gc, gl_lanes), 1) // RW_N
    own = row_head == lane_head
    ri = lax.broadcasted_iota(jnp.int32, (gc, gc), 0)
    ci = lax.broadcasted_iota(jnp.int32, (gc, gc), 1)
    same = (ri // chunk) == (ci // chunk)
    strict = same & ((ri % chunk) > (ci % chunk))
    incl = same & ((ri % chunk) >= (ci % chunk))
    eye = (ri == ci).astype(F32)
    eye_l = (lax.broadcasted_iota(jnp.int32, (gl_lanes, gl_lanes), 0)
             == lax.broadcasted_iota(jnp.int32, (gl_lanes, gl_lanes), 1))
    merged = gc == GROUP_ROWS
    groups = range(ng)
    sls = [slice(q * gl_lanes, (q + 1) * gl_lanes) for q in groups]
    keys = [(sc, q) for sc in range(nsub) for q in groups]

    def blockdiag(ref, key):
        xg = ref[key[0] * chunk:(key[0] + 1) * chunk, sls[key[1]]]
        xx = jnp.concatenate([xg] * nh, axis=0) if nh > 1 else xg
        return jnp.where(own, xx, jnp.zeros_like(xx))

    r_bd = {k: blockdiag(rt_ref, k) for k in keys}
    k_bd = {k: blockdiag(kt_ref, k) for k in keys}
    a_bd = {k: blockdiag(at_ref, k) for k in keys}
    b_bd = {k: blockdiag(bt_ref, k) for k in keys}
    v_f = {k: blockdiag(v_ref, k) for k in keys}
    v_bd = {k: v_f[k].astype(BF16) for k in keys}
    bonus = {k: jnp.sum(r_bd[k].astype(F32) * k_bd[k] * vec_ref[2:3, sls[k[1]]], axis=1, keepdims=True)
             for k in keys}
    if merged:
        ar = {k: jnp.concatenate([a_bd[k], r_bd[k]], axis=0).astype(BF16) for k in keys}
        bk = {k: jnp.concatenate([b_bd[k], k_bd[k]], axis=0).astype(BF16) for k in keys}
        amat = {k: _bdot_nt(ar[k], bk[k]) for k in keys}
        a_ab = {k: jnp.where(strict, amat[k][:gc, :gc], 0.0) for k in keys}
        a_ak = {k: jnp.where(strict, amat[k][:gc, gc:], 0.0).astype(BF16) for k in keys}
        a_rbk = {k: jnp.concatenate([jnp.where(incl, amat[k][gc:, :gc], 0.0),
                                     jnp.where(incl, amat[k][gc:, gc:], 0.0)], axis=1).astype(BF16) for k in keys}
    else:
        ab_ = {k: a_bd[k].astype(BF16) for k in keys}
        rb_ = {k: r_bd[k].astype(BF16) for k in keys}
        bb_ = {k: b_bd[k].astype(BF16) for k in keys}
        kb_ = {k: k_bd[k].astype(BF16) for k in keys}
        a_ab = {k: jnp.where(strict, _bdot_nt(ab_[k], bb_[k]), 0.0) for k in keys}
        a_ak = {k: jnp.where(strict, _bdot_nt(ab_[k], kb_[k]), 0.0).astype(BF16) for k in keys}
        a_rb = {k: jnp.where(incl, _bdot_nt(rb_[k], bb_[k]), 0.0).astype(BF16) for k in keys}
        a_rk = {k: jnp.where(incl, _bdot_nt(rb_[k], kb_[k]), 0.0).astype(BF16) for k in keys}
    akv = {k: _bdot(a_ak[k], v_bd[k]) for k in keys}

    p = {k: a_ab[k].astype(BF16) for k in keys}
    x = {k: eye + a_ab[k] for k in keys}
    span = 2
    if span < chunk:
        p = {k: _bdot(p[k], p[k]) for k in keys}
    while span < chunk:
        last = span * 2 >= chunk
        if merged and not last:
            px = {k: _bdot(p[k], jnp.concatenate([p[k].astype(BF16), x[k].astype(BF16)], axis=1)) for k in keys}
            p = {k: px[k][:, :gc] for k in keys}
            x = {k: x[k] + px[k][:, gc:] for k in keys}
        else:
            pb = {k: p[k].astype(BF16) for k in keys}
            x = {k: x[k] + _bdot(pb[k], x[k]) for k in keys}
            if not last:
                p = {k: _bdot(pb[k], pb[k]) for k in keys}
        span *= 2
    tinv = {k: x[k].astype(BF16) for k in keys}

    for sc in range(nsub):
        hs = [ho_ref[0, q] for q in groups]
        hs_b = [h.astype(BF16) for h in hs]
        gl_rows = [gl_ref[sc, :, sl] for sl in sls]
        if merged:
            arh = [_bdot(ar[(sc, q)], hs_b[q]) for q in groups]
            u = [_bdot(tinv[(sc, q)], arh[q][:gc] + akv[(sc, q)]).astype(BF16) for q in groups]
            uv = [jnp.concatenate([u[q], v_bd[(sc, q)]], axis=0) for q in groups]
            y_bd = [arh[q][gc:] + _bdot(a_rbk[(sc, q)], uv[q]) for q in groups]
            for q in groups:
                gl_col = jnp.sum(jnp.where(eye_l, gl_rows[q], 0.0), axis=1, keepdims=True)
                bk_g = jnp.concatenate([b_bd[(sc, q)] * gl_rows[q], k_bd[(sc, q)] * gl_rows[q]], axis=0)
                ho_ref[0, q] = hs[q] * gl_col + _bdot_tn(bk_g, uv[q])
        else:
            ah = [_bdot(ab_[(sc, q)], hs_b[q]) for q in groups]
            rh = [_bdot(rb_[(sc, q)], hs_b[q]) for q in groups]
            u = [_bdot(tinv[(sc, q)], ah[q] + akv[(sc, q)]).astype(BF16) for q in groups]
            y_bd = [rh[q] + _bdot(a_rb[(sc, q)], u[q]) + _bdot(a_rk[(sc, q)], v_bd[(sc, q)]) for q in groups]
            for q in groups:
                gl_col = jnp.sum(jnp.where(eye_l, gl_rows[q], 0.0), axis=1, keepdims=True)
                ho_ref[0, q] = (hs[q] * gl_col + _bdot_tn(b_bd[(sc, q)] * gl_rows[q], u[q])
                                + _bdot_tn(k_bd[(sc, q)] * gl_rows[q], v_bd[(sc, q)]))

        for q in groups:
            sl = sls[q]
            mu = jnp.sum(y_bd[q], axis=1, keepdims=True) * (1.0 / RW_N)
            yc = jnp.where(own, y_bd[q] - mu, 0.0)
            var = jnp.sum(yc * yc, axis=1, keepdims=True) * (1.0 / RW_N)
            tot = (yc * lax.rsqrt(var + RW_LNX_EPS) * vec_ref[0:1, sl] + jnp.where(own, vec_ref[1:2, sl], 0.0)
                   + bonus[(sc, q)] * v_f[(sc, q)])
            y = tot[0:chunk]
            for hh in range(1, nh):
                y = y + tot[hh * chunk:(hh + 1) * chunk]
            y_ref[sc * chunk:(sc + 1) * chunk, sl] = y.astype(y_ref.dtype)


def _rwkv_scan_body_old(rt_ref, kt_ref, at_ref, bt_ref, v_ref, gl_ref, h0_ref, vec_ref, ones_ref,
                        y_ref, ho_ref, *, nh, chunk):
    @pl.when(pl.program_id(1) == 0)
    def _():
        ho_ref[...] = h0_ref[...]

    gl_lanes = nh * RW_N
    ng = RW_H // nh
    gc = nh * chunk
    row_head = lax.broadcasted_iota(jnp.int32, (gc, gl_lanes), 0) // chunk
    lane_head = lax.broadcasted_iota(jnp.int32, (gc, gl_lanes), 1) // RW_N
    own = row_head == lane_head
    ri = lax.broadcasted_iota(jnp.int32, (gc, gc), 0)
    ci = lax.broadcasted_iota(jnp.int32, (gc, gc), 1)
    same = (ri // chunk) == (ci // chunk)
    strict = same & ((ri % chunk) > (ci % chunk))
    incl = same & ((ri % chunk) >= (ci % chunk))
    eye = (ri == ci).astype(F32)
    eye_l = (lax.broadcasted_iota(jnp.int32, (gl_lanes, gl_lanes), 0)
             == lax.broadcasted_iota(jnp.int32, (gl_lanes, gl_lanes), 1))
    ones_bd = ones_ref[...]

    def blockdiag(xg):
        xx = jnp.concatenate([xg] * nh, axis=0) if nh > 1 else xg
        return jnp.where(own, xx, 0.0)

    groups = range(ng)
    sls = [slice(q * gl_lanes, (q + 1) * gl_lanes) for q in groups]
    merged = gc == GROUP_ROWS
    gl_rows = [gl_ref[0, :, sl] for sl in sls]
    r_bd = [blockdiag(rt_ref[:, sl]) for sl in sls]
    k_bd = [blockdiag(kt_ref[:, sl]) for sl in sls]
    a_bd = [blockdiag(at_ref[:, sl]) for sl in sls]
    b_bd = [blockdiag(bt_ref[:, sl]) for sl in sls]
    v_bd = [blockdiag(v_ref[:, sl]).astype(BF16) for sl in sls]
    hs = [ho_ref[0, q] for q in groups]
    hs_b = [h.astype(BF16) for h in hs]
    if merged:
        ar = [jnp.concatenate([a_bd[q], r_bd[q]], axis=0).astype(BF16) for q in groups]
        bk = [jnp.concatenate([b_bd[q], k_bd[q]], axis=0).astype(BF16) for q in groups]
        amat = [_bdot_nt(ar[q], bk[q]) for q in groups]
        a_ab = [jnp.where(strict, m[:gc, :gc], 0.0) for m in amat]
        a_ak = [jnp.where(strict, m[:gc, gc:], 0.0).astype(BF16) for m in amat]
        a_rbk = [jnp.concatenate([jnp.where(incl, m[gc:, :gc], 0.0), jnp.where(incl, m[gc:, gc:], 0.0)],
                                 axis=1).astype(BF16) for m in amat]
        arh = [_bdot(ar[q], hs_b[q]) for q in groups]
        ah = [m[:gc] for m in arh]
        rh = [m[gc:] for m in arh]
    else:
        ab_, rb_ = [x.astype(BF16) for x in a_bd], [x.astype(BF16) for x in r_bd]
        bb_, kb_ = [x.astype(BF16) for x in b_bd], [x.astype(BF16) for x in k_bd]
        a_ab = [jnp.where(strict, _bdot_nt(ab_[q], bb_[q]), 0.0) for q in groups]
        a_ak = [jnp.where(strict, _bdot_nt(ab_[q], kb_[q]), 0.0).astype(BF16) for q in groups]
        a_rb = [jnp.where(incl, _bdot_nt(rb_[q], bb_[q]), 0.0).astype(BF16) for q in groups]
        a_rk = [jnp.where(incl, _bdot_nt(rb_[q], kb_[q]), 0.0).astype(BF16) for q in groups]
        ah = [_bdot(ab_[q], hs_b[q]) for q in groups]
        rh = [_bdot(rb_[q], hs_b[q]) for q in groups]
    akv = [_bdot(a_ak[q], v_bd[q]) for q in groups]

    p = [m.astype(BF16) for m in a_ab]
    x = [eye + m for m in a_ab]
    span = 2
    if span < chunk:
        p = [_bdot(p[q], p[q]) for q in groups]
    while span < chunk:
        last = span * 2 >= chunk
        if merged and not last:
            px = [_bdot(p[q], jnp.concatenate([p[q].astype(BF16), x[q].astype(BF16)], axis=1)) for q in groups]
            p = [m[:, :gc] for m in px]
            x = [x[q] + px[q][:, gc:] for q in groups]
        else:
            pb = [m.astype(BF16) for m in p]
            x = [x[q] + _bdot(pb[q], x[q]) for q in groups]
            if not last:
                p = [_bdot(pb[q], pb[q]) for q in groups]
        span *= 2

    u = [_bdot(x[q], ah[q] + akv[q]).astype(BF16) for q in groups]
    if merged:
        uv = [jnp.concatenate([u[q], v_bd[q]], axis=0) for q in groups]
        y_bd = [rh[q] + _bdot(a_rbk[q], uv[q]) for q in groups]
        for q in groups:
            gl_col = jnp.sum(jnp.where(eye_l, gl_rows[q], 0.0), axis=1, keepdims=True)
            bk_g = jnp.concatenate([b_bd[q] * gl_rows[q], k_bd[q] * gl_rows[q]], axis=0)
            ho_ref[0, q] = hs[q] * gl_col + _bdot_tn(bk_g, uv[q])
    else:
        y_bd = [rh[q] + _bdot(a_rb[q], u[q]) + _bdot(a_rk[q], v_bd[q]) for q in groups]
        for q in groups:
            gl_col = jnp.sum(jnp.where(eye_l, gl_rows[q], 0.0), axis=1, keepdims=True)
            ho_ref[0, q] = (hs[q] * gl_col + _bdot_tn(b_bd[q] * gl_rows[q], u[q])
                            + _bdot_tn(k_bd[q] * gl_rows[q], v_bd[q]))

    for q in groups:
        sl = sls[q]
        y = y_bd[q][0:chunk]
        for hh in range(1, nh):
            y = y + y_bd[q][hh * chunk:(hh + 1) * chunk]
        mu = _lane_group_sum(y, ones_bd) * (1.0 / RW_N)
        yc = y - mu
        var = _lane_group_sum(yc * yc, ones_bd) * (1.0 / RW_N)
        yn = yc * lax.rsqrt(var + RW_LNX_EPS) * vec_ref[0:1, sl] + vec_ref[1:2, sl]
        bonus = _lane_group_sum(rt_ref[:, sl] * kt_ref[:, sl] * vec_ref[2:3, sl], ones_bd) * v_ref[:, sl]
        y_ref[:, sl] = yn + bonus


RWKV_SUBCHUNKS = 4
PROJ_LANES = 256


def _rwkv_layer(x2d, shift_prev, s0, v_first, w, ri, nw, b, t):
    n = b * t
    chunk = _chunk_of(t)
    tl = _tiling(b, t, 512)
    G, J, R, P = tl["G"], tl["J"], tl["R"], tl["P"]
    has_vres = v_first is not None
    vi = ri - 1
    adt = BF16 if chunk % 16 == 0 else F32
    bc = chunk if chunk == 64 else R
    tri = _chunk_masks(chunk, bc).astype(BF16)
    tri = jnp.concatenate([tri, tri], axis=1)
    li = jnp.arange(LANES)
    ones_bd = ((li[:, None] // RW_N) == (li[None, :] // RW_N)).astype(BF16)
    ones_bd = jnp.concatenate([ones_bd, ones_bd], axis=0)
    zero = jnp.zeros((D_MODEL,), F32)
    vec = jnp.stack([*(w["rw_mu"][ri][i] for i in range(6)), w["rw_w0"][ri], w["rw_a0"][ri], w["rw_kk"][ri],
                     w["rw_ka"][ri], nw[0], w["rw_v0"][vi] if has_vres else zero, zero, zero, zero, zero])
    row = lambda g, j: (g * J + j, 0)
    row_spec = pl.BlockSpec((R, D_MODEL), row)
    ins = [x2d, _pad_state(shift_prev[:, None, :])]
    specs = [row_spec, pl.BlockSpec((P, D_MODEL), lambda g, j: (g, 0))]
    if has_vres:
        ins.append(v_first)
        specs.append(row_spec)
    wl = [vec, w["rw_wrkv"][ri].astype(BF16), w["rw_w1"][ri].astype(BF16), w["rw_w2"][ri].astype(BF16),
          w["rw_a1"][ri].astype(BF16), w["rw_a2"][ri].astype(BF16)]
    if has_vres:
        wl += [w["rw_v1"][vi].astype(BF16), w["rw_v2"][vi].astype(BF16)]
    wl += [w["rw_g1"][ri].astype(BF16), w["rw_g2"][ri].astype(BF16), tri, ones_bd]
    ins += wl
    specs += [_const_spec(a.shape) for a in wl]
    nc_tile = R // chunk
    outs = pl.pallas_call(
        functools.partial(_rwkv_proj_body, has_vres=has_vres, chunk=chunk),
        grid=(G, J),
        in_specs=specs,
        out_specs=[row_spec] * 6 + [pl.BlockSpec((nc_tile, 1, D_MODEL), lambda g, j: (g * J + j, 0, 0)),
                                    pl.BlockSpec((P, D_MODEL), lambda g, j: (g, 0))],
        out_shape=[jax.ShapeDtypeStruct((n, D_MODEL), adt)] * 6
        + [jax.ShapeDtypeStruct((n // chunk, 1, D_MODEL), F32), jax.ShapeDtypeStruct((b * SUBLANES, D_MODEL), F32)],
        scratch_shapes=[pltpu.VMEM((P, D_MODEL), F32)],
        compiler_params=_params(2),
        name="rwkv_proj",
    )(*ins)
    rt, kt, at, bt, v, g, gl, hl = outs
    shift_new = hl.reshape(b, SUBLANES, D_MODEL)[:, -1]

    nh = LANES // RW_N
    ng = RW_H // nh
    gl_lanes = nh * RW_N
    hkv = jnp.swapaxes(s0, -1, -2).reshape(b, ng, nh, RW_N, RW_N)
    zblk = jnp.zeros((b, ng, RW_N, RW_N), F32)
    h0 = jnp.concatenate(
        [jnp.concatenate([hkv[:, :, i] if i == jj else zblk for jj in range(nh)], axis=-1) for i in range(nh)],
        axis=-2)
    svec = jnp.stack([w["rw_lnx_w"][ri], w["rw_lnx_b"][ri], w["rw_rk"][ri].reshape(D_MODEL),
                      zero, zero, zero, zero, zero])
    nsub = RWKV_SUBCHUNKS if (t // chunk) % RWKV_SUBCHUNKS == 0 else 1
    nct = t // (chunk * nsub)
    crow = lambda bb, j: (bb * nct + j, 0)
    cspec = pl.BlockSpec((chunk * nsub, D_MODEL), crow)
    hspec = pl.BlockSpec((1, ng, gl_lanes, gl_lanes), lambda bb, j: (bb, 0, 0, 0))
    y, hout = pl.pallas_call(
        functools.partial(_rwkv_scan_body, nh=nh, chunk=chunk, nsub=nsub),
        grid=(b, nct),
        in_specs=[cspec] * 5 + [pl.BlockSpec((nsub, 1, D_MODEL), lambda bb, j: (bb * nct + j, 0, 0)), hspec,
                                _const_spec((8, D_MODEL))],
        out_specs=[cspec, hspec],
        out_shape=[jax.ShapeDtypeStruct((n, D_MODEL), adt),
                   jax.ShapeDtypeStruct((b, ng, gl_lanes, gl_lanes), F32)],
        compiler_params=_params(2),
        name="rwkv_scan",
    )(rt, kt, at, bt, v, gl, h0, svec)
    s_new = jnp.stack([hout[:, :, i * RW_N:(i + 1) * RW_N, i * RW_N:(i + 1) * RW_N] for i in range(nh)],
                      axis=2)
    s_new = jnp.swapaxes(s_new, -1, -2).reshape(b, RW_H, RW_N, RW_N)
    nwp = jnp.concatenate([nw[1:2], jnp.zeros((7, D_MODEL), F32)])
    x_new = _outproj(x2d, y, g, w["rw_wo"][ri].astype(BF16), nwp, "rwkv_out")
    return x_new, shift_new, s_new, (v if not has_vres else v_first)


def _rope_lanes(x, tab_ref):
    half = MLA_ROPE // 2
    return (x * tab_ref[0] + pltpu.roll(x, LANES - half, 1) * tab_ref[1] + pltpu.roll(x, half, 1) * tab_ref[2])


def _mla_proj_body(x_ref, nw_ref, tab_ref, winq_ref, winc_ref, wink_ref, qn_ref, kvn_ref, wqn_ref, wqr_ref,
                   wuk_ref, c_ref, kr_ref, kcat_ref, qcat_ref):
    h = _rms(x_ref[...], nw_ref[0:1, :]).astype(BF16)
    cq = _rms(jnp.dot(h, winq_ref[...], preferred_element_type=F32), qn_ref[...]).astype(BF16)
    c = _rms(jnp.dot(h, winc_ref[...], preferred_element_type=F32), kvn_ref[...])
    kr = _rope_lanes(jnp.dot(h, wink_ref[...], preferred_element_type=F32), tab_ref)
    c_ref[...] = c
    kr_ref[...] = kr
    adt = kcat_ref.dtype
    kcat_ref[:, 0:MLA_KV_LORA] = c.astype(adt)
    kcat_ref[:, MLA_KV_LORA:MLA_QK] = kr.astype(adt)
    qn = jnp.dot(cq, wqn_ref[...], preferred_element_type=F32).astype(BF16)
    qr = jnp.dot(cq, wqr_ref[...], preferred_element_type=F32)
    for pr in range(MLA_H // 2):
        ql = jnp.dot(qn[:, pr * LANES:(pr + 1) * LANES], wuk_ref[pr], preferred_element_type=F32) * MLA_SCALE
        qcat_ref[2 * pr, :, 0:MLA_KV_LORA] = ql[:, :MLA_KV_LORA].astype(adt)
        qcat_ref[2 * pr + 1, :, 0:MLA_KV_LORA] = ql[:, MLA_KV_LORA:].astype(adt)
    for hh in range(MLA_H):
        qro = _rope_lanes(qr[:, hh * LANES:(hh + 1) * LANES], tab_ref) * MLA_SCALE
        qcat_ref[hh, :, MLA_KV_LORA:MLA_QK] = qro.astype(adt)


MLA_TQ = 128
MLA_TK = 256
MLA_SPLIT = 8


def _mla_prompt_body(q_ref, k_ref, o_ref, m_ref, l_ref, acc_ref):
    i = pl.program_id(1)
    rows = MLA_H * MLA_TQ
    q = q_ref[...].reshape(rows, MLA_QK)
    m_ref[...] = jnp.full((rows, LANES), -jnp.inf, F32)
    l_ref[...] = jnp.zeros((rows, LANES), F32)
    acc_ref[...] = jnp.zeros((rows, MLA_KV_LORA), F32)
    ones = jnp.ones((MLA_TK, LANES), BF16)
    reps = MLA_TK // LANES

    sub = rows // MLA_SPLIT

    def block(k0, masked):
        kblk = k_ref[pl.ds(k0, MLA_TK), :]
        ss = [lax.dot_general(q[g * sub:(g + 1) * sub], kblk, (((1,), (1,)), ((), ())),
                              preferred_element_type=F32) for g in range(MLA_SPLIT)]
        for g in range(MLA_SPLIT):
            rs = slice(g * sub, (g + 1) * sub)
            s = ss[g]
            if masked:
                qpos = i * MLA_TQ + (g * sub + lax.broadcasted_iota(jnp.int32, (sub, MLA_TK), 0)) % MLA_TQ
                kpos = k0 + lax.broadcasted_iota(jnp.int32, (sub, MLA_TK), 1)
                s = jnp.where(kpos <= qpos, s, -jnp.inf)
            m_old = m_ref[rs, :]
            m_new = jnp.maximum(m_old, jnp.max(s, axis=1, keepdims=True))
            alpha = jnp.exp(m_old - m_new)
            pf = jnp.exp(s - jnp.concatenate([m_new] * reps, axis=1))
            l_ref[rs, :] = l_ref[rs, :] * alpha + jnp.sum(pf, axis=1, keepdims=True)
            acc_ref[rs, :] = (acc_ref[rs, :] * jnp.concatenate([alpha] * (MLA_KV_LORA // LANES), axis=1)
                              + jnp.dot(pf.astype(BF16), kblk[:, :MLA_KV_LORA], preferred_element_type=F32))
            m_ref[rs, :] = m_new

    def full_step(kb, carry):
        block(pl.multiple_of(kb * MLA_TK, MLA_TK), False)
        return carry

    n_full = (i * MLA_TQ) // MLA_TK
    lax.fori_loop(0, n_full, full_step, 0)
    block(pl.multiple_of(n_full * MLA_TK, MLA_TK), True)
    o = acc_ref[...] / jnp.concatenate([l_ref[...]] * (MLA_KV_LORA // LANES), axis=1)
    o_ref[...] = o.reshape(MLA_H, MLA_TQ, MLA_KV_LORA).astype(BF16)


MLA_PP = 16
MLA_GROUPS = 2


def _mla_sample_body(pt_ref, q_ref, kn_ref, *rest):
    lat_refs = rest[:MLA_PP]
    kro_refs = rest[MLA_PP:2 * MLA_PP]
    o_ref, m_ref, l_ref, acc_ref = rest[2 * MLA_PP:]
    j = pl.program_id(1)
    t = q_ref.shape[1]
    rows = MLA_H * t
    q = q_ref[...].reshape(rows, MLA_QK).astype(BF16)
    ql = q[:, :MLA_KV_LORA]
    qr = q[:, MLA_KV_LORA:MLA_KV_LORA + MLA_ROPE]

    @pl.when(j == 0)
    def _():
        m_ref[...] = jnp.full(m_ref.shape, -jnp.inf, F32)
        l_ref[...] = jnp.zeros(l_ref.shape, F32)
        acc_ref[...] = jnp.zeros(acc_ref.shape, F32)

    vrep = MLA_KV_LORA // LANES

    def update(g, s, vals, row_sum):
        m_old = m_ref[g]
        m_new = jnp.maximum(m_old, jnp.max(s, axis=1, keepdims=True))
        alpha = jnp.exp(m_old - m_new)
        if s.shape[1] % LANES == 0:
            p = jnp.exp(s - jnp.concatenate([m_new] * (s.shape[1] // LANES), axis=1)).astype(BF16)
        else:
            p = jnp.exp(s - m_new[:, 0:1]).astype(BF16)
        l_ref[g] = l_ref[g] * alpha + row_sum(p)
        acc_ref[g] = (acc_ref[g] * jnp.concatenate([alpha] * vrep, axis=1)
                      + jnp.dot(p, vals, preferred_element_type=F32))
        m_ref[g] = m_new

    per = MLA_PP // MLA_GROUPS
    ones = jnp.ones((per * lat_refs[0].shape[1], LANES), BF16)
    scores, values = [], []
    for g in range(MLA_GROUPS):
        cbs, s_parts = [], []
        for pp in range(g * per, (g + 1) * per):
            cb = lat_refs[pp][0].astype(BF16)
            kbt = kro_refs[pp][0].astype(BF16)
            s_parts.append(lax.dot_general(ql, cb, (((1,), (1,)), ((), ())), preferred_element_type=F32)
                           + jnp.dot(qr, kbt, preferred_element_type=F32))
            cbs.append(cb)
        scores.append(jnp.concatenate(s_parts, axis=1))
        values.append(jnp.concatenate(cbs, axis=0))
    for g in range(MLA_GROUPS):
        update(g, scores[g], values[g], lambda p: jnp.dot(p, ones, preferred_element_type=F32))

    @pl.when(j == pl.num_programs(1) - 1)
    def _():
        kn = kn_ref[...].astype(BF16)
        s = lax.dot_general(q, kn, (((1,), (1,)), ((), ())), preferred_element_type=F32)
        qpos = lax.broadcasted_iota(jnp.int32, (rows, t), 0) % t
        kpos = lax.broadcasted_iota(jnp.int32, (rows, t), 1)
        s = jnp.where(kpos <= qpos, s, -jnp.inf)
        update(0, s, kn[:, :MLA_KV_LORA], lambda p: jnp.sum(p.astype(F32), axis=1, keepdims=True))
        m_all = m_ref[0]
        for g in range(1, MLA_GROUPS):
            m_all = jnp.maximum(m_all, m_ref[g])
        l_all = jnp.zeros((rows, LANES), F32)
        acc = jnp.zeros((rows, MLA_KV_LORA), F32)
        for g in range(MLA_GROUPS):
            wgt = jnp.exp(m_ref[g] - m_all)
            l_all = l_all + l_ref[g] * wgt
            acc = acc + acc_ref[g] * jnp.concatenate([wgt] * vrep, axis=1)
        o = acc / jnp.concatenate([l_all] * vrep, axis=1)
        o_ref[...] = o.reshape(MLA_H, t, MLA_KV_LORA).astype(o_ref.dtype)


def _mla_out_body(x_ref, o_ref, wuv_ref, wo_ref, nw_ref, xo_ref):
    parts = []
    for pr in range(MLA_H // 2):
        wp = wuv_ref[pr]
        parts.append(jnp.dot(o_ref[2 * pr].astype(BF16), wp[:MLA_KV_LORA], preferred_element_type=F32)
                     + jnp.dot(o_ref[2 * pr + 1].astype(BF16), wp[MLA_KV_LORA:], preferred_element_type=F32))
    v = jnp.concatenate(parts, axis=1).astype(BF16)
    o = jnp.dot(v, wo_ref[...], preferred_element_type=F32)
    xo_ref[...] = x_ref[...] + _rms(o, nw_ref[0:1, :])


def _mla_layer(x2d, pos, w, mi, nw, b, t, paged):
    n = b * t
    tl = _tiling(b, t, 512)
    G, J, R = tl["G"], tl["J"], tl["R"]
    adt = BF16 if t % 16 == 0 else F32
    half = MLA_ROPE // 2
    inv = ROPE_THETA ** (-jnp.arange(half, dtype=F32) / half)
    ang = pos.astype(F32)[:, None] * inv[None, :]
    cos, sin = jnp.cos(ang), jnp.sin(ang)
    zpad = jnp.zeros((t, LANES - MLA_ROPE), F32)
    zh = jnp.zeros((t, half), F32)
    tab = jnp.stack([jnp.concatenate([cos, cos, zpad], 1), jnp.concatenate([-sin, zh, zpad], 1),
                     jnp.concatenate([zh, sin, zpad], 1)])
    if G == 1:
        tab = jnp.tile(tab, (1, b, 1))
    w_in = w["mla_w_in"][mi]
    winq = w_in[:, :MLA_Q_LORA].astype(BF16)
    winc = w_in[:, MLA_Q_LORA:MLA_Q_LORA + MLA_KV_LORA].astype(BF16)
    wink = jnp.pad(w_in[:, MLA_Q_LORA + MLA_KV_LORA:], ((0, 0), (0, LANES - MLA_ROPE))).astype(BF16)
    wqb = w["mla_w_qb"][mi].reshape(MLA_Q_LORA, MLA_H, MLA_NOPE + MLA_ROPE)
    wqn = wqb[:, :, :MLA_NOPE].reshape(MLA_Q_LORA, MLA_H * MLA_NOPE).astype(BF16)
    wqr = jnp.pad(wqb[:, :, MLA_NOPE:], ((0, 0), (0, 0), (0, LANES - MLA_ROPE))
                  ).reshape(MLA_Q_LORA, MLA_H * LANES).astype(BF16)
    wuk = jnp.transpose(w["mla_w_uk"][mi], (1, 2, 0)).reshape(MLA_H // 2, 2, MLA_NOPE, MLA_KV_LORA)
    wuk_bd = jnp.einsum("pinc,ij->pinjc", wuk, jnp.eye(2, dtype=F32)).reshape(
        MLA_H // 2, 2 * MLA_NOPE, 2 * MLA_KV_LORA).astype(BF16)
    wuv = jnp.transpose(w["mla_w_uv"][mi], (1, 0, 2)).reshape(MLA_H // 2, 2, MLA_KV_LORA, MLA_V)
    wuv_bd = jnp.einsum("picv,ij->picjv", wuv, jnp.eye(2, dtype=F32)).reshape(
        MLA_H // 2, 2 * MLA_KV_LORA, 2 * MLA_V).astype(BF16)
    nwa = jnp.concatenate([nw[0:1], jnp.zeros((7, D_MODEL), F32)])
    nwb = jnp.concatenate([nw[1:2], jnp.zeros((7, D_MODEL), F32)])
    row = lambda g, j: (g * J + j, 0)
    wl = [winq, winc, wink, w["mla_q_norm"][mi][None, :], w["mla_kv_norm"][mi][None, :], wqn, wqr, wuk_bd]
    c, kr, kcat, qcat = pl.pallas_call(
        _mla_proj_body,
        grid=(G, J),
        in_specs=[pl.BlockSpec((R, D_MODEL), row), _const_spec((8, D_MODEL)),
                  pl.BlockSpec((3, R, LANES), lambda g, j: (0, j, 0))] + [_const_spec(a.shape) for a in wl],
        out_specs=[pl.BlockSpec((R, MLA_KV_LORA), row), pl.BlockSpec((R, LANES), row),
                   pl.BlockSpec((R, MLA_QK), row), pl.BlockSpec((MLA_H, R, MLA_QK), lambda g, j: (0, g * J + j, 0))],
        out_shape=[jax.ShapeDtypeStruct((n, MLA_KV_LORA), F32), jax.ShapeDtypeStruct((n, LANES), F32),
                   jax.ShapeDtypeStruct((n, MLA_QK), adt), jax.ShapeDtypeStruct((MLA_H, n, MLA_QK), adt)],
        compiler_params=_params(2),
        name="mla_proj",
    )(x2d, nwa, tab, *wl)

    if paged is None:
        nq = t // MLA_TQ
        rows = MLA_H * MLA_TQ
        o = pl.pallas_call(
            _mla_prompt_body,
            grid=(b, nq),
            in_specs=[pl.BlockSpec((MLA_H, MLA_TQ, MLA_QK), lambda bb, i: (0, bb * nq + i, 0)),
                      pl.BlockSpec((t, MLA_QK), lambda bb, i: (bb, 0))],
            out_specs=pl.BlockSpec((MLA_H, MLA_TQ, MLA_KV_LORA), lambda bb, i: (0, bb * nq + i, 0)),
            out_shape=jax.ShapeDtypeStruct((MLA_H, n, MLA_KV_LORA), BF16),
            scratch_shapes=[pltpu.VMEM((rows, LANES), F32), pltpu.VMEM((rows, LANES), F32),
                            pltpu.VMEM((rows, MLA_KV_LORA), F32)],
            compiler_params=_params(2),
            name="mla_attend_prompt",
        )(qcat, kcat)
    else:
        pages_c, pages_kr, page_table = paged
        page = pages_c.shape[1]
        npg = page_table.shape[1]
        assert npg % MLA_PP == 0
        rows = MLA_H * t

        def page_map(pp):
            return lambda bb, j, pt: (pt[bb, j * MLA_PP + pp], 0, 0)

        grid_spec = pltpu.PrefetchScalarGridSpec(
            num_scalar_prefetch=1,
            grid=(b, npg // MLA_PP),
            in_specs=[pl.BlockSpec((MLA_H, t, MLA_QK), lambda bb, j, pt: (0, bb, 0)),
                      pl.BlockSpec((t, MLA_QK), lambda bb, j, pt: (bb, 0))]
            + [pl.BlockSpec((1, page, MLA_KV_LORA), page_map(pp)) for pp in range(MLA_PP)]
            + [pl.BlockSpec((1, MLA_ROPE, page), page_map(pp)) for pp in range(MLA_PP)],
            out_specs=pl.BlockSpec((MLA_H, t, MLA_KV_LORA), lambda bb, j, pt: (0, bb, 0)),
            scratch_shapes=[pltpu.VMEM((MLA_GROUPS, rows, LANES), F32), pltpu.VMEM((MLA_GROUPS, rows, LANES), F32),
                            pltpu.VMEM((MLA_GROUPS, rows, MLA_KV_LORA), F32)],
        )
        o = pl.pallas_call(
            _mla_sample_body,
            grid_spec=grid_spec,
            out_shape=jax.ShapeDtypeStruct((MLA_H, n, MLA_KV_LORA), adt),
            compiler_params=_params(2),
            name="mla_attend_sample",
        )(page_table, qcat, kcat, *([pages_c] * MLA_PP), *([jnp.swapaxes(pages_kr, 1, 2)] * MLA_PP))

    Ro = min(n, 512)
    x_new = pl.pallas_call(
        _mla_out_body,
        grid=(n // Ro,),
        in_specs=[pl.BlockSpec((Ro, D_MODEL), lambda i: (i, 0)),
                  pl.BlockSpec((MLA_H, Ro, MLA_KV_LORA), lambda i: (0, i, 0)),
                  _const_spec(wuv_bd.shape), _const_spec((MLA_H * MLA_V, D_MODEL)), _const_spec((8, D_MODEL))],
        out_specs=pl.BlockSpec((Ro, D_MODEL), lambda i: (i, 0)),
        out_shape=jax.ShapeDtypeStruct((n, D_MODEL), F32),
        compiler_params=_params(1),
        name="mla_out",
    )(x2d, o, wuv_bd, w["mla_wo"][mi].astype(BF16), nwb)
    return x_new, c.reshape(b, t, MLA_KV_LORA), kr[:, :MLA_ROPE].reshape(b, t, MLA_ROPE)


GDN_CW = 512


def _gdn_proj_body(x_ref, prev_ref, nw_ref, wqkv_ref, wz_ref, wbg_ref, cw_ref, gvec_ref, tri_ref,
                   q_ref, k_ref, v_ref, z_ref, beta_ref, gc_ref, st_ref, carry_ref):
    @pl.when(pl.program_id(1) == 0)
    def _():
        carry_ref[...] = prev_ref[...]

    rows = x_ref.shape[0]
    p = carry_ref.shape[0]
    h = _rms(x_ref[...], nw_ref[0:1, :]).astype(BF16)
    adt = z_ref.dtype
    z_ref[...] = jnp.dot(h, wz_ref[...], preferred_element_type=F32).astype(adt)
    bg = jnp.dot(h, wbg_ref[...], preferred_element_type=F32)
    beta_ref[...] = _sigmoid(bg)
    g = -jnp.exp(gvec_ref[0:1, :]) * _softplus(bg + gvec_ref[1:2, :])
    gc_ref[...] = _chunk_cumsum(g, tri_ref[...])
    nch = GDN_CONV_DIM // GDN_CW

    def up(c):
        return jnp.dot(h, wqkv_ref[:, c * GDN_CW:(c + 1) * GDN_CW], preferred_element_type=F32)

    u_nxt = up(0)
    for c in range(nch):
        sl = slice(c * GDN_CW, (c + 1) * GDN_CW)
        u = u_nxt
        u_nxt = up(c + 1) if c + 1 < nch else None
        prev = carry_ref[:, sl]
        y = cw_ref[3:4, sl] * u
        for s in range(1, GDN_CONV):
            y = y + cw_ref[3 - s:4 - s, sl] * _shift_rows(u, prev, s)
        tail = u[rows - p:, :]
        carry_ref[:, sl] = tail
        st_ref[:, sl] = tail
        y = _silu(y)
        off = c * GDN_CW
        if off < 2 * GDN_QK_DIM:
            dst, base, scale = (q_ref, off, GDN_DK ** -0.5) if off < GDN_QK_DIM else (k_ref, off - GDN_QK_DIM, 1.0)
            for hh in range(GDN_CW // GDN_DK):
                yh = y[:, hh * GDN_DK:(hh + 1) * GDN_DK]
                yh = yh * lax.rsqrt(jnp.sum(yh * yh, axis=-1, keepdims=True) + 1e-6)
                dst[:, base + hh * GDN_DK:base + (hh + 1) * GDN_DK] = (yh * scale if scale != 1.0 else yh).astype(adt)
        else:
            v_ref[:, off - 2 * GDN_QK_DIM:off - 2 * GDN_QK_DIM + GDN_CW] = y.astype(adt)


def _gdn_chunk_body(q_ref, k_ref, v_ref, z_ref, gc_ref, beta_ref, s0_ref, nw_ref,
                    o_ref, so_ref, *, nh, chunk, nsub):
    @pl.when(pl.program_id(1) == 0)
    def _():
        so_ref[...] = s0_ref[...]

    ng = GDN_V_H // nh
    gc = nh * chunk
    rep = GDN_V_H // GDN_QK_H
    ri = lax.broadcasted_iota(jnp.int32, (gc, gc), 0)
    ci = lax.broadcasted_iota(jnp.int32, (gc, gc), 1)
    same = (ri // chunk) == (ci // chunk)
    strict = same & ((ri % chunk) > (ci % chunk))
    incl = same & ((ri % chunk) >= (ci % chunk))
    last = same & ((ci % chunk) == chunk - 1)
    eye = (ri == ci).astype(F32)
    row_head = lax.broadcasted_iota(jnp.int32, (gc, GDN_DK), 0) // chunk
    groups = range(ng)
    heads = [[q * nh + i for i in range(nh)] for q in groups]
    keys = [(sc, q) for sc in range(nsub) for q in groups]

    def stack(ref, sc, hds, width):
        parts = [ref[sc * chunk:(sc + 1) * chunk, hd * width:(hd + 1) * width] for hd in hds]
        return parts[0] if len(parts) == 1 else jnp.concatenate(parts, axis=0)

    def col(ref, sc, lanes):
        parts = [ref[sc * chunk:(sc + 1) * chunk, ln:ln + 1] for ln in lanes]
        return parts[0] if len(parts) == 1 else jnp.concatenate(parts, axis=0)

    k_st = {k: stack(k_ref, k[0], [hd // rep for hd in heads[k[1]]], GDN_DK) for k in keys}
    q_st = {k: stack(q_ref, k[0], [hd // rep for hd in heads[k[1]]], GDN_DK) for k in keys}
    v_st = {k: stack(v_ref, k[0], heads[k[1]], GDN_DV) for k in keys}
    gcol = {k: col(gc_ref, k[0], [GDN_V_H + hd for hd in heads[k[1]]]) for k in keys}
    bcol = {k: col(beta_ref, k[0], heads[k[1]]) for k in keys}
    grow = {k: jnp.sum(jnp.where(ri == ci, gcol[k], 0.0), axis=0, keepdims=True) for k in keys}
    k_b = {k: k_st[k].astype(BF16) for k in keys}
    kq = {k: _bdot_nt(jnp.concatenate([k_b[k], q_st[k].astype(BF16)], axis=0), k_b[k]) for k in keys}
    decay = {k: jnp.exp(jnp.where(incl, gcol[k] - grow[k], -jnp.inf)) for k in keys}
    a = {k: jnp.where(strict, kq[k][:gc] * bcol[k] * decay[k], 0.0) for k in keys}
    aqk = {k: jnp.where(incl, kq[k][gc:] * decay[k], 0.0).astype(BF16) for k in keys}

    p = {k: (-a[k]).astype(BF16) for k in keys}
    x = {k: eye - a[k] for k in keys}
    span = 2
    if span < chunk:
        p = {k: _bdot(p[k], p[k]) for k in keys}
    while span < chunk:
        if span * 2 < chunk:
            px = {k: _bdot(p[k], jnp.concatenate([p[k].astype(BF16), x[k].astype(BF16)], axis=1)) for k in keys}
            p = {k: px[k][:, :gc] for k in keys}
            x = {k: x[k] + px[k][:, gc:] for k in keys}
        else:
            x = {k: x[k] + _bdot(p[k], x[k]) for k in keys}
        span *= 2

    egc = {k: jnp.exp(gcol[k]) for k in keys}
    uw = {k: _bdot(x[k], jnp.concatenate([v_st[k] * bcol[k], k_st[k] * (bcol[k] * egc[k])], axis=1)) for k in keys}
    glast = {k: jnp.sum(jnp.where(last, grow[k], 0.0), axis=1, keepdims=True) for k in keys}
    kg = {k: k_st[k] * jnp.exp(glast[k] - gcol[k]) for k in keys}
    qg = {k: q_st[k] * egc[k] for k in keys}

    for sc in range(nsub):
        states = [[so_ref[0, hd] for hd in heads[q]] for q in groups]
        wq_s = []
        for q in groups:
            wm = uw[(sc, q)][:, GDN_DV:]
            parts = []
            for i in range(nh):
                rs = slice(i * chunk, (i + 1) * chunk)
                parts.append(_bdot(jnp.concatenate([wm[rs], qg[(sc, q)][rs]], axis=0), states[q][i]))
            wq_s.append(parts)
        v_new, o_st = [], []
        for q in groups:
            ws = jnp.concatenate([m[:chunk] for m in wq_s[q]], axis=0) if nh > 1 else wq_s[q][0][:chunk]
            qs = jnp.concatenate([m[chunk:] for m in wq_s[q]], axis=0) if nh > 1 else wq_s[q][0][chunk:]
            vn = (uw[(sc, q)][:, :GDN_DV] - ws).astype(BF16)
            v_new.append(vn)
            o_st.append(qs + _bdot(aqk[(sc, q)], vn))
        for q in groups:
            for i, hd in enumerate(heads[q]):
                rs = slice(i * chunk, (i + 1) * chunk)
                gl_h = jnp.exp(glast[(sc, q)][i * chunk:i * chunk + 1, :])
                if chunk % 16 == 0:
                    upd = _bdot_tn(kg[(sc, q)][rs], v_new[q][rs])
                else:
                    upd = _bdot_tn(jnp.where(row_head == i, kg[(sc, q)], 0.0), v_new[q])
                so_ref[0, hd] = states[q][i] * gl_h + upd
        for q in groups:
            z_st = stack(z_ref, sc, heads[q], GDN_DV).astype(F32)
            og = (_rms(o_st[q], nw_ref[0:1, :]) * _silu(z_st)).astype(o_ref.dtype)
            for i, hd in enumerate(heads[q]):
                o_ref[sc * chunk:(sc + 1) * chunk, hd * GDN_DV:(hd + 1) * GDN_DV] = og[i * chunk:(i + 1) * chunk]


def _gdn_chunk_body_old(q_ref, k_ref, v_ref, z_ref, gc_ref, beta_ref, s0_ref, nw_ref,
                        o_ref, so_ref, *, nh, chunk):
    @pl.when(pl.program_id(1) == 0)
    def _():
        so_ref[...] = s0_ref[...]

    ng = GDN_V_H // nh
    gc = nh * chunk
    rep = GDN_V_H // GDN_QK_H
    ri = lax.broadcasted_iota(jnp.int32, (gc, gc), 0)
    ci = lax.broadcasted_iota(jnp.int32, (gc, gc), 1)
    same = (ri // chunk) == (ci // chunk)
    strict = same & ((ri % chunk) > (ci % chunk))
    incl = same & ((ri % chunk) >= (ci % chunk))
    last = same & ((ci % chunk) == chunk - 1)
    eye = (ri == ci).astype(F32)
    row_head = lax.broadcasted_iota(jnp.int32, (gc, GDN_DK), 0) // chunk

    def stack(ref, heads, width):
        parts = [ref[:, hd * width:(hd + 1) * width] for hd in heads]
        return parts[0] if len(parts) == 1 else jnp.concatenate(parts, axis=0)

    groups = range(ng)
    heads = [[q * nh + i for i in range(nh)] for q in groups]
    k_st = [stack(k_ref, [hd // rep for hd in heads[q]], GDN_DK) for q in groups]
    q_st = [stack(q_ref, [hd // rep for hd in heads[q]], GDN_DK) for q in groups]
    v_st = [stack(v_ref, heads[q], GDN_DV) for q in groups]
    def col(ref, lanes):
        parts = [ref[:, ln:ln + 1] for ln in lanes]
        return parts[0] if len(parts) == 1 else jnp.concatenate(parts, axis=0)

    gcol = [col(gc_ref, [GDN_V_H + hd for hd in heads[q]]) for q in groups]
    bcol = [col(beta_ref, heads[q]) for q in groups]
    grow = [jnp.sum(jnp.where(ri == ci, gcol[q], 0.0), axis=0, keepdims=True) for q in groups]
    k_b = [x.astype(BF16) for x in k_st]
    kq = [_bdot_nt(jnp.concatenate([k_b[q], q_st[q].astype(BF16)], axis=0), k_b[q]) for q in groups]
    decay = [jnp.exp(jnp.where(incl, gcol[q] - grow[q], -jnp.inf)) for q in groups]
    a = [jnp.where(strict, kq[q][:gc] * bcol[q] * decay[q], 0.0) for q in groups]
    aqk = [jnp.where(incl, kq[q][gc:] * decay[q], 0.0).astype(BF16) for q in groups]

    p = [(-m).astype(BF16) for m in a]
    x = [eye - m for m in a]
    span = 2
    if span < chunk:
        p = [_bdot(p[q], p[q]) for q in groups]
    while span < chunk:
        if span * 2 < chunk:
            px = [_bdot(p[q], jnp.concatenate([p[q].astype(BF16), x[q].astype(BF16)], axis=1)) for q in groups]
            p = [m[:, :gc] for m in px]
            x = [x[q] + px[q][:, gc:] for q in groups]
        else:
            x = [x[q] + _bdot(p[q], x[q]) for q in groups]
        span *= 2

    egc = [jnp.exp(g) for g in gcol]
    uw = [_bdot(x[q], jnp.concatenate([v_st[q] * bcol[q], k_st[q] * (bcol[q] * egc[q])], axis=1)) for q in groups]
    glast = [jnp.sum(jnp.where(last, grow[q], 0.0), axis=1, keepdims=True) for q in groups]
    kg = [k_st[q] * jnp.exp(glast[q] - gcol[q]) for q in groups]
    states = [[so_ref[0, hd] for hd in heads[q]] for q in groups]
    wq_s = []
    for q in groups:
        wm = uw[q][:, GDN_DV:]
        qg = q_st[q] * egc[q]
        parts = []
        for i in range(nh):
            rs = slice(i * chunk, (i + 1) * chunk)
            parts.append(_bdot(jnp.concatenate([wm[rs], qg[rs]], axis=0), states[q][i]))
        wq_s.append(parts)
    v_new, o_st = [], []
    for q in groups:
        ws = jnp.concatenate([m[:chunk] for m in wq_s[q]], axis=0) if nh > 1 else wq_s[q][0][:chunk]
        qs = jnp.concatenate([m[chunk:] for m in wq_s[q]], axis=0) if nh > 1 else wq_s[q][0][chunk:]
        vn = (uw[q][:, :GDN_DV] - ws).astype(BF16)
        v_new.append(vn)
        o_st.append(qs + _bdot(aqk[q], vn))
    for q in groups:
        for i, hd in enumerate(heads[q]):
            rs = slice(i * chunk, (i + 1) * chunk)
            gl_h = jnp.exp(glast[q][i * chunk:i * chunk + 1, :])
            if chunk % 16 == 0:
                upd = _bdot_tn(kg[q][rs], v_new[q][rs])
            else:
                upd = _bdot_tn(jnp.where(row_head == i, kg[q], 0.0), v_new[q])
            so_ref[0, hd] = states[q][i] * gl_h + upd
    for q in groups:
        z_st = stack(z_ref, heads[q], GDN_DV)
        og = _rms(o_st[q], nw_ref[0:1, :]) * _silu(z_st)
        for i, hd in enumerate(heads[q]):
            o_ref[:, hd * GDN_DV:(hd + 1) * GDN_DV] = og[i * chunk:(i + 1) * chunk]


GDN_SUBCHUNKS = 4


def _gdn_layer(x2d, conv_prev, s0, w, gi, nw, b, t):
    n = b * t
    chunk = _chunk_of(t)
    tl = _tiling(b, t, 256)
    G, J, R, P = tl["G"], tl["J"], tl["R"], tl["P"]
    adt = BF16 if chunk % 16 == 0 else F32
    w_in = w["gdn_w_in"][gi]
    o1 = GDN_CONV_DIM
    o2 = o1 + GDN_V_DIM
    wqkv = w_in[:, :o1].astype(BF16)
    wz = w_in[:, o1:o2].astype(BF16)
    wbg = jnp.pad(w_in[:, o2:], ((0, 0), (0, LANES - 2 * GDN_V_H))).astype(BF16)
    cw = jnp.pad(w["gdn_conv_w"][gi], ((0, 8 - GDN_CONV), (0, 0)))
    gvec = jnp.zeros((8, LANES), F32)
    gvec = gvec.at[0, GDN_V_H:2 * GDN_V_H].set(w["gdn_a_log"][gi]).at[1, GDN_V_H:2 * GDN_V_H].set(w["gdn_dt_bias"][gi])
    bc = chunk if chunk == 64 else R
    tri = _chunk_masks(chunk, bc)
    nwa = jnp.concatenate([nw[0:1], jnp.zeros((7, D_MODEL), F32)])
    row = lambda g, j: (g * J + j, 0)
    st_spec = pl.BlockSpec((P, GDN_CONV_DIM), lambda g, j: (g, 0))
    qn, kn, v, z, beta, gcs, st = pl.pallas_call(
        _gdn_proj_body,
        grid=(G, J),
        in_specs=[pl.BlockSpec((R, D_MODEL), row), st_spec, _const_spec((8, D_MODEL)), _const_spec(wqkv.shape),
                  _const_spec(wz.shape), _const_spec(wbg.shape), _const_spec(cw.shape), _const_spec(gvec.shape),
                  _const_spec(tri.shape)],
        out_specs=[pl.BlockSpec((R, GDN_QK_DIM), row), pl.BlockSpec((R, GDN_QK_DIM), row),
                   pl.BlockSpec((R, GDN_V_DIM), row), pl.BlockSpec((R, GDN_V_DIM), row),
                   pl.BlockSpec((R, LANES), row), pl.BlockSpec((R, LANES), row), st_spec],
        out_shape=[jax.ShapeDtypeStruct((n, GDN_QK_DIM), adt), jax.ShapeDtypeStruct((n, GDN_QK_DIM), adt),
                   jax.ShapeDtypeStruct((n, GDN_V_DIM), adt), jax.ShapeDtypeStruct((n, GDN_V_DIM), adt),
                   jax.ShapeDtypeStruct((n, LANES), F32), jax.ShapeDtypeStruct((n, LANES), F32),
                   jax.ShapeDtypeStruct((b * SUBLANES, GDN_CONV_DIM), F32)],
        scratch_shapes=[pltpu.VMEM((P, GDN_CONV_DIM), F32)],
        compiler_params=_params(2),
        name="gdn_proj",
    )(x2d, _pad_state(conv_prev), nwa, wqkv, wz, wbg, cw, gvec, tri)
    conv_new = st.reshape(b, SUBLANES, GDN_CONV_DIM)[:, SUBLANES - (GDN_CONV - 1):]

    nh = GROUP_ROWS // chunk
    ng = GDN_V_H // nh
    nsub = GDN_SUBCHUNKS if (t // chunk) % GDN_SUBCHUNKS == 0 else 1
    nct = t // (chunk * nsub)
    br = chunk * nsub
    crow = lambda bb, j: (bb * nct + j, 0)
    sspec = pl.BlockSpec((1, GDN_V_H, GDN_DK, GDN_DV), lambda bb, j: (bb, 0, 0, 0))
    nwn = jnp.concatenate([w["gdn_norm_w"][gi][None, :], jnp.zeros((7, GDN_DV), F32)])
    o, s_new = pl.pallas_call(
        functools.partial(_gdn_chunk_body, nh=nh, chunk=chunk, nsub=nsub),
        grid=(b, nct),
        in_specs=[pl.BlockSpec((br, GDN_QK_DIM), crow), pl.BlockSpec((br, GDN_QK_DIM), crow),
                  pl.BlockSpec((br, GDN_V_DIM), crow), pl.BlockSpec((br, GDN_V_DIM), crow),
                  pl.BlockSpec((br, LANES), crow), pl.BlockSpec((br, LANES), crow),
                  sspec, _const_spec((8, GDN_DV))],
        out_specs=[pl.BlockSpec((br, GDN_V_DIM), crow), sspec],
        out_shape=[jax.ShapeDtypeStruct((n, GDN_V_DIM), adt),
                   jax.ShapeDtypeStruct((b, GDN_V_H, GDN_DK, GDN_DV), F32)],
        compiler_params=_params(2),
        name="gdn_chunk",
    )(qn, kn, v, z, gcs, beta, s0, nwn)
    nwb = jnp.concatenate([nw[1:2], jnp.zeros((7, D_MODEL), F32)])
    x_new = _outproj(x2d, o, None, w["gdn_wo"][gi].astype(BF16), nwb, "gdn_out")
    return x_new, conv_new, s_new


def _trunk(x, pos, rw_s, rw_shift, gdn_s, gdn_conv, ffn_conv, w, paged):
    b, t, _ = x.shape
    x2d = x.reshape(b * t, D_MODEL)
    new = {k: [] for k in ("rw_S", "rw_shift", "mla_c", "mla_kr", "gdn_S", "gdn_conv", "ffn_conv")}
    v_first = None
    ri = mi = gi = 0
    for l, kind in enumerate(LAYER_MIXER):
        nw = w["norm_w"][l]
        if kind == 0:
            x2d, sh, s_new, v_first = _rwkv_layer(x2d, rw_shift[ri], rw_s[ri], v_first, w, ri, nw, b, t)
            new["rw_S"].append(s_new)
            new["rw_shift"].append(sh)
            ri += 1
        elif kind == 1:
            x2d, c, kr = _mla_layer(x2d, pos, w, mi, nw, b, t, None if paged is None else
                                    (paged[0][mi], paged[1][mi], paged[2]))
            new["mla_c"].append(c)
            new["mla_kr"].append(kr)
            mi += 1
        else:
            x2d, cb, s_new = _gdn_layer(x2d, gdn_conv[gi], gdn_s[gi], w, gi, nw, b, t)
            new["gdn_S"].append(s_new)
            new["gdn_conv"].append(cb)
            gi += 1
        nwf = jnp.concatenate([nw[2:4], jnp.zeros((6, D_MODEL), F32)])
        cwb = jnp.concatenate([w["ffn_conv_w"][l], w["ffn_conv_b"][l][None, :],
                               jnp.zeros((8 - FFN_CONV - 1, 2 * D_FF), F32)])
        x2d, st = _ffn(x2d, _pad_state(ffn_conv[l]), nwf, w["ffn_w_up"][l].astype(BF16), cwb,
                       w["ffn_w_down"][l].astype(BF16), b, t)
        new["ffn_conv"].append(st.reshape(b, SUBLANES, 2 * D_FF)[:, SUBLANES - (FFN_CONV - 1):])
    return x2d.reshape(b, t, D_MODEL), {k: jnp.stack(v) for k, v in new.items()}


def kernel(x_prompt, x_sample, state_rwkv_wkv, state_rwkv_shift, cache_mla_latent, cache_mla_krope, state_gdn_S, state_gdn_conv, state_ffn_conv, page_table, norm_w, rw_mu, rw_wrkv, rw_w0, rw_w1, rw_w2, rw_a0, rw_a1, rw_a2, rw_v0, rw_v1, rw_v2, rw_g1, rw_g2, rw_kk, rw_ka, rw_rk, rw_lnx_w, rw_lnx_b, rw_wo, mla_w_in, mla_q_norm, mla_kv_norm, mla_w_qb, mla_w_uk, mla_w_uv, mla_wo, gdn_w_in, gdn_conv_w, gdn_a_log, gdn_dt_bias, gdn_norm_w, gdn_wo, ffn_w_up, ffn_conv_w, ffn_conv_b, ffn_w_down):
    w = dict(norm_w=norm_w, rw_mu=rw_mu, rw_wrkv=rw_wrkv, rw_w0=rw_w0, rw_w1=rw_w1, rw_w2=rw_w2, rw_a0=rw_a0,
             rw_a1=rw_a1, rw_a2=rw_a2, rw_v0=rw_v0, rw_v1=rw_v1, rw_v2=rw_v2, rw_g1=rw_g1, rw_g2=rw_g2,
             rw_kk=rw_kk, rw_ka=rw_ka, rw_rk=rw_rk, rw_lnx_w=rw_lnx_w, rw_lnx_b=rw_lnx_b, rw_wo=rw_wo,
             mla_w_in=mla_w_in, mla_q_norm=mla_q_norm, mla_kv_norm=mla_kv_norm, mla_w_qb=mla_w_qb,
             mla_w_uk=mla_w_uk, mla_w_uv=mla_w_uv, mla_wo=mla_wo, gdn_w_in=gdn_w_in, gdn_conv_w=gdn_conv_w,
             gdn_a_log=gdn_a_log, gdn_dt_bias=gdn_dt_bias, gdn_norm_w=gdn_norm_w, gdn_wo=gdn_wo,
             ffn_w_up=ffn_w_up, ffn_conv_w=ffn_conv_w, ffn_conv_b=ffn_conv_b, ffn_w_down=ffn_w_down)
    b, t = x_prompt.shape[0], x_prompt.shape[1]
    n_rw, n_gdn, depth = state_rwkv_wkv.shape[0], state_gdn_S.shape[0], state_ffn_conv.shape[0]
    y_p, sp = _trunk(
        x_prompt, jnp.arange(t),
        jnp.zeros((n_rw, b) + state_rwkv_wkv.shape[2:], F32), jnp.zeros((n_rw, b, D_MODEL), F32),
        jnp.zeros((n_gdn, b) + state_gdn_S.shape[2:], F32), jnp.zeros((n_gdn, b) + state_gdn_conv.shape[2:], F32),
        jnp.zeros((depth, b) + state_ffn_conv.shape[2:], F32), w, None)
    past_len = page_table.shape[1] * cache_mla_latent.shape[2]
    pos_s = past_len + jnp.arange(x_sample.shape[1])
    y_s, ss = _trunk(x_sample, pos_s, state_rwkv_wkv, state_rwkv_shift, state_gdn_S, state_gdn_conv,
                     state_ffn_conv, w, (cache_mla_latent, cache_mla_krope, page_table))
    names = ("rw_S", "rw_shift", "mla_c", "mla_kr", "gdn_S", "gdn_conv", "ffn_conv")
    return (y_p, y_s) + tuple(sp[k] for k in names) + tuple(ss[k] for k in names)
```

```python
import functools

import jax
import jax.numpy as jnp
from jax import lax
from jax.experimental import pallas as pl
from jax.experimental.pallas import tpu as pltpu

F32 = jnp.float32
BF16 = jnp.bfloat16
HIGHEST = lax.Precision.HIGHEST

D_MODEL = 1024
NORM_EPS = 1e-6
RW_N = 64
RW_H = D_MODEL // RW_N
RW_LNX_EPS = 64e-5
MLA_H = 16
MLA_NOPE = 64
MLA_ROPE = 32
MLA_V = 64
MLA_Q_LORA = 512
MLA_KV_LORA = 256
MLA_SCALE = (MLA_NOPE + MLA_ROPE) ** -0.5
ROPE_THETA = 10000.0
MLA_QK = MLA_KV_LORA + 128
GDN_QK_H = 8
GDN_V_H = 16
GDN_DK = 128
GDN_DV = 128
GDN_QK_DIM = GDN_QK_H * GDN_DK
GDN_V_DIM = GDN_V_H * GDN_DV
GDN_CONV_DIM = 2 * GDN_QK_DIM + GDN_V_DIM
GDN_CONV = 4
D_FF = 2816
FFN_CONV = 3
LAYER_MIXER = (0, 1, 2, 0)

SUBLANES = 8
LANES = 128
GROUP_ROWS = 128
VMEM_LIMIT = 56 * 1024 * 1024


def _rms(x, w):
    return x * lax.rsqrt(jnp.mean(x * x, axis=-1, keepdims=True) + NORM_EPS) * w


def _bdot(a, b):
    return jnp.dot(a.astype(BF16), b.astype(BF16), preferred_element_type=F32)


def _bdot_nt(a, b):
    return lax.dot_general(a.astype(BF16), b.astype(BF16), (((1,), (1,)), ((), ())),
                           preferred_element_type=F32)


def _bdot_tn(a, b):
    return lax.dot_general(a.astype(BF16), b.astype(BF16), (((0,), (0,)), ((), ())),
                           preferred_element_type=F32)


def _hdot(a, b):
    return jnp.dot(a, b, precision=HIGHEST, preferred_element_type=F32)


def _sigmoid(x):
    return 1.0 / (1.0 + jnp.exp(-x))


def _softplus(x):
    return jnp.maximum(x, 0.0) + jnp.log(1.0 + jnp.exp(-jnp.abs(x)))


def _silu(x):
    return x * _sigmoid(x)


def _shift_rows(u, prev, s):
    rows, cols = u.shape
    p = prev.shape[0]
    rolled = pltpu.roll(u, s, 0)
    fix = pltpu.roll(prev, (p - SUBLANES + s) % p, 0)
    t = lax.broadcasted_iota(jnp.int32, (p, cols), 0) % SUBLANES
    if p == rows:
        return jnp.where(t < s, fix, rolled)
    head = jnp.where(t < s, fix, rolled[:SUBLANES])
    return jnp.concatenate([head, rolled[SUBLANES:]], axis=0)


def _lane_group_sum(x, ones2):
    parts = []
    for i in range(x.shape[1] // LANES):
        xs = x[:, i * LANES:(i + 1) * LANES]
        hi = xs.astype(BF16)
        lo = (xs - hi.astype(F32)).astype(BF16)
        parts.append(jnp.dot(jnp.concatenate([hi, lo], axis=1), ones2, preferred_element_type=F32))
    return parts[0] if len(parts) == 1 else jnp.concatenate(parts, axis=1)


def _split_dot(m2, x):
    hi = x.astype(BF16)
    lo = (x - hi.astype(F32)).astype(BF16)
    return jnp.dot(m2, jnp.concatenate([hi, lo], axis=0), preferred_element_type=F32)


def _chunk_cumsum(x, tri):
    bc = tri.shape[0]
    parts = [_hdot(tri, x[i * bc:(i + 1) * bc]) for i in range(x.shape[0] // bc)]
    return parts[0] if len(parts) == 1 else jnp.concatenate(parts, axis=0)


def _tiling(b, t, tt_max):
    if t == SUBLANES:
        return dict(G=1, J=1, R=b * t, P=b * t)
    tt = min(t, tt_max)
    assert t % tt == 0 and tt % 64 == 0, (t, tt)
    return dict(G=b, J=t // tt, R=tt, P=SUBLANES)


def _chunk_of(t):
    return 64 if t % 64 == 0 else t


def _const_spec(shape):
    nd = len(shape)
    return pl.BlockSpec(shape, lambda *_: (0,) * nd, pipeline_mode=pl.Buffered(1))


def _params(n_axes):
    return pltpu.CompilerParams(dimension_semantics=("arbitrary",) * n_axes,
                                vmem_limit_bytes=VMEM_LIMIT)


def _pad_state(st):
    b, k1, c = st.shape
    return jnp.pad(st, ((0, 0), (SUBLANES - k1, 0), (0, 0))).reshape(b * SUBLANES, c)


def _chunk_masks(chunk, rows):
    i = jnp.arange(rows)
    same = (i[:, None] // chunk) == (i[None, :] // chunk)
    tri = same & ((i[None, :] % chunk) <= (i[:, None] % chunk))
    return tri.astype(F32)


FFN_CW = 256


ROW_BLOCK = 64


def _store_strided(ref, val):
    ref[...] = pltpu.einshape("(vs)d->(sv)d", val, s=SUBLANES)


def _stage(buf_ref, u, carry_ref, st_ref, sl, taps):
    rows = u.shape[0]
    hb = taps * SUBLANES
    buf_ref[hb:hb + rows, :] = u
    first = lax.broadcasted_iota(jnp.int32, (SUBLANES, u.shape[1]), 0) == 0
    for i in range(taps):
        back = taps - i
        src = u[rows - back * SUBLANES:rows - (back - 1) * SUBLANES, :]
        crow = carry_ref[SUBLANES - back:SUBLANES - back + 1, sl]
        buf_ref[i * SUBLANES:(i + 1) * SUBLANES, :] = jnp.where(first, crow, pltpu.roll(src, 1, 0))
    sq = SUBLANES * SUBLANES
    tail = pltpu.einshape("(vs)d->(sv)d", u[rows - sq:, :], s=SUBLANES)[sq - SUBLANES:, :]
    carry_ref[:, sl] = tail
    st_ref[:, sl] = tail


def _taps(buf_ref, r0, nrows, taps):
    hb = taps * SUBLANES
    cur = buf_ref[hb + r0:hb + r0 + nrows, :]
    return cur, [buf_ref[hb - j * SUBLANES + r0:hb - j * SUBLANES + r0 + nrows, :] for j in range(taps, 0, -1)]


def _mixer_out(kind, refs):
    if kind == "mla":
        o_ref, wuv_ref, wo_ref = refs
        parts = []
        for pr in range(MLA_H // 2):
            wp = wuv_ref[pr]
            parts.append(jnp.dot(o_ref[2 * pr].astype(BF16), wp[:MLA_KV_LORA], preferred_element_type=F32)
                         + jnp.dot(o_ref[2 * pr + 1].astype(BF16), wp[MLA_KV_LORA:], preferred_element_type=F32))
        y = jnp.concatenate(parts, axis=1)
    elif kind == "gated":
        y_ref, g_ref, wo_ref = refs
        y = y_ref[...].astype(F32) * g_ref[...].astype(F32)
    else:
        y_ref, wo_ref = refs
        y = y_ref[...]
    return jnp.dot(y.astype(BF16), wo_ref[...], preferred_element_type=F32)


def _ffn_body(*refs, kind, n_pre):
    x_ref = refs[0]
    pre_refs = refs[1:1 + n_pre]
    (prev_ref, nw_ref, wup_ref, cwb_ref, wdn_ref, xo_ref, st_ref, carry_ref, act_ref, buf_ref) = refs[1 + n_pre:]

    @pl.when(pl.program_id(1) == 0)
    def _():
        carry_ref[...] = prev_ref[...]

    rows = x_ref.shape[0]
    stacked = carry_ref.shape[0] == rows
    x = x_ref[...] + _rms(_mixer_out(kind, pre_refs), nw_ref[2:3, :])
    if not stacked:
        x = pltpu.einshape("(sv)d->(vs)d", x, s=SUBLANES)
    h = _rms(x, nw_ref[0:1, :]).astype(BF16)
    nch = D_FF // FFN_CW

    def cols(c, half):
        return slice(half * D_FF + c * FFN_CW, half * D_FF + (c + 1) * FFN_CW)

    def conv_gate(c, taps):
        ys = []
        for half in range(2):
            sl = cols(c, half)
            u, (u2, u1) = taps[half]
            ys.append(cwb_ref[0:1, sl] * u2 + cwb_ref[1:2, sl] * u1 + cwb_ref[2:3, sl] * u + cwb_ref[3:4, sl])
        return (_silu(ys[0]) * ys[1]).astype(BF16)

    def up(c):
        us = [jnp.dot(h, wup_ref[:, cols(c, half)], preferred_element_type=F32) for half in range(2)]
        if stacked:
            return us
        for half in range(2):
            _stage(buf_ref.at[(c % 2) * 2 + half], us[half], carry_ref, st_ref, cols(c, half), FFN_CONV - 1)
        return None

    u_cur = up(0)
    for c in range(nch):
        u_nxt = up(c + 1) if c + 1 < nch else None
        csl = slice(c * FFN_CW, (c + 1) * FFN_CW)
        if stacked:
            taps = []
            for half in range(2):
                sl = cols(c, half)
                u = u_cur[half]
                prev = carry_ref[:, sl]
                taps.append((u, [_shift_rows(u, prev, 2), _shift_rows(u, prev, 1)]))
                carry_ref[:, sl] = u
                st_ref[:, sl] = u
            act_ref[:, csl] = conv_gate(c, taps)
        else:
            for r0 in range(0, rows, ROW_BLOCK):
                taps = [_taps(buf_ref.at[(c % 2) * 2 + half], r0, ROW_BLOCK, FFN_CONV - 1) for half in range(2)]
                act_ref[r0:r0 + ROW_BLOCK, csl] = conv_gate(c, taps)
        u_cur = u_nxt
    f = jnp.dot(act_ref[...], wdn_ref[...], preferred_element_type=F32)
    out = x + _rms(f, nw_ref[1:2, :])
    if stacked:
        xo_ref[...] = out
    else:
        _store_strided(xo_ref, out)


def _ffn(x2d, pre, prev, nw, wup, cwb, wdn, b, t):
    tl = _tiling(b, t, 512)
    G, J, R, P = tl["G"], tl["J"], tl["R"], tl["P"]
    n = b * t
    kind, acts, wts = pre
    row = lambda g, j: (g * J + j, 0)
    if kind == "mla":
        act_specs = [pl.BlockSpec((MLA_H, R, MLA_KV_LORA), lambda g, j: (0, g * J + j, 0))]
    else:
        act_specs = [pl.BlockSpec((R, a.shape[1]), row) for a in acts]
    pre_specs = act_specs + [_const_spec(wt.shape) for wt in wts]
    return pl.pallas_call(
        functools.partial(_ffn_body, kind=kind, n_pre=len(pre_specs)),
        grid=(G, J),
        in_specs=[pl.BlockSpec((R, D_MODEL), row)] + pre_specs + [
            pl.BlockSpec((P, 2 * D_FF), lambda g, j: (g, 0)),
            _const_spec((8, D_MODEL)),
            _const_spec((D_MODEL, 2 * D_FF)),
            _const_spec((8, 2 * D_FF)),
            _const_spec((D_FF, D_MODEL)),
        ],
        out_specs=[
            pl.BlockSpec((R, D_MODEL), lambda g, j: (g * J + j, 0)),
            pl.BlockSpec((P, 2 * D_FF), lambda g, j: (g, 0)),
        ],
        out_shape=[jax.ShapeDtypeStruct((n, D_MODEL), F32),
                   jax.ShapeDtypeStruct((b * SUBLANES, 2 * D_FF), F32)],
        scratch_shapes=[pltpu.VMEM((P, 2 * D_FF), F32), pltpu.VMEM((R, D_FF), BF16),
                        pltpu.VMEM((4, R + (FFN_CONV - 1) * SUBLANES, FFN_CW), F32)],
        compiler_params=_params(2),
        name="conv_ffn",
    )(x2d, *acts, *wts, prev, nw, wup, cwb, wdn)


def _outproj_body(*refs, gated):
    x_ref, y_ref = refs[0], refs[1]
    wo_ref, nw_ref, xo_ref = refs[-3:]
    y = y_ref[...]
    if gated:
        y = y.astype(F32) * refs[2][...].astype(F32)
    o = jnp.dot(y.astype(BF16), wo_ref[...], preferred_element_type=F32)
    xo_ref[...] = x_ref[...] + _rms(o, nw_ref[0:1, :])


def _outproj(x2d, y2d, gate2d, wo, nw, name):
    n, k = y2d.shape
    R = min(n, 512)
    row = lambda i: (i, 0)
    acts = [y2d] if gate2d is None else [y2d, gate2d]
    return pl.pallas_call(
        functools.partial(_outproj_body, gated=gate2d is not None),
        grid=(n // R,),
        in_specs=[pl.BlockSpec((R, D_MODEL), row)] + [pl.BlockSpec((R, k), row)] * len(acts)
        + [_const_spec((k, D_MODEL)), _const_spec((8, D_MODEL))],
        out_specs=pl.BlockSpec((R, D_MODEL), row),
        out_shape=jax.ShapeDtypeStruct((n, D_MODEL), F32),
        compiler_params=_params(1),
        name=name,
    )(x2d, *acts, wo, nw)


def _rwkv_proj_body(*refs, has_vres, chunk):
    it = iter(refs)
    x_ref, prev_ref = next(it), next(it)
    vf_ref = next(it) if has_vres else None
    vec_ref, wrkv_ref, w1_ref, w2_ref, a1_ref, a2_ref = (next(it) for _ in range(6))
    v1_ref, v2_ref = (next(it), next(it)) if has_vres else (None, None)
    g1_ref, g2_ref, tri_ref, ones_ref = (next(it) for _ in range(4))
    rt_ref, kt_ref, at_ref, bt_ref, v_ref, g_ref, gl_ref, hl_ref, carry_ref = (next(it) for _ in range(9))

    @pl.when(pl.program_id(1) == 0)
    def _():
        carry_ref[...] = prev_ref[...]

    x = x_ref[...]
    rows = x.shape[0]
    p = carry_ref.shape[0]
    h = _rms(x, vec_ref[10:11, :])
    d = _shift_rows(h, carry_ref[...], 1) - h
    tail = h[rows - p:, :]
    carry_ref[...] = tail
    hl_ref[...] = tail

    def mix(i):
        return (h + d * vec_ref[i:i + 1, :]).astype(BF16)

    r = jnp.dot(mix(0), wrkv_ref[0], preferred_element_type=F32)
    k = jnp.dot(mix(1), wrkv_ref[1], preferred_element_type=F32)
    xv = mix(2)
    v = jnp.dot(xv, wrkv_ref[2], preferred_element_type=F32)
    w_lora = _bdot(jnp.tanh(_bdot(mix(3), w1_ref[...])), w2_ref[...])
    v_lora = _bdot(_bdot(xv, v1_ref[...]), v2_ref[...]) if has_vres else None
    a_lora = _bdot(_bdot(mix(4), a1_ref[...]), a2_ref[...])
    g_ref[...] = _bdot(_sigmoid(_bdot(mix(5), g1_ref[...])), g2_ref[...]).astype(g_ref.dtype)
    adt = rt_ref.dtype

    bc = tri_ref.shape[0]
    for r0 in range(0, rows, bc):
        for l0 in range(0, D_MODEL, PROJ_LANES):
            rs, ls = slice(r0, r0 + bc), slice(l0, l0 + PROJ_LANES)
            vb = v[rs, ls]
            if has_vres:
                vb = vb + (vf_ref[rs, ls] - vb) * _sigmoid(vec_ref[11:12, ls] + v_lora[rs, ls])
            v_ref[rs, ls] = vb.astype(adt)
            a = _sigmoid(vec_ref[7:8, ls] + a_lora[rs, ls])
            kb = k[rs, ls]
            kk = kb * vec_ref[8:9, ls]
            kk = kk * lax.rsqrt(_lane_group_sum(kk * kk, ones_ref[...]) + 1e-6)
            kb = kb * (1.0 + (a - 1.0) * vec_ref[9:10, ls])
            w = -_softplus(-(vec_ref[6:7, ls] + w_lora[rs, ls])) - 0.5
            lw = -jnp.exp(w)
            cum = _split_dot(tri_ref[...], lw)
            e_bwd = jnp.exp(-cum)
            rt_ref[rs, ls] = (r[rs, ls] * jnp.exp(cum)).astype(adt)
            kt_ref[rs, ls] = (kb * e_bwd).astype(adt)
            at_ref[rs, ls] = (-kk * jnp.exp(cum - lw)).astype(adt)
            bt_ref[rs, ls] = (kk * a * e_bwd).astype(adt)
            for c in range(bc // chunk):
                row = (c + 1) * chunk - 1
                gl_ref[r0 // chunk + c, :, ls] = jnp.exp(cum[row:row + 1, :])


def _rwkv_scan_body(rt_ref, kt_ref, at_ref, bt_ref, v_ref, gl_ref, h0_ref, vec_ref, y_ref, ho_ref,
                    *, nh, chunk, nsub):
    @pl.when(pl.program_id(1) == 0)
    def _():
        ho_ref[...] = h0_ref[...]

    gl_lanes = nh * RW_N
    ng = RW_H // nh
    gc = nh * chunk
    row_head = lax.broadcasted_iota(jnp.int32, (gc, gl_lanes), 0) // chunk
    lane_head = lax.broadcasted_iota(jnp.int32, (gc, gl_lanes), 1) // RW_N
    own = row_head == lane_head
    ri = lax.broadcasted_iota(jnp.int32, (gc, gc), 0)
    ci = lax.broadcasted_iota(jnp.int32, (gc, gc), 1)
    same = (ri // chunk) == (ci // chunk)
    strict = same & ((ri % chunk) > (ci % chunk))
    incl = same & ((ri % chunk) >= (ci % chunk))
    eye = (ri == ci).astype(F32)
    eye_l = (lax.broadcasted_iota(jnp.int32, (gl_lanes, gl_lanes), 0)
             == lax.broadcasted_iota(jnp.int32, (gl_lanes, gl_lanes), 1))
    merged = gc == GROUP_ROWS
    groups = range(ng)
    sls = [slice(q * gl_lanes, (q + 1) * gl_lanes) for q in groups]
    keys = [(sc, q) for sc in range(nsub) for q in groups]

    def blockdiag(ref, key):
        xg = ref[key[0] * chunk:(key[0] + 1) * chunk, sls[key[1]]]
        xx = jnp.concatenate([xg] * nh, axis=0) if nh > 1 else xg
        return jnp.where(own, xx, jnp.zeros_like(xx))

    r_bd = {k: blockdiag(rt_ref, k) for k in keys}
    k_bd = {k: blockdiag(kt_ref, k) for k in keys}
    a_bd = {k: blockdiag(at_ref, k) for k in keys}
    b_bd = {k: blockdiag(bt_ref, k) for k in keys}
    v_f = {k: blockdiag(v_ref, k) for k in keys}
    v_bd = {k: v_f[k].astype(BF16) for k in keys}
    bonus = {k: jnp.sum(r_bd[k].astype(F32) * k_bd[k] * vec_ref[2:3, sls[k[1]]], axis=1, keepdims=True)
             for k in keys}
    if merged:
        ar = {k: jnp.concatenate([a_bd[k], r_bd[k]], axis=0).astype(BF16) for k in keys}
        bk = {k: jnp.concatenate([b_bd[k], k_bd[k]], axis=0).astype(BF16) for k in keys}
        amat = {k: _bdot_nt(ar[k], bk[k]) for k in keys}
        a_ab = {k: jnp.where(strict, amat[k][:gc, :gc], 0.0) for k in keys}
        a_ak = {k: jnp.where(strict, amat[k][:gc, gc:], 0.0).astype(BF16) for k in keys}
        a_rbk = {k: jnp.concatenate([jnp.where(incl, amat[k][gc:, :gc], 0.0),
                                     jnp.where(incl, amat[k][gc:, gc:], 0.0)], axis=1).astype(BF16) for k in keys}
    else:
        ab_ = {k: a_bd[k].astype(BF16) for k in keys}
        rb_ = {k: r_bd[k].astype(BF16) for k in keys}
        bb_ = {k: b_bd[k].astype(BF16) for k in keys}
        kb_ = {k: k_bd[k].astype(BF16) for k in keys}
        a_ab = {k: jnp.where(strict, _bdot_nt(ab_[k], bb_[k]), 0.0) for k in keys}
        a_ak = {k: jnp.where(strict, _bdot_nt(ab_[k], kb_[k]), 0.0).astype(BF16) for k in keys}
        a_rb = {k: jnp.where(incl, _bdot_nt(rb_[k], bb_[k]), 0.0).astype(BF16) for k in keys}
        a_rk = {k: jnp.where(incl, _bdot_nt(rb_[k], kb_[k]), 0.0).astype(BF16) for k in keys}
    akv = {k: _bdot(a_ak[k], v_bd[k]) for k in keys}

    p = {k: a_ab[k].astype(BF16) for k in keys}
    x = {k: eye + a_ab[k] for k in keys}
    span = 2
    if span < chunk:
        p = {k: _bdot(p[k], p[k]) for k in keys}
    while span < chunk:
        last = span * 2 >= chunk
        if merged and not last:
            px = {k: _bdot(p[k], jnp.concatenate([p[k].astype(BF16), x[k].astype(BF16)], axis=1)) for k in keys}
            p = {k: px[k][:, :gc] for k in keys}
            x = {k: x[k] + px[k][:, gc:] for k in keys}
        else:
            pb = {k: p[k].astype(BF16) for k in keys}
            x = {k: x[k] + _bdot(pb[k], x[k]) for k in keys}
            if not last:
                p = {k: _bdot(pb[k], pb[k]) for k in keys}
        span *= 2
    tinv = {k: x[k].astype(BF16) for k in keys}

    for sc in range(nsub):
        hs = [ho_ref[0, q] for q in groups]
        hs_b = [h.astype(BF16) for h in hs]
        gl_rows = [gl_ref[sc, :, sl] for sl in sls]
        if merged:
            arh = [_bdot(ar[(sc, q)], hs_b[q]) for q in groups]
            u = [_bdot(tinv[(sc, q)], arh[q][:gc] + akv[(sc, q)]).astype(BF16) for q in groups]
            uv = [jnp.concatenate([u[q], v_bd[(sc, q)]], axis=0) for q in groups]
            y_bd = [arh[q][gc:] + _bdot(a_rbk[(sc, q)], uv[q]) for q in groups]
            for q in groups:
                gl_col = jnp.sum(jnp.where(eye_l, gl_rows[q], 0.0), axis=1, keepdims=True)
                bk_g = jnp.concatenate([b_bd[(sc, q)] * gl_rows[q], k_bd[(sc, q)] * gl_rows[q]], axis=0)
                ho_ref[0, q] = hs[q] * gl_col + _bdot_tn(bk_g, uv[q])
        else:
            ah = [_bdot(ab_[(sc, q)], hs_b[q]) for q in groups]
            rh = [_bdot(rb_[(sc, q)], hs_b[q]) for q in groups]
            u = [_bdot(tinv[(sc, q)], ah[q] + akv[(sc, q)]).astype(BF16) for q in groups]
            y_bd = [rh[q] + _bdot(a_rb[(sc, q)], u[q]) + _bdot(a_rk[(sc, q)], v_bd[(sc, q)]) for q in groups]
            for q in groups:
                gl_col = jnp.sum(jnp.where(eye_l, gl_rows[q], 0.0), axis=1, keepdims=True)
                ho_ref[0, q] = (hs[q] * gl_col + _bdot_tn(b_bd[(sc, q)] * gl_rows[q], u[q])
                                + _bdot_tn(k_bd[(sc, q)] * gl_rows[q], v_bd[(sc, q)]))

        for q in groups:
            sl = sls[q]
            mu = jnp.sum(y_bd[q], axis=1, keepdims=True) * (1.0 / RW_N)
            yc = jnp.where(own, y_bd[q] - mu, 0.0)
            var = jnp.sum(yc * yc, axis=1, keepdims=True) * (1.0 / RW_N)
            tot = (yc * lax.rsqrt(var + RW_LNX_EPS) * vec_ref[0:1, sl] + jnp.where(own, vec_ref[1:2, sl], 0.0)
                   + bonus[(sc, q)] * v_f[(sc, q)])
            y = tot[0:chunk]
            for hh in range(1, nh):
                y = y + tot[hh * chunk:(hh + 1) * chunk]
            y_ref[sc * chunk:(sc + 1) * chunk, sl] = y.astype(y_ref.dtype)


def _rwkv_scan_body_old(rt_ref, kt_ref, at_ref, bt_ref, v_ref, gl_ref, h0_ref, vec_ref, ones_ref,
                        y_ref, ho_ref, *, nh, chunk):
    @pl.when(pl.program_id(1) == 0)
    def _():
        ho_ref[...] = h0_ref[...]

    gl_lanes = nh * RW_N
    ng = RW_H // nh
    gc = nh * chunk
    row_head = lax.broadcasted_iota(jnp.int32, (gc, gl_lanes), 0) // chunk
    lane_head = lax.broadcasted_iota(jnp.int32, (gc, gl_lanes), 1) // RW_N
    own = row_head == lane_head
    ri = lax.broadcasted_iota(jnp.int32, (gc, gc), 0)
    ci = lax.broadcasted_iota(jnp.int32, (gc, gc), 1)
    same = (ri // chunk) == (ci // chunk)
    strict = same & ((ri % chunk) > (ci % chunk))
    incl = same & ((ri % chunk) >= (ci % chunk))
    eye = (ri == ci).astype(F32)
    eye_l = (lax.broadcasted_iota(jnp.int32, (gl_lanes, gl_lanes), 0)
             == lax.broadcasted_iota(jnp.int32, (gl_lanes, gl_lanes), 1))
    ones_bd = ones_ref[...]

    def blockdiag(xg):
        xx = jnp.concatenate([xg] * nh, axis=0) if nh > 1 else xg
        return jnp.where(own, xx, 0.0)

    groups = range(ng)
    sls = [slice(q * gl_lanes, (q + 1) * gl_lanes) for q in groups]
    merged = gc == GROUP_ROWS
    gl_rows = [gl_ref[0, :, sl] for sl in sls]
    r_bd = [blockdiag(rt_ref[:, sl]) for sl in sls]
    k_bd = [blockdiag(kt_ref[:, sl]) for sl in sls]
    a_bd = [blockdiag(at_ref[:, sl]) for sl in sls]
    b_bd = [blockdiag(bt_ref[:, sl]) for sl in sls]
    v_bd = [blockdiag(v_ref[:, sl]).astype(BF16) for sl in sls]
    hs = [ho_ref[0, q] for q in groups]
    hs_b = [h.astype(BF16) for h in hs]
    if merged:
        ar = [jnp.concatenate([a_bd[q], r_bd[q]], axis=0).astype(BF16) for q in groups]
        bk = [jnp.concatenate([b_bd[q], k_bd[q]], axis=0).astype(BF16) for q in groups]
        amat = [_bdot_nt(ar[q], bk[q]) for q in groups]
        a_ab = [jnp.where(strict, m[:gc, :gc], 0.0) for m in amat]
        a_ak = [jnp.where(strict, m[:gc, gc:], 0.0).astype(BF16) for m in amat]
        a_rbk = [jnp.concatenate([jnp.where(incl, m[gc:, :gc], 0.0), jnp.where(incl, m[gc:, gc:], 0.0)],
                                 axis=1).astype(BF16) for m in amat]
        arh = [_bdot(ar[q], hs_b[q]) for q in groups]
        ah = [m[:gc] for m in arh]
        rh = [m[gc:] for m in arh]
    else:
        ab_, rb_ = [x.astype(BF16) for x in a_bd], [x.astype(BF16) for x in r_bd]
        bb_, kb_ = [x.astype(BF16) for x in b_bd], [x.astype(BF16) for x in k_bd]
        a_ab = [jnp.where(strict, _bdot_nt(ab_[q], bb_[q]), 0.0) for q in groups]
        a_ak = [jnp.where(strict, _bdot_nt(ab_[q], kb_[q]), 0.0).astype(BF16) for q in groups]
        a_rb = [jnp.where(incl, _bdot_nt(rb_[q], bb_[q]), 0.0).astype(BF16) for q in groups]
        a_rk = [jnp.where(incl, _bdot_nt(rb_[q], kb_[q]), 0.0).astype(BF16) for q in groups]
        ah = [_bdot(ab_[q], hs_b[q]) for q in groups]
        rh = [_bdot(rb_[q], hs_b[q]) for q in groups]
    akv = [_bdot(a_ak[q], v_bd[q]) for q in groups]

    p = [m.astype(BF16) for m in a_ab]
    x = [eye + m for m in a_ab]
    span = 2
    if span < chunk:
        p = [_bdot(p[q], p[q]) for q in groups]
    while span < chunk:
        last = span * 2 >= chunk
        if merged and not last:
            px = [_bdot(p[q], jnp.concatenate([p[q].astype(BF16), x[q].astype(BF16)], axis=1)) for q in groups]
            p = [m[:, :gc] for m in px]
            x = [x[q] + px[q][:, gc:] for q in groups]
        else:
            pb = [m.astype(BF16) for m in p]
            x = [x[q] + _bdot(pb[q], x[q]) for q in groups]
            if not last:
                p = [_bdot(pb[q], pb[q]) for q in groups]
        span *= 2

    u = [_bdot(x[q], ah[q] + akv[q]).astype(BF16) for q in groups]
    if merged:
        uv = [jnp.concatenate([u[q], v_bd[q]], axis=0) for q in groups]
        y_bd = [rh[q] + _bdot(a_rbk[q], uv[q]) for q in groups]
        for q in groups:
            gl_col = jnp.sum(jnp.where(eye_l, gl_rows[q], 0.0), axis=1, keepdims=True)
            bk_g = jnp.concatenate([b_bd[q] * gl_rows[q], k_bd[q] * gl_rows[q]], axis=0)
            ho_ref[0, q] = hs[q] * gl_col + _bdot_tn(bk_g, uv[q])
    else:
        y_bd = [rh[q] + _bdot(a_rb[q], u[q]) + _bdot(a_rk[q], v_bd[q]) for q in groups]
        for q in groups:
            gl_col = jnp.sum(jnp.where(eye_l, gl_rows[q], 0.0), axis=1, keepdims=True)
            ho_ref[0, q] = (hs[q] * gl_col + _bdot_tn(b_bd[q] * gl_rows[q], u[q])
                            + _bdot_tn(k_bd[q] * gl_rows[q], v_bd[q]))

    for q in groups:
        sl = sls[q]
        y = y_bd[q][0:chunk]
        for hh in range(1, nh):
            y = y + y_bd[q][hh * chunk:(hh + 1) * chunk]
        mu = _lane_group_sum(y, ones_bd) * (1.0 / RW_N)
        yc = y - mu
        var = _lane_group_sum(yc * yc, ones_bd) * (1.0 / RW_N)
        yn = yc * lax.rsqrt(var + RW_LNX_EPS) * vec_ref[0:1, sl] + vec_ref[1:2, sl]
        bonus = _lane_group_sum(rt_ref[:, sl] * kt_ref[:, sl] * vec_ref[2:3, sl], ones_bd) * v_ref[:, sl]
        y_ref[:, sl] = yn + bonus


RWKV_SUBCHUNKS = 4
PROJ_LANES = 256


def _rwkv_layer(x2d, shift_prev, s0, v_first, w, ri, nw, b, t):
    n = b * t
    chunk = _chunk_of(t)
    tl = _tiling(b, t, 512)
    G, J, R, P = tl["G"], tl["J"], tl["R"], tl["P"]
    has_vres = v_first is not None
    vi = ri - 1
    adt = BF16 if chunk % 16 == 0 else F32
    bc = chunk if chunk == 64 else R
    tri = _chunk_masks(chunk, bc).astype(BF16)
    tri = jnp.concatenate([tri, tri], axis=1)
    li = jnp.arange(LANES)
    ones_bd = ((li[:, None] // RW_N) == (li[None, :] // RW_N)).astype(BF16)
    ones_bd = jnp.concatenate([ones_bd, ones_bd], axis=0)
    zero = jnp.zeros((D_MODEL,), F32)
    vec = jnp.stack([*(w["rw_mu"][ri][i] for i in range(6)), w["rw_w0"][ri], w["rw_a0"][ri], w["rw_kk"][ri],
                     w["rw_ka"][ri], nw[0], w["rw_v0"][vi] if has_vres else zero, zero, zero, zero, zero])
    row = lambda g, j: (g * J + j, 0)
    row_spec = pl.BlockSpec((R, D_MODEL), row)
    ins = [x2d, _pad_state(shift_prev[:, None, :])]
    specs = [row_spec, pl.BlockSpec((P, D_MODEL), lambda g, j: (g, 0))]
    if has_vres:
        ins.append(v_first)
        specs.append(row_spec)
    wl = [vec, w["rw_wrkv"][ri].astype(BF16), w["rw_w1"][ri].astype(BF16), w["rw_w2"][ri].astype(BF16),
          w["rw_a1"][ri].astype(BF16), w["rw_a2"][ri].astype(BF16)]
    if has_vres:
        wl += [w["rw_v1"][vi].astype(BF16), w["rw_v2"][vi].astype(BF16)]
    wl += [w["rw_g1"][ri].astype(BF16), w["rw_g2"][ri].astype(BF16), tri, ones_bd]
    ins += wl
    specs += [_const_spec(a.shape) for a in wl]
    nc_tile = R // chunk
    outs = pl.pallas_call(
        functools.partial(_rwkv_proj_body, has_vres=has_vres, chunk=chunk),
        grid=(G, J),
        in_specs=specs,
        out_specs=[row_spec] * 6 + [pl.BlockSpec((nc_tile, 1, D_MODEL), lambda g, j: (g * J + j, 0, 0)),
                                    pl.BlockSpec((P, D_MODEL), lambda g, j: (g, 0))],
        out_shape=[jax.ShapeDtypeStruct((n, D_MODEL), adt)] * 6
        + [jax.ShapeDtypeStruct((n // chunk, 1, D_MODEL), F32), jax.ShapeDtypeStruct((b * SUBLANES, D_MODEL), F32)],
        scratch_shapes=[pltpu.VMEM((P, D_MODEL), F32)],
        compiler_params=_params(2),
        name="rwkv_proj",
    )(*ins)
    rt, kt, at, bt, v, g, gl, hl = outs
    shift_new = hl.reshape(b, SUBLANES, D_MODEL)[:, -1]

    nh = LANES // RW_N
    ng = RW_H // nh
    gl_lanes = nh * RW_N
    hkv = jnp.swapaxes(s0, -1, -2).reshape(b, ng, nh, RW_N, RW_N)
    zblk = jnp.zeros((b, ng, RW_N, RW_N), F32)
    h0 = jnp.concatenate(
        [jnp.concatenate([hkv[:, :, i] if i == jj else zblk for jj in range(nh)], axis=-1) for i in range(nh)],
        axis=-2)
    svec = jnp.stack([w["rw_lnx_w"][ri], w["rw_lnx_b"][ri], w["rw_rk"][ri].reshape(D_MODEL),
                      zero, zero, zero, zero, zero])
    nsub = RWKV_SUBCHUNKS if (t // chunk) % RWKV_SUBCHUNKS == 0 else 1
    nct = t // (chunk * nsub)
    crow = lambda bb, j: (bb * nct + j, 0)
    cspec = pl.BlockSpec((chunk * nsub, D_MODEL), crow)
    hspec = pl.BlockSpec((1, ng, gl_lanes, gl_lanes), lambda bb, j: (bb, 0, 0, 0))
    y, hout = pl.pallas_call(
        functools.partial(_rwkv_scan_body, nh=nh, chunk=chunk, nsub=nsub),
        grid=(b, nct),
        in_specs=[cspec] * 5 + [pl.BlockSpec((nsub, 1, D_MODEL), lambda bb, j: (bb * nct + j, 0, 0)), hspec,
                                _const_spec((8, D_MODEL))],
        out_specs=[cspec, hspec],
        out_shape=[jax.ShapeDtypeStruct((n, D_MODEL), adt),
                   jax.ShapeDtypeStruct((b, ng, gl_lanes, gl_lanes), F32)],
        compiler_params=_params(2),
        name="rwkv_scan",
    )(rt, kt, at, bt, v, gl, h0, svec)
    s_new = jnp.stack([hout[:, :, i * RW_N:(i + 1) * RW_N, i * RW_N:(i + 1) * RW_N] for i in range(nh)],
                      axis=2)
    s_new = jnp.swapaxes(s_new, -1, -2).reshape(b, RW_H, RW_N, RW_N)
    pre = ("gated", [y, g], [w["rw_wo"][ri].astype(BF16)])
    return pre, shift_new, s_new, (v if not has_vres else v_first)


def _rope_lanes(x, tab_ref):
    half = MLA_ROPE // 2
    return (x * tab_ref[0] + pltpu.roll(x, LANES - half, 1) * tab_ref[1] + pltpu.roll(x, half, 1) * tab_ref[2])


def _mla_proj_body(x_ref, nw_ref, tab_ref, winq_ref, winc_ref, wink_ref, qn_ref, kvn_ref, wqn_ref, wqr_ref,
                   wuk_ref, c_ref, kr_ref, kcat_ref, qcat_ref):
    h = _rms(x_ref[...], nw_ref[0:1, :]).astype(BF16)
    cq = _rms(jnp.dot(h, winq_ref[...], preferred_element_type=F32), qn_ref[...]).astype(BF16)
    c = _rms(jnp.dot(h, winc_ref[...], preferred_element_type=F32), kvn_ref[...])
    kr = _rope_lanes(jnp.dot(h, wink_ref[...], preferred_element_type=F32), tab_ref)
    c_ref[...] = c
    kr_ref[...] = kr
    adt = kcat_ref.dtype
    kcat_ref[:, 0:MLA_KV_LORA] = c.astype(adt)
    kcat_ref[:, MLA_KV_LORA:MLA_QK] = kr.astype(adt)
    qn = jnp.dot(cq, wqn_ref[...], preferred_element_type=F32).astype(BF16)
    qr = jnp.dot(cq, wqr_ref[...], preferred_element_type=F32)
    for pr in range(MLA_H // 2):
        ql = jnp.dot(qn[:, pr * LANES:(pr + 1) * LANES], wuk_ref[pr], preferred_element_type=F32) * MLA_SCALE
        qcat_ref[2 * pr, :, 0:MLA_KV_LORA] = ql[:, :MLA_KV_LORA].astype(adt)
        qcat_ref[2 * pr + 1, :, 0:MLA_KV_LORA] = ql[:, MLA_KV_LORA:].astype(adt)
    for hh in range(MLA_H):
        qro = _rope_lanes(qr[:, hh * LANES:(hh + 1) * LANES], tab_ref) * MLA_SCALE
        qcat_ref[hh, :, MLA_KV_LORA:MLA_QK] = qro.astype(adt)


MLA_TQ = 256
MLA_TK = 256
MLA_SPLIT = 16


def _mla_prompt_body(q_ref, k_ref, o_ref, m_ref, l_ref, acc_ref):
    i = pl.program_id(1)
    rows = MLA_H * MLA_TQ
    q = q_ref[...].reshape(rows, MLA_QK)
    m_ref[...] = jnp.full((rows, LANES), -jnp.inf, F32)
    l_ref[...] = jnp.zeros((rows, LANES), F32)
    acc_ref[...] = jnp.zeros((rows, MLA_KV_LORA), F32)
    ones = jnp.ones((MLA_TK, LANES), BF16)
    reps = MLA_TK // LANES

    sub = rows // MLA_SPLIT

    def block(k0, masked):
        kblk = k_ref[pl.ds(k0, MLA_TK), :]
        ss = [lax.dot_general(q[g * sub:(g + 1) * sub], kblk, (((1,), (1,)), ((), ())),
                              preferred_element_type=F32) for g in range(MLA_SPLIT)]
        for g in range(MLA_SPLIT):
            rs = slice(g * sub, (g + 1) * sub)
            s = ss[g]
            if masked:
                qpos = i * MLA_TQ + (g * sub + lax.broadcasted_iota(jnp.int32, (sub, MLA_TK), 0)) % MLA_TQ
                kpos = k0 + lax.broadcasted_iota(jnp.int32, (sub, MLA_TK), 1)
                s = jnp.where(kpos <= qpos, s, -jnp.inf)
            m_old = m_ref[rs, :]
            m_new = jnp.maximum(m_old, jnp.max(s, axis=1, keepdims=True))
            alpha = jnp.exp(m_old - m_new)
            pf = jnp.exp(s - jnp.concatenate([m_new] * reps, axis=1))
            l_ref[rs, :] = l_ref[rs, :] * alpha + jnp.sum(pf, axis=1, keepdims=True)
            acc_ref[rs, :] = (acc_ref[rs, :] * jnp.concatenate([alpha] * (MLA_KV_LORA // LANES), axis=1)
                              + jnp.dot(pf.astype(BF16), kblk[:, :MLA_KV_LORA], preferred_element_type=F32))
            m_ref[rs, :] = m_new

    def full_step(kb, carry):
        block(pl.multiple_of(kb * MLA_TK, MLA_TK), False)
        return carry

    n_full = (i * MLA_TQ) // MLA_TK
    lax.fori_loop(0, n_full, full_step, 0)
    block(pl.multiple_of(n_full * MLA_TK, MLA_TK), True)
    o = acc_ref[...] / jnp.concatenate([l_ref[...]] * (MLA_KV_LORA // LANES), axis=1)
    o_ref[...] = o.reshape(MLA_H, MLA_TQ, MLA_KV_LORA).astype(BF16)


MLA_PP = 16
MLA_GROUPS = 2


def _mla_sample_body(pt_ref, q_ref, kn_ref, *rest):
    lat_refs = rest[:MLA_PP]
    kro_refs = rest[MLA_PP:2 * MLA_PP]
    o_ref, m_ref, l_ref, acc_ref = rest[2 * MLA_PP:]
    j = pl.program_id(1)
    t = q_ref.shape[1]
    rows = MLA_H * t
    q = q_ref[...].reshape(rows, MLA_QK).astype(BF16)
    ql = q[:, :MLA_KV_LORA]
    qr = q[:, MLA_KV_LORA:MLA_KV_LORA + MLA_ROPE]

    @pl.when(j == 0)
    def _():
        m_ref[...] = jnp.full(m_ref.shape, -jnp.inf, F32)
        l_ref[...] = jnp.zeros(l_ref.shape, F32)
        acc_ref[...] = jnp.zeros(acc_ref.shape, F32)

    vrep = MLA_KV_LORA // LANES

    def update(g, s, vals, row_sum):
        m_old = m_ref[g]
        m_new = jnp.maximum(m_old, jnp.max(s, axis=1, keepdims=True))
        alpha = jnp.exp(m_old - m_new)
        if s.shape[1] % LANES == 0:
            p = jnp.exp(s - jnp.concatenate([m_new] * (s.shape[1] // LANES), axis=1)).astype(BF16)
        else:
            p = jnp.exp(s - m_new[:, 0:1]).astype(BF16)
        l_ref[g] = l_ref[g] * alpha + row_sum(p)
        acc_ref[g] = (acc_ref[g] * jnp.concatenate([alpha] * vrep, axis=1)
                      + jnp.dot(p, vals, preferred_element_type=F32))
        m_ref[g] = m_new

    per = MLA_PP // MLA_GROUPS
    ones = jnp.ones((per * lat_refs[0].shape[1], LANES), BF16)
    scores, values = [], []
    for g in range(MLA_GROUPS):
        cbs, s_parts = [], []
        for pp in range(g * per, (g + 1) * per):
            cb = lat_refs[pp][0].astype(BF16)
            kbt = kro_refs[pp][0].astype(BF16)
            s_parts.append(lax.dot_general(ql, cb, (((1,), (1,)), ((), ())), preferred_element_type=F32)
                           + jnp.dot(qr, kbt, preferred_element_type=F32))
            cbs.append(cb)
        scores.append(jnp.concatenate(s_parts, axis=1))
        values.append(jnp.concatenate(cbs, axis=0))
    for g in range(MLA_GROUPS):
        update(g, scores[g], values[g], lambda p: jnp.dot(p, ones, preferred_element_type=F32))

    @pl.when(j == pl.num_programs(1) - 1)
    def _():
        kn = kn_ref[...].astype(BF16)
        s = lax.dot_general(q, kn, (((1,), (1,)), ((), ())), preferred_element_type=F32)
        qpos = lax.broadcasted_iota(jnp.int32, (rows, t), 0) % t
        kpos = lax.broadcasted_iota(jnp.int32, (rows, t), 1)
        s = jnp.where(kpos <= qpos, s, -jnp.inf)
        update(0, s, kn[:, :MLA_KV_LORA], lambda p: jnp.sum(p.astype(F32), axis=1, keepdims=True))
        m_all = m_ref[0]
        for g in range(1, MLA_GROUPS):
            m_all = jnp.maximum(m_all, m_ref[g])
        l_all = jnp.zeros((rows, LANES), F32)
        acc = jnp.zeros((rows, MLA_KV_LORA), F32)
        for g in range(MLA_GROUPS):
            wgt = jnp.exp(m_ref[g] - m_all)
            l_all = l_all + l_ref[g] * wgt
            acc = acc + acc_ref[g] * jnp.concatenate([wgt] * vrep, axis=1)
        o = acc / jnp.concatenate([l_all] * vrep, axis=1)
        o_ref[...] = o.reshape(MLA_H, t, MLA_KV_LORA).astype(o_ref.dtype)


def _mla_out_body(x_ref, o_ref, wuv_ref, wo_ref, nw_ref, xo_ref):
    parts = []
    for pr in range(MLA_H // 2):
        wp = wuv_ref[pr]
        parts.append(jnp.dot(o_ref[2 * pr].astype(BF16), wp[:MLA_KV_LORA], preferred_element_type=F32)
                     + jnp.dot(o_ref[2 * pr + 1].astype(BF16), wp[MLA_KV_LORA:], preferred_element_type=F32))
    v = jnp.concatenate(parts, axis=1).astype(BF16)
    o = jnp.dot(v, wo_ref[...], preferred_element_type=F32)
    xo_ref[...] = x_ref[...] + _rms(o, nw_ref[0:1, :])


def _mla_layer(x2d, pos, w, mi, nw, b, t, paged):
    n = b * t
    tl = _tiling(b, t, 512)
    G, J, R = tl["G"], tl["J"], tl["R"]
    adt = BF16 if t % 16 == 0 else F32
    half = MLA_ROPE // 2
    inv = ROPE_THETA ** (-jnp.arange(half, dtype=F32) / half)
    ang = pos.astype(F32)[:, None] * inv[None, :]
    cos, sin = jnp.cos(ang), jnp.sin(ang)
    zpad = jnp.zeros((t, LANES - MLA_ROPE), F32)
    zh = jnp.zeros((t, half), F32)
    tab = jnp.stack([jnp.concatenate([cos, cos, zpad], 1), jnp.concatenate([-sin, zh, zpad], 1),
                     jnp.concatenate([zh, sin, zpad], 1)])
    if G == 1:
        tab = jnp.tile(tab, (1, b, 1))
    w_in = w["mla_w_in"][mi]
    winq = w_in[:, :MLA_Q_LORA].astype(BF16)
    winc = w_in[:, MLA_Q_LORA:MLA_Q_LORA + MLA_KV_LORA].astype(BF16)
    wink = jnp.pad(w_in[:, MLA_Q_LORA + MLA_KV_LORA:], ((0, 0), (0, LANES - MLA_ROPE))).astype(BF16)
    wqb = w["mla_w_qb"][mi].reshape(MLA_Q_LORA, MLA_H, MLA_NOPE + MLA_ROPE)
    wqn = wqb[:, :, :MLA_NOPE].reshape(MLA_Q_LORA, MLA_H * MLA_NOPE).astype(BF16)
    wqr = jnp.pad(wqb[:, :, MLA_NOPE:], ((0, 0), (0, 0), (0, LANES - MLA_ROPE))
                  ).reshape(MLA_Q_LORA, MLA_H * LANES).astype(BF16)
    wuk = jnp.transpose(w["mla_w_uk"][mi], (1, 2, 0)).reshape(MLA_H // 2, 2, MLA_NOPE, MLA_KV_LORA)
    wuk_bd = jnp.einsum("pinc,ij->pinjc", wuk, jnp.eye(2, dtype=F32)).reshape(
        MLA_H // 2, 2 * MLA_NOPE, 2 * MLA_KV_LORA).astype(BF16)
    wuv = jnp.transpose(w["mla_w_uv"][mi], (1, 0, 2)).reshape(MLA_H // 2, 2, MLA_KV_LORA, MLA_V)
    wuv_bd = jnp.einsum("picv,ij->picjv", wuv, jnp.eye(2, dtype=F32)).reshape(
        MLA_H // 2, 2 * MLA_KV_LORA, 2 * MLA_V).astype(BF16)
    nwa = jnp.concatenate([nw[0:1], jnp.zeros((7, D_MODEL), F32)])
    nwb = jnp.concatenate([nw[1:2], jnp.zeros((7, D_MODEL), F32)])
    row = lambda g, j: (g * J + j, 0)
    wl = [winq, winc, wink, w["mla_q_norm"][mi][None, :], w["mla_kv_norm"][mi][None, :], wqn, wqr, wuk_bd]
    c, kr, kcat, qcat = pl.pallas_call(
        _mla_proj_body,
        grid=(G, J),
        in_specs=[pl.BlockSpec((R, D_MODEL), row), _const_spec((8, D_MODEL)),
                  pl.BlockSpec((3, R, LANES), lambda g, j: (0, j, 0))] + [_const_spec(a.shape) for a in wl],
        out_specs=[pl.BlockSpec((R, MLA_KV_LORA), row), pl.BlockSpec((R, LANES), row),
                   pl.BlockSpec((R, MLA_QK), row), pl.BlockSpec((MLA_H, R, MLA_QK), lambda g, j: (0, g * J + j, 0))],
        out_shape=[jax.ShapeDtypeStruct((n, MLA_KV_LORA), F32), jax.ShapeDtypeStruct((n, LANES), F32),
                   jax.ShapeDtypeStruct((n, MLA_QK), adt), jax.ShapeDtypeStruct((MLA_H, n, MLA_QK), adt)],
        compiler_params=_params(2),
        name="mla_proj",
    )(x2d, nwa, tab, *wl)

    if paged is None:
        nq = t // MLA_TQ
        rows = MLA_H * MLA_TQ
        o = pl.pallas_call(
            _mla_prompt_body,
            grid=(b, nq),
            in_specs=[pl.BlockSpec((MLA_H, MLA_TQ, MLA_QK), lambda bb, i: (0, bb * nq + i, 0)),
                      pl.BlockSpec((t, MLA_QK), lambda bb, i: (bb, 0))],
            out_specs=pl.BlockSpec((MLA_H, MLA_TQ, MLA_KV_LORA), lambda bb, i: (0, bb * nq + i, 0)),
            out_shape=jax.ShapeDtypeStruct((MLA_H, n, MLA_KV_LORA), BF16),
            scratch_shapes=[pltpu.VMEM((rows, LANES), F32), pltpu.VMEM((rows, LANES), F32),
                            pltpu.VMEM((rows, MLA_KV_LORA), F32)],
            compiler_params=_params(2),
            name="mla_attend_prompt",
        )(qcat, kcat)
    else:
        pages_c, pages_kr, page_table = paged
        page = pages_c.shape[1]
        npg = page_table.shape[1]
        assert npg % MLA_PP == 0
        rows = MLA_H * t

        def page_map(pp):
            return lambda bb, j, pt: (pt[bb, j * MLA_PP + pp], 0, 0)

        grid_spec = pltpu.PrefetchScalarGridSpec(
            num_scalar_prefetch=1,
            grid=(b, npg // MLA_PP),
            in_specs=[pl.BlockSpec((MLA_H, t, MLA_QK), lambda bb, j, pt: (0, bb, 0)),
                      pl.BlockSpec((t, MLA_QK), lambda bb, j, pt: (bb, 0))]
            + [pl.BlockSpec((1, page, MLA_KV_LORA), page_map(pp)) for pp in range(MLA_PP)]
            + [pl.BlockSpec((1, MLA_ROPE, page), page_map(pp)) for pp in range(MLA_PP)],
            out_specs=pl.BlockSpec((MLA_H, t, MLA_KV_LORA), lambda bb, j, pt: (0, bb, 0)),
            scratch_shapes=[pltpu.VMEM((MLA_GROUPS, rows, LANES), F32), pltpu.VMEM((MLA_GROUPS, rows, LANES), F32),
                            pltpu.VMEM((MLA_GROUPS, rows, MLA_KV_LORA), F32)],
        )
        o = pl.pallas_call(
            _mla_sample_body,
            grid_spec=grid_spec,
            out_shape=jax.ShapeDtypeStruct((MLA_H, n, MLA_KV_LORA), adt),
            compiler_params=_params(2),
            name="mla_attend_sample",
        )(page_table, qcat, kcat, *([pages_c] * MLA_PP), *([jnp.swapaxes(pages_kr, 1, 2)] * MLA_PP))

    pre = ("mla", [o], [wuv_bd, w["mla_wo"][mi].astype(BF16)])
    return pre, c.reshape(b, t, MLA_KV_LORA), kr[:, :MLA_ROPE].reshape(b, t, MLA_ROPE)


GDN_CW = 512


def _gdn_proj_body(x_ref, prev_ref, nw_ref, wqkv_ref, wz_ref, wbg_ref, cw_ref, gvec_ref, tri_ref,
                   q_ref, k_ref, v_ref, z_ref, beta_ref, gc_ref, st_ref, carry_ref):
    @pl.when(pl.program_id(1) == 0)
    def _():
        carry_ref[...] = prev_ref[...]

    rows = x_ref.shape[0]
    p = carry_ref.shape[0]
    h = _rms(x_ref[...], nw_ref[0:1, :]).astype(BF16)
    adt = z_ref.dtype
    z_ref[...] = jnp.dot(h, wz_ref[...], preferred_element_type=F32).astype(adt)
    bg = jnp.dot(h, wbg_ref[...], preferred_element_type=F32)
    beta_ref[...] = _sigmoid(bg)
    g = -jnp.exp(gvec_ref[0:1, :]) * _softplus(bg + gvec_ref[1:2, :])
    gc_ref[...] = _chunk_cumsum(g, tri_ref[...])
    nch = GDN_CONV_DIM // GDN_CW

    def up(c):
        return jnp.dot(h, wqkv_ref[:, c * GDN_CW:(c + 1) * GDN_CW], preferred_element_type=F32)

    u_nxt = up(0)
    for c in range(nch):
        sl = slice(c * GDN_CW, (c + 1) * GDN_CW)
        u = u_nxt
        u_nxt = up(c + 1) if c + 1 < nch else None
        prev = carry_ref[:, sl]
        y = cw_ref[3:4, sl] * u
        for s in range(1, GDN_CONV):
            y = y + cw_ref[3 - s:4 - s, sl] * _shift_rows(u, prev, s)
        tail = u[rows - p:, :]
        carry_ref[:, sl] = tail
        st_ref[:, sl] = tail
        y = _silu(y)
        off = c * GDN_CW
        if off < 2 * GDN_QK_DIM:
            dst, base, scale = (q_ref, off, GDN_DK ** -0.5) if off < GDN_QK_DIM else (k_ref, off - GDN_QK_DIM, 1.0)
            for hh in range(GDN_CW // GDN_DK):
                yh = y[:, hh * GDN_DK:(hh + 1) * GDN_DK]
                yh = yh * lax.rsqrt(jnp.sum(yh * yh, axis=-1, keepdims=True) + 1e-6)
                dst[:, base + hh * GDN_DK:base + (hh + 1) * GDN_DK] = (yh * scale if scale != 1.0 else yh).astype(adt)
        else:
            v_ref[:, off - 2 * GDN_QK_DIM:off - 2 * GDN_QK_DIM + GDN_CW] = y.astype(adt)


def _gdn_chunk_body(q_ref, k_ref, v_ref, z_ref, gc_ref, beta_ref, s0_ref, nw_ref,
                    o_ref, so_ref, *, nh, chunk, nsub):
    @pl.when(pl.program_id(1) == 0)
    def _():
        so_ref[...] = s0_ref[...]

    ng = GDN_V_H // nh
    gc = nh * chunk
    rep = GDN_V_H // GDN_QK_H
    ri = lax.broadcasted_iota(jnp.int32, (gc, gc), 0)
    ci = lax.broadcasted_iota(jnp.int32, (gc, gc), 1)
    same = (ri // chunk) == (ci // chunk)
    strict = same & ((ri % chunk) > (ci % chunk))
    incl = same & ((ri % chunk) >= (ci % chunk))
    last = same & ((ci % chunk) == chunk - 1)
    eye = (ri == ci).astype(F32)
    row_head = lax.broadcasted_iota(jnp.int32, (gc, GDN_DK), 0) // chunk
    groups = range(ng)
    heads = [[q * nh + i for i in range(nh)] for q in groups]
    keys = [(sc, q) for sc in range(nsub) for q in groups]

    def stack(ref, sc, hds, width):
        parts = [ref[sc * chunk:(sc + 1) * chunk, hd * width:(hd + 1) * width] for hd in hds]
        return parts[0] if len(parts) == 1 else jnp.concatenate(parts, axis=0)

    def col(ref, sc, lanes):
        parts = [ref[sc * chunk:(sc + 1) * chunk, ln:ln + 1] for ln in lanes]
        return parts[0] if len(parts) == 1 else jnp.concatenate(parts, axis=0)

    k_st = {k: stack(k_ref, k[0], [hd // rep for hd in heads[k[1]]], GDN_DK) for k in keys}
    q_st = {k: stack(q_ref, k[0], [hd // rep for hd in heads[k[1]]], GDN_DK) for k in keys}
    v_st = {k: stack(v_ref, k[0], heads[k[1]], GDN_DV) for k in keys}
    gcol = {k: col(gc_ref, k[0], [GDN_V_H + hd for hd in heads[k[1]]]) for k in keys}
    bcol = {k: col(beta_ref, k[0], heads[k[1]]) for k in keys}
    grow = {k: jnp.sum(jnp.where(ri == ci, gcol[k], 0.0), axis=0, keepdims=True) for k in keys}
    k_b = {k: k_st[k].astype(BF16) for k in keys}
    kq = {k: _bdot_nt(jnp.concatenate([k_b[k], q_st[k].astype(BF16)], axis=0), k_b[k]) for k in keys}
    decay = {k: jnp.exp(jnp.where(incl, gcol[k] - grow[k], -jnp.inf)) for k in keys}
    a = {k: jnp.where(strict, kq[k][:gc] * bcol[k] * decay[k], 0.0) for k in keys}
    aqk = {k: jnp.where(incl, kq[k][gc:] * decay[k], 0.0).astype(BF16) for k in keys}

    p = {k: (-a[k]).astype(BF16) for k in keys}
    x = {k: eye - a[k] for k in keys}
    span = 2
    if span < chunk:
        p = {k: _bdot(p[k], p[k]) for k in keys}
    while span < chunk:
        if span * 2 < chunk:
            px = {k: _bdot(p[k], jnp.concatenate([p[k].astype(BF16), x[k].astype(BF16)], axis=1)) for k in keys}
            p = {k: px[k][:, :gc] for k in keys}
            x = {k: x[k] + px[k][:, gc:] for k in keys}
        else:
            x = {k: x[k] + _bdot(p[k], x[k]) for k in keys}
        span *= 2

    egc = {k: jnp.exp(gcol[k]) for k in keys}
    uw = {k: _bdot(x[k], jnp.concatenate([v_st[k] * bcol[k], k_st[k] * (bcol[k] * egc[k])], axis=1)) for k in keys}
    glast = {k: jnp.sum(jnp.where(last, grow[k], 0.0), axis=1, keepdims=True) for k in keys}
    kg = {k: k_st[k] * jnp.exp(glast[k] - gcol[k]) for k in keys}
    qg = {k: q_st[k] * egc[k] for k in keys}

    for sc in range(nsub):
        states = [[so_ref[0, hd] for hd in heads[q]] for q in groups]
        wq_s = []
        for q in groups:
            wm = uw[(sc, q)][:, GDN_DV:]
            parts = []
            for i in range(nh):
                rs = slice(i * chunk, (i + 1) * chunk)
                parts.append(_bdot(jnp.concatenate([wm[rs], qg[(sc, q)][rs]], axis=0), states[q][i]))
            wq_s.append(parts)
        v_new, o_st = [], []
        for q in groups:
            ws = jnp.concatenate([m[:chunk] for m in wq_s[q]], axis=0) if nh > 1 else wq_s[q][0][:chunk]
            qs = jnp.concatenate([m[chunk:] for m in wq_s[q]], axis=0) if nh > 1 else wq_s[q][0][chunk:]
            vn = (uw[(sc, q)][:, :GDN_DV] - ws).astype(BF16)
            v_new.append(vn)
            o_st.append(qs + _bdot(aqk[(sc, q)], vn))
        for q in groups:
            for i, hd in enumerate(heads[q]):
                rs = slice(i * chunk, (i + 1) * chunk)
                gl_h = jnp.exp(glast[(sc, q)][i * chunk:i * chunk + 1, :])
                if chunk % 16 == 0:
                    upd = _bdot_tn(kg[(sc, q)][rs], v_new[q][rs])
                else:
                    upd = _bdot_tn(jnp.where(row_head == i, kg[(sc, q)], 0.0), v_new[q])
                so_ref[0, hd] = states[q][i] * gl_h + upd
        for q in groups:
            z_st = stack(z_ref, sc, heads[q], GDN_DV).astype(F32)
            og = (_rms(o_st[q], nw_ref[0:1, :]) * _silu(z_st)).astype(o_ref.dtype)
            for i, hd in enumerate(heads[q]):
                o_ref[sc * chunk:(sc + 1) * chunk, hd * GDN_DV:(hd + 1) * GDN_DV] = og[i * chunk:(i + 1) * chunk]


def _gdn_chunk_body_old(q_ref, k_ref, v_ref, z_ref, gc_ref, beta_ref, s0_ref, nw_ref,
                        o_ref, so_ref, *, nh, chunk):
    @pl.when(pl.program_id(1) == 0)
    def _():
        so_ref[...] = s0_ref[...]

    ng = GDN_V_H // nh
    gc = nh * chunk
    rep = GDN_V_H // GDN_QK_H
    ri = lax.broadcasted_iota(jnp.int32, (gc, gc), 0)
    ci = lax.broadcasted_iota(jnp.int32, (gc, gc), 1)
    same = (ri // chunk) == (ci // chunk)
    strict = same & ((ri % chunk) > (ci % chunk))
    incl = same & ((ri % chunk) >= (ci % chunk))
    last = same & ((ci % chunk) == chunk - 1)
    eye = (ri == ci).astype(F32)
    row_head = lax.broadcasted_iota(jnp.int32, (gc, GDN_DK), 0) // chunk

    def stack(ref, heads, width):
        parts = [ref[:, hd * width:(hd + 1) * width] for hd in heads]
        return parts[0] if len(parts) == 1 else jnp.concatenate(parts, axis=0)

    groups = range(ng)
    heads = [[q * nh + i for i in range(nh)] for q in groups]
    k_st = [stack(k_ref, [hd // rep for hd in heads[q]], GDN_DK) for q in groups]
    q_st = [stack(q_ref, [hd // rep for hd in heads[q]], GDN_DK) for q in groups]
    v_st = [stack(v_ref, heads[q], GDN_DV) for q in groups]
    def col(ref, lanes):
        parts = [ref[:, ln:ln + 1] for ln in lanes]
        return parts[0] if len(parts) == 1 else jnp.concatenate(parts, axis=0)

    gcol = [col(gc_ref, [GDN_V_H + hd for hd in heads[q]]) for q in groups]
    bcol = [col(beta_ref, heads[q]) for q in groups]
    grow = [jnp.sum(jnp.where(ri == ci, gcol[q], 0.0), axis=0, keepdims=True) for q in groups]
    k_b = [x.astype(BF16) for x in k_st]
    kq = [_bdot_nt(jnp.concatenate([k_b[q], q_st[q].astype(BF16)], axis=0), k_b[q]) for q in groups]
    decay = [jnp.exp(jnp.where(incl, gcol[q] - grow[q], -jnp.inf)) for q in groups]
    a = [jnp.where(strict, kq[q][:gc] * bcol[q] * decay[q], 0.0) for q in groups]
    aqk = [jnp.where(incl, kq[q][gc:] * decay[q], 0.0).astype(BF16) for q in groups]

    p = [(-m).astype(BF16) for m in a]
    x = [eye - m for m in a]
    span = 2
    if span < chunk:
        p = [_bdot(p[q], p[q]) for q in groups]
    while span < chunk:
        if span * 2 < chunk:
            px = [_bdot(p[q], jnp.concatenate([p[q].astype(BF16), x[q].astype(BF16)], axis=1)) for q in groups]
            p = [m[:, :gc] for m in px]
            x = [x[q] + px[q][:, gc:] for q in groups]
        else:
            x = [x[q] + _bdot(p[q], x[q]) for q in groups]
        span *= 2

    egc = [jnp.exp(g) for g in gcol]
    uw = [_bdot(x[q], jnp.concatenate([v_st[q] * bcol[q], k_st[q] * (bcol[q] * egc[q])], axis=1)) for q in groups]
    glast = [jnp.sum(jnp.where(last, grow[q], 0.0), axis=1, keepdims=True) for q in groups]
    kg = [k_st[q] * jnp.exp(glast[q] - gcol[q]) for q in groups]
    states = [[so_ref[0, hd] for hd in heads[q]] for q in groups]
    wq_s = []
    for q in groups:
        wm = uw[q][:, GDN_DV:]
        qg = q_st[q] * egc[q]
        parts = []
        for i in range(nh):
            rs = slice(i * chunk, (i + 1) * chunk)
            parts.append(_bdot(jnp.concatenate([wm[rs], qg[rs]], axis=0), states[q][i]))
        wq_s.append(parts)
    v_new, o_st = [], []
    for q in groups:
        ws = jnp.concatenate([m[:chunk] for m in wq_s[q]], axis=0) if nh > 1 else wq_s[q][0][:chunk]
        qs = jnp.concatenate([m[chunk:] for m in wq_s[q]], axis=0) if nh > 1 else wq_s[q][0][chunk:]
        vn = (uw[q][:, :GDN_DV] - ws).astype(BF16)
        v_new.append(vn)
        o_st.append(qs + _bdot(aqk[q], vn))
    for q in groups:
        for i, hd in enumerate(heads[q]):
            rs = slice(i * chunk, (i + 1) * chunk)
            gl_h = jnp.exp(glast[q][i * chunk:i * chunk + 1, :])
            if chunk % 16 == 0:
                upd = _bdot_tn(kg[q][rs], v_new[q][rs])
            else:
                upd = _bdot_tn(jnp.where(row_head == i, kg[q], 0.0), v_new[q])
            so_ref[0, hd] = states[q][i] * gl_h + upd
    for q in groups:
        z_st = stack(z_ref, heads[q], GDN_DV)
        og = _rms(o_st[q], nw_ref[0:1, :]) * _silu(z_st)
        for i, hd in enumerate(heads[q]):
            o_ref[:, hd * GDN_DV:(hd + 1) * GDN_DV] = og[i * chunk:(i + 1) * chunk]


GDN_SUBCHUNKS = 4


def _gdn_layer(x2d, conv_prev, s0, w, gi, nw, b, t):
    n = b * t
    chunk = _chunk_of(t)
    tl = _tiling(b, t, 256)
    G, J, R, P = tl["G"], tl["J"], tl["R"], tl["P"]
    adt = BF16 if chunk % 16 == 0 else F32
    w_in = w["gdn_w_in"][gi]
    o1 = GDN_CONV_DIM
    o2 = o1 + GDN_V_DIM
    wqkv = w_in[:, :o1].astype(BF16)
    wz = w_in[:, o1:o2].astype(BF16)
    wbg = jnp.pad(w_in[:, o2:], ((0, 0), (0, LANES - 2 * GDN_V_H))).astype(BF16)
    cw = jnp.pad(w["gdn_conv_w"][gi], ((0, 8 - GDN_CONV), (0, 0)))
    gvec = jnp.zeros((8, LANES), F32)
    gvec = gvec.at[0, GDN_V_H:2 * GDN_V_H].set(w["gdn_a_log"][gi]).at[1, GDN_V_H:2 * GDN_V_H].set(w["gdn_dt_bias"][gi])
    bc = chunk if chunk == 64 else R
    tri = _chunk_masks(chunk, bc)
    nwa = jnp.concatenate([nw[0:1], jnp.zeros((7, D_MODEL), F32)])
    row = lambda g, j: (g * J + j, 0)
    st_spec = pl.BlockSpec((P, GDN_CONV_DIM), lambda g, j: (g, 0))
    qn, kn, v, z, beta, gcs, st = pl.pallas_call(
        _gdn_proj_body,
        grid=(G, J),
        in_specs=[pl.BlockSpec((R, D_MODEL), row), st_spec, _const_spec((8, D_MODEL)), _const_spec(wqkv.shape),
                  _const_spec(wz.shape), _const_spec(wbg.shape), _const_spec(cw.shape), _const_spec(gvec.shape),
                  _const_spec(tri.shape)],
        out_specs=[pl.BlockSpec((R, GDN_QK_DIM), row), pl.BlockSpec((R, GDN_QK_DIM), row),
                   pl.BlockSpec((R, GDN_V_DIM), row), pl.BlockSpec((R, GDN_V_DIM), row),
                   pl.BlockSpec((R, LANES), row), pl.BlockSpec((R, LANES), row), st_spec],
        out_shape=[jax.ShapeDtypeStruct((n, GDN_QK_DIM), adt), jax.ShapeDtypeStruct((n, GDN_QK_DIM), adt),
                   jax.ShapeDtypeStruct((n, GDN_V_DIM), adt), jax.ShapeDtypeStruct((n, GDN_V_DIM), adt),
                   jax.ShapeDtypeStruct((n, LANES), F32), jax.ShapeDtypeStruct((n, LANES), F32),
                   jax.ShapeDtypeStruct((b * SUBLANES, GDN_CONV_DIM), F32)],
        scratch_shapes=[pltpu.VMEM((P, GDN_CONV_DIM), F32)],
        compiler_params=_params(2),
        name="gdn_proj",
    )(x2d, _pad_state(conv_prev), nwa, wqkv, wz, wbg, cw, gvec, tri)
    conv_new = st.reshape(b, SUBLANES, GDN_CONV_DIM)[:, SUBLANES - (GDN_CONV - 1):]

    nh = GROUP_ROWS // chunk
    ng = GDN_V_H // nh
    nsub = GDN_SUBCHUNKS if (t // chunk) % GDN_SUBCHUNKS == 0 else 1
    nct = t // (chunk * nsub)
    br = chunk * nsub
    crow = lambda bb, j: (bb * nct + j, 0)
    sspec = pl.BlockSpec((1, GDN_V_H, GDN_DK, GDN_DV), lambda bb, j: (bb, 0, 0, 0))
    nwn = jnp.concatenate([w["gdn_norm_w"][gi][None, :], jnp.zeros((7, GDN_DV), F32)])
    o, s_new = pl.pallas_call(
        functools.partial(_gdn_chunk_body, nh=nh, chunk=chunk, nsub=nsub),
        grid=(b, nct),
        in_specs=[pl.BlockSpec((br, GDN_QK_DIM), crow), pl.BlockSpec((br, GDN_QK_DIM), crow),
                  pl.BlockSpec((br, GDN_V_DIM), crow), pl.BlockSpec((br, GDN_V_DIM), crow),
                  pl.BlockSpec((br, LANES), crow), pl.BlockSpec((br, LANES), crow),
                  sspec, _const_spec((8, GDN_DV))],
        out_specs=[pl.BlockSpec((br, GDN_V_DIM), crow), sspec],
        out_shape=[jax.ShapeDtypeStruct((n, GDN_V_DIM), adt),
                   jax.ShapeDtypeStruct((b, GDN_V_H, GDN_DK, GDN_DV), F32)],
        compiler_params=_params(2),
        name="gdn_chunk",
    )(qn, kn, v, z, gcs, beta, s0, nwn)
    pre = ("plain", [o], [w["gdn_wo"][gi].astype(BF16)])
    return pre, conv_new, s_new


def _trunk(x, pos, rw_s, rw_shift, gdn_s, gdn_conv, ffn_conv, w, paged):
    b, t, _ = x.shape
    x2d = x.reshape(b * t, D_MODEL)
    new = {k: [] for k in ("rw_S", "rw_shift", "mla_c", "mla_kr", "gdn_S", "gdn_conv", "ffn_conv")}
    v_first = None
    ri = mi = gi = 0
    for l, kind in enumerate(LAYER_MIXER):
        nw = w["norm_w"][l]
        if kind == 0:
            pre, sh, s_new, v_first = _rwkv_layer(x2d, rw_shift[ri], rw_s[ri], v_first, w, ri, nw, b, t)
            new["rw_S"].append(s_new)
            new["rw_shift"].append(sh)
            ri += 1
        elif kind == 1:
            pre, c, kr = _mla_layer(x2d, pos, w, mi, nw, b, t, None if paged is None else
                                    (paged[0][mi], paged[1][mi], paged[2]))
            new["mla_c"].append(c)
            new["mla_kr"].append(kr)
            mi += 1
        else:
            pre, cb, s_new = _gdn_layer(x2d, gdn_conv[gi], gdn_s[gi], w, gi, nw, b, t)
            new["gdn_S"].append(s_new)
            new["gdn_conv"].append(cb)
            gi += 1
        nwf = jnp.concatenate([nw[2:4], nw[1:2], jnp.zeros((5, D_MODEL), F32)])
        cwb = jnp.concatenate([w["ffn_conv_w"][l], w["ffn_conv_b"][l][None, :],
                               jnp.zeros((8 - FFN_CONV - 1, 2 * D_FF), F32)])
        x2d, st = _ffn(x2d, pre, _pad_state(ffn_conv[l]), nwf, w["ffn_w_up"][l].astype(BF16), cwb,
                       w["ffn_w_down"][l].astype(BF16), b, t)
        new["ffn_conv"].append(st.reshape(b, SUBLANES, 2 * D_FF)[:, SUBLANES - (FFN_CONV - 1):])
    return x2d.reshape(b, t, D_MODEL), {k: jnp.stack(v) for k, v in new.items()}


def kernel(x_prompt, x_sample, state_rwkv_wkv, state_rwkv_shift, cache_mla_latent, cache_mla_krope, state_gdn_S, state_gdn_conv, state_ffn_conv, page_table, norm_w, rw_mu, rw_wrkv, rw_w0, rw_w1, rw_w2, rw_a0, rw_a1, rw_a2, rw_v0, rw_v1, rw_v2, rw_g1, rw_g2, rw_kk, rw_ka, rw_rk, rw_lnx_w, rw_lnx_b, rw_wo, mla_w_in, mla_q_norm, mla_kv_norm, mla_w_qb, mla_w_uk, mla_w_uv, mla_wo, gdn_w_in, gdn_conv_w, gdn_a_log, gdn_dt_bias, gdn_norm_w, gdn_wo, ffn_w_up, ffn_conv_w, ffn_conv_b, ffn_w_down):
    w = dict(norm_w=norm_w, rw_mu=rw_mu, rw_wrkv=rw_wrkv, rw_w0=rw_w0, rw_w1=rw_w1, rw_w2=rw_w2, rw_a0=rw_a0,
             rw_a1=rw_a1, rw_a2=rw_a2, rw_v0=rw_v0, rw_v1=rw_v1, rw_v2=rw_v2, rw_g1=rw_g1, rw_g2=rw_g2,
             rw_kk=rw_kk, rw_ka=rw_ka, rw_rk=rw_rk, rw_lnx_w=rw_lnx_w, rw_lnx_b=rw_lnx_b, rw_wo=rw_wo,
             mla_w_in=mla_w_in, mla_q_norm=mla_q_norm, mla_kv_norm=mla_kv_norm, mla_w_qb=mla_w_qb,
             mla_w_uk=mla_w_uk, mla_w_uv=mla_w_uv, mla_wo=mla_wo, gdn_w_in=gdn_w_in, gdn_conv_w=gdn_conv_w,
             gdn_a_log=gdn_a_log, gdn_dt_bias=gdn_dt_bias, gdn_norm_w=gdn_norm_w, gdn_wo=gdn_wo,
             ffn_w_up=ffn_w_up, ffn_conv_w=ffn_conv_w, ffn_conv_b=ffn_conv_b, ffn_w_down=ffn_w_down)
    b, t = x_prompt.shape[0], x_prompt.shape[1]
    n_rw, n_gdn, depth = state_rwkv_wkv.shape[0], state_gdn_S.shape[0], state_ffn_conv.shape[0]
    y_p, sp = _trunk(
        x_prompt, jnp.arange(t),
        jnp.zeros((n_rw, b) + state_rwkv_wkv.shape[2:], F32), jnp.zeros((n_rw, b, D_MODEL), F32),
        jnp.zeros((n_gdn, b) + state_gdn_S.shape[2:], F32), jnp.zeros((n_gdn, b) + state_gdn_conv.shape[2:], F32),
        jnp.zeros((depth, b) + state_ffn_conv.shape[2:], F32), w, None)
    past_len = page_table.shape[1] * cache_mla_latent.shape[2]
    pos_s = past_len + jnp.arange(x_sample.shape[1])
    y_s, ss = _trunk(x_sample, pos_s, state_rwkv_wkv, state_rwkv_shift, state_gdn_S, state_gdn_conv,
                     state_ffn_conv, w, (cache_mla_latent, cache_mla_krope, page_table))
    names = ("rw_S", "rw_shift", "mla_c", "mla_kr", "gdn_S", "gdn_conv", "ffn_conv")
    return (y_p, y_s) + tuple(sp[k] for k in names) + tuple(ss[k] for k in names)
```

```python
import functools

import jax
import jax.numpy as jnp
from jax import lax
from jax.experimental import pallas as pl
from jax.experimental.pallas import tpu as pltpu

F32 = jnp.float32
BF16 = jnp.bfloat16
HIGHEST = lax.Precision.HIGHEST

D_MODEL = 1024
NORM_EPS = 1e-6
RW_N = 64
RW_H = D_MODEL // RW_N
RW_LNX_EPS = 64e-5
MLA_H = 16
MLA_NOPE = 64
MLA_ROPE = 32
MLA_V = 64
MLA_Q_LORA = 512
MLA_KV_LORA = 256
MLA_SCALE = (MLA_NOPE + MLA_ROPE) ** -0.5
ROPE_THETA = 10000.0
MLA_QK = MLA_KV_LORA + 128
GDN_QK_H = 8
GDN_V_H = 16
GDN_DK = 128
GDN_DV = 128
GDN_QK_DIM = GDN_QK_H * GDN_DK
GDN_V_DIM = GDN_V_H * GDN_DV
GDN_CONV_DIM = 2 * GDN_QK_DIM + GDN_V_DIM
GDN_CONV = 4
D_FF = 2816
FFN_CONV = 3
LAYER_MIXER = (0, 1, 2, 0)

SUBLANES = 8
LANES = 128
GROUP_ROWS = 128
VMEM_LIMIT = 56 * 1024 * 1024


def _rms(x, w):
    return x * lax.rsqrt(jnp.mean(x * x, axis=-1, keepdims=True) + NORM_EPS) * w


def _bdot(a, b):
    return jnp.dot(a.astype(BF16), b.astype(BF16), preferred_element_type=F32)


def _bdot_nt(a, b):
    return lax.dot_general(a.astype(BF16), b.astype(BF16), (((1,), (1,)), ((), ())),
                           preferred_element_type=F32)


def _bdot_tn(a, b):
    return lax.dot_general(a.astype(BF16), b.astype(BF16), (((0,), (0,)), ((), ())),
                           preferred_element_type=F32)


def _hdot(a, b):
    return jnp.dot(a, b, precision=HIGHEST, preferred_element_type=F32)


def _sigmoid(x):
    return 1.0 / (1.0 + jnp.exp(-x))


def _softplus(x):
    return jnp.maximum(x, 0.0) + jnp.log(1.0 + jnp.exp(-jnp.abs(x)))


def _silu(x):
    return x * _sigmoid(x)


def _shift_rows(u, prev, s):
    rows, cols = u.shape
    p = prev.shape[0]
    rolled = pltpu.roll(u, s, 0)
    fix = pltpu.roll(prev, (p - SUBLANES + s) % p, 0)
    t = lax.broadcasted_iota(jnp.int32, (p, cols), 0) % SUBLANES
    if p == rows:
        return jnp.where(t < s, fix, rolled)
    head = jnp.where(t < s, fix, rolled[:SUBLANES])
    return jnp.concatenate([head, rolled[SUBLANES:]], axis=0)


def _lane_group_sum(x, ones2):
    parts = []
    for i in range(x.shape[1] // LANES):
        xs = x[:, i * LANES:(i + 1) * LANES]
        hi = xs.astype(BF16)
        lo = (xs - hi.astype(F32)).astype(BF16)
        parts.append(jnp.dot(jnp.concatenate([hi, lo], axis=1), ones2, preferred_element_type=F32))
    return parts[0] if len(parts) == 1 else jnp.concatenate(parts, axis=1)


def _split_dot(m2, x):
    hi = x.astype(BF16)
    lo = (x - hi.astype(F32)).astype(BF16)
    return jnp.dot(m2, jnp.concatenate([hi, lo], axis=0), preferred_element_type=F32)


def _chunk_cumsum(x, tri):
    bc = tri.shape[0]
    parts = [_hdot(tri, x[i * bc:(i + 1) * bc]) for i in range(x.shape[0] // bc)]
    return parts[0] if len(parts) == 1 else jnp.concatenate(parts, axis=0)


def _tiling(b, t, tt_max):
    if t == SUBLANES:
        return dict(G=1, J=1, R=b * t, P=b * t)
    tt = min(t, tt_max)
    assert t % tt == 0 and tt % 64 == 0, (t, tt)
    return dict(G=b, J=t // tt, R=tt, P=SUBLANES)


def _chunk_of(t):
    return 64 if t % 64 == 0 else t


def _const_spec(shape):
    nd = len(shape)
    return pl.BlockSpec(shape, lambda *_: (0,) * nd, pipeline_mode=pl.Buffered(1))


def _params(n_axes):
    return pltpu.CompilerParams(dimension_semantics=("arbitrary",) * n_axes,
                                vmem_limit_bytes=VMEM_LIMIT)


def _pad_state(st):
    b, k1, c = st.shape
    return jnp.pad(st, ((0, 0), (SUBLANES - k1, 0), (0, 0))).reshape(b * SUBLANES, c)


def _chunk_masks(chunk, rows):
    i = jnp.arange(rows)
    same = (i[:, None] // chunk) == (i[None, :] // chunk)
    tri = same & ((i[None, :] % chunk) <= (i[:, None] % chunk))
    return tri.astype(F32)


FFN_CW = 256


ROW_BLOCK = 64


def _stage(buf_ref, u, carry_ref, st_ref, sl, taps):
    rows = u.shape[0]
    hb = taps * SUBLANES
    buf_ref[hb:hb + rows, :] = u
    first = lax.broadcasted_iota(jnp.int32, (SUBLANES, u.shape[1]), 0) == 0
    for i in range(taps):
        back = taps - i
        src = u[rows - back * SUBLANES:rows - (back - 1) * SUBLANES, :]
        crow = carry_ref[SUBLANES - back:SUBLANES - back + 1, sl]
        buf_ref[i * SUBLANES:(i + 1) * SUBLANES, :] = jnp.where(first, crow, pltpu.roll(src, 1, 0))
    sq = SUBLANES * SUBLANES
    tail = pltpu.einshape("(vs)d->(sv)d", u[rows - sq:, :], s=SUBLANES)[sq - SUBLANES:, :]
    carry_ref[:, sl] = tail
    st_ref[:, sl] = tail


def _taps(buf_ref, r0, nrows, taps):
    hb = taps * SUBLANES
    cur = buf_ref[hb + r0:hb + r0 + nrows, :]
    return cur, [buf_ref[hb - j * SUBLANES + r0:hb - j * SUBLANES + r0 + nrows, :] for j in range(taps, 0, -1)]


FFN_PARTS = 2


def _mixer_out(kind, refs, rs):
    if kind == "mla":
        o_ref, wuv_ref, wo_ref = refs
        parts = []
        for pr in range(MLA_H // 2):
            wp = wuv_ref[pr]
            parts.append(jnp.dot(o_ref[2 * pr, rs, :].astype(BF16), wp[:MLA_KV_LORA], preferred_element_type=F32)
                         + jnp.dot(o_ref[2 * pr + 1, rs, :].astype(BF16), wp[MLA_KV_LORA:],
                                   preferred_element_type=F32))
        y = jnp.concatenate(parts, axis=1)
    elif kind == "gated":
        y_ref, g_ref, wo_ref = refs
        y = y_ref[rs, :].astype(F32) * g_ref[rs, :].astype(F32)
    else:
        y_ref, wo_ref = refs
        y = y_ref[rs, :]
    return jnp.dot(y.astype(BF16), wo_ref[...], preferred_element_type=F32)


def _ffn_body(*refs, kind, n_pre):
    x_ref = refs[0]
    pre_refs = refs[1:1 + n_pre]
    (prev_ref, nw_ref, wup_ref, cwb_ref, wdn_ref, xo_ref, st_ref, carry_ref, act_ref, buf_ref) = refs[1 + n_pre:]

    @pl.when(pl.program_id(1) == 0)
    def _():
        carry_ref[...] = prev_ref[...]

    rows = x_ref.shape[0]
    stacked = carry_ref.shape[0] == rows
    nparts = 1 if stacked else FFN_PARTS
    prows = rows // nparts
    parts = [slice(i * prows, (i + 1) * prows) for i in range(nparts)]
    nch = D_FF // FFN_CW

    def cols(c, half):
        return slice(half * D_FF + c * FFN_CW, half * D_FF + (c + 1) * FFN_CW)

    def conv_gate(c, taps):
        ys = []
        for half in range(2):
            sl = cols(c, half)
            u, (u2, u1) = taps[half]
            ys.append(cwb_ref[0:1, sl] * u2 + cwb_ref[1:2, sl] * u1 + cwb_ref[2:3, sl] * u + cwb_ref[3:4, sl])
        return (_silu(ys[0]) * ys[1]).astype(BF16)

    xs = [x_ref[rs, :] + _rms(_mixer_out(kind, pre_refs, rs), nw_ref[2:3, :]) for rs in parts]
    for pi, rs in enumerate(parts):
        x = xs[pi] if stacked else pltpu.einshape("(sv)d->(vs)d", xs[pi], s=SUBLANES)
        h = _rms(x, nw_ref[0:1, :]).astype(BF16)

        def up(c):
            us = [jnp.dot(h, wup_ref[:, cols(c, half)], preferred_element_type=F32) for half in range(2)]
            if stacked:
                return us
            for half in range(2):
                _stage(buf_ref.at[(c % 2) * 2 + half], us[half], carry_ref, st_ref, cols(c, half), FFN_CONV - 1)
            return None

        u_cur = up(0)
        for c in range(nch):
            u_nxt = up(c + 1) if c + 1 < nch else None
            csl = slice(c * FFN_CW, (c + 1) * FFN_CW)
            if stacked:
                taps = []
                for half in range(2):
                    sl = cols(c, half)
                    u = u_cur[half]
                    prev = carry_ref[:, sl]
                    taps.append((u, [_shift_rows(u, prev, 2), _shift_rows(u, prev, 1)]))
                    carry_ref[:, sl] = u
                    st_ref[:, sl] = u
                act_ref[:, csl] = conv_gate(c, taps)
            else:
                for r0 in range(0, prows, ROW_BLOCK):
                    taps = [_taps(buf_ref.at[(c % 2) * 2 + half], r0, ROW_BLOCK, FFN_CONV - 1) for half in range(2)]
                    act_ref[rs.start + r0:rs.start + r0 + ROW_BLOCK, csl] = conv_gate(c, taps)
            u_cur = u_nxt
        f = jnp.dot(act_ref[rs, :], wdn_ref[...], preferred_element_type=F32)
        out = x + _rms(f, nw_ref[1:2, :])
        xo_ref[rs, :] = out if stacked else pltpu.einshape("(vs)d->(sv)d", out, s=SUBLANES)


def _ffn(x2d, pre, prev, nw, wup, cwb, wdn, b, t):
    tl = _tiling(b, t, 512)
    G, J, R, P = tl["G"], tl["J"], tl["R"], tl["P"]
    n = b * t
    kind, acts, wts = pre
    row = lambda g, j: (g * J + j, 0)
    if kind == "mla":
        act_specs = [pl.BlockSpec((MLA_H, R, MLA_KV_LORA), lambda g, j: (0, g * J + j, 0))]
    else:
        act_specs = [pl.BlockSpec((R, a.shape[1]), row) for a in acts]
    pre_specs = act_specs + [_const_spec(wt.shape) for wt in wts]
    return pl.pallas_call(
        functools.partial(_ffn_body, kind=kind, n_pre=len(pre_specs)),
        grid=(G, J),
        in_specs=[pl.BlockSpec((R, D_MODEL), row)] + pre_specs + [
            pl.BlockSpec((P, 2 * D_FF), lambda g, j: (g, 0)),
            _const_spec((8, D_MODEL)),
            _const_spec((D_MODEL, 2 * D_FF)),
            _const_spec((8, 2 * D_FF)),
            _const_spec((D_FF, D_MODEL)),
        ],
        out_specs=[
            pl.BlockSpec((R, D_MODEL), lambda g, j: (g * J + j, 0)),
            pl.BlockSpec((P, 2 * D_FF), lambda g, j: (g, 0)),
        ],
        out_shape=[jax.ShapeDtypeStruct((n, D_MODEL), F32),
                   jax.ShapeDtypeStruct((b * SUBLANES, 2 * D_FF), F32)],
        scratch_shapes=[pltpu.VMEM((P, 2 * D_FF), F32), pltpu.VMEM((R, D_FF), BF16),
                        pltpu.VMEM((4, R + (FFN_CONV - 1) * SUBLANES, FFN_CW), F32)],
        compiler_params=_params(2),
        name="conv_ffn",
    )(x2d, *acts, *wts, prev, nw, wup, cwb, wdn)


def _outproj_body(*refs, gated):
    x_ref, y_ref = refs[0], refs[1]
    wo_ref, nw_ref, xo_ref = refs[-3:]
    y = y_ref[...]
    if gated:
        y = y.astype(F32) * refs[2][...].astype(F32)
    o = jnp.dot(y.astype(BF16), wo_ref[...], preferred_element_type=F32)
    xo_ref[...] = x_ref[...] + _rms(o, nw_ref[0:1, :])


def _outproj(x2d, y2d, gate2d, wo, nw, name):
    n, k = y2d.shape
    R = min(n, 512)
    row = lambda i: (i, 0)
    acts = [y2d] if gate2d is None else [y2d, gate2d]
    return pl.pallas_call(
        functools.partial(_outproj_body, gated=gate2d is not None),
        grid=(n // R,),
        in_specs=[pl.BlockSpec((R, D_MODEL), row)] + [pl.BlockSpec((R, k), row)] * len(acts)
        + [_const_spec((k, D_MODEL)), _const_spec((8, D_MODEL))],
        out_specs=pl.BlockSpec((R, D_MODEL), row),
        out_shape=jax.ShapeDtypeStruct((n, D_MODEL), F32),
        compiler_params=_params(1),
        name=name,
    )(x2d, *acts, wo, nw)


def _rwkv_proj_body(*refs, has_vres, chunk):
    it = iter(refs)
    x_ref, prev_ref = next(it), next(it)
    vf_ref = next(it) if has_vres else None
    vec_ref, wrkv_ref, w1_ref, w2_ref, a1_ref, a2_ref = (next(it) for _ in range(6))
    v1_ref, v2_ref = (next(it), next(it)) if has_vres else (None, None)
    g1_ref, g2_ref, tri_ref, ones_ref = (next(it) for _ in range(4))
    rt_ref, kt_ref, at_ref, bt_ref, v_ref, g_ref, gl_ref, hl_ref, carry_ref = (next(it) for _ in range(9))

    @pl.when(pl.program_id(1) == 0)
    def _():
        carry_ref[...] = prev_ref[...]

    x = x_ref[...]
    rows = x.shape[0]
    p = carry_ref.shape[0]
    h = _rms(x, vec_ref[10:11, :])
    d = _shift_rows(h, carry_ref[...], 1) - h
    tail = h[rows - p:, :]
    carry_ref[...] = tail
    hl_ref[...] = tail

    def mix(i):
        return (h + d * vec_ref[i:i + 1, :]).astype(BF16)

    r = jnp.dot(mix(0), wrkv_ref[0], preferred_element_type=F32)
    k = jnp.dot(mix(1), wrkv_ref[1], preferred_element_type=F32)
    xv = mix(2)
    v = jnp.dot(xv, wrkv_ref[2], preferred_element_type=F32)
    w_lora = _bdot(jnp.tanh(_bdot(mix(3), w1_ref[...])), w2_ref[...])
    v_lora = _bdot(_bdot(xv, v1_ref[...]), v2_ref[...]) if has_vres else None
    a_lora = _bdot(_bdot(mix(4), a1_ref[...]), a2_ref[...])
    g_ref[...] = _bdot(_sigmoid(_bdot(mix(5), g1_ref[...])), g2_ref[...]).astype(g_ref.dtype)
    adt = rt_ref.dtype

    bc = tri_ref.shape[0]
    for r0 in range(0, rows, bc):
        for l0 in range(0, D_MODEL, PROJ_LANES):
            rs, ls = slice(r0, r0 + bc), slice(l0, l0 + PROJ_LANES)
            vb = v[rs, ls]
            if has_vres:
                vb = vb + (vf_ref[rs, ls] - vb) * _sigmoid(vec_ref[11:12, ls] + v_lora[rs, ls])
            v_ref[rs, ls] = vb.astype(adt)
            a = _sigmoid(vec_ref[7:8, ls] + a_lora[rs, ls])
            kb = k[rs, ls]
            kk = kb * vec_ref[8:9, ls]
            kk = kk * lax.rsqrt(_lane_group_sum(kk * kk, ones_ref[...]) + 1e-6)
            kb = kb * (1.0 + (a - 1.0) * vec_ref[9:10, ls])
            w = -_softplus(-(vec_ref[6:7, ls] + w_lora[rs, ls])) - 0.5
            lw = -jnp.exp(w)
            cum = _split_dot(tri_ref[...], lw)
            e_bwd = jnp.exp(-cum)
            rt_ref[rs, ls] = (r[rs, ls] * jnp.exp(cum)).astype(adt)
            kt_ref[rs, ls] = (kb * e_bwd).astype(adt)
            at_ref[rs, ls] = (-kk * jnp.exp(cum - lw)).astype(adt)
            bt_ref[rs, ls] = (kk * a * e_bwd).astype(adt)
            for c in range(bc // chunk):
                row = (c + 1) * chunk - 1
                gl_ref[r0 // chunk + c, :, ls] = jnp.exp(cum[row:row + 1, :])


def _rwkv_scan_body(rt_ref, kt_ref, at_ref, bt_ref, v_ref, gl_ref, h0_ref, vec_ref, y_ref, ho_ref,
                    *, nh, chunk, nsub, per_seq):
    @pl.when(pl.program_id(1) == 0)
    def _():
        ho_ref[...] = h0_ref[...]

    gl_lanes = nh * RW_N
    ng = RW_H // nh
    gc = nh * chunk
    row_head = lax.broadcasted_iota(jnp.int32, (gc, gl_lanes), 0) // chunk
    lane_head = lax.broadcasted_iota(jnp.int32, (gc, gl_lanes), 1) // RW_N
    own = row_head == lane_head
    ri = lax.broadcasted_iota(jnp.int32, (gc, gc), 0)
    ci = lax.broadcasted_iota(jnp.int32, (gc, gc), 1)
    same = (ri // chunk) == (ci // chunk)
    strict = same & ((ri % chunk) > (ci % chunk))
    incl = same & ((ri % chunk) >= (ci % chunk))
    eye = (ri == ci).astype(F32)
    eye_l = (lax.broadcasted_iota(jnp.int32, (gl_lanes, gl_lanes), 0)
             == lax.broadcasted_iota(jnp.int32, (gl_lanes, gl_lanes), 1))
    merged = gc == GROUP_ROWS
    groups = range(ng)
    sls = [slice(q * gl_lanes, (q + 1) * gl_lanes) for q in groups]
    keys = [(sc, q) for sc in range(nsub) for q in groups]

    def blockdiag(ref, key):
        xg = ref[key[0] * chunk:(key[0] + 1) * chunk, sls[key[1]]]
        xx = jnp.concatenate([xg] * nh, axis=0) if nh > 1 else xg
        return jnp.where(own, xx, jnp.zeros_like(xx))

    r_bd = {k: blockdiag(rt_ref, k) for k in keys}
    k_bd = {k: blockdiag(kt_ref, k) for k in keys}
    a_bd = {k: blockdiag(at_ref, k) for k in keys}
    b_bd = {k: blockdiag(bt_ref, k) for k in keys}
    v_f = {k: blockdiag(v_ref, k) for k in keys}
    v_bd = {k: v_f[k].astype(BF16) for k in keys}
    bonus = {k: jnp.sum(r_bd[k].astype(F32) * k_bd[k] * vec_ref[2:3, sls[k[1]]], axis=1, keepdims=True)
             for k in keys}
    if merged:
        ar = {k: jnp.concatenate([a_bd[k], r_bd[k]], axis=0).astype(BF16) for k in keys}
        bk = {k: jnp.concatenate([b_bd[k], k_bd[k]], axis=0).astype(BF16) for k in keys}
        amat = {k: _bdot_nt(ar[k], bk[k]) for k in keys}
        a_ab = {k: jnp.where(strict, amat[k][:gc, :gc], 0.0) for k in keys}
        a_ak = {k: jnp.where(strict, amat[k][:gc, gc:], 0.0).astype(BF16) for k in keys}
        a_rbk = {k: jnp.concatenate([jnp.where(incl, amat[k][gc:, :gc], 0.0),
                                     jnp.where(incl, amat[k][gc:, gc:], 0.0)], axis=1).astype(BF16) for k in keys}
    else:
        ab_ = {k: a_bd[k].astype(BF16) for k in keys}
        rb_ = {k: r_bd[k].astype(BF16) for k in keys}
        bb_ = {k: b_bd[k].astype(BF16) for k in keys}
        kb_ = {k: k_bd[k].astype(BF16) for k in keys}
        a_ab = {k: jnp.where(strict, _bdot_nt(ab_[k], bb_[k]), 0.0) for k in keys}
        a_ak = {k: jnp.where(strict, _bdot_nt(ab_[k], kb_[k]), 0.0).astype(BF16) for k in keys}
        a_rb = {k: jnp.where(incl, _bdot_nt(rb_[k], bb_[k]), 0.0).astype(BF16) for k in keys}
        a_rk = {k: jnp.where(incl, _bdot_nt(rb_[k], kb_[k]), 0.0).astype(BF16) for k in keys}
    akv = {k: _bdot(a_ak[k], v_bd[k]) for k in keys}

    p = {k: a_ab[k].astype(BF16) for k in keys}
    x = {k: eye + a_ab[k] for k in keys}
    span = 2
    if span < chunk:
        p = {k: _bdot(p[k], p[k]) for k in keys}
    while span < chunk:
        last = span * 2 >= chunk
        if merged and not last:
            px = {k: _bdot(p[k], jnp.concatenate([p[k].astype(BF16), x[k].astype(BF16)], axis=1)) for k in keys}
            p = {k: px[k][:, :gc] for k in keys}
            x = {k: x[k] + px[k][:, gc:] for k in keys}
        else:
            pb = {k: p[k].astype(BF16) for k in keys}
            x = {k: x[k] + _bdot(pb[k], x[k]) for k in keys}
            if not last:
                p = {k: _bdot(pb[k], pb[k]) for k in keys}
        span *= 2
    tinv = {k: x[k].astype(BF16) for k in keys}

    for sc in range(nsub):
        slot = sc if per_seq else 0
        hs = [ho_ref[slot, q] for q in groups]
        hs_b = [h.astype(BF16) for h in hs]
        gl_rows = [gl_ref[sc, :, sl] for sl in sls]
        if merged:
            arh = [_bdot(ar[(sc, q)], hs_b[q]) for q in groups]
            u = [_bdot(tinv[(sc, q)], arh[q][:gc] + akv[(sc, q)]).astype(BF16) for q in groups]
            uv = [jnp.concatenate([u[q], v_bd[(sc, q)]], axis=0) for q in groups]
            y_bd = [arh[q][gc:] + _bdot(a_rbk[(sc, q)], uv[q]) for q in groups]
            for q in groups:
                gl_col = jnp.sum(jnp.where(eye_l, gl_rows[q], 0.0), axis=1, keepdims=True)
                bk_g = jnp.concatenate([b_bd[(sc, q)] * gl_rows[q], k_bd[(sc, q)] * gl_rows[q]], axis=0)
                ho_ref[slot, q] = hs[q] * gl_col + _bdot_tn(bk_g, uv[q])
        else:
            ah = [_bdot(ab_[(sc, q)], hs_b[q]) for q in groups]
            rh = [_bdot(rb_[(sc, q)], hs_b[q]) for q in groups]
            u = [_bdot(tinv[(sc, q)], ah[q] + akv[(sc, q)]).astype(BF16) for q in groups]
            y_bd = [rh[q] + _bdot(a_rb[(sc, q)], u[q]) + _bdot(a_rk[(sc, q)], v_bd[(sc, q)]) for q in groups]
            for q in groups:
                gl_col = jnp.sum(jnp.where(eye_l, gl_rows[q], 0.0), axis=1, keepdims=True)
                ho_ref[slot, q] = (hs[q] * gl_col + _bdot_tn(b_bd[(sc, q)] * gl_rows[q], u[q])
                                   + _bdot_tn(k_bd[(sc, q)] * gl_rows[q], v_bd[(sc, q)]))

        for q in groups:
            sl = sls[q]
            mu = jnp.sum(y_bd[q], axis=1, keepdims=True) * (1.0 / RW_N)
            yc = jnp.where(own, y_bd[q] - mu, 0.0)
            var = jnp.sum(yc * yc, axis=1, keepdims=True) * (1.0 / RW_N)
            tot = (yc * lax.rsqrt(var + RW_LNX_EPS) * vec_ref[0:1, sl] + jnp.where(own, vec_ref[1:2, sl], 0.0)
                   + bonus[(sc, q)] * v_f[(sc, q)])
            y = tot[0:chunk]
            for hh in range(1, nh):
                y = y + tot[hh * chunk:(hh + 1) * chunk]
            y_ref[sc * chunk:(sc + 1) * chunk, sl] = y.astype(y_ref.dtype)


def _rwkv_scan_body_old(rt_ref, kt_ref, at_ref, bt_ref, v_ref, gl_ref, h0_ref, vec_ref, ones_ref,
                        y_ref, ho_ref, *, nh, chunk):
    @pl.when(pl.program_id(1) == 0)
    def _():
        ho_ref[...] = h0_ref[...]

    gl_lanes = nh * RW_N
    ng = RW_H // nh
    gc = nh * chunk
    row_head = lax.broadcasted_iota(jnp.int32, (gc, gl_lanes), 0) // chunk
    lane_head = lax.broadcasted_iota(jnp.int32, (gc, gl_lanes), 1) // RW_N
    own = row_head == lane_head
    ri = lax.broadcasted_iota(jnp.int32, (gc, gc), 0)
    ci = lax.broadcasted_iota(jnp.int32, (gc, gc), 1)
    same = (ri // chunk) == (ci // chunk)
    strict = same & ((ri % chunk) > (ci % chunk))
    incl = same & ((ri % chunk) >= (ci % chunk))
    eye = (ri == ci).astype(F32)
    eye_l = (lax.broadcasted_iota(jnp.int32, (gl_lanes, gl_lanes), 0)
             == lax.broadcasted_iota(jnp.int32, (gl_lanes, gl_lanes), 1))
    ones_bd = ones_ref[...]

    def blockdiag(xg):
        xx = jnp.concatenate([xg] * nh, axis=0) if nh > 1 else xg
        return jnp.where(own, xx, 0.0)

    groups = range(ng)
    sls = [slice(q * gl_lanes, (q + 1) * gl_lanes) for q in groups]
    merged = gc == GROUP_ROWS
    gl_rows = [gl_ref[0, :, sl] for sl in sls]
    r_bd = [blockdiag(rt_ref[:, sl]) for sl in sls]
    k_bd = [blockdiag(kt_ref[:, sl]) for sl in sls]
    a_bd = [blockdiag(at_ref[:, sl]) for sl in sls]
    b_bd = [blockdiag(bt_ref[:, sl]) for sl in sls]
    v_bd = [blockdiag(v_ref[:, sl]).astype(BF16) for sl in sls]
    hs = [ho_ref[0, q] for q in groups]
    hs_b = [h.astype(BF16) for h in hs]
    if merged:
        ar = [jnp.concatenate([a_bd[q], r_bd[q]], axis=0).astype(BF16) for q in groups]
        bk = [jnp.concatenate([b_bd[q], k_bd[q]], axis=0).astype(BF16) for q in groups]
        amat = [_bdot_nt(ar[q], bk[q]) for q in groups]
        a_ab = [jnp.where(strict, m[:gc, :gc], 0.0) for m in amat]
        a_ak = [jnp.where(strict, m[:gc, gc:], 0.0).astype(BF16) for m in amat]
        a_rbk = [jnp.concatenate([jnp.where(incl, m[gc:, :gc], 0.0), jnp.where(incl, m[gc:, gc:], 0.0)],
                                 axis=1).astype(BF16) for m in amat]
        arh = [_bdot(ar[q], hs_b[q]) for q in groups]
        ah = [m[:gc] for m in arh]
        rh = [m[gc:] for m in arh]
    else:
        ab_, rb_ = [x.astype(BF16) for x in a_bd], [x.astype(BF16) for x in r_bd]
        bb_, kb_ = [x.astype(BF16) for x in b_bd], [x.astype(BF16) for x in k_bd]
        a_ab = [jnp.where(strict, _bdot_nt(ab_[q], bb_[q]), 0.0) for q in groups]
        a_ak = [jnp.where(strict, _bdot_nt(ab_[q], kb_[q]), 0.0).astype(BF16) for q in groups]
        a_rb = [jnp.where(incl, _bdot_nt(rb_[q], bb_[q]), 0.0).astype(BF16) for q in groups]
        a_rk = [jnp.where(incl, _bdot_nt(rb_[q], kb_[q]), 0.0).astype(BF16) for q in groups]
        ah = [_bdot(ab_[q], hs_b[q]) for q in groups]
        rh = [_bdot(rb_[q], hs_b[q]) for q in groups]
    akv = [_bdot(a_ak[q], v_bd[q]) for q in groups]

    p = [m.astype(BF16) for m in a_ab]
    x = [eye + m for m in a_ab]
    span = 2
    if span < chunk:
        p = [_bdot(p[q], p[q]) for q in groups]
    while span < chunk:
        last = span * 2 >= chunk
        if merged and not last:
            px = [_bdot(p[q], jnp.concatenate([p[q].astype(BF16), x[q].astype(BF16)], axis=1)) for q in groups]
            p = [m[:, :gc] for m in px]
            x = [x[q] + px[q][:, gc:] for q in groups]
        else:
            pb = [m.astype(BF16) for m in p]
            x = [x[q] + _bdot(pb[q], x[q]) for q in groups]
            if not last:
                p = [_bdot(pb[q], pb[q]) for q in groups]
        span *= 2

    u = [_bdot(x[q], ah[q] + akv[q]).astype(BF16) for q in groups]
    if merged:
        uv = [jnp.concatenate([u[q], v_bd[q]], axis=0) for q in groups]
        y_bd = [rh[q] + _bdot(a_rbk[q], uv[q]) for q in groups]
        for q in groups:
            gl_col = jnp.sum(jnp.where(eye_l, gl_rows[q], 0.0), axis=1, keepdims=True)
            bk_g = jnp.concatenate([b_bd[q] * gl_rows[q], k_bd[q] * gl_rows[q]], axis=0)
            ho_ref[0, q] = hs[q] * gl_col + _bdot_tn(bk_g, uv[q])
    else:
        y_bd = [rh[q] + _bdot(a_rb[q], u[q]) + _bdot(a_rk[q], v_bd[q]) for q in groups]
        for q in groups:
            gl_col = jnp.sum(jnp.where(eye_l, gl_rows[q], 0.0), axis=1, keepdims=True)
            ho_ref[0, q] = (hs[q] * gl_col + _bdot_tn(b_bd[q] * gl_rows[q], u[q])
                            + _bdot_tn(k_bd[q] * gl_rows[q], v_bd[q]))

    for q in groups:
        sl = sls[q]
        y = y_bd[q][0:chunk]
        for hh in range(1, nh):
            y = y + y_bd[q][hh * chunk:(hh + 1) * chunk]
        mu = _lane_group_sum(y, ones_bd) * (1.0 / RW_N)
        yc = y - mu
        var = _lane_group_sum(yc * yc, ones_bd) * (1.0 / RW_N)
        yn = yc * lax.rsqrt(var + RW_LNX_EPS) * vec_ref[0:1, sl] + vec_ref[1:2, sl]
        bonus = _lane_group_sum(rt_ref[:, sl] * kt_ref[:, sl] * vec_ref[2:3, sl], ones_bd) * v_ref[:, sl]
        y_ref[:, sl] = yn + bonus


RWKV_SUBCHUNKS = 4
PROJ_LANES = 256


def _rwkv_layer(x2d, shift_prev, s0, v_first, w, ri, nw, b, t):
    n = b * t
    chunk = _chunk_of(t)
    tl = _tiling(b, t, 512)
    G, J, R, P = tl["G"], tl["J"], tl["R"], tl["P"]
    has_vres = v_first is not None
    vi = ri - 1
    adt = BF16 if chunk % 16 == 0 else F32
    bc = chunk if chunk == 64 else R
    tri = _chunk_masks(chunk, bc).astype(BF16)
    tri = jnp.concatenate([tri, tri], axis=1)
    li = jnp.arange(LANES)
    ones_bd = ((li[:, None] // RW_N) == (li[None, :] // RW_N)).astype(BF16)
    ones_bd = jnp.concatenate([ones_bd, ones_bd], axis=0)
    zero = jnp.zeros((D_MODEL,), F32)
    vec = jnp.stack([*(w["rw_mu"][ri][i] for i in range(6)), w["rw_w0"][ri], w["rw_a0"][ri], w["rw_kk"][ri],
                     w["rw_ka"][ri], nw[0], w["rw_v0"][vi] if has_vres else zero, zero, zero, zero, zero])
    row = lambda g, j: (g * J + j, 0)
    row_spec = pl.BlockSpec((R, D_MODEL), row)
    ins = [x2d, _pad_state(shift_prev[:, None, :])]
    specs = [row_spec, pl.BlockSpec((P, D_MODEL), lambda g, j: (g, 0))]
    if has_vres:
        ins.append(v_first)
        specs.append(row_spec)
    wl = [vec, w["rw_wrkv"][ri].astype(BF16), w["rw_w1"][ri].astype(BF16), w["rw_w2"][ri].astype(BF16),
          w["rw_a1"][ri].astype(BF16), w["rw_a2"][ri].astype(BF16)]
    if has_vres:
        wl += [w["rw_v1"][vi].astype(BF16), w["rw_v2"][vi].astype(BF16)]
    wl += [w["rw_g1"][ri].astype(BF16), w["rw_g2"][ri].astype(BF16), tri, ones_bd]
    ins += wl
    specs += [_const_spec(a.shape) for a in wl]
    nc_tile = R // chunk
    outs = pl.pallas_call(
        functools.partial(_rwkv_proj_body, has_vres=has_vres, chunk=chunk),
        grid=(G, J),
        in_specs=specs,
        out_specs=[row_spec] * 6 + [pl.BlockSpec((nc_tile, 1, D_MODEL), lambda g, j: (g * J + j, 0, 0)),
                                    pl.BlockSpec((P, D_MODEL), lambda g, j: (g, 0))],
        out_shape=[jax.ShapeDtypeStruct((n, D_MODEL), adt)] * 6
        + [jax.ShapeDtypeStruct((n // chunk, 1, D_MODEL), F32), jax.ShapeDtypeStruct((b * SUBLANES, D_MODEL), F32)],
        scratch_shapes=[pltpu.VMEM((P, D_MODEL), F32)],
        compiler_params=_params(2),
        name="rwkv_proj",
    )(*ins)
    rt, kt, at, bt, v, g, gl, hl = outs
    shift_new = hl.reshape(b, SUBLANES, D_MODEL)[:, -1]

    nh = LANES // RW_N
    ng = RW_H // nh
    gl_lanes = nh * RW_N
    hkv = jnp.swapaxes(s0, -1, -2).reshape(b, ng, nh, RW_N, RW_N)
    zblk = jnp.zeros((b, ng, RW_N, RW_N), F32)
    h0 = jnp.concatenate(
        [jnp.concatenate([hkv[:, :, i] if i == jj else zblk for jj in range(nh)], axis=-1) for i in range(nh)],
        axis=-2)
    svec = jnp.stack([w["rw_lnx_w"][ri], w["rw_lnx_b"][ri], w["rw_rk"][ri].reshape(D_MODEL),
                      zero, zero, zero, zero, zero])
    per_seq = t == chunk and b % RWKV_SUBCHUNKS == 0
    nsub = RWKV_SUBCHUNKS if per_seq or (t // chunk) % RWKV_SUBCHUNKS == 0 else 1
    nct = 1 if per_seq else t // (chunk * nsub)
    nseq = nsub if per_seq else 1
    crow = lambda bb, j: (bb * nct + j, 0)
    cspec = pl.BlockSpec((chunk * nsub, D_MODEL), crow)
    hspec = pl.BlockSpec((nseq, ng, gl_lanes, gl_lanes), lambda bb, j: (bb, 0, 0, 0))
    y, hout = pl.pallas_call(
        functools.partial(_rwkv_scan_body, nh=nh, chunk=chunk, nsub=nsub, per_seq=per_seq),
        grid=(b // nseq, nct),
        in_specs=[cspec] * 5 + [pl.BlockSpec((nsub, 1, D_MODEL), lambda bb, j: (bb * nct + j, 0, 0)), hspec,
                                _const_spec((8, D_MODEL))],
        out_specs=[cspec, hspec],
        out_shape=[jax.ShapeDtypeStruct((n, D_MODEL), adt),
                   jax.ShapeDtypeStruct((b, ng, gl_lanes, gl_lanes), F32)],
        compiler_params=_params(2),
        name="rwkv_scan",
    )(rt, kt, at, bt, v, gl, h0, svec)
    s_new = jnp.stack([hout[:, :, i * RW_N:(i + 1) * RW_N, i * RW_N:(i + 1) * RW_N] for i in range(nh)],
                      axis=2)
    s_new = jnp.swapaxes(s_new, -1, -2).reshape(b, RW_H, RW_N, RW_N)
    pre = ("gated", [y, g], [w["rw_wo"][ri].astype(BF16)])
    return pre, shift_new, s_new, (v if not has_vres else v_first)


def _rope_lanes(x, tab_ref):
    half = MLA_ROPE // 2
    return (x * tab_ref[0] + pltpu.roll(x, LANES - half, 1) * tab_ref[1] + pltpu.roll(x, half, 1) * tab_ref[2])


def _mla_proj_body(x_ref, nw_ref, tab_ref, winq_ref, winc_ref, wink_ref, qn_ref, kvn_ref, wqn_ref, wqr_ref,
                   wuk_ref, c_ref, kr_ref, kcat_ref, qcat_ref):
    h = _rms(x_ref[...], nw_ref[0:1, :]).astype(BF16)
    cq = _rms(jnp.dot(h, winq_ref[...], preferred_element_type=F32), qn_ref[...]).astype(BF16)
    c = _rms(jnp.dot(h, winc_ref[...], preferred_element_type=F32), kvn_ref[...])
    kr = _rope_lanes(jnp.dot(h, wink_ref[...], preferred_element_type=F32), tab_ref)
    c_ref[...] = c
    kr_ref[...] = kr
    adt = kcat_ref.dtype
    kcat_ref[:, 0:MLA_KV_LORA] = c.astype(adt)
    kcat_ref[:, MLA_KV_LORA:MLA_QK] = kr.astype(adt)
    qn = jnp.dot(cq, wqn_ref[...], preferred_element_type=F32).astype(BF16)
    qr = jnp.dot(cq, wqr_ref[...], preferred_element_type=F32)
    for pr in range(MLA_H // 2):
        ql = jnp.dot(qn[:, pr * LANES:(pr + 1) * LANES], wuk_ref[pr], preferred_element_type=F32) * MLA_SCALE
        qcat_ref[2 * pr, :, 0:MLA_KV_LORA] = ql[:, :MLA_KV_LORA].astype(adt)
        qcat_ref[2 * pr + 1, :, 0:MLA_KV_LORA] = ql[:, MLA_KV_LORA:].astype(adt)
    for hh in range(MLA_H):
        qro = _rope_lanes(qr[:, hh * LANES:(hh + 1) * LANES], tab_ref) * MLA_SCALE
        qcat_ref[hh, :, MLA_KV_LORA:MLA_QK] = qro.astype(adt)


MLA_TQ = 256
MLA_TK = 256
MLA_SPLIT = 16


def _mla_prompt_body(q_ref, k_ref, o_ref, m_ref, l_ref, acc_ref):
    i = pl.program_id(1)
    rows = MLA_H * MLA_TQ
    q = q_ref[...].reshape(rows, MLA_QK)
    m_ref[...] = jnp.full((rows, LANES), -jnp.inf, F32)
    l_ref[...] = jnp.zeros((rows, LANES), F32)
    acc_ref[...] = jnp.zeros((rows, MLA_KV_LORA), F32)
    ones = jnp.ones((MLA_TK, LANES), BF16)
    reps = MLA_TK // LANES

    sub = rows // MLA_SPLIT

    def block(k0, masked):
        kblk = k_ref[pl.ds(k0, MLA_TK), :]
        ss = [lax.dot_general(q[g * sub:(g + 1) * sub], kblk, (((1,), (1,)), ((), ())),
                              preferred_element_type=F32) for g in range(MLA_SPLIT)]
        for g in range(MLA_SPLIT):
            rs = slice(g * sub, (g + 1) * sub)
            s = ss[g]
            if masked:
                qpos = i * MLA_TQ + (g * sub + lax.broadcasted_iota(jnp.int32, (sub, MLA_TK), 0)) % MLA_TQ
                kpos = k0 + lax.broadcasted_iota(jnp.int32, (sub, MLA_TK), 1)
                s = jnp.where(kpos <= qpos, s, -jnp.inf)
            m_old = m_ref[rs, :]
            m_new = jnp.maximum(m_old, jnp.max(s, axis=1, keepdims=True))
            alpha = jnp.exp(m_old - m_new)
            pf = jnp.exp(s - jnp.concatenate([m_new] * reps, axis=1))
            l_ref[rs, :] = l_ref[rs, :] * alpha + jnp.sum(pf, axis=1, keepdims=True)
            acc_ref[rs, :] = (acc_ref[rs, :] * jnp.concatenate([alpha] * (MLA_KV_LORA // LANES), axis=1)
                              + jnp.dot(pf.astype(BF16), kblk[:, :MLA_KV_LORA], preferred_element_type=F32))
            m_ref[rs, :] = m_new

    def full_step(kb, carry):
        block(pl.multiple_of(kb * MLA_TK, MLA_TK), False)
        return carry

    n_full = (i * MLA_TQ) // MLA_TK
    lax.fori_loop(0, n_full, full_step, 0)
    block(pl.multiple_of(n_full * MLA_TK, MLA_TK), True)
    o = acc_ref[...] / jnp.concatenate([l_ref[...]] * (MLA_KV_LORA // LANES), axis=1)
    o_ref[...] = o.reshape(MLA_H, MLA_TQ, MLA_KV_LORA).astype(BF16)


MLA_PP = 32
MLA_GROUPS = 2


def _mla_sample_body(pt_ref, q_ref, kn_ref, *rest):
    lat_refs = rest[:MLA_PP]
    kro_refs = rest[MLA_PP:2 * MLA_PP]
    o_ref, m_ref, l_ref, acc_ref = rest[2 * MLA_PP:]
    j = pl.program_id(1)
    t = q_ref.shape[1]
    rows = MLA_H * t
    q = q_ref[...].reshape(rows, MLA_QK).astype(BF16)
    ql = q[:, :MLA_KV_LORA]
    qr = q[:, MLA_KV_LORA:MLA_KV_LORA + MLA_ROPE]

    @pl.when(j == 0)
    def _():
        m_ref[...] = jnp.full(m_ref.shape, -jnp.inf, F32)
        l_ref[...] = jnp.zeros(l_ref.shape, F32)
        acc_ref[...] = jnp.zeros(acc_ref.shape, F32)

    vrep = MLA_KV_LORA // LANES

    def update(g, s, vals, row_sum):
        m_old = m_ref[g]
        m_new = jnp.maximum(m_old, jnp.max(s, axis=1, keepdims=True))
        alpha = jnp.exp(m_old - m_new)
        if s.shape[1] % LANES == 0:
            p = jnp.exp(s - jnp.concatenate([m_new] * (s.shape[1] // LANES), axis=1)).astype(BF16)
        else:
            p = jnp.exp(s - m_new[:, 0:1]).astype(BF16)
        l_ref[g] = l_ref[g] * alpha + row_sum(p)
        acc_ref[g] = (acc_ref[g] * jnp.concatenate([alpha] * vrep, axis=1)
                      + jnp.dot(p, vals, preferred_element_type=F32))
        m_ref[g] = m_new

    per = MLA_PP // MLA_GROUPS
    ones = jnp.ones((per * lat_refs[0].shape[1], LANES), BF16)
    scores, values = [], []
    for g in range(MLA_GROUPS):
        cbs, s_parts = [], []
        for pp in range(g * per, (g + 1) * per):
            cb = lat_refs[pp][0].astype(BF16)
            kbt = kro_refs[pp][0].astype(BF16)
            s_parts.append(lax.dot_general(ql, cb, (((1,), (1,)), ((), ())), preferred_element_type=F32)
                           + jnp.dot(qr, kbt, preferred_element_type=F32))
            cbs.append(cb)
        scores.append(jnp.concatenate(s_parts, axis=1))
        values.append(jnp.concatenate(cbs, axis=0))
    for g in range(MLA_GROUPS):
        update(g, scores[g], values[g], lambda p: jnp.dot(p, ones, preferred_element_type=F32))

    @pl.when(j == pl.num_programs(1) - 1)
    def _():
        kn = kn_ref[...].astype(BF16)
        s = lax.dot_general(q, kn, (((1,), (1,)), ((), ())), preferred_element_type=F32)
        qpos = lax.broadcasted_iota(jnp.int32, (rows, t), 0) % t
        kpos = lax.broadcasted_iota(jnp.int32, (rows, t), 1)
        s = jnp.where(kpos <= qpos, s, -jnp.inf)
        update(0, s, kn[:, :MLA_KV_LORA], lambda p: jnp.sum(p.astype(F32), axis=1, keepdims=True))
        m_all = m_ref[0]
        for g in range(1, MLA_GROUPS):
            m_all = jnp.maximum(m_all, m_ref[g])
        l_all = jnp.zeros((rows, LANES), F32)
        acc = jnp.zeros((rows, MLA_KV_LORA), F32)
        for g in range(MLA_GROUPS):
            wgt = jnp.exp(m_ref[g] - m_all)
            l_all = l_all + l_ref[g] * wgt
            acc = acc + acc_ref[g] * jnp.concatenate([wgt] * vrep, axis=1)
        o = acc / jnp.concatenate([l_all] * vrep, axis=1)
        o_ref[...] = o.reshape(MLA_H, t, MLA_KV_LORA).astype(o_ref.dtype)


def _mla_out_body(x_ref, o_ref, wuv_ref, wo_ref, nw_ref, xo_ref):
    parts = []
    for pr in range(MLA_H // 2):
        wp = wuv_ref[pr]
        parts.append(jnp.dot(o_ref[2 * pr].astype(BF16), wp[:MLA_KV_LORA], preferred_element_type=F32)
                     + jnp.dot(o_ref[2 * pr + 1].astype(BF16), wp[MLA_KV_LORA:], preferred_element_type=F32))
    v = jnp.concatenate(parts, axis=1).astype(BF16)
    o = jnp.dot(v, wo_ref[...], preferred_element_type=F32)
    xo_ref[...] = x_ref[...] + _rms(o, nw_ref[0:1, :])


def _mla_layer(x2d, pos, w, mi, nw, b, t, paged):
    n = b * t
    tl = _tiling(b, t, 512)
    G, J, R = tl["G"], tl["J"], tl["R"]
    adt = BF16 if t % 16 == 0 else F32
    half = MLA_ROPE // 2
    inv = ROPE_THETA ** (-jnp.arange(half, dtype=F32) / half)
    ang = pos.astype(F32)[:, None] * inv[None, :]
    cos, sin = jnp.cos(ang), jnp.sin(ang)
    zpad = jnp.zeros((t, LANES - MLA_ROPE), F32)
    zh = jnp.zeros((t, half), F32)
    tab = jnp.stack([jnp.concatenate([cos, cos, zpad], 1), jnp.concatenate([-sin, zh, zpad], 1),
                     jnp.concatenate([zh, sin, zpad], 1)])
    if G == 1:
        tab = jnp.tile(tab, (1, b, 1))
    w_in = w["mla_w_in"][mi]
    winq = w_in[:, :MLA_Q_LORA].astype(BF16)
    winc = w_in[:, MLA_Q_LORA:MLA_Q_LORA + MLA_KV_LORA].astype(BF16)
    wink = jnp.pad(w_in[:, MLA_Q_LORA + MLA_KV_LORA:], ((0, 0), (0, LANES - MLA_ROPE))).astype(BF16)
    wqb = w["mla_w_qb"][mi].reshape(MLA_Q_LORA, MLA_H, MLA_NOPE + MLA_ROPE)
    wqn = wqb[:, :, :MLA_NOPE].reshape(MLA_Q_LORA, MLA_H * MLA_NOPE).astype(BF16)
    wqr = jnp.pad(wqb[:, :, MLA_NOPE:], ((0, 0), (0, 0), (0, LANES - MLA_ROPE))
                  ).reshape(MLA_Q_LORA, MLA_H * LANES).astype(BF16)
    wuk = jnp.transpose(w["mla_w_uk"][mi], (1, 2, 0)).reshape(MLA_H // 2, 2, MLA_NOPE, MLA_KV_LORA)
    wuk_bd = jnp.einsum("pinc,ij->pinjc", wuk, jnp.eye(2, dtype=F32)).reshape(
        MLA_H // 2, 2 * MLA_NOPE, 2 * MLA_KV_LORA).astype(BF16)
    wuv = jnp.transpose(w["mla_w_uv"][mi], (1, 0, 2)).reshape(MLA_H // 2, 2, MLA_KV_LORA, MLA_V)
    wuv_bd = jnp.einsum("picv,ij->picjv", wuv, jnp.eye(2, dtype=F32)).reshape(
        MLA_H // 2, 2 * MLA_KV_LORA, 2 * MLA_V).astype(BF16)
    nwa = jnp.concatenate([nw[0:1], jnp.zeros((7, D_MODEL), F32)])
    nwb = jnp.concatenate([nw[1:2], jnp.zeros((7, D_MODEL), F32)])
    row = lambda g, j: (g * J + j, 0)
    wl = [winq, winc, wink, w["mla_q_norm"][mi][None, :], w["mla_kv_norm"][mi][None, :], wqn, wqr, wuk_bd]
    c, kr, kcat, qcat = pl.pallas_call(
        _mla_proj_body,
        grid=(G, J),
        in_specs=[pl.BlockSpec((R, D_MODEL), row), _const_spec((8, D_MODEL)),
                  pl.BlockSpec((3, R, LANES), lambda g, j: (0, j, 0))] + [_const_spec(a.shape) for a in wl],
        out_specs=[pl.BlockSpec((R, MLA_KV_LORA), row), pl.BlockSpec((R, LANES), row),
                   pl.BlockSpec((R, MLA_QK), row), pl.BlockSpec((MLA_H, R, MLA_QK), lambda g, j: (0, g * J + j, 0))],
        out_shape=[jax.ShapeDtypeStruct((n, MLA_KV_LORA), F32), jax.ShapeDtypeStruct((n, LANES), F32),
                   jax.ShapeDtypeStruct((n, MLA_QK), adt), jax.ShapeDtypeStruct((MLA_H, n, MLA_QK), adt)],
        compiler_params=_params(2),
        name="mla_proj",
    )(x2d, nwa, tab, *wl)

    if paged is None:
        nq = t // MLA_TQ
        rows = MLA_H * MLA_TQ
        o = pl.pallas_call(
            _mla_prompt_body,
            grid=(b, nq),
            in_specs=[pl.BlockSpec((MLA_H, MLA_TQ, MLA_QK), lambda bb, i: (0, bb * nq + i, 0)),
                      pl.BlockSpec((t, MLA_QK), lambda bb, i: (bb, 0))],
            out_specs=pl.BlockSpec((MLA_H, MLA_TQ, MLA_KV_LORA), lambda bb, i: (0, bb * nq + i, 0)),
            out_shape=jax.ShapeDtypeStruct((MLA_H, n, MLA_KV_LORA), BF16),
            scratch_shapes=[pltpu.VMEM((rows, LANES), F32), pltpu.VMEM((rows, LANES), F32),
                            pltpu.VMEM((rows, MLA_KV_LORA), F32)],
            compiler_params=_params(2),
            name="mla_attend_prompt",
        )(qcat, kcat)
    else:
        pages_c, pages_kr, page_table = paged
        page = pages_c.shape[1]
        npg = page_table.shape[1]
        assert npg % MLA_PP == 0
        rows = MLA_H * t

        def page_map(pp):
            return lambda bb, j, pt: (pt[bb, j * MLA_PP + pp], 0, 0)

        grid_spec = pltpu.PrefetchScalarGridSpec(
            num_scalar_prefetch=1,
            grid=(b, npg // MLA_PP),
            in_specs=[pl.BlockSpec((MLA_H, t, MLA_QK), lambda bb, j, pt: (0, bb, 0)),
                      pl.BlockSpec((t, MLA_QK), lambda bb, j, pt: (bb, 0))]
            + [pl.BlockSpec((1, page, MLA_KV_LORA), page_map(pp)) for pp in range(MLA_PP)]
            + [pl.BlockSpec((1, MLA_ROPE, page), page_map(pp)) for pp in range(MLA_PP)],
            out_specs=pl.BlockSpec((MLA_H, t, MLA_KV_LORA), lambda bb, j, pt: (0, bb, 0)),
            scratch_shapes=[pltpu.VMEM((MLA_GROUPS, rows, LANES), F32), pltpu.VMEM((MLA_GROUPS, rows, LANES), F32),
                            pltpu.VMEM((MLA_GROUPS, rows, MLA_KV_LORA), F32)],
        )
        o = pl.pallas_call(
            _mla_sample_body,
            grid_spec=grid_spec,
            out_shape=jax.ShapeDtypeStruct((MLA_H, n, MLA_KV_LORA), adt),
            compiler_params=_params(2),
            name="mla_attend_sample",
        )(page_table, qcat, kcat, *([pages_c] * MLA_PP), *([jnp.swapaxes(pages_kr, 1, 2)] * MLA_PP))

    pre = ("mla", [o], [wuv_bd, w["mla_wo"][mi].astype(BF16)])
    return pre, c.reshape(b, t, MLA_KV_LORA), kr[:, :MLA_ROPE].reshape(b, t, MLA_ROPE)


GDN_CW = 512


def _gdn_proj_body(x_ref, prev_ref, nw_ref, wqkv_ref, wz_ref, wbg_ref, cw_ref, gvec_ref, tri_ref,
                   q_ref, k_ref, v_ref, z_ref, beta_ref, gc_ref, st_ref, carry_ref):
    @pl.when(pl.program_id(1) == 0)
    def _():
        carry_ref[...] = prev_ref[...]

    rows = x_ref.shape[0]
    p = carry_ref.shape[0]
    h = _rms(x_ref[...], nw_ref[0:1, :]).astype(BF16)
    adt = z_ref.dtype
    bg = jnp.dot(h, wbg_ref[...], preferred_element_type=F32)
    beta_ref[...] = _sigmoid(bg)
    g = -jnp.exp(gvec_ref[0:1, :]) * _softplus(bg + gvec_ref[1:2, :])
    gc_ref[...] = _chunk_cumsum(g, tri_ref[...])
    nch = GDN_CONV_DIM // GDN_CW
    zw = GDN_V_DIM // nch

    def up(c):
        return jnp.dot(h, wqkv_ref[:, c * GDN_CW:(c + 1) * GDN_CW], preferred_element_type=F32)

    u_nxt = up(0)
    for c in range(nch):
        sl = slice(c * GDN_CW, (c + 1) * GDN_CW)
        u = u_nxt
        u_nxt = up(c + 1) if c + 1 < nch else None
        z_ref[:, c * zw:(c + 1) * zw] = jnp.dot(h, wz_ref[:, c * zw:(c + 1) * zw],
                                                preferred_element_type=F32).astype(adt)
        prev = carry_ref[:, sl]
        u1 = _shift_rows(u, prev, 1)
        near = cw_ref[3:4, sl] * u + cw_ref[2:3, sl] * u1
        far = cw_ref[1:2, sl] * u + cw_ref[0:1, sl] * u1
        prev1 = pltpu.roll(prev, 1, 0)
        far_prev = cw_ref[1:2, sl] * prev + cw_ref[0:1, sl] * prev1
        y = near + _shift_rows(far, far_prev, 2)
        tail = u[rows - p:, :]
        carry_ref[:, sl] = tail
        st_ref[:, sl] = tail
        y = _silu(y)
        off = c * GDN_CW
        if off < 2 * GDN_QK_DIM:
            dst, base, scale = (q_ref, off, GDN_DK ** -0.5) if off < GDN_QK_DIM else (k_ref, off - GDN_QK_DIM, 1.0)
            for hh in range(GDN_CW // GDN_DK):
                yh = y[:, hh * GDN_DK:(hh + 1) * GDN_DK]
                inv = lax.rsqrt(jnp.sum(yh * yh, axis=-1, keepdims=True) + 1e-6)
                dst[:, base + hh * GDN_DK:base + (hh + 1) * GDN_DK] = (yh * (inv * scale)).astype(adt)
        else:
            v_ref[:, off - 2 * GDN_QK_DIM:off - 2 * GDN_QK_DIM + GDN_CW] = y.astype(adt)


def _gdn_chunk_body(q_ref, k_ref, v_ref, z_ref, gc_ref, beta_ref, s0_ref, nw_ref,
                    o_ref, so_ref, *, nh, chunk, nsub):
    @pl.when(pl.program_id(1) == 0)
    def _():
        so_ref[...] = s0_ref[...]

    ng = GDN_V_H // nh
    gc = nh * chunk
    rep = GDN_V_H // GDN_QK_H
    ri = lax.broadcasted_iota(jnp.int32, (gc, gc), 0)
    ci = lax.broadcasted_iota(jnp.int32, (gc, gc), 1)
    same = (ri // chunk) == (ci // chunk)
    strict = same & ((ri % chunk) > (ci % chunk))
    incl = same & ((ri % chunk) >= (ci % chunk))
    last = same & ((ci % chunk) == chunk - 1)
    eye = (ri == ci).astype(F32)
    row_head = lax.broadcasted_iota(jnp.int32, (gc, GDN_DK), 0) // chunk
    groups = range(ng)
    heads = [[q * nh + i for i in range(nh)] for q in groups]
    keys = [(sc, q) for sc in range(nsub) for q in groups]

    def stack(ref, sc, hds, width):
        parts = [ref[sc * chunk:(sc + 1) * chunk, hd * width:(hd + 1) * width] for hd in hds]
        return parts[0] if len(parts) == 1 else jnp.concatenate(parts, axis=0)

    def col(ref, sc, lanes):
        parts = [ref[sc * chunk:(sc + 1) * chunk, ln:ln + 1] for ln in lanes]
        return parts[0] if len(parts) == 1 else jnp.concatenate(parts, axis=0)

    k_st = {k: stack(k_ref, k[0], [hd // rep for hd in heads[k[1]]], GDN_DK) for k in keys}
    q_st = {k: stack(q_ref, k[0], [hd // rep for hd in heads[k[1]]], GDN_DK) for k in keys}
    v_st = {k: stack(v_ref, k[0], heads[k[1]], GDN_DV) for k in keys}
    gcol = {k: col(gc_ref, k[0], [GDN_V_H + hd for hd in heads[k[1]]]) for k in keys}
    bcol = {k: col(beta_ref, k[0], heads[k[1]]) for k in keys}
    grow = {k: jnp.sum(jnp.where(ri == ci, gcol[k], 0.0), axis=0, keepdims=True) for k in keys}
    k_b = {k: k_st[k].astype(BF16) for k in keys}
    kq = {k: _bdot_nt(jnp.concatenate([k_b[k], q_st[k].astype(BF16)], axis=0), k_b[k]) for k in keys}
    decay = {k: jnp.exp(jnp.where(incl, gcol[k] - grow[k], -jnp.inf)) for k in keys}
    a = {k: jnp.where(strict, kq[k][:gc] * bcol[k] * decay[k], 0.0) for k in keys}
    aqk = {k: jnp.where(incl, kq[k][gc:] * decay[k], 0.0).astype(BF16) for k in keys}

    p = {k: (-a[k]).astype(BF16) for k in keys}
    x = {k: eye - a[k] for k in keys}
    span = 2
    if span < chunk:
        p = {k: _bdot(p[k], p[k]) for k in keys}
    while span < chunk:
        if span * 2 < chunk:
            px = {k: _bdot(p[k], jnp.concatenate([p[k].astype(BF16), x[k].astype(BF16)], axis=1)) for k in keys}
            p = {k: px[k][:, :gc] for k in keys}
            x = {k: x[k] + px[k][:, gc:] for k in keys}
        else:
            x = {k: x[k] + _bdot(p[k], x[k]) for k in keys}
        span *= 2

    egc = {k: jnp.exp(gcol[k]) for k in keys}
    uw = {k: _bdot(x[k], jnp.concatenate([v_st[k] * bcol[k], k_st[k] * (bcol[k] * egc[k])], axis=1)) for k in keys}
    glast = {k: jnp.sum(jnp.where(last, grow[k], 0.0), axis=1, keepdims=True) for k in keys}
    kg = {k: k_st[k] * jnp.exp(glast[k] - gcol[k]) for k in keys}
    qg = {k: q_st[k] * egc[k] for k in keys}

    for sc in range(nsub):
        states = [[so_ref[0, hd] for hd in heads[q]] for q in groups]
        wq_s = []
        for q in groups:
            wm = uw[(sc, q)][:, GDN_DV:]
            parts = []
            for i in range(nh):
                rs = slice(i * chunk, (i + 1) * chunk)
                parts.append(_bdot(jnp.concatenate([wm[rs], qg[(sc, q)][rs]], axis=0), states[q][i]))
            wq_s.append(parts)
        v_new, o_st = [], []
        for q in groups:
            ws = jnp.concatenate([m[:chunk] for m in wq_s[q]], axis=0) if nh > 1 else wq_s[q][0][:chunk]
            qs = jnp.concatenate([m[chunk:] for m in wq_s[q]], axis=0) if nh > 1 else wq_s[q][0][chunk:]
            vn = (uw[(sc, q)][:, :GDN_DV] - ws).astype(BF16)
            v_new.append(vn)
            o_st.append(qs + _bdot(aqk[(sc, q)], vn))
        for q in groups:
            for i, hd in enumerate(heads[q]):
                rs = slice(i * chunk, (i + 1) * chunk)
                gl_h = jnp.exp(glast[(sc, q)][i * chunk:i * chunk + 1, :])
                if chunk % 16 == 0:
                    upd = _bdot_tn(kg[(sc, q)][rs], v_new[q][rs])
                else:
                    upd = _bdot_tn(jnp.where(row_head == i, kg[(sc, q)], 0.0), v_new[q])
                so_ref[0, hd] = states[q][i] * gl_h + upd
        for q in groups:
            z_st = stack(z_ref, sc, heads[q], GDN_DV).astype(F32)
            og = (_rms(o_st[q], nw_ref[0:1, :]) * _silu(z_st)).astype(o_ref.dtype)
            for i, hd in enumerate(heads[q]):
                o_ref[sc * chunk:(sc + 1) * chunk, hd * GDN_DV:(hd + 1) * GDN_DV] = og[i * chunk:(i + 1) * chunk]


def _gdn_chunk_body_old(q_ref, k_ref, v_ref, z_ref, gc_ref, beta_ref, s0_ref, nw_ref,
                        o_ref, so_ref, *, nh, chunk):
    @pl.when(pl.program_id(1) == 0)
    def _():
        so_ref[...] = s0_ref[...]

    ng = GDN_V_H // nh
    gc = nh * chunk
    rep = GDN_V_H // GDN_QK_H
    ri = lax.broadcasted_iota(jnp.int32, (gc, gc), 0)
    ci = lax.broadcasted_iota(jnp.int32, (gc, gc), 1)
    same = (ri // chunk) == (ci // chunk)
    strict = same & ((ri % chunk) > (ci % chunk))
    incl = same & ((ri % chunk) >= (ci % chunk))
    last = same & ((ci % chunk) == chunk - 1)
    eye = (ri == ci).astype(F32)
    row_head = lax.broadcasted_iota(jnp.int32, (gc, GDN_DK), 0) // chunk

    def stack(ref, heads, width):
        parts = [ref[:, hd * width:(hd + 1) * width] for hd in heads]
        return parts[0] if len(parts) == 1 else jnp.concatenate(parts, axis=0)

    groups = range(ng)
    heads = [[q * nh + i for i in range(nh)] for q in groups]
    k_st = [stack(k_ref, [hd // rep for hd in heads[q]], GDN_DK) for q in groups]
    q_st = [stack(q_ref, [hd // rep for hd in heads[q]], GDN_DK) for q in groups]
    v_st = [stack(v_ref, heads[q], GDN_DV) for q in groups]
    def col(ref, lanes):
        parts = [ref[:, ln:ln + 1] for ln in lanes]
        return parts[0] if len(parts) == 1 else jnp.concatenate(parts, axis=0)

    gcol = [col(gc_ref, [GDN_V_H + hd for hd in heads[q]]) for q in groups]
    bcol = [col(beta_ref, heads[q]) for q in groups]
    grow = [jnp.sum(jnp.where(ri == ci, gcol[q], 0.0), axis=0, keepdims=True) for q in groups]
    k_b = [x.astype(BF16) for x in k_st]
    kq = [_bdot_nt(jnp.concatenate([k_b[q], q_st[q].astype(BF16)], axis=0), k_b[q]) for q in groups]
    decay = [jnp.exp(jnp.where(incl, gcol[q] - grow[q], -jnp.inf)) for q in groups]
    a = [jnp.where(strict, kq[q][:gc] * bcol[q] * decay[q], 0.0) for q in groups]
    aqk = [jnp.where(incl, kq[q][gc:] * decay[q], 0.0).astype(BF16) for q in groups]

    p = [(-m).astype(BF16) for m in a]
    x = [eye - m for m in a]
    span = 2
    if span < chunk:
        p = [_bdot(p[q], p[q]) for q in groups]
    while span < chunk:
        if span * 2 < chunk:
            px = [_bdot(p[q], jnp.concatenate([p[q].astype(BF16), x[q].astype(BF16)], axis=1)) for q in groups]
            p = [m[:, :gc] for m in px]
            x = [x[q] + px[q][:, gc:] for q in groups]
        else:
            x = [x[q] + _bdot(p[q], x[q]) for q in groups]
        span *= 2

    egc = [jnp.exp(g) for g in gcol]
    uw = [_bdot(x[q], jnp.concatenate([v_st[q] * bcol[q], k_st[q] * (bcol[q] * egc[q])], axis=1)) for q in groups]
    glast = [jnp.sum(jnp.where(last, grow[q], 0.0), axis=1, keepdims=True) for q in groups]
    kg = [k_st[q] * jnp.exp(glast[q] - gcol[q]) for q in groups]
    states = [[so_ref[0, hd] for hd in heads[q]] for q in groups]
    wq_s = []
    for q in groups:
        wm = uw[q][:, GDN_DV:]
        qg = q_st[q] * egc[q]
        parts = []
        for i in range(nh):
            rs = slice(i * chunk, (i + 1) * chunk)
            parts.append(_bdot(jnp.concatenate([wm[rs], qg[rs]], axis=0), states[q][i]))
        wq_s.append(parts)
    v_new, o_st = [], []
    for q in groups:
        ws = jnp.concatenate([m[:chunk] for m in wq_s[q]], axis=0) if nh > 1 else wq_s[q][0][:chunk]
        qs = jnp.concatenate([m[chunk:] for m in wq_s[q]], axis=0) if nh > 1 else wq_s[q][0][chunk:]
        vn = (uw[q][:, :GDN_DV] - ws).astype(BF16)
        v_new.append(vn)
        o_st.append(qs + _bdot(aqk[q], vn))
    for q in groups:
        for i, hd in enumerate(heads[q]):
            rs = slice(i * chunk, (i + 1) * chunk)
            gl_h = jnp.exp(glast[q][i * chunk:i * chunk + 1, :])
            if chunk % 16 == 0:
                upd = _bdot_tn(kg[q][rs], v_new[q][rs])
            else:
                upd = _bdot_tn(jnp.where(row_head == i, kg[q], 0.0), v_new[q])
            so_ref[0, hd] = states[q][i] * gl_h + upd
    for q in groups:
        z_st = stack(z_ref, heads[q], GDN_DV)
        og = _rms(o_st[q], nw_ref[0:1, :]) * _silu(z_st)
        for i, hd in enumerate(heads[q]):
            o_ref[:, hd * GDN_DV:(hd + 1) * GDN_DV] = og[i * chunk:(i + 1) * chunk]


GDN_SUBCHUNKS = 4


def _gdn_layer(x2d, conv_prev, s0, w, gi, nw, b, t):
    n = b * t
    chunk = _chunk_of(t)
    tl = _tiling(b, t, 256)
    G, J, R, P = tl["G"], tl["J"], tl["R"], tl["P"]
    adt = BF16 if chunk % 16 == 0 else F32
    w_in = w["gdn_w_in"][gi]
    o1 = GDN_CONV_DIM
    o2 = o1 + GDN_V_DIM
    wqkv = w_in[:, :o1].astype(BF16)
    wz = w_in[:, o1:o2].astype(BF16)
    wbg = jnp.pad(w_in[:, o2:], ((0, 0), (0, LANES - 2 * GDN_V_H))).astype(BF16)
    cw = jnp.pad(w["gdn_conv_w"][gi], ((0, 8 - GDN_CONV), (0, 0)))
    gvec = jnp.zeros((8, LANES), F32)
    gvec = gvec.at[0, GDN_V_H:2 * GDN_V_H].set(w["gdn_a_log"][gi]).at[1, GDN_V_H:2 * GDN_V_H].set(w["gdn_dt_bias"][gi])
    bc = chunk if chunk == 64 else R
    tri = _chunk_masks(chunk, bc)
    nwa = jnp.concatenate([nw[0:1], jnp.zeros((7, D_MODEL), F32)])
    row = lambda g, j: (g * J + j, 0)
    st_spec = pl.BlockSpec((P, GDN_CONV_DIM), lambda g, j: (g, 0))
    qn, kn, v, z, beta, gcs, st = pl.pallas_call(
        _gdn_proj_body,
        grid=(G, J),
        in_specs=[pl.BlockSpec((R, D_MODEL), row), st_spec, _const_spec((8, D_MODEL)), _const_spec(wqkv.shape),
                  _const_spec(wz.shape), _const_spec(wbg.shape), _const_spec(cw.shape), _const_spec(gvec.shape),
                  _const_spec(tri.shape)],
        out_specs=[pl.BlockSpec((R, GDN_QK_DIM), row), pl.BlockSpec((R, GDN_QK_DIM), row),
                   pl.BlockSpec((R, GDN_V_DIM), row), pl.BlockSpec((R, GDN_V_DIM), row),
                   pl.BlockSpec((R, LANES), row), pl.BlockSpec((R, LANES), row), st_spec],
        out_shape=[jax.ShapeDtypeStruct((n, GDN_QK_DIM), adt), jax.ShapeDtypeStruct((n, GDN_QK_DIM), adt),
                   jax.ShapeDtypeStruct((n, GDN_V_DIM), adt), jax.ShapeDtypeStruct((n, GDN_V_DIM), adt),
                   jax.ShapeDtypeStruct((n, LANES), F32), jax.ShapeDtypeStruct((n, LANES), F32),
                   jax.ShapeDtypeStruct((b * SUBLANES, GDN_CONV_DIM), F32)],
        scratch_shapes=[pltpu.VMEM((P, GDN_CONV_DIM), F32)],
        compiler_params=_params(2),
        name="gdn_proj",
    )(x2d, _pad_state(conv_prev), nwa, wqkv, wz, wbg, cw, gvec, tri)
    conv_new = st.reshape(b, SUBLANES, GDN_CONV_DIM)[:, SUBLANES - (GDN_CONV - 1):]

    nh = GROUP_ROWS // chunk
    ng = GDN_V_H // nh
    nsub = GDN_SUBCHUNKS if (t // chunk) % GDN_SUBCHUNKS == 0 else 1
    nct = t // (chunk * nsub)
    br = chunk * nsub
    crow = lambda bb, j: (bb * nct + j, 0)
    sspec = pl.BlockSpec((1, GDN_V_H, GDN_DK, GDN_DV), lambda bb, j: (bb, 0, 0, 0))
    nwn = jnp.concatenate([w["gdn_norm_w"][gi][None, :], jnp.zeros((7, GDN_DV), F32)])
    o, s_new = pl.pallas_call(
        functools.partial(_gdn_chunk_body, nh=nh, chunk=chunk, nsub=nsub),
        grid=(b, nct),
        in_specs=[pl.BlockSpec((br, GDN_QK_DIM), crow), pl.BlockSpec((br, GDN_QK_DIM), crow),
                  pl.BlockSpec((br, GDN_V_DIM), crow), pl.BlockSpec((br, GDN_V_DIM), crow),
                  pl.BlockSpec((br, LANES), crow), pl.BlockSpec((br, LANES), crow),
                  sspec, _const_spec((8, GDN_DV))],
        out_specs=[pl.BlockSpec((br, GDN_V_DIM), crow), sspec],
        out_shape=[jax.ShapeDtypeStruct((n, GDN_V_DIM), adt),
                   jax.ShapeDtypeStruct((b, GDN_V_H, GDN_DK, GDN_DV), F32)],
        compiler_params=_params(2),
        name="gdn_chunk",
    )(qn, kn, v, z, gcs, beta, s0, nwn)
    pre = ("plain", [o], [w["gdn_wo"][gi].astype(BF16)])
    return pre, conv_new, s_new


def _trunk(x, pos, rw_s, rw_shift, gdn_s, gdn_conv, ffn_conv, w, paged):
    b, t, _ = x.shape
    x2d = x.reshape(b * t, D_MODEL)
    new = {k: [] for k in ("rw_S", "rw_shift", "mla_c", "mla_kr", "gdn_S", "gdn_conv", "ffn_conv")}
    v_first = None
    ri = mi = gi = 0
    for l, kind in enumerate(LAYER_MIXER):
        nw = w["norm_w"][l]
        if kind == 0:
            pre, sh, s_new, v_first = _rwkv_layer(x2d, rw_shift[ri], rw_s[ri], v_first, w, ri, nw, b, t)
            new["rw_S"].append(s_new)
            new["rw_shift"].append(sh)
            ri += 1
        elif kind == 1:
            pre, c, kr = _mla_layer(x2d, pos, w, mi, nw, b, t, None if paged is None else
                                    (paged[0][mi], paged[1][mi], paged[2]))
            new["mla_c"].append(c)
            new["mla_kr"].append(kr)
            mi += 1
        else:
            pre, cb, s_new = _gdn_layer(x2d, gdn_conv[gi], gdn_s[gi], w, gi, nw, b, t)
            new["gdn_S"].append(s_new)
            new["gdn_conv"].append(cb)
            gi += 1
        nwf = jnp.concatenate([nw[2:4], nw[1:2], jnp.zeros((5, D_MODEL), F32)])
        cwb = jnp.concatenate([w["ffn_conv_w"][l], w["ffn_conv_b"][l][None, :],
                               jnp.zeros((8 - FFN_CONV - 1, 2 * D_FF), F32)])
        x2d, st = _ffn(x2d, pre, _pad_state(ffn_conv[l]), nwf, w["ffn_w_up"][l].astype(BF16), cwb,
                       w["ffn_w_down"][l].astype(BF16), b, t)
        new["ffn_conv"].append(st.reshape(b, SUBLANES, 2 * D_FF)[:, SUBLANES - (FFN_CONV - 1):])
    return x2d.reshape(b, t, D_MODEL), {k: jnp.stack(v) for k, v in new.items()}


def kernel(x_prompt, x_sample, state_rwkv_wkv, state_rwkv_shift, cache_mla_latent, cache_mla_krope, state_gdn_S, state_gdn_conv, state_ffn_conv, page_table, norm_w, rw_mu, rw_wrkv, rw_w0, rw_w1, rw_w2, rw_a0, rw_a1, rw_a2, rw_v0, rw_v1, rw_v2, rw_g1, rw_g2, rw_kk, rw_ka, rw_rk, rw_lnx_w, rw_lnx_b, rw_wo, mla_w_in, mla_q_norm, mla_kv_norm, mla_w_qb, mla_w_uk, mla_w_uv, mla_wo, gdn_w_in, gdn_conv_w, gdn_a_log, gdn_dt_bias, gdn_norm_w, gdn_wo, ffn_w_up, ffn_conv_w, ffn_conv_b, ffn_w_down):
    w = dict(norm_w=norm_w, rw_mu=rw_mu, rw_wrkv=rw_wrkv, rw_w0=rw_w0, rw_w1=rw_w1, rw_w2=rw_w2, rw_a0=rw_a0,
             rw_a1=rw_a1, rw_a2=rw_a2, rw_v0=rw_v0, rw_v1=rw_v1, rw_v2=rw_v2, rw_g1=rw_g1, rw_g2=rw_g2,
             rw_kk=rw_kk, rw_ka=rw_ka, rw_rk=rw_rk, rw_lnx_w=rw_lnx_w, rw_lnx_b=rw_lnx_b, rw_wo=rw_wo,
             mla_w_in=mla_w_in, mla_q_norm=mla_q_norm, mla_kv_norm=mla_kv_norm, mla_w_qb=mla_w_qb,
             mla_w_uk=mla_w_uk, mla_w_uv=mla_w_uv, mla_wo=mla_wo, gdn_w_in=gdn_w_in, gdn_conv_w=gdn_conv_w,
             gdn_a_log=gdn_a_log, gdn_dt_bias=gdn_dt_bias, gdn_norm_w=gdn_norm_w, gdn_wo=gdn_wo,
             ffn_w_up=ffn_w_up, ffn_conv_w=ffn_conv_w, ffn_conv_b=ffn_conv_b, ffn_w_down=ffn_w_down)
    b, t = x_prompt.shape[0], x_prompt.shape[1]
    n_rw, n_gdn, depth = state_rwkv_wkv.shape[0], state_gdn_S.shape[0], state_ffn_conv.shape[0]
    y_p, sp = _trunk(
        x_prompt, jnp.arange(t),
        jnp.zeros((n_rw, b) + state_rwkv_wkv.shape[2:], F32), jnp.zeros((n_rw, b, D_MODEL), F32),
        jnp.zeros((n_gdn, b) + state_gdn_S.shape[2:], F32), jnp.zeros((n_gdn, b) + state_gdn_conv.shape[2:], F32),
        jnp.zeros((depth, b) + state_ffn_conv.shape[2:], F32), w, None)
    past_len = page_table.shape[1] * cache_mla_latent.shape[2]
    pos_s = past_len + jnp.arange(x_sample.shape[1])
    y_s, ss = _trunk(x_sample, pos_s, state_rwkv_wkv, state_rwkv_shift, state_gdn_S, state_gdn_conv,
                     state_ffn_conv, w, (cache_mla_latent, cache_mla_krope, page_table))
    names = ("rw_S", "rw_shift", "mla_c", "mla_kr", "gdn_S", "gdn_conv", "ffn_conv")
    return (y_p, y_s) + tuple(sp[k] for k in names) + tuple(ss[k] for k in names)
```

```python
import functools

import jax
import jax.numpy as jnp
from jax import lax
from jax.experimental import pallas as pl
from jax.experimental.pallas import tpu as pltpu

F32 = jnp.float32
BF16 = jnp.bfloat16
HIGHEST = lax.Precision.HIGHEST

D_MODEL = 1024
NORM_EPS = 1e-6
RW_N = 64
RW_H = D_MODEL // RW_N
RW_LNX_EPS = 64e-5
MLA_H = 16
MLA_NOPE = 64
MLA_ROPE = 32
MLA_V = 64
MLA_Q_LORA = 512
MLA_KV_LORA = 256
MLA_SCALE = (MLA_NOPE + MLA_ROPE) ** -0.5
ROPE_THETA = 10000.0
MLA_QK = MLA_KV_LORA + 128
GDN_QK_H = 8
GDN_V_H = 16
GDN_DK = 128
GDN_DV = 128
GDN_QK_DIM = GDN_QK_H * GDN_DK
GDN_V_DIM = GDN_V_H * GDN_DV
GDN_CONV_DIM = 2 * GDN_QK_DIM + GDN_V_DIM
GDN_CONV = 4
D_FF = 2816
FFN_CONV = 3
LAYER_MIXER = (0, 1, 2, 0)

SUBLANES = 8
LANES = 128
GROUP_ROWS = 128
VMEM_LIMIT = 56 * 1024 * 1024


def _rms(x, w):
    return x * lax.rsqrt(jnp.mean(x * x, axis=-1, keepdims=True) + NORM_EPS) * w


def _bdot(a, b):
    return jnp.dot(a.astype(BF16), b.astype(BF16), preferred_element_type=F32)


def _bdot_nt(a, b):
    return lax.dot_general(a.astype(BF16), b.astype(BF16), (((1,), (1,)), ((), ())),
                           preferred_element_type=F32)


def _bdot_tn(a, b):
    return lax.dot_general(a.astype(BF16), b.astype(BF16), (((0,), (0,)), ((), ())),
                           preferred_element_type=F32)


def _hdot(a, b):
    return jnp.dot(a, b, precision=HIGHEST, preferred_element_type=F32)


def _sigmoid(x):
    return 1.0 / (1.0 + jnp.exp(-x))


def _softplus(x):
    return jnp.maximum(x, 0.0) + jnp.log(1.0 + jnp.exp(-jnp.abs(x)))


def _silu(x):
    return x * _sigmoid(x)


def _shift_rows(u, prev, s):
    rows, cols = u.shape
    p = prev.shape[0]
    rolled = pltpu.roll(u, s, 0)
    fix = pltpu.roll(prev, (p - SUBLANES + s) % p, 0)
    t = lax.broadcasted_iota(jnp.int32, (p, cols), 0) % SUBLANES
    if p == rows:
        return jnp.where(t < s, fix, rolled)
    head = jnp.where(t < s, fix, rolled[:SUBLANES])
    return jnp.concatenate([head, rolled[SUBLANES:]], axis=0)


def _lane_group_sum(x, ones2):
    parts = []
    for i in range(x.shape[1] // LANES):
        xs = x[:, i * LANES:(i + 1) * LANES]
        hi = xs.astype(BF16)
        lo = (xs - hi.astype(F32)).astype(BF16)
        parts.append(jnp.dot(jnp.concatenate([hi, lo], axis=1), ones2, preferred_element_type=F32))
    return parts[0] if len(parts) == 1 else jnp.concatenate(parts, axis=1)


def _split_dot(m2, x):
    hi = x.astype(BF16)
    lo = (x - hi.astype(F32)).astype(BF16)
    return jnp.dot(m2, jnp.concatenate([hi, lo], axis=0), preferred_element_type=F32)


def _chunk_cumsum(x, tri):
    bc = tri.shape[0]
    parts = [_hdot(tri, x[i * bc:(i + 1) * bc]) for i in range(x.shape[0] // bc)]
    return parts[0] if len(parts) == 1 else jnp.concatenate(parts, axis=0)


def _tiling(b, t, tt_max):
    if t == SUBLANES:
        return dict(G=1, J=1, R=b * t, P=b * t)
    tt = min(t, tt_max)
    assert t % tt == 0 and tt % 64 == 0, (t, tt)
    return dict(G=b, J=t // tt, R=tt, P=SUBLANES)


def _chunk_of(t):
    return 64 if t % 64 == 0 else t


def _const_spec(shape):
    nd = len(shape)
    return pl.BlockSpec(shape, lambda *_: (0,) * nd, pipeline_mode=pl.Buffered(1))


def _params(n_axes):
    return pltpu.CompilerParams(dimension_semantics=("arbitrary",) * n_axes,
                                vmem_limit_bytes=VMEM_LIMIT)


def _pad_state(st):
    b, k1, c = st.shape
    return jnp.pad(st, ((0, 0), (SUBLANES - k1, 0), (0, 0))).reshape(b * SUBLANES, c)


def _chunk_masks(chunk, rows):
    i = jnp.arange(rows)
    same = (i[:, None] // chunk) == (i[None, :] // chunk)
    tri = same & ((i[None, :] % chunk) <= (i[:, None] % chunk))
    return tri.astype(F32)


FFN_CW = 256


ROW_BLOCK = 64


def _stage(buf_ref, u, carry_ref, st_ref, sl, taps):
    rows = u.shape[0]
    hb = taps * SUBLANES
    buf_ref[hb:hb + rows, :] = u
    first = lax.broadcasted_iota(jnp.int32, (SUBLANES, u.shape[1]), 0) == 0
    for i in range(taps):
        back = taps - i
        src = u[rows - back * SUBLANES:rows - (back - 1) * SUBLANES, :]
        crow = carry_ref[SUBLANES - back:SUBLANES - back + 1, sl]
        buf_ref[i * SUBLANES:(i + 1) * SUBLANES, :] = jnp.where(first, crow, pltpu.roll(src, 1, 0))
    sq = SUBLANES * SUBLANES
    tail = pltpu.einshape("(vs)d->(sv)d", u[rows - sq:, :], s=SUBLANES)[sq - SUBLANES:, :]
    carry_ref[:, sl] = tail
    st_ref[:, sl] = tail


def _taps(buf_ref, r0, nrows, taps):
    hb = taps * SUBLANES
    cur = buf_ref[hb + r0:hb + r0 + nrows, :]
    return cur, [buf_ref[hb - j * SUBLANES + r0:hb - j * SUBLANES + r0 + nrows, :] for j in range(taps, 0, -1)]


FFN_PARTS = 1


def _mixer_out(kind, refs, rs):
    if kind == "mla":
        o_ref, wuv_ref, wo_ref = refs
        parts = []
        for pr in range(MLA_H // 2):
            wp = wuv_ref[pr]
            parts.append(jnp.dot(o_ref[2 * pr, rs, :].astype(BF16), wp[:MLA_KV_LORA], preferred_element_type=F32)
                         + jnp.dot(o_ref[2 * pr + 1, rs, :].astype(BF16), wp[MLA_KV_LORA:],
                                   preferred_element_type=F32))
        y = jnp.concatenate(parts, axis=1)
    elif kind == "gated":
        y_ref, g_ref, wo_ref = refs
        y = y_ref[rs, :].astype(F32) * g_ref[rs, :].astype(F32)
    else:
        y_ref, wo_ref = refs
        y = y_ref[rs, :]
    return jnp.dot(y.astype(BF16), wo_ref[...], preferred_element_type=F32)


def _ffn_body(*refs, kind, n_pre):
    x_ref = refs[0]
    pre_refs = refs[1:1 + n_pre]
    (prev_ref, nw_ref, wup_ref, cwb_ref, wdn_ref, xo_ref, st_ref, carry_ref, act_ref, buf_ref) = refs[1 + n_pre:]

    @pl.when(pl.program_id(1) == 0)
    def _():
        carry_ref[...] = prev_ref[...]

    rows = x_ref.shape[0]
    stacked = carry_ref.shape[0] == rows
    nparts = 1 if stacked else FFN_PARTS
    prows = rows // nparts
    parts = [slice(i * prows, (i + 1) * prows) for i in range(nparts)]
    nch = D_FF // FFN_CW

    def cols(c, half):
        return slice(half * D_FF + c * FFN_CW, half * D_FF + (c + 1) * FFN_CW)

    def conv_gate(c, taps):
        ys = []
        for half in range(2):
            sl = cols(c, half)
            u, (u2, u1) = taps[half]
            ys.append(cwb_ref[0:1, sl] * u2 + cwb_ref[1:2, sl] * u1 + cwb_ref[2:3, sl] * u + cwb_ref[3:4, sl])
        return (_silu(ys[0]) * ys[1]).astype(BF16)

    xs = [x_ref[rs, :] + _rms(_mixer_out(kind, pre_refs, rs), nw_ref[2:3, :]) for rs in parts]
    for pi, rs in enumerate(parts):
        x = xs[pi] if stacked else pltpu.einshape("(sv)d->(vs)d", xs[pi], s=SUBLANES)
        h = _rms(x, nw_ref[0:1, :]).astype(BF16)

        def up(c):
            us = [jnp.dot(h, wup_ref[:, cols(c, half)], preferred_element_type=F32) for half in range(2)]
            if stacked:
                return us
            for half in range(2):
                _stage(buf_ref.at[(c % 2) * 2 + half], us[half], carry_ref, st_ref, cols(c, half), FFN_CONV - 1)
            return None

        u_cur = up(0)
        for c in range(nch):
            u_nxt = up(c + 1) if c + 1 < nch else None
            csl = slice(c * FFN_CW, (c + 1) * FFN_CW)
            if stacked:
                taps = []
                for half in range(2):
                    sl = cols(c, half)
                    u = u_cur[half]
                    prev = carry_ref[:, sl]
                    taps.append((u, [_shift_rows(u, prev, 2), _shift_rows(u, prev, 1)]))
                    carry_ref[:, sl] = u
                    st_ref[:, sl] = u
                act_ref[:, csl] = conv_gate(c, taps)
            else:
                for r0 in range(0, prows, ROW_BLOCK):
                    taps = [_taps(buf_ref.at[(c % 2) * 2 + half], r0, ROW_BLOCK, FFN_CONV - 1) for half in range(2)]
                    act_ref[rs.start + r0:rs.start + r0 + ROW_BLOCK, csl] = conv_gate(c, taps)
            u_cur = u_nxt
        f = jnp.dot(act_ref[rs, :], wdn_ref[...], preferred_element_type=F32)
        out = x + _rms(f, nw_ref[1:2, :])
        xo_ref[rs, :] = out if stacked else pltpu.einshape("(vs)d->(sv)d", out, s=SUBLANES)


def _ffn(x2d, pre, prev, nw, wup, cwb, wdn, b, t):
    tl = _tiling(b, t, 512)
    G, J, R, P = tl["G"], tl["J"], tl["R"], tl["P"]
    n = b * t
    kind, acts, wts = pre
    row = lambda g, j: (g * J + j, 0)
    if kind == "mla":
        act_specs = [pl.BlockSpec((MLA_H, R, MLA_KV_LORA), lambda g, j: (0, g * J + j, 0))]
    else:
        act_specs = [pl.BlockSpec((R, a.shape[1]), row) for a in acts]
    pre_specs = act_specs + [_const_spec(wt.shape) for wt in wts]
    return pl.pallas_call(
        functools.partial(_ffn_body, kind=kind, n_pre=len(pre_specs)),
        grid=(G, J),
        in_specs=[pl.BlockSpec((R, D_MODEL), row)] + pre_specs + [
            pl.BlockSpec((P, 2 * D_FF), lambda g, j: (g, 0)),
            _const_spec((8, D_MODEL)),
            _const_spec((D_MODEL, 2 * D_FF)),
            _const_spec((8, 2 * D_FF)),
            _const_spec((D_FF, D_MODEL)),
        ],
        out_specs=[
            pl.BlockSpec((R, D_MODEL), lambda g, j: (g * J + j, 0)),
            pl.BlockSpec((P, 2 * D_FF), lambda g, j: (g, 0)),
        ],
        out_shape=[jax.ShapeDtypeStruct((n, D_MODEL), F32),
                   jax.ShapeDtypeStruct((b * SUBLANES, 2 * D_FF), F32)],
        scratch_shapes=[pltpu.VMEM((P, 2 * D_FF), F32), pltpu.VMEM((R, D_FF), BF16),
                        pltpu.VMEM((4, R + (FFN_CONV - 1) * SUBLANES, FFN_CW), F32)],
        compiler_params=_params(2),
        name="conv_ffn",
    )(x2d, *acts, *wts, prev, nw, wup, cwb, wdn)


def _rwkv_proj_body(*refs, has_vres, chunk):
    it = iter(refs)
    x_ref, prev_ref = next(it), next(it)
    vf_ref = next(it) if has_vres else None
    vec_ref, wrkv_ref, w1_ref, w2_ref, a1_ref, a2_ref = (next(it) for _ in range(6))
    v1_ref, v2_ref = (next(it), next(it)) if has_vres else (None, None)
    g1_ref, g2_ref, tri_ref, ones_ref = (next(it) for _ in range(4))
    rt_ref, kt_ref, at_ref, bt_ref, v_ref, g_ref, gl_ref, hl_ref, carry_ref = (next(it) for _ in range(9))

    @pl.when(pl.program_id(1) == 0)
    def _():
        carry_ref[...] = prev_ref[...]

    x = x_ref[...]
    rows = x.shape[0]
    p = carry_ref.shape[0]
    h = _rms(x, vec_ref[10:11, :])
    d = _shift_rows(h, carry_ref[...], 1) - h
    tail = h[rows - p:, :]
    carry_ref[...] = tail
    hl_ref[...] = tail

    def mix(i):
        return (h + d * vec_ref[i:i + 1, :]).astype(BF16)

    r = jnp.dot(mix(0), wrkv_ref[0], preferred_element_type=F32)
    k = jnp.dot(mix(1), wrkv_ref[1], preferred_element_type=F32)
    xv = mix(2)
    v = jnp.dot(xv, wrkv_ref[2], preferred_element_type=F32)
    w_lora = _bdot(jnp.tanh(_bdot(mix(3), w1_ref[...])), w2_ref[...])
    v_lora = _bdot(_bdot(xv, v1_ref[...]), v2_ref[...]) if has_vres else None
    a_lora = _bdot(_bdot(mix(4), a1_ref[...]), a2_ref[...])
    g_ref[...] = _bdot(_sigmoid(_bdot(mix(5), g1_ref[...])), g2_ref[...]).astype(g_ref.dtype)
    adt = rt_ref.dtype

    bc = tri_ref.shape[0]
    for r0 in range(0, rows, bc):
        for l0 in range(0, D_MODEL, PROJ_LANES):
            rs, ls = slice(r0, r0 + bc), slice(l0, l0 + PROJ_LANES)
            vb = v[rs, ls]
            if has_vres:
                vb = vb + (vf_ref[rs, ls] - vb) * _sigmoid(vec_ref[11:12, ls] + v_lora[rs, ls])
            v_ref[rs, ls] = vb.astype(adt)
            a = _sigmoid(vec_ref[7:8, ls] + a_lora[rs, ls])
            kb = k[rs, ls]
            kk = kb * vec_ref[8:9, ls]
            kk = kk * lax.rsqrt(_lane_group_sum(kk * kk, ones_ref[...]) + 1e-6)
            kb = kb * (1.0 + (a - 1.0) * vec_ref[9:10, ls])
            w = -_softplus(-(vec_ref[6:7, ls] + w_lora[rs, ls])) - 0.5
            lw = -jnp.exp(w)
            cum = _split_dot(tri_ref[...], lw)
            e_bwd = jnp.exp(-cum)
            rt_ref[rs, ls] = (r[rs, ls] * jnp.exp(cum)).astype(adt)
            kt_ref[rs, ls] = (kb * e_bwd).astype(adt)
            at_ref[rs, ls] = (-kk * jnp.exp(cum - lw)).astype(adt)
            bt_ref[rs, ls] = (kk * a * e_bwd).astype(adt)
            for c in range(bc // chunk):
                row = (c + 1) * chunk - 1
                gl_ref[r0 // chunk + c, :, ls] = jnp.exp(cum[row:row + 1, :])


def _rwkv_scan_body(rt_ref, kt_ref, at_ref, bt_ref, v_ref, gl_ref, h0_ref, vec_ref, y_ref, ho_ref,
                    *, nh, chunk, nsub, per_seq):
    @pl.when(pl.program_id(1) == 0)
    def _():
        ho_ref[...] = h0_ref[...]

    gl_lanes = nh * RW_N
    ng = RW_H // nh
    gc = nh * chunk
    row_head = lax.broadcasted_iota(jnp.int32, (gc, gl_lanes), 0) // chunk
    lane_head = lax.broadcasted_iota(jnp.int32, (gc, gl_lanes), 1) // RW_N
    own = row_head == lane_head
    ri = lax.broadcasted_iota(jnp.int32, (gc, gc), 0)
    ci = lax.broadcasted_iota(jnp.int32, (gc, gc), 1)
    same = (ri // chunk) == (ci // chunk)
    strict = same & ((ri % chunk) > (ci % chunk))
    incl = same & ((ri % chunk) >= (ci % chunk))
    eye = (ri == ci).astype(F32)
    eye_l = (lax.broadcasted_iota(jnp.int32, (gl_lanes, gl_lanes), 0)
             == lax.broadcasted_iota(jnp.int32, (gl_lanes, gl_lanes), 1))
    merged = gc == GROUP_ROWS
    groups = range(ng)
    sls = [slice(q * gl_lanes, (q + 1) * gl_lanes) for q in groups]
    keys = [(sc, q) for sc in range(nsub) for q in groups]

    def blockdiag(ref, key):
        xg = ref[key[0] * chunk:(key[0] + 1) * chunk, sls[key[1]]]
        xx = jnp.concatenate([xg] * nh, axis=0) if nh > 1 else xg
        return jnp.where(own, xx, jnp.zeros_like(xx))

    r_bd = {k: blockdiag(rt_ref, k) for k in keys}
    k_bd = {k: blockdiag(kt_ref, k) for k in keys}
    a_bd = {k: blockdiag(at_ref, k) for k in keys}
    b_bd = {k: blockdiag(bt_ref, k) for k in keys}
    v_f = {k: blockdiag(v_ref, k) for k in keys}
    v_bd = {k: v_f[k].astype(BF16) for k in keys}
    bonus = {k: jnp.sum(r_bd[k].astype(F32) * k_bd[k] * vec_ref[2:3, sls[k[1]]], axis=1, keepdims=True)
             for k in keys}
    if merged:
        ar = {k: jnp.concatenate([a_bd[k], r_bd[k]], axis=0).astype(BF16) for k in keys}
        bk = {k: jnp.concatenate([b_bd[k], k_bd[k]], axis=0).astype(BF16) for k in keys}
        amat = {k: _bdot_nt(ar[k], bk[k]) for k in keys}
        a_ab = {k: jnp.where(strict, amat[k][:gc, :gc], 0.0) for k in keys}
        a_ak = {k: jnp.where(strict, amat[k][:gc, gc:], 0.0).astype(BF16) for k in keys}
        a_rbk = {k: jnp.concatenate([jnp.where(incl, amat[k][gc:, :gc], 0.0),
                                     jnp.where(incl, amat[k][gc:, gc:], 0.0)], axis=1).astype(BF16) for k in keys}
    else:
        ab_ = {k: a_bd[k].astype(BF16) for k in keys}
        rb_ = {k: r_bd[k].astype(BF16) for k in keys}
        bb_ = {k: b_bd[k].astype(BF16) for k in keys}
        kb_ = {k: k_bd[k].astype(BF16) for k in keys}
        a_ab = {k: jnp.where(strict, _bdot_nt(ab_[k], bb_[k]), 0.0) for k in keys}
        a_ak = {k: jnp.where(strict, _bdot_nt(ab_[k], kb_[k]), 0.0).astype(BF16) for k in keys}
        a_rb = {k: jnp.where(incl, _bdot_nt(rb_[k], bb_[k]), 0.0).astype(BF16) for k in keys}
        a_rk = {k: jnp.where(incl, _bdot_nt(rb_[k], kb_[k]), 0.0).astype(BF16) for k in keys}
    akv = {k: _bdot(a_ak[k], v_bd[k]) for k in keys}

    p = {k: a_ab[k].astype(BF16) for k in keys}
    x = {k: eye + a_ab[k] for k in keys}
    span = 2
    if span < chunk:
        p = {k: _bdot(p[k], p[k]) for k in keys}
    while span < chunk:
        last = span * 2 >= chunk
        if merged and not last:
            px = {k: _bdot(p[k], jnp.concatenate([p[k].astype(BF16), x[k].astype(BF16)], axis=1)) for k in keys}
            p = {k: px[k][:, :gc] for k in keys}
            x = {k: x[k] + px[k][:, gc:] for k in keys}
        else:
            pb = {k: p[k].astype(BF16) for k in keys}
            x = {k: x[k] + _bdot(pb[k], x[k]) for k in keys}
            if not last:
                p = {k: _bdot(pb[k], pb[k]) for k in keys}
        span *= 2
    tinv = {k: x[k].astype(BF16) for k in keys}

    for sc in range(nsub):
        slot = sc if per_seq else 0
        hs = [ho_ref[slot, q] for q in groups]
        hs_b = [h.astype(BF16) for h in hs]
        gl_rows = [gl_ref[sc, :, sl] for sl in sls]
        if merged:
            arh = [_bdot(ar[(sc, q)], hs_b[q]) for q in groups]
            u = [_bdot(tinv[(sc, q)], arh[q][:gc] + akv[(sc, q)]).astype(BF16) for q in groups]
            uv = [jnp.concatenate([u[q], v_bd[(sc, q)]], axis=0) for q in groups]
            y_bd = [arh[q][gc:] + _bdot(a_rbk[(sc, q)], uv[q]) for q in groups]
            for q in groups:
                gl_col = jnp.sum(jnp.where(eye_l, gl_rows[q], 0.0), axis=1, keepdims=True)
                bk_g = jnp.concatenate([b_bd[(sc, q)] * gl_rows[q], k_bd[(sc, q)] * gl_rows[q]], axis=0)
                ho_ref[slot, q] = hs[q] * gl_col + _bdot_tn(bk_g, uv[q])
        else:
            ah = [_bdot(ab_[(sc, q)], hs_b[q]) for q in groups]
            rh = [_bdot(rb_[(sc, q)], hs_b[q]) for q in groups]
            u = [_bdot(tinv[(sc, q)], ah[q] + akv[(sc, q)]).astype(BF16) for q in groups]
            y_bd = [rh[q] + _bdot(a_rb[(sc, q)], u[q]) + _bdot(a_rk[(sc, q)], v_bd[(sc, q)]) for q in groups]
            for q in groups:
                gl_col = jnp.sum(jnp.where(eye_l, gl_rows[q], 0.0), axis=1, keepdims=True)
                ho_ref[slot, q] = (hs[q] * gl_col + _bdot_tn(b_bd[(sc, q)] * gl_rows[q], u[q])
                                   + _bdot_tn(k_bd[(sc, q)] * gl_rows[q], v_bd[(sc, q)]))

        for q in groups:
            sl = sls[q]
            mu = jnp.sum(y_bd[q], axis=1, keepdims=True) * (1.0 / RW_N)
            yc = jnp.where(own, y_bd[q] - mu, 0.0)
            var = jnp.sum(yc * yc, axis=1, keepdims=True) * (1.0 / RW_N)
            tot = (yc * lax.rsqrt(var + RW_LNX_EPS) * vec_ref[0:1, sl] + jnp.where(own, vec_ref[1:2, sl], 0.0)
                   + bonus[(sc, q)] * v_f[(sc, q)])
            y = tot[0:chunk]
            for hh in range(1, nh):
                y = y + tot[hh * chunk:(hh + 1) * chunk]
            y_ref[sc * chunk:(sc + 1) * chunk, sl] = y.astype(y_ref.dtype)


RWKV_SUBCHUNKS = 4
PROJ_LANES = 256


def _rwkv_layer(x2d, shift_prev, s0, v_first, w, ri, nw, b, t):
    n = b * t
    chunk = _chunk_of(t)
    tl = _tiling(b, t, 512)
    G, J, R, P = tl["G"], tl["J"], tl["R"], tl["P"]
    has_vres = v_first is not None
    vi = ri - 1
    adt = BF16 if chunk % 16 == 0 else F32
    bc = chunk if chunk == 64 else R
    tri = _chunk_masks(chunk, bc).astype(BF16)
    tri = jnp.concatenate([tri, tri], axis=1)
    li = jnp.arange(LANES)
    ones_bd = ((li[:, None] // RW_N) == (li[None, :] // RW_N)).astype(BF16)
    ones_bd = jnp.concatenate([ones_bd, ones_bd], axis=0)
    zero = jnp.zeros((D_MODEL,), F32)
    vec = jnp.stack([*(w["rw_mu"][ri][i] for i in range(6)), w["rw_w0"][ri], w["rw_a0"][ri], w["rw_kk"][ri],
                     w["rw_ka"][ri], nw[0], w["rw_v0"][vi] if has_vres else zero, zero, zero, zero, zero])
    row = lambda g, j: (g * J + j, 0)
    row_spec = pl.BlockSpec((R, D_MODEL), row)
    ins = [x2d, _pad_state(shift_prev[:, None, :])]
    specs = [row_spec, pl.BlockSpec((P, D_MODEL), lambda g, j: (g, 0))]
    if has_vres:
        ins.append(v_first)
        specs.append(row_spec)
    wl = [vec, w["rw_wrkv"][ri].astype(BF16), w["rw_w1"][ri].astype(BF16), w["rw_w2"][ri].astype(BF16),
          w["rw_a1"][ri].astype(BF16), w["rw_a2"][ri].astype(BF16)]
    if has_vres:
        wl += [w["rw_v1"][vi].astype(BF16), w["rw_v2"][vi].astype(BF16)]
    wl += [w["rw_g1"][ri].astype(BF16), w["rw_g2"][ri].astype(BF16), tri, ones_bd]
    ins += wl
    specs += [_const_spec(a.shape) for a in wl]
    nc_tile = R // chunk
    outs = pl.pallas_call(
        functools.partial(_rwkv_proj_body, has_vres=has_vres, chunk=chunk),
        grid=(G, J),
        in_specs=specs,
        out_specs=[row_spec] * 6 + [pl.BlockSpec((nc_tile, 1, D_MODEL), lambda g, j: (g * J + j, 0, 0)),
                                    pl.BlockSpec((P, D_MODEL), lambda g, j: (g, 0))],
        out_shape=[jax.ShapeDtypeStruct((n, D_MODEL), adt)] * 6
        + [jax.ShapeDtypeStruct((n // chunk, 1, D_MODEL), F32), jax.ShapeDtypeStruct((b * SUBLANES, D_MODEL), F32)],
        scratch_shapes=[pltpu.VMEM((P, D_MODEL), F32)],
        compiler_params=_params(2),
        name="rwkv_proj",
    )(*ins)
    rt, kt, at, bt, v, g, gl, hl = outs
    shift_new = hl.reshape(b, SUBLANES, D_MODEL)[:, -1]

    nh = LANES // RW_N
    ng = RW_H // nh
    gl_lanes = nh * RW_N
    hkv = jnp.swapaxes(s0, -1, -2).reshape(b, ng, nh, RW_N, RW_N)
    zblk = jnp.zeros((b, ng, RW_N, RW_N), F32)
    h0 = jnp.concatenate(
        [jnp.concatenate([hkv[:, :, i] if i == jj else zblk for jj in range(nh)], axis=-1) for i in range(nh)],
        axis=-2)
    svec = jnp.stack([w["rw_lnx_w"][ri], w["rw_lnx_b"][ri], w["rw_rk"][ri].reshape(D_MODEL),
                      zero, zero, zero, zero, zero])
    per_seq = t == chunk and b % RWKV_SUBCHUNKS == 0
    nsub = RWKV_SUBCHUNKS if per_seq or (t // chunk) % RWKV_SUBCHUNKS == 0 else 1
    nct = 1 if per_seq else t // (chunk * nsub)
    nseq = nsub if per_seq else 1
    crow = lambda bb, j: (bb * nct + j, 0)
    cspec = pl.BlockSpec((chunk * nsub, D_MODEL), crow)
    hspec = pl.BlockSpec((nseq, ng, gl_lanes, gl_lanes), lambda bb, j: (bb, 0, 0, 0))
    y, hout = pl.pallas_call(
        functools.partial(_rwkv_scan_body, nh=nh, chunk=chunk, nsub=nsub, per_seq=per_seq),
        grid=(b // nseq, nct),
        in_specs=[cspec] * 5 + [pl.BlockSpec((nsub, 1, D_MODEL), lambda bb, j: (bb * nct + j, 0, 0)), hspec,
                                _const_spec((8, D_MODEL))],
        out_specs=[cspec, hspec],
        out_shape=[jax.ShapeDtypeStruct((n, D_MODEL), adt),
                   jax.ShapeDtypeStruct((b, ng, gl_lanes, gl_lanes), F32)],
        compiler_params=_params(2),
        name="rwkv_scan",
    )(rt, kt, at, bt, v, gl, h0, svec)
    s_new = jnp.stack([hout[:, :, i * RW_N:(i + 1) * RW_N, i * RW_N:(i + 1) * RW_N] for i in range(nh)],
                      axis=2)
    s_new = jnp.swapaxes(s_new, -1, -2).reshape(b, RW_H, RW_N, RW_N)
    pre = ("gated", [y, g], [w["rw_wo"][ri].astype(BF16)])
    return pre, shift_new, s_new, (v if not has_vres else v_first)


def _rope_lanes(x, tab_ref):
    half = MLA_ROPE // 2
    return (x * tab_ref[0] + pltpu.roll(x, LANES - half, 1) * tab_ref[1] + pltpu.roll(x, half, 1) * tab_ref[2])


def _mla_proj_body(x_ref, nw_ref, tab_ref, winq_ref, winc_ref, wink_ref, qn_ref, kvn_ref, wqn_ref, wqr_ref,
                   wuk_ref, c_ref, kr_ref, kcat_ref, qcat_ref):
    h = _rms(x_ref[...], nw_ref[0:1, :]).astype(BF16)
    cq = _rms(jnp.dot(h, winq_ref[...], preferred_element_type=F32), qn_ref[...]).astype(BF16)
    c = _rms(jnp.dot(h, winc_ref[...], preferred_element_type=F32), kvn_ref[...])
    kr = _rope_lanes(jnp.dot(h, wink_ref[...], preferred_element_type=F32), tab_ref)
    c_ref[...] = c
    kr_ref[...] = kr
    adt = kcat_ref.dtype
    kcat_ref[:, 0:MLA_KV_LORA] = c.astype(adt)
    kcat_ref[:, MLA_KV_LORA:MLA_QK] = kr.astype(adt)
    qn = jnp.dot(cq, wqn_ref[...], preferred_element_type=F32).astype(BF16)
    qr = jnp.dot(cq, wqr_ref[...], preferred_element_type=F32)
    for pr in range(MLA_H // 2):
        ql = jnp.dot(qn[:, pr * LANES:(pr + 1) * LANES], wuk_ref[pr], preferred_element_type=F32) * MLA_SCALE
        qcat_ref[2 * pr, :, 0:MLA_KV_LORA] = ql[:, :MLA_KV_LORA].astype(adt)
        qcat_ref[2 * pr + 1, :, 0:MLA_KV_LORA] = ql[:, MLA_KV_LORA:].astype(adt)
    for hh in range(MLA_H):
        qro = _rope_lanes(qr[:, hh * LANES:(hh + 1) * LANES], tab_ref) * MLA_SCALE
        qcat_ref[hh, :, MLA_KV_LORA:MLA_QK] = qro.astype(adt)


MLA_TQ = 256
MLA_TK = 256
MLA_SPLIT = 16


def _mla_prompt_body(q_ref, k_ref, o_ref, m_ref, l_ref, acc_ref):
    i = pl.program_id(1)
    rows = MLA_H * MLA_TQ
    q = q_ref[...].reshape(rows, MLA_QK)
    m_ref[...] = jnp.full((rows, LANES), -jnp.inf, F32)
    l_ref[...] = jnp.zeros((rows, LANES), F32)
    acc_ref[...] = jnp.zeros((rows, MLA_KV_LORA), F32)
    ones = jnp.ones((MLA_TK, LANES), BF16)
    reps = MLA_TK // LANES

    sub = rows // MLA_SPLIT

    def block(k0, masked):
        kblk = k_ref[pl.ds(k0, MLA_TK), :]
        ss = [lax.dot_general(q[g * sub:(g + 1) * sub], kblk, (((1,), (1,)), ((), ())),
                              preferred_element_type=F32) for g in range(MLA_SPLIT)]
        def causal(g):
            qpos = i * MLA_TQ + (g * sub + lax.broadcasted_iota(jnp.int32, (sub, MLA_TK), 0)) % MLA_TQ
            kpos = k0 + lax.broadcasted_iota(jnp.int32, (sub, MLA_TK), 1)
            return kpos <= qpos

        shared = causal(0) if masked and sub % MLA_TQ == 0 else None
        for g in range(MLA_SPLIT):
            rs = slice(g * sub, (g + 1) * sub)
            s = ss[g]
            if masked:
                s = jnp.where(shared if shared is not None else causal(g), s, -jnp.inf)
            m_old = m_ref[rs, :]
            m_new = jnp.maximum(m_old, jnp.max(s, axis=1, keepdims=True))
            alpha = jnp.exp(m_old - m_new)
            pf = jnp.exp(s - jnp.concatenate([m_new] * reps, axis=1))
            l_ref[rs, :] = l_ref[rs, :] * alpha + jnp.sum(pf, axis=1, keepdims=True)
            acc_ref[rs, :] = (acc_ref[rs, :] * jnp.concatenate([alpha] * (MLA_KV_LORA // LANES), axis=1)
                              + jnp.dot(pf.astype(BF16), kblk[:, :MLA_KV_LORA], preferred_element_type=F32))
            m_ref[rs, :] = m_new

    def full_step(kb, carry):
        block(pl.multiple_of(kb * MLA_TK, MLA_TK), False)
        return carry

    n_full = (i * MLA_TQ) // MLA_TK
    lax.fori_loop(0, n_full, full_step, 0)
    block(pl.multiple_of(n_full * MLA_TK, MLA_TK), True)
    o = acc_ref[...] / jnp.concatenate([l_ref[...]] * (MLA_KV_LORA // LANES), axis=1)
    o_ref[...] = o.reshape(MLA_H, MLA_TQ, MLA_KV_LORA).astype(BF16)


MLA_PP = 32
MLA_GROUPS = 2


def _mla_sample_body(pt_ref, q_ref, kn_ref, *rest):
    lat_refs = rest[:MLA_PP]
    kro_refs = rest[MLA_PP:2 * MLA_PP]
    o_ref, m_ref, l_ref, acc_ref = rest[2 * MLA_PP:]
    j = pl.program_id(1)
    t = q_ref.shape[1]
    rows = MLA_H * t
    q = q_ref[...].reshape(rows, MLA_QK).astype(BF16)
    ql = q[:, :MLA_KV_LORA]
    qr = q[:, MLA_KV_LORA:MLA_KV_LORA + MLA_ROPE]

    @pl.when(j == 0)
    def _():
        m_ref[...] = jnp.full(m_ref.shape, -jnp.inf, F32)
        l_ref[...] = jnp.zeros(l_ref.shape, F32)
        acc_ref[...] = jnp.zeros(acc_ref.shape, F32)

    vrep = MLA_KV_LORA // LANES

    def update(g, s, vals, row_sum):
        m_old = m_ref[g]
        m_new = jnp.maximum(m_old, jnp.max(s, axis=1, keepdims=True))
        alpha = jnp.exp(m_old - m_new)
        if s.shape[1] % LANES == 0:
            p = jnp.exp(s - jnp.concatenate([m_new] * (s.shape[1] // LANES), axis=1)).astype(BF16)
        else:
            p = jnp.exp(s - m_new[:, 0:1]).astype(BF16)
        l_ref[g] = l_ref[g] * alpha + row_sum(p)
        acc_ref[g] = (acc_ref[g] * jnp.concatenate([alpha] * vrep, axis=1)
                      + jnp.dot(p, vals, preferred_element_type=F32))
        m_ref[g] = m_new

    per = MLA_PP // MLA_GROUPS
    ones = jnp.ones((per * lat_refs[0].shape[1], LANES), BF16)
    scores, values = [], []
    for g in range(MLA_GROUPS):
        cbs, s_parts = [], []
        for pp in range(g * per, (g + 1) * per):
            cb = lat_refs[pp][0].astype(BF16)
            kbt = kro_refs[pp][0].astype(BF16)
            s_parts.append(lax.dot_general(ql, cb, (((1,), (1,)), ((), ())), preferred_element_type=F32)
                           + jnp.dot(qr, kbt, preferred_element_type=F32))
            cbs.append(cb)
        scores.append(jnp.concatenate(s_parts, axis=1))
        values.append(jnp.concatenate(cbs, axis=0))
    for g in range(MLA_GROUPS):
        update(g, scores[g], values[g], lambda p: jnp.dot(p, ones, preferred_element_type=F32))

    @pl.when(j == pl.num_programs(1) - 1)
    def _():
        kn = kn_ref[...].astype(BF16)
        s = lax.dot_general(q, kn, (((1,), (1,)), ((), ())), preferred_element_type=F32)
        qpos = lax.broadcasted_iota(jnp.int32, (rows, t), 0) % t
        kpos = lax.broadcasted_iota(jnp.int32, (rows, t), 1)
        s = jnp.where(kpos <= qpos, s, -jnp.inf)
        update(0, s, kn[:, :MLA_KV_LORA], lambda p: jnp.sum(p.astype(F32), axis=1, keepdims=True))
        m_all = m_ref[0]
        for g in range(1, MLA_GROUPS):
            m_all = jnp.maximum(m_all, m_ref[g])
        l_all = jnp.zeros((rows, LANES), F32)
        acc = jnp.zeros((rows, MLA_KV_LORA), F32)
        for g in range(MLA_GROUPS):
            wgt = jnp.exp(m_ref[g] - m_all)
            l_all = l_all + l_ref[g] * wgt
            acc = acc + acc_ref[g] * jnp.concatenate([wgt] * vrep, axis=1)
        o = acc / jnp.concatenate([l_all] * vrep, axis=1)
        o_ref[...] = o.reshape(MLA_H, t, MLA_KV_LORA).astype(o_ref.dtype)


def _mla_layer(x2d, pos, w, mi, nw, b, t, paged):
    n = b * t
    tl = _tiling(b, t, 512)
    G, J, R = tl["G"], tl["J"], tl["R"]
    adt = BF16 if t % 16 == 0 else F32
    half = MLA_ROPE // 2
    inv = ROPE_THETA ** (-jnp.arange(half, dtype=F32) / half)
    ang = pos.astype(F32)[:, None] * inv[None, :]
    cos, sin = jnp.cos(ang), jnp.sin(ang)
    zpad = jnp.zeros((t, LANES - MLA_ROPE), F32)
    zh = jnp.zeros((t, half), F32)
    tab = jnp.stack([jnp.concatenate([cos, cos, zpad], 1), jnp.concatenate([-sin, zh, zpad], 1),
                     jnp.concatenate([zh, sin, zpad], 1)])
    if G == 1:
        tab = jnp.tile(tab, (1, b, 1))
    w_in = w["mla_w_in"][mi]
    winq = w_in[:, :MLA_Q_LORA].astype(BF16)
    winc = w_in[:, MLA_Q_LORA:MLA_Q_LORA + MLA_KV_LORA].astype(BF16)
    wink = jnp.pad(w_in[:, MLA_Q_LORA + MLA_KV_LORA:], ((0, 0), (0, LANES - MLA_ROPE))).astype(BF16)
    wqb = w["mla_w_qb"][mi].reshape(MLA_Q_LORA, MLA_H, MLA_NOPE + MLA_ROPE)
    wqn = wqb[:, :, :MLA_NOPE].reshape(MLA_Q_LORA, MLA_H * MLA_NOPE).astype(BF16)
    wqr = jnp.pad(wqb[:, :, MLA_NOPE:], ((0, 0), (0, 0), (0, LANES - MLA_ROPE))
                  ).reshape(MLA_Q_LORA, MLA_H * LANES).astype(BF16)
    wuk = jnp.transpose(w["mla_w_uk"][mi], (1, 2, 0)).reshape(MLA_H // 2, 2, MLA_NOPE, MLA_KV_LORA)
    wuk_bd = jnp.einsum("pinc,ij->pinjc", wuk, jnp.eye(2, dtype=F32)).reshape(
        MLA_H // 2, 2 * MLA_NOPE, 2 * MLA_KV_LORA).astype(BF16)
    wuv = jnp.transpose(w["mla_w_uv"][mi], (1, 0, 2)).reshape(MLA_H // 2, 2, MLA_KV_LORA, MLA_V)
    wuv_bd = jnp.einsum("picv,ij->picjv", wuv, jnp.eye(2, dtype=F32)).reshape(
        MLA_H // 2, 2 * MLA_KV_LORA, 2 * MLA_V).astype(BF16)
    nwa = jnp.concatenate([nw[0:1], jnp.zeros((7, D_MODEL), F32)])
    nwb = jnp.concatenate([nw[1:2], jnp.zeros((7, D_MODEL), F32)])
    row = lambda g, j: (g * J + j, 0)
    wl = [winq, winc, wink, w["mla_q_norm"][mi][None, :], w["mla_kv_norm"][mi][None, :], wqn, wqr, wuk_bd]
    c, kr, kcat, qcat = pl.pallas_call(
        _mla_proj_body,
        grid=(G, J),
        in_specs=[pl.BlockSpec((R, D_MODEL), row), _const_spec((8, D_MODEL)),
                  pl.BlockSpec((3, R, LANES), lambda g, j: (0, j, 0))] + [_const_spec(a.shape) for a in wl],
        out_specs=[pl.BlockSpec((R, MLA_KV_LORA), row), pl.BlockSpec((R, LANES), row),
                   pl.BlockSpec((R, MLA_QK), row), pl.BlockSpec((MLA_H, R, MLA_QK), lambda g, j: (0, g * J + j, 0))],
        out_shape=[jax.ShapeDtypeStruct((n, MLA_KV_LORA), F32), jax.ShapeDtypeStruct((n, LANES), F32),
                   jax.ShapeDtypeStruct((n, MLA_QK), adt), jax.ShapeDtypeStruct((MLA_H, n, MLA_QK), adt)],
        compiler_params=_params(2),
        name="mla_proj",
    )(x2d, nwa, tab, *wl)

    if paged is None:
        nq = t // MLA_TQ
        rows = MLA_H * MLA_TQ
        o = pl.pallas_call(
            _mla_prompt_body,
            grid=(b, nq),
            in_specs=[pl.BlockSpec((MLA_H, MLA_TQ, MLA_QK), lambda bb, i: (0, bb * nq + i, 0)),
                      pl.BlockSpec((t, MLA_QK), lambda bb, i: (bb, 0))],
            out_specs=pl.BlockSpec((MLA_H, MLA_TQ, MLA_KV_LORA), lambda bb, i: (0, bb * nq + i, 0)),
            out_shape=jax.ShapeDtypeStruct((MLA_H, n, MLA_KV_LORA), BF16),
            scratch_shapes=[pltpu.VMEM((rows, LANES), F32), pltpu.VMEM((rows, LANES), F32),
                            pltpu.VMEM((rows, MLA_KV_LORA), F32)],
            compiler_params=_params(2),
            name="mla_attend_prompt",
        )(qcat, kcat)
    else:
        pages_c, pages_kr, page_table = paged
        page = pages_c.shape[1]
        npg = page_table.shape[1]
        assert npg % MLA_PP == 0
        rows = MLA_H * t

        def page_map(pp):
            return lambda bb, j, pt: (pt[bb, j * MLA_PP + pp], 0, 0)

        grid_spec = pltpu.PrefetchScalarGridSpec(
            num_scalar_prefetch=1,
            grid=(b, npg // MLA_PP),
            in_specs=[pl.BlockSpec((MLA_H, t, MLA_QK), lambda bb, j, pt: (0, bb, 0)),
                      pl.BlockSpec((t, MLA_QK), lambda bb, j, pt: (bb, 0))]
            + [pl.BlockSpec((1, page, MLA_KV_LORA), page_map(pp)) for pp in range(MLA_PP)]
            + [pl.BlockSpec((1, MLA_ROPE, page), page_map(pp)) for pp in range(MLA_PP)],
            out_specs=pl.BlockSpec((MLA_H, t, MLA_KV_LORA), lambda bb, j, pt: (0, bb, 0)),
            scratch_shapes=[pltpu.VMEM((MLA_GROUPS, rows, LANES), F32), pltpu.VMEM((MLA_GROUPS, rows, LANES), F32),
                            pltpu.VMEM((MLA_GROUPS, rows, MLA_KV_LORA), F32)],
        )
        o = pl.pallas_call(
            _mla_sample_body,
            grid_spec=grid_spec,
            out_shape=jax.ShapeDtypeStruct((MLA_H, n, MLA_KV_LORA), adt),
            compiler_params=_params(2),
            name="mla_attend_sample",
        )(page_table, qcat, kcat, *([pages_c] * MLA_PP), *([jnp.swapaxes(pages_kr, 1, 2)] * MLA_PP))

    pre = ("mla", [o], [wuv_bd, w["mla_wo"][mi].astype(BF16)])
    return pre, c.reshape(b, t, MLA_KV_LORA), kr[:, :MLA_ROPE].reshape(b, t, MLA_ROPE)


GDN_CW = 512


def _gdn_proj_body(x_ref, prev_ref, nw_ref, wqkv_ref, wz_ref, wbg_ref, cw_ref, gvec_ref, tri_ref,
                   q_ref, k_ref, v_ref, z_ref, beta_ref, gc_ref, st_ref, carry_ref):
    @pl.when(pl.program_id(1) == 0)
    def _():
        carry_ref[...] = prev_ref[...]

    rows = x_ref.shape[0]
    p = carry_ref.shape[0]
    h = _rms(x_ref[...], nw_ref[0:1, :]).astype(BF16)
    adt = z_ref.dtype
    bg = jnp.dot(h, wbg_ref[...], preferred_element_type=F32)
    beta_ref[...] = _sigmoid(bg)
    g = -jnp.exp(gvec_ref[0:1, :]) * _softplus(bg + gvec_ref[1:2, :])
    gc_ref[...] = _chunk_cumsum(g, tri_ref[...])
    nch = GDN_CONV_DIM // GDN_CW
    zw = GDN_V_DIM // nch

    def up(c):
        return jnp.dot(h, wqkv_ref[:, c * GDN_CW:(c + 1) * GDN_CW], preferred_element_type=F32)

    u_nxt = up(0)
    for c in range(nch):
        sl = slice(c * GDN_CW, (c + 1) * GDN_CW)
        u = u_nxt
        u_nxt = up(c + 1) if c + 1 < nch else None
        z_ref[:, c * zw:(c + 1) * zw] = jnp.dot(h, wz_ref[:, c * zw:(c + 1) * zw],
                                                preferred_element_type=F32).astype(adt)
        prev = carry_ref[:, sl]
        u1 = _shift_rows(u, prev, 1)
        near = cw_ref[3:4, sl] * u + cw_ref[2:3, sl] * u1
        far = cw_ref[1:2, sl] * u + cw_ref[0:1, sl] * u1
        prev1 = pltpu.roll(prev, 1, 0)
        far_prev = cw_ref[1:2, sl] * prev + cw_ref[0:1, sl] * prev1
        y = near + _shift_rows(far, far_prev, 2)
        tail = u[rows - p:, :]
        carry_ref[:, sl] = tail
        st_ref[:, sl] = tail
        y = _silu(y)
        off = c * GDN_CW
        if off < 2 * GDN_QK_DIM:
            dst, base, scale = (q_ref, off, GDN_DK ** -0.5) if off < GDN_QK_DIM else (k_ref, off - GDN_QK_DIM, 1.0)
            for hh in range(GDN_CW // GDN_DK):
                yh = y[:, hh * GDN_DK:(hh + 1) * GDN_DK]
                inv = lax.rsqrt(jnp.sum(yh * yh, axis=-1, keepdims=True) + 1e-6)
                dst[:, base + hh * GDN_DK:base + (hh + 1) * GDN_DK] = (yh * (inv * scale)).astype(adt)
        else:
            v_ref[:, off - 2 * GDN_QK_DIM:off - 2 * GDN_QK_DIM + GDN_CW] = y.astype(adt)


def _gdn_chunk_body(q_ref, k_ref, v_ref, z_ref, gc_ref, beta_ref, s0_ref, nw_ref,
                    o_ref, so_ref, *, nh, chunk, nsub):
    @pl.when(pl.program_id(1) == 0)
    def _():
        so_ref[...] = s0_ref[...]

    ng = GDN_V_H // nh
    gc = nh * chunk
    rep = GDN_V_H // GDN_QK_H
    ri = lax.broadcasted_iota(jnp.int32, (gc, gc), 0)
    ci = lax.broadcasted_iota(jnp.int32, (gc, gc), 1)
    same = (ri // chunk) == (ci // chunk)
    strict = same & ((ri % chunk) > (ci % chunk))
    incl = same & ((ri % chunk) >= (ci % chunk))
    last = same & ((ci % chunk) == chunk - 1)
    eye = (ri == ci).astype(F32)
    row_head = lax.broadcasted_iota(jnp.int32, (gc, GDN_DK), 0) // chunk
    groups = range(ng)
    heads = [[q * nh + i for i in range(nh)] for q in groups]
    keys = [(sc, q) for sc in range(nsub) for q in groups]

    def stack(ref, sc, hds, width):
        parts = [ref[sc * chunk:(sc + 1) * chunk, hd * width:(hd + 1) * width] for hd in hds]
        return parts[0] if len(parts) == 1 else jnp.concatenate(parts, axis=0)

    def col(ref, sc, lanes):
        parts = [ref[sc * chunk:(sc + 1) * chunk, ln:ln + 1] for ln in lanes]
        return parts[0] if len(parts) == 1 else jnp.concatenate(parts, axis=0)

    k_st = {k: stack(k_ref, k[0], [hd // rep for hd in heads[k[1]]], GDN_DK) for k in keys}
    q_st = {k: stack(q_ref, k[0], [hd // rep for hd in heads[k[1]]], GDN_DK) for k in keys}
    v_st = {k: stack(v_ref, k[0], heads[k[1]], GDN_DV) for k in keys}
    gcol = {k: col(gc_ref, k[0], [GDN_V_H + hd for hd in heads[k[1]]]) for k in keys}
    bcol = {k: col(beta_ref, k[0], heads[k[1]]) for k in keys}
    grow = {k: jnp.sum(jnp.where(ri == ci, gcol[k], 0.0), axis=0, keepdims=True) for k in keys}
    k_b = {k: k_st[k].astype(BF16) for k in keys}
    kq = {k: _bdot_nt(jnp.concatenate([k_b[k], q_st[k].astype(BF16)], axis=0), k_b[k]) for k in keys}
    decay = {k: jnp.exp(jnp.where(incl, gcol[k] - grow[k], -jnp.inf)) for k in keys}
    a = {k: jnp.where(strict, kq[k][:gc] * bcol[k] * decay[k], 0.0) for k in keys}
    aqk = {k: jnp.where(incl, kq[k][gc:] * decay[k], 0.0).astype(BF16) for k in keys}

    p = {k: (-a[k]).astype(BF16) for k in keys}
    x = {k: eye - a[k] for k in keys}
    span = 2
    if span < chunk:
        p = {k: _bdot(p[k], p[k]) for k in keys}
    while span < chunk:
        if span * 2 < chunk:
            px = {k: _bdot(p[k], jnp.concatenate([p[k].astype(BF16), x[k].astype(BF16)], axis=1)) for k in keys}
            p = {k: px[k][:, :gc] for k in keys}
            x = {k: x[k] + px[k][:, gc:] for k in keys}
        else:
            x = {k: x[k] + _bdot(p[k], x[k]) for k in keys}
        span *= 2

    egc = {k: jnp.exp(gcol[k]) for k in keys}
    uw = {k: _bdot(x[k], jnp.concatenate([v_st[k] * bcol[k], k_st[k] * (bcol[k] * egc[k])], axis=1)) for k in keys}
    glast = {k: jnp.sum(jnp.where(last, grow[k], 0.0), axis=1, keepdims=True) for k in keys}
    kg = {k: k_st[k] * jnp.exp(glast[k] - gcol[k]) for k in keys}
    qg = {k: q_st[k] * egc[k] for k in keys}

    for sc in range(nsub):
        states = [[so_ref[0, hd] for hd in heads[q]] for q in groups]
        wq_s = []
        for q in groups:
            wm = uw[(sc, q)][:, GDN_DV:]
            parts = []
            for i in range(nh):
                rs = slice(i * chunk, (i + 1) * chunk)
                parts.append(_bdot(jnp.concatenate([wm[rs], qg[(sc, q)][rs]], axis=0), states[q][i]))
            wq_s.append(parts)
        v_new, o_st = [], []
        for q in groups:
            ws = jnp.concatenate([m[:chunk] for m in wq_s[q]], axis=0) if nh > 1 else wq_s[q][0][:chunk]
            qs = jnp.concatenate([m[chunk:] for m in wq_s[q]], axis=0) if nh > 1 else wq_s[q][0][chunk:]
            vn = (uw[(sc, q)][:, :GDN_DV] - ws).astype(BF16)
            v_new.append(vn)
            o_st.append(qs + _bdot(aqk[(sc, q)], vn))
        for q in groups:
            for i, hd in enumerate(heads[q]):
                rs = slice(i * chunk, (i + 1) * chunk)
                gl_h = jnp.exp(glast[(sc, q)][i * chunk:i * chunk + 1, :])
                if chunk % 16 == 0:
                    upd = _bdot_tn(kg[(sc, q)][rs], v_new[q][rs])
                else:
                    upd = _bdot_tn(jnp.where(row_head == i, kg[(sc, q)], 0.0), v_new[q])
                so_ref[0, hd] = states[q][i] * gl_h + upd
        for q in groups:
            z_st = stack(z_ref, sc, heads[q], GDN_DV).astype(F32)
            og = (_rms(o_st[q], nw_ref[0:1, :]) * _silu(z_st)).astype(o_ref.dtype)
            for i, hd in enumerate(heads[q]):
                o_ref[sc * chunk:(sc + 1) * chunk, hd * GDN_DV:(hd + 1) * GDN_DV] = og[i * chunk:(i + 1) * chunk]


GDN_SUBCHUNKS = 4


def _gdn_layer(x2d, conv_prev, s0, w, gi, nw, b, t):
    n = b * t
    chunk = _chunk_of(t)
    tl = _tiling(b, t, 256)
    G, J, R, P = tl["G"], tl["J"], tl["R"], tl["P"]
    adt = BF16 if chunk % 16 == 0 else F32
    w_in = w["gdn_w_in"][gi]
    o1 = GDN_CONV_DIM
    o2 = o1 + GDN_V_DIM
    wqkv = w_in[:, :o1].astype(BF16)
    wz = w_in[:, o1:o2].astype(BF16)
    wbg = jnp.pad(w_in[:, o2:], ((0, 0), (0, LANES - 2 * GDN_V_H))).astype(BF16)
    cw = jnp.pad(w["gdn_conv_w"][gi], ((0, 8 - GDN_CONV), (0, 0)))
    gvec = jnp.zeros((8, LANES), F32)
    gvec = gvec.at[0, GDN_V_H:2 * GDN_V_H].set(w["gdn_a_log"][gi]).at[1, GDN_V_H:2 * GDN_V_H].set(w["gdn_dt_bias"][gi])
    bc = chunk if chunk == 64 else R
    tri = _chunk_masks(chunk, bc)
    nwa = jnp.concatenate([nw[0:1], jnp.zeros((7, D_MODEL), F32)])
    row = lambda g, j: (g * J + j, 0)
    st_spec = pl.BlockSpec((P, GDN_CONV_DIM), lambda g, j: (g, 0))
    qn, kn, v, z, beta, gcs, st = pl.pallas_call(
        _gdn_proj_body,
        grid=(G, J),
        in_specs=[pl.BlockSpec((R, D_MODEL), row), st_spec, _const_spec((8, D_MODEL)), _const_spec(wqkv.shape),
                  _const_spec(wz.shape), _const_spec(wbg.shape), _const_spec(cw.shape), _const_spec(gvec.shape),
                  _const_spec(tri.shape)],
        out_specs=[pl.BlockSpec((R, GDN_QK_DIM), row), pl.BlockSpec((R, GDN_QK_DIM), row),
                   pl.BlockSpec((R, GDN_V_DIM), row), pl.BlockSpec((R, GDN_V_DIM), row),
                   pl.BlockSpec((R, LANES), row), pl.BlockSpec((R, LANES), row), st_spec],
        out_shape=[jax.ShapeDtypeStruct((n, GDN_QK_DIM), adt), jax.ShapeDtypeStruct((n, GDN_QK_DIM), adt),
                   jax.ShapeDtypeStruct((n, GDN_V_DIM), adt), jax.ShapeDtypeStruct((n, GDN_V_DIM), adt),
                   jax.ShapeDtypeStruct((n, LANES), F32), jax.ShapeDtypeStruct((n, LANES), F32),
                   jax.ShapeDtypeStruct((b * SUBLANES, GDN_CONV_DIM), F32)],
        scratch_shapes=[pltpu.VMEM((P, GDN_CONV_DIM), F32)],
        compiler_params=_params(2),
        name="gdn_proj",
    )(x2d, _pad_state(conv_prev), nwa, wqkv, wz, wbg, cw, gvec, tri)
    conv_new = st.reshape(b, SUBLANES, GDN_CONV_DIM)[:, SUBLANES - (GDN_CONV - 1):]

    nh = GROUP_ROWS // chunk
    ng = GDN_V_H // nh
    nsub = GDN_SUBCHUNKS if (t // chunk) % GDN_SUBCHUNKS == 0 else 1
    nct = t // (chunk * nsub)
    br = chunk * nsub
    crow = lambda bb, j: (bb * nct + j, 0)
    sspec = pl.BlockSpec((1, GDN_V_H, GDN_DK, GDN_DV), lambda bb, j: (bb, 0, 0, 0))
    nwn = jnp.concatenate([w["gdn_norm_w"][gi][None, :], jnp.zeros((7, GDN_DV), F32)])
    o, s_new = pl.pallas_call(
        functools.partial(_gdn_chunk_body, nh=nh, chunk=chunk, nsub=nsub),
        grid=(b, nct),
        in_specs=[pl.BlockSpec((br, GDN_QK_DIM), crow), pl.BlockSpec((br, GDN_QK_DIM), crow),
                  pl.BlockSpec((br, GDN_V_DIM), crow), pl.BlockSpec((br, GDN_V_DIM), crow),
                  pl.BlockSpec((br, LANES), crow), pl.BlockSpec((br, LANES), crow),
                  sspec, _const_spec((8, GDN_DV))],
        out_specs=[pl.BlockSpec((br, GDN_V_DIM), crow), sspec],
        out_shape=[jax.ShapeDtypeStruct((n, GDN_V_DIM), adt),
                   jax.ShapeDtypeStruct((b, GDN_V_H, GDN_DK, GDN_DV), F32)],
        compiler_params=_params(2),
        name="gdn_chunk",
    )(qn, kn, v, z, gcs, beta, s0, nwn)
    pre = ("plain", [o], [w["gdn_wo"][gi].astype(BF16)])
    return pre, conv_new, s_new


def _trunk(x, pos, rw_s, rw_shift, gdn_s, gdn_conv, ffn_conv, w, paged):
    b, t, _ = x.shape
    x2d = x.reshape(b * t, D_MODEL)
    new = {k: [] for k in ("rw_S", "rw_shift", "mla_c", "mla_kr", "gdn_S", "gdn_conv", "ffn_conv")}
    v_first = None
    ri = mi = gi = 0
    for l, kind in enumerate(LAYER_MIXER):
        nw = w["norm_w"][l]
        if kind == 0:
            pre, sh, s_new, v_first = _rwkv_layer(x2d, rw_shift[ri], rw_s[ri], v_first, w, ri, nw, b, t)
            new["rw_S"].append(s_new)
            new["rw_shift"].append(sh)
            ri += 1
        elif kind == 1:
            pre, c, kr = _mla_layer(x2d, pos, w, mi, nw, b, t, None if paged is None else
                                    (paged[0][mi], paged[1][mi], paged[2]))
            new["mla_c"].append(c)
            new["mla_kr"].append(kr)
            mi += 1
        else:
            pre, cb, s_new = _gdn_layer(x2d, gdn_conv[gi], gdn_s[gi], w, gi, nw, b, t)
            new["gdn_S"].append(s_new)
            new["gdn_conv"].append(cb)
            gi += 1
        nwf = jnp.concatenate([nw[2:4], nw[1:2], jnp.zeros((5, D_MODEL), F32)])
        cwb = jnp.concatenate([w["ffn_conv_w"][l], w["ffn_conv_b"][l][None, :],
                               jnp.zeros((8 - FFN_CONV - 1, 2 * D_FF), F32)])
        x2d, st = _ffn(x2d, pre, _pad_state(ffn_conv[l]), nwf, w["ffn_w_up"][l].astype(BF16), cwb,
                       w["ffn_w_down"][l].astype(BF16), b, t)
        new["ffn_conv"].append(st.reshape(b, SUBLANES, 2 * D_FF)[:, SUBLANES - (FFN_CONV - 1):])
    return x2d.reshape(b, t, D_MODEL), {k: jnp.stack(v) for k, v in new.items()}


def kernel(x_prompt, x_sample, state_rwkv_wkv, state_rwkv_shift, cache_mla_latent, cache_mla_krope, state_gdn_S, state_gdn_conv, state_ffn_conv, page_table, norm_w, rw_mu, rw_wrkv, rw_w0, rw_w1, rw_w2, rw_a0, rw_a1, rw_a2, rw_v0, rw_v1, rw_v2, rw_g1, rw_g2, rw_kk, rw_ka, rw_rk, rw_lnx_w, rw_lnx_b, rw_wo, mla_w_in, mla_q_norm, mla_kv_norm, mla_w_qb, mla_w_uk, mla_w_uv, mla_wo, gdn_w_in, gdn_conv_w, gdn_a_log, gdn_dt_bias, gdn_norm_w, gdn_wo, ffn_w_up, ffn_conv_w, ffn_conv_b, ffn_w_down):
    w = dict(norm_w=norm_w, rw_mu=rw_mu, rw_wrkv=rw_wrkv, rw_w0=rw_w0, rw_w1=rw_w1, rw_w2=rw_w2, rw_a0=rw_a0,
             rw_a1=rw_a1, rw_a2=rw_a2, rw_v0=rw_v0, rw_v1=rw_v1, rw_v2=rw_v2, rw_g1=rw_g1, rw_g2=rw_g2,
             rw_kk=rw_kk, rw_ka=rw_ka, rw_rk=rw_rk, rw_lnx_w=rw_lnx_w, rw_lnx_b=rw_lnx_b, rw_wo=rw_wo,
             mla_w_in=mla_w_in, mla_q_norm=mla_q_norm, mla_kv_norm=mla_kv_norm, mla_w_qb=mla_w_qb,
             mla_w_uk=mla_w_uk, mla_w_uv=mla_w_uv, mla_wo=mla_wo, gdn_w_in=gdn_w_in, gdn_conv_w=gdn_conv_w,
             gdn_a_log=gdn_a_log, gdn_dt_bias=gdn_dt_bias, gdn_norm_w=gdn_norm_w, gdn_wo=gdn_wo,
             ffn_w_up=ffn_w_up, ffn_conv_w=ffn_conv_w, ffn_conv_b=ffn_conv_b, ffn_w_down=ffn_w_down)
    b, t = x_prompt.shape[0], x_prompt.shape[1]
    n_rw, n_gdn, depth = state_rwkv_wkv.shape[0], state_gdn_S.shape[0], state_ffn_conv.shape[0]
    y_p, sp = _trunk(
        x_prompt, jnp.arange(t),
        jnp.zeros((n_rw, b) + state_rwkv_wkv.shape[2:], F32), jnp.zeros((n_rw, b, D_MODEL), F32),
        jnp.zeros((n_gdn, b) + state_gdn_S.shape[2:], F32), jnp.zeros((n_gdn, b) + state_gdn_conv.shape[2:], F32),
        jnp.zeros((depth, b) + state_ffn_conv.shape[2:], F32), w, None)
    past_len = page_table.shape[1] * cache_mla_latent.shape[2]
    pos_s = past_len + jnp.arange(x_sample.shape[1])
    y_s, ss = _trunk(x_sample, pos_s, state_rwkv_wkv, state_rwkv_shift, state_gdn_S, state_gdn_conv,
                     state_ffn_conv, w, (cache_mla_latent, cache_mla_krope, page_table))
    names = ("rw_S", "rw_shift", "mla_c", "mla_kr", "gdn_S", "gdn_conv", "ffn_conv")
    return (y_p, y_s) + tuple(sp[k] for k in names) + tuple(ss[k] for k in names)
```

```python
import functools

import jax
import jax.numpy as jnp
from jax import lax
from jax.experimental import pallas as pl
from jax.experimental.pallas import tpu as pltpu

F32 = jnp.float32
BF16 = jnp.bfloat16
HIGHEST = lax.Precision.HIGHEST

D_MODEL = 1024
NORM_EPS = 1e-6
RW_N = 64
RW_H = D_MODEL // RW_N
RW_LNX_EPS = 64e-5
MLA_H = 16
MLA_NOPE = 64
MLA_ROPE = 32
MLA_V = 64
MLA_Q_LORA = 512
MLA_KV_LORA = 256
MLA_SCALE = (MLA_NOPE + MLA_ROPE) ** -0.5
ROPE_THETA = 10000.0
MLA_QK = MLA_KV_LORA + 128
GDN_QK_H = 8
GDN_V_H = 16
GDN_DK = 128
GDN_DV = 128
GDN_QK_DIM = GDN_QK_H * GDN_DK
GDN_V_DIM = GDN_V_H * GDN_DV
GDN_CONV_DIM = 2 * GDN_QK_DIM + GDN_V_DIM
GDN_CONV = 4
D_FF = 2816
FFN_CONV = 3
LAYER_MIXER = (0, 1, 2, 0)

SUBLANES = 8
LANES = 128
GROUP_ROWS = 128
VMEM_LIMIT = 56 * 1024 * 1024


def _rms(x, w):
    return x * lax.rsqrt(jnp.mean(x * x, axis=-1, keepdims=True) + NORM_EPS) * w


def _bdot(a, b):
    return jnp.dot(a.astype(BF16), b.astype(BF16), preferred_element_type=F32)


def _bdot_nt(a, b):
    return lax.dot_general(a.astype(BF16), b.astype(BF16), (((1,), (1,)), ((), ())),
                           preferred_element_type=F32)


def _bdot_tn(a, b):
    return lax.dot_general(a.astype(BF16), b.astype(BF16), (((0,), (0,)), ((), ())),
                           preferred_element_type=F32)


def _hdot(a, b):
    return jnp.dot(a, b, precision=HIGHEST, preferred_element_type=F32)


def _sigmoid(x):
    return 1.0 / (1.0 + jnp.exp(-x))


def _softplus(x):
    return jnp.maximum(x, 0.0) + jnp.log(1.0 + jnp.exp(-jnp.abs(x)))


def _silu(x):
    return x * _sigmoid(x)


def _shift_rows(u, prev, s):
    rows, cols = u.shape
    p = prev.shape[0]
    rolled = pltpu.roll(u, s, 0)
    fix = pltpu.roll(prev, (p - SUBLANES + s) % p, 0)
    t = lax.broadcasted_iota(jnp.int32, (p, cols), 0) % SUBLANES
    if p == rows:
        return jnp.where(t < s, fix, rolled)
    head = jnp.where(t < s, fix, rolled[:SUBLANES])
    return jnp.concatenate([head, rolled[SUBLANES:]], axis=0)


def _lane_group_sum(x, ones2):
    parts = []
    for i in range(x.shape[1] // LANES):
        xs = x[:, i * LANES:(i + 1) * LANES]
        hi = xs.astype(BF16)
        lo = (xs - hi.astype(F32)).astype(BF16)
        parts.append(jnp.dot(jnp.concatenate([hi, lo], axis=1), ones2, preferred_element_type=F32))
    return parts[0] if len(parts) == 1 else jnp.concatenate(parts, axis=1)


def _split_dot(m2, x):
    hi = x.astype(BF16)
    lo = (x - hi.astype(F32)).astype(BF16)
    return jnp.dot(m2, jnp.concatenate([hi, lo], axis=0), preferred_element_type=F32)


def _chunk_cumsum(x, tri):
    bc = tri.shape[0]
    parts = [_hdot(tri, x[i * bc:(i + 1) * bc]) for i in range(x.shape[0] // bc)]
    return parts[0] if len(parts) == 1 else jnp.concatenate(parts, axis=0)


def _unit_lower_inverse(n_mats, ri, ci, chunk):
    keys = list(n_mats)
    eye = (ri == ci).astype(F32)
    base = min(chunk, SUBLANES)
    same = (ri // base) == (ci // base)
    n0 = {k: jnp.where(same, n_mats[k], 0.0) for k in keys}
    x = {k: eye + n0[k] for k in keys}
    p = {k: n0[k].astype(BF16) for k in keys}
    span = 2
    while span < base:
        p = {k: _bdot(p[k], p[k]).astype(BF16) for k in keys}
        x = {k: x[k] + _bdot(p[k], x[k]) for k in keys}
        span *= 2
    size = base
    while size < chunk:
        pair = ((ri // (2 * size)) == (ci // (2 * size))) & ((ri // size) != (ci // size))
        xb = {k: x[k].astype(BF16) for k in keys}
        t = {k: _bdot(jnp.where(pair, n_mats[k], 0.0), xb[k]).astype(BF16) for k in keys}
        x = {k: x[k] + _bdot(xb[k], t[k]) for k in keys}
        size *= 2
    return x


def _tiling(b, t, tt_max):
    if t == SUBLANES:
        return dict(G=1, J=1, R=b * t, P=b * t)
    tt = min(t, tt_max)
    assert t % tt == 0 and tt % 64 == 0, (t, tt)
    return dict(G=b, J=t // tt, R=tt, P=SUBLANES)


def _chunk_of(t):
    return 64 if t % 64 == 0 else t


def _const_spec(shape):
    nd = len(shape)
    return pl.BlockSpec(shape, lambda *_: (0,) * nd, pipeline_mode=pl.Buffered(1))


def _params(n_axes):
    return pltpu.CompilerParams(dimension_semantics=("arbitrary",) * n_axes,
                                vmem_limit_bytes=VMEM_LIMIT)


def _pad_state(st):
    b, k1, c = st.shape
    return jnp.pad(st, ((0, 0), (SUBLANES - k1, 0), (0, 0))).reshape(b * SUBLANES, c)


def _chunk_masks(chunk, rows):
    i = jnp.arange(rows)
    same = (i[:, None] // chunk) == (i[None, :] // chunk)
    tri = same & ((i[None, :] % chunk) <= (i[:, None] % chunk))
    return tri.astype(F32)


FFN_CW = 256


ROW_BLOCK = 64


def _stage(buf_ref, u, carry_ref, st_ref, sl, taps):
    rows = u.shape[0]
    hb = taps * SUBLANES
    buf_ref[hb:hb + rows, :] = u
    first = lax.broadcasted_iota(jnp.int32, (SUBLANES, u.shape[1]), 0) == 0
    for i in range(taps):
        back = taps - i
        src = u[rows - back * SUBLANES:rows - (back - 1) * SUBLANES, :]
        crow = carry_ref[SUBLANES - back:SUBLANES - back + 1, sl]
        buf_ref[i * SUBLANES:(i + 1) * SUBLANES, :] = jnp.where(first, crow, pltpu.roll(src, 1, 0))
    sq = SUBLANES * SUBLANES
    tail = pltpu.einshape("(vs)d->(sv)d", u[rows - sq:, :], s=SUBLANES)[sq - SUBLANES:, :]
    carry_ref[:, sl] = tail
    st_ref[:, sl] = tail


def _taps(buf_ref, r0, nrows, taps):
    hb = taps * SUBLANES
    cur = buf_ref[hb + r0:hb + r0 + nrows, :]
    return cur, [buf_ref[hb - j * SUBLANES + r0:hb - j * SUBLANES + r0 + nrows, :] for j in range(taps, 0, -1)]


FFN_PARTS = 1


def _mixer_out(kind, refs, rs):
    if kind == "mla":
        o_ref, wuv_ref, wo_ref = refs
        parts = []
        for pr in range(MLA_H // 2):
            wp = wuv_ref[pr]
            parts.append(jnp.dot(o_ref[2 * pr, rs, :].astype(BF16), wp[:MLA_KV_LORA], preferred_element_type=F32)
                         + jnp.dot(o_ref[2 * pr + 1, rs, :].astype(BF16), wp[MLA_KV_LORA:],
                                   preferred_element_type=F32))
        y = jnp.concatenate(parts, axis=1)
    elif kind == "gated":
        y_ref, g_ref, wo_ref = refs
        y = y_ref[rs, :].astype(F32) * g_ref[rs, :].astype(F32)
    else:
        y_ref, wo_ref = refs
        y = y_ref[rs, :]
    return jnp.dot(y.astype(BF16), wo_ref[...], preferred_element_type=F32)


def _ffn_body(*refs, kind, n_pre):
    x_ref = refs[0]
    pre_refs = refs[1:1 + n_pre]
    (prev_ref, nw_ref, wup_ref, cwb_ref, wdn_ref, xo_ref, st_ref, carry_ref, act_ref, buf_ref) = refs[1 + n_pre:]

    @pl.when(pl.program_id(1) == 0)
    def _():
        carry_ref[...] = prev_ref[...]

    rows = x_ref.shape[0]
    stacked = carry_ref.shape[0] == rows
    nparts = 1 if stacked else FFN_PARTS
    prows = rows // nparts
    parts = [slice(i * prows, (i + 1) * prows) for i in range(nparts)]
    nch = D_FF // FFN_CW

    def cols(c, half):
        return slice(half * D_FF + c * FFN_CW, half * D_FF + (c + 1) * FFN_CW)

    def conv_gate(c, taps):
        ys = []
        for half in range(2):
            sl = cols(c, half)
            u, (u2, u1) = taps[half]
            ys.append(cwb_ref[0:1, sl] * u2 + cwb_ref[1:2, sl] * u1 + cwb_ref[2:3, sl] * u + cwb_ref[3:4, sl])
        return (_silu(ys[0]) * ys[1]).astype(BF16)

    xs = [x_ref[rs, :] + _rms(_mixer_out(kind, pre_refs, rs), nw_ref[2:3, :]) for rs in parts]
    for pi, rs in enumerate(parts):
        x = xs[pi] if stacked else pltpu.einshape("(sv)d->(vs)d", xs[pi], s=SUBLANES)
        h = _rms(x, nw_ref[0:1, :]).astype(BF16)

        def up(c):
            us = [jnp.dot(h, wup_ref[:, cols(c, half)], preferred_element_type=F32) for half in range(2)]
            if stacked:
                return us
            for half in range(2):
                _stage(buf_ref.at[(c % 2) * 2 + half], us[half], carry_ref, st_ref, cols(c, half), FFN_CONV - 1)
            return None

        u_cur = up(0)
        for c in range(nch):
            u_nxt = up(c + 1) if c + 1 < nch else None
            csl = slice(c * FFN_CW, (c + 1) * FFN_CW)
            if stacked:
                taps = []
                for half in range(2):
                    sl = cols(c, half)
                    u = u_cur[half]
                    prev = carry_ref[:, sl]
                    taps.append((u, [_shift_rows(u, prev, 2), _shift_rows(u, prev, 1)]))
                    carry_ref[:, sl] = u
                    st_ref[:, sl] = u
                act_ref[:, csl] = conv_gate(c, taps)
            else:
                for r0 in range(0, prows, ROW_BLOCK):
                    taps = [_taps(buf_ref.at[(c % 2) * 2 + half], r0, ROW_BLOCK, FFN_CONV - 1) for half in range(2)]
                    act_ref[rs.start + r0:rs.start + r0 + ROW_BLOCK, csl] = conv_gate(c, taps)
            u_cur = u_nxt
        f = jnp.dot(act_ref[rs, :], wdn_ref[...], preferred_element_type=F32)
        out = x + _rms(f, nw_ref[1:2, :])
        xo_ref[rs, :] = out if stacked else pltpu.einshape("(vs)d->(sv)d", out, s=SUBLANES)


def _ffn(x2d, pre, prev, nw, wup, cwb, wdn, b, t):
    tl = _tiling(b, t, 512)
    G, J, R, P = tl["G"], tl["J"], tl["R"], tl["P"]
    n = b * t
    kind, acts, wts = pre
    row = lambda g, j: (g * J + j, 0)
    if kind == "mla":
        act_specs = [pl.BlockSpec((MLA_H, R, MLA_KV_LORA), lambda g, j: (0, g * J + j, 0))]
    else:
        act_specs = [pl.BlockSpec((R, a.shape[1]), row) for a in acts]
    pre_specs = act_specs + [_const_spec(wt.shape) for wt in wts]
    return pl.pallas_call(
        functools.partial(_ffn_body, kind=kind, n_pre=len(pre_specs)),
        grid=(G, J),
        in_specs=[pl.BlockSpec((R, D_MODEL), row)] + pre_specs + [
            pl.BlockSpec((P, 2 * D_FF), lambda g, j: (g, 0)),
            _const_spec((8, D_MODEL)),
            _const_spec((D_MODEL, 2 * D_FF)),
            _const_spec((8, 2 * D_FF)),
            _const_spec((D_FF, D_MODEL)),
        ],
        out_specs=[
            pl.BlockSpec((R, D_MODEL), lambda g, j: (g * J + j, 0)),
            pl.BlockSpec((P, 2 * D_FF), lambda g, j: (g, 0)),
        ],
        out_shape=[jax.ShapeDtypeStruct((n, D_MODEL), F32),
                   jax.ShapeDtypeStruct((b * SUBLANES, 2 * D_FF), F32)],
        scratch_shapes=[pltpu.VMEM((P, 2 * D_FF), F32), pltpu.VMEM((R, D_FF), BF16),
                        pltpu.VMEM((4, R + (FFN_CONV - 1) * SUBLANES, FFN_CW), F32)],
        compiler_params=_params(2),
        name="conv_ffn",
    )(x2d, *acts, *wts, prev, nw, wup, cwb, wdn)


def _rwkv_proj_body(*refs, has_vres, chunk):
    it = iter(refs)
    x_ref, prev_ref = next(it), next(it)
    vf_ref = next(it) if has_vres else None
    vec_ref, wrkv_ref, w1_ref, w2_ref, a1_ref, a2_ref = (next(it) for _ in range(6))
    v1_ref, v2_ref = (next(it), next(it)) if has_vres else (None, None)
    g1_ref, g2_ref, tri_ref, ones_ref = (next(it) for _ in range(4))
    rt_ref, kt_ref, at_ref, bt_ref, v_ref, g_ref, gl_ref, hl_ref, carry_ref = (next(it) for _ in range(9))

    @pl.when(pl.program_id(1) == 0)
    def _():
        carry_ref[...] = prev_ref[...]

    x = x_ref[...]
    rows = x.shape[0]
    p = carry_ref.shape[0]
    h = _rms(x, vec_ref[10:11, :])
    d = _shift_rows(h, carry_ref[...], 1) - h
    tail = h[rows - p:, :]
    carry_ref[...] = tail
    hl_ref[...] = tail

    def mix(i):
        return (h + d * vec_ref[i:i + 1, :]).astype(BF16)

    r = jnp.dot(mix(0), wrkv_ref[0], preferred_element_type=F32)
    k = jnp.dot(mix(1), wrkv_ref[1], preferred_element_type=F32)
    xv = mix(2)
    v = jnp.dot(xv, wrkv_ref[2], preferred_element_type=F32)
    w_lora = _bdot(jnp.tanh(_bdot(mix(3), w1_ref[...])), w2_ref[...])
    v_lora = _bdot(_bdot(xv, v1_ref[...]), v2_ref[...]) if has_vres else None
    a_lora = _bdot(_bdot(mix(4), a1_ref[...]), a2_ref[...])
    g_ref[...] = _bdot(_sigmoid(_bdot(mix(5), g1_ref[...])), g2_ref[...]).astype(g_ref.dtype)
    adt = rt_ref.dtype

    bc = tri_ref.shape[0]
    for r0 in range(0, rows, bc):
        for l0 in range(0, D_MODEL, PROJ_LANES):
            rs, ls = slice(r0, r0 + bc), slice(l0, l0 + PROJ_LANES)
            vb = v[rs, ls]
            if has_vres:
                vb = vb + (vf_ref[rs, ls] - vb) * _sigmoid(vec_ref[11:12, ls] + v_lora[rs, ls])
            v_ref[rs, ls] = vb.astype(adt)
            a = _sigmoid(vec_ref[7:8, ls] + a_lora[rs, ls])
            kb = k[rs, ls]
            kk = kb * vec_ref[8:9, ls]
            kk = kk * lax.rsqrt(_lane_group_sum(kk * kk, ones_ref[...]) + 1e-6)
            kb = kb * (1.0 + (a - 1.0) * vec_ref[9:10, ls])
            w = -_softplus(-(vec_ref[6:7, ls] + w_lora[rs, ls])) - 0.5
            lw = -jnp.exp(w)
            cum = _split_dot(tri_ref[...], lw)
            e_bwd = jnp.exp(-cum)
            rt_ref[rs, ls] = (r[rs, ls] * jnp.exp(cum)).astype(adt)
            kt_ref[rs, ls] = (kb * e_bwd).astype(adt)
            at_ref[rs, ls] = (-kk * jnp.exp(cum - lw)).astype(adt)
            bt_ref[rs, ls] = (kk * a * e_bwd).astype(adt)
            for c in range(bc // chunk):
                row = (c + 1) * chunk - 1
                gl_ref[r0 // chunk + c, :, ls] = jnp.exp(cum[row:row + 1, :])


def _rwkv_scan_body(rt_ref, kt_ref, at_ref, bt_ref, v_ref, gl_ref, h0_ref, vec_ref, y_ref, ho_ref,
                    *, nh, chunk, nsub, per_seq):
    @pl.when(pl.program_id(1) == 0)
    def _():
        ho_ref[...] = h0_ref[...]

    gl_lanes = nh * RW_N
    ng = RW_H // nh
    gc = nh * chunk
    row_head = lax.broadcasted_iota(jnp.int32, (gc, gl_lanes), 0) // chunk
    lane_head = lax.broadcasted_iota(jnp.int32, (gc, gl_lanes), 1) // RW_N
    own = row_head == lane_head
    ri = lax.broadcasted_iota(jnp.int32, (gc, gc), 0)
    ci = lax.broadcasted_iota(jnp.int32, (gc, gc), 1)
    same = (ri // chunk) == (ci // chunk)
    strict = same & ((ri % chunk) > (ci % chunk))
    incl = same & ((ri % chunk) >= (ci % chunk))
    eye = (ri == ci).astype(F32)
    eye_l = (lax.broadcasted_iota(jnp.int32, (gl_lanes, gl_lanes), 0)
             == lax.broadcasted_iota(jnp.int32, (gl_lanes, gl_lanes), 1))
    merged = gc == GROUP_ROWS
    groups = range(ng)
    sls = [slice(q * gl_lanes, (q + 1) * gl_lanes) for q in groups]
    keys = [(sc, q) for sc in range(nsub) for q in groups]

    def blockdiag(ref, key):
        xg = ref[key[0] * chunk:(key[0] + 1) * chunk, sls[key[1]]]
        xx = jnp.concatenate([xg] * nh, axis=0) if nh > 1 else xg
        return jnp.where(own, xx, jnp.zeros_like(xx))

    r_bd = {k: blockdiag(rt_ref, k) for k in keys}
    k_bd = {k: blockdiag(kt_ref, k) for k in keys}
    a_bd = {k: blockdiag(at_ref, k) for k in keys}
    b_bd = {k: blockdiag(bt_ref, k) for k in keys}
    v_f = {k: blockdiag(v_ref, k) for k in keys}
    v_bd = {k: v_f[k].astype(BF16) for k in keys}
    bonus = {k: jnp.sum(r_bd[k].astype(F32) * k_bd[k] * vec_ref[2:3, sls[k[1]]], axis=1, keepdims=True)
             for k in keys}
    if merged:
        ar = {k: jnp.concatenate([a_bd[k], r_bd[k]], axis=0).astype(BF16) for k in keys}
        bk = {k: jnp.concatenate([b_bd[k], k_bd[k]], axis=0).astype(BF16) for k in keys}
        amat = {k: _bdot_nt(ar[k], bk[k]) for k in keys}
        a_ab = {k: jnp.where(strict, amat[k][:gc, :gc], 0.0) for k in keys}
        a_ak = {k: jnp.where(strict, amat[k][:gc, gc:], 0.0).astype(BF16) for k in keys}
        a_rbk = {k: jnp.concatenate([jnp.where(incl, amat[k][gc:, :gc], 0.0),
                                     jnp.where(incl, amat[k][gc:, gc:], 0.0)], axis=1).astype(BF16) for k in keys}
    else:
        ab_ = {k: a_bd[k].astype(BF16) for k in keys}
        rb_ = {k: r_bd[k].astype(BF16) for k in keys}
        bb_ = {k: b_bd[k].astype(BF16) for k in keys}
        kb_ = {k: k_bd[k].astype(BF16) for k in keys}
        a_ab = {k: jnp.where(strict, _bdot_nt(ab_[k], bb_[k]), 0.0) for k in keys}
        a_ak = {k: jnp.where(strict, _bdot_nt(ab_[k], kb_[k]), 0.0).astype(BF16) for k in keys}
        a_rb = {k: jnp.where(incl, _bdot_nt(rb_[k], bb_[k]), 0.0).astype(BF16) for k in keys}
        a_rk = {k: jnp.where(incl, _bdot_nt(rb_[k], kb_[k]), 0.0).astype(BF16) for k in keys}
    akv = {k: _bdot(a_ak[k], v_bd[k]) for k in keys}

    x = _unit_lower_inverse(a_ab, ri, ci, chunk)
    tinv = {k: x[k].astype(BF16) for k in keys}

    for sc in range(nsub):
        slot = sc if per_seq else 0
        hs = [ho_ref[slot, q] for q in groups]
        hs_b = [h.astype(BF16) for h in hs]
        gl_rows = [gl_ref[sc, :, sl] for sl in sls]
        if merged:
            arh = [_bdot(ar[(sc, q)], hs_b[q]) for q in groups]
            u = [_bdot(tinv[(sc, q)], arh[q][:gc] + akv[(sc, q)]).astype(BF16) for q in groups]
            uv = [jnp.concatenate([u[q], v_bd[(sc, q)]], axis=0) for q in groups]
            y_bd = [arh[q][gc:] + _bdot(a_rbk[(sc, q)], uv[q]) for q in groups]
            for q in groups:
                gl_col = jnp.sum(jnp.where(eye_l, gl_rows[q], 0.0), axis=1, keepdims=True)
                bk_g = jnp.concatenate([b_bd[(sc, q)] * gl_rows[q], k_bd[(sc, q)] * gl_rows[q]], axis=0)
                ho_ref[slot, q] = hs[q] * gl_col + _bdot_tn(bk_g, uv[q])
        else:
            ah = [_bdot(ab_[(sc, q)], hs_b[q]) for q in groups]
            rh = [_bdot(rb_[(sc, q)], hs_b[q]) for q in groups]
            u = [_bdot(tinv[(sc, q)], ah[q] + akv[(sc, q)]).astype(BF16) for q in groups]
            y_bd = [rh[q] + _bdot(a_rb[(sc, q)], u[q]) + _bdot(a_rk[(sc, q)], v_bd[(sc, q)]) for q in groups]
            for q in groups:
                gl_col = jnp.sum(jnp.where(eye_l, gl_rows[q], 0.0), axis=1, keepdims=True)
                ho_ref[slot, q] = (hs[q] * gl_col + _bdot_tn(b_bd[(sc, q)] * gl_rows[q], u[q])
                                   + _bdot_tn(k_bd[(sc, q)] * gl_rows[q], v_bd[(sc, q)]))

        for q in groups:
            sl = sls[q]
            mu = jnp.sum(y_bd[q], axis=1, keepdims=True) * (1.0 / RW_N)
            yc = jnp.where(own, y_bd[q] - mu, 0.0)
            var = jnp.sum(yc * yc, axis=1, keepdims=True) * (1.0 / RW_N)
            tot = (yc * lax.rsqrt(var + RW_LNX_EPS) * vec_ref[0:1, sl] + jnp.where(own, vec_ref[1:2, sl], 0.0)
                   + bonus[(sc, q)] * v_f[(sc, q)])
            y = tot[0:chunk]
            for hh in range(1, nh):
                y = y + tot[hh * chunk:(hh + 1) * chunk]
            y_ref[sc * chunk:(sc + 1) * chunk, sl] = y.astype(y_ref.dtype)


RWKV_SUBCHUNKS = 4
PROJ_LANES = 256


def _rwkv_layer(x2d, shift_prev, s0, v_first, w, ri, nw, b, t):
    n = b * t
    chunk = _chunk_of(t)
    tl = _tiling(b, t, 512)
    G, J, R, P = tl["G"], tl["J"], tl["R"], tl["P"]
    has_vres = v_first is not None
    vi = ri - 1
    adt = BF16 if chunk % 16 == 0 else F32
    bc = chunk if chunk == 64 else R
    tri = _chunk_masks(chunk, bc).astype(BF16)
    tri = jnp.concatenate([tri, tri], axis=1)
    li = jnp.arange(LANES)
    ones_bd = ((li[:, None] // RW_N) == (li[None, :] // RW_N)).astype(BF16)
    ones_bd = jnp.concatenate([ones_bd, ones_bd], axis=0)
    zero = jnp.zeros((D_MODEL,), F32)
    vec = jnp.stack([*(w["rw_mu"][ri][i] for i in range(6)), w["rw_w0"][ri], w["rw_a0"][ri], w["rw_kk"][ri],
                     w["rw_ka"][ri], nw[0], w["rw_v0"][vi] if has_vres else zero, zero, zero, zero, zero])
    row = lambda g, j: (g * J + j, 0)
    row_spec = pl.BlockSpec((R, D_MODEL), row)
    ins = [x2d, _pad_state(shift_prev[:, None, :])]
    specs = [row_spec, pl.BlockSpec((P, D_MODEL), lambda g, j: (g, 0))]
    if has_vres:
        ins.append(v_first)
        specs.append(row_spec)
    wl = [vec, w["rw_wrkv"][ri].astype(BF16), w["rw_w1"][ri].astype(BF16), w["rw_w2"][ri].astype(BF16),
          w["rw_a1"][ri].astype(BF16), w["rw_a2"][ri].astype(BF16)]
    if has_vres:
        wl += [w["rw_v1"][vi].astype(BF16), w["rw_v2"][vi].astype(BF16)]
    wl += [w["rw_g1"][ri].astype(BF16), w["rw_g2"][ri].astype(BF16), tri, ones_bd]
    ins += wl
    specs += [_const_spec(a.shape) for a in wl]
    nc_tile = R // chunk
    outs = pl.pallas_call(
        functools.partial(_rwkv_proj_body, has_vres=has_vres, chunk=chunk),
        grid=(G, J),
        in_specs=specs,
        out_specs=[row_spec] * 6 + [pl.BlockSpec((nc_tile, 1, D_MODEL), lambda g, j: (g * J + j, 0, 0)),
                                    pl.BlockSpec((P, D_MODEL), lambda g, j: (g, 0))],
        out_shape=[jax.ShapeDtypeStruct((n, D_MODEL), adt)] * 6
        + [jax.ShapeDtypeStruct((n // chunk, 1, D_MODEL), F32), jax.ShapeDtypeStruct((b * SUBLANES, D_MODEL), F32)],
        scratch_shapes=[pltpu.VMEM((P, D_MODEL), F32)],
        compiler_params=_params(2),
        name="rwkv_proj",
    )(*ins)
    rt, kt, at, bt, v, g, gl, hl = outs
    shift_new = hl.reshape(b, SUBLANES, D_MODEL)[:, -1]

    nh = LANES // RW_N
    ng = RW_H // nh
    gl_lanes = nh * RW_N
    hkv = jnp.swapaxes(s0, -1, -2).reshape(b, ng, nh, RW_N, RW_N)
    zblk = jnp.zeros((b, ng, RW_N, RW_N), F32)
    h0 = jnp.concatenate(
        [jnp.concatenate([hkv[:, :, i] if i == jj else zblk for jj in range(nh)], axis=-1) for i in range(nh)],
        axis=-2)
    svec = jnp.stack([w["rw_lnx_w"][ri], w["rw_lnx_b"][ri], w["rw_rk"][ri].reshape(D_MODEL),
                      zero, zero, zero, zero, zero])
    per_seq = t == chunk and b % RWKV_SUBCHUNKS == 0
    nsub = RWKV_SUBCHUNKS if per_seq or (t // chunk) % RWKV_SUBCHUNKS == 0 else 1
    nct = 1 if per_seq else t // (chunk * nsub)
    nseq = nsub if per_seq else 1
    crow = lambda bb, j: (bb * nct + j, 0)
    cspec = pl.BlockSpec((chunk * nsub, D_MODEL), crow)
    hspec = pl.BlockSpec((nseq, ng, gl_lanes, gl_lanes), lambda bb, j: (bb, 0, 0, 0))
    y, hout = pl.pallas_call(
        functools.partial(_rwkv_scan_body, nh=nh, chunk=chunk, nsub=nsub, per_seq=per_seq),
        grid=(b // nseq, nct),
        in_specs=[cspec] * 5 + [pl.BlockSpec((nsub, 1, D_MODEL), lambda bb, j: (bb * nct + j, 0, 0)), hspec,
                                _const_spec((8, D_MODEL))],
        out_specs=[cspec, hspec],
        out_shape=[jax.ShapeDtypeStruct((n, D_MODEL), adt),
                   jax.ShapeDtypeStruct((b, ng, gl_lanes, gl_lanes), F32)],
        compiler_params=_params(2),
        name="rwkv_scan",
    )(rt, kt, at, bt, v, gl, h0, svec)
    s_new = jnp.stack([hout[:, :, i * RW_N:(i + 1) * RW_N, i * RW_N:(i + 1) * RW_N] for i in range(nh)],
                      axis=2)
    s_new = jnp.swapaxes(s_new, -1, -2).reshape(b, RW_H, RW_N, RW_N)
    pre = ("gated", [y, g], [w["rw_wo"][ri].astype(BF16)])
    return pre, shift_new, s_new, (v if not has_vres else v_first)


def _rope_lanes(x, tab_ref):
    half = MLA_ROPE // 2
    return (x * tab_ref[0] + pltpu.roll(x, LANES - half, 1) * tab_ref[1] + pltpu.roll(x, half, 1) * tab_ref[2])


def _mla_proj_body(x_ref, nw_ref, tab_ref, winq_ref, winc_ref, wink_ref, qn_ref, kvn_ref, wqn_ref, wqr_ref,
                   wuk_ref, c_ref, kr_ref, kcat_ref, qcat_ref):
    h = _rms(x_ref[...], nw_ref[0:1, :]).astype(BF16)
    cq = _rms(jnp.dot(h, winq_ref[...], preferred_element_type=F32), qn_ref[...]).astype(BF16)
    c = _rms(jnp.dot(h, winc_ref[...], preferred_element_type=F32), kvn_ref[...])
    kr = _rope_lanes(jnp.dot(h, wink_ref[...], preferred_element_type=F32), tab_ref)
    c_ref[...] = c
    kr_ref[...] = kr
    adt = kcat_ref.dtype
    kcat_ref[:, 0:MLA_KV_LORA] = c.astype(adt)
    kcat_ref[:, MLA_KV_LORA:MLA_QK] = kr.astype(adt)
    qn = jnp.dot(cq, wqn_ref[...], preferred_element_type=F32).astype(BF16)
    qr = jnp.dot(cq, wqr_ref[...], preferred_element_type=F32)
    for pr in range(MLA_H // 2):
        ql = jnp.dot(qn[:, pr * LANES:(pr + 1) * LANES], wuk_ref[pr], preferred_element_type=F32) * MLA_SCALE
        qcat_ref[2 * pr, :, 0:MLA_KV_LORA] = ql[:, :MLA_KV_LORA].astype(adt)
        qcat_ref[2 * pr + 1, :, 0:MLA_KV_LORA] = ql[:, MLA_KV_LORA:].astype(adt)
    for hh in range(MLA_H):
        qro = _rope_lanes(qr[:, hh * LANES:(hh + 1) * LANES], tab_ref) * MLA_SCALE
        qcat_ref[hh, :, MLA_KV_LORA:MLA_QK] = qro.astype(adt)


MLA_TQ = 256
MLA_TK = 256
MLA_SPLIT = 16


def _mla_prompt_body(q_ref, k_ref, o_ref, m_ref, l_ref, acc_ref):
    i = pl.program_id(1)
    rows = MLA_H * MLA_TQ
    q = q_ref[...].reshape(rows, MLA_QK)
    m_ref[...] = jnp.full((rows, LANES), -jnp.inf, F32)
    l_ref[...] = jnp.zeros((rows, LANES), F32)
    acc_ref[...] = jnp.zeros((rows, MLA_KV_LORA), F32)
    ones = jnp.ones((MLA_TK, LANES), BF16)
    reps = MLA_TK // LANES

    sub = rows // MLA_SPLIT

    def block(k0, masked):
        kblk = k_ref[pl.ds(k0, MLA_TK), :]
        ss = [lax.dot_general(q[g * sub:(g + 1) * sub], kblk, (((1,), (1,)), ((), ())),
                              preferred_element_type=F32) for g in range(MLA_SPLIT)]
        def causal(g):
            qpos = i * MLA_TQ + (g * sub + lax.broadcasted_iota(jnp.int32, (sub, MLA_TK), 0)) % MLA_TQ
            kpos = k0 + lax.broadcasted_iota(jnp.int32, (sub, MLA_TK), 1)
            return kpos <= qpos

        shared = causal(0) if masked and sub % MLA_TQ == 0 else None
        for g in range(MLA_SPLIT):
            rs = slice(g * sub, (g + 1) * sub)
            s = ss[g]
            if masked:
                s = jnp.where(shared if shared is not None else causal(g), s, -jnp.inf)
            m_old = m_ref[rs, :]
            m_new = jnp.maximum(m_old, jnp.max(s, axis=1, keepdims=True))
            alpha = jnp.exp(m_old - m_new)
            pf = jnp.exp(s - jnp.concatenate([m_new] * reps, axis=1))
            l_ref[rs, :] = l_ref[rs, :] * alpha + jnp.sum(pf, axis=1, keepdims=True)
            acc_ref[rs, :] = (acc_ref[rs, :] * jnp.concatenate([alpha] * (MLA_KV_LORA // LANES), axis=1)
                              + jnp.dot(pf.astype(BF16), kblk[:, :MLA_KV_LORA], preferred_element_type=F32))
            m_ref[rs, :] = m_new

    def full_step(kb, carry):
        block(pl.multiple_of(kb * MLA_TK, MLA_TK), False)
        return carry

    n_full = (i * MLA_TQ) // MLA_TK
    lax.fori_loop(0, n_full, full_step, 0)
    block(pl.multiple_of(n_full * MLA_TK, MLA_TK), True)
    o = acc_ref[...] / jnp.concatenate([l_ref[...]] * (MLA_KV_LORA // LANES), axis=1)
    o_ref[...] = o.reshape(MLA_H, MLA_TQ, MLA_KV_LORA).astype(BF16)


MLA_PP = 32
MLA_GROUPS = 2


def _mla_sample_body(pt_ref, q_ref, kn_ref, *rest):
    lat_refs = rest[:MLA_PP]
    kro_refs = rest[MLA_PP:2 * MLA_PP]
    o_ref, m_ref, l_ref, acc_ref = rest[2 * MLA_PP:]
    j = pl.program_id(1)
    t = q_ref.shape[1]
    rows = MLA_H * t
    q = q_ref[...].reshape(rows, MLA_QK).astype(BF16)
    ql = q[:, :MLA_KV_LORA]
    qr = q[:, MLA_KV_LORA:MLA_KV_LORA + MLA_ROPE]

    @pl.when(j == 0)
    def _():
        m_ref[...] = jnp.full(m_ref.shape, -jnp.inf, F32)
        l_ref[...] = jnp.zeros(l_ref.shape, F32)
        acc_ref[...] = jnp.zeros(acc_ref.shape, F32)

    vrep = MLA_KV_LORA // LANES

    def update(g, s, vals, row_sum):
        m_old = m_ref[g]
        m_new = jnp.maximum(m_old, jnp.max(s, axis=1, keepdims=True))
        alpha = jnp.exp(m_old - m_new)
        if s.shape[1] % LANES == 0:
            p = jnp.exp(s - jnp.concatenate([m_new] * (s.shape[1] // LANES), axis=1)).astype(BF16)
        else:
            p = jnp.exp(s - m_new[:, 0:1]).astype(BF16)
        l_ref[g] = l_ref[g] * alpha + row_sum(p)
        acc_ref[g] = (acc_ref[g] * jnp.concatenate([alpha] * vrep, axis=1)
                      + jnp.dot(p, vals, preferred_element_type=F32))
        m_ref[g] = m_new

    per = MLA_PP // MLA_GROUPS
    ones = jnp.ones((per * lat_refs[0].shape[1], LANES), BF16)
    scores, values = [], []
    for g in range(MLA_GROUPS):
        cbs, s_parts = [], []
        for pp in range(g * per, (g + 1) * per):
            cb = lat_refs[pp][0].astype(BF16)
            kbt = kro_refs[pp][0].astype(BF16)
            s_parts.append(lax.dot_general(ql, cb, (((1,), (1,)), ((), ())), preferred_element_type=F32)
                           + jnp.dot(qr, kbt, preferred_element_type=F32))
            cbs.append(cb)
        scores.append(jnp.concatenate(s_parts, axis=1))
        values.append(jnp.concatenate(cbs, axis=0))
    for g in range(MLA_GROUPS):
        update(g, scores[g], values[g], lambda p: jnp.dot(p, ones, preferred_element_type=F32))

    @pl.when(j == pl.num_programs(1) - 1)
    def _():
        kn = kn_ref[...].astype(BF16)
        s = lax.dot_general(q, kn, (((1,), (1,)), ((), ())), preferred_element_type=F32)
        qpos = lax.broadcasted_iota(jnp.int32, (rows, t), 0) % t
        kpos = lax.broadcasted_iota(jnp.int32, (rows, t), 1)
        s = jnp.where(kpos <= qpos, s, -jnp.inf)
        update(0, s, kn[:, :MLA_KV_LORA], lambda p: jnp.sum(p.astype(F32), axis=1, keepdims=True))
        m_all = m_ref[0]
        for g in range(1, MLA_GROUPS):
            m_all = jnp.maximum(m_all, m_ref[g])
        l_all = jnp.zeros((rows, LANES), F32)
        acc = jnp.zeros((rows, MLA_KV_LORA), F32)
        for g in range(MLA_GROUPS):
            wgt = jnp.exp(m_ref[g] - m_all)
            l_all = l_all + l_ref[g] * wgt
            acc = acc + acc_ref[g] * jnp.concatenate([wgt] * vrep, axis=1)
        o = acc / jnp.concatenate([l_all] * vrep, axis=1)
        o_ref[...] = o.reshape(MLA_H, t, MLA_KV_LORA).astype(o_ref.dtype)


def _mla_layer(x2d, pos, w, mi, nw, b, t, paged):
    n = b * t
    tl = _tiling(b, t, 512)
    G, J, R = tl["G"], tl["J"], tl["R"]
    adt = BF16 if t % 16 == 0 else F32
    half = MLA_ROPE // 2
    inv = ROPE_THETA ** (-jnp.arange(half, dtype=F32) / half)
    ang = pos.astype(F32)[:, None] * inv[None, :]
    cos, sin = jnp.cos(ang), jnp.sin(ang)
    zpad = jnp.zeros((t, LANES - MLA_ROPE), F32)
    zh = jnp.zeros((t, half), F32)
    tab = jnp.stack([jnp.concatenate([cos, cos, zpad], 1), jnp.concatenate([-sin, zh, zpad], 1),
                     jnp.concatenate([zh, sin, zpad], 1)])
    if G == 1:
        tab = jnp.tile(tab, (1, b, 1))
    w_in = w["mla_w_in"][mi]
    winq = w_in[:, :MLA_Q_LORA].astype(BF16)
    winc = w_in[:, MLA_Q_LORA:MLA_Q_LORA + MLA_KV_LORA].astype(BF16)
    wink = jnp.pad(w_in[:, MLA_Q_LORA + MLA_KV_LORA:], ((0, 0), (0, LANES - MLA_ROPE))).astype(BF16)
    wqb = w["mla_w_qb"][mi].reshape(MLA_Q_LORA, MLA_H, MLA_NOPE + MLA_ROPE)
    wqn = wqb[:, :, :MLA_NOPE].reshape(MLA_Q_LORA, MLA_H * MLA_NOPE).astype(BF16)
    wqr = jnp.pad(wqb[:, :, MLA_NOPE:], ((0, 0), (0, 0), (0, LANES - MLA_ROPE))
                  ).reshape(MLA_Q_LORA, MLA_H * LANES).astype(BF16)
    wuk = jnp.transpose(w["mla_w_uk"][mi], (1, 2, 0)).reshape(MLA_H // 2, 2, MLA_NOPE, MLA_KV_LORA)
    wuk_bd = jnp.einsum("pinc,ij->pinjc", wuk, jnp.eye(2, dtype=F32)).reshape(
        MLA_H // 2, 2 * MLA_NOPE, 2 * MLA_KV_LORA).astype(BF16)
    wuv = jnp.transpose(w["mla_w_uv"][mi], (1, 0, 2)).reshape(MLA_H // 2, 2, MLA_KV_LORA, MLA_V)
    wuv_bd = jnp.einsum("picv,ij->picjv", wuv, jnp.eye(2, dtype=F32)).reshape(
        MLA_H // 2, 2 * MLA_KV_LORA, 2 * MLA_V).astype(BF16)
    nwa = jnp.concatenate([nw[0:1], jnp.zeros((7, D_MODEL), F32)])
    nwb = jnp.concatenate([nw[1:2], jnp.zeros((7, D_MODEL), F32)])
    row = lambda g, j: (g * J + j, 0)
    wl = [winq, winc, wink, w["mla_q_norm"][mi][None, :], w["mla_kv_norm"][mi][None, :], wqn, wqr, wuk_bd]
    c, kr, kcat, qcat = pl.pallas_call(
        _mla_proj_body,
        grid=(G, J),
        in_specs=[pl.BlockSpec((R, D_MODEL), row), _const_spec((8, D_MODEL)),
                  pl.BlockSpec((3, R, LANES), lambda g, j: (0, j, 0))] + [_const_spec(a.shape) for a in wl],
        out_specs=[pl.BlockSpec((R, MLA_KV_LORA), row), pl.BlockSpec((R, LANES), row),
                   pl.BlockSpec((R, MLA_QK), row), pl.BlockSpec((MLA_H, R, MLA_QK), lambda g, j: (0, g * J + j, 0))],
        out_shape=[jax.ShapeDtypeStruct((n, MLA_KV_LORA), F32), jax.ShapeDtypeStruct((n, LANES), F32),
                   jax.ShapeDtypeStruct((n, MLA_QK), adt), jax.ShapeDtypeStruct((MLA_H, n, MLA_QK), adt)],
        compiler_params=_params(2),
        name="mla_proj",
    )(x2d, nwa, tab, *wl)

    if paged is None:
        nq = t // MLA_TQ
        rows = MLA_H * MLA_TQ
        o = pl.pallas_call(
            _mla_prompt_body,
            grid=(b, nq),
            in_specs=[pl.BlockSpec((MLA_H, MLA_TQ, MLA_QK), lambda bb, i: (0, bb * nq + i, 0)),
                      pl.BlockSpec((t, MLA_QK), lambda bb, i: (bb, 0))],
            out_specs=pl.BlockSpec((MLA_H, MLA_TQ, MLA_KV_LORA), lambda bb, i: (0, bb * nq + i, 0)),
            out_shape=jax.ShapeDtypeStruct((MLA_H, n, MLA_KV_LORA), BF16),
            scratch_shapes=[pltpu.VMEM((rows, LANES), F32), pltpu.VMEM((rows, LANES), F32),
                            pltpu.VMEM((rows, MLA_KV_LORA), F32)],
            compiler_params=_params(2),
            name="mla_attend_prompt",
        )(qcat, kcat)
    else:
        pages_c, pages_kr, page_table = paged
        page = pages_c.shape[1]
        npg = page_table.shape[1]
        assert npg % MLA_PP == 0
        rows = MLA_H * t

        def page_map(pp):
            return lambda bb, j, pt: (pt[bb, j * MLA_PP + pp], 0, 0)

        grid_spec = pltpu.PrefetchScalarGridSpec(
            num_scalar_prefetch=1,
            grid=(b, npg // MLA_PP),
            in_specs=[pl.BlockSpec((MLA_H, t, MLA_QK), lambda bb, j, pt: (0, bb, 0)),
                      pl.BlockSpec((t, MLA_QK), lambda bb, j, pt: (bb, 0))]
            + [pl.BlockSpec((1, page, MLA_KV_LORA), page_map(pp)) for pp in range(MLA_PP)]
            + [pl.BlockSpec((1, MLA_ROPE, page), page_map(pp)) for pp in range(MLA_PP)],
            out_specs=pl.BlockSpec((MLA_H, t, MLA_KV_LORA), lambda bb, j, pt: (0, bb, 0)),
            scratch_shapes=[pltpu.VMEM((MLA_GROUPS, rows, LANES), F32), pltpu.VMEM((MLA_GROUPS, rows, LANES), F32),
                            pltpu.VMEM((MLA_GROUPS, rows, MLA_KV_LORA), F32)],
        )
        o = pl.pallas_call(
            _mla_sample_body,
            grid_spec=grid_spec,
            out_shape=jax.ShapeDtypeStruct((MLA_H, n, MLA_KV_LORA), adt),
            compiler_params=_params(2),
            name="mla_attend_sample",
        )(page_table, qcat, kcat, *([pages_c] * MLA_PP), *([jnp.swapaxes(pages_kr, 1, 2)] * MLA_PP))

    pre = ("mla", [o], [wuv_bd, w["mla_wo"][mi].astype(BF16)])
    return pre, c.reshape(b, t, MLA_KV_LORA), kr[:, :MLA_ROPE].reshape(b, t, MLA_ROPE)


GDN_CW = 512


def _gdn_proj_body(x_ref, prev_ref, nw_ref, wqkv_ref, wz_ref, wbg_ref, cw_ref, gvec_ref, tri_ref,
                   q_ref, k_ref, v_ref, z_ref, beta_ref, gc_ref, st_ref, carry_ref):
    @pl.when(pl.program_id(1) == 0)
    def _():
        carry_ref[...] = prev_ref[...]

    rows = x_ref.shape[0]
    p = carry_ref.shape[0]
    h = _rms(x_ref[...], nw_ref[0:1, :]).astype(BF16)
    adt = z_ref.dtype
    bg = jnp.dot(h, wbg_ref[...], preferred_element_type=F32)
    beta_ref[...] = _sigmoid(bg)
    g = -jnp.exp(gvec_ref[0:1, :]) * _softplus(bg + gvec_ref[1:2, :])
    gc_ref[...] = _chunk_cumsum(g, tri_ref[...])
    nch = GDN_CONV_DIM // GDN_CW
    zw = GDN_V_DIM // nch

    def up(c):
        return jnp.dot(h, wqkv_ref[:, c * GDN_CW:(c + 1) * GDN_CW], preferred_element_type=F32)

    u_nxt = up(0)
    for c in range(nch):
        sl = slice(c * GDN_CW, (c + 1) * GDN_CW)
        u = u_nxt
        u_nxt = up(c + 1) if c + 1 < nch else None
        z_ref[:, c * zw:(c + 1) * zw] = jnp.dot(h, wz_ref[:, c * zw:(c + 1) * zw],
                                                preferred_element_type=F32).astype(adt)
        prev = carry_ref[:, sl]
        u1 = _shift_rows(u, prev, 1)
        near = cw_ref[3:4, sl] * u + cw_ref[2:3, sl] * u1
        far = cw_ref[1:2, sl] * u + cw_ref[0:1, sl] * u1
        prev1 = pltpu.roll(prev, 1, 0)
        far_prev = cw_ref[1:2, sl] * prev + cw_ref[0:1, sl] * prev1
        y = near + _shift_rows(far, far_prev, 2)
        tail = u[rows - p:, :]
        carry_ref[:, sl] = tail
        st_ref[:, sl] = tail
        y = _silu(y)
        off = c * GDN_CW
        if off < 2 * GDN_QK_DIM:
            dst, base, scale = (q_ref, off, GDN_DK ** -0.5) if off < GDN_QK_DIM else (k_ref, off - GDN_QK_DIM, 1.0)
            for hh in range(GDN_CW // GDN_DK):
                yh = y[:, hh * GDN_DK:(hh + 1) * GDN_DK]
                inv = lax.rsqrt(jnp.sum(yh * yh, axis=-1, keepdims=True) + 1e-6)
                dst[:, base + hh * GDN_DK:base + (hh + 1) * GDN_DK] = (yh * (inv * scale)).astype(adt)
        else:
            v_ref[:, off - 2 * GDN_QK_DIM:off - 2 * GDN_QK_DIM + GDN_CW] = y.astype(adt)


def _gdn_chunk_body(q_ref, k_ref, v_ref, z_ref, gc_ref, beta_ref, s0_ref, nw_ref,
                    o_ref, so_ref, *, nh, chunk, nsub):
    @pl.when(pl.program_id(1) == 0)
    def _():
        so_ref[...] = s0_ref[...]

    ng = GDN_V_H // nh
    gc = nh * chunk
    rep = GDN_V_H // GDN_QK_H
    ri = lax.broadcasted_iota(jnp.int32, (gc, gc), 0)
    ci = lax.broadcasted_iota(jnp.int32, (gc, gc), 1)
    same = (ri // chunk) == (ci // chunk)
    strict = same & ((ri % chunk) > (ci % chunk))
    incl = same & ((ri % chunk) >= (ci % chunk))
    last = same & ((ci % chunk) == chunk - 1)
    eye = (ri == ci).astype(F32)
    row_head = lax.broadcasted_iota(jnp.int32, (gc, GDN_DK), 0) // chunk
    groups = range(ng)
    heads = [[q * nh + i for i in range(nh)] for q in groups]
    keys = [(sc, q) for sc in range(nsub) for q in groups]

    def stack(ref, sc, hds, width):
        parts = [ref[sc * chunk:(sc + 1) * chunk, hd * width:(hd + 1) * width] for hd in hds]
        return parts[0] if len(parts) == 1 else jnp.concatenate(parts, axis=0)

    def col(ref, sc, lanes):
        parts = [ref[sc * chunk:(sc + 1) * chunk, ln:ln + 1] for ln in lanes]
        return parts[0] if len(parts) == 1 else jnp.concatenate(parts, axis=0)

    k_st = {k: stack(k_ref, k[0], [hd // rep for hd in heads[k[1]]], GDN_DK) for k in keys}
    q_st = {k: stack(q_ref, k[0], [hd // rep for hd in heads[k[1]]], GDN_DK) for k in keys}
    v_st = {k: stack(v_ref, k[0], heads[k[1]], GDN_DV) for k in keys}
    gcol = {k: col(gc_ref, k[0], [GDN_V_H + hd for hd in heads[k[1]]]) for k in keys}
    bcol = {k: col(beta_ref, k[0], heads[k[1]]) for k in keys}
    grow = {k: jnp.sum(jnp.where(ri == ci, gcol[k], 0.0), axis=0, keepdims=True) for k in keys}
    k_b = {k: k_st[k].astype(BF16) for k in keys}
    kq = {k: _bdot_nt(jnp.concatenate([k_b[k], q_st[k].astype(BF16)], axis=0), k_b[k]) for k in keys}
    decay = {k: jnp.exp(jnp.where(incl, gcol[k] - grow[k], -jnp.inf)) for k in keys}
    a = {k: jnp.where(strict, kq[k][:gc] * bcol[k] * decay[k], 0.0) for k in keys}
    aqk = {k: jnp.where(incl, kq[k][gc:] * decay[k], 0.0).astype(BF16) for k in keys}

    x = _unit_lower_inverse({k: -a[k] for k in keys}, ri, ci, chunk)

    egc = {k: jnp.exp(gcol[k]) for k in keys}
    uw = {k: _bdot(x[k], jnp.concatenate([v_st[k] * bcol[k], k_st[k] * (bcol[k] * egc[k])], axis=1)) for k in keys}
    glast = {k: jnp.sum(jnp.where(last, grow[k], 0.0), axis=1, keepdims=True) for k in keys}
    kg = {k: k_st[k] * jnp.exp(glast[k] - gcol[k]) for k in keys}
    qg = {k: q_st[k] * egc[k] for k in keys}

    for sc in range(nsub):
        states = [[so_ref[0, hd] for hd in heads[q]] for q in groups]
        wq_s = []
        for q in groups:
            wm = uw[(sc, q)][:, GDN_DV:]
            parts = []
            for i in range(nh):
                rs = slice(i * chunk, (i + 1) * chunk)
                parts.append(_bdot(jnp.concatenate([wm[rs], qg[(sc, q)][rs]], axis=0), states[q][i]))
            wq_s.append(parts)
        v_new, o_st = [], []
        for q in groups:
            ws = jnp.concatenate([m[:chunk] for m in wq_s[q]], axis=0) if nh > 1 else wq_s[q][0][:chunk]
            qs = jnp.concatenate([m[chunk:] for m in wq_s[q]], axis=0) if nh > 1 else wq_s[q][0][chunk:]
            vn = (uw[(sc, q)][:, :GDN_DV] - ws).astype(BF16)
            v_new.append(vn)
            o_st.append(qs + _bdot(aqk[(sc, q)], vn))
        for q in groups:
            for i, hd in enumerate(heads[q]):
                rs = slice(i * chunk, (i + 1) * chunk)
                gl_h = jnp.exp(glast[(sc, q)][i * chunk:i * chunk + 1, :])
                if chunk % 16 == 0:
                    upd = _bdot_tn(kg[(sc, q)][rs], v_new[q][rs])
                else:
                    upd = _bdot_tn(jnp.where(row_head == i, kg[(sc, q)], 0.0), v_new[q])
                so_ref[0, hd] = states[q][i] * gl_h + upd
        for q in groups:
            z_st = stack(z_ref, sc, heads[q], GDN_DV).astype(F32)
            og = (_rms(o_st[q], nw_ref[0:1, :]) * _silu(z_st)).astype(o_ref.dtype)
            for i, hd in enumerate(heads[q]):
                o_ref[sc * chunk:(sc + 1) * chunk, hd * GDN_DV:(hd + 1) * GDN_DV] = og[i * chunk:(i + 1) * chunk]


GDN_SUBCHUNKS = 4


def _gdn_layer(x2d, conv_prev, s0, w, gi, nw, b, t):
    n = b * t
    chunk = _chunk_of(t)
    tl = _tiling(b, t, 256)
    G, J, R, P = tl["G"], tl["J"], tl["R"], tl["P"]
    adt = BF16 if chunk % 16 == 0 else F32
    w_in = w["gdn_w_in"][gi]
    o1 = GDN_CONV_DIM
    o2 = o1 + GDN_V_DIM
    wqkv = w_in[:, :o1].astype(BF16)
    wz = w_in[:, o1:o2].astype(BF16)
    wbg = jnp.pad(w_in[:, o2:], ((0, 0), (0, LANES - 2 * GDN_V_H))).astype(BF16)
    cw = jnp.pad(w["gdn_conv_w"][gi], ((0, 8 - GDN_CONV), (0, 0)))
    gvec = jnp.zeros((8, LANES), F32)
    gvec = gvec.at[0, GDN_V_H:2 * GDN_V_H].set(w["gdn_a_log"][gi]).at[1, GDN_V_H:2 * GDN_V_H].set(w["gdn_dt_bias"][gi])
    bc = chunk if chunk == 64 else R
    tri = _chunk_masks(chunk, bc)
    nwa = jnp.concatenate([nw[0:1], jnp.zeros((7, D_MODEL), F32)])
    row = lambda g, j: (g * J + j, 0)
    st_spec = pl.BlockSpec((P, GDN_CONV_DIM), lambda g, j: (g, 0))
    qn, kn, v, z, beta, gcs, st = pl.pallas_call(
        _gdn_proj_body,
        grid=(G, J),
        in_specs=[pl.BlockSpec((R, D_MODEL), row), st_spec, _const_spec((8, D_MODEL)), _const_spec(wqkv.shape),
                  _const_spec(wz.shape), _const_spec(wbg.shape), _const_spec(cw.shape), _const_spec(gvec.shape),
                  _const_spec(tri.shape)],
        out_specs=[pl.BlockSpec((R, GDN_QK_DIM), row), pl.BlockSpec((R, GDN_QK_DIM), row),
                   pl.BlockSpec((R, GDN_V_DIM), row), pl.BlockSpec((R, GDN_V_DIM), row),
                   pl.BlockSpec((R, LANES), row), pl.BlockSpec((R, LANES), row), st_spec],
        out_shape=[jax.ShapeDtypeStruct((n, GDN_QK_DIM), adt), jax.ShapeDtypeStruct((n, GDN_QK_DIM), adt),
                   jax.ShapeDtypeStruct((n, GDN_V_DIM), adt), jax.ShapeDtypeStruct((n, GDN_V_DIM), adt),
                   jax.ShapeDtypeStruct((n, LANES), F32), jax.ShapeDtypeStruct((n, LANES), F32),
                   jax.ShapeDtypeStruct((b * SUBLANES, GDN_CONV_DIM), F32)],
        scratch_shapes=[pltpu.VMEM((P, GDN_CONV_DIM), F32)],
        compiler_params=_params(2),
        name="gdn_proj",
    )(x2d, _pad_state(conv_prev), nwa, wqkv, wz, wbg, cw, gvec, tri)
    conv_new = st.reshape(b, SUBLANES, GDN_CONV_DIM)[:, SUBLANES - (GDN_CONV - 1):]

    nh = GROUP_ROWS // chunk
    ng = GDN_V_H // nh
    nsub = GDN_SUBCHUNKS if (t // chunk) % GDN_SUBCHUNKS == 0 else 1
    nct = t // (chunk * nsub)
    br = chunk * nsub
    crow = lambda bb, j: (bb * nct + j, 0)
    sspec = pl.BlockSpec((1, GDN_V_H, GDN_DK, GDN_DV), lambda bb, j: (bb, 0, 0, 0))
    nwn = jnp.concatenate([w["gdn_norm_w"][gi][None, :], jnp.zeros((7, GDN_DV), F32)])
    o, s_new = pl.pallas_call(
        functools.partial(_gdn_chunk_body, nh=nh, chunk=chunk, nsub=nsub),
        grid=(b, nct),
        in_specs=[pl.BlockSpec((br, GDN_QK_DIM), crow), pl.BlockSpec((br, GDN_QK_DIM), crow),
                  pl.BlockSpec((br, GDN_V_DIM), crow), pl.BlockSpec((br, GDN_V_DIM), crow),
                  pl.BlockSpec((br, LANES), crow), pl.BlockSpec((br, LANES), crow),
                  sspec, _const_spec((8, GDN_DV))],
        out_specs=[pl.BlockSpec((br, GDN_V_DIM), crow), sspec],
        out_shape=[jax.ShapeDtypeStruct((n, GDN_V_DIM), adt),
                   jax.ShapeDtypeStruct((b, GDN_V_H, GDN_DK, GDN_DV), F32)],
        compiler_params=_params(2),
        name="gdn_chunk",
    )(qn, kn, v, z, gcs, beta, s0, nwn)
    pre = ("plain", [o], [w["gdn_wo"][gi].astype(BF16)])
    return pre, conv_new, s_new


def _trunk(x, pos, rw_s, rw_shift, gdn_s, gdn_conv, ffn_conv, w, paged):
    b, t, _ = x.shape
    x2d = x.reshape(b * t, D_MODEL)
    new = {k: [] for k in ("rw_S", "rw_shift", "mla_c", "mla_kr", "gdn_S", "gdn_conv", "ffn_conv")}
    v_first = None
    ri = mi = gi = 0
    for l, kind in enumerate(LAYER_MIXER):
        nw = w["norm_w"][l]
        if kind == 0:
            pre, sh, s_new, v_first = _rwkv_layer(x2d, rw_shift[ri], rw_s[ri], v_first, w, ri, nw, b, t)
            new["rw_S"].append(s_new)
            new["rw_shift"].append(sh)
            ri += 1
        elif kind == 1:
            pre, c, kr = _mla_layer(x2d, pos, w, mi, nw, b, t, None if paged is None else
                                    (paged[0][mi], paged[1][mi], paged[2]))
            new["mla_c"].append(c)
            new["mla_kr"].append(kr)
            mi += 1
        else:
            pre, cb, s_new = _gdn_layer(x2d, gdn_conv[gi], gdn_s[gi], w, gi, nw, b, t)
            new["gdn_S"].append(s_new)
            new["gdn_conv"].append(cb)
            gi += 1
        nwf = jnp.concatenate([nw[2:4], nw[1:2], jnp.zeros((5, D_MODEL), F32)])
        cwb = jnp.concatenate([w["ffn_conv_w"][l], w["ffn_conv_b"][l][None, :],
                               jnp.zeros((8 - FFN_CONV - 1, 2 * D_FF), F32)])
        x2d, st = _ffn(x2d, pre, _pad_state(ffn_conv[l]), nwf, w["ffn_w_up"][l].astype(BF16), cwb,
                       w["ffn_w_down"][l].astype(BF16), b, t)
        new["ffn_conv"].append(st.reshape(b, SUBLANES, 2 * D_FF)[:, SUBLANES - (FFN_CONV - 1):])
    return x2d.reshape(b, t, D_MODEL), {k: jnp.stack(v) for k, v in new.items()}


def kernel(x_prompt, x_sample, state_rwkv_wkv, state_rwkv_shift, cache_mla_latent, cache_mla_krope, state_gdn_S, state_gdn_conv, state_ffn_conv, page_table, norm_w, rw_mu, rw_wrkv, rw_w0, rw_w1, rw_w2, rw_a0, rw_a1, rw_a2, rw_v0, rw_v1, rw_v2, rw_g1, rw_g2, rw_kk, rw_ka, rw_rk, rw_lnx_w, rw_lnx_b, rw_wo, mla_w_in, mla_q_norm, mla_kv_norm, mla_w_qb, mla_w_uk, mla_w_uv, mla_wo, gdn_w_in, gdn_conv_w, gdn_a_log, gdn_dt_bias, gdn_norm_w, gdn_wo, ffn_w_up, ffn_conv_w, ffn_conv_b, ffn_w_down):
    w = dict(norm_w=norm_w, rw_mu=rw_mu, rw_wrkv=rw_wrkv, rw_w0=rw_w0, rw_w1=rw_w1, rw_w2=rw_w2, rw_a0=rw_a0,
             rw_a1=rw_a1, rw_a2=rw_a2, rw_v0=rw_v0, rw_v1=rw_v1, rw_v2=rw_v2, rw_g1=rw_g1, rw_g2=rw_g2,
             rw_kk=rw_kk, rw_ka=rw_ka, rw_rk=rw_rk, rw_lnx_w=rw_lnx_w, rw_lnx_b=rw_lnx_b, rw_wo=rw_wo,
             mla_w_in=mla_w_in, mla_q_norm=mla_q_norm, mla_kv_norm=mla_kv_norm, mla_w_qb=mla_w_qb,
             mla_w_uk=mla_w_uk, mla_w_uv=mla_w_uv, mla_wo=mla_wo, gdn_w_in=gdn_w_in, gdn_conv_w=gdn_conv_w,
             gdn_a_log=gdn_a_log, gdn_dt_bias=gdn_dt_bias, gdn_norm_w=gdn_norm_w, gdn_wo=gdn_wo,
             ffn_w_up=ffn_w_up, ffn_conv_w=ffn_conv_w, ffn_conv_b=ffn_conv_b, ffn_w_down=ffn_w_down)
    b, t = x_prompt.shape[0], x_prompt.shape[1]
    n_rw, n_gdn, depth = state_rwkv_wkv.shape[0], state_gdn_S.shape[0], state_ffn_conv.shape[0]
    y_p, sp = _trunk(
        x_prompt, jnp.arange(t),
        jnp.zeros((n_rw, b) + state_rwkv_wkv.shape[2:], F32), jnp.zeros((n_rw, b, D_MODEL), F32),
        jnp.zeros((n_gdn, b) + state_gdn_S.shape[2:], F32), jnp.zeros((n_gdn, b) + state_gdn_conv.shape[2:], F32),
        jnp.zeros((depth, b) + state_ffn_conv.shape[2:], F32), w, None)
    past_len = page_table.shape[1] * cache_mla_latent.shape[2]
    pos_s = past_len + jnp.arange(x_sample.shape[1])
    y_s, ss = _trunk(x_sample, pos_s, state_rwkv_wkv, state_rwkv_shift, state_gdn_S, state_gdn_conv,
                     state_ffn_conv, w, (cache_mla_latent, cache_mla_krope, page_table))
    names = ("rw_S", "rw_shift", "mla_c", "mla_kr", "gdn_S", "gdn_conv", "ffn_conv")
    return (y_p, y_s) + tuple(sp[k] for k in names) + tuple(ss[k] for k in names)
```

```python
import functools

import jax
import jax.numpy as jnp
from jax import lax
from jax.experimental import pallas as pl
from jax.experimental.pallas import tpu as pltpu

F32 = jnp.float32
BF16 = jnp.bfloat16
HIGHEST = lax.Precision.HIGHEST

D_MODEL = 1024
NORM_EPS = 1e-6
RW_N = 64
RW_H = D_MODEL // RW_N
RW_LNX_EPS = 64e-5
MLA_H = 16
MLA_NOPE = 64
MLA_ROPE = 32
MLA_V = 64
MLA_Q_LORA = 512
MLA_KV_LORA = 256
MLA_SCALE = (MLA_NOPE + MLA_ROPE) ** -0.5
ROPE_THETA = 10000.0
MLA_QK = MLA_KV_LORA + 128
GDN_QK_H = 8
GDN_V_H = 16
GDN_DK = 128
GDN_DV = 128
GDN_QK_DIM = GDN_QK_H * GDN_DK
GDN_V_DIM = GDN_V_H * GDN_DV
GDN_CONV_DIM = 2 * GDN_QK_DIM + GDN_V_DIM
GDN_CONV = 4
D_FF = 2816
FFN_CONV = 3
LAYER_MIXER = (0, 1, 2, 0)

SUBLANES = 8
LANES = 128
GROUP_ROWS = 128
VMEM_LIMIT = 56 * 1024 * 1024


def _rms(x, w):
    return x * lax.rsqrt(jnp.mean(x * x, axis=-1, keepdims=True) + NORM_EPS) * w


def _bdot(a, b):
    return jnp.dot(a.astype(BF16), b.astype(BF16), preferred_element_type=F32)


def _bdot_nt(a, b):
    return lax.dot_general(a.astype(BF16), b.astype(BF16), (((1,), (1,)), ((), ())),
                           preferred_element_type=F32)


def _bdot_tn(a, b):
    return lax.dot_general(a.astype(BF16), b.astype(BF16), (((0,), (0,)), ((), ())),
                           preferred_element_type=F32)


def _hdot(a, b):
    return jnp.dot(a, b, precision=HIGHEST, preferred_element_type=F32)


def _sigmoid(x):
    return 1.0 / (1.0 + jnp.exp(-x))


def _softplus(x):
    return jnp.maximum(x, 0.0) + jnp.log(1.0 + jnp.exp(-jnp.abs(x)))


def _silu(x):
    return x * _sigmoid(x)


def _shift_rows(u, prev, s):
    rows, cols = u.shape
    p = prev.shape[0]
    rolled = pltpu.roll(u, s, 0)
    fix = pltpu.roll(prev, (p - SUBLANES + s) % p, 0)
    t = lax.broadcasted_iota(jnp.int32, (p, cols), 0) % SUBLANES
    if p == rows:
        return jnp.where(t < s, fix, rolled)
    head = jnp.where(t < s, fix, rolled[:SUBLANES])
    return jnp.concatenate([head, rolled[SUBLANES:]], axis=0)


def _lane_group_sum(x, ones2):
    parts = []
    for i in range(x.shape[1] // LANES):
        xs = x[:, i * LANES:(i + 1) * LANES]
        hi = xs.astype(BF16)
        lo = (xs - hi.astype(F32)).astype(BF16)
        parts.append(jnp.dot(jnp.concatenate([hi, lo], axis=1), ones2, preferred_element_type=F32))
    return parts[0] if len(parts) == 1 else jnp.concatenate(parts, axis=1)


def _split_dot(m2, x):
    hi = x.astype(BF16)
    lo = (x - hi.astype(F32)).astype(BF16)
    return jnp.dot(m2, jnp.concatenate([hi, lo], axis=0), preferred_element_type=F32)


def _chunk_cumsum(x, tri):
    bc = tri.shape[0]
    parts = [_hdot(tri, x[i * bc:(i + 1) * bc]) for i in range(x.shape[0] // bc)]
    return parts[0] if len(parts) == 1 else jnp.concatenate(parts, axis=0)


def _unit_lower_inverse(n_mats, ri, ci, chunk):
    keys = list(n_mats)
    eye = (ri == ci).astype(F32)
    base = min(chunk, SUBLANES)
    same = (ri // base) == (ci // base)
    n0 = {k: jnp.where(same, n_mats[k], 0.0) for k in keys}
    x = {k: eye + n0[k] for k in keys}
    p = {k: n0[k].astype(BF16) for k in keys}
    span = 2
    while span < base:
        p = {k: _bdot(p[k], p[k]).astype(BF16) for k in keys}
        x = {k: x[k] + _bdot(p[k], x[k]) for k in keys}
        span *= 2
    size = base
    while size < chunk:
        pair = ((ri // (2 * size)) == (ci // (2 * size))) & ((ri // size) != (ci // size))
        xb = {k: x[k].astype(BF16) for k in keys}
        t = {k: _bdot(jnp.where(pair, n_mats[k], 0.0), xb[k]).astype(BF16) for k in keys}
        x = {k: x[k] + _bdot(xb[k], t[k]) for k in keys}
        size *= 2
    return x


def _tiling(b, t, tt_max):
    if t == SUBLANES:
        return dict(G=1, J=1, R=b * t, P=b * t)
    tt = min(t, tt_max)
    assert t % tt == 0 and tt % 64 == 0, (t, tt)
    return dict(G=b, J=t // tt, R=tt, P=SUBLANES)


def _chunk_of(t):
    return 64 if t % 64 == 0 else t


def _const_spec(shape):
    nd = len(shape)
    return pl.BlockSpec(shape, lambda *_: (0,) * nd, pipeline_mode=pl.Buffered(1))


def _params(n_axes):
    return pltpu.CompilerParams(dimension_semantics=("arbitrary",) * n_axes,
                                vmem_limit_bytes=VMEM_LIMIT)


def _pad_state(st):
    b, k1, c = st.shape
    return jnp.pad(st, ((0, 0), (SUBLANES - k1, 0), (0, 0))).reshape(b * SUBLANES, c)


def _chunk_masks(chunk, rows):
    i = jnp.arange(rows)
    same = (i[:, None] // chunk) == (i[None, :] // chunk)
    tri = same & ((i[None, :] % chunk) <= (i[:, None] % chunk))
    return tri.astype(F32)


FFN_CW = 256


ROW_BLOCK = 64


def _stage(buf_ref, u, carry_ref, st_ref, sl, taps):
    rows = u.shape[0]
    hb = taps * SUBLANES
    buf_ref[hb:hb + rows, :] = u
    first = lax.broadcasted_iota(jnp.int32, (SUBLANES, u.shape[1]), 0) == 0
    for i in range(taps):
        back = taps - i
        src = u[rows - back * SUBLANES:rows - (back - 1) * SUBLANES, :]
        crow = carry_ref[SUBLANES - back:SUBLANES - back + 1, sl]
        buf_ref[i * SUBLANES:(i + 1) * SUBLANES, :] = jnp.where(first, crow, pltpu.roll(src, 1, 0))
    sq = SUBLANES * SUBLANES
    tail = pltpu.einshape("(vs)d->(sv)d", u[rows - sq:, :], s=SUBLANES)[sq - SUBLANES:, :]
    carry_ref[:, sl] = tail
    st_ref[:, sl] = tail


def _taps(buf_ref, r0, nrows, taps):
    hb = taps * SUBLANES
    cur = buf_ref[hb + r0:hb + r0 + nrows, :]
    return cur, [buf_ref[hb - j * SUBLANES + r0:hb - j * SUBLANES + r0 + nrows, :] for j in range(taps, 0, -1)]


FFN_PARTS = 1


def _mixer_out(kind, refs, rs):
    if kind == "mla":
        o_ref, wuv_ref, wo_ref = refs
        parts = []
        for pr in range(MLA_H // 2):
            wp = wuv_ref[pr]
            parts.append(jnp.dot(o_ref[2 * pr, rs, :].astype(BF16), wp[:MLA_KV_LORA], preferred_element_type=F32)
                         + jnp.dot(o_ref[2 * pr + 1, rs, :].astype(BF16), wp[MLA_KV_LORA:],
                                   preferred_element_type=F32))
        y = jnp.concatenate(parts, axis=1)
    elif kind == "gated":
        y_ref, g_ref, wo_ref = refs
        y = y_ref[rs, :].astype(F32) * g_ref[rs, :].astype(F32)
    else:
        y_ref, wo_ref = refs
        y = y_ref[rs, :]
    return jnp.dot(y.astype(BF16), wo_ref[...], preferred_element_type=F32)


def _ffn_body(*refs, kind, n_pre):
    x_ref = refs[0]
    pre_refs = refs[1:1 + n_pre]
    (prev_ref, nw_ref, wup_ref, cwb_ref, wdn_ref, xo_ref, st_ref, carry_ref, act_ref, buf_ref) = refs[1 + n_pre:]

    @pl.when(pl.program_id(1) == 0)
    def _():
        carry_ref[...] = prev_ref[...]

    rows = x_ref.shape[0]
    stacked = carry_ref.shape[0] == rows
    nparts = 1 if stacked else FFN_PARTS
    prows = rows // nparts
    parts = [slice(i * prows, (i + 1) * prows) for i in range(nparts)]
    nch = D_FF // FFN_CW

    def cols(c, half):
        return slice(half * D_FF + c * FFN_CW, half * D_FF + (c + 1) * FFN_CW)

    def conv_gate(c, taps):
        ys = []
        for half in range(2):
            sl = cols(c, half)
            u, (u2, u1) = taps[half]
            ys.append(cwb_ref[0:1, sl] * u2 + cwb_ref[1:2, sl] * u1 + cwb_ref[2:3, sl] * u + cwb_ref[3:4, sl])
        return (_silu(ys[0]) * ys[1]).astype(BF16)

    xs = [x_ref[rs, :] + _rms(_mixer_out(kind, pre_refs, rs), nw_ref[2:3, :]) for rs in parts]
    for pi, rs in enumerate(parts):
        x = xs[pi] if stacked else pltpu.einshape("(sv)d->(vs)d", xs[pi], s=SUBLANES)
        h = _rms(x, nw_ref[0:1, :]).astype(BF16)

        def up(c):
            us = [jnp.dot(h, wup_ref[:, cols(c, half)], preferred_element_type=F32) for half in range(2)]
            if stacked:
                return us
            for half in range(2):
                _stage(buf_ref.at[(c % 2) * 2 + half], us[half], carry_ref, st_ref, cols(c, half), FFN_CONV - 1)
            return None

        u_cur = up(0)
        for c in range(nch):
            u_nxt = up(c + 1) if c + 1 < nch else None
            csl = slice(c * FFN_CW, (c + 1) * FFN_CW)
            if stacked:
                taps = []
                for half in range(2):
                    sl = cols(c, half)
                    u = u_cur[half]
                    prev = carry_ref[:, sl]
                    taps.append((u, [_shift_rows(u, prev, 2), _shift_rows(u, prev, 1)]))
                    carry_ref[:, sl] = u
                    st_ref[:, sl] = u
                act_ref[:, csl] = conv_gate(c, taps)
            else:
                for r0 in range(0, prows, ROW_BLOCK):
                    taps = [_taps(buf_ref.at[(c % 2) * 2 + half], r0, ROW_BLOCK, FFN_CONV - 1) for half in range(2)]
                    act_ref[rs.start + r0:rs.start + r0 + ROW_BLOCK, csl] = conv_gate(c, taps)
            u_cur = u_nxt
        f = jnp.dot(act_ref[rs, :], wdn_ref[...], preferred_element_type=F32)
        out = x + _rms(f, nw_ref[1:2, :])
        xo_ref[rs, :] = out if stacked else pltpu.einshape("(vs)d->(sv)d", out, s=SUBLANES)


def _ffn(x2d, pre, prev, nw, wup, cwb, wdn, b, t):
    tl = _tiling(b, t, 512)
    G, J, R, P = tl["G"], tl["J"], tl["R"], tl["P"]
    n = b * t
    kind, acts, wts = pre
    row = lambda g, j: (g * J + j, 0)
    if kind == "mla":
        act_specs = [pl.BlockSpec((MLA_H, R, MLA_KV_LORA), lambda g, j: (0, g * J + j, 0))]
    else:
        act_specs = [pl.BlockSpec((R, a.shape[1]), row) for a in acts]
    pre_specs = act_specs + [_const_spec(wt.shape) for wt in wts]
    return pl.pallas_call(
        functools.partial(_ffn_body, kind=kind, n_pre=len(pre_specs)),
        grid=(G, J),
        in_specs=[pl.BlockSpec((R, D_MODEL), row)] + pre_specs + [
            pl.BlockSpec((P, 2 * D_FF), lambda g, j: (g, 0)),
            _const_spec((8, D_MODEL)),
            _const_spec((D_MODEL, 2 * D_FF)),
            _const_spec((8, 2 * D_FF)),
            _const_spec((D_FF, D_MODEL)),
        ],
        out_specs=[
            pl.BlockSpec((R, D_MODEL), lambda g, j: (g * J + j, 0)),
            pl.BlockSpec((P, 2 * D_FF), lambda g, j: (g, 0)),
        ],
        out_shape=[jax.ShapeDtypeStruct((n, D_MODEL), F32),
                   jax.ShapeDtypeStruct((b * SUBLANES, 2 * D_FF), F32)],
        scratch_shapes=[pltpu.VMEM((P, 2 * D_FF), F32), pltpu.VMEM((R, D_FF), BF16),
                        pltpu.VMEM((4, R + (FFN_CONV - 1) * SUBLANES, FFN_CW), F32)],
        compiler_params=_params(2),
        name="conv_ffn",
    )(x2d, *acts, *wts, prev, nw, wup, cwb, wdn)


def _rwkv_proj_body(*refs, has_vres, chunk):
    it = iter(refs)
    x_ref, prev_ref = next(it), next(it)
    vf_ref = next(it) if has_vres else None
    vec_ref, wrkv_ref, w1_ref, w2_ref, a1_ref, a2_ref = (next(it) for _ in range(6))
    v1_ref, v2_ref = (next(it), next(it)) if has_vres else (None, None)
    g1_ref, g2_ref, tri_ref, ones_ref = (next(it) for _ in range(4))
    rt_ref, kt_ref, at_ref, bt_ref, v_ref, g_ref, gl_ref, hl_ref, carry_ref = (next(it) for _ in range(9))

    @pl.when(pl.program_id(1) == 0)
    def _():
        carry_ref[...] = prev_ref[...]

    x = x_ref[...]
    rows = x.shape[0]
    p = carry_ref.shape[0]
    h = _rms(x, vec_ref[10:11, :])
    d = _shift_rows(h, carry_ref[...], 1) - h
    tail = h[rows - p:, :]
    carry_ref[...] = tail
    hl_ref[...] = tail

    def mix(i):
        return (h + d * vec_ref[i:i + 1, :]).astype(BF16)

    r = jnp.dot(mix(0), wrkv_ref[0], preferred_element_type=F32)
    k = jnp.dot(mix(1), wrkv_ref[1], preferred_element_type=F32)
    xv = mix(2)
    v = jnp.dot(xv, wrkv_ref[2], preferred_element_type=F32)
    w_lora = _bdot(jnp.tanh(_bdot(mix(3), w1_ref[...])), w2_ref[...])
    v_lora = _bdot(_bdot(xv, v1_ref[...]), v2_ref[...]) if has_vres else None
    a_lora = _bdot(_bdot(mix(4), a1_ref[...]), a2_ref[...])
    g_ref[...] = _bdot(_sigmoid(_bdot(mix(5), g1_ref[...])), g2_ref[...]).astype(g_ref.dtype)
    adt = rt_ref.dtype

    bc = tri_ref.shape[0]
    for r0 in range(0, rows, bc):
        for l0 in range(0, D_MODEL, PROJ_LANES):
            rs, ls = slice(r0, r0 + bc), slice(l0, l0 + PROJ_LANES)
            vb = v[rs, ls]
            if has_vres:
                vb = vb + (vf_ref[rs, ls] - vb) * _sigmoid(vec_ref[11:12, ls] + v_lora[rs, ls])
            v_ref[rs, ls] = vb.astype(adt)
            a = _sigmoid(vec_ref[7:8, ls] + a_lora[rs, ls])
            kb = k[rs, ls]
            kk = kb * vec_ref[8:9, ls]
            kk = kk * lax.rsqrt(_lane_group_sum(kk * kk, ones_ref[...]) + 1e-6)
            kb = kb * (1.0 + (a - 1.0) * vec_ref[9:10, ls])
            w = -_softplus(-(vec_ref[6:7, ls] + w_lora[rs, ls])) - 0.5
            lw = -jnp.exp(w)
            cum = _split_dot(tri_ref[...], lw)
            e_bwd = jnp.exp(-cum)
            rt_ref[rs, ls] = (r[rs, ls] * jnp.exp(cum)).astype(adt)
            kt_ref[rs, ls] = (kb * e_bwd).astype(adt)
            at_ref[rs, ls] = (-kk * jnp.exp(cum - lw)).astype(adt)
            bt_ref[rs, ls] = (kk * a * e_bwd).astype(adt)
            for c in range(bc // chunk):
                row = (c + 1) * chunk - 1
                gl_ref[r0 // chunk + c, :, ls] = jnp.exp(cum[row:row + 1, :])


def _rwkv_scan_body(rt_ref, kt_ref, at_ref, bt_ref, v_ref, gl_ref, h0_ref, vec_ref, y_ref, ho_ref,
                    *, nh, chunk, nsub, per_seq):
    @pl.when(pl.program_id(1) == 0)
    def _():
        ho_ref[...] = h0_ref[...]

    gl_lanes = nh * RW_N
    ng = RW_H // nh
    gc = nh * chunk
    row_head = lax.broadcasted_iota(jnp.int32, (gc, gl_lanes), 0) // chunk
    lane_head = lax.broadcasted_iota(jnp.int32, (gc, gl_lanes), 1) // RW_N
    own = row_head == lane_head
    ri = lax.broadcasted_iota(jnp.int32, (gc, gc), 0)
    ci = lax.broadcasted_iota(jnp.int32, (gc, gc), 1)
    same = (ri // chunk) == (ci // chunk)
    strict = same & ((ri % chunk) > (ci % chunk))
    incl = same & ((ri % chunk) >= (ci % chunk))
    eye = (ri == ci).astype(F32)
    eye_l = (lax.broadcasted_iota(jnp.int32, (gl_lanes, gl_lanes), 0)
             == lax.broadcasted_iota(jnp.int32, (gl_lanes, gl_lanes), 1))
    merged = gc == GROUP_ROWS
    groups = range(ng)
    sls = [slice(q * gl_lanes, (q + 1) * gl_lanes) for q in groups]
    keys = [(sc, q) for sc in range(nsub) for q in groups]

    def blockdiag(ref, key):
        xg = ref[key[0] * chunk:(key[0] + 1) * chunk, sls[key[1]]]
        xx = jnp.concatenate([xg] * nh, axis=0) if nh > 1 else xg
        return jnp.where(own, xx, jnp.zeros_like(xx))

    r_bd = {k: blockdiag(rt_ref, k) for k in keys}
    k_bd = {k: blockdiag(kt_ref, k) for k in keys}
    a_bd = {k: blockdiag(at_ref, k) for k in keys}
    b_bd = {k: blockdiag(bt_ref, k) for k in keys}
    v_f = {k: blockdiag(v_ref, k) for k in keys}
    v_bd = {k: v_f[k].astype(BF16) for k in keys}
    bonus = {k: jnp.sum(r_bd[k].astype(F32) * k_bd[k] * vec_ref[2:3, sls[k[1]]], axis=1, keepdims=True)
             for k in keys}
    if merged:
        ar = {k: jnp.concatenate([a_bd[k], r_bd[k]], axis=0).astype(BF16) for k in keys}
        bk = {k: jnp.concatenate([b_bd[k], k_bd[k]], axis=0).astype(BF16) for k in keys}
        amat = {k: _bdot_nt(ar[k], bk[k]) for k in keys}
        a_ab = {k: jnp.where(strict, amat[k][:gc, :gc], 0.0) for k in keys}
        a_ak = {k: jnp.where(strict, amat[k][:gc, gc:], 0.0).astype(BF16) for k in keys}
        a_rbk = {k: jnp.concatenate([jnp.where(incl, amat[k][gc:, :gc], 0.0),
                                     jnp.where(incl, amat[k][gc:, gc:], 0.0)], axis=1).astype(BF16) for k in keys}
    else:
        ab_ = {k: a_bd[k].astype(BF16) for k in keys}
        rb_ = {k: r_bd[k].astype(BF16) for k in keys}
        bb_ = {k: b_bd[k].astype(BF16) for k in keys}
        kb_ = {k: k_bd[k].astype(BF16) for k in keys}
        a_ab = {k: jnp.where(strict, _bdot_nt(ab_[k], bb_[k]), 0.0) for k in keys}
        a_ak = {k: jnp.where(strict, _bdot_nt(ab_[k], kb_[k]), 0.0).astype(BF16) for k in keys}
        a_rb = {k: jnp.where(incl, _bdot_nt(rb_[k], bb_[k]), 0.0).astype(BF16) for k in keys}
        a_rk = {k: jnp.where(incl, _bdot_nt(rb_[k], kb_[k]), 0.0).astype(BF16) for k in keys}
    akv = {k: _bdot(a_ak[k], v_bd[k]) for k in keys}

    x = _unit_lower_inverse(a_ab, ri, ci, chunk)
    tinv = {k: x[k].astype(BF16) for k in keys}

    for sc in range(nsub):
        slot = sc if per_seq else 0
        hs = [ho_ref[slot, q] for q in groups]
        hs_b = [h.astype(BF16) for h in hs]
        gl_rows = [gl_ref[sc, :, sl] for sl in sls]
        if merged:
            arh = [_bdot(ar[(sc, q)], hs_b[q]) for q in groups]
            u = [_bdot(tinv[(sc, q)], arh[q][:gc] + akv[(sc, q)]).astype(BF16) for q in groups]
            uv = [jnp.concatenate([u[q], v_bd[(sc, q)]], axis=0) for q in groups]
            y_bd = [arh[q][gc:] + _bdot(a_rbk[(sc, q)], uv[q]) for q in groups]
            for q in groups:
                gl_col = jnp.sum(jnp.where(eye_l, gl_rows[q], 0.0), axis=1, keepdims=True)
                bk_g = jnp.concatenate([b_bd[(sc, q)] * gl_rows[q], k_bd[(sc, q)] * gl_rows[q]], axis=0)
                ho_ref[slot, q] = hs[q] * gl_col + _bdot_tn(bk_g, uv[q])
        else:
            ah = [_bdot(ab_[(sc, q)], hs_b[q]) for q in groups]
            rh = [_bdot(rb_[(sc, q)], hs_b[q]) for q in groups]
            u = [_bdot(tinv[(sc, q)], ah[q] + akv[(sc, q)]).astype(BF16) for q in groups]
            y_bd = [rh[q] + _bdot(a_rb[(sc, q)], u[q]) + _bdot(a_rk[(sc, q)], v_bd[(sc, q)]) for q in groups]
            for q in groups:
                gl_col = jnp.sum(jnp.where(eye_l, gl_rows[q], 0.0), axis=1, keepdims=True)
                ho_ref[slot, q] = (hs[q] * gl_col + _bdot_tn(b_bd[(sc, q)] * gl_rows[q], u[q])
                                   + _bdot_tn(k_bd[(sc, q)] * gl_rows[q], v_bd[(sc, q)]))

        for q in groups:
            sl = sls[q]
            mu = jnp.sum(y_bd[q], axis=1, keepdims=True) * (1.0 / RW_N)
            yc = jnp.where(own, y_bd[q] - mu, 0.0)
            var = jnp.sum(yc * yc, axis=1, keepdims=True) * (1.0 / RW_N)
            tot = (yc * lax.rsqrt(var + RW_LNX_EPS) * vec_ref[0:1, sl] + jnp.where(own, vec_ref[1:2, sl], 0.0)
                   + bonus[(sc, q)] * v_f[(sc, q)])
            y = tot[0:chunk]
            for hh in range(1, nh):
                y = y + tot[hh * chunk:(hh + 1) * chunk]
            y_ref[sc * chunk:(sc + 1) * chunk, sl] = y.astype(y_ref.dtype)


RWKV_SUBCHUNKS = 4
PROJ_LANES = 256


def _rwkv_layer(x2d, shift_prev, s0, v_first, w, ri, nw, b, t):
    n = b * t
    chunk = _chunk_of(t)
    tl = _tiling(b, t, 512)
    G, J, R, P = tl["G"], tl["J"], tl["R"], tl["P"]
    has_vres = v_first is not None
    vi = ri - 1
    adt = BF16 if chunk % 16 == 0 else F32
    bc = chunk if chunk == 64 else R
    tri = _chunk_masks(chunk, bc).astype(BF16)
    tri = jnp.concatenate([tri, tri], axis=1)
    li = jnp.arange(LANES)
    ones_bd = ((li[:, None] // RW_N) == (li[None, :] // RW_N)).astype(BF16)
    ones_bd = jnp.concatenate([ones_bd, ones_bd], axis=0)
    zero = jnp.zeros((D_MODEL,), F32)
    vec = jnp.stack([*(w["rw_mu"][ri][i] for i in range(6)), w["rw_w0"][ri], w["rw_a0"][ri], w["rw_kk"][ri],
                     w["rw_ka"][ri], nw[0], w["rw_v0"][vi] if has_vres else zero, zero, zero, zero, zero])
    row = lambda g, j: (g * J + j, 0)
    row_spec = pl.BlockSpec((R, D_MODEL), row)
    ins = [x2d, _pad_state(shift_prev[:, None, :])]
    specs = [row_spec, pl.BlockSpec((P, D_MODEL), lambda g, j: (g, 0))]
    if has_vres:
        ins.append(v_first)
        specs.append(row_spec)
    wl = [vec, w["rw_wrkv"][ri].astype(BF16), w["rw_w1"][ri].astype(BF16), w["rw_w2"][ri].astype(BF16),
          w["rw_a1"][ri].astype(BF16), w["rw_a2"][ri].astype(BF16)]
    if has_vres:
        wl += [w["rw_v1"][vi].astype(BF16), w["rw_v2"][vi].astype(BF16)]
    wl += [w["rw_g1"][ri].astype(BF16), w["rw_g2"][ri].astype(BF16), tri, ones_bd]
    ins += wl
    specs += [_const_spec(a.shape) for a in wl]
    nc_tile = R // chunk
    outs = pl.pallas_call(
        functools.partial(_rwkv_proj_body, has_vres=has_vres, chunk=chunk),
        grid=(G, J),
        in_specs=specs,
        out_specs=[row_spec] * 6 + [pl.BlockSpec((nc_tile, 1, D_MODEL), lambda g, j: (g * J + j, 0, 0)),
                                    pl.BlockSpec((P, D_MODEL), lambda g, j: (g, 0))],
        out_shape=[jax.ShapeDtypeStruct((n, D_MODEL), adt)] * 6
        + [jax.ShapeDtypeStruct((n // chunk, 1, D_MODEL), F32), jax.ShapeDtypeStruct((b * SUBLANES, D_MODEL), F32)],
        scratch_shapes=[pltpu.VMEM((P, D_MODEL), F32)],
        compiler_params=_params(2),
        name="rwkv_proj",
    )(*ins)
    rt, kt, at, bt, v, g, gl, hl = outs
    shift_new = hl.reshape(b, SUBLANES, D_MODEL)[:, -1]

    nh = LANES // RW_N
    ng = RW_H // nh
    gl_lanes = nh * RW_N
    hkv = jnp.swapaxes(s0, -1, -2).reshape(b, ng, nh, RW_N, RW_N)
    zblk = jnp.zeros((b, ng, RW_N, RW_N), F32)
    h0 = jnp.concatenate(
        [jnp.concatenate([hkv[:, :, i] if i == jj else zblk for jj in range(nh)], axis=-1) for i in range(nh)],
        axis=-2)
    svec = jnp.stack([w["rw_lnx_w"][ri], w["rw_lnx_b"][ri], w["rw_rk"][ri].reshape(D_MODEL),
                      zero, zero, zero, zero, zero])
    per_seq = t == chunk and b % RWKV_SUBCHUNKS == 0
    nsub = RWKV_SUBCHUNKS if per_seq or (t // chunk) % RWKV_SUBCHUNKS == 0 else 1
    nct = 1 if per_seq else t // (chunk * nsub)
    nseq = nsub if per_seq else 1
    crow = lambda bb, j: (bb * nct + j, 0)
    cspec = pl.BlockSpec((chunk * nsub, D_MODEL), crow)
    hspec = pl.BlockSpec((nseq, ng, gl_lanes, gl_lanes), lambda bb, j: (bb, 0, 0, 0))
    y, hout = pl.pallas_call(
        functools.partial(_rwkv_scan_body, nh=nh, chunk=chunk, nsub=nsub, per_seq=per_seq),
        grid=(b // nseq, nct),
        in_specs=[cspec] * 5 + [pl.BlockSpec((nsub, 1, D_MODEL), lambda bb, j: (bb * nct + j, 0, 0)), hspec,
                                _const_spec((8, D_MODEL))],
        out_specs=[cspec, hspec],
        out_shape=[jax.ShapeDtypeStruct((n, D_MODEL), adt),
                   jax.ShapeDtypeStruct((b, ng, gl_lanes, gl_lanes), F32)],
        compiler_params=_params(2),
        name="rwkv_scan",
    )(rt, kt, at, bt, v, gl, h0, svec)
    s_new = jnp.stack([hout[:, :, i * RW_N:(i + 1) * RW_N, i * RW_N:(i + 1) * RW_N] for i in range(nh)],
                      axis=2)
    s_new = jnp.swapaxes(s_new, -1, -2).reshape(b, RW_H, RW_N, RW_N)
    pre = ("gated", [y, g], [w["rw_wo"][ri].astype(BF16)])
    return pre, shift_new, s_new, (v if not has_vres else v_first)


def _rope_lanes(x, tab_ref):
    half = MLA_ROPE // 2
    return (x * tab_ref[0] + pltpu.roll(x, LANES - half, 1) * tab_ref[1] + pltpu.roll(x, half, 1) * tab_ref[2])


def _mla_proj_body(x_ref, nw_ref, tab_ref, winq_ref, winc_ref, wink_ref, qn_ref, kvn_ref, wqn_ref, wqr_ref,
                   wuk_ref, c_ref, kr_ref, kcat_ref, qcat_ref):
    h = _rms(x_ref[...], nw_ref[0:1, :]).astype(BF16)
    cq = _rms(jnp.dot(h, winq_ref[...], preferred_element_type=F32), qn_ref[...]).astype(BF16)
    c = _rms(jnp.dot(h, winc_ref[...], preferred_element_type=F32), kvn_ref[...])
    kr = _rope_lanes(jnp.dot(h, wink_ref[...], preferred_element_type=F32), tab_ref)
    c_ref[...] = c
    kr_ref[...] = kr
    adt = kcat_ref.dtype
    kcat_ref[:, 0:MLA_KV_LORA] = c.astype(adt)
    kcat_ref[:, MLA_KV_LORA:MLA_QK] = kr.astype(adt)
    qn = jnp.dot(cq, wqn_ref[...], preferred_element_type=F32).astype(BF16)
    qr = jnp.dot(cq, wqr_ref[...], preferred_element_type=F32)
    for pr in range(MLA_H // 2):
        ql = jnp.dot(qn[:, pr * LANES:(pr + 1) * LANES], wuk_ref[pr], preferred_element_type=F32) * MLA_SCALE
        qcat_ref[2 * pr, :, 0:MLA_KV_LORA] = ql[:, :MLA_KV_LORA].astype(adt)
        qcat_ref[2 * pr + 1, :, 0:MLA_KV_LORA] = ql[:, MLA_KV_LORA:].astype(adt)
    for hh in range(MLA_H):
        qro = _rope_lanes(qr[:, hh * LANES:(hh + 1) * LANES], tab_ref) * MLA_SCALE
        qcat_ref[hh, :, MLA_KV_LORA:MLA_QK] = qro.astype(adt)


MLA_TQ = 256
MLA_TK = 512
MLA_SPLIT = 16


def _mla_prompt_body(q_ref, k_ref, o_ref, m_ref, l_ref, acc_ref):
    i = pl.program_id(1)
    rows = MLA_H * MLA_TQ
    q = q_ref[...].reshape(rows, MLA_QK)
    m_ref[...] = jnp.full((rows, LANES), -jnp.inf, F32)
    l_ref[...] = jnp.zeros((rows, LANES), F32)
    acc_ref[...] = jnp.zeros((rows, MLA_KV_LORA), F32)
    ones = jnp.ones((MLA_TK, LANES), BF16)
    reps = MLA_TK // LANES

    sub = rows // MLA_SPLIT

    def block(k0, masked):
        kblk = k_ref[pl.ds(k0, MLA_TK), :]
        ss = [lax.dot_general(q[g * sub:(g + 1) * sub], kblk, (((1,), (1,)), ((), ())),
                              preferred_element_type=F32) for g in range(MLA_SPLIT)]
        def causal(g):
            qpos = i * MLA_TQ + (g * sub + lax.broadcasted_iota(jnp.int32, (sub, MLA_TK), 0)) % MLA_TQ
            kpos = k0 + lax.broadcasted_iota(jnp.int32, (sub, MLA_TK), 1)
            return kpos <= qpos

        shared = causal(0) if masked and sub % MLA_TQ == 0 else None
        for g in range(MLA_SPLIT):
            rs = slice(g * sub, (g + 1) * sub)
            s = ss[g]
            if masked:
                s = jnp.where(shared if shared is not None else causal(g), s, -jnp.inf)
            m_old = m_ref[rs, :]
            m_new = jnp.maximum(m_old, jnp.max(s, axis=1, keepdims=True))
            alpha = jnp.exp(m_old - m_new)
            pf = jnp.exp(s - jnp.concatenate([m_new] * reps, axis=1))
            l_ref[rs, :] = l_ref[rs, :] * alpha + jnp.sum(pf, axis=1, keepdims=True)
            acc_ref[rs, :] = (acc_ref[rs, :] * jnp.concatenate([alpha] * (MLA_KV_LORA // LANES), axis=1)
                              + jnp.dot(pf.astype(BF16), kblk[:, :MLA_KV_LORA], preferred_element_type=F32))
            m_ref[rs, :] = m_new

    def full_step(kb, carry):
        block(pl.multiple_of(kb * MLA_TK, MLA_TK), False)
        return carry

    n_full = (i * MLA_TQ) // MLA_TK
    lax.fori_loop(0, n_full, full_step, 0)
    block(pl.multiple_of(n_full * MLA_TK, MLA_TK), True)
    o = acc_ref[...] / jnp.concatenate([l_ref[...]] * (MLA_KV_LORA // LANES), axis=1)
    o_ref[...] = o.reshape(MLA_H, MLA_TQ, MLA_KV_LORA).astype(BF16)


MLA_PP = 32
MLA_GROUPS = 2


def _mla_sample_body(pt_ref, q_ref, kn_ref, *rest):
    lat_refs = rest[:MLA_PP]
    kro_refs = rest[MLA_PP:2 * MLA_PP]
    o_ref, m_ref, l_ref, acc_ref = rest[2 * MLA_PP:]
    j = pl.program_id(1)
    t = q_ref.shape[1]
    rows = MLA_H * t
    q = q_ref[...].reshape(rows, MLA_QK).astype(BF16)
    ql = q[:, :MLA_KV_LORA]
    qr = q[:, MLA_KV_LORA:MLA_KV_LORA + MLA_ROPE]

    @pl.when(j == 0)
    def _():
        m_ref[...] = jnp.full(m_ref.shape, -jnp.inf, F32)
        l_ref[...] = jnp.zeros(l_ref.shape, F32)
        acc_ref[...] = jnp.zeros(acc_ref.shape, F32)

    vrep = MLA_KV_LORA // LANES

    def update(g, s, vals, row_sum):
        m_old = m_ref[g]
        m_new = jnp.maximum(m_old, jnp.max(s, axis=1, keepdims=True))
        alpha = jnp.exp(m_old - m_new)
        if s.shape[1] % LANES == 0:
            p = jnp.exp(s - jnp.concatenate([m_new] * (s.shape[1] // LANES), axis=1)).astype(BF16)
        else:
            p = jnp.exp(s - m_new[:, 0:1]).astype(BF16)
        l_ref[g] = l_ref[g] * alpha + row_sum(p)
        acc_ref[g] = (acc_ref[g] * jnp.concatenate([alpha] * vrep, axis=1)
                      + jnp.dot(p, vals, preferred_element_type=F32))
        m_ref[g] = m_new

    per = MLA_PP // MLA_GROUPS
    ones = jnp.ones((per * lat_refs[0].shape[1], LANES), BF16)
    scores, values = [], []
    for g in range(MLA_GROUPS):
        cbs, s_parts = [], []
        for pp in range(g * per, (g + 1) * per):
            cb = lat_refs[pp][0].astype(BF16)
            kbt = kro_refs[pp][0].astype(BF16)
            s_parts.append(lax.dot_general(ql, cb, (((1,), (1,)), ((), ())), preferred_element_type=F32)
                           + jnp.dot(qr, kbt, preferred_element_type=F32))
            cbs.append(cb)
        scores.append(jnp.concatenate(s_parts, axis=1))
        values.append(jnp.concatenate(cbs, axis=0))
    for g in range(MLA_GROUPS):
        update(g, scores[g], values[g], lambda p: jnp.dot(p, ones, preferred_element_type=F32))

    @pl.when(j == pl.num_programs(1) - 1)
    def _():
        kn = kn_ref[...].astype(BF16)
        s = lax.dot_general(q, kn, (((1,), (1,)), ((), ())), preferred_element_type=F32)
        qpos = lax.broadcasted_iota(jnp.int32, (rows, t), 0) % t
        kpos = lax.broadcasted_iota(jnp.int32, (rows, t), 1)
        s = jnp.where(kpos <= qpos, s, -jnp.inf)
        update(0, s, kn[:, :MLA_KV_LORA], lambda p: jnp.sum(p.astype(F32), axis=1, keepdims=True))
        m_all = m_ref[0]
        for g in range(1, MLA_GROUPS):
            m_all = jnp.maximum(m_all, m_ref[g])
        l_all = jnp.zeros((rows, LANES), F32)
        acc = jnp.zeros((rows, MLA_KV_LORA), F32)
        for g in range(MLA_GROUPS):
            wgt = jnp.exp(m_ref[g] - m_all)
            l_all = l_all + l_ref[g] * wgt
            acc = acc + acc_ref[g] * jnp.concatenate([wgt] * vrep, axis=1)
        o = acc / jnp.concatenate([l_all] * vrep, axis=1)
        o_ref[...] = o.reshape(MLA_H, t, MLA_KV_LORA).astype(o_ref.dtype)


def _mla_layer(x2d, pos, w, mi, nw, b, t, paged):
    n = b * t
    tl = _tiling(b, t, 512)
    G, J, R = tl["G"], tl["J"], tl["R"]
    adt = BF16 if t % 16 == 0 else F32
    half = MLA_ROPE // 2
    inv = ROPE_THETA ** (-jnp.arange(half, dtype=F32) / half)
    ang = pos.astype(F32)[:, None] * inv[None, :]
    cos, sin = jnp.cos(ang), jnp.sin(ang)
    zpad = jnp.zeros((t, LANES - MLA_ROPE), F32)
    zh = jnp.zeros((t, half), F32)
    tab = jnp.stack([jnp.concatenate([cos, cos, zpad], 1), jnp.concatenate([-sin, zh, zpad], 1),
                     jnp.concatenate([zh, sin, zpad], 1)])
    if G == 1:
        tab = jnp.tile(tab, (1, b, 1))
    w_in = w["mla_w_in"][mi]
    winq = w_in[:, :MLA_Q_LORA].astype(BF16)
    winc = w_in[:, MLA_Q_LORA:MLA_Q_LORA + MLA_KV_LORA].astype(BF16)
    wink = jnp.pad(w_in[:, MLA_Q_LORA + MLA_KV_LORA:], ((0, 0), (0, LANES - MLA_ROPE))).astype(BF16)
    wqb = w["mla_w_qb"][mi].reshape(MLA_Q_LORA, MLA_H, MLA_NOPE + MLA_ROPE)
    wqn = wqb[:, :, :MLA_NOPE].reshape(MLA_Q_LORA, MLA_H * MLA_NOPE).astype(BF16)
    wqr = jnp.pad(wqb[:, :, MLA_NOPE:], ((0, 0), (0, 0), (0, LANES - MLA_ROPE))
                  ).reshape(MLA_Q_LORA, MLA_H * LANES).astype(BF16)
    wuk = jnp.transpose(w["mla_w_uk"][mi], (1, 2, 0)).reshape(MLA_H // 2, 2, MLA_NOPE, MLA_KV_LORA)
    wuk_bd = jnp.einsum("pinc,ij->pinjc", wuk, jnp.eye(2, dtype=F32)).reshape(
        MLA_H // 2, 2 * MLA_NOPE, 2 * MLA_KV_LORA).astype(BF16)
    wuv = jnp.transpose(w["mla_w_uv"][mi], (1, 0, 2)).reshape(MLA_H // 2, 2, MLA_KV_LORA, MLA_V)
    wuv_bd = jnp.einsum("picv,ij->picjv", wuv, jnp.eye(2, dtype=F32)).reshape(
        MLA_H // 2, 2 * MLA_KV_LORA, 2 * MLA_V).astype(BF16)
    nwa = jnp.concatenate([nw[0:1], jnp.zeros((7, D_MODEL), F32)])
    nwb = jnp.concatenate([nw[1:2], jnp.zeros((7, D_MODEL), F32)])
    row = lambda g, j: (g * J + j, 0)
    wl = [winq, winc, wink, w["mla_q_norm"][mi][None, :], w["mla_kv_norm"][mi][None, :], wqn, wqr, wuk_bd]
    c, kr, kcat, qcat = pl.pallas_call(
        _mla_proj_body,
        grid=(G, J),
        in_specs=[pl.BlockSpec((R, D_MODEL), row), _const_spec((8, D_MODEL)),
                  pl.BlockSpec((3, R, LANES), lambda g, j: (0, j, 0))] + [_const_spec(a.shape) for a in wl],
        out_specs=[pl.BlockSpec((R, MLA_KV_LORA), row), pl.BlockSpec((R, LANES), row),
                   pl.BlockSpec((R, MLA_QK), row), pl.BlockSpec((MLA_H, R, MLA_QK), lambda g, j: (0, g * J + j, 0))],
        out_shape=[jax.ShapeDtypeStruct((n, MLA_KV_LORA), F32), jax.ShapeDtypeStruct((n, LANES), F32),
                   jax.ShapeDtypeStruct((n, MLA_QK), adt), jax.ShapeDtypeStruct((MLA_H, n, MLA_QK), adt)],
        compiler_params=_params(2),
        name="mla_proj",
    )(x2d, nwa, tab, *wl)

    if paged is None:
        nq = t // MLA_TQ
        rows = MLA_H * MLA_TQ
        o = pl.pallas_call(
            _mla_prompt_body,
            grid=(b, nq),
            in_specs=[pl.BlockSpec((MLA_H, MLA_TQ, MLA_QK), lambda bb, i: (0, bb * nq + i, 0)),
                      pl.BlockSpec((t, MLA_QK), lambda bb, i: (bb, 0))],
            out_specs=pl.BlockSpec((MLA_H, MLA_TQ, MLA_KV_LORA), lambda bb, i: (0, bb * nq + i, 0)),
            out_shape=jax.ShapeDtypeStruct((MLA_H, n, MLA_KV_LORA), BF16),
            scratch_shapes=[pltpu.VMEM((rows, LANES), F32), pltpu.VMEM((rows, LANES), F32),
                            pltpu.VMEM((rows, MLA_KV_LORA), F32)],
            compiler_params=_params(2),
            name="mla_attend_prompt",
        )(qcat, kcat)
    else:
        pages_c, pages_kr, page_table = paged
        page = pages_c.shape[1]
        npg = page_table.shape[1]
        assert npg % MLA_PP == 0
        rows = MLA_H * t

        def page_map(pp):
            return lambda bb, j, pt: (pt[bb, j * MLA_PP + pp], 0, 0)

        grid_spec = pltpu.PrefetchScalarGridSpec(
            num_scalar_prefetch=1,
            grid=(b, npg // MLA_PP),
            in_specs=[pl.BlockSpec((MLA_H, t, MLA_QK), lambda bb, j, pt: (0, bb, 0)),
                      pl.BlockSpec((t, MLA_QK), lambda bb, j, pt: (bb, 0))]
            + [pl.BlockSpec((1, page, MLA_KV_LORA), page_map(pp)) for pp in range(MLA_PP)]
            + [pl.BlockSpec((1, MLA_ROPE, page), page_map(pp)) for pp in range(MLA_PP)],
            out_specs=pl.BlockSpec((MLA_H, t, MLA_KV_LORA), lambda bb, j, pt: (0, bb, 0)),
            scratch_shapes=[pltpu.VMEM((MLA_GROUPS, rows, LANES), F32), pltpu.VMEM((MLA_GROUPS, rows, LANES), F32),
                            pltpu.VMEM((MLA_GROUPS, rows, MLA_KV_LORA), F32)],
        )
        o = pl.pallas_call(
            _mla_sample_body,
            grid_spec=grid_spec,
            out_shape=jax.ShapeDtypeStruct((MLA_H, n, MLA_KV_LORA), adt),
            compiler_params=_params(2),
            name="mla_attend_sample",
        )(page_table, qcat, kcat, *([pages_c] * MLA_PP), *([jnp.swapaxes(pages_kr, 1, 2)] * MLA_PP))

    pre = ("mla", [o], [wuv_bd, w["mla_wo"][mi].astype(BF16)])
    return pre, c.reshape(b, t, MLA_KV_LORA), kr[:, :MLA_ROPE].reshape(b, t, MLA_ROPE)


GDN_CW = 512


def _gdn_proj_body(x_ref, prev_ref, nw_ref, wqkv_ref, wz_ref, wbg_ref, cw_ref, gvec_ref, tri_ref,
                   q_ref, k_ref, v_ref, z_ref, beta_ref, gc_ref, st_ref, carry_ref):
    @pl.when(pl.program_id(1) == 0)
    def _():
        carry_ref[...] = prev_ref[...]

    rows = x_ref.shape[0]
    p = carry_ref.shape[0]
    h = _rms(x_ref[...], nw_ref[0:1, :]).astype(BF16)
    adt = z_ref.dtype
    bg = jnp.dot(h, wbg_ref[...], preferred_element_type=F32)
    beta_ref[...] = _sigmoid(bg)
    g = -jnp.exp(gvec_ref[0:1, :]) * _softplus(bg + gvec_ref[1:2, :])
    gc_ref[...] = _chunk_cumsum(g, tri_ref[...])
    nch = GDN_CONV_DIM // GDN_CW
    zw = GDN_V_DIM // nch

    def up(c):
        return jnp.dot(h, wqkv_ref[:, c * GDN_CW:(c + 1) * GDN_CW], preferred_element_type=F32)

    u_nxt = up(0)
    for c in range(nch):
        sl = slice(c * GDN_CW, (c + 1) * GDN_CW)
        u = u_nxt
        u_nxt = up(c + 1) if c + 1 < nch else None
        z_ref[:, c * zw:(c + 1) * zw] = jnp.dot(h, wz_ref[:, c * zw:(c + 1) * zw],
                                                preferred_element_type=F32).astype(adt)
        prev = carry_ref[:, sl]
        u1 = _shift_rows(u, prev, 1)
        near = cw_ref[3:4, sl] * u + cw_ref[2:3, sl] * u1
        far = cw_ref[1:2, sl] * u + cw_ref[0:1, sl] * u1
        prev1 = pltpu.roll(prev, 1, 0)
        far_prev = cw_ref[1:2, sl] * prev + cw_ref[0:1, sl] * prev1
        y = near + _shift_rows(far, far_prev, 2)
        tail = u[rows - p:, :]
        carry_ref[:, sl] = tail
        st_ref[:, sl] = tail
        y = _silu(y)
        off = c * GDN_CW
        if off < 2 * GDN_QK_DIM:
            dst, base, scale = (q_ref, off, GDN_DK ** -0.5) if off < GDN_QK_DIM else (k_ref, off - GDN_QK_DIM, 1.0)
            for hh in range(GDN_CW // GDN_DK):
                yh = y[:, hh * GDN_DK:(hh + 1) * GDN_DK]
                inv = lax.rsqrt(jnp.sum(yh * yh, axis=-1, keepdims=True) + 1e-6)
                dst[:, base + hh * GDN_DK:base + (hh + 1) * GDN_DK] = (yh * (inv * scale)).astype(adt)
        else:
            v_ref[:, off - 2 * GDN_QK_DIM:off - 2 * GDN_QK_DIM + GDN_CW] = y.astype(adt)


def _gdn_chunk_body(q_ref, k_ref, v_ref, z_ref, gc_ref, beta_ref, s0_ref, nw_ref,
                    o_ref, so_ref, *, nh, chunk, nsub):
    @pl.when(pl.program_id(1) == 0)
    def _():
        so_ref[...] = s0_ref[...]

    ng = GDN_V_H // nh
    gc = nh * chunk
    rep = GDN_V_H // GDN_QK_H
    ri = lax.broadcasted_iota(jnp.int32, (gc, gc), 0)
    ci = lax.broadcasted_iota(jnp.int32, (gc, gc), 1)
    same = (ri // chunk) == (ci // chunk)
    strict = same & ((ri % chunk) > (ci % chunk))
    incl = same & ((ri % chunk) >= (ci % chunk))
    last = same & ((ci % chunk) == chunk - 1)
    eye = (ri == ci).astype(F32)
    row_head = lax.broadcasted_iota(jnp.int32, (gc, GDN_DK), 0) // chunk
    groups = range(ng)
    heads = [[q * nh + i for i in range(nh)] for q in groups]
    keys = [(sc, q) for sc in range(nsub) for q in groups]

    def stack(ref, sc, hds, width):
        parts = [ref[sc * chunk:(sc + 1) * chunk, hd * width:(hd + 1) * width] for hd in hds]
        return parts[0] if len(parts) == 1 else jnp.concatenate(parts, axis=0)

    def col(ref, sc, lanes):
        parts = [ref[sc * chunk:(sc + 1) * chunk, ln:ln + 1] for ln in lanes]
        return parts[0] if len(parts) == 1 else jnp.concatenate(parts, axis=0)

    k_st = {k: stack(k_ref, k[0], [hd // rep for hd in heads[k[1]]], GDN_DK) for k in keys}
    q_st = {k: stack(q_ref, k[0], [hd // rep for hd in heads[k[1]]], GDN_DK) for k in keys}
    v_st = {k: stack(v_ref, k[0], heads[k[1]], GDN_DV) for k in keys}
    gcol = {k: col(gc_ref, k[0], [GDN_V_H + hd for hd in heads[k[1]]]) for k in keys}
    bcol = {k: col(beta_ref, k[0], heads[k[1]]) for k in keys}
    grow = {k: jnp.sum(jnp.where(ri == ci, gcol[k], 0.0), axis=0, keepdims=True) for k in keys}
    k_b = {k: k_st[k].astype(BF16) for k in keys}
    kq = {k: _bdot_nt(jnp.concatenate([k_b[k], q_st[k].astype(BF16)], axis=0), k_b[k]) for k in keys}
    decay = {k: jnp.exp(jnp.where(incl, gcol[k] - grow[k], -jnp.inf)) for k in keys}
    a = {k: jnp.where(strict, kq[k][:gc] * bcol[k] * decay[k], 0.0) for k in keys}
    aqk = {k: jnp.where(incl, kq[k][gc:] * decay[k], 0.0).astype(BF16) for k in keys}

    x = _unit_lower_inverse({k: -a[k] for k in keys}, ri, ci, chunk)

    egc = {k: jnp.exp(gcol[k]) for k in keys}
    uw = {k: _bdot(x[k], jnp.concatenate([v_st[k] * bcol[k], k_st[k] * (bcol[k] * egc[k])], axis=1)) for k in keys}
    glast = {k: jnp.sum(jnp.where(last, grow[k], 0.0), axis=1, keepdims=True) for k in keys}
    kg = {k: k_st[k] * jnp.exp(glast[k] - gcol[k]) for k in keys}
    qg = {k: q_st[k] * egc[k] for k in keys}

    for sc in range(nsub):
        states = [[so_ref[0, hd] for hd in heads[q]] for q in groups]
        wq_s = []
        for q in groups:
            wm = uw[(sc, q)][:, GDN_DV:]
            parts = []
            for i in range(nh):
                rs = slice(i * chunk, (i + 1) * chunk)
                parts.append(_bdot(jnp.concatenate([wm[rs], qg[(sc, q)][rs]], axis=0), states[q][i]))
            wq_s.append(parts)
        v_new, o_st = [], []
        for q in groups:
            ws = jnp.concatenate([m[:chunk] for m in wq_s[q]], axis=0) if nh > 1 else wq_s[q][0][:chunk]
            qs = jnp.concatenate([m[chunk:] for m in wq_s[q]], axis=0) if nh > 1 else wq_s[q][0][chunk:]
            vn = (uw[(sc, q)][:, :GDN_DV] - ws).astype(BF16)
            v_new.append(vn)
            o_st.append(qs + _bdot(aqk[(sc, q)], vn))
        for q in groups:
            for i, hd in enumerate(heads[q]):
                rs = slice(i * chunk, (i + 1) * chunk)
                gl_h = jnp.exp(glast[(sc, q)][i * chunk:i * chunk + 1, :])
                if chunk % 16 == 0:
                    upd = _bdot_tn(kg[(sc, q)][rs], v_new[q][rs])
                else:
                    upd = _bdot_tn(jnp.where(row_head == i, kg[(sc, q)], 0.0), v_new[q])
                so_ref[0, hd] = states[q][i] * gl_h + upd
        for q in groups:
            z_st = stack(z_ref, sc, heads[q], GDN_DV).astype(F32)
            og = (_rms(o_st[q], nw_ref[0:1, :]) * _silu(z_st)).astype(o_ref.dtype)
            for i, hd in enumerate(heads[q]):
                o_ref[sc * chunk:(sc + 1) * chunk, hd * GDN_DV:(hd + 1) * GDN_DV] = og[i * chunk:(i + 1) * chunk]


GDN_SUBCHUNKS = 4


def _gdn_layer(x2d, conv_prev, s0, w, gi, nw, b, t):
    n = b * t
    chunk = _chunk_of(t)
    tl = _tiling(b, t, 256)
    G, J, R, P = tl["G"], tl["J"], tl["R"], tl["P"]
    adt = BF16 if chunk % 16 == 0 else F32
    w_in = w["gdn_w_in"][gi]
    o1 = GDN_CONV_DIM
    o2 = o1 + GDN_V_DIM
    wqkv = w_in[:, :o1].astype(BF16)
    wz = w_in[:, o1:o2].astype(BF16)
    wbg = jnp.pad(w_in[:, o2:], ((0, 0), (0, LANES - 2 * GDN_V_H))).astype(BF16)
    cw = jnp.pad(w["gdn_conv_w"][gi], ((0, 8 - GDN_CONV), (0, 0)))
    gvec = jnp.zeros((8, LANES), F32)
    gvec = gvec.at[0, GDN_V_H:2 * GDN_V_H].set(w["gdn_a_log"][gi]).at[1, GDN_V_H:2 * GDN_V_H].set(w["gdn_dt_bias"][gi])
    bc = chunk if chunk == 64 else R
    tri = _chunk_masks(chunk, bc)
    nwa = jnp.concatenate([nw[0:1], jnp.zeros((7, D_MODEL), F32)])
    row = lambda g, j: (g * J + j, 0)
    st_spec = pl.BlockSpec((P, GDN_CONV_DIM), lambda g, j: (g, 0))
    qn, kn, v, z, beta, gcs, st = pl.pallas_call(
        _gdn_proj_body,
        grid=(G, J),
        in_specs=[pl.BlockSpec((R, D_MODEL), row), st_spec, _const_spec((8, D_MODEL)), _const_spec(wqkv.shape),
                  _const_spec(wz.shape), _const_spec(wbg.shape), _const_spec(cw.shape), _const_spec(gvec.shape),
                  _const_spec(tri.shape)],
        out_specs=[pl.BlockSpec((R, GDN_QK_DIM), row), pl.BlockSpec((R, GDN_QK_DIM), row),
                   pl.BlockSpec((R, GDN_V_DIM), row), pl.BlockSpec((R, GDN_V_DIM), row),
                   pl.BlockSpec((R, LANES), row), pl.BlockSpec((R, LANES), row), st_spec],
        out_shape=[jax.ShapeDtypeStruct((n, GDN_QK_DIM), adt), jax.ShapeDtypeStruct((n, GDN_QK_DIM), adt),
                   jax.ShapeDtypeStruct((n, GDN_V_DIM), adt), jax.ShapeDtypeStruct((n, GDN_V_DIM), adt),
                   jax.ShapeDtypeStruct((n, LANES), F32), jax.ShapeDtypeStruct((n, LANES), F32),
                   jax.ShapeDtypeStruct((b * SUBLANES, GDN_CONV_DIM), F32)],
        scratch_shapes=[pltpu.VMEM((P, GDN_CONV_DIM), F32)],
        compiler_params=_params(2),
        name="gdn_proj",
    )(x2d, _pad_state(conv_prev), nwa, wqkv, wz, wbg, cw, gvec, tri)
    conv_new = st.reshape(b, SUBLANES, GDN_CONV_DIM)[:, SUBLANES - (GDN_CONV - 1):]

    nh = GROUP_ROWS // chunk
    ng = GDN_V_H // nh
    nsub = GDN_SUBCHUNKS if (t // chunk) % GDN_SUBCHUNKS == 0 else 1
    nct = t // (chunk * nsub)
    br = chunk * nsub
    crow = lambda bb, j: (bb * nct + j, 0)
    sspec = pl.BlockSpec((1, GDN_V_H, GDN_DK, GDN_DV), lambda bb, j: (bb, 0, 0, 0))
    nwn = jnp.concatenate([w["gdn_norm_w"][gi][None, :], jnp.zeros((7, GDN_DV), F32)])
    o, s_new = pl.pallas_call(
        functools.partial(_gdn_chunk_body, nh=nh, chunk=chunk, nsub=nsub),
        grid=(b, nct),
        in_specs=[pl.BlockSpec((br, GDN_QK_DIM), crow), pl.BlockSpec((br, GDN_QK_DIM), crow),
                  pl.BlockSpec((br, GDN_V_DIM), crow), pl.BlockSpec((br, GDN_V_DIM), crow),
                  pl.BlockSpec((br, LANES), crow), pl.BlockSpec((br, LANES), crow),
                  sspec, _const_spec((8, GDN_DV))],
        out_specs=[pl.BlockSpec((br, GDN_V_DIM), crow), sspec],
        out_shape=[jax.ShapeDtypeStruct((n, GDN_V_DIM), adt),
                   jax.ShapeDtypeStruct((b, GDN_V_H, GDN_DK, GDN_DV), F32)],
        compiler_params=_params(2),
        name="gdn_chunk",
    )(qn, kn, v, z, gcs, beta, s0, nwn)
    pre = ("plain", [o], [w["gdn_wo"][gi].astype(BF16)])
    return pre, conv_new, s_new


def _trunk(x, pos, rw_s, rw_shift, gdn_s, gdn_conv, ffn_conv, w, paged):
    b, t, _ = x.shape
    x2d = x.reshape(b * t, D_MODEL)
    new = {k: [] for k in ("rw_S", "rw_shift", "mla_c", "mla_kr", "gdn_S", "gdn_conv", "ffn_conv")}
    v_first = None
    ri = mi = gi = 0
    for l, kind in enumerate(LAYER_MIXER):
        nw = w["norm_w"][l]
        if kind == 0:
            pre, sh, s_new, v_first = _rwkv_layer(x2d, rw_shift[ri], rw_s[ri], v_first, w, ri, nw, b, t)
            new["rw_S"].append(s_new)
            new["rw_shift"].append(sh)
            ri += 1
        elif kind == 1:
            pre, c, kr = _mla_layer(x2d, pos, w, mi, nw, b, t, None if paged is None else
                                    (paged[0][mi], paged[1][mi], paged[2]))
            new["mla_c"].append(c)
            new["mla_kr"].append(kr)
            mi += 1
        else:
            pre, cb, s_new = _gdn_layer(x2d, gdn_conv[gi], gdn_s[gi], w, gi, nw, b, t)
            new["gdn_S"].append(s_new)
            new["gdn_conv"].append(cb)
            gi += 1
        nwf = jnp.concatenate([nw[2:4], nw[1:2], jnp.zeros((5, D_MODEL), F32)])
        cwb = jnp.concatenate([w["ffn_conv_w"][l], w["ffn_conv_b"][l][None, :],
                               jnp.zeros((8 - FFN_CONV - 1, 2 * D_FF), F32)])
        x2d, st = _ffn(x2d, pre, _pad_state(ffn_conv[l]), nwf, w["ffn_w_up"][l].astype(BF16), cwb,
                       w["ffn_w_down"][l].astype(BF16), b, t)
        new["ffn_conv"].append(st.reshape(b, SUBLANES, 2 * D_FF)[:, SUBLANES - (FFN_CONV - 1):])
    return x2d.reshape(b, t, D_MODEL), {k: jnp.stack(v) for k, v in new.items()}


def kernel(x_prompt, x_sample, state_rwkv_wkv, state_rwkv_shift, cache_mla_latent, cache_mla_krope, state_gdn_S, state_gdn_conv, state_ffn_conv, page_table, norm_w, rw_mu, rw_wrkv, rw_w0, rw_w1, rw_w2, rw_a0, rw_a1, rw_a2, rw_v0, rw_v1, rw_v2, rw_g1, rw_g2, rw_kk, rw_ka, rw_rk, rw_lnx_w, rw_lnx_b, rw_wo, mla_w_in, mla_q_norm, mla_kv_norm, mla_w_qb, mla_w_uk, mla_w_uv, mla_wo, gdn_w_in, gdn_conv_w, gdn_a_log, gdn_dt_bias, gdn_norm_w, gdn_wo, ffn_w_up, ffn_conv_w, ffn_conv_b, ffn_w_down):
    w = dict(norm_w=norm_w, rw_mu=rw_mu, rw_wrkv=rw_wrkv, rw_w0=rw_w0, rw_w1=rw_w1, rw_w2=rw_w2, rw_a0=rw_a0,
             rw_a1=rw_a1, rw_a2=rw_a2, rw_v0=rw_v0, rw_v1=rw_v1, rw_v2=rw_v2, rw_g1=rw_g1, rw_g2=rw_g2,
             rw_kk=rw_kk, rw_ka=rw_ka, rw_rk=rw_rk, rw_lnx_w=rw_lnx_w, rw_lnx_b=rw_lnx_b, rw_wo=rw_wo,
             mla_w_in=mla_w_in, mla_q_norm=mla_q_norm, mla_kv_norm=mla_kv_norm, mla_w_qb=mla_w_qb,
             mla_w_uk=mla_w_uk, mla_w_uv=mla_w_uv, mla_wo=mla_wo, gdn_w_in=gdn_w_in, gdn_conv_w=gdn_conv_w,
             gdn_a_log=gdn_a_log, gdn_dt_bias=gdn_dt_bias, gdn_norm_w=gdn_norm_w, gdn_wo=gdn_wo,
             ffn_w_up=ffn_w_up, ffn_conv_w=ffn_conv_w, ffn_conv_b=ffn_conv_b, ffn_w_down=ffn_w_down)
    b, t = x_prompt.shape[0], x_prompt.shape[1]
    n_rw, n_gdn, depth = state_rwkv_wkv.shape[0], state_gdn_S.shape[0], state_ffn_conv.shape[0]
    y_p, sp = _trunk(
        x_prompt, jnp.arange(t),
        jnp.zeros((n_rw, b) + state_rwkv_wkv.shape[2:], F32), jnp.zeros((n_rw, b, D_MODEL), F32),
        jnp.zeros((n_gdn, b) + state_gdn_S.shape[2:], F32), jnp.zeros((n_gdn, b) + state_gdn_conv.shape[2:], F32),
        jnp.zeros((depth, b) + state_ffn_conv.shape[2:], F32), w, None)
    past_len = page_table.shape[1] * cache_mla_latent.shape[2]
    pos_s = past_len + jnp.arange(x_sample.shape[1])
    y_s, ss = _trunk(x_sample, pos_s, state_rwkv_wkv, state_rwkv_shift, state_gdn_S, state_gdn_conv,
                     state_ffn_conv, w, (cache_mla_latent, cache_mla_krope, page_table))
    names = ("rw_S", "rw_shift", "mla_c", "mla_kr", "gdn_S", "gdn_conv", "ffn_conv")
    return (y_p, y_s) + tuple(sp[k] for k in names) + tuple(ss[k] for k in names)
```

```python
import functools

import jax
import jax.numpy as jnp
from jax import lax
from jax.experimental import pallas as pl
from jax.experimental.pallas import tpu as pltpu

F32 = jnp.float32
BF16 = jnp.bfloat16
HIGHEST = lax.Precision.HIGHEST

D_MODEL = 1024
NORM_EPS = 1e-6
RW_N = 64
RW_H = D_MODEL // RW_N
RW_LNX_EPS = 64e-5
MLA_H = 16
MLA_NOPE = 64
MLA_ROPE = 32
MLA_V = 64
MLA_Q_LORA = 512
MLA_KV_LORA = 256
MLA_SCALE = (MLA_NOPE + MLA_ROPE) ** -0.5
ROPE_THETA = 10000.0
MLA_QK = MLA_KV_LORA + 128
GDN_QK_H = 8
GDN_V_H = 16
GDN_DK = 128
GDN_DV = 128
GDN_QK_DIM = GDN_QK_H * GDN_DK
GDN_V_DIM = GDN_V_H * GDN_DV
GDN_CONV_DIM = 2 * GDN_QK_DIM + GDN_V_DIM
GDN_CONV = 4
D_FF = 2816
FFN_CONV = 3
LAYER_MIXER = (0, 1, 2, 0)

SUBLANES = 8
LANES = 128
GROUP_ROWS = 128
VMEM_LIMIT = 56 * 1024 * 1024


def _rms(x, w):
    return x * lax.rsqrt(jnp.mean(x * x, axis=-1, keepdims=True) + NORM_EPS) * w


def _bdot(a, b):
    return jnp.dot(a.astype(BF16), b.astype(BF16), preferred_element_type=F32)


def _bdot_nt(a, b):
    return lax.dot_general(a.astype(BF16), b.astype(BF16), (((1,), (1,)), ((), ())),
                           preferred_element_type=F32)


def _bdot_tn(a, b):
    return lax.dot_general(a.astype(BF16), b.astype(BF16), (((0,), (0,)), ((), ())),
                           preferred_element_type=F32)


def _hdot(a, b):
    return jnp.dot(a, b, precision=HIGHEST, preferred_element_type=F32)


def _sigmoid(x):
    return 1.0 / (1.0 + jnp.exp(-x))


def _softplus(x):
    return jnp.maximum(x, 0.0) + jnp.log(1.0 + jnp.exp(-jnp.abs(x)))


def _silu(x):
    return x * _sigmoid(x)


def _shift_rows(u, prev, s):
    rows, cols = u.shape
    p = prev.shape[0]
    rolled = pltpu.roll(u, s, 0)
    fix = pltpu.roll(prev, (p - SUBLANES + s) % p, 0)
    t = lax.broadcasted_iota(jnp.int32, (p, cols), 0) % SUBLANES
    if p == rows:
        return jnp.where(t < s, fix, rolled)
    head = jnp.where(t < s, fix, rolled[:SUBLANES])
    return jnp.concatenate([head, rolled[SUBLANES:]], axis=0)


def _lane_group_sum(x, ones2):
    parts = []
    for i in range(x.shape[1] // LANES):
        xs = x[:, i * LANES:(i + 1) * LANES]
        hi = xs.astype(BF16)
        lo = (xs - hi.astype(F32)).astype(BF16)
        parts.append(jnp.dot(jnp.concatenate([hi, lo], axis=1), ones2, preferred_element_type=F32))
    return parts[0] if len(parts) == 1 else jnp.concatenate(parts, axis=1)


def _split_dot(m2, x):
    hi = x.astype(BF16)
    lo = (x - hi.astype(F32)).astype(BF16)
    return jnp.dot(m2, jnp.concatenate([hi, lo], axis=0), preferred_element_type=F32)


def _chunk_cumsum(x, tri):
    bc = tri.shape[0]
    parts = [_hdot(tri, x[i * bc:(i + 1) * bc]) for i in range(x.shape[0] // bc)]
    return parts[0] if len(parts) == 1 else jnp.concatenate(parts, axis=0)


def _unit_lower_inverse(n_mats, ri, ci, chunk):
    keys = list(n_mats)
    eye = (ri == ci).astype(F32)
    base = min(chunk, SUBLANES)
    same = (ri // base) == (ci // base)
    n0 = {k: jnp.where(same, n_mats[k], 0.0) for k in keys}
    x = {k: eye + n0[k] for k in keys}
    p = {k: n0[k].astype(BF16) for k in keys}
    span = 2
    while span < base:
        p = {k: _bdot(p[k], p[k]).astype(BF16) for k in keys}
        x = {k: x[k] + _bdot(p[k], x[k]) for k in keys}
        span *= 2
    size = base
    while size < chunk:
        pair = ((ri // (2 * size)) == (ci // (2 * size))) & ((ri // size) != (ci // size))
        xb = {k: x[k].astype(BF16) for k in keys}
        t = {k: _bdot(jnp.where(pair, n_mats[k], 0.0), xb[k]).astype(BF16) for k in keys}
        x = {k: x[k] + _bdot(xb[k], t[k]) for k in keys}
        size *= 2
    return x


def _tiling(b, t, tt_max):
    if t == SUBLANES:
        return dict(G=1, J=1, R=b * t, P=b * t)
    tt = min(t, tt_max)
    assert t % tt == 0 and tt % 64 == 0, (t, tt)
    return dict(G=b, J=t // tt, R=tt, P=SUBLANES)


def _chunk_of(t):
    return 64 if t % 64 == 0 else t


def _const_spec(shape):
    nd = len(shape)
    return pl.BlockSpec(shape, lambda *_: (0,) * nd, pipeline_mode=pl.Buffered(1))


def _params(n_axes):
    return pltpu.CompilerParams(dimension_semantics=("arbitrary",) * n_axes,
                                vmem_limit_bytes=VMEM_LIMIT)


def _pad_state(st):
    b, k1, c = st.shape
    return jnp.pad(st, ((0, 0), (SUBLANES - k1, 0), (0, 0))).reshape(b * SUBLANES, c)


def _chunk_masks(chunk, rows):
    i = jnp.arange(rows)
    same = (i[:, None] // chunk) == (i[None, :] // chunk)
    tri = same & ((i[None, :] % chunk) <= (i[:, None] % chunk))
    return tri.astype(F32)


FFN_CW = 256


ROW_BLOCK = 64


def _stage(buf_ref, u, carry_ref, st_ref, sl, taps):
    rows = u.shape[0]
    hb = taps * SUBLANES
    buf_ref[hb:hb + rows, :] = u
    first = lax.broadcasted_iota(jnp.int32, (SUBLANES, u.shape[1]), 0) == 0
    for i in range(taps):
        back = taps - i
        src = u[rows - back * SUBLANES:rows - (back - 1) * SUBLANES, :]
        crow = carry_ref[SUBLANES - back:SUBLANES - back + 1, sl]
        buf_ref[i * SUBLANES:(i + 1) * SUBLANES, :] = jnp.where(first, crow, pltpu.roll(src, 1, 0))
    sq = SUBLANES * SUBLANES
    tail = pltpu.einshape("(vs)d->(sv)d", u[rows - sq:, :], s=SUBLANES)[sq - SUBLANES:, :]
    carry_ref[:, sl] = tail
    st_ref[:, sl] = tail


def _taps(buf_ref, r0, nrows, taps):
    hb = taps * SUBLANES
    cur = buf_ref[hb + r0:hb + r0 + nrows, :]
    return cur, [buf_ref[hb - j * SUBLANES + r0:hb - j * SUBLANES + r0 + nrows, :] for j in range(taps, 0, -1)]


FFN_PARTS = 1


def _mixer_out(kind, refs, rs):
    if kind == "mla":
        o_ref, wuv_ref, wo_ref = refs
        parts = []
        for pr in range(MLA_H // 2):
            wp = wuv_ref[pr]
            parts.append(jnp.dot(o_ref[2 * pr, rs, :].astype(BF16), wp[:MLA_KV_LORA], preferred_element_type=F32)
                         + jnp.dot(o_ref[2 * pr + 1, rs, :].astype(BF16), wp[MLA_KV_LORA:],
                                   preferred_element_type=F32))
        y = jnp.concatenate(parts, axis=1)
    elif kind == "gated":
        y_ref, g_ref, wo_ref = refs
        y = y_ref[rs, :].astype(F32) * g_ref[rs, :].astype(F32)
    else:
        y_ref, wo_ref = refs
        y = y_ref[rs, :]
    return jnp.dot(y.astype(BF16), wo_ref[...], preferred_element_type=F32)


def _ffn_body(*refs, kind, n_pre):
    x_ref = refs[0]
    pre_refs = refs[1:1 + n_pre]
    (prev_ref, nw_ref, wup_ref, cwb_ref, wdn_ref, xo_ref, st_ref, carry_ref, act_ref, buf_ref) = refs[1 + n_pre:]

    @pl.when(pl.program_id(1) == 0)
    def _():
        carry_ref[...] = prev_ref[...]

    rows = x_ref.shape[0]
    stacked = carry_ref.shape[0] == rows
    nparts = 1 if stacked else FFN_PARTS
    prows = rows // nparts
    parts = [slice(i * prows, (i + 1) * prows) for i in range(nparts)]
    nch = D_FF // FFN_CW

    def cols(c, half):
        return slice(half * D_FF + c * FFN_CW, half * D_FF + (c + 1) * FFN_CW)

    def conv_gate(c, taps):
        ys = []
        for half in range(2):
            sl = cols(c, half)
            u, (u2, u1) = taps[half]
            ys.append(cwb_ref[0:1, sl] * u2 + cwb_ref[1:2, sl] * u1 + cwb_ref[2:3, sl] * u + cwb_ref[3:4, sl])
        return (_silu(ys[0]) * ys[1]).astype(BF16)

    xs = [x_ref[rs, :] + _rms(_mixer_out(kind, pre_refs, rs), nw_ref[2:3, :]) for rs in parts]
    for pi, rs in enumerate(parts):
        x = xs[pi] if stacked else pltpu.einshape("(sv)d->(vs)d", xs[pi], s=SUBLANES)
        h = _rms(x, nw_ref[0:1, :]).astype(BF16)

        def up(c):
            us = [jnp.dot(h, wup_ref[:, cols(c, half)], preferred_element_type=F32) for half in range(2)]
            if stacked:
                return us
            for half in range(2):
                _stage(buf_ref.at[(c % 2) * 2 + half], us[half], carry_ref, st_ref, cols(c, half), FFN_CONV - 1)
            return None

        u_cur = up(0)
        for c in range(nch):
            u_nxt = up(c + 1) if c + 1 < nch else None
            csl = slice(c * FFN_CW, (c + 1) * FFN_CW)
            if stacked:
                taps = []
                for half in range(2):
                    sl = cols(c, half)
                    u = u_cur[half]
                    prev = carry_ref[:, sl]
                    taps.append((u, [_shift_rows(u, prev, 2), _shift_rows(u, prev, 1)]))
                    carry_ref[:, sl] = u
                    st_ref[:, sl] = u
                act_ref[:, csl] = conv_gate(c, taps)
            else:
                for r0 in range(0, prows, ROW_BLOCK):
                    taps = [_taps(buf_ref.at[(c % 2) * 2 + half], r0, ROW_BLOCK, FFN_CONV - 1) for half in range(2)]
                    act_ref[rs.start + r0:rs.start + r0 + ROW_BLOCK, csl] = conv_gate(c, taps)
            u_cur = u_nxt
        f = jnp.dot(act_ref[rs, :], wdn_ref[...], preferred_element_type=F32)
        out = x + _rms(f, nw_ref[1:2, :])
        xo_ref[rs, :] = out if stacked else pltpu.einshape("(vs)d->(sv)d", out, s=SUBLANES)


def _ffn(x2d, pre, prev, nw, wup, cwb, wdn, b, t):
    tl = _tiling(b, t, 512)
    G, J, R, P = tl["G"], tl["J"], tl["R"], tl["P"]
    n = b * t
    kind, acts, wts = pre
    row = lambda g, j: (g * J + j, 0)
    if kind == "mla":
        act_specs = [pl.BlockSpec((MLA_H, R, MLA_KV_LORA), lambda g, j: (0, g * J + j, 0))]
    else:
        act_specs = [pl.BlockSpec((R, a.shape[1]), row) for a in acts]
    pre_specs = act_specs + [_const_spec(wt.shape) for wt in wts]
    return pl.pallas_call(
        functools.partial(_ffn_body, kind=kind, n_pre=len(pre_specs)),
        grid=(G, J),
        in_specs=[pl.BlockSpec((R, D_MODEL), row)] + pre_specs + [
            pl.BlockSpec((P, 2 * D_FF), lambda g, j: (g, 0)),
            _const_spec((8, D_MODEL)),
            _const_spec((D_MODEL, 2 * D_FF)),
            _const_spec((8, 2 * D_FF)),
            _const_spec((D_FF, D_MODEL)),
        ],
        out_specs=[
            pl.BlockSpec((R, D_MODEL), lambda g, j: (g * J + j, 0)),
            pl.BlockSpec((P, 2 * D_FF), lambda g, j: (g, 0)),
        ],
        out_shape=[jax.ShapeDtypeStruct((n, D_MODEL), F32),
                   jax.ShapeDtypeStruct((b * SUBLANES, 2 * D_FF), F32)],
        scratch_shapes=[pltpu.VMEM((P, 2 * D_FF), F32), pltpu.VMEM((R, D_FF), BF16),
                        pltpu.VMEM((4, R + (FFN_CONV - 1) * SUBLANES, FFN_CW), F32)],
        compiler_params=_params(2),
        name="conv_ffn",
    )(x2d, *acts, *wts, prev, nw, wup, cwb, wdn)


def _rwkv_proj_body(*refs, has_vres, chunk):
    it = iter(refs)
    x_ref, prev_ref = next(it), next(it)
    vf_ref = next(it) if has_vres else None
    vec_ref, wrkv_ref, w1_ref, w2_ref, a1_ref, a2_ref = (next(it) for _ in range(6))
    v1_ref, v2_ref = (next(it), next(it)) if has_vres else (None, None)
    g1_ref, g2_ref, tri_ref, ones_ref = (next(it) for _ in range(4))
    rt_ref, kt_ref, at_ref, bt_ref, v_ref, g_ref, gl_ref, hl_ref, carry_ref = (next(it) for _ in range(9))

    @pl.when(pl.program_id(1) == 0)
    def _():
        carry_ref[...] = prev_ref[...]

    x = x_ref[...]
    rows = x.shape[0]
    p = carry_ref.shape[0]
    h = _rms(x, vec_ref[10:11, :])
    d = _shift_rows(h, carry_ref[...], 1) - h
    tail = h[rows - p:, :]
    carry_ref[...] = tail
    hl_ref[...] = tail

    def mix(i):
        return (h + d * vec_ref[i:i + 1, :]).astype(BF16)

    r = jnp.dot(mix(0), wrkv_ref[0], preferred_element_type=F32)
    k = jnp.dot(mix(1), wrkv_ref[1], preferred_element_type=F32)
    xv = mix(2)
    v = jnp.dot(xv, wrkv_ref[2], preferred_element_type=F32)
    w_lora = _bdot(jnp.tanh(_bdot(mix(3), w1_ref[...])), w2_ref[...])
    v_lora = _bdot(_bdot(xv, v1_ref[...]), v2_ref[...]) if has_vres else None
    a_lora = _bdot(_bdot(mix(4), a1_ref[...]), a2_ref[...])
    g_ref[...] = _bdot(_sigmoid(_bdot(mix(5), g1_ref[...])), g2_ref[...]).astype(g_ref.dtype)
    adt = rt_ref.dtype

    bc = tri_ref.shape[0]
    for r0 in range(0, rows, bc):
        for l0 in range(0, D_MODEL, PROJ_LANES):
            rs, ls = slice(r0, r0 + bc), slice(l0, l0 + PROJ_LANES)
            vb = v[rs, ls]
            if has_vres:
                vb = vb + (vf_ref[rs, ls] - vb) * _sigmoid(vec_ref[11:12, ls] + v_lora[rs, ls])
            v_ref[rs, ls] = vb.astype(adt)
            a = _sigmoid(vec_ref[7:8, ls] + a_lora[rs, ls])
            kb = k[rs, ls]
            kk = kb * vec_ref[8:9, ls]
            kk = kk * lax.rsqrt(_lane_group_sum(kk * kk, ones_ref[...]) + 1e-6)
            kb = kb * (1.0 + (a - 1.0) * vec_ref[9:10, ls])
            w = -_softplus(-(vec_ref[6:7, ls] + w_lora[rs, ls])) - 0.5
            lw = -jnp.exp(w)
            cum = _split_dot(tri_ref[...], lw)
            e_bwd = jnp.exp(-cum)
            rt_ref[rs, ls] = (r[rs, ls] * jnp.exp(cum)).astype(adt)
            kt_ref[rs, ls] = (kb * e_bwd).astype(adt)
            at_ref[rs, ls] = (-kk * jnp.exp(cum - lw)).astype(adt)
            bt_ref[rs, ls] = (kk * a * e_bwd).astype(adt)
            for c in range(bc // chunk):
                row = (c + 1) * chunk - 1
                gl_ref[r0 // chunk + c, :, ls] = jnp.exp(cum[row:row + 1, :])


def _rwkv_scan_body(rt_ref, kt_ref, at_ref, bt_ref, v_ref, gl_ref, h0_ref, vec_ref, y_ref, ho_ref,
                    *, nh, chunk, nsub, per_seq):
    @pl.when(pl.program_id(1) == 0)
    def _():
        ho_ref[...] = h0_ref[...]

    gl_lanes = nh * RW_N
    ng = RW_H // nh
    gc = nh * chunk
    row_head = lax.broadcasted_iota(jnp.int32, (gc, gl_lanes), 0) // chunk
    lane_head = lax.broadcasted_iota(jnp.int32, (gc, gl_lanes), 1) // RW_N
    own = row_head == lane_head
    ri = lax.broadcasted_iota(jnp.int32, (gc, gc), 0)
    ci = lax.broadcasted_iota(jnp.int32, (gc, gc), 1)
    same = (ri // chunk) == (ci // chunk)
    strict = same & ((ri % chunk) > (ci % chunk))
    incl = same & ((ri % chunk) >= (ci % chunk))
    eye = (ri == ci).astype(F32)
    eye_l = (lax.broadcasted_iota(jnp.int32, (gl_lanes, gl_lanes), 0)
             == lax.broadcasted_iota(jnp.int32, (gl_lanes, gl_lanes), 1))
    merged = gc == GROUP_ROWS
    groups = range(ng)
    sls = [slice(q * gl_lanes, (q + 1) * gl_lanes) for q in groups]
    keys = [(sc, q) for sc in range(nsub) for q in groups]

    def blockdiag(ref, key):
        xg = ref[key[0] * chunk:(key[0] + 1) * chunk, sls[key[1]]]
        xx = jnp.concatenate([xg] * nh, axis=0) if nh > 1 else xg
        return jnp.where(own, xx, jnp.zeros_like(xx))

    r_bd = {k: blockdiag(rt_ref, k) for k in keys}
    k_bd = {k: blockdiag(kt_ref, k) for k in keys}
    a_bd = {k: blockdiag(at_ref, k) for k in keys}
    b_bd = {k: blockdiag(bt_ref, k) for k in keys}
    v_f = {k: blockdiag(v_ref, k) for k in keys}
    v_bd = {k: v_f[k].astype(BF16) for k in keys}
    bonus = {k: jnp.sum(r_bd[k].astype(F32) * k_bd[k] * vec_ref[2:3, sls[k[1]]], axis=1, keepdims=True)
             for k in keys}
    if merged:
        ar = {k: jnp.concatenate([a_bd[k], r_bd[k]], axis=0).astype(BF16) for k in keys}
        bk = {k: jnp.concatenate([b_bd[k], k_bd[k]], axis=0).astype(BF16) for k in keys}
        amat = {k: _bdot_nt(ar[k], bk[k]) for k in keys}
        a_ab = {k: jnp.where(strict, amat[k][:gc, :gc], 0.0) for k in keys}
        a_ak = {k: jnp.where(strict, amat[k][:gc, gc:], 0.0).astype(BF16) for k in keys}
        a_rbk = {k: jnp.concatenate([jnp.where(incl, amat[k][gc:, :gc], 0.0),
                                     jnp.where(incl, amat[k][gc:, gc:], 0.0)], axis=1).astype(BF16) for k in keys}
    else:
        ab_ = {k: a_bd[k].astype(BF16) for k in keys}
        rb_ = {k: r_bd[k].astype(BF16) for k in keys}
        bb_ = {k: b_bd[k].astype(BF16) for k in keys}
        kb_ = {k: k_bd[k].astype(BF16) for k in keys}
        a_ab = {k: jnp.where(strict, _bdot_nt(ab_[k], bb_[k]), 0.0) for k in keys}
        a_ak = {k: jnp.where(strict, _bdot_nt(ab_[k], kb_[k]), 0.0).astype(BF16) for k in keys}
        a_rb = {k: jnp.where(incl, _bdot_nt(rb_[k], bb_[k]), 0.0).astype(BF16) for k in keys}
        a_rk = {k: jnp.where(incl, _bdot_nt(rb_[k], kb_[k]), 0.0).astype(BF16) for k in keys}
    akv = {k: _bdot(a_ak[k], v_bd[k]) for k in keys}

    x = _unit_lower_inverse(a_ab, ri, ci, chunk)
    tinv = {k: x[k].astype(BF16) for k in keys}

    for sc in range(nsub):
        slot = sc if per_seq else 0
        hs = [ho_ref[slot, q] for q in groups]
        hs_b = [h.astype(BF16) for h in hs]
        gl_rows = [gl_ref[sc, :, sl] for sl in sls]
        if merged:
            arh = [_bdot(ar[(sc, q)], hs_b[q]) for q in groups]
            u = [_bdot(tinv[(sc, q)], arh[q][:gc] + akv[(sc, q)]).astype(BF16) for q in groups]
            uv = [jnp.concatenate([u[q], v_bd[(sc, q)]], axis=0) for q in groups]
            y_bd = [arh[q][gc:] + _bdot(a_rbk[(sc, q)], uv[q]) for q in groups]
            for q in groups:
                gl_col = jnp.sum(jnp.where(eye_l, gl_rows[q], 0.0), axis=1, keepdims=True)
                bk_g = jnp.concatenate([b_bd[(sc, q)] * gl_rows[q], k_bd[(sc, q)] * gl_rows[q]], axis=0)
                ho_ref[slot, q] = hs[q] * gl_col + _bdot_tn(bk_g, uv[q])
        else:
            ah = [_bdot(ab_[(sc, q)], hs_b[q]) for q in groups]
            rh = [_bdot(rb_[(sc, q)], hs_b[q]) for q in groups]
            u = [_bdot(tinv[(sc, q)], ah[q] + akv[(sc, q)]).astype(BF16) for q in groups]
            y_bd = [rh[q] + _bdot(a_rb[(sc, q)], u[q]) + _bdot(a_rk[(sc, q)], v_bd[(sc, q)]) for q in groups]
            for q in groups:
                gl_col = jnp.sum(jnp.where(eye_l, gl_rows[q], 0.0), axis=1, keepdims=True)
                ho_ref[slot, q] = (hs[q] * gl_col + _bdot_tn(b_bd[(sc, q)] * gl_rows[q], u[q])
                                   + _bdot_tn(k_bd[(sc, q)] * gl_rows[q], v_bd[(sc, q)]))

        for q in groups:
            sl = sls[q]
            mu = jnp.sum(y_bd[q], axis=1, keepdims=True) * (1.0 / RW_N)
            yc = jnp.where(own, y_bd[q] - mu, 0.0)
            var = jnp.sum(yc * yc, axis=1, keepdims=True) * (1.0 / RW_N)
            tot = (yc * lax.rsqrt(var + RW_LNX_EPS) * vec_ref[0:1, sl] + jnp.where(own, vec_ref[1:2, sl], 0.0)
                   + bonus[(sc, q)] * v_f[(sc, q)])
            y = tot[0:chunk]
            for hh in range(1, nh):
                y = y + tot[hh * chunk:(hh + 1) * chunk]
            y_ref[sc * chunk:(sc + 1) * chunk, sl] = y.astype(y_ref.dtype)


RWKV_SUBCHUNKS = 4
PROJ_LANES = 256


def _rwkv_layer(x2d, shift_prev, s0, v_first, w, ri, nw, b, t):
    n = b * t
    chunk = _chunk_of(t)
    tl = _tiling(b, t, 512)
    G, J, R, P = tl["G"], tl["J"], tl["R"], tl["P"]
    has_vres = v_first is not None
    vi = ri - 1
    adt = BF16 if chunk % 16 == 0 else F32
    bc = chunk if chunk == 64 else R
    tri = _chunk_masks(chunk, bc).astype(BF16)
    tri = jnp.concatenate([tri, tri], axis=1)
    li = jnp.arange(LANES)
    ones_bd = ((li[:, None] // RW_N) == (li[None, :] // RW_N)).astype(BF16)
    ones_bd = jnp.concatenate([ones_bd, ones_bd], axis=0)
    zero = jnp.zeros((D_MODEL,), F32)
    vec = jnp.stack([*(w["rw_mu"][ri][i] for i in range(6)), w["rw_w0"][ri], w["rw_a0"][ri], w["rw_kk"][ri],
                     w["rw_ka"][ri], nw[0], w["rw_v0"][vi] if has_vres else zero, zero, zero, zero, zero])
    row = lambda g, j: (g * J + j, 0)
    row_spec = pl.BlockSpec((R, D_MODEL), row)
    ins = [x2d, _pad_state(shift_prev[:, None, :])]
    specs = [row_spec, pl.BlockSpec((P, D_MODEL), lambda g, j: (g, 0))]
    if has_vres:
        ins.append(v_first)
        specs.append(row_spec)
    wl = [vec, w["rw_wrkv"][ri].astype(BF16), w["rw_w1"][ri].astype(BF16), w["rw_w2"][ri].astype(BF16),
          w["rw_a1"][ri].astype(BF16), w["rw_a2"][ri].astype(BF16)]
    if has_vres:
        wl += [w["rw_v1"][vi].astype(BF16), w["rw_v2"][vi].astype(BF16)]
    wl += [w["rw_g1"][ri].astype(BF16), w["rw_g2"][ri].astype(BF16), tri, ones_bd]
    ins += wl
    specs += [_const_spec(a.shape) for a in wl]
    nc_tile = R // chunk
    outs = pl.pallas_call(
        functools.partial(_rwkv_proj_body, has_vres=has_vres, chunk=chunk),
        grid=(G, J),
        in_specs=specs,
        out_specs=[row_spec] * 6 + [pl.BlockSpec((nc_tile, 1, D_MODEL), lambda g, j: (g * J + j, 0, 0)),
                                    pl.BlockSpec((P, D_MODEL), lambda g, j: (g, 0))],
        out_shape=[jax.ShapeDtypeStruct((n, D_MODEL), adt)] * 6
        + [jax.ShapeDtypeStruct((n // chunk, 1, D_MODEL), F32), jax.ShapeDtypeStruct((b * SUBLANES, D_MODEL), F32)],
        scratch_shapes=[pltpu.VMEM((P, D_MODEL), F32)],
        compiler_params=_params(2),
        name="rwkv_proj",
    )(*ins)
    rt, kt, at, bt, v, g, gl, hl = outs
    shift_new = hl.reshape(b, SUBLANES, D_MODEL)[:, -1]

    nh = LANES // RW_N
    ng = RW_H // nh
    gl_lanes = nh * RW_N
    hkv = jnp.swapaxes(s0, -1, -2).reshape(b, ng, nh, RW_N, RW_N)
    zblk = jnp.zeros((b, ng, RW_N, RW_N), F32)
    h0 = jnp.concatenate(
        [jnp.concatenate([hkv[:, :, i] if i == jj else zblk for jj in range(nh)], axis=-1) for i in range(nh)],
        axis=-2)
    svec = jnp.stack([w["rw_lnx_w"][ri], w["rw_lnx_b"][ri], w["rw_rk"][ri].reshape(D_MODEL),
                      zero, zero, zero, zero, zero])
    per_seq = t == chunk and b % RWKV_SUBCHUNKS == 0
    nsub = RWKV_SUBCHUNKS if per_seq or (t // chunk) % RWKV_SUBCHUNKS == 0 else 1
    nct = 1 if per_seq else t // (chunk * nsub)
    nseq = nsub if per_seq else 1
    crow = lambda bb, j: (bb * nct + j, 0)
    cspec = pl.BlockSpec((chunk * nsub, D_MODEL), crow)
    hspec = pl.BlockSpec((nseq, ng, gl_lanes, gl_lanes), lambda bb, j: (bb, 0, 0, 0))
    y, hout = pl.pallas_call(
        functools.partial(_rwkv_scan_body, nh=nh, chunk=chunk, nsub=nsub, per_seq=per_seq),
        grid=(b // nseq, nct),
        in_specs=[cspec] * 5 + [pl.BlockSpec((nsub, 1, D_MODEL), lambda bb, j: (bb * nct + j, 0, 0)), hspec,
                                _const_spec((8, D_MODEL))],
        out_specs=[cspec, hspec],
        out_shape=[jax.ShapeDtypeStruct((n, D_MODEL), adt),
                   jax.ShapeDtypeStruct((b, ng, gl_lanes, gl_lanes), F32)],
        compiler_params=_params(2),
        name="rwkv_scan",
    )(rt, kt, at, bt, v, gl, h0, svec)
    s_new = jnp.stack([hout[:, :, i * RW_N:(i + 1) * RW_N, i * RW_N:(i + 1) * RW_N] for i in range(nh)],
                      axis=2)
    s_new = jnp.swapaxes(s_new, -1, -2).reshape(b, RW_H, RW_N, RW_N)
    pre = ("gated", [y, g], [w["rw_wo"][ri].astype(BF16)])
    return pre, shift_new, s_new, (v if not has_vres else v_first)


def _rope_lanes(x, tab_ref):
    half = MLA_ROPE // 2
    return (x * tab_ref[0] + pltpu.roll(x, LANES - half, 1) * tab_ref[1] + pltpu.roll(x, half, 1) * tab_ref[2])


def _mla_proj_body(x_ref, nw_ref, tab_ref, winq_ref, winc_ref, wink_ref, qn_ref, kvn_ref, wqn_ref, wqr_ref,
                   wuk_ref, c_ref, kr_ref, kcat_ref, qcat_ref):
    h = _rms(x_ref[...], nw_ref[0:1, :]).astype(BF16)
    cq = _rms(jnp.dot(h, winq_ref[...], preferred_element_type=F32), qn_ref[...]).astype(BF16)
    c = _rms(jnp.dot(h, winc_ref[...], preferred_element_type=F32), kvn_ref[...])
    kr = _rope_lanes(jnp.dot(h, wink_ref[...], preferred_element_type=F32), tab_ref)
    c_ref[...] = c
    kr_ref[...] = kr
    adt = kcat_ref.dtype
    kcat_ref[:, 0:MLA_KV_LORA] = c.astype(adt)
    kcat_ref[:, MLA_KV_LORA:MLA_QK] = kr.astype(adt)
    qn = jnp.dot(cq, wqn_ref[...], preferred_element_type=F32).astype(BF16)
    qr = jnp.dot(cq, wqr_ref[...], preferred_element_type=F32)
    for pr in range(MLA_H // 2):
        ql = jnp.dot(qn[:, pr * LANES:(pr + 1) * LANES], wuk_ref[pr], preferred_element_type=F32) * MLA_SCALE
        qcat_ref[2 * pr, :, 0:MLA_KV_LORA] = ql[:, :MLA_KV_LORA].astype(adt)
        qcat_ref[2 * pr + 1, :, 0:MLA_KV_LORA] = ql[:, MLA_KV_LORA:].astype(adt)
    for hh in range(MLA_H):
        qro = _rope_lanes(qr[:, hh * LANES:(hh + 1) * LANES], tab_ref) * MLA_SCALE
        qcat_ref[hh, :, MLA_KV_LORA:MLA_QK] = qro.astype(adt)


MLA_TQ = 256
MLA_TK = 512
MLA_SPLIT = 16


def _mla_prompt_body(q_ref, k_ref, o_ref, m_ref, l_ref, acc_ref):
    i = pl.program_id(1)
    rows = MLA_H * MLA_TQ
    q = q_ref[...].reshape(rows, MLA_QK)
    m_ref[...] = jnp.full((rows, LANES), -jnp.inf, F32)
    l_ref[...] = jnp.zeros((rows, LANES), F32)
    acc_ref[...] = jnp.zeros((rows, MLA_KV_LORA), F32)
    ones = jnp.ones((MLA_TK, LANES), BF16)
    reps = MLA_TK // LANES

    sub = rows // MLA_SPLIT

    def block(k0, masked):
        kblk = k_ref[pl.ds(k0, MLA_TK), :]
        ss = [lax.dot_general(q[g * sub:(g + 1) * sub], kblk, (((1,), (1,)), ((), ())),
                              preferred_element_type=F32) for g in range(MLA_SPLIT)]
        def causal(g):
            qpos = i * MLA_TQ + (g * sub + lax.broadcasted_iota(jnp.int32, (sub, MLA_TK), 0)) % MLA_TQ
            kpos = k0 + lax.broadcasted_iota(jnp.int32, (sub, MLA_TK), 1)
            return kpos <= qpos

        shared = causal(0) if masked and sub % MLA_TQ == 0 else None
        for g in range(MLA_SPLIT):
            rs = slice(g * sub, (g + 1) * sub)
            s = ss[g]
            if masked:
                s = jnp.where(shared if shared is not None else causal(g), s, -jnp.inf)
            m_old = m_ref[rs, :]
            m_new = jnp.maximum(m_old, jnp.max(s, axis=1, keepdims=True))
            alpha = jnp.exp(m_old - m_new)
            pf = jnp.exp(s - jnp.concatenate([m_new] * reps, axis=1))
            l_ref[rs, :] = l_ref[rs, :] * alpha + jnp.sum(pf, axis=1, keepdims=True)
            acc_ref[rs, :] = (acc_ref[rs, :] * jnp.concatenate([alpha] * (MLA_KV_LORA // LANES), axis=1)
                              + jnp.dot(pf.astype(BF16), kblk[:, :MLA_KV_LORA], preferred_element_type=F32))
            m_ref[rs, :] = m_new

    def full_step(kb, carry):
        block(pl.multiple_of(kb * MLA_TK, MLA_TK), False)
        return carry

    n_full = (i * MLA_TQ) // MLA_TK
    lax.fori_loop(0, n_full, full_step, 0)
    block(pl.multiple_of(n_full * MLA_TK, MLA_TK), True)
    o = acc_ref[...] / jnp.concatenate([l_ref[...]] * (MLA_KV_LORA // LANES), axis=1)
    o_ref[...] = o.reshape(MLA_H, MLA_TQ, MLA_KV_LORA).astype(BF16)


MLA_PP = 32
MLA_GROUPS = 2


def _mla_sample_body(pt_ref, q_ref, kn_ref, *rest):
    lat_refs = rest[:MLA_PP]
    kro_refs = rest[MLA_PP:2 * MLA_PP]
    o_ref, m_ref, l_ref, acc_ref = rest[2 * MLA_PP:]
    j = pl.program_id(1)
    t = q_ref.shape[1]
    rows = MLA_H * t
    q = q_ref[...].reshape(rows, MLA_QK).astype(BF16)
    ql = q[:, :MLA_KV_LORA]
    qr = q[:, MLA_KV_LORA:MLA_KV_LORA + MLA_ROPE]

    @pl.when(j == 0)
    def _():
        m_ref[...] = jnp.full(m_ref.shape, -jnp.inf, F32)
        l_ref[...] = jnp.zeros(l_ref.shape, F32)
        acc_ref[...] = jnp.zeros(acc_ref.shape, F32)

    vrep = MLA_KV_LORA // LANES

    def update(g, s, vals, row_sum):
        m_old = m_ref[g]
        m_new = jnp.maximum(m_old, jnp.max(s, axis=1, keepdims=True))
        alpha = jnp.exp(m_old - m_new)
        if s.shape[1] % LANES == 0:
            p = jnp.exp(s - jnp.concatenate([m_new] * (s.shape[1] // LANES), axis=1)).astype(BF16)
        else:
            p = jnp.exp(s - m_new[:, 0:1]).astype(BF16)
        l_ref[g] = l_ref[g] * alpha + row_sum(p)
        acc_ref[g] = (acc_ref[g] * jnp.concatenate([alpha] * vrep, axis=1)
                      + jnp.dot(p, vals, preferred_element_type=F32))
        m_ref[g] = m_new

    per = MLA_PP // MLA_GROUPS
    ones = jnp.ones((per * lat_refs[0].shape[1], LANES), BF16)
    scores, values = [], []
    for g in range(MLA_GROUPS):
        cbs, s_parts = [], []
        for pp in range(g * per, (g + 1) * per):
            cb = lat_refs[pp][0].astype(BF16)
            kbt = kro_refs[pp][0].astype(BF16)
            s_parts.append(lax.dot_general(ql, cb, (((1,), (1,)), ((), ())), preferred_element_type=F32)
                           + jnp.dot(qr, kbt, preferred_element_type=F32))
            cbs.append(cb)
        scores.append(jnp.concatenate(s_parts, axis=1))
        values.append(jnp.concatenate(cbs, axis=0))
    for g in range(MLA_GROUPS):
        update(g, scores[g], values[g], lambda p: jnp.dot(p, ones, preferred_element_type=F32))

    @pl.when(j == pl.num_programs(1) - 1)
    def _():
        kn = kn_ref[...].astype(BF16)
        s = lax.dot_general(q, kn, (((1,), (1,)), ((), ())), preferred_element_type=F32)
        qpos = lax.broadcasted_iota(jnp.int32, (rows, t), 0) % t
        kpos = lax.broadcasted_iota(jnp.int32, (rows, t), 1)
        s = jnp.where(kpos <= qpos, s, -jnp.inf)
        update(0, s, kn[:, :MLA_KV_LORA], lambda p: jnp.sum(p.astype(F32), axis=1, keepdims=True))
        m_all = m_ref[0]
        for g in range(1, MLA_GROUPS):
            m_all = jnp.maximum(m_all, m_ref[g])
        l_all = jnp.zeros((rows, LANES), F32)
        acc = jnp.zeros((rows, MLA_KV_LORA), F32)
        for g in range(MLA_GROUPS):
            wgt = jnp.exp(m_ref[g] - m_all)
            l_all = l_all + l_ref[g] * wgt
            acc = acc + acc_ref[g] * jnp.concatenate([wgt] * vrep, axis=1)
        o = acc / jnp.concatenate([l_all] * vrep, axis=1)
        o_ref[...] = o.reshape(MLA_H, t, MLA_KV_LORA).astype(o_ref.dtype)


def _mla_layer(x2d, pos, w, mi, nw, b, t, paged):
    n = b * t
    tl = _tiling(b, t, 512)
    G, J, R = tl["G"], tl["J"], tl["R"]
    adt = BF16 if t % 16 == 0 else F32
    half = MLA_ROPE // 2
    inv = ROPE_THETA ** (-jnp.arange(half, dtype=F32) / half)
    ang = pos.astype(F32)[:, None] * inv[None, :]
    cos, sin = jnp.cos(ang), jnp.sin(ang)
    zpad = jnp.zeros((t, LANES - MLA_ROPE), F32)
    zh = jnp.zeros((t, half), F32)
    tab = jnp.stack([jnp.concatenate([cos, cos, zpad], 1), jnp.concatenate([-sin, zh, zpad], 1),
                     jnp.concatenate([zh, sin, zpad], 1)])
    if G == 1:
        tab = jnp.tile(tab, (1, b, 1))
    w_in = w["mla_w_in"][mi]
    winq = w_in[:, :MLA_Q_LORA].astype(BF16)
    winc = w_in[:, MLA_Q_LORA:MLA_Q_LORA + MLA_KV_LORA].astype(BF16)
    wink = jnp.pad(w_in[:, MLA_Q_LORA + MLA_KV_LORA:], ((0, 0), (0, LANES - MLA_ROPE))).astype(BF16)
    wqb = w["mla_w_qb"][mi].reshape(MLA_Q_LORA, MLA_H, MLA_NOPE + MLA_ROPE)
    wqn = wqb[:, :, :MLA_NOPE].reshape(MLA_Q_LORA, MLA_H * MLA_NOPE).astype(BF16)
    wqr = jnp.pad(wqb[:, :, MLA_NOPE:], ((0, 0), (0, 0), (0, LANES - MLA_ROPE))
                  ).reshape(MLA_Q_LORA, MLA_H * LANES).astype(BF16)
    wuk = jnp.transpose(w["mla_w_uk"][mi], (1, 2, 0)).reshape(MLA_H // 2, 2, MLA_NOPE, MLA_KV_LORA)
    wuk_bd = jnp.einsum("pinc,ij->pinjc", wuk, jnp.eye(2, dtype=F32)).reshape(
        MLA_H // 2, 2 * MLA_NOPE, 2 * MLA_KV_LORA).astype(BF16)
    wuv = jnp.transpose(w["mla_w_uv"][mi], (1, 0, 2)).reshape(MLA_H // 2, 2, MLA_KV_LORA, MLA_V)
    wuv_bd = jnp.einsum("picv,ij->picjv", wuv, jnp.eye(2, dtype=F32)).reshape(
        MLA_H // 2, 2 * MLA_KV_LORA, 2 * MLA_V).astype(BF16)
    nwa = jnp.concatenate([nw[0:1], jnp.zeros((7, D_MODEL), F32)])
    nwb = jnp.concatenate([nw[1:2], jnp.zeros((7, D_MODEL), F32)])
    row = lambda g, j: (g * J + j, 0)
    wl = [winq, winc, wink, w["mla_q_norm"][mi][None, :], w["mla_kv_norm"][mi][None, :], wqn, wqr, wuk_bd]
    c, kr, kcat, qcat = pl.pallas_call(
        _mla_proj_body,
        grid=(G, J),
        in_specs=[pl.BlockSpec((R, D_MODEL), row), _const_spec((8, D_MODEL)),
                  pl.BlockSpec((3, R, LANES), lambda g, j: (0, j, 0))] + [_const_spec(a.shape) for a in wl],
        out_specs=[pl.BlockSpec((R, MLA_KV_LORA), row), pl.BlockSpec((R, LANES), row),
                   pl.BlockSpec((R, MLA_QK), row), pl.BlockSpec((MLA_H, R, MLA_QK), lambda g, j: (0, g * J + j, 0))],
        out_shape=[jax.ShapeDtypeStruct((n, MLA_KV_LORA), F32), jax.ShapeDtypeStruct((n, LANES), F32),
                   jax.ShapeDtypeStruct((n, MLA_QK), adt), jax.ShapeDtypeStruct((MLA_H, n, MLA_QK), adt)],
        compiler_params=_params(2),
        name="mla_proj",
    )(x2d, nwa, tab, *wl)

    if paged is None:
        nq = t // MLA_TQ
        rows = MLA_H * MLA_TQ
        o = pl.pallas_call(
            _mla_prompt_body,
            grid=(b, nq),
            in_specs=[pl.BlockSpec((MLA_H, MLA_TQ, MLA_QK), lambda bb, i: (0, bb * nq + i, 0)),
                      pl.BlockSpec((t, MLA_QK), lambda bb, i: (bb, 0))],
            out_specs=pl.BlockSpec((MLA_H, MLA_TQ, MLA_KV_LORA), lambda bb, i: (0, bb * nq + i, 0)),
            out_shape=jax.ShapeDtypeStruct((MLA_H, n, MLA_KV_LORA), BF16),
            scratch_shapes=[pltpu.VMEM((rows, LANES), F32), pltpu.VMEM((rows, LANES), F32),
                            pltpu.VMEM((rows, MLA_KV_LORA), F32)],
            compiler_params=_params(2),
            name="mla_attend_prompt",
        )(qcat, kcat)
    else:
        pages_c, pages_kr, page_table = paged
        page = pages_c.shape[1]
        npg = page_table.shape[1]
        assert npg % MLA_PP == 0
        rows = MLA_H * t

        def page_map(pp):
            return lambda bb, j, pt: (pt[bb * npg + j * MLA_PP + pp], 0, 0)

        grid_spec = pltpu.PrefetchScalarGridSpec(
            num_scalar_prefetch=1,
            grid=(b, npg // MLA_PP),
            in_specs=[pl.BlockSpec((MLA_H, t, MLA_QK), lambda bb, j, pt: (0, bb, 0)),
                      pl.BlockSpec((t, MLA_QK), lambda bb, j, pt: (bb, 0))]
            + [pl.BlockSpec((1, page, MLA_KV_LORA), page_map(pp)) for pp in range(MLA_PP)]
            + [pl.BlockSpec((1, MLA_ROPE, page), page_map(pp)) for pp in range(MLA_PP)],
            out_specs=pl.BlockSpec((MLA_H, t, MLA_KV_LORA), lambda bb, j, pt: (0, bb, 0)),
            scratch_shapes=[pltpu.VMEM((MLA_GROUPS, rows, LANES), F32), pltpu.VMEM((MLA_GROUPS, rows, LANES), F32),
                            pltpu.VMEM((MLA_GROUPS, rows, MLA_KV_LORA), F32)],
        )
        o = pl.pallas_call(
            _mla_sample_body,
            grid_spec=grid_spec,
            out_shape=jax.ShapeDtypeStruct((MLA_H, n, MLA_KV_LORA), adt),
            compiler_params=_params(2),
            name="mla_attend_sample",
        )(page_table.reshape(-1), qcat, kcat, *([pages_c] * MLA_PP), *([jnp.swapaxes(pages_kr, 1, 2)] * MLA_PP))

    pre = ("mla", [o], [wuv_bd, w["mla_wo"][mi].astype(BF16)])
    return pre, c.reshape(b, t, MLA_KV_LORA), kr[:, :MLA_ROPE].reshape(b, t, MLA_ROPE)


GDN_CW = 512


def _gdn_proj_body(x_ref, prev_ref, nw_ref, wqkv_ref, wz_ref, wbg_ref, cw_ref, gvec_ref, tri_ref,
                   q_ref, k_ref, v_ref, z_ref, beta_ref, gc_ref, st_ref, carry_ref):
    @pl.when(pl.program_id(1) == 0)
    def _():
        carry_ref[...] = prev_ref[...]

    rows = x_ref.shape[0]
    p = carry_ref.shape[0]
    h = _rms(x_ref[...], nw_ref[0:1, :]).astype(BF16)
    adt = z_ref.dtype
    bg = jnp.dot(h, wbg_ref[...], preferred_element_type=F32)
    beta_ref[...] = _sigmoid(bg)
    g = -jnp.exp(gvec_ref[0:1, :]) * _softplus(bg + gvec_ref[1:2, :])
    gc_ref[...] = _chunk_cumsum(g, tri_ref[...])
    nch = GDN_CONV_DIM // GDN_CW
    zw = GDN_V_DIM // nch

    def up(c):
        return jnp.dot(h, wqkv_ref[:, c * GDN_CW:(c + 1) * GDN_CW], preferred_element_type=F32)

    u_nxt = up(0)
    for c in range(nch):
        sl = slice(c * GDN_CW, (c + 1) * GDN_CW)
        u = u_nxt
        u_nxt = up(c + 1) if c + 1 < nch else None
        z_ref[:, c * zw:(c + 1) * zw] = jnp.dot(h, wz_ref[:, c * zw:(c + 1) * zw],
                                                preferred_element_type=F32).astype(adt)
        prev = carry_ref[:, sl]
        u1 = _shift_rows(u, prev, 1)
        near = cw_ref[3:4, sl] * u + cw_ref[2:3, sl] * u1
        far = cw_ref[1:2, sl] * u + cw_ref[0:1, sl] * u1
        prev1 = pltpu.roll(prev, 1, 0)
        far_prev = cw_ref[1:2, sl] * prev + cw_ref[0:1, sl] * prev1
        y = near + _shift_rows(far, far_prev, 2)
        tail = u[rows - p:, :]
        carry_ref[:, sl] = tail
        st_ref[:, sl] = tail
        y = _silu(y)
        off = c * GDN_CW
        if off < 2 * GDN_QK_DIM:
            dst, base, scale = (q_ref, off, GDN_DK ** -0.5) if off < GDN_QK_DIM else (k_ref, off - GDN_QK_DIM, 1.0)
            for hh in range(GDN_CW // GDN_DK):
                yh = y[:, hh * GDN_DK:(hh + 1) * GDN_DK]
                inv = lax.rsqrt(jnp.sum(yh * yh, axis=-1, keepdims=True) + 1e-6)
                dst[:, base + hh * GDN_DK:base + (hh + 1) * GDN_DK] = (yh * (inv * scale)).astype(adt)
        else:
            v_ref[:, off - 2 * GDN_QK_DIM:off - 2 * GDN_QK_DIM + GDN_CW] = y.astype(adt)


def _gdn_chunk_body(q_ref, k_ref, v_ref, z_ref, gc_ref, beta_ref, s0_ref, nw_ref,
                    o_ref, so_ref, *, nh, chunk, nsub):
    @pl.when(pl.program_id(1) == 0)
    def _():
        so_ref[...] = s0_ref[...]

    ng = GDN_V_H // nh
    gc = nh * chunk
    rep = GDN_V_H // GDN_QK_H
    ri = lax.broadcasted_iota(jnp.int32, (gc, gc), 0)
    ci = lax.broadcasted_iota(jnp.int32, (gc, gc), 1)
    same = (ri // chunk) == (ci // chunk)
    strict = same & ((ri % chunk) > (ci % chunk))
    incl = same & ((ri % chunk) >= (ci % chunk))
    last = same & ((ci % chunk) == chunk - 1)
    eye = (ri == ci).astype(F32)
    row_head = lax.broadcasted_iota(jnp.int32, (gc, GDN_DK), 0) // chunk
    groups = range(ng)
    heads = [[q * nh + i for i in range(nh)] for q in groups]
    keys = [(sc, q) for sc in range(nsub) for q in groups]

    def stack(ref, sc, hds, width):
        parts = [ref[sc * chunk:(sc + 1) * chunk, hd * width:(hd + 1) * width] for hd in hds]
        return parts[0] if len(parts) == 1 else jnp.concatenate(parts, axis=0)

    def col(ref, sc, lanes):
        parts = [ref[sc * chunk:(sc + 1) * chunk, ln:ln + 1] for ln in lanes]
        return parts[0] if len(parts) == 1 else jnp.concatenate(parts, axis=0)

    k_st = {k: stack(k_ref, k[0], [hd // rep for hd in heads[k[1]]], GDN_DK) for k in keys}
    q_st = {k: stack(q_ref, k[0], [hd // rep for hd in heads[k[1]]], GDN_DK) for k in keys}
    v_st = {k: stack(v_ref, k[0], heads[k[1]], GDN_DV) for k in keys}
    gcol = {k: col(gc_ref, k[0], [GDN_V_H + hd for hd in heads[k[1]]]) for k in keys}
    bcol = {k: col(beta_ref, k[0], heads[k[1]]) for k in keys}
    grow = {k: jnp.sum(jnp.where(ri == ci, gcol[k], 0.0), axis=0, keepdims=True) for k in keys}
    k_b = {k: k_st[k].astype(BF16) for k in keys}
    kq = {k: _bdot_nt(jnp.concatenate([k_b[k], q_st[k].astype(BF16)], axis=0), k_b[k]) for k in keys}
    decay = {k: jnp.exp(jnp.where(incl, gcol[k] - grow[k], -jnp.inf)) for k in keys}
    a = {k: jnp.where(strict, kq[k][:gc] * bcol[k] * decay[k], 0.0) for k in keys}
    aqk = {k: jnp.where(incl, kq[k][gc:] * decay[k], 0.0).astype(BF16) for k in keys}

    x = _unit_lower_inverse({k: -a[k] for k in keys}, ri, ci, chunk)

    egc = {k: jnp.exp(gcol[k]) for k in keys}
    uw = {k: _bdot(x[k], jnp.concatenate([v_st[k] * bcol[k], k_st[k] * (bcol[k] * egc[k])], axis=1)) for k in keys}
    glast = {k: jnp.sum(jnp.where(last, grow[k], 0.0), axis=1, keepdims=True) for k in keys}
    kg = {k: k_st[k] * jnp.exp(glast[k] - gcol[k]) for k in keys}
    qg = {k: q_st[k] * egc[k] for k in keys}

    for sc in range(nsub):
        states = [[so_ref[0, hd] for hd in heads[q]] for q in groups]
        wq_s = []
        for q in groups:
            wm = uw[(sc, q)][:, GDN_DV:]
            parts = []
            for i in range(nh):
                rs = slice(i * chunk, (i + 1) * chunk)
                parts.append(_bdot(jnp.concatenate([wm[rs], qg[(sc, q)][rs]], axis=0), states[q][i]))
            wq_s.append(parts)
        v_new, o_st = [], []
        for q in groups:
            ws = jnp.concatenate([m[:chunk] for m in wq_s[q]], axis=0) if nh > 1 else wq_s[q][0][:chunk]
            qs = jnp.concatenate([m[chunk:] for m in wq_s[q]], axis=0) if nh > 1 else wq_s[q][0][chunk:]
            vn = (uw[(sc, q)][:, :GDN_DV] - ws).astype(BF16)
            v_new.append(vn)
            o_st.append(qs + _bdot(aqk[(sc, q)], vn))
        for q in groups:
            for i, hd in enumerate(heads[q]):
                rs = slice(i * chunk, (i + 1) * chunk)
                gl_h = jnp.exp(glast[(sc, q)][i * chunk:i * chunk + 1, :])
                if chunk % 16 == 0:
                    upd = _bdot_tn(kg[(sc, q)][rs], v_new[q][rs])
                else:
                    upd = _bdot_tn(jnp.where(row_head == i, kg[(sc, q)], 0.0), v_new[q])
                so_ref[0, hd] = states[q][i] * gl_h + upd
        for q in groups:
            z_st = stack(z_ref, sc, heads[q], GDN_DV).astype(F32)
            og = (_rms(o_st[q], nw_ref[0:1, :]) * _silu(z_st)).astype(o_ref.dtype)
            for i, hd in enumerate(heads[q]):
                o_ref[sc * chunk:(sc + 1) * chunk, hd * GDN_DV:(hd + 1) * GDN_DV] = og[i * chunk:(i + 1) * chunk]


GDN_SUBCHUNKS = 4


def _gdn_layer(x2d, conv_prev, s0, w, gi, nw, b, t):
    n = b * t
    chunk = _chunk_of(t)
    tl = _tiling(b, t, 256)
    G, J, R, P = tl["G"], tl["J"], tl["R"], tl["P"]
    adt = BF16 if chunk % 16 == 0 else F32
    w_in = w["gdn_w_in"][gi]
    o1 = GDN_CONV_DIM
    o2 = o1 + GDN_V_DIM
    wqkv = w_in[:, :o1].astype(BF16)
    wz = w_in[:, o1:o2].astype(BF16)
    wbg = jnp.pad(w_in[:, o2:], ((0, 0), (0, LANES - 2 * GDN_V_H))).astype(BF16)
    cw = jnp.pad(w["gdn_conv_w"][gi], ((0, 8 - GDN_CONV), (0, 0)))
    gvec = jnp.zeros((8, LANES), F32)
    gvec = gvec.at[0, GDN_V_H:2 * GDN_V_H].set(w["gdn_a_log"][gi]).at[1, GDN_V_H:2 * GDN_V_H].set(w["gdn_dt_bias"][gi])
    bc = chunk if chunk == 64 else R
    tri = _chunk_masks(chunk, bc)
    nwa = jnp.concatenate([nw[0:1], jnp.zeros((7, D_MODEL), F32)])
    row = lambda g, j: (g * J + j, 0)
    st_spec = pl.BlockSpec((P, GDN_CONV_DIM), lambda g, j: (g, 0))
    qn, kn, v, z, beta, gcs, st = pl.pallas_call(
        _gdn_proj_body,
        grid=(G, J),
        in_specs=[pl.BlockSpec((R, D_MODEL), row), st_spec, _const_spec((8, D_MODEL)), _const_spec(wqkv.shape),
                  _const_spec(wz.shape), _const_spec(wbg.shape), _const_spec(cw.shape), _const_spec(gvec.shape),
                  _const_spec(tri.shape)],
        out_specs=[pl.BlockSpec((R, GDN_QK_DIM), row), pl.BlockSpec((R, GDN_QK_DIM), row),
                   pl.BlockSpec((R, GDN_V_DIM), row), pl.BlockSpec((R, GDN_V_DIM), row),
                   pl.BlockSpec((R, LANES), row), pl.BlockSpec((R, LANES), row), st_spec],
        out_shape=[jax.ShapeDtypeStruct((n, GDN_QK_DIM), adt), jax.ShapeDtypeStruct((n, GDN_QK_DIM), adt),
                   jax.ShapeDtypeStruct((n, GDN_V_DIM), adt), jax.ShapeDtypeStruct((n, GDN_V_DIM), adt),
                   jax.ShapeDtypeStruct((n, LANES), F32), jax.ShapeDtypeStruct((n, LANES), F32),
                   jax.ShapeDtypeStruct((b * SUBLANES, GDN_CONV_DIM), F32)],
        scratch_shapes=[pltpu.VMEM((P, GDN_CONV_DIM), F32)],
        compiler_params=_params(2),
        name="gdn_proj",
    )(x2d, _pad_state(conv_prev), nwa, wqkv, wz, wbg, cw, gvec, tri)
    conv_new = st.reshape(b, SUBLANES, GDN_CONV_DIM)[:, SUBLANES - (GDN_CONV - 1):]

    nh = GROUP_ROWS // chunk
    ng = GDN_V_H // nh
    nsub = GDN_SUBCHUNKS if (t // chunk) % GDN_SUBCHUNKS == 0 else 1
    nct = t // (chunk * nsub)
    br = chunk * nsub
    crow = lambda bb, j: (bb * nct + j, 0)
    sspec = pl.BlockSpec((1, GDN_V_H, GDN_DK, GDN_DV), lambda bb, j: (bb, 0, 0, 0))
    nwn = jnp.concatenate([w["gdn_norm_w"][gi][None, :], jnp.zeros((7, GDN_DV), F32)])
    o, s_new = pl.pallas_call(
        functools.partial(_gdn_chunk_body, nh=nh, chunk=chunk, nsub=nsub),
        grid=(b, nct),
        in_specs=[pl.BlockSpec((br, GDN_QK_DIM), crow), pl.BlockSpec((br, GDN_QK_DIM), crow),
                  pl.BlockSpec((br, GDN_V_DIM), crow), pl.BlockSpec((br, GDN_V_DIM), crow),
                  pl.BlockSpec((br, LANES), crow), pl.BlockSpec((br, LANES), crow),
                  sspec, _const_spec((8, GDN_DV))],
        out_specs=[pl.BlockSpec((br, GDN_V_DIM), crow), sspec],
        out_shape=[jax.ShapeDtypeStruct((n, GDN_V_DIM), adt),
                   jax.ShapeDtypeStruct((b, GDN_V_H, GDN_DK, GDN_DV), F32)],
        compiler_params=_params(2),
        name="gdn_chunk",
    )(qn, kn, v, z, gcs, beta, s0, nwn)
    pre = ("plain", [o], [w["gdn_wo"][gi].astype(BF16)])
    return pre, conv_new, s_new


def _trunk(x, pos, rw_s, rw_shift, gdn_s, gdn_conv, ffn_conv, w, paged):
    b, t, _ = x.shape
    x2d = x.reshape(b * t, D_MODEL)
    new = {k: [] for k in ("rw_S", "rw_shift", "mla_c", "mla_kr", "gdn_S", "gdn_conv", "ffn_conv")}
    v_first = None
    ri = mi = gi = 0
    for l, kind in enumerate(LAYER_MIXER):
        nw = w["norm_w"][l]
        if kind == 0:
            pre, sh, s_new, v_first = _rwkv_layer(x2d, rw_shift[ri], rw_s[ri], v_first, w, ri, nw, b, t)
            new["rw_S"].append(s_new)
            new["rw_shift"].append(sh)
            ri += 1
        elif kind == 1:
            pre, c, kr = _mla_layer(x2d, pos, w, mi, nw, b, t, None if paged is None else
                                    (paged[0][mi], paged[1][mi], paged[2]))
            new["mla_c"].append(c)
            new["mla_kr"].append(kr)
            mi += 1
        else:
            pre, cb, s_new = _gdn_layer(x2d, gdn_conv[gi], gdn_s[gi], w, gi, nw, b, t)
            new["gdn_S"].append(s_new)
            new["gdn_conv"].append(cb)
            gi += 1
        nwf = jnp.concatenate([nw[2:4], nw[1:2], jnp.zeros((5, D_MODEL), F32)])
        cwb = jnp.concatenate([w["ffn_conv_w"][l], w["ffn_conv_b"][l][None, :],
                               jnp.zeros((8 - FFN_CONV - 1, 2 * D_FF), F32)])
        x2d, st = _ffn(x2d, pre, _pad_state(ffn_conv[l]), nwf, w["ffn_w_up"][l].astype(BF16), cwb,
                       w["ffn_w_down"][l].astype(BF16), b, t)
        new["ffn_conv"].append(st.reshape(b, SUBLANES, 2 * D_FF)[:, SUBLANES - (FFN_CONV - 1):])
    return x2d.reshape(b, t, D_MODEL), {k: jnp.stack(v) for k, v in new.items()}


def kernel(x_prompt, x_sample, state_rwkv_wkv, state_rwkv_shift, cache_mla_latent, cache_mla_krope, state_gdn_S, state_gdn_conv, state_ffn_conv, page_table, norm_w, rw_mu, rw_wrkv, rw_w0, rw_w1, rw_w2, rw_a0, rw_a1, rw_a2, rw_v0, rw_v1, rw_v2, rw_g1, rw_g2, rw_kk, rw_ka, rw_rk, rw_lnx_w, rw_lnx_b, rw_wo, mla_w_in, mla_q_norm, mla_kv_norm, mla_w_qb, mla_w_uk, mla_w_uv, mla_wo, gdn_w_in, gdn_conv_w, gdn_a_log, gdn_dt_bias, gdn_norm_w, gdn_wo, ffn_w_up, ffn_conv_w, ffn_conv_b, ffn_w_down):
    w = dict(norm_w=norm_w, rw_mu=rw_mu, rw_wrkv=rw_wrkv, rw_w0=rw_w0, rw_w1=rw_w1, rw_w2=rw_w2, rw_a0=rw_a0,
             rw_a1=rw_a1, rw_a2=rw_a2, rw_v0=rw_v0, rw_v1=rw_v1, rw_v2=rw_v2, rw_g1=rw_g1, rw_g2=rw_g2,
             rw_kk=rw_kk, rw_ka=rw_ka, rw_rk=rw_rk, rw_lnx_w=rw_lnx_w, rw_lnx_b=rw_lnx_b, rw_wo=rw_wo,
             mla_w_in=mla_w_in, mla_q_norm=mla_q_norm, mla_kv_norm=mla_kv_norm, mla_w_qb=mla_w_qb,
             mla_w_uk=mla_w_uk, mla_w_uv=mla_w_uv, mla_wo=mla_wo, gdn_w_in=gdn_w_in, gdn_conv_w=gdn_conv_w,
             gdn_a_log=gdn_a_log, gdn_dt_bias=gdn_dt_bias, gdn_norm_w=gdn_norm_w, gdn_wo=gdn_wo,
             ffn_w_up=ffn_w_up, ffn_conv_w=ffn_conv_w, ffn_conv_b=ffn_conv_b, ffn_w_down=ffn_w_down)
    b, t = x_prompt.shape[0], x_prompt.shape[1]
    n_rw, n_gdn, depth = state_rwkv_wkv.shape[0], state_gdn_S.shape[0], state_ffn_conv.shape[0]
    y_p, sp = _trunk(
        x_prompt, jnp.arange(t),
        jnp.zeros((n_rw, b) + state_rwkv_wkv.shape[2:], F32), jnp.zeros((n_rw, b, D_MODEL), F32),
        jnp.zeros((n_gdn, b) + state_gdn_S.shape[2:], F32), jnp.zeros((n_gdn, b) + state_gdn_conv.shape[2:], F32),
        jnp.zeros((depth, b) + state_ffn_conv.shape[2:], F32), w, None)
    past_len = page_table.shape[1] * cache_mla_latent.shape[2]
    pos_s = past_len + jnp.arange(x_sample.shape[1])
    y_s, ss = _trunk(x_sample, pos_s, state_rwkv_wkv, state_rwkv_shift, state_gdn_S, state_gdn_conv,
                     state_ffn_conv, w, (cache_mla_latent, cache_mla_krope, page_table))
    names = ("rw_S", "rw_shift", "mla_c", "mla_kr", "gdn_S", "gdn_conv", "ffn_conv")
    return (y_p, y_s) + tuple(sp[k] for k in names) + tuple(ss[k] for k in names)
```
